```python
import math
import jax, jax.numpy as jnp
from jax import lax
import numpy as np

D_MODEL = 1024
BATCH = 4
SEQ = 8192
DEPTH = 2

HEAD_DIM = 64
A_Q_HEADS = 8
A_KV_HEADS = 2
A_GROUP = A_Q_HEADS // A_KV_HEADS
WINDOW = 128
BLOCK = 128
ROPE_THETA = 500000.0
ROPE_DIM = HEAD_DIM // 4
A_WIDTH = A_Q_HEADS * HEAD_DIM
KV_WIDTH = A_KV_HEADS * HEAD_DIM
B_GROUPS = 8
B_GROUP_DIM = 64
B_WIDTH = B_GROUPS * B_GROUP_DIM
CHUNK = 128
EVEN_IN_WIDTH = A_WIDTH + 2 * KV_WIDTH + 2 * B_WIDTH
EVEN_MIX_WIDTH = A_WIDTH + B_WIDTH
POOL_WINDOWS = (2, 4, 8, 16)
POOL_GROUPS = len(POOL_WINDOWS)
POOL_WIDTH = D_MODEL
POOL_GROUP_DIM = POOL_WIDTH // POOL_GROUPS
MEM_LEN = 256
X_HEADS = 4
X_HEAD_DIM = D_MODEL // X_HEADS
N_EXPERTS = 16
N_EXPERT_GROUPS = 4
EXPERTS_PER_GROUP = N_EXPERTS // N_EXPERT_GROUPS
TOP_K = 2
D_EXPERT = D_MODEL // 2
ALPHA = (2.0 * DEPTH) ** 0.25
BETA = (8.0 * DEPTH) ** -0.25
LN_EPS = 1e-5

kernel_name = "hybrid_swa_sgu_pool_memxattn_groupmoe"


def layer_norm(x, g, b):
    xf = x.astype(jnp.float32)
    mu = jnp.mean(xf, axis=-1, keepdims=True)
    var = jnp.mean(jnp.square(xf - mu), axis=-1, keepdims=True)
    y = (xf - mu) * lax.rsqrt(var + LN_EPS) * g.astype(jnp.float32) + b.astype(jnp.float32)
    return y.astype(x.dtype)


def partial_rope(t, positions):
    half = ROPE_DIM // 2
    inv_freq = ROPE_THETA ** (-(jnp.arange(half, dtype=jnp.float32) * 2.0 / ROPE_DIM))
    ang = positions.astype(jnp.float32)[..., None] * inv_freq
    cos = jnp.cos(ang)[:, :, None, :]
    sin = jnp.sin(ang)[:, :, None, :]
    tr = t[..., :ROPE_DIM].astype(jnp.float32)
    t1, t2 = tr[..., :half], tr[..., half:]
    rot = jnp.concatenate([t1 * cos - t2 * sin, t2 * cos + t1 * sin], axis=-1).astype(t.dtype)
    return jnp.concatenate([rot, t[..., ROPE_DIM:]], axis=-1)


def sliding_window_sink_attention(q, k, v, sinks):
    bsz, s_len = q.shape[0], q.shape[1]
    nb = s_len // BLOCK
    qb = q.reshape(bsz, nb, BLOCK, A_KV_HEADS, A_GROUP, HEAD_DIM)

    def with_prev(t):
        tb = t.reshape(bsz, nb, BLOCK, A_KV_HEADS, HEAD_DIM)
        prev = jnp.pad(tb[:, :-1], ((0, 0), (1, 0), (0, 0), (0, 0), (0, 0)))
        return jnp.concatenate([prev, tb], axis=2)

    kb, vb = with_prev(k), with_prev(v)
    s = jnp.einsum('bnqhgd,bnkhd->bnhgqk', qb, kb).astype(jnp.float32) * (HEAD_DIM ** -0.5)
    qi = jnp.arange(BLOCK)[:, None]
    kj = jnp.arange(2 * BLOCK)[None, :]
    rel = qi + BLOCK - kj
    band = (rel >= 0) & (rel < WINDOW)
    not_before_start = (jnp.arange(nb)[:, None, None] > 0) | (kj[None] >= BLOCK)
    valid = band[None] & not_before_start
    s = jnp.where(valid[None, :, None, None], s, -jnp.inf)
    sink = sinks.astype(jnp.float32).reshape(A_KV_HEADS, A_GROUP)[None, None, :, :, None, None]
    m = jnp.maximum(jnp.max(s, axis=-1, keepdims=True), sink)
    p = jnp.exp(s - m)
    denom = jnp.sum(p, axis=-1, keepdims=True) + jnp.exp(sink - m)
    p = (p / denom).astype(v.dtype)
    o = jnp.einsum('bnhgqk,bnkhd->bnqhgd', p, vb)
    return o.reshape(bsz, s_len, A_WIDTH)


def chunked_spatial_gating(u, v, ln_g, ln_b, w_s, b_s):
    bsz, s_len = u.shape[0], u.shape[1]
    nc = s_len // CHUNK
    vg = v.reshape(bsz, s_len, B_GROUPS, B_GROUP_DIM)
    vg = layer_norm(vg, ln_g.reshape(B_GROUPS, B_GROUP_DIM), ln_b.reshape(B_GROUPS, B_GROUP_DIM))
    vg = vg.reshape(bsz, nc, CHUNK, B_GROUPS, B_GROUP_DIM)
    causal = jnp.tril(jnp.ones((CHUNK, CHUNK), dtype=bool))
    w = jnp.where(causal[None], w_s, jnp.zeros_like(w_s))
    mixed = jnp.einsum('gts,bnsgc->bntgc', w, vg) + b_s.T[None, None, :, :, None]
    return u * mixed.reshape(bsz, s_len, B_WIDTH)


def even_mixer(x, positions, w_in, sinks, sgu_ln_g, sgu_ln_b, sgu_w, sgu_b, w_out):
    bsz, s_len, _ = x.shape
    h = x @ w_in
    c1 = A_WIDTH
    c2 = c1 + KV_WIDTH
    c3 = c2 + KV_WIDTH
    c4 = c3 + B_WIDTH
    q, k, v, bu, bv = h[..., :c1], h[..., c1:c2], h[..., c2:c3], h[..., c3:c4], h[..., c4:]
    q = partial_rope(q.reshape(bsz, s_len, A_Q_HEADS, HEAD_DIM), positions)
    k = partial_rope(k.reshape(bsz, s_len, A_KV_HEADS, HEAD_DIM), positions)
    v = v.reshape(bsz, s_len, A_KV_HEADS, HEAD_DIM)
    a_out = sliding_window_sink_attention(q, k, v, sinks)
    b_out = chunked_spatial_gating(jax.nn.gelu(bu), jax.nn.gelu(bv), sgu_ln_g, sgu_ln_b, sgu_w, sgu_b)
    return jnp.concatenate([a_out, b_out], axis=-1) @ w_out


def odd_mixer(x, w_in, pool_w, pool_scale, w_out):
    bsz, s_len, _ = x.shape
    h = x @ w_in
    hf = h.astype(jnp.float32)
    csum = jnp.cumsum(hf, axis=1)
    counts_base = jnp.arange(1, s_len + 1, dtype=jnp.float32)
    groups = []
    for g, win in enumerate(POOL_WINDOWS):
        sl = slice(g * POOL_GROUP_DIM, (g + 1) * POOL_GROUP_DIM)
        c = csum[..., sl]
        lagged = jnp.pad(c, ((0, 0), (win, 0), (0, 0)))[:, :s_len]
        count = jnp.minimum(counts_base, float(win))[None, :, None]
        groups.append((c - lagged) / count - hf[..., sl])
    pooled = jnp.stack(groups, axis=2).astype(x.dtype)
    mapped = jnp.einsum('bsgc,gcd->bsgd', pooled, pool_w).reshape(bsz, s_len, POOL_WIDTH)
    return (mapped * pool_scale) @ w_out


def memory_cross_attention(x, mem, wq, wkv, wo):
    bsz, s_len, _ = x.shape
    q = (x @ wq).reshape(bsz, s_len, X_HEADS, X_HEAD_DIM)
    kv = (mem @ wkv).reshape(bsz, mem.shape[1], 2, X_HEADS, X_HEAD_DIM)
    k, v = kv[:, :, 0], kv[:, :, 1]
    s = jnp.einsum('bshd,bmhd->bhsm', q, k).astype(jnp.float32) * (X_HEAD_DIM ** -0.5)
    p = jax.nn.softmax(s, axis=-1).astype(v.dtype)
    o = jnp.einsum('bhsm,bmhd->bshd', p, v).reshape(bsz, s_len, D_MODEL)
    return o @ wo


def grouped_moe(x, router_w, router_bias, w_gate, w_up, w_down):
    bsz, s_len, d = x.shape
    xt = x.reshape(-1, d)
    scores = jax.nn.softmax((xt @ router_w).astype(jnp.float32), axis=-1)
    biased = scores + router_bias.astype(jnp.float32)
    grouped = biased.reshape(-1, N_EXPERT_GROUPS, EXPERTS_PER_GROUP)
    group_score = jnp.sum(lax.top_k(grouped, TOP_K)[0], axis=-1)
    g_sel = jnp.argmax(group_score, axis=-1)
    in_group = jnp.take_along_axis(grouped, g_sel[:, None, None], axis=1)[:, 0]
    _, local_idx = lax.top_k(in_group, TOP_K)
    expert_idx = g_sel[:, None] * EXPERTS_PER_GROUP + local_idx
    sel = jnp.take_along_axis(scores, expert_idx, axis=1)
    weights = sel / jnp.sum(sel, axis=-1, keepdims=True)
    combine = jnp.sum(jax.nn.one_hot(expert_idx, N_EXPERTS, dtype=jnp.float32) * weights[..., None], axis=1)
    y = jnp.zeros(xt.shape, dtype=jnp.float32)
    for e in range(N_EXPERTS):
        he = jax.nn.silu(xt @ w_gate[e]) * (xt @ w_up[e])
        y = y + combine[:, e:e + 1] * (he @ w_down[e]).astype(jnp.float32)
    return y.astype(x.dtype).reshape(bsz, s_len, d)


def setup_inputs(seed: int = 0) -> dict:
    key = jax.random.key(seed)
    keys = iter(jax.random.split(key, 64))
    f32 = jnp.float32

    def nrm(shape, scale):
        return jax.random.normal(next(keys), shape, f32) * scale

    def gain(n):
        return 1.0 + nrm((n,), 0.05)

    def bias(n):
        return nrm((n,), 0.02)

    inp = {}
    inp["x"] = nrm((BATCH, SEQ, D_MODEL), 1.0)
    inp["mem"] = nrm((BATCH, MEM_LEN, D_MODEL), 1.0)
    offset = jax.random.randint(next(keys), (BATCH, 1), 0, 4096, dtype=jnp.int32)
    inp["positions"] = (offset + jnp.arange(SEQ, dtype=jnp.int32)[None, :]).astype(jnp.int32)
    inp["router_w"] = nrm((D_MODEL, N_EXPERTS), D_MODEL ** -0.5)
    inp["router_bias"] = nrm((N_EXPERTS,), 0.01)

    def common(prefix):
        inp[prefix + "ln1_g"] = gain(D_MODEL)
        inp[prefix + "ln1_b"] = bias(D_MODEL)
        inp[prefix + "xq"] = nrm((D_MODEL, D_MODEL), D_MODEL ** -0.5)
        inp[prefix + "xkv"] = nrm((D_MODEL, 2 * D_MODEL), D_MODEL ** -0.5)
        inp[prefix + "xo"] = nrm((D_MODEL, D_MODEL), BETA * D_MODEL ** -0.5)
        inp[prefix + "ln2_g"] = gain(D_MODEL)
        inp[prefix + "ln2_b"] = bias(D_MODEL)
        inp[prefix + "e_gate"] = nrm((N_EXPERTS, D_MODEL, D_EXPERT), D_MODEL ** -0.5)
        inp[prefix + "e_up"] = nrm((N_EXPERTS, D_MODEL, D_EXPERT), D_MODEL ** -0.5)
        inp[prefix + "e_down"] = nrm((N_EXPERTS, D_EXPERT, D_MODEL), BETA * D_EXPERT ** -0.5)
        inp[prefix + "ln3_g"] = gain(D_MODEL)
        inp[prefix + "ln3_b"] = bias(D_MODEL)

    inp["l0_w_in"] = nrm((D_MODEL, EVEN_IN_WIDTH), D_MODEL ** -0.5)
    inp["l0_sinks"] = nrm((A_Q_HEADS,), 0.5)
    inp["l0_sgu_ln_g"] = gain(B_WIDTH)
    inp["l0_sgu_ln_b"] = bias(B_WIDTH)
    inp["l0_sgu_w"] = nrm((B_GROUPS, CHUNK, CHUNK), CHUNK ** -0.5)
    inp["l0_sgu_b"] = 1.0 + nrm((B_GROUPS, CHUNK), 0.1)
    inp["l0_w_out"] = nrm((EVEN_MIX_WIDTH, D_MODEL), BETA * EVEN_MIX_WIDTH ** -0.5)
    common("l0_")
    inp["l1_w_in"] = nrm((D_MODEL, POOL_WIDTH), D_MODEL ** -0.5)
    inp["l1_pool_w"] = nrm((POOL_GROUPS, POOL_GROUP_DIM, POOL_GROUP_DIM), POOL_GROUP_DIM ** -0.5)
    inp["l1_pool_scale"] = 1.0 + nrm((POOL_WIDTH,), 0.1)
    inp["l1_w_out"] = nrm((POOL_WIDTH, D_MODEL), BETA * POOL_WIDTH ** -0.5)
    common("l1_")

    order = ["x", "mem", "positions", "router_w", "router_bias",
             "l0_w_in", "l0_sinks", "l0_sgu_ln_g", "l0_sgu_ln_b", "l0_sgu_w", "l0_sgu_b", "l0_w_out",
             "l0_ln1_g", "l0_ln1_b", "l0_xq", "l0_xkv", "l0_xo", "l0_ln2_g", "l0_ln2_b",
             "l0_e_gate", "l0_e_up", "l0_e_down", "l0_ln3_g", "l0_ln3_b",
             "l1_w_in", "l1_pool_w", "l1_pool_scale", "l1_w_out",
             "l1_ln1_g", "l1_ln1_b", "l1_xq", "l1_xkv", "l1_xo", "l1_ln2_g", "l1_ln2_b",
             "l1_e_gate", "l1_e_up", "l1_e_down", "l1_ln3_g", "l1_ln3_b"]
    return {name: inp[name] for name in order}


def reference(x, mem, positions, router_w, router_bias,
              l0_w_in, l0_sinks, l0_sgu_ln_g, l0_sgu_ln_b, l0_sgu_w, l0_sgu_b, l0_w_out,
              l0_ln1_g, l0_ln1_b, l0_xq, l0_xkv, l0_xo, l0_ln2_g, l0_ln2_b,
              l0_e_gate, l0_e_up, l0_e_down, l0_ln3_g, l0_ln3_b,
              l1_w_in, l1_pool_w, l1_pool_scale, l1_w_out,
              l1_ln1_g, l1_ln1_b, l1_xq, l1_xkv, l1_xo, l1_ln2_g, l1_ln2_b,
              l1_e_gate, l1_e_up, l1_e_down, l1_ln3_g, l1_ln3_b):
    mixer_params = [
        (l0_w_in, l0_sinks, l0_sgu_ln_g, l0_sgu_ln_b, l0_sgu_w, l0_sgu_b, l0_w_out),
        (l1_w_in, l1_pool_w, l1_pool_scale, l1_w_out),
    ]
    layer_params = [
        (l0_ln1_g, l0_ln1_b, l0_xq, l0_xkv, l0_xo, l0_ln2_g, l0_ln2_b,
         l0_e_gate, l0_e_up, l0_e_down, l0_ln3_g, l0_ln3_b),
        (l1_ln1_g, l1_ln1_b, l1_xq, l1_xkv, l1_xo, l1_ln2_g, l1_ln2_b,
         l1_e_gate, l1_e_up, l1_e_down, l1_ln3_g, l1_ln3_b),
    ]
    for layer in range(DEPTH):
        if layer % 2 == 0:
            mix = even_mixer(x, positions, *mixer_params[layer])
        else:
            mix = odd_mixer(x, *mixer_params[layer])
        (ln1_g, ln1_b, xq, xkv, xo, ln2_g, ln2_b,
         e_gate, e_up, e_down, ln3_g, ln3_b) = layer_params[layer]
        x = layer_norm(ALPHA * x + mix, ln1_g, ln1_b)
        x = layer_norm(ALPHA * x + memory_cross_attention(x, mem, xq, xkv, xo), ln2_g, ln2_b)
        x = layer_norm(ALPHA * x + grouped_moe(x, router_w, router_bias, e_gate, e_up, e_down), ln3_g, ln3_b)
    return x
```

```python
import functools
import math

import jax
import jax.numpy as jnp
from jax import lax
from jax.experimental import pallas as pl
from jax.experimental.pallas import tpu as pltpu
from jax.experimental.pallas import tpu_sc as plsc

F32 = jnp.float32
BF16 = jnp.bfloat16
I32 = jnp.int32

DEPTH = 2
ALPHA = (2.0 * DEPTH) ** 0.25
LN_EPS = 1e-5

HEAD_DIM = 64
A_Q_HEADS = 8
A_KV_HEADS = 2
A_GROUP = A_Q_HEADS // A_KV_HEADS
BLOCK = 128
ROPE_THETA = 500000.0
ROPE_DIM = HEAD_DIM // 4
A_WIDTH = A_Q_HEADS * HEAD_DIM
KV_WIDTH = A_KV_HEADS * HEAD_DIM
B_GROUPS = 8
B_GROUP_DIM = 64
B_WIDTH = B_GROUPS * B_GROUP_DIM
POOL_WINDOWS = (2, 4, 8, 16)
POOL_HALO = 16
X_HEADS = 4
N_EXPERTS = 16
N_EXPERT_GROUPS = 4
EXPERTS_PER_GROUP = 4

TOKEN_TILE = 512
EXPERT_ROWS = 256
SC_WORKERS = 32
SC_LANES = 16
VMEM_LIMIT = 56 * 1024 * 1024
NEG_BIG = -1e30


def _layer_norm(z, g, b):
    mu = jnp.mean(z, axis=-1, keepdims=True)
    d = z - mu
    var = jnp.mean(d * d, axis=-1, keepdims=True)
    return d * lax.rsqrt(var + LN_EPS) * g + b


def _dot(a, b):
    return jnp.dot(a, b, preferred_element_type=F32)


def _dot_nt(a, b):
    return lax.dot_general(a, b, (((1,), (1,)), ((), ())), preferred_element_type=F32)


def _split_bf16(v):
    hi = v.astype(BF16)
    lo = (v - hi.astype(F32)).astype(BF16)
    return hi, lo


def _tc_params(n_axes):
    return pltpu.CompilerParams(dimension_semantics=("arbitrary",) * n_axes,
                                vmem_limit_bytes=VMEM_LIMIT)


def _const_spec(shape):
    nd = len(shape)
    return pl.BlockSpec(shape, lambda *_: (0,) * nd)


def _mixer0_kernel(sinks_ref, x_ref, pos_ref, win_ref, invf_ref, m1_ref, m2_ref, gsum_ref,
                   lng_ref, lnb_ref, ws_ref, bs_ref, wout_ref, g1_ref, b1_ref,
                   o_ref, q_s, kv_s, u_s, vn_s, mix_s, wt_s):
    b = pl.program_id(0)
    j = pl.program_id(1)
    tq = x_ref.shape[0]
    nblk = tq // BLOCK

    @pl.when(jnp.logical_and(b == 0, j == 0))
    def _():
        r = lax.broadcasted_iota(I32, (BLOCK, BLOCK), 0)
        c = lax.broadcasted_iota(I32, (BLOCK, BLOCK), 1)
        for g in range(B_GROUPS):
            wt_s[g] = jnp.where(c <= r, ws_ref[g], 0.0).astype(BF16)

    @pl.when(j == 0)
    def _():
        kv_s[0:BLOCK, :] = jnp.zeros((BLOCK, 2 * KV_WIDTH), BF16)

    x = x_ref[...]
    h = _dot(x.astype(BF16), win_ref[...])

    ang = pos_ref[...].astype(F32) * invf_ref[...]
    cs = jnp.cos(ang)
    sn = jnp.sin(ang)
    sa = sn * m1_ref[...]
    sb = sn * m2_ref[...]

    def rope(t):
        return t * cs + pltpu.roll(t, ROPE_DIM // 2, 1) * sa + pltpu.roll(t, 128 - ROPE_DIM // 2, 1) * sb

    for c in range(A_WIDTH // 128):
        t = h[:, c * 128:(c + 1) * 128] * (HEAD_DIM ** -0.5)
        q_s[:, c * 128:(c + 1) * 128] = rope(t).astype(BF16)
    c1 = A_WIDTH
    c2 = c1 + KV_WIDTH
    c3 = c2 + KV_WIDTH
    c4 = c3 + B_WIDTH
    kv_s[BLOCK:, 0:KV_WIDTH] = rope(h[:, c1:c2]).astype(BF16)
    kv_s[BLOCK:, KV_WIDTH:] = h[:, c2:c3].astype(BF16)

    u_s[...] = jax.nn.gelu(h[:, c3:c4])
    v = jax.nn.gelu(h[:, c4:])
    gsum = gsum_ref[...]
    vh, vl = _split_bf16(v)
    mean = (_dot(vh, gsum) + _dot(vl, gsum)) * (1.0 / B_GROUP_DIM)
    d = v - mean
    dh, dl = _split_bf16(d * d)
    var = (_dot(dh, gsum) + _dot(dl, gsum)) * (1.0 / B_GROUP_DIM)
    vn_s[...] = (d * lax.rsqrt(var + LN_EPS) * lng_ref[...] + lnb_ref[...]).astype(BF16)

    qi = lax.broadcasted_iota(I32, (BLOCK, 2 * BLOCK), 0)
    kj = lax.broadcasted_iota(I32, (BLOCK, 2 * BLOCK), 1)
    rel = qi + BLOCK - kj
    band = jnp.logical_and(rel >= 0, rel < BLOCK)

    def block_body(n, carry):
        r0 = pl.multiple_of(n * BLOCK, BLOCK)
        kv = kv_s[pl.ds(r0, 2 * BLOCK), :]
        qb = q_s[pl.ds(r0, BLOCK), :]
        first = jnp.logical_and(j == 0, n == 0)
        valid = jnp.logical_and(band, kj >= jnp.where(first, BLOCK, 0))
        for hq in range(A_Q_HEADS):
            hk = hq // A_GROUP
            qh = qb[:, hq * HEAD_DIM:(hq + 1) * HEAD_DIM]
            kh = kv[:, hk * HEAD_DIM:(hk + 1) * HEAD_DIM]
            vv = kv[:, KV_WIDTH + hk * HEAD_DIM:KV_WIDTH + (hk + 1) * HEAD_DIM]
            s = jnp.where(valid, _dot_nt(qh, kh), NEG_BIG)
            sink = sinks_ref[hq]
            m = jnp.maximum(jnp.max(s, axis=-1, keepdims=True), sink)
            p = jnp.exp(s - m)
            den = jnp.sum(p, axis=-1, keepdims=True) + jnp.exp(sink - m)
            o = _dot(p.astype(BF16), vv) / den
            mix_s[pl.ds(r0, BLOCK), hq * HEAD_DIM:(hq + 1) * HEAD_DIM] = o.astype(BF16)
        vnb = vn_s[pl.ds(r0, BLOCK), :]
        parts = [_dot(wt_s[g], vnb[:, g * B_GROUP_DIM:(g + 1) * B_GROUP_DIM]) for g in range(B_GROUPS)]
        mixed = jnp.concatenate(parts, axis=1) + bs_ref[...]
        mix_s[pl.ds(r0, BLOCK), A_WIDTH:] = (u_s[pl.ds(r0, BLOCK), :] * mixed).astype(BF16)
        return carry

    lax.fori_loop(0, nblk, block_body, 0)
    kv_s[0:BLOCK, :] = kv_s[tq:tq + BLOCK, :]

    z = ALPHA * x + _dot(mix_s[...], wout_ref[...])
    o_ref[...] = _layer_norm(z, g1_ref[...], b1_ref[...])


def _mixer0(x, pos_col, sinks, w_in, invf, m1, m2, gsum, lng, lnb, w_s, bs_full, w_out, g1, b1, bsz, s_len):
    t_all, d = x.shape
    tq = TOKEN_TILE
    nj = s_len // tq
    row = lambda bb, jj: (bb * nj + jj, 0)
    in_w = w_in.shape[1]
    return pl.pallas_call(
        _mixer0_kernel,
        grid=(bsz, nj),
        in_specs=[
            pl.BlockSpec(memory_space=pltpu.SMEM),
            pl.BlockSpec((tq, d), row),
            pl.BlockSpec((tq, 1), row),
            _const_spec((d, in_w)),
            _const_spec((1, 128)), _const_spec((1, 128)), _const_spec((1, 128)),
            _const_spec((B_WIDTH, B_WIDTH)),
            _const_spec((1, B_WIDTH)), _const_spec((1, B_WIDTH)),
            _const_spec((B_GROUPS, BLOCK, BLOCK)),
            _const_spec((BLOCK, B_WIDTH)),
            _const_spec((A_WIDTH + B_WIDTH, d)),
            _const_spec((1, d)), _const_spec((1, d)),
        ],
        out_specs=pl.BlockSpec((tq, d), row),
        out_shape=jax.ShapeDtypeStruct((t_all, d), F32),
        scratch_shapes=[
            pltpu.VMEM((tq, A_WIDTH), BF16),
            pltpu.VMEM((tq + BLOCK, 2 * KV_WIDTH), BF16),
            pltpu.VMEM((tq, B_WIDTH), F32),
            pltpu.VMEM((tq, B_WIDTH), BF16),
            pltpu.VMEM((tq, A_WIDTH + B_WIDTH), BF16),
            pltpu.VMEM((B_GROUPS, BLOCK, BLOCK), BF16),
        ],
        compiler_params=_tc_params(2),
        name="mixer0",
    )(sinks, x, pos_col, w_in, invf, m1, m2, gsum, lng, lnb, w_s, bs_full, w_out, g1, b1)


def _mixer1_kernel(x_ref, win_ref, pw_ref, ps_ref, wout_ref, g1_ref, b1_ref, o_ref, h_s, mp_s):
    j = pl.program_id(1)
    tq = x_ref.shape[0]
    gw = x_ref.shape[1] // len(POOL_WINDOWS)

    @pl.when(j == 0)
    def _():
        h_s[0:POOL_HALO, :] = jnp.zeros((POOL_HALO, h_s.shape[1]), F32)

    x = x_ref[...]
    h_s[POOL_HALO:, :] = _dot(x.astype(BF16), win_ref[...])
    t_pos = j * tq + lax.broadcasted_iota(I32, (tq, 1), 0)
    for g, win in enumerate(POOL_WINDOWS):
        lo, hi = g * gw, (g + 1) * gw
        cur = h_s[POOL_HALO:, lo:hi]
        acc = cur
        for k in range(1, win):
            acc = acc + h_s[POOL_HALO - k:POOL_HALO - k + tq, lo:hi]
        count = jnp.minimum(t_pos + 1, win).astype(F32)
        pooled = acc / count - cur
        mapped = _dot(pooled.astype(BF16), pw_ref[g])
        mp_s[:, lo:hi] = (mapped * ps_ref[:, lo:hi]).astype(BF16)
    h_s[0:POOL_HALO, :] = h_s[tq:tq + POOL_HALO, :]
    z = ALPHA * x + _dot(mp_s[...], wout_ref[...])
    o_ref[...] = _layer_norm(z, g1_ref[...], b1_ref[...])


def _mixer1(x, w_in, pool_w, pool_scale, w_out, g1, b1, bsz, s_len):
    t_all, d = x.shape
    tq = TOKEN_TILE
    nj = s_len // tq
    row = lambda bb, jj: (bb * nj + jj, 0)
    ng = len(POOL_WINDOWS)
    return pl.pallas_call(
        _mixer1_kernel,
        grid=(bsz, nj),
        in_specs=[
            pl.BlockSpec((tq, d), row),
            _const_spec((d, d)),
            _const_spec((ng, d // ng, d // ng)),
            _const_spec((1, d)),
            _const_spec((d, d)),
            _const_spec((1, d)), _const_spec((1, d)),
        ],
        out_specs=pl.BlockSpec((tq, d), row),
        out_shape=jax.ShapeDtypeStruct((t_all, d), F32),
        scratch_shapes=[pltpu.VMEM((tq + POOL_HALO, d), F32), pltpu.VMEM((tq, d), BF16)],
        compiler_params=_tc_params(2),
        name="mixer1",
    )(x, w_in, pool_w, pool_scale, w_out, g1, b1)


def _kv_kernel(mem_ref, w_ref, o_ref):
    o_ref[...] = _dot(mem_ref[...].astype(BF16), w_ref[...]).astype(BF16)


def _kv_proj(mem2d, wkv, bsz):
    rows, d = mem2d.shape
    m = rows // bsz
    return pl.pallas_call(
        _kv_kernel,
        grid=(bsz,),
        in_specs=[pl.BlockSpec((m, d), lambda i: (i, 0)), _const_spec(wkv.shape)],
        out_specs=pl.BlockSpec((m, wkv.shape[1]), lambda i: (i, 0)),
        out_shape=jax.ShapeDtypeStruct((rows, wkv.shape[1]), BF16),
        compiler_params=_tc_params(1),
        name="kv_proj",
    )(mem2d, wkv)


def _top2_of4(v):
    hi01, lo01 = jnp.maximum(v[0], v[1]), jnp.minimum(v[0], v[1])
    hi23, lo23 = jnp.maximum(v[2], v[3]), jnp.minimum(v[2], v[3])
    return jnp.maximum(hi01, hi23) + jnp.maximum(jnp.minimum(hi01, hi23), jnp.maximum(lo01, lo23))


def _argmax_first(vals):
    best, idx = vals[0], jnp.zeros(vals[0].shape, I32)
    for i in range(1, len(vals)):
        better = vals[i] > best
        best = jnp.where(better, vals[i], best)
        idx = jnp.where(better, i, idx)
    return best, idx


def _xattn_kernel(x_ref, kv_ref, wq_ref, wo_ref, g2_ref, b2_ref, rwh_ref, rwl_ref, rb_ref,
                  x2_ref, xp_ref, rt_ref, cnt_ref, run_s):
    first = jnp.logical_and(pl.program_id(0) == 0, pl.program_id(1) == 0)
    tq, d = x_ref.shape
    hd = d // X_HEADS

    @pl.when(first)
    def _():
        run_s[...] = jnp.zeros(run_s.shape, F32)

    x = x_ref[...]
    q = _dot(x.astype(BF16), wq_ref[...]) * (hd ** -0.5)
    outs = []
    for hx in range(X_HEADS):
        qh = q[:, hx * hd:(hx + 1) * hd].astype(BF16)
        kh = kv_ref[:, hx * hd:(hx + 1) * hd]
        vh = kv_ref[:, d + hx * hd:d + (hx + 1) * hd]
        s = _dot_nt(qh, kh)
        p = jnp.exp(s - jnp.max(s, axis=-1, keepdims=True))
        o = _dot(p.astype(BF16), vh) / jnp.sum(p, axis=-1, keepdims=True)
        outs.append(o.astype(BF16))
    att = _dot(jnp.concatenate(outs, axis=1), wo_ref[...])
    x2 = _layer_norm(ALPHA * x + att, g2_ref[...], b2_ref[...])
    x2_ref[...] = x2

    half = d // 2
    lo = pltpu.bitcast(x2[:, :half].astype(BF16).astype(F32), jnp.uint32) >> 16
    hi = pltpu.bitcast(x2[:, half:].astype(BF16).astype(F32), jnp.uint32) & jnp.uint32(0xFFFF0000)
    xp_ref[...] = pltpu.bitcast(hi | lo, I32)

    xh, xl = _split_bf16(x2)
    logits = _dot_nt(rwh_ref[...], xh) + _dot_nt(rwh_ref[...], xl) + _dot_nt(rwl_ref[...], xh)
    e_max = jnp.max(logits, axis=0, keepdims=True)
    ex = jnp.exp(logits - e_max)
    scores = ex / jnp.sum(ex, axis=0, keepdims=True)
    biased = scores + rb_ref[...]
    sc = [scores[e:e + 1, :] for e in range(N_EXPERTS)]
    bi = [biased[e:e + 1, :] for e in range(N_EXPERTS)]
    epg = EXPERTS_PER_GROUP
    gscore = [_top2_of4(bi[g * epg:(g + 1) * epg]) for g in range(N_EXPERT_GROUPS)]
    _, gsel = _argmax_first(gscore)

    def pick(vals):
        return [functools.reduce(lambda acc, g: jnp.where(gsel == g, vals[g * epg + i], acc),
                                 range(1, N_EXPERT_GROUPS), vals[i]) for i in range(epg)]

    in_b = pick(bi)
    in_s = pick(sc)
    _, i0 = _argmax_first(in_b)
    _, i1 = _argmax_first([jnp.where(i0 == i, -jnp.inf, in_b[i]) for i in range(epg)])

    def take(vals, idx):
        return functools.reduce(lambda acc, i: jnp.where(idx == i, vals[i], acc), range(1, epg), vals[0])

    s0, s1 = take(in_s, i0), take(in_s, i1)
    w0, w1 = s0 / (s0 + s1), s1 / (s0 + s1)
    e0, e1 = gsel * epg + i0, gsel * epg + i1

    eid = lax.broadcasted_iota(I32, (N_EXPERTS, tq), 0)
    oh0 = eid == e0
    oh1 = eid == e1
    onehot = jnp.where(jnp.logical_or(oh0, oh1), 1.0, 0.0)
    rr = lax.broadcasted_iota(I32, (tq, tq), 0)
    cc = lax.broadcasted_iota(I32, (tq, tq), 1)
    upper = jnp.where(rr < cc, 1.0, 0.0).astype(BF16)
    prefix = _dot(onehot.astype(BF16), upper) + run_s[...]
    r0 = jnp.sum(jnp.where(oh0, prefix, 0.0), axis=0, keepdims=True).astype(I32)
    r1 = jnp.sum(jnp.where(oh1, prefix, 0.0), axis=0, keepdims=True).astype(I32)
    run_s[...] = run_s[...] + jnp.sum(onehot, axis=1, keepdims=True)
    cnt_ref[...] = jnp.broadcast_to(run_s[...], cnt_ref.shape).astype(I32)

    zero = jnp.zeros((1, tq), I32)
    rt_ref[...] = jnp.concatenate(
        [e0, e1, r0, r1, pltpu.bitcast(w0, I32), pltpu.bitcast(w1, I32), zero, zero], axis=0)


def _xattn_route(x1, kv, wq, wo, g2, b2, rw_hi, rw_lo, rbias, bsz, s_len):
    t_all, d = x1.shape
    tq = TOKEN_TILE
    nj = s_len // tq
    m = kv.shape[0] // bsz
    row = lambda bb, jj: (bb * nj + jj, 0)
    return pl.pallas_call(
        _xattn_kernel,
        grid=(bsz, nj),
        in_specs=[
            pl.BlockSpec((tq, d), row),
            pl.BlockSpec((m, 2 * d), lambda bb, jj: (bb, 0)),
            _const_spec((d, d)), _const_spec((d, d)),
            _const_spec((1, d)), _const_spec((1, d)),
            _const_spec((N_EXPERTS, d)), _const_spec((N_EXPERTS, d)),
            _const_spec((N_EXPERTS, 1)),
        ],
        out_specs=[
            pl.BlockSpec((tq, d), row),
            pl.BlockSpec((tq, d // 2), row),
            pl.BlockSpec((8, tq), lambda bb, jj: (0, bb * nj + jj)),
            _const_spec((N_EXPERTS, 128)),
        ],
        out_shape=[
            jax.ShapeDtypeStruct((t_all, d), F32),
            jax.ShapeDtypeStruct((t_all, d // 2), I32),
            jax.ShapeDtypeStruct((8, t_all), I32),
            jax.ShapeDtypeStruct((N_EXPERTS, 128), I32),
        ],
        scratch_shapes=[pltpu.VMEM((N_EXPERTS, 1), F32)],
        compiler_params=_tc_params(2),
        name="xattn_route",
    )(x1, kv, wq, wo, g2, b2, rw_hi, rw_lo, rbias)


def _sc_mesh():
    return plsc.VectorSubcoreMesh(core_axis_name="c", subcore_axis_name="s")


def _sc_params():
    return pltpu.CompilerParams(needs_layout_passes=False)


def _worker_id():
    return lax.axis_index("s") * lax.axis_size("c") + lax.axis_index("c")


def _sc_dispatch(xp, dest, n_rows):
    t_all, width = xp.shape
    n_dest = dest.shape[0]
    rows_w = n_rows // SC_WORKERS
    chunk = 64
    stage = min(8192, n_dest)
    assert n_rows % (SC_WORKERS * chunk) == 0 and n_dest % stage == 0
    assert t_all & (t_all - 1) == 0

    def body(x_hbm, dest_hbm, out_hbm, dest_v, src_v, buf, sem):
        base = _worker_id() * rows_w
        zeros = jnp.zeros((SC_LANES,), I32)

        @pl.loop(0, rows_w // SC_LANES)
        def _(i):
            src_v[pl.ds(i * SC_LANES, SC_LANES)] = zeros

        lane = lax.iota(I32, SC_LANES)

        @pl.loop(0, n_dest // stage)
        def _(c):
            pltpu.sync_copy(dest_hbm.at[pl.ds(c * stage, stage)], dest_v)

            @pl.loop(0, stage // SC_LANES)
            def _(i):
                dv = dest_v[pl.ds(i * SC_LANES, SC_LANES)] - base
                tok = jnp.bitwise_and(c * stage + i * SC_LANES + lane, t_all - 1)
                mine = jnp.logical_and(dv >= 0, dv < rows_w)
                plsc.store_scatter(src_v, [jnp.where(mine, dv, 0)], tok, mask=mine)

        @pl.loop(0, rows_w // chunk)
        def _(c):
            pltpu.async_copy(x_hbm.at[src_v.at[pl.ds(c * chunk, chunk)]], buf, sem).wait()
            pltpu.sync_copy(buf, out_hbm.at[pl.ds(base + c * chunk, chunk)])

    return pl.kernel(
        body,
        out_type=jax.ShapeDtypeStruct((n_rows, width), xp.dtype),
        mesh=_sc_mesh(),
        scratch_types=[
            pltpu.VMEM((stage,), I32),
            pltpu.VMEM((rows_w,), I32),
            pltpu.VMEM((chunk, width), xp.dtype),
            pltpu.SemaphoreType.DMA,
        ],
        compiler_params=_sc_params(),
        name="sc_dispatch",
    )(xp, dest)


def _sc_combine(ys, dest):
    n_dest = dest.shape[0]
    width = ys.shape[1]
    per_w = n_dest // SC_WORKERS
    chunk = 32
    assert n_dest % (SC_WORKERS * chunk) == 0

    def body(y_hbm, dest_hbm, out_hbm, idx_v, buf, sem):
        base = _worker_id() * per_w
        pltpu.sync_copy(dest_hbm.at[pl.ds(base, per_w)], idx_v)

        @pl.loop(0, per_w // chunk)
        def _(c):
            pltpu.async_copy(y_hbm.at[idx_v.at[pl.ds(c * chunk, chunk)]], buf, sem).wait()
            pltpu.sync_copy(buf, out_hbm.at[pl.ds(base + c * chunk, chunk)])

    return pl.kernel(
        body,
        out_type=jax.ShapeDtypeStruct((n_dest, width), ys.dtype),
        mesh=_sc_mesh(),
        scratch_types=[
            pltpu.VMEM((per_w,), I32),
            pltpu.VMEM((chunk, width), ys.dtype),
            pltpu.SemaphoreType.DMA,
        ],
        compiler_params=_sc_params(),
        name="sc_combine",
    )(ys, dest)


def _ffn_kernel(be_ref, nu_ref, xs_ref, wg_ref, wu_ref, wd_ref, o_ref):
    @pl.when(pl.program_id(0) < nu_ref[0])
    def _():
        w = pltpu.bitcast(xs_ref[...], jnp.uint32)
        lo = pltpu.bitcast(w << 16, F32)
        hi = pltpu.bitcast(w & jnp.uint32(0xFFFF0000), F32)
        xb = jnp.concatenate([lo, hi], axis=1).astype(BF16)
        act = jax.nn.silu(_dot(xb, wg_ref[0])) * _dot(xb, wu_ref[0])
        o_ref[...] = _dot(act.astype(BF16), wd_ref[0])


def _expert_ffn(xs, blk_expert, n_used, w_gate, w_up, w_down):
    n_rows, half = xs.shape
    d = 2 * half
    de = w_gate.shape[2]
    bm = EXPERT_ROWS
    rows = lambda i, be, nu: (jnp.minimum(i, nu[0] - 1), 0)
    wsel = lambda i, be, nu: (be[i], 0, 0)
    return pl.pallas_call(
        _ffn_kernel,
        grid_spec=pltpu.PrefetchScalarGridSpec(
            num_scalar_prefetch=2,
            grid=(n_rows // bm,),
            in_specs=[
                pl.BlockSpec((bm, half), rows),
                pl.BlockSpec((1, d, de), wsel),
                pl.BlockSpec((1, d, de), wsel),
                pl.BlockSpec((1, de, d), wsel),
            ],
            out_specs=pl.BlockSpec((bm, d), rows),
        ),
        out_shape=jax.ShapeDtypeStruct((n_rows, d), F32),
        compiler_params=_tc_params(1),
        name="expert_ffn",
    )(blk_expert, n_used, xs, w_gate, w_up, w_down)


def _combine_kernel(x_ref, y0_ref, y1_ref, rt_ref, g_ref, b_ref, o_ref):
    wt = pltpu.bitcast(rt_ref[...], F32).T
    y = wt[:, 4:5] * y0_ref[...] + wt[:, 5:6] * y1_ref[...]
    o_ref[...] = _layer_norm(ALPHA * x_ref[...] + y, g_ref[...], b_ref[...])


def _combine_ln(x2, y01, rt, g3, b3):
    t_all, d = x2.shape
    tq = TOKEN_TILE
    nt = t_all // tq
    return pl.pallas_call(
        _combine_kernel,
        grid=(nt,),
        in_specs=[
            pl.BlockSpec((tq, d), lambda i: (i, 0)),
            pl.BlockSpec((tq, d), lambda i: (i, 0)),
            pl.BlockSpec((tq, d), lambda i: (i + nt, 0)),
            pl.BlockSpec((8, tq), lambda i: (0, i)),
            _const_spec((1, d)), _const_spec((1, d)),
        ],
        out_specs=pl.BlockSpec((tq, d), lambda i: (i, 0)),
        out_shape=jax.ShapeDtypeStruct((t_all, d), F32),
        compiler_params=_tc_params(1),
        name="combine_ln",
    )(x2, y01, y01, rt, g3, b3)


def _routing_tables(rt, counts):
    bm = EXPERT_ROWS
    t_all = rt.shape[1]
    n_rows = 2 * t_all + N_EXPERTS * bm
    cnt = counts[:, 0]
    padded = (cnt + bm - 1) // bm * bm
    ends = jnp.cumsum(padded)
    offs = ends - padded
    experts = rt[0:2]
    off_tok = jnp.sum(jnp.where(experts[None] == jnp.arange(N_EXPERTS, dtype=I32)[:, None, None],
                                offs[:, None, None], 0), axis=0)
    dest = (off_tok + rt[2:4]).reshape(-1).astype(I32)
    blk_start = jnp.arange(n_rows // bm, dtype=I32) * bm
    blk_expert = jnp.minimum(jnp.sum(blk_start[:, None] >= ends[None, :], axis=1), N_EXPERTS - 1).astype(I32)
    n_used = (ends[-1:] // bm).astype(I32)
    return dest, blk_expert, n_used, n_rows


def _layer_tail(x1, mem2d, p, router, bsz, s_len):
    rw_hi, rw_lo, rbias = router
    kv = _kv_proj(mem2d, p["xkv"], bsz)
    x2, xp, rt, counts = _xattn_route(x1, kv, p["xq"], p["xo"], p["ln2_g"], p["ln2_b"],
                                      rw_hi, rw_lo, rbias, bsz, s_len)
    dest, blk_expert, n_used, n_rows = _routing_tables(rt, counts)
    xs = _sc_dispatch(xp, dest, n_rows)
    ys = _expert_ffn(xs, blk_expert, n_used, p["e_gate"], p["e_up"], p["e_down"])
    y01 = _sc_combine(ys, dest)
    return _combine_ln(x2, y01, rt, p["ln3_g"], p["ln3_b"])


def _row(v):
    return v.reshape(1, -1).astype(F32)


def _common_params(xq, xkv, xo, ln2_g, ln2_b, e_gate, e_up, e_down, ln3_g, ln3_b):
    return dict(xq=xq.astype(BF16), xkv=xkv.astype(BF16), xo=xo.astype(BF16),
                ln2_g=_row(ln2_g), ln2_b=_row(ln2_b),
                e_gate=e_gate.astype(BF16), e_up=e_up.astype(BF16), e_down=e_down.astype(BF16),
                ln3_g=_row(ln3_g), ln3_b=_row(ln3_b))


def kernel(x, mem, positions, router_w, router_bias, l0_w_in, l0_sinks, l0_sgu_ln_g, l0_sgu_ln_b, l0_sgu_w, l0_sgu_b, l0_w_out, l0_ln1_g, l0_ln1_b, l0_xq, l0_xkv, l0_xo, l0_ln2_g, l0_ln2_b, l0_e_gate, l0_e_up, l0_e_down, l0_ln3_g, l0_ln3_b, l1_w_in, l1_pool_w, l1_pool_scale, l1_w_out, l1_ln1_g, l1_ln1_b, l1_xq, l1_xkv, l1_xo, l1_ln2_g, l1_ln2_b, l1_e_gate, l1_e_up, l1_e_down, l1_ln3_g, l1_ln3_b):
    bsz, s_len, d = x.shape
    assert s_len % TOKEN_TILE == 0 and TOKEN_TILE % BLOCK == 0
    xt = x.reshape(bsz * s_len, d)
    mem2d = mem.reshape(-1, d)
    pos_col = positions.reshape(-1, 1).astype(I32)

    rw_t = router_w.T.astype(F32)
    rw_hi = rw_t.astype(BF16)
    rw_lo = (rw_t - rw_hi.astype(F32)).astype(BF16)
    router = (rw_hi, rw_lo, router_bias.reshape(-1, 1).astype(F32))

    half = ROPE_DIM // 2
    inv_freq = ROPE_THETA ** (-(jnp.arange(half, dtype=F32) * 2.0 / ROPE_DIM))
    lane = jnp.arange(128) % HEAD_DIM
    invf = jnp.where(lane < ROPE_DIM, inv_freq[lane % half], 0.0).reshape(1, 128).astype(F32)
    m1 = jnp.where((lane >= half) & (lane < ROPE_DIM), 1.0, 0.0).reshape(1, 128).astype(F32)
    m2 = jnp.where(lane < half, -1.0, 0.0).reshape(1, 128).astype(F32)
    grp = jnp.arange(B_WIDTH) // B_GROUP_DIM
    gsum = (grp[:, None] == grp[None, :]).astype(BF16)
    bs_full = jnp.repeat(l0_sgu_b.T.astype(F32), B_GROUP_DIM, axis=1)

    x1 = _mixer0(xt, pos_col, l0_sinks.astype(F32), l0_w_in.astype(BF16), invf, m1, m2, gsum,
                 _row(l0_sgu_ln_g), _row(l0_sgu_ln_b), l0_sgu_w.astype(F32), bs_full,
                 l0_w_out.astype(BF16), _row(l0_ln1_g), _row(l0_ln1_b), bsz, s_len)
    p0 = _common_params(l0_xq, l0_xkv, l0_xo, l0_ln2_g, l0_ln2_b, l0_e_gate, l0_e_up, l0_e_down,
                        l0_ln3_g, l0_ln3_b)
    x3 = _layer_tail(x1, mem2d, p0, router, bsz, s_len)

    x1 = _mixer1(x3, l1_w_in.astype(BF16), l1_pool_w.astype(BF16), _row(l1_pool_scale),
                 l1_w_out.astype(BF16), _row(l1_ln1_g), _row(l1_ln1_b), bsz, s_len)
    p1 = _common_params(l1_xq, l1_xkv, l1_xo, l1_ln2_g, l1_ln2_b, l1_e_gate, l1_e_up, l1_e_down,
                        l1_ln3_g, l1_ln3_b)
    x3 = _layer_tail(x1, mem2d, p1, router, bsz, s_len)
    return x3.reshape(bsz, s_len, d)
```

```python
import functools
import math

import jax
import jax.numpy as jnp
from jax import lax
from jax.experimental import pallas as pl
from jax.experimental.pallas import tpu as pltpu
from jax.experimental.pallas import tpu_sc as plsc

F32 = jnp.float32
BF16 = jnp.bfloat16
I32 = jnp.int32

DEPTH = 2
ALPHA = (2.0 * DEPTH) ** 0.25
LN_EPS = 1e-5

HEAD_DIM = 64
A_Q_HEADS = 8
A_KV_HEADS = 2
A_GROUP = A_Q_HEADS // A_KV_HEADS
BLOCK = 128
ROPE_THETA = 500000.0
ROPE_DIM = HEAD_DIM // 4
A_WIDTH = A_Q_HEADS * HEAD_DIM
KV_WIDTH = A_KV_HEADS * HEAD_DIM
B_GROUPS = 8
B_GROUP_DIM = 64
B_WIDTH = B_GROUPS * B_GROUP_DIM
POOL_WINDOWS = (2, 4, 8, 16)
POOL_HALO = 16
X_HEADS = 4
N_EXPERTS = 16
N_EXPERT_GROUPS = 4
EXPERTS_PER_GROUP = 4

TOKEN_TILE = 512
EXPERT_ROWS = 256
SC_WORKERS = 32
SC_CHUNK = 64
VMEM_LIMIT = 56 * 1024 * 1024
NEG_BIG = -1e30


def _layer_norm(z, g, b):
    mu = jnp.mean(z, axis=-1, keepdims=True)
    d = z - mu
    var = jnp.mean(d * d, axis=-1, keepdims=True)
    return d * lax.rsqrt(var + LN_EPS) * g + b


def _dot(a, b):
    return jnp.dot(a, b, preferred_element_type=F32)


def _dot_nt(a, b):
    return lax.dot_general(a, b, (((1,), (1,)), ((), ())), preferred_element_type=F32)


def _split_bf16(v):
    hi = v.astype(BF16)
    lo = (v - hi.astype(F32)).astype(BF16)
    return hi, lo


def _tc_params(n_axes):
    return pltpu.CompilerParams(dimension_semantics=("arbitrary",) * n_axes,
                                vmem_limit_bytes=VMEM_LIMIT)


def _const_spec(shape):
    nd = len(shape)
    return pl.BlockSpec(shape, lambda *_: (0,) * nd)


def _mixer0_kernel(sinks_ref, x_ref, pos_ref, win_ref, invf_ref, m1_ref, m2_ref, gsum_ref,
                   lng_ref, lnb_ref, ws_ref, bs_ref, wout_ref, g1_ref, b1_ref,
                   o_ref, q_s, kv_s, u_s, vn_s, mix_s, wt_s):
    b = pl.program_id(0)
    j = pl.program_id(1)
    tq = x_ref.shape[0]
    nblk = tq // BLOCK

    @pl.when(jnp.logical_and(b == 0, j == 0))
    def _():
        r = lax.broadcasted_iota(I32, (BLOCK, BLOCK), 0)
        c = lax.broadcasted_iota(I32, (BLOCK, BLOCK), 1)
        for g in range(B_GROUPS):
            wt_s[g] = jnp.where(c <= r, ws_ref[g], 0.0).astype(BF16)

    @pl.when(j == 0)
    def _():
        kv_s[0:BLOCK, :] = jnp.zeros((BLOCK, 2 * KV_WIDTH), BF16)

    x = x_ref[...]
    h = _dot(x.astype(BF16), win_ref[...])

    ang = pos_ref[...].astype(F32) * invf_ref[...]
    cs = jnp.cos(ang)
    sn = jnp.sin(ang)
    sa = sn * m1_ref[...]
    sb = sn * m2_ref[...]

    def rope(t):
        return t * cs + pltpu.roll(t, ROPE_DIM // 2, 1) * sa + pltpu.roll(t, 128 - ROPE_DIM // 2, 1) * sb

    for c in range(A_WIDTH // 128):
        t = h[:, c * 128:(c + 1) * 128] * (HEAD_DIM ** -0.5)
        q_s[:, c * 128:(c + 1) * 128] = rope(t).astype(BF16)
    c1 = A_WIDTH
    c2 = c1 + KV_WIDTH
    c3 = c2 + KV_WIDTH
    c4 = c3 + B_WIDTH
    kv_s[BLOCK:, 0:KV_WIDTH] = rope(h[:, c1:c2]).astype(BF16)
    kv_s[BLOCK:, KV_WIDTH:] = h[:, c2:c3].astype(BF16)

    u_s[...] = jax.nn.gelu(h[:, c3:c4])
    v = jax.nn.gelu(h[:, c4:])
    gsum = gsum_ref[...]
    vh, vl = _split_bf16(v)
    mean = (_dot(vh, gsum) + _dot(vl, gsum)) * (1.0 / B_GROUP_DIM)
    d = v - mean
    dh, dl = _split_bf16(d * d)
    var = (_dot(dh, gsum) + _dot(dl, gsum)) * (1.0 / B_GROUP_DIM)
    vn_s[...] = (d * lax.rsqrt(var + LN_EPS) * lng_ref[...] + lnb_ref[...]).astype(BF16)

    qi = lax.broadcasted_iota(I32, (BLOCK, 2 * BLOCK), 0)
    kj = lax.broadcasted_iota(I32, (BLOCK, 2 * BLOCK), 1)
    rel = qi + BLOCK - kj
    band = jnp.logical_and(rel >= 0, rel < BLOCK)

    def block_body(n, carry):
        r0 = pl.multiple_of(n * BLOCK, BLOCK)
        kv = kv_s[pl.ds(r0, 2 * BLOCK), :]
        qb = q_s[pl.ds(r0, BLOCK), :]
        first = jnp.logical_and(j == 0, n == 0)
        valid = jnp.logical_and(band, kj >= jnp.where(first, BLOCK, 0))
        for hq in range(A_Q_HEADS):
            hk = hq // A_GROUP
            qh = qb[:, hq * HEAD_DIM:(hq + 1) * HEAD_DIM]
            kh = kv[:, hk * HEAD_DIM:(hk + 1) * HEAD_DIM]
            vv = kv[:, KV_WIDTH + hk * HEAD_DIM:KV_WIDTH + (hk + 1) * HEAD_DIM]
            s = jnp.where(valid, _dot_nt(qh, kh), NEG_BIG)
            sink = sinks_ref[hq]
            m = jnp.maximum(jnp.max(s, axis=-1, keepdims=True), sink)
            p = jnp.exp(s - m)
            den = jnp.sum(p, axis=-1, keepdims=True) + jnp.exp(sink - m)
            o = _dot(p.astype(BF16), vv) / den
            mix_s[pl.ds(r0, BLOCK), hq * HEAD_DIM:(hq + 1) * HEAD_DIM] = o.astype(BF16)
        vnb = vn_s[pl.ds(r0, BLOCK), :]
        parts = [_dot(wt_s[g], vnb[:, g * B_GROUP_DIM:(g + 1) * B_GROUP_DIM]) for g in range(B_GROUPS)]
        mixed = jnp.concatenate(parts, axis=1) + bs_ref[...]
        mix_s[pl.ds(r0, BLOCK), A_WIDTH:] = (u_s[pl.ds(r0, BLOCK), :] * mixed).astype(BF16)
        return carry

    lax.fori_loop(0, nblk, block_body, 0)
    kv_s[0:BLOCK, :] = kv_s[tq:tq + BLOCK, :]

    z = ALPHA * x + _dot(mix_s[...], wout_ref[...])
    o_ref[...] = _layer_norm(z, g1_ref[...], b1_ref[...])


def _mixer0(x, pos_col, sinks, w_in, invf, m1, m2, gsum, lng, lnb, w_s, bs_full, w_out, g1, b1, bsz, s_len):
    t_all, d = x.shape
    tq = TOKEN_TILE
    nj = s_len // tq
    row = lambda bb, jj: (bb * nj + jj, 0)
    in_w = w_in.shape[1]
    return pl.pallas_call(
        _mixer0_kernel,
        grid=(bsz, nj),
        in_specs=[
            pl.BlockSpec(memory_space=pltpu.SMEM),
            pl.BlockSpec((tq, d), row),
            pl.BlockSpec((tq, 1), row),
            _const_spec((d, in_w)),
            _const_spec((1, 128)), _const_spec((1, 128)), _const_spec((1, 128)),
            _const_spec((B_WIDTH, B_WIDTH)),
            _const_spec((1, B_WIDTH)), _const_spec((1, B_WIDTH)),
            _const_spec((B_GROUPS, BLOCK, BLOCK)),
            _const_spec((BLOCK, B_WIDTH)),
            _const_spec((A_WIDTH + B_WIDTH, d)),
            _const_spec((1, d)), _const_spec((1, d)),
        ],
        out_specs=pl.BlockSpec((tq, d), row),
        out_shape=jax.ShapeDtypeStruct((t_all, d), F32),
        scratch_shapes=[
            pltpu.VMEM((tq, A_WIDTH), BF16),
            pltpu.VMEM((tq + BLOCK, 2 * KV_WIDTH), BF16),
            pltpu.VMEM((tq, B_WIDTH), F32),
            pltpu.VMEM((tq, B_WIDTH), BF16),
            pltpu.VMEM((tq, A_WIDTH + B_WIDTH), BF16),
            pltpu.VMEM((B_GROUPS, BLOCK, BLOCK), BF16),
        ],
        compiler_params=_tc_params(2),
        name="mixer0",
    )(sinks, x, pos_col, w_in, invf, m1, m2, gsum, lng, lnb, w_s, bs_full, w_out, g1, b1)


def _mixer1_kernel(x_ref, win_ref, pw_ref, ps_ref, wout_ref, g1_ref, b1_ref, o_ref, h_s, mp_s):
    j = pl.program_id(1)
    tq = x_ref.shape[0]
    gw = x_ref.shape[1] // len(POOL_WINDOWS)

    @pl.when(j == 0)
    def _():
        h_s[0:POOL_HALO, :] = jnp.zeros((POOL_HALO, h_s.shape[1]), F32)

    x = x_ref[...]
    h_s[POOL_HALO:, :] = _dot(x.astype(BF16), win_ref[...])
    t_pos = j * tq + lax.broadcasted_iota(I32, (tq, 1), 0)
    for g, win in enumerate(POOL_WINDOWS):
        lo, hi = g * gw, (g + 1) * gw
        cur = h_s[POOL_HALO:, lo:hi]
        acc = cur
        for k in range(1, win):
            acc = acc + h_s[POOL_HALO - k:POOL_HALO - k + tq, lo:hi]
        count = jnp.minimum(t_pos + 1, win).astype(F32)
        pooled = acc / count - cur
        mapped = _dot(pooled.astype(BF16), pw_ref[g])
        mp_s[:, lo:hi] = (mapped * ps_ref[:, lo:hi]).astype(BF16)
    h_s[0:POOL_HALO, :] = h_s[tq:tq + POOL_HALO, :]
    z = ALPHA * x + _dot(mp_s[...], wout_ref[...])
    o_ref[...] = _layer_norm(z, g1_ref[...], b1_ref[...])


def _mixer1(x, w_in, pool_w, pool_scale, w_out, g1, b1, bsz, s_len):
    t_all, d = x.shape
    tq = TOKEN_TILE
    nj = s_len // tq
    row = lambda bb, jj: (bb * nj + jj, 0)
    ng = len(POOL_WINDOWS)
    return pl.pallas_call(
        _mixer1_kernel,
        grid=(bsz, nj),
        in_specs=[
            pl.BlockSpec((tq, d), row),
            _const_spec((d, d)),
            _const_spec((ng, d // ng, d // ng)),
            _const_spec((1, d)),
            _const_spec((d, d)),
            _const_spec((1, d)), _const_spec((1, d)),
        ],
        out_specs=pl.BlockSpec((tq, d), row),
        out_shape=jax.ShapeDtypeStruct((t_all, d), F32),
        scratch_shapes=[pltpu.VMEM((tq + POOL_HALO, d), F32), pltpu.VMEM((tq, d), BF16)],
        compiler_params=_tc_params(2),
        name="mixer1",
    )(x, w_in, pool_w, pool_scale, w_out, g1, b1)


def _kv_kernel(mem_ref, w_ref, o_ref):
    o_ref[...] = _dot(mem_ref[...].astype(BF16), w_ref[...]).astype(BF16)


def _kv_proj(mem2d, wkv, bsz):
    rows, d = mem2d.shape
    m = rows // bsz
    return pl.pallas_call(
        _kv_kernel,
        grid=(bsz,),
        in_specs=[pl.BlockSpec((m, d), lambda i: (i, 0)), _const_spec(wkv.shape)],
        out_specs=pl.BlockSpec((m, wkv.shape[1]), lambda i: (i, 0)),
        out_shape=jax.ShapeDtypeStruct((rows, wkv.shape[1]), BF16),
        compiler_params=_tc_params(1),
        name="kv_proj",
    )(mem2d, wkv)


def _top2_of4(v):
    hi01, lo01 = jnp.maximum(v[0], v[1]), jnp.minimum(v[0], v[1])
    hi23, lo23 = jnp.maximum(v[2], v[3]), jnp.minimum(v[2], v[3])
    return jnp.maximum(hi01, hi23) + jnp.maximum(jnp.minimum(hi01, hi23), jnp.maximum(lo01, lo23))


def _argmax_first(vals):
    best, idx = vals[0], jnp.zeros(vals[0].shape, I32)
    for i in range(1, len(vals)):
        better = vals[i] > best
        best = jnp.where(better, vals[i], best)
        idx = jnp.where(better, i, idx)
    return best, idx


def _xattn_kernel(x_ref, kv_ref, wq_ref, wo_ref, g2_ref, b2_ref, rwh_ref, rwl_ref, rb_ref,
                  x2_ref, xp_ref, rt_ref, cnt_ref, run_s):
    first = jnp.logical_and(pl.program_id(0) == 0, pl.program_id(1) == 0)
    tq, d = x_ref.shape
    hd = d // X_HEADS

    @pl.when(first)
    def _():
        run_s[...] = jnp.zeros(run_s.shape, F32)

    x = x_ref[...]
    q = _dot(x.astype(BF16), wq_ref[...]) * (hd ** -0.5)
    outs = []
    for hx in range(X_HEADS):
        qh = q[:, hx * hd:(hx + 1) * hd].astype(BF16)
        kh = kv_ref[:, hx * hd:(hx + 1) * hd]
        vh = kv_ref[:, d + hx * hd:d + (hx + 1) * hd]
        s = _dot_nt(qh, kh)
        p = jnp.exp(s - jnp.max(s, axis=-1, keepdims=True))
        o = _dot(p.astype(BF16), vh) / jnp.sum(p, axis=-1, keepdims=True)
        outs.append(o.astype(BF16))
    att = _dot(jnp.concatenate(outs, axis=1), wo_ref[...])
    x2 = _layer_norm(ALPHA * x + att, g2_ref[...], b2_ref[...])
    x2_ref[...] = x2

    xp_ref[...] = _pack_bf16_pairs(x2)

    xh, xl = _split_bf16(x2)
    logits = _dot_nt(rwh_ref[...], xh) + _dot_nt(rwh_ref[...], xl) + _dot_nt(rwl_ref[...], xh)
    e_max = jnp.max(logits, axis=0, keepdims=True)
    ex = jnp.exp(logits - e_max)
    scores = ex / jnp.sum(ex, axis=0, keepdims=True)
    biased = scores + rb_ref[...]
    sc = [scores[e:e + 1, :] for e in range(N_EXPERTS)]
    bi = [biased[e:e + 1, :] for e in range(N_EXPERTS)]
    epg = EXPERTS_PER_GROUP
    gscore = [_top2_of4(bi[g * epg:(g + 1) * epg]) for g in range(N_EXPERT_GROUPS)]
    _, gsel = _argmax_first(gscore)

    def pick(vals):
        return [functools.reduce(lambda acc, g: jnp.where(gsel == g, vals[g * epg + i], acc),
                                 range(1, N_EXPERT_GROUPS), vals[i]) for i in range(epg)]

    in_b = pick(bi)
    in_s = pick(sc)
    _, i0 = _argmax_first(in_b)
    _, i1 = _argmax_first([jnp.where(i0 == i, -jnp.inf, in_b[i]) for i in range(epg)])

    def take(vals, idx):
        return functools.reduce(lambda acc, i: jnp.where(idx == i, vals[i], acc), range(1, epg), vals[0])

    s0, s1 = take(in_s, i0), take(in_s, i1)
    w0, w1 = s0 / (s0 + s1), s1 / (s0 + s1)
    e0, e1 = gsel * epg + i0, gsel * epg + i1

    eid = lax.broadcasted_iota(I32, (N_EXPERTS, tq), 0)
    oh0 = eid == e0
    oh1 = eid == e1
    onehot = jnp.where(jnp.logical_or(oh0, oh1), 1.0, 0.0)
    rr = lax.broadcasted_iota(I32, (tq, tq), 0)
    cc = lax.broadcasted_iota(I32, (tq, tq), 1)
    upper = jnp.where(rr < cc, 1.0, 0.0).astype(BF16)
    prefix = _dot(onehot.astype(BF16), upper) + run_s[...]
    r0 = jnp.sum(jnp.where(oh0, prefix, 0.0), axis=0, keepdims=True).astype(I32)
    r1 = jnp.sum(jnp.where(oh1, prefix, 0.0), axis=0, keepdims=True).astype(I32)
    run_s[...] = run_s[...] + jnp.sum(onehot, axis=1, keepdims=True)
    cnt_ref[...] = jnp.broadcast_to(run_s[...], cnt_ref.shape).astype(I32)

    zero = jnp.zeros((1, tq), I32)
    rt_ref[...] = jnp.concatenate(
        [e0, e1, r0, r1, pltpu.bitcast(w0, I32), pltpu.bitcast(w1, I32), zero, zero], axis=0)


def _xattn_route(x1, kv, wq, wo, g2, b2, rw_hi, rw_lo, rbias, bsz, s_len):
    t_all, d = x1.shape
    tq = TOKEN_TILE
    nj = s_len // tq
    m = kv.shape[0] // bsz
    row = lambda bb, jj: (bb * nj + jj, 0)
    return pl.pallas_call(
        _xattn_kernel,
        grid=(bsz, nj),
        in_specs=[
            pl.BlockSpec((tq, d), row),
            pl.BlockSpec((m, 2 * d), lambda bb, jj: (bb, 0)),
            _const_spec((d, d)), _const_spec((d, d)),
            _const_spec((1, d)), _const_spec((1, d)),
            _const_spec((N_EXPERTS, d)), _const_spec((N_EXPERTS, d)),
            _const_spec((N_EXPERTS, 1)),
        ],
        out_specs=[
            pl.BlockSpec((tq, d), row),
            pl.BlockSpec((tq, d // 2), row),
            pl.BlockSpec((8, tq), lambda bb, jj: (0, bb * nj + jj)),
            _const_spec((N_EXPERTS, 128)),
        ],
        out_shape=[
            jax.ShapeDtypeStruct((t_all, d), F32),
            jax.ShapeDtypeStruct((t_all, d // 2), I32),
            jax.ShapeDtypeStruct((8, t_all), I32),
            jax.ShapeDtypeStruct((N_EXPERTS, 128), I32),
        ],
        scratch_shapes=[pltpu.VMEM((N_EXPERTS, 1), F32)],
        compiler_params=_tc_params(2),
        name="xattn_route",
    )(x1, kv, wq, wo, g2, b2, rw_hi, rw_lo, rbias)


def _sc_mesh():
    return plsc.VectorSubcoreMesh(core_axis_name="c", subcore_axis_name="s")


def _sc_params():
    return pltpu.CompilerParams(needs_layout_passes=False)


def _worker_id():
    return lax.axis_index("s") * lax.axis_size("c") + lax.axis_index("c")


def _sc_dispatch(xp, dest2d, n_rows):
    t_all, width = xp.shape
    chunk = dest2d.shape[1]
    tok_w = t_all // SC_WORKERS
    nch = tok_w // chunk
    slot1 = t_all // chunk
    assert t_all % (SC_WORKERS * chunk * 2) == 0

    def body(x_hbm, dest_hbm, out_hbm, idx0_v, idx1_v, buf0, buf1, sem_r, sem_w):
        wid = _worker_id()
        base = wid * tok_w
        pltpu.sync_copy(dest_hbm.at[pl.ds(wid * nch, nch)], idx0_v)
        pltpu.sync_copy(dest_hbm.at[pl.ds(slot1 + wid * nch, nch)], idx1_v)

        def read(c, buf, k):
            return pltpu.make_async_copy(x_hbm.at[pl.ds(base + c * chunk, chunk)], buf, sem_r.at[k])

        def scatter(c, buf):
            a = pltpu.make_async_copy(buf, out_hbm.at[idx0_v.at[c]], sem_w.at[0])
            b = pltpu.make_async_copy(buf, out_hbm.at[idx1_v.at[c]], sem_w.at[1])
            a.start()
            b.start()
            a.wait()
            b.wait()

        read(0, buf0, 0).start()

        @pl.loop(0, nch // 2)
        def _(g):
            c = 2 * g
            read(c + 1, buf1, 1).start()
            read(c, buf0, 0).wait()
            scatter(c, buf0)

            @pl.when(c + 2 < nch)
            def _():
                read(c + 2, buf0, 0).start()

            read(c + 1, buf1, 1).wait()
            scatter(c + 1, buf1)

    return pl.kernel(
        body,
        out_type=jax.ShapeDtypeStruct((n_rows, width), xp.dtype),
        mesh=_sc_mesh(),
        scratch_types=[
            pltpu.VMEM((nch, chunk), I32),
            pltpu.VMEM((nch, chunk), I32),
            pltpu.VMEM((chunk, width), xp.dtype),
            pltpu.VMEM((chunk, width), xp.dtype),
            pltpu.SemaphoreType.DMA((2,)),
            pltpu.SemaphoreType.DMA((2,)),
        ],
        compiler_params=_sc_params(),
        name="sc_dispatch",
    )(xp, dest2d)


def _sc_combine(ys, dest2d):
    n_idx_rows, chunk = dest2d.shape
    width = ys.shape[1]
    nch = n_idx_rows // SC_WORKERS
    assert n_idx_rows % (SC_WORKERS * 2) == 0

    def body(y_hbm, dest_hbm, out_hbm, idx_v, buf0, buf1, sem_g):
        wid = _worker_id()
        base = wid * nch * chunk
        pltpu.sync_copy(dest_hbm.at[pl.ds(wid * nch, nch)], idx_v)

        def gather(c, buf, k):
            return pltpu.make_async_copy(y_hbm.at[idx_v.at[c]], buf, sem_g.at[k])

        def write(c, buf):
            pltpu.sync_copy(buf, out_hbm.at[pl.ds(base + c * chunk, chunk)])

        gather(0, buf0, 0).start()

        @pl.loop(0, nch // 2)
        def _(g):
            c = 2 * g
            gather(c + 1, buf1, 1).start()
            gather(c, buf0, 0).wait()
            write(c, buf0)

            @pl.when(c + 2 < nch)
            def _():
                gather(c + 2, buf0, 0).start()

            gather(c + 1, buf1, 1).wait()
            write(c + 1, buf1)

    return pl.kernel(
        body,
        out_type=jax.ShapeDtypeStruct((n_idx_rows * chunk, width), ys.dtype),
        mesh=_sc_mesh(),
        scratch_types=[
            pltpu.VMEM((nch, chunk), I32),
            pltpu.VMEM((chunk, width), ys.dtype),
            pltpu.VMEM((chunk, width), ys.dtype),
            pltpu.SemaphoreType.DMA((2,)),
        ],
        compiler_params=_sc_params(),
        name="sc_combine",
    )(ys, dest2d)


def _pack_bf16_pairs(v):
    half = v.shape[1] // 2
    lo = pltpu.bitcast(v[:, :half].astype(BF16).astype(F32), jnp.uint32) >> 16
    hi = pltpu.bitcast(v[:, half:].astype(BF16).astype(F32), jnp.uint32) & jnp.uint32(0xFFFF0000)
    return pltpu.bitcast(hi | lo, I32)


def _unpack_bf16_pairs(w):
    w = pltpu.bitcast(w, jnp.uint32)
    lo = pltpu.bitcast(w << 16, F32)
    hi = pltpu.bitcast(w & jnp.uint32(0xFFFF0000), F32)
    return jnp.concatenate([lo, hi], axis=1)


def _ffn_kernel(be_ref, nv_ref, nu_ref, xs_ref, wg_ref, wu_ref, wd_ref, o_ref):
    i = pl.program_id(0)

    @pl.when(i < nu_ref[0])
    def _():
        live = lax.broadcasted_iota(I32, xs_ref.shape, 0) < nv_ref[i]
        xb = _unpack_bf16_pairs(jnp.where(live, xs_ref[...], 0)).astype(BF16)
        act = jax.nn.silu(_dot(xb, wg_ref[0])) * _dot(xb, wu_ref[0])
        o_ref[...] = _pack_bf16_pairs(_dot(act.astype(BF16), wd_ref[0]))


def _expert_ffn(xs, blk_expert, blk_valid, n_used, w_gate, w_up, w_down):
    n_rows, half = xs.shape
    d = 2 * half
    de = w_gate.shape[2]
    bm = EXPERT_ROWS
    rows = lambda i, be, nv, nu: (jnp.minimum(i, nu[0] - 1), 0)
    wsel = lambda i, be, nv, nu: (be[i], 0, 0)
    return pl.pallas_call(
        _ffn_kernel,
        grid_spec=pltpu.PrefetchScalarGridSpec(
            num_scalar_prefetch=3,
            grid=(n_rows // bm,),
            in_specs=[
                pl.BlockSpec((bm, half), rows),
                pl.BlockSpec((1, d, de), wsel),
                pl.BlockSpec((1, d, de), wsel),
                pl.BlockSpec((1, de, d), wsel),
            ],
            out_specs=pl.BlockSpec((bm, half), rows),
        ),
        out_shape=jax.ShapeDtypeStruct((n_rows, half), I32),
        compiler_params=_tc_params(1),
        name="expert_ffn",
    )(blk_expert, blk_valid, n_used, xs, w_gate, w_up, w_down)


def _combine_kernel(x_ref, y0_ref, y1_ref, rt_ref, g_ref, b_ref, o_ref):
    wt = pltpu.bitcast(rt_ref[...], F32).T
    y = wt[:, 4:5] * _unpack_bf16_pairs(y0_ref[...]) + wt[:, 5:6] * _unpack_bf16_pairs(y1_ref[...])
    o_ref[...] = _layer_norm(ALPHA * x_ref[...] + y, g_ref[...], b_ref[...])


def _combine_ln(x2, y01, rt, g3, b3):
    t_all, d = x2.shape
    tq = TOKEN_TILE
    nt = t_all // tq
    return pl.pallas_call(
        _combine_kernel,
        grid=(nt,),
        in_specs=[
            pl.BlockSpec((tq, d), lambda i: (i, 0)),
            pl.BlockSpec((tq, d // 2), lambda i: (i, 0)),
            pl.BlockSpec((tq, d // 2), lambda i: (i + nt, 0)),
            pl.BlockSpec((8, tq), lambda i: (0, i)),
            _const_spec((1, d)), _const_spec((1, d)),
        ],
        out_specs=pl.BlockSpec((tq, d), lambda i: (i, 0)),
        out_shape=jax.ShapeDtypeStruct((t_all, d), F32),
        compiler_params=_tc_params(1),
        name="combine_ln",
    )(x2, y01, y01, rt, g3, b3)


def _routing_tables(rt, counts):
    bm = EXPERT_ROWS
    t_all = rt.shape[1]
    n_rows = 2 * t_all + N_EXPERTS * bm
    cnt = counts[:, 0]
    padded = (cnt + bm - 1) // bm * bm
    ends = jnp.cumsum(padded)
    offs = ends - padded
    experts = rt[0:2]
    off_tok = jnp.sum(jnp.where(experts[None] == jnp.arange(N_EXPERTS, dtype=I32)[:, None, None],
                                offs[:, None, None], 0), axis=0)
    dest2d = (off_tok + rt[2:4]).reshape(-1, SC_CHUNK).astype(I32)
    blk_start = jnp.arange(n_rows // bm, dtype=I32) * bm
    blk_expert = jnp.minimum(jnp.sum(blk_start[:, None] >= ends[None, :], axis=1), N_EXPERTS - 1).astype(I32)
    live_end = jnp.sum(jnp.where(blk_expert[:, None] == jnp.arange(N_EXPERTS, dtype=I32)[None, :],
                                 (offs + cnt)[None, :], 0), axis=1)
    blk_valid = jnp.clip(live_end - blk_start, 0, bm).astype(I32)
    n_used = (ends[-1:] // bm).astype(I32)
    return dest2d, blk_expert, blk_valid, n_used, n_rows


def _layer_tail(x1, mem2d, p, router, bsz, s_len):
    rw_hi, rw_lo, rbias = router
    kv = _kv_proj(mem2d, p["xkv"], bsz)
    x2, xp, rt, counts = _xattn_route(x1, kv, p["xq"], p["xo"], p["ln2_g"], p["ln2_b"],
                                      rw_hi, rw_lo, rbias, bsz, s_len)
    dest2d, blk_expert, blk_valid, n_used, n_rows = _routing_tables(rt, counts)
    xs = _sc_dispatch(xp, dest2d, n_rows)
    ys = _expert_ffn(xs, blk_expert, blk_valid, n_used, p["e_gate"], p["e_up"], p["e_down"])
    y01 = _sc_combine(ys, dest2d)
    return _combine_ln(x2, y01, rt, p["ln3_g"], p["ln3_b"])


def _row(v):
    return v.reshape(1, -1).astype(F32)


def _common_params(xq, xkv, xo, ln2_g, ln2_b, e_gate, e_up, e_down, ln3_g, ln3_b):
    return dict(xq=xq.astype(BF16), xkv=xkv.astype(BF16), xo=xo.astype(BF16),
                ln2_g=_row(ln2_g), ln2_b=_row(ln2_b),
                e_gate=e_gate.astype(BF16), e_up=e_up.astype(BF16), e_down=e_down.astype(BF16),
                ln3_g=_row(ln3_g), ln3_b=_row(ln3_b))


def kernel(x, mem, positions, router_w, router_bias, l0_w_in, l0_sinks, l0_sgu_ln_g, l0_sgu_ln_b, l0_sgu_w, l0_sgu_b, l0_w_out, l0_ln1_g, l0_ln1_b, l0_xq, l0_xkv, l0_xo, l0_ln2_g, l0_ln2_b, l0_e_gate, l0_e_up, l0_e_down, l0_ln3_g, l0_ln3_b, l1_w_in, l1_pool_w, l1_pool_scale, l1_w_out, l1_ln1_g, l1_ln1_b, l1_xq, l1_xkv, l1_xo, l1_ln2_g, l1_ln2_b, l1_e_gate, l1_e_up, l1_e_down, l1_ln3_g, l1_ln3_b):
    bsz, s_len, d = x.shape
    assert s_len % TOKEN_TILE == 0 and TOKEN_TILE % BLOCK == 0
    xt = x.reshape(bsz * s_len, d)
    mem2d = mem.reshape(-1, d)
    pos_col = positions.reshape(-1, 1).astype(I32)

    rw_t = router_w.T.astype(F32)
    rw_hi = rw_t.astype(BF16)
    rw_lo = (rw_t - rw_hi.astype(F32)).astype(BF16)
    router = (rw_hi, rw_lo, router_bias.reshape(-1, 1).astype(F32))

    half = ROPE_DIM // 2
    inv_freq = ROPE_THETA ** (-(jnp.arange(half, dtype=F32) * 2.0 / ROPE_DIM))
    lane = jnp.arange(128) % HEAD_DIM
    invf = jnp.where(lane < ROPE_DIM, inv_freq[lane % half], 0.0).reshape(1, 128).astype(F32)
    m1 = jnp.where((lane >= half) & (lane < ROPE_DIM), 1.0, 0.0).reshape(1, 128).astype(F32)
    m2 = jnp.where(lane < half, -1.0, 0.0).reshape(1, 128).astype(F32)
    grp = jnp.arange(B_WIDTH) // B_GROUP_DIM
    gsum = (grp[:, None] == grp[None, :]).astype(BF16)
    bs_full = jnp.repeat(l0_sgu_b.T.astype(F32), B_GROUP_DIM, axis=1)

    x1 = _mixer0(xt, pos_col, l0_sinks.astype(F32), l0_w_in.astype(BF16), invf, m1, m2, gsum,
                 _row(l0_sgu_ln_g), _row(l0_sgu_ln_b), l0_sgu_w.astype(F32), bs_full,
                 l0_w_out.astype(BF16), _row(l0_ln1_g), _row(l0_ln1_b), bsz, s_len)
    p0 = _common_params(l0_xq, l0_xkv, l0_xo, l0_ln2_g, l0_ln2_b, l0_e_gate, l0_e_up, l0_e_down,
                        l0_ln3_g, l0_ln3_b)
    x3 = _layer_tail(x1, mem2d, p0, router, bsz, s_len)

    x1 = _mixer1(x3, l1_w_in.astype(BF16), l1_pool_w.astype(BF16), _row(l1_pool_scale),
                 l1_w_out.astype(BF16), _row(l1_ln1_g), _row(l1_ln1_b), bsz, s_len)
    p1 = _common_params(l1_xq, l1_xkv, l1_xo, l1_ln2_g, l1_ln2_b, l1_e_gate, l1_e_up, l1_e_down,
                        l1_ln3_g, l1_ln3_b)
    x3 = _layer_tail(x1, mem2d, p1, router, bsz, s_len)
    return x3.reshape(bsz, s_len, d)
```

```python
import functools
import math

import jax
import jax.numpy as jnp
from jax import lax
from jax.experimental import pallas as pl
from jax.experimental.pallas import tpu as pltpu
from jax.experimental.pallas import tpu_sc as plsc

F32 = jnp.float32
BF16 = jnp.bfloat16
I32 = jnp.int32

DEPTH = 2
ALPHA = (2.0 * DEPTH) ** 0.25
LN_EPS = 1e-5

HEAD_DIM = 64
A_Q_HEADS = 8
A_KV_HEADS = 2
A_GROUP = A_Q_HEADS // A_KV_HEADS
BLOCK = 128
ROPE_THETA = 500000.0
ROPE_DIM = HEAD_DIM // 4
A_WIDTH = A_Q_HEADS * HEAD_DIM
KV_WIDTH = A_KV_HEADS * HEAD_DIM
B_GROUPS = 8
B_GROUP_DIM = 64
B_WIDTH = B_GROUPS * B_GROUP_DIM
POOL_WINDOWS = (2, 4, 8, 16)
POOL_HALO = 16
X_HEADS = 4
N_EXPERTS = 16
N_EXPERT_GROUPS = 4
EXPERTS_PER_GROUP = 4

TOKEN_TILE = 512
EXPERT_ROWS = 512
SC_WORKERS = 32
SC_CHUNK = 64
VMEM_LIMIT = 56 * 1024 * 1024
NEG_BIG = -1e30


def _layer_norm(z, g, b):
    mu = jnp.mean(z, axis=-1, keepdims=True)
    d = z - mu
    var = jnp.mean(d * d, axis=-1, keepdims=True)
    return d * lax.rsqrt(var + LN_EPS) * g + b


def _dot(a, b):
    return jnp.dot(a, b, preferred_element_type=F32)


def _dot_nt(a, b):
    return lax.dot_general(a, b, (((1,), (1,)), ((), ())), preferred_element_type=F32)


def _split_bf16(v):
    hi = v.astype(BF16)
    lo = (v - hi.astype(F32)).astype(BF16)
    return hi, lo


def _tc_params(n_axes):
    return pltpu.CompilerParams(dimension_semantics=("arbitrary",) * n_axes,
                                vmem_limit_bytes=VMEM_LIMIT)


def _const_spec(shape):
    nd = len(shape)
    return pl.BlockSpec(shape, lambda *_: (0,) * nd, pipeline_mode=pl.Buffered(1))

def _mixer0_kernel(sinks_ref, x_ref, pos_ref, win_ref, invf_ref, m1_ref, m2_ref, gsum_ref,
                   lng_ref, lnb_ref, ws_ref, bs_ref, wout_ref, g1_ref, b1_ref,
                   o_ref, q_s, kv_s, u_s, vn_s, mix_s, wt_s, win_s, wout_s):
    b = pl.program_id(0)
    j = pl.program_id(1)
    tq = x_ref.shape[0]
    nblk = tq // BLOCK

    @pl.when(jnp.logical_and(b == 0, j == 0))
    def _():
        win_s[...] = win_ref[...].astype(BF16)
        wout_s[...] = wout_ref[...].astype(BF16)
        r = lax.broadcasted_iota(I32, (BLOCK, BLOCK), 0)
        c = lax.broadcasted_iota(I32, (BLOCK, BLOCK), 1)
        for g in range(B_GROUPS):
            wt_s[g] = jnp.where(c <= r, ws_ref[g], 0.0).astype(BF16)

    @pl.when(j == 0)
    def _():
        kv_s[0:BLOCK, :] = jnp.zeros((BLOCK, 2 * KV_WIDTH), BF16)

    x = x_ref[...]
    h = _dot(x.astype(BF16), win_s[...])

    ang = pos_ref[...].astype(F32) * invf_ref[...]
    cs = jnp.cos(ang)
    sn = jnp.sin(ang)
    sa = sn * m1_ref[...]
    sb = sn * m2_ref[...]

    def rope(t):
        return t * cs + pltpu.roll(t, ROPE_DIM // 2, 1) * sa + pltpu.roll(t, 128 - ROPE_DIM // 2, 1) * sb

    for c in range(A_WIDTH // 128):
        t = h[:, c * 128:(c + 1) * 128] * (HEAD_DIM ** -0.5)
        q_s[:, c * 128:(c + 1) * 128] = rope(t).astype(BF16)
    c1 = A_WIDTH
    c2 = c1 + KV_WIDTH
    c3 = c2 + KV_WIDTH
    c4 = c3 + B_WIDTH
    kv_s[BLOCK:, 0:KV_WIDTH] = rope(h[:, c1:c2]).astype(BF16)
    kv_s[BLOCK:, KV_WIDTH:] = h[:, c2:c3].astype(BF16)

    u_s[...] = jax.nn.gelu(h[:, c3:c4])
    v = jax.nn.gelu(h[:, c4:])
    gsum = gsum_ref[...]
    vh, vl = _split_bf16(v)
    mean = (_dot(vh, gsum) + _dot(vl, gsum)) * (1.0 / B_GROUP_DIM)
    d = v - mean
    dh, dl = _split_bf16(d * d)
    var = (_dot(dh, gsum) + _dot(dl, gsum)) * (1.0 / B_GROUP_DIM)
    vn_s[...] = (d * lax.rsqrt(var + LN_EPS) * lng_ref[...] + lnb_ref[...]).astype(BF16)

    qi = lax.broadcasted_iota(I32, (BLOCK, 2 * BLOCK), 0)
    kj = lax.broadcasted_iota(I32, (BLOCK, 2 * BLOCK), 1)
    rel = qi + BLOCK - kj
    band = jnp.logical_and(rel >= 0, rel < BLOCK)

    def block_body(n, carry):
        r0 = pl.multiple_of(n * BLOCK, BLOCK)
        kv = kv_s[pl.ds(r0, 2 * BLOCK), :]
        qb = q_s[pl.ds(r0, BLOCK), :]
        first = jnp.logical_and(j == 0, n == 0)
        valid = jnp.logical_and(band, kj >= jnp.where(first, BLOCK, 0))
        for hq in range(A_Q_HEADS):
            hk = hq // A_GROUP
            qh = qb[:, hq * HEAD_DIM:(hq + 1) * HEAD_DIM]
            kh = kv[:, hk * HEAD_DIM:(hk + 1) * HEAD_DIM]
            vv = kv[:, KV_WIDTH + hk * HEAD_DIM:KV_WIDTH + (hk + 1) * HEAD_DIM]
            s = jnp.where(valid, _dot_nt(qh, kh), NEG_BIG)
            sink = sinks_ref[hq]
            m = jnp.maximum(jnp.max(s, axis=-1, keepdims=True), sink)
            p = jnp.exp(s - m)
            den = jnp.sum(p, axis=-1, keepdims=True) + jnp.exp(sink - m)
            o = _dot(p.astype(BF16), vv) / den
            mix_s[pl.ds(r0, BLOCK), hq * HEAD_DIM:(hq + 1) * HEAD_DIM] = o.astype(BF16)
        vnb = vn_s[pl.ds(r0, BLOCK), :]
        parts = [_dot(wt_s[g], vnb[:, g * B_GROUP_DIM:(g + 1) * B_GROUP_DIM]) for g in range(B_GROUPS)]
        mixed = jnp.concatenate(parts, axis=1) + bs_ref[...]
        mix_s[pl.ds(r0, BLOCK), A_WIDTH:] = (u_s[pl.ds(r0, BLOCK), :] * mixed).astype(BF16)
        return carry

    lax.fori_loop(0, nblk, block_body, 0)
    kv_s[0:BLOCK, :] = kv_s[tq:tq + BLOCK, :]

    z = ALPHA * x + _dot(mix_s[...], wout_s[...])
    o_ref[...] = _layer_norm(z, g1_ref[...], b1_ref[...])


def _mixer0(x, pos_col, sinks, w_in, invf, m1, m2, gsum, lng, lnb, w_s, bs_full, w_out, g1, b1, bsz, s_len):
    t_all, d = x.shape
    tq = TOKEN_TILE
    nj = s_len // tq
    row = lambda bb, jj: (bb * nj + jj, 0)
    in_w = w_in.shape[1]
    return pl.pallas_call(
        _mixer0_kernel,
        grid=(bsz, nj),
        in_specs=[
            pl.BlockSpec(memory_space=pltpu.SMEM),
            pl.BlockSpec((tq, d), row),
            pl.BlockSpec((tq, 1), row),
            _const_spec((d, in_w)),
            _const_spec((1, 128)), _const_spec((1, 128)), _const_spec((1, 128)),
            _const_spec((B_WIDTH, B_WIDTH)),
            _const_spec((1, B_WIDTH)), _const_spec((1, B_WIDTH)),
            _const_spec((B_GROUPS, BLOCK, BLOCK)),
            _const_spec((BLOCK, B_WIDTH)),
            _const_spec((A_WIDTH + B_WIDTH, d)),
            _const_spec((1, d)), _const_spec((1, d)),
        ],
        out_specs=pl.BlockSpec((tq, d), row),
        out_shape=jax.ShapeDtypeStruct((t_all, d), F32),
        scratch_shapes=[
            pltpu.VMEM((tq, A_WIDTH), BF16),
            pltpu.VMEM((tq + BLOCK, 2 * KV_WIDTH), BF16),
            pltpu.VMEM((tq, B_WIDTH), F32),
            pltpu.VMEM((tq, B_WIDTH), BF16),
            pltpu.VMEM((tq, A_WIDTH + B_WIDTH), BF16),
            pltpu.VMEM((B_GROUPS, BLOCK, BLOCK), BF16),
            pltpu.VMEM((d, in_w), BF16),
            pltpu.VMEM((A_WIDTH + B_WIDTH, d), BF16),
        ],
        compiler_params=_tc_params(2),
        name="mixer0",
    )(sinks, x, pos_col, w_in, invf, m1, m2, gsum, lng, lnb, w_s, bs_full, w_out, g1, b1)


def _mixer1_kernel(x_ref, win_ref, pw_ref, ps_ref, wout_ref, g1_ref, b1_ref, o_ref,
                   h_s, mp_s, win_s, pw_s, wout_s):
    j = pl.program_id(1)
    tq = x_ref.shape[0]
    gw = x_ref.shape[1] // len(POOL_WINDOWS)

    @pl.when(jnp.logical_and(pl.program_id(0) == 0, j == 0))
    def _():
        win_s[...] = win_ref[...].astype(BF16)
        pw_s[...] = pw_ref[...].astype(BF16)
        wout_s[...] = wout_ref[...].astype(BF16)

    @pl.when(j == 0)
    def _():
        h_s[0:POOL_HALO, :] = jnp.zeros((POOL_HALO, h_s.shape[1]), F32)

    x = x_ref[...]
    h_s[POOL_HALO:, :] = _dot(x.astype(BF16), win_s[...])
    t_pos = j * tq + lax.broadcasted_iota(I32, (tq, 1), 0)
    for g, win in enumerate(POOL_WINDOWS):
        lo, hi = g * gw, (g + 1) * gw
        cur = h_s[POOL_HALO:, lo:hi]
        acc = cur
        for k in range(1, win):
            acc = acc + h_s[POOL_HALO - k:POOL_HALO - k + tq, lo:hi]
        count = jnp.minimum(t_pos + 1, win).astype(F32)
        pooled = acc / count - cur
        mapped = _dot(pooled.astype(BF16), pw_s[g])
        mp_s[:, lo:hi] = (mapped * ps_ref[:, lo:hi]).astype(BF16)
    h_s[0:POOL_HALO, :] = h_s[tq:tq + POOL_HALO, :]
    z = ALPHA * x + _dot(mp_s[...], wout_s[...])
    o_ref[...] = _layer_norm(z, g1_ref[...], b1_ref[...])


def _mixer1(x, w_in, pool_w, pool_scale, w_out, g1, b1, bsz, s_len):
    t_all, d = x.shape
    tq = TOKEN_TILE
    nj = s_len // tq
    row = lambda bb, jj: (bb * nj + jj, 0)
    ng = len(POOL_WINDOWS)
    return pl.pallas_call(
        _mixer1_kernel,
        grid=(bsz, nj),
        in_specs=[
            pl.BlockSpec((tq, d), row),
            _const_spec((d, d)),
            _const_spec((ng, d // ng, d // ng)),
            _const_spec((1, d)),
            _const_spec((d, d)),
            _const_spec((1, d)), _const_spec((1, d)),
        ],
        out_specs=pl.BlockSpec((tq, d), row),
        out_shape=jax.ShapeDtypeStruct((t_all, d), F32),
        scratch_shapes=[pltpu.VMEM((tq + POOL_HALO, d), F32), pltpu.VMEM((tq, d), BF16),
                        pltpu.VMEM((d, d), BF16), pltpu.VMEM((ng, d // ng, d // ng), BF16),
                        pltpu.VMEM((d, d), BF16)],
        compiler_params=_tc_params(2),
        name="mixer1",
    )(x, w_in, pool_w, pool_scale, w_out, g1, b1)


def _kv_kernel(mem_ref, w_ref, o_ref, w_s):
    @pl.when(pl.program_id(0) == 0)
    def _():
        w_s[...] = w_ref[...].astype(BF16)

    o_ref[...] = _dot(mem_ref[...].astype(BF16), w_s[...]).astype(BF16)


def _kv_proj(mem2d, wkv, bsz):
    rows, d = mem2d.shape
    m = rows // bsz
    return pl.pallas_call(
        _kv_kernel,
        grid=(bsz,),
        in_specs=[pl.BlockSpec((m, d), lambda i: (i, 0)), _const_spec(wkv.shape)],
        out_specs=pl.BlockSpec((m, wkv.shape[1]), lambda i: (i, 0)),
        out_shape=jax.ShapeDtypeStruct((rows, wkv.shape[1]), BF16),
        scratch_shapes=[pltpu.VMEM(wkv.shape, BF16)],
        compiler_params=_tc_params(1),
        name="kv_proj",
    )(mem2d, wkv)


def _top2_of4(v):
    hi01, lo01 = jnp.maximum(v[0], v[1]), jnp.minimum(v[0], v[1])
    hi23, lo23 = jnp.maximum(v[2], v[3]), jnp.minimum(v[2], v[3])
    return jnp.maximum(hi01, hi23) + jnp.maximum(jnp.minimum(hi01, hi23), jnp.maximum(lo01, lo23))


def _argmax_first(vals):
    best, idx = vals[0], jnp.zeros(vals[0].shape, I32)
    for i in range(1, len(vals)):
        better = vals[i] > best
        best = jnp.where(better, vals[i], best)
        idx = jnp.where(better, i, idx)
    return best, idx


def _xattn_kernel(x_ref, kv_ref, wq_ref, wo_ref, g2_ref, b2_ref, rwh_ref, rwl_ref, rb_ref,
                  x2_ref, xp_ref, rt_ref, cnt_ref, run_s, wq_s, wo_s):
    first = jnp.logical_and(pl.program_id(0) == 0, pl.program_id(1) == 0)
    tq, d = x_ref.shape
    hd = d // X_HEADS

    @pl.when(first)
    def _():
        run_s[...] = jnp.zeros(run_s.shape, F32)
        wq_s[...] = wq_ref[...].astype(BF16)
        wo_s[...] = wo_ref[...].astype(BF16)

    x = x_ref[...]
    q = _dot(x.astype(BF16), wq_s[...]) * (hd ** -0.5)
    outs = []
    for hx in range(X_HEADS):
        qh = q[:, hx * hd:(hx + 1) * hd].astype(BF16)
        kh = kv_ref[:, hx * hd:(hx + 1) * hd]
        vh = kv_ref[:, d + hx * hd:d + (hx + 1) * hd]
        s = _dot_nt(qh, kh)
        p = jnp.exp(s - jnp.max(s, axis=-1, keepdims=True))
        o = _dot(p.astype(BF16), vh) / jnp.sum(p, axis=-1, keepdims=True)
        outs.append(o.astype(BF16))
    att = _dot(jnp.concatenate(outs, axis=1), wo_s[...])
    x2 = _layer_norm(ALPHA * x + att, g2_ref[...], b2_ref[...])
    x2_ref[...] = x2

    xp_ref[...] = _pack_bf16_pairs(x2)

    xh, xl = _split_bf16(x2)
    logits = _dot_nt(rwh_ref[...], xh) + _dot_nt(rwh_ref[...], xl) + _dot_nt(rwl_ref[...], xh)
    e_max = jnp.max(logits, axis=0, keepdims=True)
    ex = jnp.exp(logits - e_max)
    scores = ex / jnp.sum(ex, axis=0, keepdims=True)
    biased = scores + rb_ref[...]
    sc = [scores[e:e + 1, :] for e in range(N_EXPERTS)]
    bi = [biased[e:e + 1, :] for e in range(N_EXPERTS)]
    epg = EXPERTS_PER_GROUP
    gscore = [_top2_of4(bi[g * epg:(g + 1) * epg]) for g in range(N_EXPERT_GROUPS)]
    _, gsel = _argmax_first(gscore)

    def pick(vals):
        return [functools.reduce(lambda acc, g: jnp.where(gsel == g, vals[g * epg + i], acc),
                                 range(1, N_EXPERT_GROUPS), vals[i]) for i in range(epg)]

    in_b = pick(bi)
    in_s = pick(sc)
    _, i0 = _argmax_first(in_b)
    _, i1 = _argmax_first([jnp.where(i0 == i, -jnp.inf, in_b[i]) for i in range(epg)])

    def take(vals, idx):
        return functools.reduce(lambda acc, i: jnp.where(idx == i, vals[i], acc), range(1, epg), vals[0])

    s0, s1 = take(in_s, i0), take(in_s, i1)
    w0, w1 = s0 / (s0 + s1), s1 / (s0 + s1)
    e0, e1 = gsel * epg + i0, gsel * epg + i1

    eid = lax.broadcasted_iota(I32, (N_EXPERTS, tq), 0)
    oh0 = eid == e0
    oh1 = eid == e1
    onehot = jnp.where(jnp.logical_or(oh0, oh1), 1.0, 0.0)
    rr = lax.broadcasted_iota(I32, (tq, tq), 0)
    cc = lax.broadcasted_iota(I32, (tq, tq), 1)
    upper = jnp.where(rr < cc, 1.0, 0.0).astype(BF16)
    prefix = _dot(onehot.astype(BF16), upper) + run_s[...]
    r0 = jnp.sum(jnp.where(oh0, prefix, 0.0), axis=0, keepdims=True).astype(I32)
    r1 = jnp.sum(jnp.where(oh1, prefix, 0.0), axis=0, keepdims=True).astype(I32)
    run_s[...] = run_s[...] + jnp.sum(onehot, axis=1, keepdims=True)
    cnt_ref[...] = jnp.broadcast_to(run_s[...], cnt_ref.shape).astype(I32)

    zero = jnp.zeros((1, tq), I32)
    rt_ref[...] = jnp.concatenate(
        [e0, e1, r0, r1, pltpu.bitcast(w0, I32), pltpu.bitcast(w1, I32), zero, zero], axis=0)


def _xattn_route(x1, kv, wq, wo, g2, b2, rw_hi, rw_lo, rbias, bsz, s_len):
    t_all, d = x1.shape
    tq = TOKEN_TILE
    nj = s_len // tq
    m = kv.shape[0] // bsz
    row = lambda bb, jj: (bb * nj + jj, 0)
    return pl.pallas_call(
        _xattn_kernel,
        grid=(bsz, nj),
        in_specs=[
            pl.BlockSpec((tq, d), row),
            pl.BlockSpec((m, 2 * d), lambda bb, jj: (bb, 0)),
            _const_spec((d, d)), _const_spec((d, d)),
            _const_spec((1, d)), _const_spec((1, d)),
            _const_spec((N_EXPERTS, d)), _const_spec((N_EXPERTS, d)),
            _const_spec((N_EXPERTS, 1)),
        ],
        out_specs=[
            pl.BlockSpec((tq, d), row),
            pl.BlockSpec((tq, d // 2), row),
            pl.BlockSpec((8, tq), lambda bb, jj: (0, bb * nj + jj)),
            _const_spec((N_EXPERTS, 128)),
        ],
        out_shape=[
            jax.ShapeDtypeStruct((t_all, d), F32),
            jax.ShapeDtypeStruct((t_all, d // 2), I32),
            jax.ShapeDtypeStruct((8, t_all), I32),
            jax.ShapeDtypeStruct((N_EXPERTS, 128), I32),
        ],
        scratch_shapes=[pltpu.VMEM((N_EXPERTS, 1), F32), pltpu.VMEM((d, d), BF16), pltpu.VMEM((d, d), BF16)],
        compiler_params=_tc_params(2),
        name="xattn_route",
    )(x1, kv, wq, wo, g2, b2, rw_hi, rw_lo, rbias)


def _sc_mesh():
    return plsc.VectorSubcoreMesh(core_axis_name="c", subcore_axis_name="s")


def _sc_params():
    return pltpu.CompilerParams(needs_layout_passes=False)


def _worker_id():
    return lax.axis_index("s") * lax.axis_size("c") + lax.axis_index("c")


def _sc_dispatch(xp, dest2d, n_rows):
    t_all, width = xp.shape
    chunk = dest2d.shape[1]
    tok_w = t_all // SC_WORKERS
    nch = tok_w // chunk
    slot1 = t_all // chunk
    assert t_all % (SC_WORKERS * chunk * 2) == 0

    def body(x_hbm, dest_hbm, out_hbm, idx0_v, idx1_v, buf0, buf1, sem_r, sem_w):
        wid = _worker_id()
        base = wid * tok_w
        pltpu.sync_copy(dest_hbm.at[pl.ds(wid * nch, nch)], idx0_v)
        pltpu.sync_copy(dest_hbm.at[pl.ds(slot1 + wid * nch, nch)], idx1_v)

        def read(c, buf, k):
            return pltpu.make_async_copy(x_hbm.at[pl.ds(base + c * chunk, chunk)], buf, sem_r.at[k])

        def scatter(c, buf):
            a = pltpu.make_async_copy(buf, out_hbm.at[idx0_v.at[c]], sem_w.at[0])
            b = pltpu.make_async_copy(buf, out_hbm.at[idx1_v.at[c]], sem_w.at[1])
            a.start()
            b.start()
            a.wait()
            b.wait()

        read(0, buf0, 0).start()

        @pl.loop(0, nch // 2)
        def _(g):
            c = 2 * g
            read(c + 1, buf1, 1).start()
            read(c, buf0, 0).wait()
            scatter(c, buf0)

            @pl.when(c + 2 < nch)
            def _():
                read(c + 2, buf0, 0).start()

            read(c + 1, buf1, 1).wait()
            scatter(c + 1, buf1)

    return pl.kernel(
        body,
        out_type=jax.ShapeDtypeStruct((n_rows, width), xp.dtype),
        mesh=_sc_mesh(),
        scratch_types=[
            pltpu.VMEM((nch, chunk), I32),
            pltpu.VMEM((nch, chunk), I32),
            pltpu.VMEM((chunk, width), xp.dtype),
            pltpu.VMEM((chunk, width), xp.dtype),
            pltpu.SemaphoreType.DMA((2,)),
            pltpu.SemaphoreType.DMA((2,)),
        ],
        compiler_params=_sc_params(),
        name="sc_dispatch",
    )(xp, dest2d)


def _sc_combine(ys, dest2d):
    n_idx_rows, chunk = dest2d.shape
    width = ys.shape[1]
    nch = n_idx_rows // SC_WORKERS
    assert n_idx_rows % (SC_WORKERS * 2) == 0

    def body(y_hbm, dest_hbm, out_hbm, idx_v, buf0, buf1, sem_g):
        wid = _worker_id()
        base = wid * nch * chunk
        pltpu.sync_copy(dest_hbm.at[pl.ds(wid * nch, nch)], idx_v)

        def gather(c, buf, k):
            return pltpu.make_async_copy(y_hbm.at[idx_v.at[c]], buf, sem_g.at[k])

        def write(c, buf):
            pltpu.sync_copy(buf, out_hbm.at[pl.ds(base + c * chunk, chunk)])

        gather(0, buf0, 0).start()

        @pl.loop(0, nch // 2)
        def _(g):
            c = 2 * g
            gather(c + 1, buf1, 1).start()
            gather(c, buf0, 0).wait()
            write(c, buf0)

            @pl.when(c + 2 < nch)
            def _():
                gather(c + 2, buf0, 0).start()

            gather(c + 1, buf1, 1).wait()
            write(c + 1, buf1)

    return pl.kernel(
        body,
        out_type=jax.ShapeDtypeStruct((n_idx_rows * chunk, width), ys.dtype),
        mesh=_sc_mesh(),
        scratch_types=[
            pltpu.VMEM((nch, chunk), I32),
            pltpu.VMEM((chunk, width), ys.dtype),
            pltpu.VMEM((chunk, width), ys.dtype),
            pltpu.SemaphoreType.DMA((2,)),
        ],
        compiler_params=_sc_params(),
        name="sc_combine",
    )(ys, dest2d)


def _pack_bf16_pairs(v):
    half = v.shape[1] // 2
    lo = pltpu.bitcast(v[:, :half].astype(BF16).astype(F32), jnp.uint32) >> 16
    hi = pltpu.bitcast(v[:, half:].astype(BF16).astype(F32), jnp.uint32) & jnp.uint32(0xFFFF0000)
    return pltpu.bitcast(hi | lo, I32)


def _unpack_bf16_pairs(w):
    w = pltpu.bitcast(w, jnp.uint32)
    lo = pltpu.bitcast(w << 16, F32)
    hi = pltpu.bitcast(w & jnp.uint32(0xFFFF0000), F32)
    return jnp.concatenate([lo, hi], axis=1)


def _ffn_kernel(be_ref, nv_ref, nu_ref, xs_ref, wg_ref, wu_ref, wd_ref, o_ref, wg_s, wu_s, wd_s):
    i = pl.program_id(0)

    @pl.when(jnp.logical_or(i == 0, be_ref[i] != be_ref[jnp.maximum(i - 1, 0)]))
    def _():
        wg_s[...] = wg_ref[0].astype(BF16)
        wu_s[...] = wu_ref[0].astype(BF16)
        wd_s[...] = wd_ref[0].astype(BF16)

    @pl.when(i < nu_ref[0])
    def _():
        live = lax.broadcasted_iota(I32, xs_ref.shape, 0) < nv_ref[i]
        xb = _unpack_bf16_pairs(jnp.where(live, xs_ref[...], 0)).astype(BF16)
        act = jax.nn.silu(_dot(xb, wg_s[...])) * _dot(xb, wu_s[...])
        o_ref[...] = _pack_bf16_pairs(_dot(act.astype(BF16), wd_s[...]))


def _expert_ffn(xs, blk_expert, blk_valid, n_used, w_gate, w_up, w_down):
    n_rows, half = xs.shape
    d = 2 * half
    de = w_gate.shape[2]
    bm = EXPERT_ROWS
    rows = lambda i, be, nv, nu: (jnp.minimum(i, nu[0] - 1), 0)
    wsel = lambda i, be, nv, nu: (be[i], 0, 0)
    return pl.pallas_call(
        _ffn_kernel,
        grid_spec=pltpu.PrefetchScalarGridSpec(
            num_scalar_prefetch=3,
            grid=(n_rows // bm,),
            in_specs=[
                pl.BlockSpec((bm, half), rows),
                pl.BlockSpec((1, d, de), wsel),
                pl.BlockSpec((1, d, de), wsel),
                pl.BlockSpec((1, de, d), wsel),
            ],
            out_specs=pl.BlockSpec((bm, half), rows),
            scratch_shapes=[pltpu.VMEM((d, de), BF16), pltpu.VMEM((d, de), BF16), pltpu.VMEM((de, d), BF16)],
        ),
        out_shape=jax.ShapeDtypeStruct((n_rows, half), I32),
        compiler_params=_tc_params(1),
        name="expert_ffn",
    )(blk_expert, blk_valid, n_used, xs, w_gate, w_up, w_down)


def _combine_kernel(x_ref, y0_ref, y1_ref, rt_ref, g_ref, b_ref, o_ref):
    wt = pltpu.bitcast(rt_ref[...], F32).T
    y = wt[:, 4:5] * _unpack_bf16_pairs(y0_ref[...]) + wt[:, 5:6] * _unpack_bf16_pairs(y1_ref[...])
    o_ref[...] = _layer_norm(ALPHA * x_ref[...] + y, g_ref[...], b_ref[...])


def _combine_ln(x2, y01, rt, g3, b3):
    t_all, d = x2.shape
    tq = TOKEN_TILE
    nt = t_all // tq
    return pl.pallas_call(
        _combine_kernel,
        grid=(nt,),
        in_specs=[
            pl.BlockSpec((tq, d), lambda i: (i, 0)),
            pl.BlockSpec((tq, d // 2), lambda i: (i, 0)),
            pl.BlockSpec((tq, d // 2), lambda i: (i + nt, 0)),
            pl.BlockSpec((8, tq), lambda i: (0, i)),
            _const_spec((1, d)), _const_spec((1, d)),
        ],
        out_specs=pl.BlockSpec((tq, d), lambda i: (i, 0)),
        out_shape=jax.ShapeDtypeStruct((t_all, d), F32),
        compiler_params=_tc_params(1),
        name="combine_ln",
    )(x2, y01, y01, rt, g3, b3)


def _routing_tables(rt, counts):
    bm = EXPERT_ROWS
    t_all = rt.shape[1]
    n_rows = 2 * t_all + N_EXPERTS * bm
    cnt = counts[:, 0]
    padded = (cnt + bm - 1) // bm * bm
    ends = jnp.cumsum(padded)
    offs = ends - padded
    experts = rt[0:2]
    off_tok = jnp.sum(jnp.where(experts[None] == jnp.arange(N_EXPERTS, dtype=I32)[:, None, None],
                                offs[:, None, None], 0), axis=0)
    dest2d = (off_tok + rt[2:4]).reshape(-1, SC_CHUNK).astype(I32)
    blk_start = jnp.arange(n_rows // bm, dtype=I32) * bm
    blk_expert = jnp.minimum(jnp.sum(blk_start[:, None] >= ends[None, :], axis=1), N_EXPERTS - 1).astype(I32)
    live_end = jnp.sum(jnp.where(blk_expert[:, None] == jnp.arange(N_EXPERTS, dtype=I32)[None, :],
                                 (offs + cnt)[None, :], 0), axis=1)
    blk_valid = jnp.clip(live_end - blk_start, 0, bm).astype(I32)
    n_used = (ends[-1:] // bm).astype(I32)
    return dest2d, blk_expert, blk_valid, n_used, n_rows


def _layer_tail(x1, mem2d, p, router, bsz, s_len):
    rw_hi, rw_lo, rbias = router
    kv = _kv_proj(mem2d, p["xkv"], bsz)
    x2, xp, rt, counts = _xattn_route(x1, kv, p["xq"], p["xo"], p["ln2_g"], p["ln2_b"],
                                      rw_hi, rw_lo, rbias, bsz, s_len)
    dest2d, blk_expert, blk_valid, n_used, n_rows = _routing_tables(rt, counts)
    xs = _sc_dispatch(xp, dest2d, n_rows)
    ys = _expert_ffn(xs, blk_expert, blk_valid, n_used, p["e_gate"], p["e_up"], p["e_down"])
    y01 = _sc_combine(ys, dest2d)
    return _combine_ln(x2, y01, rt, p["ln3_g"], p["ln3_b"])


def _row(v):
    return v.reshape(1, -1).astype(F32)


def _common_params(xq, xkv, xo, ln2_g, ln2_b, e_gate, e_up, e_down, ln3_g, ln3_b):
    return dict(xq=xq, xkv=xkv, xo=xo, ln2_g=_row(ln2_g), ln2_b=_row(ln2_b),
                e_gate=e_gate, e_up=e_up, e_down=e_down, ln3_g=_row(ln3_g), ln3_b=_row(ln3_b))


def kernel(x, mem, positions, router_w, router_bias, l0_w_in, l0_sinks, l0_sgu_ln_g, l0_sgu_ln_b, l0_sgu_w, l0_sgu_b, l0_w_out, l0_ln1_g, l0_ln1_b, l0_xq, l0_xkv, l0_xo, l0_ln2_g, l0_ln2_b, l0_e_gate, l0_e_up, l0_e_down, l0_ln3_g, l0_ln3_b, l1_w_in, l1_pool_w, l1_pool_scale, l1_w_out, l1_ln1_g, l1_ln1_b, l1_xq, l1_xkv, l1_xo, l1_ln2_g, l1_ln2_b, l1_e_gate, l1_e_up, l1_e_down, l1_ln3_g, l1_ln3_b):
    bsz, s_len, d = x.shape
    assert s_len % TOKEN_TILE == 0 and TOKEN_TILE % BLOCK == 0
    xt = x.reshape(bsz * s_len, d)
    mem2d = mem.reshape(-1, d)
    pos_col = positions.reshape(-1, 1).astype(I32)

    rw_t = router_w.T.astype(F32)
    rw_hi = rw_t.astype(BF16)
    rw_lo = (rw_t - rw_hi.astype(F32)).astype(BF16)
    router = (rw_hi, rw_lo, router_bias.reshape(-1, 1).astype(F32))

    half = ROPE_DIM // 2
    inv_freq = ROPE_THETA ** (-(jnp.arange(half, dtype=F32) * 2.0 / ROPE_DIM))
    lane = jnp.arange(128) % HEAD_DIM
    invf = jnp.where(lane < ROPE_DIM, inv_freq[lane % half], 0.0).reshape(1, 128).astype(F32)
    m1 = jnp.where((lane >= half) & (lane < ROPE_DIM), 1.0, 0.0).reshape(1, 128).astype(F32)
    m2 = jnp.where(lane < half, -1.0, 0.0).reshape(1, 128).astype(F32)
    grp = jnp.arange(B_WIDTH) // B_GROUP_DIM
    gsum = (grp[:, None] == grp[None, :]).astype(BF16)
    bs_full = jnp.repeat(l0_sgu_b.T.astype(F32), B_GROUP_DIM, axis=1)

    x1 = _mixer0(xt, pos_col, l0_sinks.astype(F32), l0_w_in, invf, m1, m2, gsum,
                 _row(l0_sgu_ln_g), _row(l0_sgu_ln_b), l0_sgu_w.astype(F32), bs_full,
                 l0_w_out, _row(l0_ln1_g), _row(l0_ln1_b), bsz, s_len)
    p0 = _common_params(l0_xq, l0_xkv, l0_xo, l0_ln2_g, l0_ln2_b, l0_e_gate, l0_e_up, l0_e_down,
                        l0_ln3_g, l0_ln3_b)
    x3 = _layer_tail(x1, mem2d, p0, router, bsz, s_len)

    x1 = _mixer1(x3, l1_w_in, l1_pool_w, _row(l1_pool_scale),
                 l1_w_out, _row(l1_ln1_g), _row(l1_ln1_b), bsz, s_len)
    p1 = _common_params(l1_xq, l1_xkv, l1_xo, l1_ln2_g, l1_ln2_b, l1_e_gate, l1_e_up, l1_e_down,
                        l1_ln3_g, l1_ln3_b)
    x3 = _layer_tail(x1, mem2d, p1, router, bsz, s_len)
    return x3.reshape(bsz, s_len, d)
```

```python
import functools

import numpy as np
import jax
import jax.numpy as jnp
from jax import lax
from jax.experimental import pallas as pl
from jax.experimental.pallas import tpu as pltpu
from jax.experimental.pallas import tpu_sc as plsc

F32 = jnp.float32
BF16 = jnp.bfloat16
I32 = jnp.int32

DEPTH = 2
ALPHA = (2.0 * DEPTH) ** 0.25
LN_EPS = 1e-5

HEAD_DIM = 64
A_Q_HEADS = 8
A_KV_HEADS = 2
A_GROUP = A_Q_HEADS // A_KV_HEADS
BLOCK = 128
ROPE_THETA = 500000.0
ROPE_DIM = HEAD_DIM // 4
A_WIDTH = A_Q_HEADS * HEAD_DIM
KV_WIDTH = A_KV_HEADS * HEAD_DIM
B_GROUPS = 8
B_GROUP_DIM = 64
B_WIDTH = B_GROUPS * B_GROUP_DIM
POOL_WINDOWS = (2, 4, 8, 16)
POOL_HALO = 16
X_HEADS = 4
N_EXPERTS = 16
N_EXPERT_GROUPS = 4
EXPERTS_PER_GROUP = 4

TOKEN_TILE = 512
EXPERT_ROWS = 512
SC_WORKERS = 32
SC_CHUNK = 64
VMEM_LIMIT = 56 * 1024 * 1024
NEG_BIG = -1e30


def _layer_norm(z, g, b):
    mu = jnp.mean(z, axis=-1, keepdims=True)
    d = z - mu
    var = jnp.mean(d * d, axis=-1, keepdims=True)
    return d * lax.rsqrt(var + LN_EPS) * g + b


def _dot(a, b):
    return jnp.dot(a, b, preferred_element_type=F32)


def _dot_nt(a, b):
    return lax.dot_general(a, b, (((1,), (1,)), ((), ())), preferred_element_type=F32)


def _split_bf16(v):
    hi = v.astype(BF16)
    lo = (v - hi.astype(F32)).astype(BF16)
    return hi, lo


def _tc_params(n_axes):
    return pltpu.CompilerParams(dimension_semantics=("arbitrary",) * n_axes,
                                vmem_limit_bytes=VMEM_LIMIT)


def _const_spec(shape):
    nd = len(shape)
    return pl.BlockSpec(shape, lambda *_: (0,) * nd, pipeline_mode=pl.Buffered(1))

def _mixer0_kernel(sinks_ref, x_ref, pos_ref, win_ref, invf_ref, etab_ref, cbase_ref, gsum_ref,
                   lng_ref, lnb_ref, ws_ref, bs_ref, wout_ref, g1_ref, b1_ref,
                   o_ref, q_s, kv_s, u_s, vn_s, mix_s, wt_s, win_s, wout_s):
    b = pl.program_id(0)
    j = pl.program_id(1)
    tq = x_ref.shape[0]
    nblk = tq // BLOCK
    kvw = kv_s.shape[1]

    @pl.when(jnp.logical_and(b == 0, j == 0))
    def _():
        win_s[...] = win_ref[...].astype(BF16)
        wout_s[...] = wout_ref[...].astype(BF16)
        r = lax.broadcasted_iota(I32, (BLOCK, BLOCK), 0)
        c = lax.broadcasted_iota(I32, (BLOCK, BLOCK), 1)
        for g in range(B_GROUPS):
            wt_s[g] = jnp.where(c <= r, ws_ref[g], 0.0).astype(BF16)

    @pl.when(j == 0)
    def _():
        kv_s[0:BLOCK, :] = jnp.zeros((BLOCK, kvw), BF16)

    x = x_ref[...]
    h = _dot(x.astype(BF16), win_s[...])

    ang = invf_ref[...] * pos_ref[...].astype(F32)
    c8 = jnp.cos(ang)
    s8 = jnp.sin(ang)
    c8h = c8.astype(BF16).astype(F32)
    s8h = s8.astype(BF16).astype(F32)
    stack = jnp.concatenate([c8h, c8 - c8h, s8h, s8 - s8h, jnp.zeros((128 - 4 * 8, tq), F32)], axis=0)
    tabs = _dot(stack.T.astype(BF16), etab_ref[...])
    cs = tabs[:, 0:128] + cbase_ref[...]
    sa = tabs[:, 128:256]
    sb = tabs[:, 256:384]

    def rope(t):
        return t * cs + pltpu.roll(t, ROPE_DIM // 2, 1) * sa + pltpu.roll(t, 128 - ROPE_DIM // 2, 1) * sb

    for c in range(A_WIDTH // 128):
        t = h[:, c * 128:(c + 1) * 128] * (HEAD_DIM ** -0.5)
        q_s[:, c * 128:(c + 1) * 128] = rope(t).astype(BF16)
    c1 = A_WIDTH
    c2 = c1 + KV_WIDTH
    c3 = c2 + KV_WIDTH
    c4 = c3 + B_WIDTH
    low = lax.broadcasted_iota(I32, (tq, 128), 1) < HEAD_DIM
    kr = rope(h[:, c1:c2])
    kx = pltpu.roll(kr, HEAD_DIM, 1)
    vr = h[:, c2:c3]
    vx = pltpu.roll(vr, HEAD_DIM, 1)
    kv_cols = [jnp.where(low, kr, kx), jnp.where(low, kx, kr),
               jnp.where(low, vr, 0.0), jnp.where(low, 0.0, vx),
               jnp.where(low, vx, 0.0), jnp.where(low, 0.0, vr)]
    for c, col in enumerate(kv_cols):
        kv_s[BLOCK:, c * 128:(c + 1) * 128] = col.astype(BF16)

    u_s[...] = jax.nn.gelu(h[:, c3:c4])
    v = jax.nn.gelu(h[:, c4:])
    gsum = gsum_ref[...]
    mean = _dot(v.astype(BF16), gsum) * (1.0 / B_GROUP_DIM)
    d = v - mean
    var = _dot((d * d).astype(BF16), gsum) * (1.0 / B_GROUP_DIM)
    vn_s[...] = (d * lax.rsqrt(var + LN_EPS) * lng_ref[...] + lnb_ref[...]).astype(BF16)

    qi = lax.broadcasted_iota(I32, (BLOCK, 2 * BLOCK), 0)
    kj = lax.broadcasted_iota(I32, (BLOCK, 2 * BLOCK), 1)
    rel = qi + BLOCK - kj
    band = jnp.logical_and(rel >= 0, rel < BLOCK)
    low_q = lax.broadcasted_iota(I32, (BLOCK, 128), 1) < HEAD_DIM
    low_k = lax.broadcasted_iota(I32, (2 * BLOCK, 128), 1) < HEAD_DIM
    ones_lo = jnp.where(low_k, 1.0, 0.0).astype(BF16)
    ones_hi = jnp.where(low_k, 0.0, 1.0).astype(BF16)
    zero_q = jnp.zeros((BLOCK, 128), BF16)

    def block_body(n, carry):
        r0 = pl.multiple_of(n * BLOCK, BLOCK)
        kv = kv_s[pl.ds(r0, 2 * BLOCK), :]
        qb = q_s[pl.ds(r0, BLOCK), :]
        first = jnp.logical_and(j == 0, n == 0)
        valid = jnp.logical_and(band, kj >= jnp.where(first, BLOCK, 0))
        for c in range(A_WIDTH // 128):
            hk = (2 * c) // A_GROUP
            qp = qb[:, c * 128:(c + 1) * 128]
            kd = kv[:, hk * 128:(hk + 1) * 128]
            v_lo = jnp.concatenate([kv[:, (2 + 2 * hk) * 128:(3 + 2 * hk) * 128], ones_lo], axis=1)
            v_hi = jnp.concatenate([kv[:, (3 + 2 * hk) * 128:(4 + 2 * hk) * 128], ones_hi], axis=1)
            res = None
            esink = []
            for half, (qm, vm) in enumerate(((jnp.where(low_q, qp, zero_q), v_lo),
                                             (jnp.where(low_q, zero_q, qp), v_hi))):
                s = jnp.where(valid, _dot_nt(qm, kd), NEG_BIG)
                sink = sinks_ref[2 * c + half]
                m = jnp.maximum(jnp.max(s, axis=-1, keepdims=True), sink)
                pv = _dot(jnp.exp(s - m).astype(BF16), vm)
                res = pv if res is None else res + pv
                esink.append(jnp.exp(sink - m))
            den = res[:, 128:] + jnp.where(low_q, esink[0], esink[1])
            mix_s[pl.ds(r0, BLOCK), c * 128:(c + 1) * 128] = (res[:, :128] / den).astype(BF16)
        vnb = vn_s[pl.ds(r0, BLOCK), :]
        parts = []
        for c in range(B_WIDTH // 128):
            vp = vnb[:, c * 128:(c + 1) * 128]
            parts.append(_dot(wt_s[2 * c], jnp.where(low_q, vp, zero_q))
                         + _dot(wt_s[2 * c + 1], jnp.where(low_q, zero_q, vp)))
        mixed = jnp.concatenate(parts, axis=1) + bs_ref[...]
        mix_s[pl.ds(r0, BLOCK), A_WIDTH:] = (u_s[pl.ds(r0, BLOCK), :] * mixed).astype(BF16)
        return carry

    lax.fori_loop(0, nblk, block_body, 0)
    kv_s[0:BLOCK, :] = kv_s[tq:tq + BLOCK, :]

    z = ALPHA * x + _dot(mix_s[...], wout_s[...])
    o_ref[...] = _layer_norm(z, g1_ref[...], b1_ref[...])


def _mixer0(x, pos_row, sinks, w_in, invf, etab, cbase, gsum, lng, lnb, w_s, bs_full, w_out, g1, b1, bsz, s_len):
    t_all, d = x.shape
    tq = TOKEN_TILE
    nj = s_len // tq
    row = lambda bb, jj: (bb * nj + jj, 0)
    in_w = w_in.shape[1]
    return pl.pallas_call(
        _mixer0_kernel,
        grid=(bsz, nj),
        in_specs=[
            pl.BlockSpec(memory_space=pltpu.SMEM),
            pl.BlockSpec((tq, d), row),
            pl.BlockSpec((1, tq), lambda bb, jj: (0, bb * nj + jj)),
            _const_spec((d, in_w)),
            _const_spec((ROPE_DIM // 2, 1)), _const_spec((128, 3 * 128)), _const_spec((1, 128)),
            _const_spec((B_WIDTH, B_WIDTH)),
            _const_spec((1, B_WIDTH)), _const_spec((1, B_WIDTH)),
            _const_spec((B_GROUPS, BLOCK, BLOCK)),
            _const_spec((BLOCK, B_WIDTH)),
            _const_spec((A_WIDTH + B_WIDTH, d)),
            _const_spec((1, d)), _const_spec((1, d)),
        ],
        out_specs=pl.BlockSpec((tq, d), row),
        out_shape=jax.ShapeDtypeStruct((t_all, d), F32),
        scratch_shapes=[
            pltpu.VMEM((tq, A_WIDTH), BF16),
            pltpu.VMEM((tq + BLOCK, 6 * 128), BF16),
            pltpu.VMEM((tq, B_WIDTH), F32),
            pltpu.VMEM((tq, B_WIDTH), BF16),
            pltpu.VMEM((tq, A_WIDTH + B_WIDTH), BF16),
            pltpu.VMEM((B_GROUPS, BLOCK, BLOCK), BF16),
            pltpu.VMEM((d, in_w), BF16),
            pltpu.VMEM((A_WIDTH + B_WIDTH, d), BF16),
        ],
        compiler_params=_tc_params(2),
        name="mixer0",
    )(sinks, x, pos_row, w_in, invf, etab, cbase, gsum, lng, lnb, w_s, bs_full, w_out, g1, b1)


def _mixer1_kernel(x_ref, win_ref, pw_ref, ps_ref, wout_ref, g1_ref, b1_ref, o_ref,
                   h_s, mp_s, win_s, pw_s, wout_s):
    j = pl.program_id(1)
    tq = x_ref.shape[0]
    gw = x_ref.shape[1] // len(POOL_WINDOWS)

    @pl.when(jnp.logical_and(pl.program_id(0) == 0, j == 0))
    def _():
        win_s[...] = win_ref[...].astype(BF16)
        pw_s[...] = pw_ref[...].astype(BF16)
        wout_s[...] = wout_ref[...].astype(BF16)

    @pl.when(j == 0)
    def _():
        h_s[0:POOL_HALO, :] = jnp.zeros((POOL_HALO, h_s.shape[1]), F32)

    x = x_ref[...]
    h_s[POOL_HALO:, :] = _dot(x.astype(BF16), win_s[...])
    t_pos = j * tq + lax.broadcasted_iota(I32, (tq, 1), 0)
    for g, win in enumerate(POOL_WINDOWS):
        lo, hi = g * gw, (g + 1) * gw
        cur = h_s[POOL_HALO:, lo:hi]
        acc = cur
        for k in range(1, win):
            acc = acc + h_s[POOL_HALO - k:POOL_HALO - k + tq, lo:hi]
        count = jnp.minimum(t_pos + 1, win).astype(F32)
        pooled = acc / count - cur
        mapped = _dot(pooled.astype(BF16), pw_s[g])
        mp_s[:, lo:hi] = (mapped * ps_ref[:, lo:hi]).astype(BF16)
    h_s[0:POOL_HALO, :] = h_s[tq:tq + POOL_HALO, :]
    z = ALPHA * x + _dot(mp_s[...], wout_s[...])
    o_ref[...] = _layer_norm(z, g1_ref[...], b1_ref[...])


def _mixer1(x, w_in, pool_w, pool_scale, w_out, g1, b1, bsz, s_len):
    t_all, d = x.shape
    tq = TOKEN_TILE
    nj = s_len // tq
    row = lambda bb, jj: (bb * nj + jj, 0)
    ng = len(POOL_WINDOWS)
    return pl.pallas_call(
        _mixer1_kernel,
        grid=(bsz, nj),
        in_specs=[
            pl.BlockSpec((tq, d), row),
            _const_spec((d, d)),
            _const_spec((ng, d // ng, d // ng)),
            _const_spec((1, d)),
            _const_spec((d, d)),
            _const_spec((1, d)), _const_spec((1, d)),
        ],
        out_specs=pl.BlockSpec((tq, d), row),
        out_shape=jax.ShapeDtypeStruct((t_all, d), F32),
        scratch_shapes=[pltpu.VMEM((tq + POOL_HALO, d), F32), pltpu.VMEM((tq, d), BF16),
                        pltpu.VMEM((d, d), BF16), pltpu.VMEM((ng, d // ng, d // ng), BF16),
                        pltpu.VMEM((d, d), BF16)],
        compiler_params=_tc_params(2),
        name="mixer1",
    )(x, w_in, pool_w, pool_scale, w_out, g1, b1)


def _kv_kernel(mem_ref, w_ref, o_ref, w_s):
    @pl.when(pl.program_id(0) == 0)
    def _():
        w_s[...] = w_ref[...].astype(BF16)

    o_ref[...] = _dot(mem_ref[...].astype(BF16), w_s[...]).astype(BF16)


def _kv_proj(mem2d, wkv, bsz):
    rows, d = mem2d.shape
    m = rows // bsz
    return pl.pallas_call(
        _kv_kernel,
        grid=(bsz,),
        in_specs=[pl.BlockSpec((m, d), lambda i: (i, 0)), _const_spec(wkv.shape)],
        out_specs=pl.BlockSpec((m, wkv.shape[1]), lambda i: (i, 0)),
        out_shape=jax.ShapeDtypeStruct((rows, wkv.shape[1]), BF16),
        scratch_shapes=[pltpu.VMEM(wkv.shape, BF16)],
        compiler_params=_tc_params(1),
        name="kv_proj",
    )(mem2d, wkv)


def _top2_of4(v):
    hi01, lo01 = jnp.maximum(v[0], v[1]), jnp.minimum(v[0], v[1])
    hi23, lo23 = jnp.maximum(v[2], v[3]), jnp.minimum(v[2], v[3])
    return jnp.maximum(hi01, hi23) + jnp.maximum(jnp.minimum(hi01, hi23), jnp.maximum(lo01, lo23))


def _argmax_first(vals):
    best, idx = vals[0], jnp.zeros(vals[0].shape, I32)
    for i in range(1, len(vals)):
        better = vals[i] > best
        best = jnp.where(better, vals[i], best)
        idx = jnp.where(better, i, idx)
    return best, idx


def _xattn_kernel(x_ref, kv_ref, wq_ref, wo_ref, g2_ref, b2_ref, rwh_ref, rwl_ref, rb_ref,
                  x2_ref, xp_ref, rt_ref, cnt_ref, run_s, wq_s, wo_s):
    first = jnp.logical_and(pl.program_id(0) == 0, pl.program_id(1) == 0)
    tq, d = x_ref.shape
    hd = d // X_HEADS

    @pl.when(first)
    def _():
        run_s[...] = jnp.zeros(run_s.shape, F32)
        wq_s[...] = wq_ref[...].astype(BF16)
        wo_s[...] = wo_ref[...].astype(BF16)

    x = x_ref[...]
    q = _dot(x.astype(BF16), wq_s[...]) * (hd ** -0.5)
    outs = []
    for hx in range(X_HEADS):
        qh = q[:, hx * hd:(hx + 1) * hd].astype(BF16)
        kh = kv_ref[:, hx * hd:(hx + 1) * hd]
        vh = kv_ref[:, d + hx * hd:d + (hx + 1) * hd]
        s = _dot_nt(qh, kh)
        p = jnp.exp(s - jnp.max(s, axis=-1, keepdims=True))
        o = _dot(p.astype(BF16), vh) / jnp.sum(p, axis=-1, keepdims=True)
        outs.append(o.astype(BF16))
    att = _dot(jnp.concatenate(outs, axis=1), wo_s[...])
    x2 = _layer_norm(ALPHA * x + att, g2_ref[...], b2_ref[...])
    x2_ref[...] = x2

    xp_ref[...] = _pack_bf16_pairs(x2)

    xh, xl = _split_bf16(x2)
    logits = _dot_nt(rwh_ref[...], xh) + _dot_nt(rwh_ref[...], xl) + _dot_nt(rwl_ref[...], xh)
    e_max = jnp.max(logits, axis=0, keepdims=True)
    ex = jnp.exp(logits - e_max)
    scores = ex / jnp.sum(ex, axis=0, keepdims=True)
    biased = scores + rb_ref[...]
    sc = [scores[e:e + 1, :] for e in range(N_EXPERTS)]
    bi = [biased[e:e + 1, :] for e in range(N_EXPERTS)]
    epg = EXPERTS_PER_GROUP
    gscore = [_top2_of4(bi[g * epg:(g + 1) * epg]) for g in range(N_EXPERT_GROUPS)]
    _, gsel = _argmax_first(gscore)

    def pick(vals):
        return [functools.reduce(lambda acc, g: jnp.where(gsel == g, vals[g * epg + i], acc),
                                 range(1, N_EXPERT_GROUPS), vals[i]) for i in range(epg)]

    in_b = pick(bi)
    in_s = pick(sc)
    _, i0 = _argmax_first(in_b)
    _, i1 = _argmax_first([jnp.where(i0 == i, -jnp.inf, in_b[i]) for i in range(epg)])

    def take(vals, idx):
        return functools.reduce(lambda acc, i: jnp.where(idx == i, vals[i], acc), range(1, epg), vals[0])

    s0, s1 = take(in_s, i0), take(in_s, i1)
    w0, w1 = s0 / (s0 + s1), s1 / (s0 + s1)
    e0, e1 = gsel * epg + i0, gsel * epg + i1

    eid = lax.broadcasted_iota(I32, (N_EXPERTS, tq), 0)
    oh0 = eid == e0
    oh1 = eid == e1
    onehot = jnp.where(jnp.logical_or(oh0, oh1), 1.0, 0.0)
    rr = lax.broadcasted_iota(I32, (tq, tq), 0)
    cc = lax.broadcasted_iota(I32, (tq, tq), 1)
    upper = jnp.where(rr < cc, 1.0, 0.0).astype(BF16)
    prefix = _dot(onehot.astype(BF16), upper) + run_s[...]
    r0 = jnp.sum(jnp.where(oh0, prefix, 0.0), axis=0, keepdims=True).astype(I32)
    r1 = jnp.sum(jnp.where(oh1, prefix, 0.0), axis=0, keepdims=True).astype(I32)
    run_s[...] = run_s[...] + jnp.sum(onehot, axis=1, keepdims=True)
    cnt_ref[...] = jnp.broadcast_to(run_s[...], cnt_ref.shape).astype(I32)

    zero = jnp.zeros((1, tq), I32)
    rt_ref[...] = jnp.concatenate(
        [e0, e1, r0, r1, pltpu.bitcast(w0, I32), pltpu.bitcast(w1, I32), zero, zero], axis=0)


def _xattn_route(x1, kv, wq, wo, g2, b2, rw_hi, rw_lo, rbias, bsz, s_len):
    t_all, d = x1.shape
    tq = TOKEN_TILE
    nj = s_len // tq
    m = kv.shape[0] // bsz
    row = lambda bb, jj: (bb * nj + jj, 0)
    return pl.pallas_call(
        _xattn_kernel,
        grid=(bsz, nj),
        in_specs=[
            pl.BlockSpec((tq, d), row),
            pl.BlockSpec((m, 2 * d), lambda bb, jj: (bb, 0)),
            _const_spec((d, d)), _const_spec((d, d)),
            _const_spec((1, d)), _const_spec((1, d)),
            _const_spec((N_EXPERTS, d)), _const_spec((N_EXPERTS, d)),
            _const_spec((N_EXPERTS, 1)),
        ],
        out_specs=[
            pl.BlockSpec((tq, d), row),
            pl.BlockSpec((tq, d // 2), row),
            pl.BlockSpec((8, tq), lambda bb, jj: (0, bb * nj + jj)),
            _const_spec((N_EXPERTS, 128)),
        ],
        out_shape=[
            jax.ShapeDtypeStruct((t_all, d), F32),
            jax.ShapeDtypeStruct((t_all, d // 2), I32),
            jax.ShapeDtypeStruct((8, t_all), I32),
            jax.ShapeDtypeStruct((N_EXPERTS, 128), I32),
        ],
        scratch_shapes=[pltpu.VMEM((N_EXPERTS, 1), F32), pltpu.VMEM((d, d), BF16), pltpu.VMEM((d, d), BF16)],
        compiler_params=_tc_params(2),
        name="xattn_route",
    )(x1, kv, wq, wo, g2, b2, rw_hi, rw_lo, rbias)


def _sc_mesh():
    return plsc.VectorSubcoreMesh(core_axis_name="c", subcore_axis_name="s")


def _sc_params():
    return pltpu.CompilerParams(needs_layout_passes=False)


def _worker_id():
    return lax.axis_index("s") * lax.axis_size("c") + lax.axis_index("c")


def _sc_dispatch(xp, dest2d, n_rows):
    t_all, width = xp.shape
    chunk = dest2d.shape[1]
    tok_w = t_all // SC_WORKERS
    nch = tok_w // chunk
    slot1 = t_all // chunk
    assert t_all % (SC_WORKERS * chunk * 2) == 0

    def body(x_hbm, dest_hbm, out_hbm, idx0_v, idx1_v, buf0, buf1, sem_r, sem_w):
        wid = _worker_id()
        base = wid * tok_w
        pltpu.sync_copy(dest_hbm.at[pl.ds(wid * nch, nch)], idx0_v)
        pltpu.sync_copy(dest_hbm.at[pl.ds(slot1 + wid * nch, nch)], idx1_v)

        def read(c, buf, k):
            return pltpu.make_async_copy(x_hbm.at[pl.ds(base + c * chunk, chunk)], buf, sem_r.at[k])

        def scatter(c, buf):
            a = pltpu.make_async_copy(buf, out_hbm.at[idx0_v.at[c]], sem_w.at[0])
            b = pltpu.make_async_copy(buf, out_hbm.at[idx1_v.at[c]], sem_w.at[1])
            a.start()
            b.start()
            a.wait()
            b.wait()

        read(0, buf0, 0).start()

        @pl.loop(0, nch // 2)
        def _(g):
            c = 2 * g
            read(c + 1, buf1, 1).start()
            read(c, buf0, 0).wait()
            scatter(c, buf0)

            @pl.when(c + 2 < nch)
            def _():
                read(c + 2, buf0, 0).start()

            read(c + 1, buf1, 1).wait()
            scatter(c + 1, buf1)

    return pl.kernel(
        body,
        out_type=jax.ShapeDtypeStruct((n_rows, width), xp.dtype),
        mesh=_sc_mesh(),
        scratch_types=[
            pltpu.VMEM((nch, chunk), I32),
            pltpu.VMEM((nch, chunk), I32),
            pltpu.VMEM((chunk, width), xp.dtype),
            pltpu.VMEM((chunk, width), xp.dtype),
            pltpu.SemaphoreType.DMA((2,)),
            pltpu.SemaphoreType.DMA((2,)),
        ],
        compiler_params=_sc_params(),
        name="sc_dispatch",
    )(xp, dest2d)


def _sc_combine(ys, dest2d):
    n_idx_rows, chunk = dest2d.shape
    width = ys.shape[1]
    nch = n_idx_rows // SC_WORKERS
    assert n_idx_rows % (SC_WORKERS * 2) == 0

    def body(y_hbm, dest_hbm, out_hbm, idx_v, buf0, buf1, sem_g):
        wid = _worker_id()
        base = wid * nch * chunk
        pltpu.sync_copy(dest_hbm.at[pl.ds(wid * nch, nch)], idx_v)

        def gather(c, buf, k):
            return pltpu.make_async_copy(y_hbm.at[idx_v.at[c]], buf, sem_g.at[k])

        def write(c, buf):
            pltpu.sync_copy(buf, out_hbm.at[pl.ds(base + c * chunk, chunk)])

        gather(0, buf0, 0).start()

        @pl.loop(0, nch // 2)
        def _(g):
            c = 2 * g
            gather(c + 1, buf1, 1).start()
            gather(c, buf0, 0).wait()
            write(c, buf0)

            @pl.when(c + 2 < nch)
            def _():
                gather(c + 2, buf0, 0).start()

            gather(c + 1, buf1, 1).wait()
            write(c + 1, buf1)

    return pl.kernel(
        body,
        out_type=jax.ShapeDtypeStruct((n_idx_rows * chunk, width), ys.dtype),
        mesh=_sc_mesh(),
        scratch_types=[
            pltpu.VMEM((nch, chunk), I32),
            pltpu.VMEM((chunk, width), ys.dtype),
            pltpu.VMEM((chunk, width), ys.dtype),
            pltpu.SemaphoreType.DMA((2,)),
        ],
        compiler_params=_sc_params(),
        name="sc_combine",
    )(ys, dest2d)


def _pack_bf16_pairs(v):
    half = v.shape[1] // 2
    lo = pltpu.bitcast(v[:, :half].astype(BF16).astype(F32), jnp.uint32) >> 16
    hi = pltpu.bitcast(v[:, half:].astype(BF16).astype(F32), jnp.uint32) & jnp.uint32(0xFFFF0000)
    return pltpu.bitcast(hi | lo, I32)


def _unpack_bf16_pairs(w):
    w = pltpu.bitcast(w, jnp.uint32)
    lo = pltpu.bitcast(w << 16, F32)
    hi = pltpu.bitcast(w & jnp.uint32(0xFFFF0000), F32)
    return jnp.concatenate([lo, hi], axis=1)


def _ffn_kernel(be_ref, nv_ref, nu_ref, xs_ref, wg_ref, wu_ref, wd_ref, o_ref, wg_s, wu_s, wd_s):
    i = pl.program_id(0)

    @pl.when(jnp.logical_or(i == 0, be_ref[i] != be_ref[jnp.maximum(i - 1, 0)]))
    def _():
        wg_s[...] = wg_ref[0].astype(BF16)
        wu_s[...] = wu_ref[0].astype(BF16)
        wd_s[...] = wd_ref[0].astype(BF16)

    @pl.when(i < nu_ref[0])
    def _():
        live = lax.broadcasted_iota(I32, xs_ref.shape, 0) < nv_ref[i]
        xb = _unpack_bf16_pairs(jnp.where(live, xs_ref[...], 0)).astype(BF16)
        act = jax.nn.silu(_dot(xb, wg_s[...])) * _dot(xb, wu_s[...])
        o_ref[...] = _pack_bf16_pairs(_dot(act.astype(BF16), wd_s[...]))


def _expert_ffn(xs, blk_expert, blk_valid, n_used, w_gate, w_up, w_down):
    n_rows, half = xs.shape
    d = 2 * half
    de = w_gate.shape[2]
    bm = EXPERT_ROWS
    rows = lambda i, be, nv, nu: (jnp.minimum(i, nu[0] - 1), 0)
    wsel = lambda i, be, nv, nu: (be[i], 0, 0)
    return pl.pallas_call(
        _ffn_kernel,
        grid_spec=pltpu.PrefetchScalarGridSpec(
            num_scalar_prefetch=3,
            grid=(n_rows // bm,),
            in_specs=[
                pl.BlockSpec((bm, half), rows),
                pl.BlockSpec((1, d, de), wsel),
                pl.BlockSpec((1, d, de), wsel),
                pl.BlockSpec((1, de, d), wsel),
            ],
            out_specs=pl.BlockSpec((bm, half), rows),
            scratch_shapes=[pltpu.VMEM((d, de), BF16), pltpu.VMEM((d, de), BF16), pltpu.VMEM((de, d), BF16)],
        ),
        out_shape=jax.ShapeDtypeStruct((n_rows, half), I32),
        compiler_params=_tc_params(1),
        name="expert_ffn",
    )(blk_expert, blk_valid, n_used, xs, w_gate, w_up, w_down)


def _combine_kernel(x_ref, y0_ref, y1_ref, rt_ref, g_ref, b_ref, o_ref):
    wt = pltpu.bitcast(rt_ref[...], F32).T
    y = wt[:, 4:5] * _unpack_bf16_pairs(y0_ref[...]) + wt[:, 5:6] * _unpack_bf16_pairs(y1_ref[...])
    o_ref[...] = _layer_norm(ALPHA * x_ref[...] + y, g_ref[...], b_ref[...])


def _combine_ln(x2, y01, rt, g3, b3):
    t_all, d = x2.shape
    tq = TOKEN_TILE
    nt = t_all // tq
    return pl.pallas_call(
        _combine_kernel,
        grid=(nt,),
        in_specs=[
            pl.BlockSpec((tq, d), lambda i: (i, 0)),
            pl.BlockSpec((tq, d // 2), lambda i: (i, 0)),
            pl.BlockSpec((tq, d // 2), lambda i: (i + nt, 0)),
            pl.BlockSpec((8, tq), lambda i: (0, i)),
            _const_spec((1, d)), _const_spec((1, d)),
        ],
        out_specs=pl.BlockSpec((tq, d), lambda i: (i, 0)),
        out_shape=jax.ShapeDtypeStruct((t_all, d), F32),
        compiler_params=_tc_params(1),
        name="combine_ln",
    )(x2, y01, y01, rt, g3, b3)


def _routing_tables(rt, counts):
    bm = EXPERT_ROWS
    t_all = rt.shape[1]
    n_rows = 2 * t_all + N_EXPERTS * bm
    cnt = counts[:, 0]
    padded = (cnt + bm - 1) // bm * bm
    ends = jnp.cumsum(padded)
    offs = ends - padded
    experts = rt[0:2]
    off_tok = jnp.sum(jnp.where(experts[None] == jnp.arange(N_EXPERTS, dtype=I32)[:, None, None],
                                offs[:, None, None], 0), axis=0)
    dest2d = (off_tok + rt[2:4]).reshape(-1, SC_CHUNK).astype(I32)
    blk_start = jnp.arange(n_rows // bm, dtype=I32) * bm
    blk_expert = jnp.minimum(jnp.sum(blk_start[:, None] >= ends[None, :], axis=1), N_EXPERTS - 1).astype(I32)
    live_end = jnp.sum(jnp.where(blk_expert[:, None] == jnp.arange(N_EXPERTS, dtype=I32)[None, :],
                                 (offs + cnt)[None, :], 0), axis=1)
    blk_valid = jnp.clip(live_end - blk_start, 0, bm).astype(I32)
    n_used = (ends[-1:] // bm).astype(I32)
    return dest2d, blk_expert, blk_valid, n_used, n_rows


def _layer_tail(x1, mem2d, p, router, bsz, s_len):
    rw_hi, rw_lo, rbias = router
    kv = _kv_proj(mem2d, p["xkv"], bsz)
    x2, xp, rt, counts = _xattn_route(x1, kv, p["xq"], p["xo"], p["ln2_g"], p["ln2_b"],
                                      rw_hi, rw_lo, rbias, bsz, s_len)
    dest2d, blk_expert, blk_valid, n_used, n_rows = _routing_tables(rt, counts)
    xs = _sc_dispatch(xp, dest2d, n_rows)
    ys = _expert_ffn(xs, blk_expert, blk_valid, n_used, p["e_gate"], p["e_up"], p["e_down"])
    y01 = _sc_combine(ys, dest2d)
    return _combine_ln(x2, y01, rt, p["ln3_g"], p["ln3_b"])


def _row(v):
    return v.reshape(1, -1).astype(F32)


def _common_params(xq, xkv, xo, ln2_g, ln2_b, e_gate, e_up, e_down, ln3_g, ln3_b):
    return dict(xq=xq, xkv=xkv, xo=xo, ln2_g=_row(ln2_g), ln2_b=_row(ln2_b),
                e_gate=e_gate, e_up=e_up, e_down=e_down, ln3_g=_row(ln3_g), ln3_b=_row(ln3_b))


def kernel(x, mem, positions, router_w, router_bias, l0_w_in, l0_sinks, l0_sgu_ln_g, l0_sgu_ln_b, l0_sgu_w, l0_sgu_b, l0_w_out, l0_ln1_g, l0_ln1_b, l0_xq, l0_xkv, l0_xo, l0_ln2_g, l0_ln2_b, l0_e_gate, l0_e_up, l0_e_down, l0_ln3_g, l0_ln3_b, l1_w_in, l1_pool_w, l1_pool_scale, l1_w_out, l1_ln1_g, l1_ln1_b, l1_xq, l1_xkv, l1_xo, l1_ln2_g, l1_ln2_b, l1_e_gate, l1_e_up, l1_e_down, l1_ln3_g, l1_ln3_b):
    bsz, s_len, d = x.shape
    assert s_len % TOKEN_TILE == 0 and TOKEN_TILE % BLOCK == 0
    xt = x.reshape(bsz * s_len, d)
    mem2d = mem.reshape(-1, d)

    rw_t = router_w.T.astype(F32)
    rw_hi = rw_t.astype(BF16)
    rw_lo = (rw_t - rw_hi.astype(F32)).astype(BF16)
    router = (rw_hi, rw_lo, router_bias.reshape(-1, 1).astype(F32))

    half = ROPE_DIM // 2
    inv_freq = (ROPE_THETA ** (-(jnp.arange(half, dtype=F32) * 2.0 / ROPE_DIM))).reshape(half, 1)
    etab_np = np.zeros((128, 3 * 128), np.float32)
    cbase_np = np.ones((1, 128), np.float32)
    for ln in range(128):
        dd = ln % HEAD_DIM
        if dd < ROPE_DIM:
            cbase_np[0, ln] = 0.0
            etab_np[[dd % half, half + dd % half], ln] = 1.0
            if dd >= half:
                etab_np[[2 * half + dd - half, 3 * half + dd - half], 128 + ln] = 1.0
            else:
                etab_np[[2 * half + dd, 3 * half + dd], 256 + ln] = -1.0
    etab = jnp.asarray(etab_np, BF16)
    cbase = jnp.asarray(cbase_np)
    pos_row = positions.reshape(1, -1).astype(I32)
    grp = jnp.arange(B_WIDTH) // B_GROUP_DIM
    gsum = (grp[:, None] == grp[None, :]).astype(BF16)
    bs_full = jnp.repeat(l0_sgu_b.T.astype(F32), B_GROUP_DIM, axis=1)

    x1 = _mixer0(xt, pos_row, l0_sinks.astype(F32), l0_w_in, inv_freq, etab, cbase, gsum,
                 _row(l0_sgu_ln_g), _row(l0_sgu_ln_b), l0_sgu_w.astype(F32), bs_full,
                 l0_w_out, _row(l0_ln1_g), _row(l0_ln1_b), bsz, s_len)
    p0 = _common_params(l0_xq, l0_xkv, l0_xo, l0_ln2_g, l0_ln2_b, l0_e_gate, l0_e_up, l0_e_down,
                        l0_ln3_g, l0_ln3_b)
    x3 = _layer_tail(x1, mem2d, p0, router, bsz, s_len)

    x1 = _mixer1(x3, l1_w_in, l1_pool_w, _row(l1_pool_scale),
                 l1_w_out, _row(l1_ln1_g), _row(l1_ln1_b), bsz, s_len)
    p1 = _common_params(l1_xq, l1_xkv, l1_xo, l1_ln2_g, l1_ln2_b, l1_e_gate, l1_e_up, l1_e_down,
                        l1_ln3_g, l1_ln3_b)
    x3 = _layer_tail(x1, mem2d, p1, router, bsz, s_len)
    return x3.reshape(bsz, s_len, d)
```

```python
import functools

import numpy as np
import jax
import jax.numpy as jnp
from jax import lax
from jax.experimental import pallas as pl
from jax.experimental.pallas import tpu as pltpu
from jax.experimental.pallas import tpu_sc as plsc

F32 = jnp.float32
BF16 = jnp.bfloat16
I32 = jnp.int32

DEPTH = 2
ALPHA = (2.0 * DEPTH) ** 0.25
LN_EPS = 1e-5

HEAD_DIM = 64
A_Q_HEADS = 8
A_KV_HEADS = 2
A_GROUP = A_Q_HEADS // A_KV_HEADS
BLOCK = 128
ROPE_THETA = 500000.0
ROPE_DIM = HEAD_DIM // 4
A_WIDTH = A_Q_HEADS * HEAD_DIM
KV_WIDTH = A_KV_HEADS * HEAD_DIM
B_GROUPS = 8
B_GROUP_DIM = 64
B_WIDTH = B_GROUPS * B_GROUP_DIM
POOL_WINDOWS = (2, 4, 8, 16)
POOL_HALO = 16
X_HEADS = 4
N_EXPERTS = 16
N_EXPERT_GROUPS = 4
EXPERTS_PER_GROUP = 4

TOKEN_TILE = 512
EXPERT_ROWS = 512
BATCH_SPLIT = 2
SC_WORKERS = 32
SC_CHUNK = 64
VMEM_LIMIT = 56 * 1024 * 1024
NEG_BIG = -1e30


def _layer_norm(z, g, b):
    mu = jnp.mean(z, axis=-1, keepdims=True)
    d = z - mu
    var = jnp.mean(d * d, axis=-1, keepdims=True)
    return d * lax.rsqrt(var + LN_EPS) * g + b


def _dot(a, b):
    return jnp.dot(a, b, preferred_element_type=F32)


def _dot_nt(a, b):
    return lax.dot_general(a, b, (((1,), (1,)), ((), ())), preferred_element_type=F32)


def _split_bf16(v):
    hi = v.astype(BF16)
    lo = (v - hi.astype(F32)).astype(BF16)
    return hi, lo


def _tc_params(n_axes):
    return pltpu.CompilerParams(dimension_semantics=("arbitrary",) * n_axes,
                                vmem_limit_bytes=VMEM_LIMIT)


def _const_spec(shape):
    nd = len(shape)
    return pl.BlockSpec(shape, lambda *_: (0,) * nd, pipeline_mode=pl.Buffered(1))

def _mixer0_kernel(sinks_ref, x_ref, pos_ref, win_ref, invf_ref, etab_ref, cbase_ref, gsum_ref,
                   lng_ref, lnb_ref, ws_ref, bs_ref, wout_ref, g1_ref, b1_ref,
                   o_ref, q_s, kv_s, u_s, vn_s, mix_s, wt_s, win_s, wout_s):
    b = pl.program_id(0)
    j = pl.program_id(1)
    tq = x_ref.shape[0]
    nblk = tq // BLOCK
    kvw = kv_s.shape[1]

    @pl.when(jnp.logical_and(b == 0, j == 0))
    def _():
        win_s[...] = win_ref[...].astype(BF16)
        wout_s[...] = wout_ref[...].astype(BF16)
        r = lax.broadcasted_iota(I32, (BLOCK, BLOCK), 0)
        c = lax.broadcasted_iota(I32, (BLOCK, BLOCK), 1)
        for g in range(B_GROUPS):
            wt_s[g] = jnp.where(c <= r, ws_ref[g], 0.0).astype(BF16)

    @pl.when(j == 0)
    def _():
        kv_s[0:BLOCK, :] = jnp.zeros((BLOCK, kvw), BF16)

    x = x_ref[...]
    h = _dot(x.astype(BF16), win_s[...])

    ang = invf_ref[...] * pos_ref[...].astype(F32)
    c8 = jnp.cos(ang)
    s8 = jnp.sin(ang)
    c8h = c8.astype(BF16).astype(F32)
    s8h = s8.astype(BF16).astype(F32)
    stack = jnp.concatenate([c8h, c8 - c8h, s8h, s8 - s8h, jnp.zeros((128 - 4 * 8, tq), F32)], axis=0)
    tabs = _dot(stack.T.astype(BF16), etab_ref[...])
    cs = tabs[:, 0:128] + cbase_ref[...]
    sa = tabs[:, 128:256]
    sb = tabs[:, 256:384]

    def rope(t):
        return t * cs + pltpu.roll(t, ROPE_DIM // 2, 1) * sa + pltpu.roll(t, 128 - ROPE_DIM // 2, 1) * sb

    for c in range(A_WIDTH // 128):
        t = h[:, c * 128:(c + 1) * 128] * (HEAD_DIM ** -0.5)
        q_s[:, c * 128:(c + 1) * 128] = rope(t).astype(BF16)
    c1 = A_WIDTH
    c2 = c1 + KV_WIDTH
    c3 = c2 + KV_WIDTH
    c4 = c3 + B_WIDTH
    low = lax.broadcasted_iota(I32, (tq, 128), 1) < HEAD_DIM
    kr = rope(h[:, c1:c2])
    kx = pltpu.roll(kr, HEAD_DIM, 1)
    vr = h[:, c2:c3]
    vx = pltpu.roll(vr, HEAD_DIM, 1)
    kv_cols = [jnp.where(low, kr, kx), jnp.where(low, kx, kr),
               jnp.where(low, vr, 0.0), jnp.where(low, 0.0, vx),
               jnp.where(low, vx, 0.0), jnp.where(low, 0.0, vr)]
    for c, col in enumerate(kv_cols):
        kv_s[BLOCK:, c * 128:(c + 1) * 128] = col.astype(BF16)

    u_s[...] = jax.nn.gelu(h[:, c3:c4])
    v = jax.nn.gelu(h[:, c4:])
    gsum = gsum_ref[...]
    mean = _dot(v.astype(BF16), gsum) * (1.0 / B_GROUP_DIM)
    d = v - mean
    var = _dot((d * d).astype(BF16), gsum) * (1.0 / B_GROUP_DIM)
    vn_s[...] = (d * lax.rsqrt(var + LN_EPS) * lng_ref[...] + lnb_ref[...]).astype(BF16)

    qi = lax.broadcasted_iota(I32, (BLOCK, 2 * BLOCK), 0)
    kj = lax.broadcasted_iota(I32, (BLOCK, 2 * BLOCK), 1)
    rel = qi + BLOCK - kj
    band = jnp.logical_and(rel >= 0, rel < BLOCK)
    low_q = lax.broadcasted_iota(I32, (BLOCK, 128), 1) < HEAD_DIM
    low_k = lax.broadcasted_iota(I32, (2 * BLOCK, 128), 1) < HEAD_DIM
    ones_lo = jnp.where(low_k, 1.0, 0.0).astype(BF16)
    ones_hi = jnp.where(low_k, 0.0, 1.0).astype(BF16)
    zero_q = jnp.zeros((BLOCK, 128), BF16)

    def block_body(n, carry):
        r0 = pl.multiple_of(n * BLOCK, BLOCK)
        kv = kv_s[pl.ds(r0, 2 * BLOCK), :]
        qb = q_s[pl.ds(r0, BLOCK), :]
        first = jnp.logical_and(j == 0, n == 0)
        valid = jnp.logical_and(band, kj >= jnp.where(first, BLOCK, 0))
        for c in range(A_WIDTH // 128):
            hk = (2 * c) // A_GROUP
            qp = qb[:, c * 128:(c + 1) * 128]
            kd = kv[:, hk * 128:(hk + 1) * 128]
            v_lo = jnp.concatenate([kv[:, (2 + 2 * hk) * 128:(3 + 2 * hk) * 128], ones_lo], axis=1)
            v_hi = jnp.concatenate([kv[:, (3 + 2 * hk) * 128:(4 + 2 * hk) * 128], ones_hi], axis=1)
            res = None
            esink = []
            for half, (qm, vm) in enumerate(((jnp.where(low_q, qp, zero_q), v_lo),
                                             (jnp.where(low_q, zero_q, qp), v_hi))):
                s = jnp.where(valid, _dot_nt(qm, kd), NEG_BIG)
                sink = sinks_ref[2 * c + half]
                m = jnp.maximum(jnp.max(s, axis=-1, keepdims=True), sink)
                pv = _dot(jnp.exp(s - m).astype(BF16), vm)
                res = pv if res is None else res + pv
                esink.append(jnp.exp(sink - m))
            den = res[:, 128:] + jnp.where(low_q, esink[0], esink[1])
            mix_s[pl.ds(r0, BLOCK), c * 128:(c + 1) * 128] = (res[:, :128] / den).astype(BF16)
        vnb = vn_s[pl.ds(r0, BLOCK), :]
        parts = []
        for c in range(B_WIDTH // 128):
            vp = vnb[:, c * 128:(c + 1) * 128]
            parts.append(_dot(wt_s[2 * c], jnp.where(low_q, vp, zero_q))
                         + _dot(wt_s[2 * c + 1], jnp.where(low_q, zero_q, vp)))
        mixed = jnp.concatenate(parts, axis=1) + bs_ref[...]
        mix_s[pl.ds(r0, BLOCK), A_WIDTH:] = (u_s[pl.ds(r0, BLOCK), :] * mixed).astype(BF16)
        return carry

    lax.fori_loop(0, nblk, block_body, 0)
    kv_s[0:BLOCK, :] = kv_s[tq:tq + BLOCK, :]

    z = ALPHA * x + _dot(mix_s[...], wout_s[...])
    o_ref[...] = _layer_norm(z, g1_ref[...], b1_ref[...])


def _mixer0(x, pos_row, sinks, w_in, invf, etab, cbase, gsum, lng, lnb, w_s, bs_full, w_out, g1, b1,
            b0, bsz, s_len):
    d = x.shape[1]
    t_all = bsz * s_len
    tq = TOKEN_TILE
    nj = s_len // tq
    row = lambda bb, jj: (bb * nj + jj, 0)
    in_w = w_in.shape[1]
    return pl.pallas_call(
        _mixer0_kernel,
        grid=(bsz, nj),
        in_specs=[
            pl.BlockSpec(memory_space=pltpu.SMEM),
            pl.BlockSpec((tq, d), lambda bb, jj: ((b0 + bb) * nj + jj, 0)),
            pl.BlockSpec((1, tq), lambda bb, jj: (0, (b0 + bb) * nj + jj)),
            _const_spec((d, in_w)),
            _const_spec((ROPE_DIM // 2, 1)), _const_spec((128, 3 * 128)), _const_spec((1, 128)),
            _const_spec((B_WIDTH, B_WIDTH)),
            _const_spec((1, B_WIDTH)), _const_spec((1, B_WIDTH)),
            _const_spec((B_GROUPS, BLOCK, BLOCK)),
            _const_spec((BLOCK, B_WIDTH)),
            _const_spec((A_WIDTH + B_WIDTH, d)),
            _const_spec((1, d)), _const_spec((1, d)),
        ],
        out_specs=pl.BlockSpec((tq, d), row),
        out_shape=jax.ShapeDtypeStruct((t_all, d), F32),
        scratch_shapes=[
            pltpu.VMEM((tq, A_WIDTH), BF16),
            pltpu.VMEM((tq + BLOCK, 6 * 128), BF16),
            pltpu.VMEM((tq, B_WIDTH), F32),
            pltpu.VMEM((tq, B_WIDTH), BF16),
            pltpu.VMEM((tq, A_WIDTH + B_WIDTH), BF16),
            pltpu.VMEM((B_GROUPS, BLOCK, BLOCK), BF16),
            pltpu.VMEM((d, in_w), BF16),
            pltpu.VMEM((A_WIDTH + B_WIDTH, d), BF16),
        ],
        compiler_params=_tc_params(2),
        name="mixer0",
    )(sinks, x, pos_row, w_in, invf, etab, cbase, gsum, lng, lnb, w_s, bs_full, w_out, g1, b1)


def _mixer1_kernel(x_ref, win_ref, pw_ref, ps_ref, wout_ref, g1_ref, b1_ref, o_ref,
                   h_s, mp_s, win_s, pw_s, wout_s):
    j = pl.program_id(1)
    tq = x_ref.shape[0]
    gw = x_ref.shape[1] // len(POOL_WINDOWS)

    @pl.when(jnp.logical_and(pl.program_id(0) == 0, j == 0))
    def _():
        win_s[...] = win_ref[...].astype(BF16)
        pw_s[...] = pw_ref[...].astype(BF16)
        wout_s[...] = wout_ref[...].astype(BF16)

    @pl.when(j == 0)
    def _():
        h_s[0:POOL_HALO, :] = jnp.zeros((POOL_HALO, h_s.shape[1]), F32)

    x = x_ref[...]
    h_s[POOL_HALO:, :] = _dot(x.astype(BF16), win_s[...])
    t_pos = j * tq + lax.broadcasted_iota(I32, (tq, 1), 0)
    for g, win in enumerate(POOL_WINDOWS):
        lo, hi = g * gw, (g + 1) * gw
        cur = h_s[POOL_HALO:, lo:hi]
        acc = cur
        for k in range(1, win):
            acc = acc + h_s[POOL_HALO - k:POOL_HALO - k + tq, lo:hi]
        count = jnp.minimum(t_pos + 1, win).astype(F32)
        pooled = acc / count - cur
        mapped = _dot(pooled.astype(BF16), pw_s[g])
        mp_s[:, lo:hi] = (mapped * ps_ref[:, lo:hi]).astype(BF16)
    h_s[0:POOL_HALO, :] = h_s[tq:tq + POOL_HALO, :]
    z = ALPHA * x + _dot(mp_s[...], wout_s[...])
    o_ref[...] = _layer_norm(z, g1_ref[...], b1_ref[...])


def _mixer1(x, w_in, pool_w, pool_scale, w_out, g1, b1, bsz, s_len):
    t_all, d = x.shape
    tq = TOKEN_TILE
    nj = s_len // tq
    row = lambda bb, jj: (bb * nj + jj, 0)
    ng = len(POOL_WINDOWS)
    return pl.pallas_call(
        _mixer1_kernel,
        grid=(bsz, nj),
        in_specs=[
            pl.BlockSpec((tq, d), row),
            _const_spec((d, d)),
            _const_spec((ng, d // ng, d // ng)),
            _const_spec((1, d)),
            _const_spec((d, d)),
            _const_spec((1, d)), _const_spec((1, d)),
        ],
        out_specs=pl.BlockSpec((tq, d), row),
        out_shape=jax.ShapeDtypeStruct((t_all, d), F32),
        scratch_shapes=[pltpu.VMEM((tq + POOL_HALO, d), F32), pltpu.VMEM((tq, d), BF16),
                        pltpu.VMEM((d, d), BF16), pltpu.VMEM((ng, d // ng, d // ng), BF16),
                        pltpu.VMEM((d, d), BF16)],
        compiler_params=_tc_params(2),
        name="mixer1",
    )(x, w_in, pool_w, pool_scale, w_out, g1, b1)


def _kv_kernel(mem_ref, w_ref, o_ref, w_s):
    @pl.when(pl.program_id(0) == 0)
    def _():
        w_s[...] = w_ref[...].astype(BF16)

    o_ref[...] = _dot(mem_ref[...].astype(BF16), w_s[...]).astype(BF16)


def _kv_proj(mem2d, wkv, bsz):
    rows, d = mem2d.shape
    m = rows // bsz
    return pl.pallas_call(
        _kv_kernel,
        grid=(bsz,),
        in_specs=[pl.BlockSpec((m, d), lambda i: (i, 0)), _const_spec(wkv.shape)],
        out_specs=pl.BlockSpec((m, wkv.shape[1]), lambda i: (i, 0)),
        out_shape=jax.ShapeDtypeStruct((rows, wkv.shape[1]), BF16),
        scratch_shapes=[pltpu.VMEM(wkv.shape, BF16)],
        compiler_params=_tc_params(1),
        name="kv_proj",
    )(mem2d, wkv)


def _top2_of4(v):
    hi01, lo01 = jnp.maximum(v[0], v[1]), jnp.minimum(v[0], v[1])
    hi23, lo23 = jnp.maximum(v[2], v[3]), jnp.minimum(v[2], v[3])
    return jnp.maximum(hi01, hi23) + jnp.maximum(jnp.minimum(hi01, hi23), jnp.maximum(lo01, lo23))


def _argmax_first(vals):
    best, idx = vals[0], jnp.zeros(vals[0].shape, I32)
    for i in range(1, len(vals)):
        better = vals[i] > best
        best = jnp.where(better, vals[i], best)
        idx = jnp.where(better, i, idx)
    return best, idx


def _xattn_kernel(x_ref, kv_ref, wq_ref, wo_ref, g2_ref, b2_ref, rwh_ref, rwl_ref, rb_ref,
                  x2_ref, xp_ref, rt_ref, cnt_ref, run_s, wq_s, wo_s):
    first = jnp.logical_and(pl.program_id(0) == 0, pl.program_id(1) == 0)
    tq, d = x_ref.shape
    hd = d // X_HEADS

    @pl.when(first)
    def _():
        run_s[...] = jnp.zeros(run_s.shape, F32)
        wq_s[...] = wq_ref[...].astype(BF16)
        wo_s[...] = wo_ref[...].astype(BF16)

    x = x_ref[...]
    q = _dot(x.astype(BF16), wq_s[...]) * (hd ** -0.5)
    outs = []
    for hx in range(X_HEADS):
        qh = q[:, hx * hd:(hx + 1) * hd].astype(BF16)
        kh = kv_ref[:, hx * hd:(hx + 1) * hd]
        vh = kv_ref[:, d + hx * hd:d + (hx + 1) * hd]
        s = _dot_nt(qh, kh)
        p = jnp.exp(s - jnp.max(s, axis=-1, keepdims=True))
        o = _dot(p.astype(BF16), vh) / jnp.sum(p, axis=-1, keepdims=True)
        outs.append(o.astype(BF16))
    att = _dot(jnp.concatenate(outs, axis=1), wo_s[...])
    x2 = _layer_norm(ALPHA * x + att, g2_ref[...], b2_ref[...])
    x2_ref[...] = x2

    xp_ref[...] = _pack_bf16_pairs(x2)

    xh, xl = _split_bf16(x2)
    logits = _dot_nt(rwh_ref[...], xh) + _dot_nt(rwh_ref[...], xl) + _dot_nt(rwl_ref[...], xh)
    e_max = jnp.max(logits, axis=0, keepdims=True)
    ex = jnp.exp(logits - e_max)
    scores = ex / jnp.sum(ex, axis=0, keepdims=True)
    biased = scores + rb_ref[...]
    sc = [scores[e:e + 1, :] for e in range(N_EXPERTS)]
    bi = [biased[e:e + 1, :] for e in range(N_EXPERTS)]
    epg = EXPERTS_PER_GROUP
    gscore = [_top2_of4(bi[g * epg:(g + 1) * epg]) for g in range(N_EXPERT_GROUPS)]
    _, gsel = _argmax_first(gscore)

    def pick(vals):
        return [functools.reduce(lambda acc, g: jnp.where(gsel == g, vals[g * epg + i], acc),
                                 range(1, N_EXPERT_GROUPS), vals[i]) for i in range(epg)]

    in_b = pick(bi)
    in_s = pick(sc)
    _, i0 = _argmax_first(in_b)
    _, i1 = _argmax_first([jnp.where(i0 == i, -jnp.inf, in_b[i]) for i in range(epg)])

    def take(vals, idx):
        return functools.reduce(lambda acc, i: jnp.where(idx == i, vals[i], acc), range(1, epg), vals[0])

    s0, s1 = take(in_s, i0), take(in_s, i1)
    w0, w1 = s0 / (s0 + s1), s1 / (s0 + s1)
    e0, e1 = gsel * epg + i0, gsel * epg + i1

    eid = lax.broadcasted_iota(I32, (N_EXPERTS, tq), 0)
    oh0 = eid == e0
    oh1 = eid == e1
    onehot = jnp.where(jnp.logical_or(oh0, oh1), 1.0, 0.0)
    rr = lax.broadcasted_iota(I32, (tq, tq), 0)
    cc = lax.broadcasted_iota(I32, (tq, tq), 1)
    upper = jnp.where(rr < cc, 1.0, 0.0).astype(BF16)
    prefix = _dot(onehot.astype(BF16), upper) + run_s[...]
    r0 = jnp.sum(jnp.where(oh0, prefix, 0.0), axis=0, keepdims=True).astype(I32)
    r1 = jnp.sum(jnp.where(oh1, prefix, 0.0), axis=0, keepdims=True).astype(I32)
    run_s[...] = run_s[...] + jnp.sum(onehot, axis=1, keepdims=True)
    cnt_ref[...] = jnp.broadcast_to(run_s[...], cnt_ref.shape).astype(I32)

    zero = jnp.zeros((1, tq), I32)
    rt_ref[...] = jnp.concatenate(
        [e0, e1, r0, r1, pltpu.bitcast(w0, I32), pltpu.bitcast(w1, I32), zero, zero], axis=0)


def _xattn_route(x1, kv, mem_len, wq, wo, g2, b2, rw_hi, rw_lo, rbias, b0, bsz, s_len):
    t_all, d = x1.shape
    tq = TOKEN_TILE
    nj = s_len // tq
    m = mem_len
    row = lambda bb, jj: (bb * nj + jj, 0)
    return pl.pallas_call(
        _xattn_kernel,
        grid=(bsz, nj),
        in_specs=[
            pl.BlockSpec((tq, d), row),
            pl.BlockSpec((m, 2 * d), lambda bb, jj: (b0 + bb, 0)),
            _const_spec((d, d)), _const_spec((d, d)),
            _const_spec((1, d)), _const_spec((1, d)),
            _const_spec((N_EXPERTS, d)), _const_spec((N_EXPERTS, d)),
            _const_spec((N_EXPERTS, 1)),
        ],
        out_specs=[
            pl.BlockSpec((tq, d), row),
            pl.BlockSpec((tq, d // 2), row),
            pl.BlockSpec((8, tq), lambda bb, jj: (0, bb * nj + jj)),
            _const_spec((N_EXPERTS, 128)),
        ],
        out_shape=[
            jax.ShapeDtypeStruct((t_all, d), F32),
            jax.ShapeDtypeStruct((t_all, d // 2), I32),
            jax.ShapeDtypeStruct((8, t_all), I32),
            jax.ShapeDtypeStruct((N_EXPERTS, 128), I32),
        ],
        scratch_shapes=[pltpu.VMEM((N_EXPERTS, 1), F32), pltpu.VMEM((d, d), BF16), pltpu.VMEM((d, d), BF16)],
        compiler_params=_tc_params(2),
        name="xattn_route",
    )(x1, kv, wq, wo, g2, b2, rw_hi, rw_lo, rbias)


def _sc_mesh():
    return plsc.VectorSubcoreMesh(core_axis_name="c", subcore_axis_name="s")


def _sc_params():
    return pltpu.CompilerParams(needs_layout_passes=False)


def _worker_id():
    return lax.axis_index("s") * lax.axis_size("c") + lax.axis_index("c")


def _sc_dispatch(xp, dest2d, n_rows):
    t_all, width = xp.shape
    chunk = dest2d.shape[1]
    tok_w = t_all // SC_WORKERS
    nch = tok_w // chunk
    slot1 = t_all // chunk
    assert t_all % (SC_WORKERS * chunk * 2) == 0

    def body(x_hbm, dest_hbm, out_hbm, idx0_v, idx1_v, buf0, buf1, sem_r, sem_w):
        wid = _worker_id()
        base = wid * tok_w
        pltpu.sync_copy(dest_hbm.at[pl.ds(wid * nch, nch)], idx0_v)
        pltpu.sync_copy(dest_hbm.at[pl.ds(slot1 + wid * nch, nch)], idx1_v)

        def read(c, buf, k):
            return pltpu.make_async_copy(x_hbm.at[pl.ds(base + c * chunk, chunk)], buf, sem_r.at[k])

        def scatter(c, buf):
            a = pltpu.make_async_copy(buf, out_hbm.at[idx0_v.at[c]], sem_w.at[0])
            b = pltpu.make_async_copy(buf, out_hbm.at[idx1_v.at[c]], sem_w.at[1])
            a.start()
            b.start()
            a.wait()
            b.wait()

        read(0, buf0, 0).start()

        @pl.loop(0, nch // 2)
        def _(g):
            c = 2 * g
            read(c + 1, buf1, 1).start()
            read(c, buf0, 0).wait()
            scatter(c, buf0)

            @pl.when(c + 2 < nch)
            def _():
                read(c + 2, buf0, 0).start()

            read(c + 1, buf1, 1).wait()
            scatter(c + 1, buf1)

    return pl.kernel(
        body,
        out_type=jax.ShapeDtypeStruct((n_rows, width), xp.dtype),
        mesh=_sc_mesh(),
        scratch_types=[
            pltpu.VMEM((nch, chunk), I32),
            pltpu.VMEM((nch, chunk), I32),
            pltpu.VMEM((chunk, width), xp.dtype),
            pltpu.VMEM((chunk, width), xp.dtype),
            pltpu.SemaphoreType.DMA((2,)),
            pltpu.SemaphoreType.DMA((2,)),
        ],
        compiler_params=_sc_params(),
        name="sc_dispatch",
    )(xp, dest2d)


def _sc_combine(ys, dest2d):
    n_idx_rows, chunk = dest2d.shape
    width = ys.shape[1]
    nch = n_idx_rows // SC_WORKERS
    assert n_idx_rows % (SC_WORKERS * 2) == 0

    def body(y_hbm, dest_hbm, out_hbm, idx_v, buf0, buf1, sem_g):
        wid = _worker_id()
        base = wid * nch * chunk
        pltpu.sync_copy(dest_hbm.at[pl.ds(wid * nch, nch)], idx_v)

        def gather(c, buf, k):
            return pltpu.make_async_copy(y_hbm.at[idx_v.at[c]], buf, sem_g.at[k])

        def write(c, buf):
            pltpu.sync_copy(buf, out_hbm.at[pl.ds(base + c * chunk, chunk)])

        gather(0, buf0, 0).start()

        @pl.loop(0, nch // 2)
        def _(g):
            c = 2 * g
            gather(c + 1, buf1, 1).start()
            gather(c, buf0, 0).wait()
            write(c, buf0)

            @pl.when(c + 2 < nch)
            def _():
                gather(c + 2, buf0, 0).start()

            gather(c + 1, buf1, 1).wait()
            write(c + 1, buf1)

    return pl.kernel(
        body,
        out_type=jax.ShapeDtypeStruct((n_idx_rows * chunk, width), ys.dtype),
        mesh=_sc_mesh(),
        scratch_types=[
            pltpu.VMEM((nch, chunk), I32),
            pltpu.VMEM((chunk, width), ys.dtype),
            pltpu.VMEM((chunk, width), ys.dtype),
            pltpu.SemaphoreType.DMA((2,)),
        ],
        compiler_params=_sc_params(),
        name="sc_combine",
    )(ys, dest2d)


def _pack_bf16_pairs(v):
    half = v.shape[1] // 2
    lo = pltpu.bitcast(v[:, :half].astype(BF16).astype(F32), jnp.uint32) >> 16
    hi = pltpu.bitcast(v[:, half:].astype(BF16).astype(F32), jnp.uint32) & jnp.uint32(0xFFFF0000)
    return pltpu.bitcast(hi | lo, I32)


def _unpack_bf16_pairs(w):
    w = pltpu.bitcast(w, jnp.uint32)
    lo = pltpu.bitcast(w << 16, F32)
    hi = pltpu.bitcast(w & jnp.uint32(0xFFFF0000), F32)
    return jnp.concatenate([lo, hi], axis=1)


def _ffn_kernel(be_ref, nv_ref, nu_ref, xs_ref, wg_ref, wu_ref, wd_ref, o_ref, wg_s, wu_s, wd_s):
    i = pl.program_id(0)

    @pl.when(jnp.logical_or(i == 0, be_ref[i] != be_ref[jnp.maximum(i - 1, 0)]))
    def _():
        wg_s[...] = wg_ref[0].astype(BF16)
        wu_s[...] = wu_ref[0].astype(BF16)
        wd_s[...] = wd_ref[0].astype(BF16)

    @pl.when(i < nu_ref[0])
    def _():
        live = lax.broadcasted_iota(I32, xs_ref.shape, 0) < nv_ref[i]
        xb = _unpack_bf16_pairs(jnp.where(live, xs_ref[...], 0)).astype(BF16)
        act = jax.nn.silu(_dot(xb, wg_s[...])) * _dot(xb, wu_s[...])
        o_ref[...] = _pack_bf16_pairs(_dot(act.astype(BF16), wd_s[...]))


def _expert_ffn(xs, blk_expert, blk_valid, n_used, w_gate, w_up, w_down):
    n_rows, half = xs.shape
    d = 2 * half
    de = w_gate.shape[2]
    bm = EXPERT_ROWS
    rows = lambda i, be, nv, nu: (jnp.minimum(i, nu[0] - 1), 0)
    wsel = lambda i, be, nv, nu: (be[i], 0, 0)
    return pl.pallas_call(
        _ffn_kernel,
        grid_spec=pltpu.PrefetchScalarGridSpec(
            num_scalar_prefetch=3,
            grid=(n_rows // bm,),
            in_specs=[
                pl.BlockSpec((bm, half), rows),
                pl.BlockSpec((1, d, de), wsel),
                pl.BlockSpec((1, d, de), wsel),
                pl.BlockSpec((1, de, d), wsel),
            ],
            out_specs=pl.BlockSpec((bm, half), rows),
            scratch_shapes=[pltpu.VMEM((d, de), BF16), pltpu.VMEM((d, de), BF16), pltpu.VMEM((de, d), BF16)],
        ),
        out_shape=jax.ShapeDtypeStruct((n_rows, half), I32),
        compiler_params=_tc_params(1),
        name="expert_ffn",
    )(blk_expert, blk_valid, n_used, xs, w_gate, w_up, w_down)


def _combine_kernel(x_ref, y0_ref, y1_ref, rt_ref, g_ref, b_ref, o_ref):
    wt = pltpu.bitcast(rt_ref[...], F32).T
    y = wt[:, 4:5] * _unpack_bf16_pairs(y0_ref[...]) + wt[:, 5:6] * _unpack_bf16_pairs(y1_ref[...])
    o_ref[...] = _layer_norm(ALPHA * x_ref[...] + y, g_ref[...], b_ref[...])


def _combine_kernel_into(x_ref, y0_ref, y1_ref, rt_ref, g_ref, b_ref, full_ref, o_ref):
    del full_ref
    _combine_kernel(x_ref, y0_ref, y1_ref, rt_ref, g_ref, b_ref, o_ref)


def _combine_ln(x2, y01, rt, g3, b3, into=None, row0=0, full_rows=None):
    t_all, d = x2.shape
    tq = TOKEN_TILE
    nt = t_all // tq
    blk0 = row0 // tq if full_rows else 0
    in_specs = [
        pl.BlockSpec((tq, d), lambda i: (i, 0)),
        pl.BlockSpec((tq, d // 2), lambda i: (i, 0)),
        pl.BlockSpec((tq, d // 2), lambda i: (i + nt, 0)),
        pl.BlockSpec((8, tq), lambda i: (0, i)),
        _const_spec((1, d)), _const_spec((1, d)),
    ]
    args = [x2, y01, y01, rt, g3, b3]
    body, aliases = _combine_kernel, {}
    if into is not None:
        in_specs.append(pl.BlockSpec(memory_space=pl.ANY))
        args.append(into)
        body, aliases = _combine_kernel_into, {len(args) - 1: 0}
    return pl.pallas_call(
        body,
        grid=(nt,),
        in_specs=in_specs,
        out_specs=pl.BlockSpec((tq, d), lambda i: (i + blk0, 0)),
        out_shape=jax.ShapeDtypeStruct((full_rows or t_all, d), F32),
        input_output_aliases=aliases,
        compiler_params=_tc_params(1),
        name="combine_ln",
    )(*args)


def _routing_tables(rt, counts):
    bm = EXPERT_ROWS
    t_all = rt.shape[1]
    n_rows = 2 * t_all + N_EXPERTS * bm
    cnt = counts[:, 0]
    padded = (cnt + bm - 1) // bm * bm
    ends = jnp.cumsum(padded)
    offs = ends - padded
    experts = rt[0:2]
    off_tok = jnp.sum(jnp.where(experts[None] == jnp.arange(N_EXPERTS, dtype=I32)[:, None, None],
                                offs[:, None, None], 0), axis=0)
    dest2d = (off_tok + rt[2:4]).reshape(-1, SC_CHUNK).astype(I32)
    blk_start = jnp.arange(n_rows // bm, dtype=I32) * bm
    blk_expert = jnp.minimum(jnp.sum(blk_start[:, None] >= ends[None, :], axis=1), N_EXPERTS - 1).astype(I32)
    live_end = jnp.sum(jnp.where(blk_expert[:, None] == jnp.arange(N_EXPERTS, dtype=I32)[None, :],
                                 (offs + cnt)[None, :], 0), axis=1)
    blk_valid = jnp.clip(live_end - blk_start, 0, bm).astype(I32)
    n_used = (ends[-1:] // bm).astype(I32)
    return dest2d, blk_expert, blk_valid, n_used, n_rows


def _layer_tail(x1, kv, mem_len, p, router, b0, bsz, s_len, into=None, full_rows=None):
    rw_hi, rw_lo, rbias = router
    x2, xp, rt, counts = _xattn_route(x1, kv, mem_len, p["xq"], p["xo"], p["ln2_g"], p["ln2_b"],
                                      rw_hi, rw_lo, rbias, b0, bsz, s_len)
    dest2d, blk_expert, blk_valid, n_used, n_rows = _routing_tables(rt, counts)
    xs = _sc_dispatch(xp, dest2d, n_rows)
    ys = _expert_ffn(xs, blk_expert, blk_valid, n_used, p["e_gate"], p["e_up"], p["e_down"])
    y01 = _sc_combine(ys, dest2d)
    return _combine_ln(x2, y01, rt, p["ln3_g"], p["ln3_b"], into=into, row0=b0 * s_len, full_rows=full_rows)


def _row(v):
    return v.reshape(1, -1).astype(F32)


def _common_params(xq, xkv, xo, ln2_g, ln2_b, e_gate, e_up, e_down, ln3_g, ln3_b):
    return dict(xq=xq, xkv=xkv, xo=xo, ln2_g=_row(ln2_g), ln2_b=_row(ln2_b),
                e_gate=e_gate, e_up=e_up, e_down=e_down, ln3_g=_row(ln3_g), ln3_b=_row(ln3_b))


def kernel(x, mem, positions, router_w, router_bias, l0_w_in, l0_sinks, l0_sgu_ln_g, l0_sgu_ln_b, l0_sgu_w, l0_sgu_b, l0_w_out, l0_ln1_g, l0_ln1_b, l0_xq, l0_xkv, l0_xo, l0_ln2_g, l0_ln2_b, l0_e_gate, l0_e_up, l0_e_down, l0_ln3_g, l0_ln3_b, l1_w_in, l1_pool_w, l1_pool_scale, l1_w_out, l1_ln1_g, l1_ln1_b, l1_xq, l1_xkv, l1_xo, l1_ln2_g, l1_ln2_b, l1_e_gate, l1_e_up, l1_e_down, l1_ln3_g, l1_ln3_b):
    bsz, s_len, d = x.shape
    assert s_len % TOKEN_TILE == 0 and TOKEN_TILE % BLOCK == 0
    xt = x.reshape(bsz * s_len, d)
    mem2d = mem.reshape(-1, d)

    rw_t = router_w.T.astype(F32)
    rw_hi = rw_t.astype(BF16)
    rw_lo = (rw_t - rw_hi.astype(F32)).astype(BF16)
    router = (rw_hi, rw_lo, router_bias.reshape(-1, 1).astype(F32))

    half = ROPE_DIM // 2
    inv_freq = (ROPE_THETA ** (-(jnp.arange(half, dtype=F32) * 2.0 / ROPE_DIM))).reshape(half, 1)
    etab_np = np.zeros((128, 3 * 128), np.float32)
    cbase_np = np.ones((1, 128), np.float32)
    for ln in range(128):
        dd = ln % HEAD_DIM
        if dd < ROPE_DIM:
            cbase_np[0, ln] = 0.0
            etab_np[[dd % half, half + dd % half], ln] = 1.0
            if dd >= half:
                etab_np[[2 * half + dd - half, 3 * half + dd - half], 128 + ln] = 1.0
            else:
                etab_np[[2 * half + dd, 3 * half + dd], 256 + ln] = -1.0
    etab = jnp.asarray(etab_np, BF16)
    cbase = jnp.asarray(cbase_np)
    pos_row = positions.reshape(1, -1).astype(I32)
    grp = jnp.arange(B_WIDTH) // B_GROUP_DIM
    gsum = (grp[:, None] == grp[None, :]).astype(BF16)
    bs_full = jnp.repeat(l0_sgu_b.T.astype(F32), B_GROUP_DIM, axis=1)

    p0 = _common_params(l0_xq, l0_xkv, l0_xo, l0_ln2_g, l0_ln2_b, l0_e_gate, l0_e_up, l0_e_down,
                        l0_ln3_g, l0_ln3_b)
    p1 = _common_params(l1_xq, l1_xkv, l1_xo, l1_ln2_g, l1_ln2_b, l1_e_gate, l1_e_up, l1_e_down,
                        l1_ln3_g, l1_ln3_b)
    kv0 = _kv_proj(mem2d, p0["xkv"], bsz)
    kv1 = _kv_proj(mem2d, p1["xkv"], bsz)
    mem_len = mem.shape[1]

    n_split = BATCH_SPLIT if bsz % BATCH_SPLIT == 0 else 1
    nb = bsz // n_split
    out = None
    for part in range(n_split):
        b0 = part * nb
        x1 = _mixer0(xt, pos_row, l0_sinks.astype(F32), l0_w_in, inv_freq, etab, cbase, gsum,
                     _row(l0_sgu_ln_g), _row(l0_sgu_ln_b), l0_sgu_w.astype(F32), bs_full,
                     l0_w_out, _row(l0_ln1_g), _row(l0_ln1_b), b0, nb, s_len)
        x3 = _layer_tail(x1, kv0, mem_len, p0, router, b0, nb, s_len)
        x1 = _mixer1(x3, l1_w_in, l1_pool_w, _row(l1_pool_scale),
                     l1_w_out, _row(l1_ln1_g), _row(l1_ln1_b), nb, s_len)
        out = _layer_tail(x1, kv1, mem_len, p1, router, b0, nb, s_len, into=out, full_rows=bsz * s_len)
    return out.reshape(bsz, s_len, d)
```

```python
import functools

import numpy as np
import jax
import jax.numpy as jnp
from jax import lax
from jax.experimental import pallas as pl
from jax.experimental.pallas import tpu as pltpu
from jax.experimental.pallas import tpu_sc as plsc

F32 = jnp.float32
BF16 = jnp.bfloat16
I32 = jnp.int32

DEPTH = 2
ALPHA = (2.0 * DEPTH) ** 0.25
LN_EPS = 1e-5

HEAD_DIM = 64
A_Q_HEADS = 8
A_KV_HEADS = 2
A_GROUP = A_Q_HEADS // A_KV_HEADS
BLOCK = 128
ROPE_THETA = 500000.0
ROPE_DIM = HEAD_DIM // 4
A_WIDTH = A_Q_HEADS * HEAD_DIM
KV_WIDTH = A_KV_HEADS * HEAD_DIM
B_GROUPS = 8
B_GROUP_DIM = 64
B_WIDTH = B_GROUPS * B_GROUP_DIM
POOL_WINDOWS = (2, 4, 8, 16)
POOL_HALO = 16
X_HEADS = 4
N_EXPERTS = 16
N_EXPERT_GROUPS = 4
EXPERTS_PER_GROUP = 4

TOKEN_TILE = 512
EXPERT_ROWS = 512
BATCH_SPLIT = 2
SC_WORKERS = 32
SC_CHUNK = 64
VMEM_LIMIT = 56 * 1024 * 1024
NEG_BIG = -1e30


def _layer_norm(z, g, b):
    mu = jnp.mean(z, axis=-1, keepdims=True)
    d = z - mu
    var = jnp.mean(d * d, axis=-1, keepdims=True)
    return d * lax.rsqrt(var + LN_EPS) * g + b


def _dot(a, b):
    return jnp.dot(a, b, preferred_element_type=F32)


def _dot_nt(a, b):
    return lax.dot_general(a, b, (((1,), (1,)), ((), ())), preferred_element_type=F32)


def _split_bf16(v):
    hi = v.astype(BF16)
    lo = (v - hi.astype(F32)).astype(BF16)
    return hi, lo


def _tc_params(n_axes):
    return pltpu.CompilerParams(dimension_semantics=("arbitrary",) * n_axes,
                                vmem_limit_bytes=VMEM_LIMIT)


def _const_spec(shape):
    nd = len(shape)
    return pl.BlockSpec(shape, lambda *_: (0,) * nd, pipeline_mode=pl.Buffered(1))

def _mixer0_kernel(sinks_ref, x_ref, pos_ref, win_ref, invf_ref, etab_ref, cbase_ref, gsum_ref,
                   lng_ref, lnb_ref, ws_ref, bs_ref, wout_ref, g1_ref, b1_ref,
                   o_ref, q_s, kv_s, u_s, vn_s, mix_s, wt_s, win_s, wout_s):
    b = pl.program_id(0)
    j = pl.program_id(1)
    tq = x_ref.shape[0]
    nblk = tq // BLOCK
    kvw = kv_s.shape[1]

    @pl.when(jnp.logical_and(b == 0, j == 0))
    def _():
        win_s[...] = win_ref[...].astype(BF16)
        wout_s[...] = wout_ref[...].astype(BF16)
        r = lax.broadcasted_iota(I32, (BLOCK, BLOCK), 0)
        c = lax.broadcasted_iota(I32, (BLOCK, BLOCK), 1)
        for g in range(B_GROUPS):
            wt_s[g] = jnp.where(c <= r, ws_ref[g], 0.0).astype(BF16)

    @pl.when(j == 0)
    def _():
        kv_s[0:BLOCK, :] = jnp.zeros((BLOCK, kvw), BF16)

    x = x_ref[...]
    h = _dot(x.astype(BF16), win_s[...])

    ang = invf_ref[...] * pos_ref[...].astype(F32)
    c8 = jnp.cos(ang)
    s8 = jnp.sin(ang)
    c8h = c8.astype(BF16).astype(F32)
    s8h = s8.astype(BF16).astype(F32)
    stack = jnp.concatenate([c8h, c8 - c8h, s8h, s8 - s8h, jnp.zeros((128 - 4 * 8, tq), F32)], axis=0)
    tabs = _dot(stack.T.astype(BF16), etab_ref[...])
    cs = tabs[:, 0:128] + cbase_ref[...]
    sa = tabs[:, 128:256]
    sb = tabs[:, 256:384]

    def rope(t):
        return t * cs + pltpu.roll(t, ROPE_DIM // 2, 1) * sa + pltpu.roll(t, 128 - ROPE_DIM // 2, 1) * sb

    for c in range(A_WIDTH // 128):
        t = h[:, c * 128:(c + 1) * 128] * (HEAD_DIM ** -0.5)
        q_s[:, c * 128:(c + 1) * 128] = rope(t).astype(BF16)
    c1 = A_WIDTH
    c2 = c1 + KV_WIDTH
    c3 = c2 + KV_WIDTH
    c4 = c3 + B_WIDTH
    low = lax.broadcasted_iota(I32, (tq, 128), 1) < HEAD_DIM
    kr = rope(h[:, c1:c2])
    kx = pltpu.roll(kr, HEAD_DIM, 1)
    vr = h[:, c2:c3]
    vx = pltpu.roll(vr, HEAD_DIM, 1)
    kv_cols = [jnp.where(low, kr, kx), jnp.where(low, kx, kr),
               jnp.where(low, vr, 0.0), jnp.where(low, 0.0, vx),
               jnp.where(low, vx, 0.0), jnp.where(low, 0.0, vr)]
    for c, col in enumerate(kv_cols):
        kv_s[BLOCK:, c * 128:(c + 1) * 128] = col.astype(BF16)

    u_s[...] = jax.nn.gelu(h[:, c3:c4])
    v = jax.nn.gelu(h[:, c4:])
    gsum = gsum_ref[...]
    mean = _dot(v.astype(BF16), gsum) * (1.0 / B_GROUP_DIM)
    d = v - mean
    var = _dot((d * d).astype(BF16), gsum) * (1.0 / B_GROUP_DIM)
    vn_s[...] = (d * lax.rsqrt(var + LN_EPS) * lng_ref[...] + lnb_ref[...]).astype(BF16)

    qi = lax.broadcasted_iota(I32, (BLOCK, 2 * BLOCK), 0)
    kj = lax.broadcasted_iota(I32, (BLOCK, 2 * BLOCK), 1)
    rel = qi + BLOCK - kj
    band = jnp.logical_and(rel >= 0, rel < BLOCK)
    low_q = lax.broadcasted_iota(I32, (BLOCK, 128), 1) < HEAD_DIM
    low_k = lax.broadcasted_iota(I32, (2 * BLOCK, 128), 1) < HEAD_DIM
    ones_lo = jnp.where(low_k, 1.0, 0.0).astype(BF16)
    ones_hi = jnp.where(low_k, 0.0, 1.0).astype(BF16)
    zero_q = jnp.zeros((BLOCK, 128), BF16)

    def block_body(n, carry):
        r0 = pl.multiple_of(n * BLOCK, BLOCK)
        kv = kv_s[pl.ds(r0, 2 * BLOCK), :]
        qb = q_s[pl.ds(r0, BLOCK), :]
        first = jnp.logical_and(j == 0, n == 0)
        valid = jnp.logical_and(band, kj >= jnp.where(first, BLOCK, 0))
        for c in range(A_WIDTH // 128):
            hk = (2 * c) // A_GROUP
            qp = qb[:, c * 128:(c + 1) * 128]
            kd = kv[:, hk * 128:(hk + 1) * 128]
            v_lo = jnp.concatenate([kv[:, (2 + 2 * hk) * 128:(3 + 2 * hk) * 128], ones_lo], axis=1)
            v_hi = jnp.concatenate([kv[:, (3 + 2 * hk) * 128:(4 + 2 * hk) * 128], ones_hi], axis=1)
            res = None
            esink = []
            for half, (qm, vm) in enumerate(((jnp.where(low_q, qp, zero_q), v_lo),
                                             (jnp.where(low_q, zero_q, qp), v_hi))):
                s = jnp.where(valid, _dot_nt(qm, kd), NEG_BIG)
                sink = sinks_ref[2 * c + half]
                m = jnp.maximum(jnp.max(s, axis=-1, keepdims=True), sink)
                pv = _dot(jnp.exp(s - m).astype(BF16), vm)
                res = pv if res is None else res + pv
                esink.append(jnp.exp(sink - m))
            den = res[:, 128:] + jnp.where(low_q, esink[0], esink[1])
            mix_s[pl.ds(r0, BLOCK), c * 128:(c + 1) * 128] = (res[:, :128] / den).astype(BF16)
        vnb = vn_s[pl.ds(r0, BLOCK), :]
        parts = []
        for c in range(B_WIDTH // 128):
            vp = vnb[:, c * 128:(c + 1) * 128]
            parts.append(_dot(wt_s[2 * c], jnp.where(low_q, vp, zero_q))
                         + _dot(wt_s[2 * c + 1], jnp.where(low_q, zero_q, vp)))
        mixed = jnp.concatenate(parts, axis=1) + bs_ref[...]
        mix_s[pl.ds(r0, BLOCK), A_WIDTH:] = (u_s[pl.ds(r0, BLOCK), :] * mixed).astype(BF16)
        return carry

    lax.fori_loop(0, nblk, block_body, 0)
    kv_s[0:BLOCK, :] = kv_s[tq:tq + BLOCK, :]

    z = ALPHA * x + _dot(mix_s[...], wout_s[...])
    o_ref[...] = _layer_norm(z, g1_ref[...], b1_ref[...])


def _mixer0(x, pos_row, sinks, w_in, invf, etab, cbase, gsum, lng, lnb, w_s, bs_full, w_out, g1, b1,
            b0, bsz, s_len):
    d = x.shape[1]
    t_all = bsz * s_len
    tq = TOKEN_TILE
    nj = s_len // tq
    row = lambda bb, jj: (bb * nj + jj, 0)
    in_w = w_in.shape[1]
    return pl.pallas_call(
        _mixer0_kernel,
        grid=(bsz, nj),
        in_specs=[
            pl.BlockSpec(memory_space=pltpu.SMEM),
            pl.BlockSpec((tq, d), lambda bb, jj: ((b0 + bb) * nj + jj, 0)),
            pl.BlockSpec((1, tq), lambda bb, jj: (0, (b0 + bb) * nj + jj)),
            _const_spec((d, in_w)),
            _const_spec((ROPE_DIM // 2, 1)), _const_spec((128, 3 * 128)), _const_spec((1, 128)),
            _const_spec((B_WIDTH, B_WIDTH)),
            _const_spec((1, B_WIDTH)), _const_spec((1, B_WIDTH)),
            _const_spec((B_GROUPS, BLOCK, BLOCK)),
            _const_spec((BLOCK, B_WIDTH)),
            _const_spec((A_WIDTH + B_WIDTH, d)),
            _const_spec((1, d)), _const_spec((1, d)),
        ],
        out_specs=pl.BlockSpec((tq, d), row),
        out_shape=jax.ShapeDtypeStruct((t_all, d), F32),
        scratch_shapes=[
            pltpu.VMEM((tq, A_WIDTH), BF16),
            pltpu.VMEM((tq + BLOCK, 6 * 128), BF16),
            pltpu.VMEM((tq, B_WIDTH), F32),
            pltpu.VMEM((tq, B_WIDTH), BF16),
            pltpu.VMEM((tq, A_WIDTH + B_WIDTH), BF16),
            pltpu.VMEM((B_GROUPS, BLOCK, BLOCK), BF16),
            pltpu.VMEM((d, in_w), BF16),
            pltpu.VMEM((A_WIDTH + B_WIDTH, d), BF16),
        ],
        compiler_params=_tc_params(2),
        name="mixer0",
    )(sinks, x, pos_row, w_in, invf, etab, cbase, gsum, lng, lnb, w_s, bs_full, w_out, g1, b1)


def _mixer1_kernel(x_ref, win_ref, pw_ref, ps_ref, wout_ref, g1_ref, b1_ref, o_ref,
                   h_s, mp_s, win_s, pw_s, wout_s):
    j = pl.program_id(1)
    tq = x_ref.shape[0]
    gw = x_ref.shape[1] // len(POOL_WINDOWS)

    @pl.when(jnp.logical_and(pl.program_id(0) == 0, j == 0))
    def _():
        win_s[...] = win_ref[...].astype(BF16)
        pw_s[...] = pw_ref[...].astype(BF16)
        wout_s[...] = wout_ref[...].astype(BF16)

    @pl.when(j == 0)
    def _():
        h_s[0:POOL_HALO, :] = jnp.zeros((POOL_HALO, h_s.shape[1]), F32)

    x = x_ref[...]
    h_s[POOL_HALO:, :] = _dot(x.astype(BF16), win_s[...])
    t_pos = j * tq + lax.broadcasted_iota(I32, (tq, 1), 0)
    for g, win in enumerate(POOL_WINDOWS):
        lo, hi = g * gw, (g + 1) * gw
        cur = h_s[POOL_HALO:, lo:hi]
        acc = cur
        for k in range(1, win):
            acc = acc + h_s[POOL_HALO - k:POOL_HALO - k + tq, lo:hi]
        count = jnp.minimum(t_pos + 1, win).astype(F32)
        pooled = acc / count - cur
        mapped = _dot(pooled.astype(BF16), pw_s[g])
        mp_s[:, lo:hi] = (mapped * ps_ref[:, lo:hi]).astype(BF16)
    h_s[0:POOL_HALO, :] = h_s[tq:tq + POOL_HALO, :]
    z = ALPHA * x + _dot(mp_s[...], wout_s[...])
    o_ref[...] = _layer_norm(z, g1_ref[...], b1_ref[...])


def _mixer1(x, w_in, pool_w, pool_scale, w_out, g1, b1, bsz, s_len):
    t_all, d = x.shape
    tq = TOKEN_TILE
    nj = s_len // tq
    row = lambda bb, jj: (bb * nj + jj, 0)
    ng = len(POOL_WINDOWS)
    return pl.pallas_call(
        _mixer1_kernel,
        grid=(bsz, nj),
        in_specs=[
            pl.BlockSpec((tq, d), row),
            _const_spec((d, d)),
            _const_spec((ng, d // ng, d // ng)),
            _const_spec((1, d)),
            _const_spec((d, d)),
            _const_spec((1, d)), _const_spec((1, d)),
        ],
        out_specs=pl.BlockSpec((tq, d), row),
        out_shape=jax.ShapeDtypeStruct((t_all, d), F32),
        scratch_shapes=[pltpu.VMEM((tq + POOL_HALO, d), F32), pltpu.VMEM((tq, d), BF16),
                        pltpu.VMEM((d, d), BF16), pltpu.VMEM((ng, d // ng, d // ng), BF16),
                        pltpu.VMEM((d, d), BF16)],
        compiler_params=_tc_params(2),
        name="mixer1",
    )(x, w_in, pool_w, pool_scale, w_out, g1, b1)


def _kv_kernel(mem_ref, w_ref, o_ref, w_s):
    @pl.when(pl.program_id(0) == 0)
    def _():
        w_s[...] = w_ref[...].astype(BF16)

    o_ref[...] = _dot(mem_ref[...].astype(BF16), w_s[...]).astype(BF16)


def _kv_proj(mem2d, wkv, bsz):
    rows, d = mem2d.shape
    m = rows // bsz
    return pl.pallas_call(
        _kv_kernel,
        grid=(bsz,),
        in_specs=[pl.BlockSpec((m, d), lambda i: (i, 0)), _const_spec(wkv.shape)],
        out_specs=pl.BlockSpec((m, wkv.shape[1]), lambda i: (i, 0)),
        out_shape=jax.ShapeDtypeStruct((rows, wkv.shape[1]), BF16),
        scratch_shapes=[pltpu.VMEM(wkv.shape, BF16)],
        compiler_params=_tc_params(1),
        name="kv_proj",
    )(mem2d, wkv)


def _top2_of4(v):
    hi01, lo01 = jnp.maximum(v[0], v[1]), jnp.minimum(v[0], v[1])
    hi23, lo23 = jnp.maximum(v[2], v[3]), jnp.minimum(v[2], v[3])
    return jnp.maximum(hi01, hi23) + jnp.maximum(jnp.minimum(hi01, hi23), jnp.maximum(lo01, lo23))


def _argmax_first(vals):
    best, idx = vals[0], jnp.zeros(vals[0].shape, I32)
    for i in range(1, len(vals)):
        better = vals[i] > best
        best = jnp.where(better, vals[i], best)
        idx = jnp.where(better, i, idx)
    return best, idx


def _xattn_kernel(x_ref, kv_ref, wq_ref, wo_ref, g2_ref, b2_ref, rwh_ref, rwl_ref, rb_ref,
                  x2_ref, xp_ref, rt_ref, cnt_ref, run_s, wq_s, wo_s):
    first = jnp.logical_and(pl.program_id(0) == 0, pl.program_id(1) == 0)
    tq, d = x_ref.shape
    hd = d // X_HEADS

    @pl.when(first)
    def _():
        run_s[...] = jnp.zeros(run_s.shape, F32)
        wq_s[...] = wq_ref[...].astype(BF16)
        wo_s[...] = wo_ref[...].astype(BF16)

    x = x_ref[...]
    q = _dot(x.astype(BF16), wq_s[...]) * (hd ** -0.5)
    outs = []
    for hx in range(X_HEADS):
        qh = q[:, hx * hd:(hx + 1) * hd].astype(BF16)
        kh = kv_ref[:, hx * hd:(hx + 1) * hd]
        vh = kv_ref[:, d + hx * hd:d + (hx + 1) * hd]
        s = _dot_nt(qh, kh)
        p = jnp.exp(s - jnp.max(s, axis=-1, keepdims=True))
        o = _dot(p.astype(BF16), vh) / jnp.sum(p, axis=-1, keepdims=True)
        outs.append(o.astype(BF16))
    att = _dot(jnp.concatenate(outs, axis=1), wo_s[...])
    x2 = _layer_norm(ALPHA * x + att, g2_ref[...], b2_ref[...])
    x2_ref[...] = x2

    xp_ref[...] = _pack_bf16_pairs(x2)

    xh, xl = _split_bf16(x2)
    logits = _dot_nt(rwh_ref[...], xh) + _dot_nt(rwh_ref[...], xl) + _dot_nt(rwl_ref[...], xh)
    e_max = jnp.max(logits, axis=0, keepdims=True)
    ex = jnp.exp(logits - e_max)
    scores = ex / jnp.sum(ex, axis=0, keepdims=True)
    biased = scores + rb_ref[...]
    sc = [scores[e:e + 1, :] for e in range(N_EXPERTS)]
    bi = [biased[e:e + 1, :] for e in range(N_EXPERTS)]
    epg = EXPERTS_PER_GROUP
    gscore = [_top2_of4(bi[g * epg:(g + 1) * epg]) for g in range(N_EXPERT_GROUPS)]
    _, gsel = _argmax_first(gscore)

    def pick(vals):
        return [functools.reduce(lambda acc, g: jnp.where(gsel == g, vals[g * epg + i], acc),
                                 range(1, N_EXPERT_GROUPS), vals[i]) for i in range(epg)]

    in_b = pick(bi)
    in_s = pick(sc)
    _, i0 = _argmax_first(in_b)
    _, i1 = _argmax_first([jnp.where(i0 == i, -jnp.inf, in_b[i]) for i in range(epg)])

    def take(vals, idx):
        return functools.reduce(lambda acc, i: jnp.where(idx == i, vals[i], acc), range(1, epg), vals[0])

    s0, s1 = take(in_s, i0), take(in_s, i1)
    w0, w1 = s0 / (s0 + s1), s1 / (s0 + s1)
    e0, e1 = gsel * epg + i0, gsel * epg + i1

    eid = lax.broadcasted_iota(I32, (N_EXPERTS, tq), 0)
    oh0 = eid == e0
    oh1 = eid == e1
    onehot = jnp.where(jnp.logical_or(oh0, oh1), 1.0, 0.0)
    rr = lax.broadcasted_iota(I32, (tq, tq), 0)
    cc = lax.broadcasted_iota(I32, (tq, tq), 1)
    upper = jnp.where(rr < cc, 1.0, 0.0).astype(BF16)
    prefix = _dot(onehot.astype(BF16), upper) + run_s[...]
    r0 = jnp.sum(jnp.where(oh0, prefix, 0.0), axis=0, keepdims=True).astype(I32)
    r1 = jnp.sum(jnp.where(oh1, prefix, 0.0), axis=0, keepdims=True).astype(I32)
    run_s[...] = run_s[...] + jnp.sum(onehot, axis=1, keepdims=True)
    cnt_ref[...] = jnp.broadcast_to(run_s[...], cnt_ref.shape).astype(I32)

    zero = jnp.zeros((1, tq), I32)
    rt_ref[...] = jnp.concatenate(
        [e0, e1, r0, r1, pltpu.bitcast(w0, I32), pltpu.bitcast(w1, I32), zero, zero], axis=0)


def _xattn_route(x1, kv, mem_len, wq, wo, g2, b2, rw_hi, rw_lo, rbias, b0, bsz, s_len):
    t_all, d = x1.shape
    tq = TOKEN_TILE
    nj = s_len // tq
    m = mem_len
    row = lambda bb, jj: (bb * nj + jj, 0)
    return pl.pallas_call(
        _xattn_kernel,
        grid=(bsz, nj),
        in_specs=[
            pl.BlockSpec((tq, d), row),
            pl.BlockSpec((m, 2 * d), lambda bb, jj: (b0 + bb, 0)),
            _const_spec((d, d)), _const_spec((d, d)),
            _const_spec((1, d)), _const_spec((1, d)),
            _const_spec((N_EXPERTS, d)), _const_spec((N_EXPERTS, d)),
            _const_spec((N_EXPERTS, 1)),
        ],
        out_specs=[
            pl.BlockSpec((tq, d), row),
            pl.BlockSpec((tq, d // 2), row),
            pl.BlockSpec((8, tq), lambda bb, jj: (0, bb * nj + jj)),
            _const_spec((N_EXPERTS, 128)),
        ],
        out_shape=[
            jax.ShapeDtypeStruct((t_all, d), F32),
            jax.ShapeDtypeStruct((t_all, d // 2), I32),
            jax.ShapeDtypeStruct((8, t_all), I32),
            jax.ShapeDtypeStruct((N_EXPERTS, 128), I32),
        ],
        scratch_shapes=[pltpu.VMEM((N_EXPERTS, 1), F32), pltpu.VMEM((d, d), BF16), pltpu.VMEM((d, d), BF16)],
        compiler_params=_tc_params(2),
        name="xattn_route",
    )(x1, kv, wq, wo, g2, b2, rw_hi, rw_lo, rbias)


def _sc_mesh():
    return plsc.VectorSubcoreMesh(core_axis_name="c", subcore_axis_name="s")


def _sc_params():
    return pltpu.CompilerParams(needs_layout_passes=False)


def _worker_id():
    return lax.axis_index("s") * lax.axis_size("c") + lax.axis_index("c")


def _sc_dispatch(xp, dest2d, n_rows):
    t_all, width = xp.shape
    chunk = dest2d.shape[1]
    tok_w = t_all // SC_WORKERS
    nch = tok_w // chunk
    slot1 = t_all // chunk
    assert t_all % (SC_WORKERS * chunk * 2) == 0

    def body(x_hbm, dest_hbm, out_hbm, idx0_v, idx1_v, buf0, buf1, sem_r, sem_w):
        wid = _worker_id()
        base = wid * tok_w
        pltpu.sync_copy(dest_hbm.at[pl.ds(wid * nch, nch)], idx0_v)
        pltpu.sync_copy(dest_hbm.at[pl.ds(slot1 + wid * nch, nch)], idx1_v)

        def read(c, buf, k):
            return pltpu.make_async_copy(x_hbm.at[pl.ds(base + c * chunk, chunk)], buf, sem_r.at[k])

        def scatter(c, buf):
            a = pltpu.make_async_copy(buf, out_hbm.at[idx0_v.at[c]], sem_w.at[0])
            b = pltpu.make_async_copy(buf, out_hbm.at[idx1_v.at[c]], sem_w.at[1])
            a.start()
            b.start()
            a.wait()
            b.wait()

        read(0, buf0, 0).start()

        @pl.loop(0, nch // 2)
        def _(g):
            c = 2 * g
            read(c + 1, buf1, 1).start()
            read(c, buf0, 0).wait()
            scatter(c, buf0)

            @pl.when(c + 2 < nch)
            def _():
                read(c + 2, buf0, 0).start()

            read(c + 1, buf1, 1).wait()
            scatter(c + 1, buf1)

    return pl.kernel(
        body,
        out_type=jax.ShapeDtypeStruct((n_rows, width), xp.dtype),
        mesh=_sc_mesh(),
        scratch_types=[
            pltpu.VMEM((nch, chunk), I32),
            pltpu.VMEM((nch, chunk), I32),
            pltpu.VMEM((chunk, width), xp.dtype),
            pltpu.VMEM((chunk, width), xp.dtype),
            pltpu.SemaphoreType.DMA((2,)),
            pltpu.SemaphoreType.DMA((2,)),
        ],
        compiler_params=_sc_params(),
        name="sc_dispatch",
    )(xp, dest2d)


def _sc_combine(ys, dest2d):
    n_idx_rows, chunk = dest2d.shape
    width = ys.shape[1]
    nch = n_idx_rows // SC_WORKERS
    assert n_idx_rows % (SC_WORKERS * 2) == 0

    def body(y_hbm, dest_hbm, out_hbm, idx_v, buf0, buf1, sem_g):
        wid = _worker_id()
        base = wid * nch * chunk
        pltpu.sync_copy(dest_hbm.at[pl.ds(wid * nch, nch)], idx_v)

        def gather(c, buf, k):
            return pltpu.make_async_copy(y_hbm.at[idx_v.at[c]], buf, sem_g.at[k])

        def write(c, buf):
            pltpu.sync_copy(buf, out_hbm.at[pl.ds(base + c * chunk, chunk)])

        gather(0, buf0, 0).start()

        @pl.loop(0, nch // 2)
        def _(g):
            c = 2 * g
            gather(c + 1, buf1, 1).start()
            gather(c, buf0, 0).wait()
            write(c, buf0)

            @pl.when(c + 2 < nch)
            def _():
                gather(c + 2, buf0, 0).start()

            gather(c + 1, buf1, 1).wait()
            write(c + 1, buf1)

    return pl.kernel(
        body,
        out_type=jax.ShapeDtypeStruct((n_idx_rows * chunk, width), ys.dtype),
        mesh=_sc_mesh(),
        scratch_types=[
            pltpu.VMEM((nch, chunk), I32),
            pltpu.VMEM((chunk, width), ys.dtype),
            pltpu.VMEM((chunk, width), ys.dtype),
            pltpu.SemaphoreType.DMA((2,)),
        ],
        compiler_params=_sc_params(),
        name="sc_combine",
    )(ys, dest2d)


def _pack_bf16_pairs(v):
    half = v.shape[1] // 2
    lo = pltpu.bitcast(v[:, :half].astype(BF16).astype(F32), jnp.uint32) >> 16
    hi = pltpu.bitcast(v[:, half:].astype(BF16).astype(F32), jnp.uint32) & jnp.uint32(0xFFFF0000)
    return pltpu.bitcast(hi | lo, I32)


def _unpack_bf16_pairs(w):
    w = pltpu.bitcast(w, jnp.uint32)
    lo = pltpu.bitcast(w << 16, F32)
    hi = pltpu.bitcast(w & jnp.uint32(0xFFFF0000), F32)
    return jnp.concatenate([lo, hi], axis=1)


def _ffn_kernel(be_ref, nv_ref, nu_ref, nxt_ref, xs_ref, wg_hbm, wu_hbm, wd_hbm, o_ref,
                wg_f, wu_f, wd_f, wg_s, wu_s, wd_s, sem):
    i = pl.program_id(0)

    def fetch(e):
        return (pltpu.make_async_copy(wg_hbm.at[e], wg_f, sem.at[0]),
                pltpu.make_async_copy(wu_hbm.at[e], wu_f, sem.at[1]),
                pltpu.make_async_copy(wd_hbm.at[e], wd_f, sem.at[2]))

    @pl.when(nxt_ref[i] >= 0)
    def _():
        @pl.when(i == 0)
        def _():
            for cp in fetch(be_ref[0]):
                cp.start()

        for cp in fetch(be_ref[i]):
            cp.wait()
        wg_s[...] = wg_f[...].astype(BF16)
        wu_s[...] = wu_f[...].astype(BF16)
        wd_s[...] = wd_f[...].astype(BF16)

        @pl.when(nxt_ref[i] < N_EXPERTS)
        def _():
            for cp in fetch(nxt_ref[i]):
                cp.start()

    @pl.when(i < nu_ref[0])
    def _():
        live = lax.broadcasted_iota(I32, xs_ref.shape, 0) < nv_ref[i]
        xb = _unpack_bf16_pairs(jnp.where(live, xs_ref[...], 0)).astype(BF16)
        act = jax.nn.silu(_dot(xb, wg_s[...])) * _dot(xb, wu_s[...])
        o_ref[...] = _pack_bf16_pairs(_dot(act.astype(BF16), wd_s[...]))


def _expert_ffn(xs, blk_expert, blk_valid, n_used, blk_next, w_gate, w_up, w_down):
    n_rows, half = xs.shape
    d = 2 * half
    de = w_gate.shape[2]
    bm = EXPERT_ROWS
    rows = lambda i, be, nv, nu, nx: (jnp.minimum(i, nu[0] - 1), 0)
    hbm = pl.BlockSpec(memory_space=pl.ANY)
    return pl.pallas_call(
        _ffn_kernel,
        grid_spec=pltpu.PrefetchScalarGridSpec(
            num_scalar_prefetch=4,
            grid=(n_rows // bm,),
            in_specs=[pl.BlockSpec((bm, half), rows), hbm, hbm, hbm],
            out_specs=pl.BlockSpec((bm, half), rows),
            scratch_shapes=[
                pltpu.VMEM((d, de), F32), pltpu.VMEM((d, de), F32), pltpu.VMEM((de, d), F32),
                pltpu.VMEM((d, de), BF16), pltpu.VMEM((d, de), BF16), pltpu.VMEM((de, d), BF16),
                pltpu.SemaphoreType.DMA((3,)),
            ],
        ),
        out_shape=jax.ShapeDtypeStruct((n_rows, half), I32),
        compiler_params=_tc_params(1),
        name="expert_ffn",
    )(blk_expert, blk_valid, n_used, blk_next, xs, w_gate, w_up, w_down)


def _combine_kernel(x_ref, y0_ref, y1_ref, rt_ref, g_ref, b_ref, o_ref):
    wt = pltpu.bitcast(rt_ref[...], F32).T
    y = wt[:, 4:5] * _unpack_bf16_pairs(y0_ref[...]) + wt[:, 5:6] * _unpack_bf16_pairs(y1_ref[...])
    o_ref[...] = _layer_norm(ALPHA * x_ref[...] + y, g_ref[...], b_ref[...])


def _combine_kernel_into(x_ref, y0_ref, y1_ref, rt_ref, g_ref, b_ref, full_ref, o_ref):
    del full_ref
    _combine_kernel(x_ref, y0_ref, y1_ref, rt_ref, g_ref, b_ref, o_ref)


def _combine_ln(x2, y01, rt, g3, b3, into=None, row0=0, full_rows=None):
    t_all, d = x2.shape
    tq = TOKEN_TILE
    nt = t_all // tq
    blk0 = row0 // tq if full_rows else 0
    in_specs = [
        pl.BlockSpec((tq, d), lambda i: (i, 0)),
        pl.BlockSpec((tq, d // 2), lambda i: (i, 0)),
        pl.BlockSpec((tq, d // 2), lambda i: (i + nt, 0)),
        pl.BlockSpec((8, tq), lambda i: (0, i)),
        _const_spec((1, d)), _const_spec((1, d)),
    ]
    args = [x2, y01, y01, rt, g3, b3]
    body, aliases = _combine_kernel, {}
    if into is not None:
        in_specs.append(pl.BlockSpec(memory_space=pl.ANY))
        args.append(into)
        body, aliases = _combine_kernel_into, {len(args) - 1: 0}
    return pl.pallas_call(
        body,
        grid=(nt,),
        in_specs=in_specs,
        out_specs=pl.BlockSpec((tq, d), lambda i: (i + blk0, 0)),
        out_shape=jax.ShapeDtypeStruct((full_rows or t_all, d), F32),
        input_output_aliases=aliases,
        compiler_params=_tc_params(1),
        name="combine_ln",
    )(*args)


def _routing_tables(rt, counts):
    bm = EXPERT_ROWS
    t_all = rt.shape[1]
    n_rows = 2 * t_all + N_EXPERTS * bm
    cnt = counts[:, 0]
    padded = (cnt + bm - 1) // bm * bm
    ends = jnp.cumsum(padded)
    offs = ends - padded
    experts = rt[0:2]
    off_tok = jnp.sum(jnp.where(experts[None] == jnp.arange(N_EXPERTS, dtype=I32)[:, None, None],
                                offs[:, None, None], 0), axis=0)
    dest2d = (off_tok + rt[2:4]).reshape(-1, SC_CHUNK).astype(I32)
    blk_start = jnp.arange(n_rows // bm, dtype=I32) * bm
    blk_expert = jnp.minimum(jnp.sum(blk_start[:, None] >= ends[None, :], axis=1), N_EXPERTS - 1).astype(I32)
    live_end = jnp.sum(jnp.where(blk_expert[:, None] == jnp.arange(N_EXPERTS, dtype=I32)[None, :],
                                 (offs + cnt)[None, :], 0), axis=1)
    blk_valid = jnp.clip(live_end - blk_start, 0, bm).astype(I32)
    n_used = (ends[-1:] // bm).astype(I32)
    eid = jnp.arange(N_EXPERTS, dtype=I32)
    later_present = jnp.logical_and(eid[None, :] > eid[:, None], (cnt > 0)[None, :])
    next_present = jnp.min(jnp.where(later_present, eid[None, :], N_EXPERTS), axis=1)
    next_of_blk = jnp.sum(jnp.where(blk_expert[:, None] == eid[None, :], next_present[None, :], 0), axis=1)
    prev_expert = jnp.concatenate([jnp.full((1,), -1, I32), blk_expert[:-1]])
    is_first = jnp.logical_and(blk_start < ends[-1], blk_expert != prev_expert)
    blk_next = jnp.where(is_first, next_of_blk, -1).astype(I32)
    return dest2d, blk_expert, blk_valid, n_used, blk_next, n_rows


def _layer_tail(x1, kv, mem_len, p, router, b0, bsz, s_len, into=None, full_rows=None):
    rw_hi, rw_lo, rbias = router
    x2, xp, rt, counts = _xattn_route(x1, kv, mem_len, p["xq"], p["xo"], p["ln2_g"], p["ln2_b"],
                                      rw_hi, rw_lo, rbias, b0, bsz, s_len)
    dest2d, blk_expert, blk_valid, n_used, blk_next, n_rows = _routing_tables(rt, counts)
    xs = _sc_dispatch(xp, dest2d, n_rows)
    ys = _expert_ffn(xs, blk_expert, blk_valid, n_used, blk_next, p["e_gate"], p["e_up"], p["e_down"])
    y01 = _sc_combine(ys, dest2d)
    return _combine_ln(x2, y01, rt, p["ln3_g"], p["ln3_b"], into=into, row0=b0 * s_len, full_rows=full_rows)


def _row(v):
    return v.reshape(1, -1).astype(F32)


def _common_params(xq, xkv, xo, ln2_g, ln2_b, e_gate, e_up, e_down, ln3_g, ln3_b):
    return dict(xq=xq, xkv=xkv, xo=xo, ln2_g=_row(ln2_g), ln2_b=_row(ln2_b),
                e_gate=e_gate, e_up=e_up, e_down=e_down, ln3_g=_row(ln3_g), ln3_b=_row(ln3_b))


def kernel(x, mem, positions, router_w, router_bias, l0_w_in, l0_sinks, l0_sgu_ln_g, l0_sgu_ln_b, l0_sgu_w, l0_sgu_b, l0_w_out, l0_ln1_g, l0_ln1_b, l0_xq, l0_xkv, l0_xo, l0_ln2_g, l0_ln2_b, l0_e_gate, l0_e_up, l0_e_down, l0_ln3_g, l0_ln3_b, l1_w_in, l1_pool_w, l1_pool_scale, l1_w_out, l1_ln1_g, l1_ln1_b, l1_xq, l1_xkv, l1_xo, l1_ln2_g, l1_ln2_b, l1_e_gate, l1_e_up, l1_e_down, l1_ln3_g, l1_ln3_b):
    bsz, s_len, d = x.shape
    assert s_len % TOKEN_TILE == 0 and TOKEN_TILE % BLOCK == 0
    xt = x.reshape(bsz * s_len, d)
    mem2d = mem.reshape(-1, d)

    rw_t = router_w.T.astype(F32)
    rw_hi = rw_t.astype(BF16)
    rw_lo = (rw_t - rw_hi.astype(F32)).astype(BF16)
    router = (rw_hi, rw_lo, router_bias.reshape(-1, 1).astype(F32))

    half = ROPE_DIM // 2
    inv_freq = (ROPE_THETA ** (-(jnp.arange(half, dtype=F32) * 2.0 / ROPE_DIM))).reshape(half, 1)
    etab_np = np.zeros((128, 3 * 128), np.float32)
    cbase_np = np.ones((1, 128), np.float32)
    for ln in range(128):
        dd = ln % HEAD_DIM
        if dd < ROPE_DIM:
            cbase_np[0, ln] = 0.0
            etab_np[[dd % half, half + dd % half], ln] = 1.0
            if dd >= half:
                etab_np[[2 * half + dd - half, 3 * half + dd - half], 128 + ln] = 1.0
            else:
                etab_np[[2 * half + dd, 3 * half + dd], 256 + ln] = -1.0
    etab = jnp.asarray(etab_np, BF16)
    cbase = jnp.asarray(cbase_np)
    pos_row = positions.reshape(1, -1).astype(I32)
    grp = jnp.arange(B_WIDTH) // B_GROUP_DIM
    gsum = (grp[:, None] == grp[None, :]).astype(BF16)
    bs_full = jnp.repeat(l0_sgu_b.T.astype(F32), B_GROUP_DIM, axis=1)

    p0 = _common_params(l0_xq, l0_xkv, l0_xo, l0_ln2_g, l0_ln2_b, l0_e_gate, l0_e_up, l0_e_down,
                        l0_ln3_g, l0_ln3_b)
    p1 = _common_params(l1_xq, l1_xkv, l1_xo, l1_ln2_g, l1_ln2_b, l1_e_gate, l1_e_up, l1_e_down,
                        l1_ln3_g, l1_ln3_b)
    kv0 = _kv_proj(mem2d, p0["xkv"], bsz)
    kv1 = _kv_proj(mem2d, p1["xkv"], bsz)
    mem_len = mem.shape[1]

    n_split = BATCH_SPLIT if bsz % BATCH_SPLIT == 0 else 1
    nb = bsz // n_split
    out = None
    for part in range(n_split):
        b0 = part * nb
        x1 = _mixer0(xt, pos_row, l0_sinks.astype(F32), l0_w_in, inv_freq, etab, cbase, gsum,
                     _row(l0_sgu_ln_g), _row(l0_sgu_ln_b), l0_sgu_w.astype(F32), bs_full,
                     l0_w_out, _row(l0_ln1_g), _row(l0_ln1_b), b0, nb, s_len)
        x3 = _layer_tail(x1, kv0, mem_len, p0, router, b0, nb, s_len)
        x1 = _mixer1(x3, l1_w_in, l1_pool_w, _row(l1_pool_scale),
                     l1_w_out, _row(l1_ln1_g), _row(l1_ln1_b), nb, s_len)
        out = _layer_tail(x1, kv1, mem_len, p1, router, b0, nb, s_len, into=out, full_rows=bsz * s_len)
    return out.reshape(bsz, s_len, d)
```

```python
import functools

import numpy as np
import jax
import jax.numpy as jnp
from jax import lax
from jax.experimental import pallas as pl
from jax.experimental.pallas import tpu as pltpu
from jax.experimental.pallas import tpu_sc as plsc

F32 = jnp.float32
BF16 = jnp.bfloat16
I32 = jnp.int32

DEPTH = 2
ALPHA = (2.0 * DEPTH) ** 0.25
LN_EPS = 1e-5

HEAD_DIM = 64
A_Q_HEADS = 8
A_KV_HEADS = 2
A_GROUP = A_Q_HEADS // A_KV_HEADS
BLOCK = 128
ROPE_THETA = 500000.0
ROPE_DIM = HEAD_DIM // 4
A_WIDTH = A_Q_HEADS * HEAD_DIM
KV_WIDTH = A_KV_HEADS * HEAD_DIM
B_GROUPS = 8
B_GROUP_DIM = 64
B_WIDTH = B_GROUPS * B_GROUP_DIM
POOL_WINDOWS = (2, 4, 8, 16)
POOL_HALO = 16
X_HEADS = 4
N_EXPERTS = 16
N_EXPERT_GROUPS = 4
EXPERTS_PER_GROUP = 4

TOKEN_TILE = 512
EXPERT_ROWS = 512
BATCH_SPLIT = 2
SC_WORKERS = 32
SC_CHUNK = 64
VMEM_LIMIT = 56 * 1024 * 1024
NEG_BIG = -1e30


def _layer_norm(z, g, b):
    mu = jnp.mean(z, axis=-1, keepdims=True)
    d = z - mu
    var = jnp.mean(d * d, axis=-1, keepdims=True)
    return d * lax.rsqrt(var + LN_EPS) * g + b


def _dot(a, b):
    return jnp.dot(a, b, preferred_element_type=F32)


def _dot_nt(a, b):
    return lax.dot_general(a, b, (((1,), (1,)), ((), ())), preferred_element_type=F32)


def _split_bf16(v):
    hi = v.astype(BF16)
    lo = (v - hi.astype(F32)).astype(BF16)
    return hi, lo


def _tc_params(n_axes):
    return pltpu.CompilerParams(dimension_semantics=("arbitrary",) * n_axes,
                                vmem_limit_bytes=VMEM_LIMIT)


def _const_spec(shape):
    nd = len(shape)
    return pl.BlockSpec(shape, lambda *_: (0,) * nd, pipeline_mode=pl.Buffered(1))

def _mixer0_kernel(sinks_ref, x_ref, pos_ref, win_ref, invf_ref, etab_ref, cbase_ref, gsum_ref,
                   lng_ref, lnb_ref, ws_ref, bs_ref, wout_ref, g1_ref, b1_ref,
                   o_ref, q_s, kv_s, u_s, vn_s, mix_s, wt_s, win_s, wout_s):
    b = pl.program_id(0)
    j = pl.program_id(1)
    tq = x_ref.shape[0]
    nblk = tq // BLOCK
    kvw = kv_s.shape[1]

    @pl.when(jnp.logical_and(b == 0, j == 0))
    def _():
        win_s[...] = win_ref[...].astype(BF16)
        wout_s[...] = wout_ref[...].astype(BF16)
        r = lax.broadcasted_iota(I32, (BLOCK, BLOCK), 0)
        c = lax.broadcasted_iota(I32, (BLOCK, BLOCK), 1)
        for g in range(B_GROUPS):
            wt_s[g] = jnp.where(c <= r, ws_ref[g], 0.0).astype(BF16)

    @pl.when(j == 0)
    def _():
        kv_s[0:BLOCK, :] = jnp.zeros((BLOCK, kvw), BF16)

    x = x_ref[...]
    h = _dot(x.astype(BF16), win_s[...])

    ang = invf_ref[...] * pos_ref[...].astype(F32)
    c8 = jnp.cos(ang)
    s8 = jnp.sin(ang)
    c8h = c8.astype(BF16).astype(F32)
    s8h = s8.astype(BF16).astype(F32)
    stack = jnp.concatenate([c8h, c8 - c8h, s8h, s8 - s8h, jnp.zeros((128 - 4 * 8, tq), F32)], axis=0)
    tabs = _dot(stack.T.astype(BF16), etab_ref[...])
    cs = tabs[:, 0:128] + cbase_ref[...]
    sa = tabs[:, 128:256]
    sb = tabs[:, 256:384]

    def rope(t):
        return t * cs + pltpu.roll(t, ROPE_DIM // 2, 1) * sa + pltpu.roll(t, 128 - ROPE_DIM // 2, 1) * sb

    for c in range(A_WIDTH // 128):
        t = h[:, c * 128:(c + 1) * 128] * (HEAD_DIM ** -0.5)
        q_s[:, c * 128:(c + 1) * 128] = rope(t).astype(BF16)
    c1 = A_WIDTH
    c2 = c1 + KV_WIDTH
    c3 = c2 + KV_WIDTH
    c4 = c3 + B_WIDTH
    low = lax.broadcasted_iota(I32, (tq, 128), 1) < HEAD_DIM
    kr = rope(h[:, c1:c2])
    kx = pltpu.roll(kr, HEAD_DIM, 1)
    vr = h[:, c2:c3]
    vx = pltpu.roll(vr, HEAD_DIM, 1)
    kv_cols = [jnp.where(low, kr, kx), jnp.where(low, kx, kr),
               jnp.where(low, vr, 0.0), jnp.where(low, 0.0, vx),
               jnp.where(low, vx, 0.0), jnp.where(low, 0.0, vr)]
    for c, col in enumerate(kv_cols):
        kv_s[BLOCK:, c * 128:(c + 1) * 128] = col.astype(BF16)

    u_s[...] = jax.nn.gelu(h[:, c3:c4])
    v = jax.nn.gelu(h[:, c4:])
    gsum = gsum_ref[...]
    mean = _dot(v.astype(BF16), gsum) * (1.0 / B_GROUP_DIM)
    d = v - mean
    var = _dot((d * d).astype(BF16), gsum) * (1.0 / B_GROUP_DIM)
    vn_s[...] = (d * lax.rsqrt(var + LN_EPS) * lng_ref[...] + lnb_ref[...]).astype(BF16)

    qi = lax.broadcasted_iota(I32, (BLOCK, 2 * BLOCK), 0)
    kj = lax.broadcasted_iota(I32, (BLOCK, 2 * BLOCK), 1)
    rel = qi + BLOCK - kj
    band = jnp.logical_and(rel >= 0, rel < BLOCK)
    low_q = lax.broadcasted_iota(I32, (BLOCK, 128), 1) < HEAD_DIM
    low_k = lax.broadcasted_iota(I32, (2 * BLOCK, 128), 1) < HEAD_DIM
    ones_lo = jnp.where(low_k, 1.0, 0.0).astype(BF16)
    ones_hi = jnp.where(low_k, 0.0, 1.0).astype(BF16)
    zero_q = jnp.zeros((BLOCK, 128), BF16)

    def block_body(n, carry):
        r0 = pl.multiple_of(n * BLOCK, BLOCK)
        kv = kv_s[pl.ds(r0, 2 * BLOCK), :]
        qb = q_s[pl.ds(r0, BLOCK), :]
        first = jnp.logical_and(j == 0, n == 0)
        valid = jnp.logical_and(band, kj >= jnp.where(first, BLOCK, 0))
        for c in range(A_WIDTH // 128):
            hk = (2 * c) // A_GROUP
            qp = qb[:, c * 128:(c + 1) * 128]
            kd = kv[:, hk * 128:(hk + 1) * 128]
            v_lo = jnp.concatenate([kv[:, (2 + 2 * hk) * 128:(3 + 2 * hk) * 128], ones_lo], axis=1)
            v_hi = jnp.concatenate([kv[:, (3 + 2 * hk) * 128:(4 + 2 * hk) * 128], ones_hi], axis=1)
            res = None
            esink = []
            for half, (qm, vm) in enumerate(((jnp.where(low_q, qp, zero_q), v_lo),
                                             (jnp.where(low_q, zero_q, qp), v_hi))):
                s = jnp.where(valid, _dot_nt(qm, kd), NEG_BIG)
                sink = sinks_ref[2 * c + half]
                m = jnp.maximum(jnp.max(s, axis=-1, keepdims=True), sink)
                pv = _dot(jnp.exp(s - m).astype(BF16), vm)
                res = pv if res is None else res + pv
                esink.append(jnp.exp(sink - m))
            den = res[:, 128:] + jnp.where(low_q, esink[0], esink[1])
            mix_s[pl.ds(r0, BLOCK), c * 128:(c + 1) * 128] = (res[:, :128] / den).astype(BF16)
        vnb = vn_s[pl.ds(r0, BLOCK), :]
        parts = []
        for c in range(B_WIDTH // 128):
            vp = vnb[:, c * 128:(c + 1) * 128]
            parts.append(_dot(wt_s[2 * c], jnp.where(low_q, vp, zero_q))
                         + _dot(wt_s[2 * c + 1], jnp.where(low_q, zero_q, vp)))
        mixed = jnp.concatenate(parts, axis=1) + bs_ref[...]
        mix_s[pl.ds(r0, BLOCK), A_WIDTH:] = (u_s[pl.ds(r0, BLOCK), :] * mixed).astype(BF16)
        return carry

    lax.fori_loop(0, nblk, block_body, 0)
    kv_s[0:BLOCK, :] = kv_s[tq:tq + BLOCK, :]

    z = ALPHA * x + _dot(mix_s[...], wout_s[...])
    o_ref[...] = _layer_norm(z, g1_ref[...], b1_ref[...])


def _mixer0(x, pos_row, sinks, w_in, invf, etab, cbase, gsum, lng, lnb, w_s, bs_full, w_out, g1, b1,
            b0, bsz, s_len):
    d = x.shape[1]
    t_all = bsz * s_len
    tq = TOKEN_TILE
    nj = s_len // tq
    row = lambda bb, jj: (bb * nj + jj, 0)
    in_w = w_in.shape[1]
    return pl.pallas_call(
        _mixer0_kernel,
        grid=(bsz, nj),
        in_specs=[
            pl.BlockSpec(memory_space=pltpu.SMEM),
            pl.BlockSpec((tq, d), lambda bb, jj: ((b0 + bb) * nj + jj, 0)),
            pl.BlockSpec((1, tq), lambda bb, jj: (0, (b0 + bb) * nj + jj)),
            _const_spec((d, in_w)),
            _const_spec((ROPE_DIM // 2, 1)), _const_spec((128, 3 * 128)), _const_spec((1, 128)),
            _const_spec((B_WIDTH, B_WIDTH)),
            _const_spec((1, B_WIDTH)), _const_spec((1, B_WIDTH)),
            _const_spec((B_GROUPS, BLOCK, BLOCK)),
            _const_spec((BLOCK, B_WIDTH)),
            _const_spec((A_WIDTH + B_WIDTH, d)),
            _const_spec((1, d)), _const_spec((1, d)),
        ],
        out_specs=pl.BlockSpec((tq, d), row),
        out_shape=jax.ShapeDtypeStruct((t_all, d), F32),
        scratch_shapes=[
            pltpu.VMEM((tq, A_WIDTH), BF16),
            pltpu.VMEM((tq + BLOCK, 6 * 128), BF16),
            pltpu.VMEM((tq, B_WIDTH), F32),
            pltpu.VMEM((tq, B_WIDTH), BF16),
            pltpu.VMEM((tq, A_WIDTH + B_WIDTH), BF16),
            pltpu.VMEM((B_GROUPS, BLOCK, BLOCK), BF16),
            pltpu.VMEM((d, in_w), BF16),
            pltpu.VMEM((A_WIDTH + B_WIDTH, d), BF16),
        ],
        compiler_params=_tc_params(2),
        name="mixer0",
    )(sinks, x, pos_row, w_in, invf, etab, cbase, gsum, lng, lnb, w_s, bs_full, w_out, g1, b1)


def _expert_combine_ln(x2, y0_packed, y1_packed, rt, g, b):
    wt = pltpu.bitcast(rt, F32).T
    y = wt[:, 4:5] * _unpack_bf16_pairs(y0_packed) + wt[:, 5:6] * _unpack_bf16_pairs(y1_packed)
    return _layer_norm(ALPHA * x2 + y, g, b)


def _mixer1_kernel(x2_ref, y0_ref, y1_ref, rt_ref, g3_ref, b3_ref,
                   win_ref, pw_ref, ps_ref, wout_ref, g1_ref, b1_ref, o_ref,
                   h_s, mp_s, win_s, pw_s, wout_s):
    j = pl.program_id(1)
    tq = x2_ref.shape[0]
    gw = x2_ref.shape[1] // len(POOL_WINDOWS)

    @pl.when(jnp.logical_and(pl.program_id(0) == 0, j == 0))
    def _():
        win_s[...] = win_ref[...].astype(BF16)
        pw_s[...] = pw_ref[...].astype(BF16)
        wout_s[...] = wout_ref[...].astype(BF16)

    @pl.when(j == 0)
    def _():
        h_s[0:POOL_HALO, :] = jnp.zeros((POOL_HALO, h_s.shape[1]), F32)

    x = _expert_combine_ln(x2_ref[...], y0_ref[...], y1_ref[...], rt_ref[...], g3_ref[...], b3_ref[...])
    h_s[POOL_HALO:, :] = _dot(x.astype(BF16), win_s[...])
    t_pos = j * tq + lax.broadcasted_iota(I32, (tq, 1), 0)
    for g, win in enumerate(POOL_WINDOWS):
        lo, hi = g * gw, (g + 1) * gw
        cur = h_s[POOL_HALO:, lo:hi]
        acc = cur
        for k in range(1, win):
            acc = acc + h_s[POOL_HALO - k:POOL_HALO - k + tq, lo:hi]
        count = jnp.minimum(t_pos + 1, win).astype(F32)
        pooled = acc / count - cur
        mapped = _dot(pooled.astype(BF16), pw_s[g])
        mp_s[:, lo:hi] = (mapped * ps_ref[:, lo:hi]).astype(BF16)
    h_s[0:POOL_HALO, :] = h_s[tq:tq + POOL_HALO, :]
    z = ALPHA * x + _dot(mp_s[...], wout_s[...])
    o_ref[...] = _layer_norm(z, g1_ref[...], b1_ref[...])


def _mixer1(x2, y01, rt, g3, b3, w_in, pool_w, pool_scale, w_out, g1, b1, bsz, s_len):
    t_all, d = x2.shape
    tq = TOKEN_TILE
    nj = s_len // tq
    nt = bsz * nj
    row = lambda bb, jj: (bb * nj + jj, 0)
    ng = len(POOL_WINDOWS)
    return pl.pallas_call(
        _mixer1_kernel,
        grid=(bsz, nj),
        in_specs=[
            pl.BlockSpec((tq, d), row),
            pl.BlockSpec((tq, d // 2), row),
            pl.BlockSpec((tq, d // 2), lambda bb, jj: (nt + bb * nj + jj, 0)),
            pl.BlockSpec((8, tq), lambda bb, jj: (0, bb * nj + jj)),
            _const_spec((1, d)), _const_spec((1, d)),
            _const_spec((d, d)),
            _const_spec((ng, d // ng, d // ng)),
            _const_spec((1, d)),
            _const_spec((d, d)),
            _const_spec((1, d)), _const_spec((1, d)),
        ],
        out_specs=pl.BlockSpec((tq, d), row),
        out_shape=jax.ShapeDtypeStruct((t_all, d), F32),
        scratch_shapes=[pltpu.VMEM((tq + POOL_HALO, d), F32), pltpu.VMEM((tq, d), BF16),
                        pltpu.VMEM((d, d), BF16), pltpu.VMEM((ng, d // ng, d // ng), BF16),
                        pltpu.VMEM((d, d), BF16)],
        compiler_params=_tc_params(2),
        name="mixer1",
    )(x2, y01, y01, rt, g3, b3, w_in, pool_w, pool_scale, w_out, g1, b1)


def _kv_kernel(mem_ref, w_ref, o_ref, w_s):
    @pl.when(pl.program_id(0) == 0)
    def _():
        w_s[...] = w_ref[...].astype(BF16)

    o_ref[...] = _dot(mem_ref[...].astype(BF16), w_s[...]).astype(BF16)


def _kv_proj(mem2d, wkv, bsz):
    rows, d = mem2d.shape
    m = rows // bsz
    return pl.pallas_call(
        _kv_kernel,
        grid=(bsz,),
        in_specs=[pl.BlockSpec((m, d), lambda i: (i, 0)), _const_spec(wkv.shape)],
        out_specs=pl.BlockSpec((m, wkv.shape[1]), lambda i: (i, 0)),
        out_shape=jax.ShapeDtypeStruct((rows, wkv.shape[1]), BF16),
        scratch_shapes=[pltpu.VMEM(wkv.shape, BF16)],
        compiler_params=_tc_params(1),
        name="kv_proj",
    )(mem2d, wkv)


def _top2_of4(v):
    hi01, lo01 = jnp.maximum(v[0], v[1]), jnp.minimum(v[0], v[1])
    hi23, lo23 = jnp.maximum(v[2], v[3]), jnp.minimum(v[2], v[3])
    return jnp.maximum(hi01, hi23) + jnp.maximum(jnp.minimum(hi01, hi23), jnp.maximum(lo01, lo23))


def _argmax_first(vals):
    best, idx = vals[0], jnp.zeros(vals[0].shape, I32)
    for i in range(1, len(vals)):
        better = vals[i] > best
        best = jnp.where(better, vals[i], best)
        idx = jnp.where(better, i, idx)
    return best, idx


def _xattn_kernel(x_ref, kv_ref, wq_ref, wo_ref, g2_ref, b2_ref, rwh_ref, rwl_ref, rb_ref,
                  x2_ref, xp_ref, rt_ref, cnt_ref, run_s, wq_s, wo_s):
    first = jnp.logical_and(pl.program_id(0) == 0, pl.program_id(1) == 0)
    tq, d = x_ref.shape
    hd = d // X_HEADS

    @pl.when(first)
    def _():
        run_s[...] = jnp.zeros(run_s.shape, F32)
        wq_s[...] = wq_ref[...].astype(BF16)
        wo_s[...] = wo_ref[...].astype(BF16)

    x = x_ref[...]
    q = _dot(x.astype(BF16), wq_s[...]) * (hd ** -0.5)
    outs = []
    for hx in range(X_HEADS):
        qh = q[:, hx * hd:(hx + 1) * hd].astype(BF16)
        kh = kv_ref[:, hx * hd:(hx + 1) * hd]
        vh = kv_ref[:, d + hx * hd:d + (hx + 1) * hd]
        s = _dot_nt(qh, kh)
        p = jnp.exp(s - jnp.max(s, axis=-1, keepdims=True))
        o = _dot(p.astype(BF16), vh) / jnp.sum(p, axis=-1, keepdims=True)
        outs.append(o.astype(BF16))
    att = _dot(jnp.concatenate(outs, axis=1), wo_s[...])
    x2 = _layer_norm(ALPHA * x + att, g2_ref[...], b2_ref[...])
    x2_ref[...] = x2

    xp_ref[...] = _pack_bf16_pairs(x2)

    xh, xl = _split_bf16(x2)
    logits = _dot_nt(rwh_ref[...], xh) + _dot_nt(rwh_ref[...], xl) + _dot_nt(rwl_ref[...], xh)
    e_max = jnp.max(logits, axis=0, keepdims=True)
    ex = jnp.exp(logits - e_max)
    scores = ex / jnp.sum(ex, axis=0, keepdims=True)
    biased = scores + rb_ref[...]
    sc = [scores[e:e + 1, :] for e in range(N_EXPERTS)]
    bi = [biased[e:e + 1, :] for e in range(N_EXPERTS)]
    epg = EXPERTS_PER_GROUP
    gscore = [_top2_of4(bi[g * epg:(g + 1) * epg]) for g in range(N_EXPERT_GROUPS)]
    _, gsel = _argmax_first(gscore)

    def pick(vals):
        return [functools.reduce(lambda acc, g: jnp.where(gsel == g, vals[g * epg + i], acc),
                                 range(1, N_EXPERT_GROUPS), vals[i]) for i in range(epg)]

    in_b = pick(bi)
    in_s = pick(sc)
    _, i0 = _argmax_first(in_b)
    _, i1 = _argmax_first([jnp.where(i0 == i, -jnp.inf, in_b[i]) for i in range(epg)])

    def take(vals, idx):
        return functools.reduce(lambda acc, i: jnp.where(idx == i, vals[i], acc), range(1, epg), vals[0])

    s0, s1 = take(in_s, i0), take(in_s, i1)
    w0, w1 = s0 / (s0 + s1), s1 / (s0 + s1)
    e0, e1 = gsel * epg + i0, gsel * epg + i1

    eid = lax.broadcasted_iota(I32, (N_EXPERTS, tq), 0)
    oh0 = eid == e0
    oh1 = eid == e1
    onehot = jnp.where(jnp.logical_or(oh0, oh1), 1.0, 0.0)
    rr = lax.broadcasted_iota(I32, (tq, tq), 0)
    cc = lax.broadcasted_iota(I32, (tq, tq), 1)
    upper = jnp.where(rr < cc, 1.0, 0.0).astype(BF16)
    prefix = _dot(onehot.astype(BF16), upper) + run_s[...]
    r0 = jnp.sum(jnp.where(oh0, prefix, 0.0), axis=0, keepdims=True).astype(I32)
    r1 = jnp.sum(jnp.where(oh1, prefix, 0.0), axis=0, keepdims=True).astype(I32)
    run_s[...] = run_s[...] + jnp.sum(onehot, axis=1, keepdims=True)
    cnt_ref[...] = jnp.broadcast_to(run_s[...], cnt_ref.shape).astype(I32)

    zero = jnp.zeros((1, tq), I32)
    rt_ref[...] = jnp.concatenate(
        [e0, e1, r0, r1, pltpu.bitcast(w0, I32), pltpu.bitcast(w1, I32), zero, zero], axis=0)


def _xattn_route(x1, kv, mem_len, wq, wo, g2, b2, rw_hi, rw_lo, rbias, b0, bsz, s_len):
    t_all, d = x1.shape
    tq = TOKEN_TILE
    nj = s_len // tq
    m = mem_len
    row = lambda bb, jj: (bb * nj + jj, 0)
    return pl.pallas_call(
        _xattn_kernel,
        grid=(bsz, nj),
        in_specs=[
            pl.BlockSpec((tq, d), row),
            pl.BlockSpec((m, 2 * d), lambda bb, jj: (b0 + bb, 0)),
            _const_spec((d, d)), _const_spec((d, d)),
            _const_spec((1, d)), _const_spec((1, d)),
            _const_spec((N_EXPERTS, d)), _const_spec((N_EXPERTS, d)),
            _const_spec((N_EXPERTS, 1)),
        ],
        out_specs=[
            pl.BlockSpec((tq, d), row),
            pl.BlockSpec((tq, d // 2), row),
            pl.BlockSpec((8, tq), lambda bb, jj: (0, bb * nj + jj)),
            _const_spec((N_EXPERTS, 128)),
        ],
        out_shape=[
            jax.ShapeDtypeStruct((t_all, d), F32),
            jax.ShapeDtypeStruct((t_all, d // 2), I32),
            jax.ShapeDtypeStruct((8, t_all), I32),
            jax.ShapeDtypeStruct((N_EXPERTS, 128), I32),
        ],
        scratch_shapes=[pltpu.VMEM((N_EXPERTS, 1), F32), pltpu.VMEM((d, d), BF16), pltpu.VMEM((d, d), BF16)],
        compiler_params=_tc_params(2),
        name="xattn_route",
    )(x1, kv, wq, wo, g2, b2, rw_hi, rw_lo, rbias)


def _sc_mesh():
    return plsc.VectorSubcoreMesh(core_axis_name="c", subcore_axis_name="s")


def _sc_params():
    return pltpu.CompilerParams(needs_layout_passes=False)


def _worker_id():
    return lax.axis_index("s") * lax.axis_size("c") + lax.axis_index("c")


def _sc_dispatch(xp, dest2d, n_rows):
    t_all, width = xp.shape
    chunk = dest2d.shape[1]
    tok_w = t_all // SC_WORKERS
    nch = tok_w // chunk
    slot1 = t_all // chunk
    assert t_all % (SC_WORKERS * chunk * 2) == 0

    def body(x_hbm, dest_hbm, out_hbm, idx0_v, idx1_v, buf0, buf1, sem_r, sem_w):
        wid = _worker_id()
        base = wid * tok_w
        pltpu.sync_copy(dest_hbm.at[pl.ds(wid * nch, nch)], idx0_v)
        pltpu.sync_copy(dest_hbm.at[pl.ds(slot1 + wid * nch, nch)], idx1_v)

        def read(c, buf, k):
            return pltpu.make_async_copy(x_hbm.at[pl.ds(base + c * chunk, chunk)], buf, sem_r.at[k])

        def scatter(c, buf):
            a = pltpu.make_async_copy(buf, out_hbm.at[idx0_v.at[c]], sem_w.at[0])
            b = pltpu.make_async_copy(buf, out_hbm.at[idx1_v.at[c]], sem_w.at[1])
            a.start()
            b.start()
            a.wait()
            b.wait()

        read(0, buf0, 0).start()

        @pl.loop(0, nch // 2)
        def _(g):
            c = 2 * g
            read(c + 1, buf1, 1).start()
            read(c, buf0, 0).wait()
            scatter(c, buf0)

            @pl.when(c + 2 < nch)
            def _():
                read(c + 2, buf0, 0).start()

            read(c + 1, buf1, 1).wait()
            scatter(c + 1, buf1)

    return pl.kernel(
        body,
        out_type=jax.ShapeDtypeStruct((n_rows, width), xp.dtype),
        mesh=_sc_mesh(),
        scratch_types=[
            pltpu.VMEM((nch, chunk), I32),
            pltpu.VMEM((nch, chunk), I32),
            pltpu.VMEM((chunk, width), xp.dtype),
            pltpu.VMEM((chunk, width), xp.dtype),
            pltpu.SemaphoreType.DMA((2,)),
            pltpu.SemaphoreType.DMA((2,)),
        ],
        compiler_params=_sc_params(),
        name="sc_dispatch",
    )(xp, dest2d)


def _sc_combine(ys, dest2d):
    n_idx_rows, chunk = dest2d.shape
    width = ys.shape[1]
    nch = n_idx_rows // SC_WORKERS
    assert n_idx_rows % (SC_WORKERS * 2) == 0

    def body(y_hbm, dest_hbm, out_hbm, idx_v, buf0, buf1, sem_g):
        wid = _worker_id()
        base = wid * nch * chunk
        pltpu.sync_copy(dest_hbm.at[pl.ds(wid * nch, nch)], idx_v)

        def gather(c, buf, k):
            return pltpu.make_async_copy(y_hbm.at[idx_v.at[c]], buf, sem_g.at[k])

        def write(c, buf):
            pltpu.sync_copy(buf, out_hbm.at[pl.ds(base + c * chunk, chunk)])

        gather(0, buf0, 0).start()

        @pl.loop(0, nch // 2)
        def _(g):
            c = 2 * g
            gather(c + 1, buf1, 1).start()
            gather(c, buf0, 0).wait()
            write(c, buf0)

            @pl.when(c + 2 < nch)
            def _():
                gather(c + 2, buf0, 0).start()

            gather(c + 1, buf1, 1).wait()
            write(c + 1, buf1)

    return pl.kernel(
        body,
        out_type=jax.ShapeDtypeStruct((n_idx_rows * chunk, width), ys.dtype),
        mesh=_sc_mesh(),
        scratch_types=[
            pltpu.VMEM((nch, chunk), I32),
            pltpu.VMEM((chunk, width), ys.dtype),
            pltpu.VMEM((chunk, width), ys.dtype),
            pltpu.SemaphoreType.DMA((2,)),
        ],
        compiler_params=_sc_params(),
        name="sc_combine",
    )(ys, dest2d)


def _pack_bf16_pairs(v):
    half = v.shape[1] // 2
    lo = pltpu.bitcast(v[:, :half].astype(BF16).astype(F32), jnp.uint32) >> 16
    hi = pltpu.bitcast(v[:, half:].astype(BF16).astype(F32), jnp.uint32) & jnp.uint32(0xFFFF0000)
    return pltpu.bitcast(hi | lo, I32)


def _unpack_bf16_pairs(w):
    w = pltpu.bitcast(w, jnp.uint32)
    lo = pltpu.bitcast(w << 16, F32)
    hi = pltpu.bitcast(w & jnp.uint32(0xFFFF0000), F32)
    return jnp.concatenate([lo, hi], axis=1)


def _ffn_kernel(be_ref, nv_ref, nu_ref, nxt_ref, xs_ref, wg_hbm, wu_hbm, wd_hbm, o_ref,
                wg_f, wu_f, wd_f, wg_s, wu_s, wd_s, sem):
    i = pl.program_id(0)

    def fetch(e):
        return (pltpu.make_async_copy(wg_hbm.at[e], wg_f, sem.at[0]),
                pltpu.make_async_copy(wu_hbm.at[e], wu_f, sem.at[1]),
                pltpu.make_async_copy(wd_hbm.at[e], wd_f, sem.at[2]))

    @pl.when(nxt_ref[i] >= 0)
    def _():
        @pl.when(i == 0)
        def _():
            for cp in fetch(be_ref[0]):
                cp.start()

        for cp in fetch(be_ref[i]):
            cp.wait()
        wg_s[...] = wg_f[...].astype(BF16)
        wu_s[...] = wu_f[...].astype(BF16)
        wd_s[...] = wd_f[...].astype(BF16)

        @pl.when(nxt_ref[i] < N_EXPERTS)
        def _():
            for cp in fetch(nxt_ref[i]):
                cp.start()

    @pl.when(i < nu_ref[0])
    def _():
        live = lax.broadcasted_iota(I32, xs_ref.shape, 0) < nv_ref[i]
        xb = _unpack_bf16_pairs(jnp.where(live, xs_ref[...], 0)).astype(BF16)
        act = jax.nn.silu(_dot(xb, wg_s[...])) * _dot(xb, wu_s[...])
        o_ref[...] = _pack_bf16_pairs(_dot(act.astype(BF16), wd_s[...]))


def _expert_ffn(xs, blk_expert, blk_valid, n_used, blk_next, w_gate, w_up, w_down):
    n_rows, half = xs.shape
    d = 2 * half
    de = w_gate.shape[2]
    bm = EXPERT_ROWS
    rows = lambda i, be, nv, nu, nx: (jnp.minimum(i, nu[0] - 1), 0)
    hbm = pl.BlockSpec(memory_space=pl.ANY)
    return pl.pallas_call(
        _ffn_kernel,
        grid_spec=pltpu.PrefetchScalarGridSpec(
            num_scalar_prefetch=4,
            grid=(n_rows // bm,),
            in_specs=[pl.BlockSpec((bm, half), rows), hbm, hbm, hbm],
            out_specs=pl.BlockSpec((bm, half), rows),
            scratch_shapes=[
                pltpu.VMEM((d, de), F32), pltpu.VMEM((d, de), F32), pltpu.VMEM((de, d), F32),
                pltpu.VMEM((d, de), BF16), pltpu.VMEM((d, de), BF16), pltpu.VMEM((de, d), BF16),
                pltpu.SemaphoreType.DMA((3,)),
            ],
        ),
        out_shape=jax.ShapeDtypeStruct((n_rows, half), I32),
        compiler_params=_tc_params(1),
        name="expert_ffn",
    )(blk_expert, blk_valid, n_used, blk_next, xs, w_gate, w_up, w_down)


def _combine_kernel(x_ref, y0_ref, y1_ref, rt_ref, g_ref, b_ref, o_ref):
    o_ref[...] = _expert_combine_ln(x_ref[...], y0_ref[...], y1_ref[...], rt_ref[...], g_ref[...], b_ref[...])


def _combine_kernel_into(x_ref, y0_ref, y1_ref, rt_ref, g_ref, b_ref, full_ref, o_ref):
    del full_ref
    _combine_kernel(x_ref, y0_ref, y1_ref, rt_ref, g_ref, b_ref, o_ref)


def _combine_ln(x2, y01, rt, g3, b3, into=None, row0=0, full_rows=None):
    t_all, d = x2.shape
    tq = TOKEN_TILE
    nt = t_all // tq
    blk0 = row0 // tq if full_rows else 0
    in_specs = [
        pl.BlockSpec((tq, d), lambda i: (i, 0)),
        pl.BlockSpec((tq, d // 2), lambda i: (i, 0)),
        pl.BlockSpec((tq, d // 2), lambda i: (i + nt, 0)),
        pl.BlockSpec((8, tq), lambda i: (0, i)),
        _const_spec((1, d)), _const_spec((1, d)),
    ]
    args = [x2, y01, y01, rt, g3, b3]
    body, aliases = _combine_kernel, {}
    if into is not None:
        in_specs.append(pl.BlockSpec(memory_space=pl.ANY))
        args.append(into)
        body, aliases = _combine_kernel_into, {len(args) - 1: 0}
    return pl.pallas_call(
        body,
        grid=(nt,),
        in_specs=in_specs,
        out_specs=pl.BlockSpec((tq, d), lambda i: (i + blk0, 0)),
        out_shape=jax.ShapeDtypeStruct((full_rows or t_all, d), F32),
        input_output_aliases=aliases,
        compiler_params=_tc_params(1),
        name="combine_ln",
    )(*args)


def _routing_tables(rt, counts):
    bm = EXPERT_ROWS
    t_all = rt.shape[1]
    n_rows = 2 * t_all + N_EXPERTS * bm
    cnt = counts[:, 0]
    padded = (cnt + bm - 1) // bm * bm
    ends = jnp.cumsum(padded)
    offs = ends - padded
    experts = rt[0:2]
    off_tok = jnp.sum(jnp.where(experts[None] == jnp.arange(N_EXPERTS, dtype=I32)[:, None, None],
                                offs[:, None, None], 0), axis=0)
    dest2d = (off_tok + rt[2:4]).reshape(-1, SC_CHUNK).astype(I32)
    blk_start = jnp.arange(n_rows // bm, dtype=I32) * bm
    blk_expert = jnp.minimum(jnp.sum(blk_start[:, None] >= ends[None, :], axis=1), N_EXPERTS - 1).astype(I32)
    live_end = jnp.sum(jnp.where(blk_expert[:, None] == jnp.arange(N_EXPERTS, dtype=I32)[None, :],
                                 (offs + cnt)[None, :], 0), axis=1)
    blk_valid = jnp.clip(live_end - blk_start, 0, bm).astype(I32)
    n_used = (ends[-1:] // bm).astype(I32)
    eid = jnp.arange(N_EXPERTS, dtype=I32)
    later_present = jnp.logical_and(eid[None, :] > eid[:, None], (cnt > 0)[None, :])
    next_present = jnp.min(jnp.where(later_present, eid[None, :], N_EXPERTS), axis=1)
    next_of_blk = jnp.sum(jnp.where(blk_expert[:, None] == eid[None, :], next_present[None, :], 0), axis=1)
    prev_expert = jnp.concatenate([jnp.full((1,), -1, I32), blk_expert[:-1]])
    is_first = jnp.logical_and(blk_start < ends[-1], blk_expert != prev_expert)
    blk_next = jnp.where(is_first, next_of_blk, -1).astype(I32)
    return dest2d, blk_expert, blk_valid, n_used, blk_next, n_rows


def _layer_tail(x1, kv, mem_len, p, router, b0, bsz, s_len):
    rw_hi, rw_lo, rbias = router
    x2, xp, rt, counts = _xattn_route(x1, kv, mem_len, p["xq"], p["xo"], p["ln2_g"], p["ln2_b"],
                                      rw_hi, rw_lo, rbias, b0, bsz, s_len)
    dest2d, blk_expert, blk_valid, n_used, blk_next, n_rows = _routing_tables(rt, counts)
    xs = _sc_dispatch(xp, dest2d, n_rows)
    ys = _expert_ffn(xs, blk_expert, blk_valid, n_used, blk_next, p["e_gate"], p["e_up"], p["e_down"])
    y01 = _sc_combine(ys, dest2d)
    return x2, y01, rt


def _row(v):
    return v.reshape(1, -1).astype(F32)


def _common_params(xq, xkv, xo, ln2_g, ln2_b, e_gate, e_up, e_down, ln3_g, ln3_b):
    return dict(xq=xq, xkv=xkv, xo=xo, ln2_g=_row(ln2_g), ln2_b=_row(ln2_b),
                e_gate=e_gate, e_up=e_up, e_down=e_down, ln3_g=_row(ln3_g), ln3_b=_row(ln3_b))


def kernel(x, mem, positions, router_w, router_bias, l0_w_in, l0_sinks, l0_sgu_ln_g, l0_sgu_ln_b, l0_sgu_w, l0_sgu_b, l0_w_out, l0_ln1_g, l0_ln1_b, l0_xq, l0_xkv, l0_xo, l0_ln2_g, l0_ln2_b, l0_e_gate, l0_e_up, l0_e_down, l0_ln3_g, l0_ln3_b, l1_w_in, l1_pool_w, l1_pool_scale, l1_w_out, l1_ln1_g, l1_ln1_b, l1_xq, l1_xkv, l1_xo, l1_ln2_g, l1_ln2_b, l1_e_gate, l1_e_up, l1_e_down, l1_ln3_g, l1_ln3_b):
    bsz, s_len, d = x.shape
    assert s_len % TOKEN_TILE == 0 and TOKEN_TILE % BLOCK == 0
    xt = x.reshape(bsz * s_len, d)
    mem2d = mem.reshape(-1, d)

    rw_t = router_w.T.astype(F32)
    rw_hi = rw_t.astype(BF16)
    rw_lo = (rw_t - rw_hi.astype(F32)).astype(BF16)
    router = (rw_hi, rw_lo, router_bias.reshape(-1, 1).astype(F32))

    half = ROPE_DIM // 2
    inv_freq = (ROPE_THETA ** (-(jnp.arange(half, dtype=F32) * 2.0 / ROPE_DIM))).reshape(half, 1)
    etab_np = np.zeros((128, 3 * 128), np.float32)
    cbase_np = np.ones((1, 128), np.float32)
    for ln in range(128):
        dd = ln % HEAD_DIM
        if dd < ROPE_DIM:
            cbase_np[0, ln] = 0.0
            etab_np[[dd % half, half + dd % half], ln] = 1.0
            if dd >= half:
                etab_np[[2 * half + dd - half, 3 * half + dd - half], 128 + ln] = 1.0
            else:
                etab_np[[2 * half + dd, 3 * half + dd], 256 + ln] = -1.0
    etab = jnp.asarray(etab_np, BF16)
    cbase = jnp.asarray(cbase_np)
    pos_row = positions.reshape(1, -1).astype(I32)
    grp = jnp.arange(B_WIDTH) // B_GROUP_DIM
    gsum = (grp[:, None] == grp[None, :]).astype(BF16)
    bs_full = jnp.repeat(l0_sgu_b.T.astype(F32), B_GROUP_DIM, axis=1)

    p0 = _common_params(l0_xq, l0_xkv, l0_xo, l0_ln2_g, l0_ln2_b, l0_e_gate, l0_e_up, l0_e_down,
                        l0_ln3_g, l0_ln3_b)
    p1 = _common_params(l1_xq, l1_xkv, l1_xo, l1_ln2_g, l1_ln2_b, l1_e_gate, l1_e_up, l1_e_down,
                        l1_ln3_g, l1_ln3_b)
    kv0 = _kv_proj(mem2d, p0["xkv"], bsz)
    kv1 = _kv_proj(mem2d, p1["xkv"], bsz)
    mem_len = mem.shape[1]

    n_split = BATCH_SPLIT if bsz % BATCH_SPLIT == 0 else 1
    nb = bsz // n_split
    out = None
    for part in range(n_split):
        b0 = part * nb
        x1 = _mixer0(xt, pos_row, l0_sinks.astype(F32), l0_w_in, inv_freq, etab, cbase, gsum,
                     _row(l0_sgu_ln_g), _row(l0_sgu_ln_b), l0_sgu_w.astype(F32), bs_full,
                     l0_w_out, _row(l0_ln1_g), _row(l0_ln1_b), b0, nb, s_len)
        x2, y01, rt = _layer_tail(x1, kv0, mem_len, p0, router, b0, nb, s_len)
        x1 = _mixer1(x2, y01, rt, p0["ln3_g"], p0["ln3_b"], l1_w_in, l1_pool_w, _row(l1_pool_scale),
                     l1_w_out, _row(l1_ln1_g), _row(l1_ln1_b), nb, s_len)
        x2, y01, rt = _layer_tail(x1, kv1, mem_len, p1, router, b0, nb, s_len)
        out = _combine_ln(x2, y01, rt, p1["ln3_g"], p1["ln3_b"], into=out, row0=b0 * s_len,
                          full_rows=bsz * s_len)
    return out.reshape(bsz, s_len, d)
```

```python
import functools

import numpy as np
import jax
import jax.numpy as jnp
from jax import lax
from jax.experimental import pallas as pl
from jax.experimental.pallas import tpu as pltpu
from jax.experimental.pallas import tpu_sc as plsc

F32 = jnp.float32
BF16 = jnp.bfloat16
I32 = jnp.int32

DEPTH = 2
ALPHA = (2.0 * DEPTH) ** 0.25
LN_EPS = 1e-5

HEAD_DIM = 64
A_Q_HEADS = 8
A_KV_HEADS = 2
A_GROUP = A_Q_HEADS // A_KV_HEADS
BLOCK = 128
ROPE_THETA = 500000.0
ROPE_DIM = HEAD_DIM // 4
A_WIDTH = A_Q_HEADS * HEAD_DIM
KV_WIDTH = A_KV_HEADS * HEAD_DIM
B_GROUPS = 8
B_GROUP_DIM = 64
B_WIDTH = B_GROUPS * B_GROUP_DIM
POOL_WINDOWS = (2, 4, 8, 16)
POOL_HALO = 16
X_HEADS = 4
N_EXPERTS = 16
N_EXPERT_GROUPS = 4
EXPERTS_PER_GROUP = 4

TOKEN_TILE = 1024
MIXER0_TILE = 512
EXPERT_ROWS = 512
BATCH_SPLIT = 2
SC_WORKERS = 32
SC_CHUNK = 64
VMEM_LIMIT = 56 * 1024 * 1024
NEG_BIG = -1e30


def _layer_norm(z, g, b):
    mu = jnp.mean(z, axis=-1, keepdims=True)
    d = z - mu
    var = jnp.mean(d * d, axis=-1, keepdims=True)
    return d * lax.rsqrt(var + LN_EPS) * g + b


def _dot(a, b):
    return jnp.dot(a, b, preferred_element_type=F32)


def _dot_nt(a, b):
    return lax.dot_general(a, b, (((1,), (1,)), ((), ())), preferred_element_type=F32)


def _split_bf16(v):
    hi = v.astype(BF16)
    lo = (v - hi.astype(F32)).astype(BF16)
    return hi, lo


def _tc_params(n_axes):
    return pltpu.CompilerParams(dimension_semantics=("arbitrary",) * n_axes,
                                vmem_limit_bytes=VMEM_LIMIT)


def _const_spec(shape):
    nd = len(shape)
    return pl.BlockSpec(shape, lambda *_: (0,) * nd, pipeline_mode=pl.Buffered(1))

def _mixer0_kernel(sinks_ref, x_ref, pos_ref, win_ref, invf_ref, etab_ref, cbase_ref, gsum_ref,
                   lng_ref, lnb_ref, ws_ref, bs_ref, wout_ref, g1_ref, b1_ref,
                   o_ref, q_s, kv_s, u_s, vn_s, mix_s, wt_s, win_s, wout_s):
    b = pl.program_id(0)
    j = pl.program_id(1)
    tq = x_ref.shape[0]
    nblk = tq // BLOCK
    kvw = kv_s.shape[1]

    @pl.when(jnp.logical_and(b == 0, j == 0))
    def _():
        win_s[...] = win_ref[...].astype(BF16)
        wout_s[...] = wout_ref[...].astype(BF16)
        r = lax.broadcasted_iota(I32, (BLOCK, BLOCK), 0)
        c = lax.broadcasted_iota(I32, (BLOCK, BLOCK), 1)
        for g in range(B_GROUPS):
            wt_s[g] = jnp.where(c <= r, ws_ref[g], 0.0).astype(BF16)

    @pl.when(j == 0)
    def _():
        kv_s[0:BLOCK, :] = jnp.zeros((BLOCK, kvw), BF16)

    x = x_ref[...]
    h = _dot(x.astype(BF16), win_s[...])

    ang = invf_ref[...] * pos_ref[...].astype(F32)
    c8 = jnp.cos(ang)
    s8 = jnp.sin(ang)
    c8h = c8.astype(BF16).astype(F32)
    s8h = s8.astype(BF16).astype(F32)
    stack = jnp.concatenate([c8h, c8 - c8h, s8h, s8 - s8h, jnp.zeros((128 - 4 * 8, tq), F32)], axis=0)
    tabs = _dot(stack.T.astype(BF16), etab_ref[...])
    cs = tabs[:, 0:128] + cbase_ref[...]
    sa = tabs[:, 128:256]
    sb = tabs[:, 256:384]

    def rope(t):
        return t * cs + pltpu.roll(t, ROPE_DIM // 2, 1) * sa + pltpu.roll(t, 128 - ROPE_DIM // 2, 1) * sb

    for c in range(A_WIDTH // 128):
        t = h[:, c * 128:(c + 1) * 128] * (HEAD_DIM ** -0.5)
        q_s[:, c * 128:(c + 1) * 128] = rope(t).astype(BF16)
    c1 = A_WIDTH
    c2 = c1 + KV_WIDTH
    c3 = c2 + KV_WIDTH
    c4 = c3 + B_WIDTH
    low = lax.broadcasted_iota(I32, (tq, 128), 1) < HEAD_DIM
    kr = rope(h[:, c1:c2])
    kx = pltpu.roll(kr, HEAD_DIM, 1)
    vr = h[:, c2:c3]
    vx = pltpu.roll(vr, HEAD_DIM, 1)
    kv_cols = [jnp.where(low, kr, kx), jnp.where(low, kx, kr),
               jnp.where(low, vr, 0.0), jnp.where(low, 0.0, vx),
               jnp.where(low, vx, 0.0), jnp.where(low, 0.0, vr)]
    for c, col in enumerate(kv_cols):
        kv_s[BLOCK:, c * 128:(c + 1) * 128] = col.astype(BF16)

    u_s[...] = jax.nn.gelu(h[:, c3:c4])
    v = jax.nn.gelu(h[:, c4:])
    gsum = gsum_ref[...]
    mean = _dot(v.astype(BF16), gsum) * (1.0 / B_GROUP_DIM)
    d = v - mean
    var = _dot((d * d).astype(BF16), gsum) * (1.0 / B_GROUP_DIM)
    vn_s[...] = (d * lax.rsqrt(var + LN_EPS) * lng_ref[...] + lnb_ref[...]).astype(BF16)

    qi = lax.broadcasted_iota(I32, (BLOCK, 2 * BLOCK), 0)
    kj = lax.broadcasted_iota(I32, (BLOCK, 2 * BLOCK), 1)
    rel = qi + BLOCK - kj
    band = jnp.logical_and(rel >= 0, rel < BLOCK)
    low_q = lax.broadcasted_iota(I32, (BLOCK, 128), 1) < HEAD_DIM
    low_k = lax.broadcasted_iota(I32, (2 * BLOCK, 128), 1) < HEAD_DIM
    ones_lo = jnp.where(low_k, 1.0, 0.0).astype(BF16)
    ones_hi = jnp.where(low_k, 0.0, 1.0).astype(BF16)
    zero_q = jnp.zeros((BLOCK, 128), BF16)

    def block_body(n, carry):
        r0 = pl.multiple_of(n * BLOCK, BLOCK)
        kv = kv_s[pl.ds(r0, 2 * BLOCK), :]
        qb = q_s[pl.ds(r0, BLOCK), :]
        first = jnp.logical_and(j == 0, n == 0)
        valid = jnp.logical_and(band, kj >= jnp.where(first, BLOCK, 0))
        for c in range(A_WIDTH // 128):
            hk = (2 * c) // A_GROUP
            qp = qb[:, c * 128:(c + 1) * 128]
            kd = kv[:, hk * 128:(hk + 1) * 128]
            v_lo = jnp.concatenate([kv[:, (2 + 2 * hk) * 128:(3 + 2 * hk) * 128], ones_lo], axis=1)
            v_hi = jnp.concatenate([kv[:, (3 + 2 * hk) * 128:(4 + 2 * hk) * 128], ones_hi], axis=1)
            res = None
            esink = []
            for half, (qm, vm) in enumerate(((jnp.where(low_q, qp, zero_q), v_lo),
                                             (jnp.where(low_q, zero_q, qp), v_hi))):
                s = jnp.where(valid, _dot_nt(qm, kd), NEG_BIG)
                sink = sinks_ref[2 * c + half]
                m = jnp.maximum(jnp.max(s, axis=-1, keepdims=True), sink)
                pv = _dot(jnp.exp(s - m).astype(BF16), vm)
                res = pv if res is None else res + pv
                esink.append(jnp.exp(sink - m))
            den = res[:, 128:] + jnp.where(low_q, esink[0], esink[1])
            mix_s[pl.ds(r0, BLOCK), c * 128:(c + 1) * 128] = (res[:, :128] / den).astype(BF16)
        vnb = vn_s[pl.ds(r0, BLOCK), :]
        parts = []
        for c in range(B_WIDTH // 128):
            vp = vnb[:, c * 128:(c + 1) * 128]
            parts.append(_dot(wt_s[2 * c], jnp.where(low_q, vp, zero_q))
                         + _dot(wt_s[2 * c + 1], jnp.where(low_q, zero_q, vp)))
        mixed = jnp.concatenate(parts, axis=1) + bs_ref[...]
        mix_s[pl.ds(r0, BLOCK), A_WIDTH:] = (u_s[pl.ds(r0, BLOCK), :] * mixed).astype(BF16)
        return carry

    lax.fori_loop(0, nblk, block_body, 0)
    kv_s[0:BLOCK, :] = kv_s[tq:tq + BLOCK, :]

    z = ALPHA * x + _dot(mix_s[...], wout_s[...])
    o_ref[...] = _layer_norm(z, g1_ref[...], b1_ref[...])


def _mixer0(x, pos_row, sinks, w_in, invf, etab, cbase, gsum, lng, lnb, w_s, bs_full, w_out, g1, b1,
            b0, bsz, s_len):
    d = x.shape[1]
    t_all = bsz * s_len
    tq = MIXER0_TILE
    nj = s_len // tq
    row = lambda bb, jj: (bb * nj + jj, 0)
    in_w = w_in.shape[1]
    return pl.pallas_call(
        _mixer0_kernel,
        grid=(bsz, nj),
        in_specs=[
            pl.BlockSpec(memory_space=pltpu.SMEM),
            pl.BlockSpec((tq, d), lambda bb, jj: ((b0 + bb) * nj + jj, 0)),
            pl.BlockSpec((1, tq), lambda bb, jj: (0, (b0 + bb) * nj + jj)),
            _const_spec((d, in_w)),
            _const_spec((ROPE_DIM // 2, 1)), _const_spec((128, 3 * 128)), _const_spec((1, 128)),
            _const_spec((B_WIDTH, B_WIDTH)),
            _const_spec((1, B_WIDTH)), _const_spec((1, B_WIDTH)),
            _const_spec((B_GROUPS, BLOCK, BLOCK)),
            _const_spec((BLOCK, B_WIDTH)),
            _const_spec((A_WIDTH + B_WIDTH, d)),
            _const_spec((1, d)), _const_spec((1, d)),
        ],
        out_specs=pl.BlockSpec((tq, d), row),
        out_shape=jax.ShapeDtypeStruct((t_all, d), F32),
        scratch_shapes=[
            pltpu.VMEM((tq, A_WIDTH), BF16),
            pltpu.VMEM((tq + BLOCK, 6 * 128), BF16),
            pltpu.VMEM((tq, B_WIDTH), F32),
            pltpu.VMEM((tq, B_WIDTH), BF16),
            pltpu.VMEM((tq, A_WIDTH + B_WIDTH), BF16),
            pltpu.VMEM((B_GROUPS, BLOCK, BLOCK), BF16),
            pltpu.VMEM((d, in_w), BF16),
            pltpu.VMEM((A_WIDTH + B_WIDTH, d), BF16),
        ],
        compiler_params=_tc_params(2),
        name="mixer0",
    )(sinks, x, pos_row, w_in, invf, etab, cbase, gsum, lng, lnb, w_s, bs_full, w_out, g1, b1)


def _expert_combine_ln(x2, y0_packed, y1_packed, rt, g, b):
    wt = pltpu.bitcast(rt, F32).T
    y = wt[:, 4:5] * _unpack_bf16_pairs(y0_packed) + wt[:, 5:6] * _unpack_bf16_pairs(y1_packed)
    return _layer_norm(ALPHA * x2 + y, g, b)


def _mixer1_kernel(x2_ref, y0_ref, y1_ref, rt_ref, g3_ref, b3_ref,
                   win_ref, pw_ref, ps_ref, wout_ref, g1_ref, b1_ref, o_ref,
                   h_s, mp_s, win_s, pw_s, wout_s):
    j = pl.program_id(1)
    tq = x2_ref.shape[0]
    gw = x2_ref.shape[1] // len(POOL_WINDOWS)

    @pl.when(jnp.logical_and(pl.program_id(0) == 0, j == 0))
    def _():
        win_s[...] = win_ref[...].astype(BF16)
        pw_s[...] = pw_ref[...].astype(BF16)
        wout_s[...] = wout_ref[...].astype(BF16)

    @pl.when(j == 0)
    def _():
        h_s[0:POOL_HALO, :] = jnp.zeros((POOL_HALO, h_s.shape[1]), F32)

    x = _expert_combine_ln(x2_ref[...], y0_ref[...], y1_ref[...], rt_ref[...], g3_ref[...], b3_ref[...])
    h_s[POOL_HALO:, :] = _dot(x.astype(BF16), win_s[...])
    t_pos = j * tq + lax.broadcasted_iota(I32, (tq, 1), 0)
    for g, win in enumerate(POOL_WINDOWS):
        lo, hi = g * gw, (g + 1) * gw
        cur = h_s[POOL_HALO:, lo:hi]
        acc = cur
        for k in range(1, win):
            acc = acc + h_s[POOL_HALO - k:POOL_HALO - k + tq, lo:hi]
        count = jnp.minimum(t_pos + 1, win).astype(F32)
        pooled = acc / count - cur
        mapped = _dot(pooled.astype(BF16), pw_s[g])
        mp_s[:, lo:hi] = (mapped * ps_ref[:, lo:hi]).astype(BF16)
    h_s[0:POOL_HALO, :] = h_s[tq:tq + POOL_HALO, :]
    z = ALPHA * x + _dot(mp_s[...], wout_s[...])
    o_ref[...] = _layer_norm(z, g1_ref[...], b1_ref[...])


def _mixer1(x2, y01, rt, g3, b3, w_in, pool_w, pool_scale, w_out, g1, b1, bsz, s_len):
    t_all, d = x2.shape
    tq = TOKEN_TILE
    nj = s_len // tq
    nt = bsz * nj
    row = lambda bb, jj: (bb * nj + jj, 0)
    ng = len(POOL_WINDOWS)
    return pl.pallas_call(
        _mixer1_kernel,
        grid=(bsz, nj),
        in_specs=[
            pl.BlockSpec((tq, d), row),
            pl.BlockSpec((tq, d // 2), row),
            pl.BlockSpec((tq, d // 2), lambda bb, jj: (nt + bb * nj + jj, 0)),
            pl.BlockSpec((8, tq), lambda bb, jj: (0, bb * nj + jj)),
            _const_spec((1, d)), _const_spec((1, d)),
            _const_spec((d, d)),
            _const_spec((ng, d // ng, d // ng)),
            _const_spec((1, d)),
            _const_spec((d, d)),
            _const_spec((1, d)), _const_spec((1, d)),
        ],
        out_specs=pl.BlockSpec((tq, d), row),
        out_shape=jax.ShapeDtypeStruct((t_all, d), F32),
        scratch_shapes=[pltpu.VMEM((tq + POOL_HALO, d), F32), pltpu.VMEM((tq, d), BF16),
                        pltpu.VMEM((d, d), BF16), pltpu.VMEM((ng, d // ng, d // ng), BF16),
                        pltpu.VMEM((d, d), BF16)],
        compiler_params=_tc_params(2),
        name="mixer1",
    )(x2, y01, y01, rt, g3, b3, w_in, pool_w, pool_scale, w_out, g1, b1)


def _kv_kernel(mem_ref, w_ref, o_ref, w_s):
    @pl.when(pl.program_id(0) == 0)
    def _():
        w_s[...] = w_ref[...].astype(BF16)

    o_ref[...] = _dot(mem_ref[...].astype(BF16), w_s[...]).astype(BF16)


def _kv_proj(mem2d, wkv, bsz):
    rows, d = mem2d.shape
    m = rows // bsz
    return pl.pallas_call(
        _kv_kernel,
        grid=(bsz,),
        in_specs=[pl.BlockSpec((m, d), lambda i: (i, 0)), _const_spec(wkv.shape)],
        out_specs=pl.BlockSpec((m, wkv.shape[1]), lambda i: (i, 0)),
        out_shape=jax.ShapeDtypeStruct((rows, wkv.shape[1]), BF16),
        scratch_shapes=[pltpu.VMEM(wkv.shape, BF16)],
        compiler_params=_tc_params(1),
        name="kv_proj",
    )(mem2d, wkv)


def _top2_of4(v):
    hi01, lo01 = jnp.maximum(v[0], v[1]), jnp.minimum(v[0], v[1])
    hi23, lo23 = jnp.maximum(v[2], v[3]), jnp.minimum(v[2], v[3])
    return jnp.maximum(hi01, hi23) + jnp.maximum(jnp.minimum(hi01, hi23), jnp.maximum(lo01, lo23))


def _argmax_first(vals):
    best, idx = vals[0], jnp.zeros(vals[0].shape, I32)
    for i in range(1, len(vals)):
        better = vals[i] > best
        best = jnp.where(better, vals[i], best)
        idx = jnp.where(better, i, idx)
    return best, idx


def _xattn_kernel(x_ref, kv_ref, wq_ref, wo_ref, g2_ref, b2_ref, rwh_ref, rwl_ref, rb_ref,
                  x2_ref, xp_ref, rt_ref, cnt_ref, run_s, wq_s, wo_s):
    first = jnp.logical_and(pl.program_id(0) == 0, pl.program_id(1) == 0)
    tq, d = x_ref.shape
    hd = d // X_HEADS

    @pl.when(first)
    def _():
        run_s[...] = jnp.zeros(run_s.shape, F32)
        wq_s[...] = wq_ref[...].astype(BF16)
        wo_s[...] = wo_ref[...].astype(BF16)

    x = x_ref[...]
    q = _dot(x.astype(BF16), wq_s[...]) * (hd ** -0.5)
    outs = []
    for hx in range(X_HEADS):
        qh = q[:, hx * hd:(hx + 1) * hd].astype(BF16)
        kh = kv_ref[:, hx * hd:(hx + 1) * hd]
        vh = kv_ref[:, d + hx * hd:d + (hx + 1) * hd]
        s = _dot_nt(qh, kh)
        p = jnp.exp(s - jnp.max(s, axis=-1, keepdims=True))
        o = _dot(p.astype(BF16), vh) / jnp.sum(p, axis=-1, keepdims=True)
        outs.append(o.astype(BF16))
    att = _dot(jnp.concatenate(outs, axis=1), wo_s[...])
    x2 = _layer_norm(ALPHA * x + att, g2_ref[...], b2_ref[...])
    x2_ref[...] = x2

    xp_ref[...] = _pack_bf16_pairs(x2)

    xh, xl = _split_bf16(x2)
    logits = _dot_nt(rwh_ref[...], xh) + _dot_nt(rwh_ref[...], xl) + _dot_nt(rwl_ref[...], xh)
    e_max = jnp.max(logits, axis=0, keepdims=True)
    ex = jnp.exp(logits - e_max)
    scores = ex / jnp.sum(ex, axis=0, keepdims=True)
    biased = scores + rb_ref[...]
    sc = [scores[e:e + 1, :] for e in range(N_EXPERTS)]
    bi = [biased[e:e + 1, :] for e in range(N_EXPERTS)]
    epg = EXPERTS_PER_GROUP
    gscore = [_top2_of4(bi[g * epg:(g + 1) * epg]) for g in range(N_EXPERT_GROUPS)]
    _, gsel = _argmax_first(gscore)

    def pick(vals):
        return [functools.reduce(lambda acc, g: jnp.where(gsel == g, vals[g * epg + i], acc),
                                 range(1, N_EXPERT_GROUPS), vals[i]) for i in range(epg)]

    in_b = pick(bi)
    in_s = pick(sc)
    _, i0 = _argmax_first(in_b)
    _, i1 = _argmax_first([jnp.where(i0 == i, -jnp.inf, in_b[i]) for i in range(epg)])

    def take(vals, idx):
        return functools.reduce(lambda acc, i: jnp.where(idx == i, vals[i], acc), range(1, epg), vals[0])

    s0, s1 = take(in_s, i0), take(in_s, i1)
    w0, w1 = s0 / (s0 + s1), s1 / (s0 + s1)
    e0, e1 = gsel * epg + i0, gsel * epg + i1

    eid = lax.broadcasted_iota(I32, (N_EXPERTS, tq), 0)
    oh0 = eid == e0
    oh1 = eid == e1
    onehot = jnp.where(jnp.logical_or(oh0, oh1), 1.0, 0.0)
    rr = lax.broadcasted_iota(I32, (tq, tq), 0)
    cc = lax.broadcasted_iota(I32, (tq, tq), 1)
    upper = jnp.where(rr < cc, 1.0, 0.0).astype(BF16)
    prefix = _dot(onehot.astype(BF16), upper) + run_s[...]
    r0 = jnp.sum(jnp.where(oh0, prefix, 0.0), axis=0, keepdims=True).astype(I32)
    r1 = jnp.sum(jnp.where(oh1, prefix, 0.0), axis=0, keepdims=True).astype(I32)
    run_s[...] = run_s[...] + jnp.sum(onehot, axis=1, keepdims=True)
    cnt_ref[...] = jnp.broadcast_to(run_s[...], cnt_ref.shape).astype(I32)

    zero = jnp.zeros((1, tq), I32)
    rt_ref[...] = jnp.concatenate(
        [e0, e1, r0, r1, pltpu.bitcast(w0, I32), pltpu.bitcast(w1, I32), zero, zero], axis=0)


def _xattn_route(x1, kv, mem_len, wq, wo, g2, b2, rw_hi, rw_lo, rbias, b0, bsz, s_len):
    t_all, d = x1.shape
    tq = TOKEN_TILE
    nj = s_len // tq
    m = mem_len
    row = lambda bb, jj: (bb * nj + jj, 0)
    return pl.pallas_call(
        _xattn_kernel,
        grid=(bsz, nj),
        in_specs=[
            pl.BlockSpec((tq, d), row),
            pl.BlockSpec((m, 2 * d), lambda bb, jj: (b0 + bb, 0)),
            _const_spec((d, d)), _const_spec((d, d)),
            _const_spec((1, d)), _const_spec((1, d)),
            _const_spec((N_EXPERTS, d)), _const_spec((N_EXPERTS, d)),
            _const_spec((N_EXPERTS, 1)),
        ],
        out_specs=[
            pl.BlockSpec((tq, d), row),
            pl.BlockSpec((tq, d // 2), row),
            pl.BlockSpec((8, tq), lambda bb, jj: (0, bb * nj + jj)),
            _const_spec((N_EXPERTS, 128)),
        ],
        out_shape=[
            jax.ShapeDtypeStruct((t_all, d), F32),
            jax.ShapeDtypeStruct((t_all, d // 2), I32),
            jax.ShapeDtypeStruct((8, t_all), I32),
            jax.ShapeDtypeStruct((N_EXPERTS, 128), I32),
        ],
        scratch_shapes=[pltpu.VMEM((N_EXPERTS, 1), F32), pltpu.VMEM((d, d), BF16), pltpu.VMEM((d, d), BF16)],
        compiler_params=_tc_params(2),
        name="xattn_route",
    )(x1, kv, wq, wo, g2, b2, rw_hi, rw_lo, rbias)


def _sc_mesh():
    return plsc.VectorSubcoreMesh(core_axis_name="c", subcore_axis_name="s")


def _sc_params():
    return pltpu.CompilerParams(needs_layout_passes=False)


def _worker_id():
    return lax.axis_index("s") * lax.axis_size("c") + lax.axis_index("c")


def _sc_dispatch(xp, dest2d, n_rows):
    t_all, width = xp.shape
    chunk = dest2d.shape[1]
    tok_w = t_all // SC_WORKERS
    nch = tok_w // chunk
    slot1 = t_all // chunk
    assert t_all % (SC_WORKERS * chunk * 2) == 0

    def body(x_hbm, dest_hbm, out_hbm, idx0_v, idx1_v, buf0, buf1, sem_r, sem_w):
        wid = _worker_id()
        base = wid * tok_w
        pltpu.sync_copy(dest_hbm.at[pl.ds(wid * nch, nch)], idx0_v)
        pltpu.sync_copy(dest_hbm.at[pl.ds(slot1 + wid * nch, nch)], idx1_v)

        def read(c, buf, k):
            return pltpu.make_async_copy(x_hbm.at[pl.ds(base + c * chunk, chunk)], buf, sem_r.at[k])

        def scatter(c, buf):
            a = pltpu.make_async_copy(buf, out_hbm.at[idx0_v.at[c]], sem_w.at[0])
            b = pltpu.make_async_copy(buf, out_hbm.at[idx1_v.at[c]], sem_w.at[1])
            a.start()
            b.start()
            a.wait()
            b.wait()

        read(0, buf0, 0).start()

        @pl.loop(0, nch // 2)
        def _(g):
            c = 2 * g
            read(c + 1, buf1, 1).start()
            read(c, buf0, 0).wait()
            scatter(c, buf0)

            @pl.when(c + 2 < nch)
            def _():
                read(c + 2, buf0, 0).start()

            read(c + 1, buf1, 1).wait()
            scatter(c + 1, buf1)

    return pl.kernel(
        body,
        out_type=jax.ShapeDtypeStruct((n_rows, width), xp.dtype),
        mesh=_sc_mesh(),
        scratch_types=[
            pltpu.VMEM((nch, chunk), I32),
            pltpu.VMEM((nch, chunk), I32),
            pltpu.VMEM((chunk, width), xp.dtype),
            pltpu.VMEM((chunk, width), xp.dtype),
            pltpu.SemaphoreType.DMA((2,)),
            pltpu.SemaphoreType.DMA((2,)),
        ],
        compiler_params=_sc_params(),
        name="sc_dispatch",
    )(xp, dest2d)


def _sc_combine(ys, dest2d):
    n_idx_rows, chunk = dest2d.shape
    width = ys.shape[1]
    nch = n_idx_rows // SC_WORKERS
    assert n_idx_rows % (SC_WORKERS * 2) == 0

    def body(y_hbm, dest_hbm, out_hbm, idx_v, buf0, buf1, sem_g):
        wid = _worker_id()
        base = wid * nch * chunk
        pltpu.sync_copy(dest_hbm.at[pl.ds(wid * nch, nch)], idx_v)

        def gather(c, buf, k):
            return pltpu.make_async_copy(y_hbm.at[idx_v.at[c]], buf, sem_g.at[k])

        def write(c, buf):
            pltpu.sync_copy(buf, out_hbm.at[pl.ds(base + c * chunk, chunk)])

        gather(0, buf0, 0).start()

        @pl.loop(0, nch // 2)
        def _(g):
            c = 2 * g
            gather(c + 1, buf1, 1).start()
            gather(c, buf0, 0).wait()
            write(c, buf0)

            @pl.when(c + 2 < nch)
            def _():
                gather(c + 2, buf0, 0).start()

            gather(c + 1, buf1, 1).wait()
            write(c + 1, buf1)

    return pl.kernel(
        body,
        out_type=jax.ShapeDtypeStruct((n_idx_rows * chunk, width), ys.dtype),
        mesh=_sc_mesh(),
        scratch_types=[
            pltpu.VMEM((nch, chunk), I32),
            pltpu.VMEM((chunk, width), ys.dtype),
            pltpu.VMEM((chunk, width), ys.dtype),
            pltpu.SemaphoreType.DMA((2,)),
        ],
        compiler_params=_sc_params(),
        name="sc_combine",
    )(ys, dest2d)


def _pack_bf16_pairs(v):
    half = v.shape[1] // 2
    lo = pltpu.bitcast(v[:, :half].astype(BF16).astype(F32), jnp.uint32) >> 16
    hi = pltpu.bitcast(v[:, half:].astype(BF16).astype(F32), jnp.uint32) & jnp.uint32(0xFFFF0000)
    return pltpu.bitcast(hi | lo, I32)


def _unpack_bf16_pairs(w):
    w = pltpu.bitcast(w, jnp.uint32)
    lo = pltpu.bitcast(w << 16, F32)
    hi = pltpu.bitcast(w & jnp.uint32(0xFFFF0000), F32)
    return jnp.concatenate([lo, hi], axis=1)


def _ffn_kernel(be_ref, nv_ref, nu_ref, nxt_ref, xs_ref, wg_hbm, wu_hbm, wd_hbm, o_ref,
                wg_f, wu_f, wd_f, wg_s, wu_s, wd_s, sem):
    i = pl.program_id(0)

    def fetch(e):
        return (pltpu.make_async_copy(wg_hbm.at[e], wg_f, sem.at[0]),
                pltpu.make_async_copy(wu_hbm.at[e], wu_f, sem.at[1]),
                pltpu.make_async_copy(wd_hbm.at[e], wd_f, sem.at[2]))

    @pl.when(nxt_ref[i] >= 0)
    def _():
        @pl.when(i == 0)
        def _():
            for cp in fetch(be_ref[0]):
                cp.start()

        for cp in fetch(be_ref[i]):
            cp.wait()
        wg_s[...] = wg_f[...].astype(BF16)
        wu_s[...] = wu_f[...].astype(BF16)
        wd_s[...] = wd_f[...].astype(BF16)

        @pl.when(nxt_ref[i] < N_EXPERTS)
        def _():
            for cp in fetch(nxt_ref[i]):
                cp.start()

    @pl.when(i < nu_ref[0])
    def _():
        live = lax.broadcasted_iota(I32, xs_ref.shape, 0) < nv_ref[i]
        xb = _unpack_bf16_pairs(jnp.where(live, xs_ref[...], 0)).astype(BF16)
        act = jax.nn.silu(_dot(xb, wg_s[...])) * _dot(xb, wu_s[...])
        o_ref[...] = _pack_bf16_pairs(_dot(act.astype(BF16), wd_s[...]))


def _expert_ffn(xs, blk_expert, blk_valid, n_used, blk_next, w_gate, w_up, w_down):
    n_rows, half = xs.shape
    d = 2 * half
    de = w_gate.shape[2]
    bm = EXPERT_ROWS
    rows = lambda i, be, nv, nu, nx: (jnp.minimum(i, nu[0] - 1), 0)
    hbm = pl.BlockSpec(memory_space=pl.ANY)
    return pl.pallas_call(
        _ffn_kernel,
        grid_spec=pltpu.PrefetchScalarGridSpec(
            num_scalar_prefetch=4,
            grid=(n_rows // bm,),
            in_specs=[pl.BlockSpec((bm, half), rows), hbm, hbm, hbm],
            out_specs=pl.BlockSpec((bm, half), rows),
            scratch_shapes=[
                pltpu.VMEM((d, de), F32), pltpu.VMEM((d, de), F32), pltpu.VMEM((de, d), F32),
                pltpu.VMEM((d, de), BF16), pltpu.VMEM((d, de), BF16), pltpu.VMEM((de, d), BF16),
                pltpu.SemaphoreType.DMA((3,)),
            ],
        ),
        out_shape=jax.ShapeDtypeStruct((n_rows, half), I32),
        compiler_params=_tc_params(1),
        name="expert_ffn",
    )(blk_expert, blk_valid, n_used, blk_next, xs, w_gate, w_up, w_down)


def _combine_kernel(x_ref, y0_ref, y1_ref, rt_ref, g_ref, b_ref, o_ref):
    o_ref[...] = _expert_combine_ln(x_ref[...], y0_ref[...], y1_ref[...], rt_ref[...], g_ref[...], b_ref[...])


def _combine_kernel_into(x_ref, y0_ref, y1_ref, rt_ref, g_ref, b_ref, full_ref, o_ref):
    del full_ref
    _combine_kernel(x_ref, y0_ref, y1_ref, rt_ref, g_ref, b_ref, o_ref)


def _combine_ln(x2, y01, rt, g3, b3, into=None, row0=0, full_rows=None):
    t_all, d = x2.shape
    tq = TOKEN_TILE
    nt = t_all // tq
    blk0 = row0 // tq if full_rows else 0
    in_specs = [
        pl.BlockSpec((tq, d), lambda i: (i, 0)),
        pl.BlockSpec((tq, d // 2), lambda i: (i, 0)),
        pl.BlockSpec((tq, d // 2), lambda i: (i + nt, 0)),
        pl.BlockSpec((8, tq), lambda i: (0, i)),
        _const_spec((1, d)), _const_spec((1, d)),
    ]
    args = [x2, y01, y01, rt, g3, b3]
    body, aliases = _combine_kernel, {}
    if into is not None:
        in_specs.append(pl.BlockSpec(memory_space=pl.ANY))
        args.append(into)
        body, aliases = _combine_kernel_into, {len(args) - 1: 0}
    return pl.pallas_call(
        body,
        grid=(nt,),
        in_specs=in_specs,
        out_specs=pl.BlockSpec((tq, d), lambda i: (i + blk0, 0)),
        out_shape=jax.ShapeDtypeStruct((full_rows or t_all, d), F32),
        input_output_aliases=aliases,
        compiler_params=_tc_params(1),
        name="combine_ln",
    )(*args)


def _routing_tables(rt, counts):
    bm = EXPERT_ROWS
    t_all = rt.shape[1]
    n_rows = 2 * t_all + N_EXPERTS * bm
    cnt = counts[:, 0]
    padded = (cnt + bm - 1) // bm * bm
    ends = jnp.cumsum(padded)
    offs = ends - padded
    experts = rt[0:2]
    off_tok = jnp.sum(jnp.where(experts[None] == jnp.arange(N_EXPERTS, dtype=I32)[:, None, None],
                                offs[:, None, None], 0), axis=0)
    dest2d = (off_tok + rt[2:4]).reshape(-1, SC_CHUNK).astype(I32)
    blk_start = jnp.arange(n_rows // bm, dtype=I32) * bm
    blk_expert = jnp.minimum(jnp.sum(blk_start[:, None] >= ends[None, :], axis=1), N_EXPERTS - 1).astype(I32)
    live_end = jnp.sum(jnp.where(blk_expert[:, None] == jnp.arange(N_EXPERTS, dtype=I32)[None, :],
                                 (offs + cnt)[None, :], 0), axis=1)
    blk_valid = jnp.clip(live_end - blk_start, 0, bm).astype(I32)
    n_used = (ends[-1:] // bm).astype(I32)
    eid = jnp.arange(N_EXPERTS, dtype=I32)
    later_present = jnp.logical_and(eid[None, :] > eid[:, None], (cnt > 0)[None, :])
    next_present = jnp.min(jnp.where(later_present, eid[None, :], N_EXPERTS), axis=1)
    next_of_blk = jnp.sum(jnp.where(blk_expert[:, None] == eid[None, :], next_present[None, :], 0), axis=1)
    prev_expert = jnp.concatenate([jnp.full((1,), -1, I32), blk_expert[:-1]])
    is_first = jnp.logical_and(blk_start < ends[-1], blk_expert != prev_expert)
    blk_next = jnp.where(is_first, next_of_blk, -1).astype(I32)
    return dest2d, blk_expert, blk_valid, n_used, blk_next, n_rows


def _layer_tail(x1, kv, mem_len, p, router, b0, bsz, s_len):
    rw_hi, rw_lo, rbias = router
    x2, xp, rt, counts = _xattn_route(x1, kv, mem_len, p["xq"], p["xo"], p["ln2_g"], p["ln2_b"],
                                      rw_hi, rw_lo, rbias, b0, bsz, s_len)
    dest2d, blk_expert, blk_valid, n_used, blk_next, n_rows = _routing_tables(rt, counts)
    xs = _sc_dispatch(xp, dest2d, n_rows)
    ys = _expert_ffn(xs, blk_expert, blk_valid, n_used, blk_next, p["e_gate"], p["e_up"], p["e_down"])
    y01 = _sc_combine(ys, dest2d)
    return x2, y01, rt


def _row(v):
    return v.reshape(1, -1).astype(F32)


def _common_params(xq, xkv, xo, ln2_g, ln2_b, e_gate, e_up, e_down, ln3_g, ln3_b):
    return dict(xq=xq, xkv=xkv, xo=xo, ln2_g=_row(ln2_g), ln2_b=_row(ln2_b),
                e_gate=e_gate, e_up=e_up, e_down=e_down, ln3_g=_row(ln3_g), ln3_b=_row(ln3_b))


def kernel(x, mem, positions, router_w, router_bias, l0_w_in, l0_sinks, l0_sgu_ln_g, l0_sgu_ln_b, l0_sgu_w, l0_sgu_b, l0_w_out, l0_ln1_g, l0_ln1_b, l0_xq, l0_xkv, l0_xo, l0_ln2_g, l0_ln2_b, l0_e_gate, l0_e_up, l0_e_down, l0_ln3_g, l0_ln3_b, l1_w_in, l1_pool_w, l1_pool_scale, l1_w_out, l1_ln1_g, l1_ln1_b, l1_xq, l1_xkv, l1_xo, l1_ln2_g, l1_ln2_b, l1_e_gate, l1_e_up, l1_e_down, l1_ln3_g, l1_ln3_b):
    bsz, s_len, d = x.shape
    assert s_len % TOKEN_TILE == 0 and s_len % MIXER0_TILE == 0 and MIXER0_TILE % BLOCK == 0
    xt = x.reshape(bsz * s_len, d)
    mem2d = mem.reshape(-1, d)

    rw_t = router_w.T.astype(F32)
    rw_hi = rw_t.astype(BF16)
    rw_lo = (rw_t - rw_hi.astype(F32)).astype(BF16)
    router = (rw_hi, rw_lo, router_bias.reshape(-1, 1).astype(F32))

    half = ROPE_DIM // 2
    inv_freq = (ROPE_THETA ** (-(jnp.arange(half, dtype=F32) * 2.0 / ROPE_DIM))).reshape(half, 1)
    etab_np = np.zeros((128, 3 * 128), np.float32)
    cbase_np = np.ones((1, 128), np.float32)
    for ln in range(128):
        dd = ln % HEAD_DIM
        if dd < ROPE_DIM:
            cbase_np[0, ln] = 0.0
            etab_np[[dd % half, half + dd % half], ln] = 1.0
            if dd >= half:
                etab_np[[2 * half + dd - half, 3 * half + dd - half], 128 + ln] = 1.0
            else:
                etab_np[[2 * half + dd, 3 * half + dd], 256 + ln] = -1.0
    etab = jnp.asarray(etab_np, BF16)
    cbase = jnp.asarray(cbase_np)
    pos_row = positions.reshape(1, -1).astype(I32)
    grp = jnp.arange(B_WIDTH) // B_GROUP_DIM
    gsum = (grp[:, None] == grp[None, :]).astype(BF16)
    bs_full = jnp.repeat(l0_sgu_b.T.astype(F32), B_GROUP_DIM, axis=1)

    p0 = _common_params(l0_xq, l0_xkv, l0_xo, l0_ln2_g, l0_ln2_b, l0_e_gate, l0_e_up, l0_e_down,
                        l0_ln3_g, l0_ln3_b)
    p1 = _common_params(l1_xq, l1_xkv, l1_xo, l1_ln2_g, l1_ln2_b, l1_e_gate, l1_e_up, l1_e_down,
                        l1_ln3_g, l1_ln3_b)
    kv0 = _kv_proj(mem2d, p0["xkv"], bsz)
    kv1 = _kv_proj(mem2d, p1["xkv"], bsz)
    mem_len = mem.shape[1]

    n_split = BATCH_SPLIT if bsz % BATCH_SPLIT == 0 else 1
    nb = bsz // n_split
    out = None
    for part in range(n_split):
        b0 = part * nb
        x1 = _mixer0(xt, pos_row, l0_sinks.astype(F32), l0_w_in, inv_freq, etab, cbase, gsum,
                     _row(l0_sgu_ln_g), _row(l0_sgu_ln_b), l0_sgu_w.astype(F32), bs_full,
                     l0_w_out, _row(l0_ln1_g), _row(l0_ln1_b), b0, nb, s_len)
        x2, y01, rt = _layer_tail(x1, kv0, mem_len, p0, router, b0, nb, s_len)
        x1 = _mixer1(x2, y01, rt, p0["ln3_g"], p0["ln3_b"], l1_w_in, l1_pool_w, _row(l1_pool_scale),
                     l1_w_out, _row(l1_ln1_g), _row(l1_ln1_b), nb, s_len)
        x2, y01, rt = _layer_tail(x1, kv1, mem_len, p1, router, b0, nb, s_len)
        out = _combine_ln(x2, y01, rt, p1["ln3_g"], p1["ln3_b"], into=out, row0=b0 * s_len,
                          full_rows=bsz * s_len)
    return out.reshape(bsz, s_len, d)
```

```python
import functools

import numpy as np
import jax
import jax.numpy as jnp
from jax import lax
from jax.experimental import pallas as pl
from jax.experimental.pallas import tpu as pltpu
from jax.experimental.pallas import tpu_sc as plsc

F32 = jnp.float32
BF16 = jnp.bfloat16
I32 = jnp.int32

DEPTH = 2
ALPHA = (2.0 * DEPTH) ** 0.25
LN_EPS = 1e-5

HEAD_DIM = 64
A_Q_HEADS = 8
A_KV_HEADS = 2
A_GROUP = A_Q_HEADS // A_KV_HEADS
BLOCK = 128
ROPE_THETA = 500000.0
ROPE_DIM = HEAD_DIM // 4
A_WIDTH = A_Q_HEADS * HEAD_DIM
KV_WIDTH = A_KV_HEADS * HEAD_DIM
B_GROUPS = 8
B_GROUP_DIM = 64
B_WIDTH = B_GROUPS * B_GROUP_DIM
POOL_WINDOWS = (2, 4, 8, 16)
POOL_HALO = 16
X_HEADS = 4
N_EXPERTS = 16
N_EXPERT_GROUPS = 4
EXPERTS_PER_GROUP = 4

TOKEN_TILE = 1024
MIXER0_TILE = 512
SLAB_ROWS = 512
EXPERT_ROWS = 512
BATCH_SPLIT = 2
SC_WORKERS = 32
SC_CHUNK = 64
VMEM_LIMIT = 56 * 1024 * 1024
NEG_BIG = -1e30


def _layer_norm(z, g, b):
    mu = jnp.mean(z, axis=-1, keepdims=True)
    d = z - mu
    var = jnp.mean(d * d, axis=-1, keepdims=True)
    return d * lax.rsqrt(var + LN_EPS) * g + b


def _dot(a, b):
    return jnp.dot(a, b, preferred_element_type=F32)


def _dot_nt(a, b):
    return lax.dot_general(a, b, (((1,), (1,)), ((), ())), preferred_element_type=F32)


def _split_bf16(v):
    hi = v.astype(BF16)
    lo = (v - hi.astype(F32)).astype(BF16)
    return hi, lo


def _tc_params(n_axes):
    return pltpu.CompilerParams(dimension_semantics=("arbitrary",) * n_axes,
                                vmem_limit_bytes=VMEM_LIMIT)


def _const_spec(shape):
    nd = len(shape)
    return pl.BlockSpec(shape, lambda *_: (0,) * nd, pipeline_mode=pl.Buffered(1))

def _mixer0_kernel(sinks_ref, x_ref, pos_ref, win_ref, invf_ref, etab_ref, cbase_ref, gsum_ref,
                   lng_ref, lnb_ref, ws_ref, bs_ref, wout_ref, g1_ref, b1_ref,
                   o_ref, q_s, kv_s, u_s, vn_s, mix_s, wt_s, win_s, wout_s):
    b = pl.program_id(0)
    j = pl.program_id(1)
    tq = x_ref.shape[0]
    nblk = tq // BLOCK
    kvw = kv_s.shape[1]

    @pl.when(jnp.logical_and(b == 0, j == 0))
    def _():
        win_s[...] = win_ref[...].astype(BF16)
        wout_s[...] = wout_ref[...].astype(BF16)
        r = lax.broadcasted_iota(I32, (BLOCK, BLOCK), 0)
        c = lax.broadcasted_iota(I32, (BLOCK, BLOCK), 1)
        for g in range(B_GROUPS):
            wt_s[g] = jnp.where(c <= r, ws_ref[g], 0.0).astype(BF16)

    @pl.when(j == 0)
    def _():
        kv_s[0:BLOCK, :] = jnp.zeros((BLOCK, kvw), BF16)

    x = x_ref[...]
    h = _dot(x.astype(BF16), win_s[...])

    ang = invf_ref[...] * pos_ref[...].astype(F32)
    c8 = jnp.cos(ang)
    s8 = jnp.sin(ang)
    c8h = c8.astype(BF16).astype(F32)
    s8h = s8.astype(BF16).astype(F32)
    stack = jnp.concatenate([c8h, c8 - c8h, s8h, s8 - s8h, jnp.zeros((128 - 4 * 8, tq), F32)], axis=0)
    tabs = _dot(stack.T.astype(BF16), etab_ref[...])
    cs = tabs[:, 0:128] + cbase_ref[...]
    sa = tabs[:, 128:256]
    sb = tabs[:, 256:384]

    def rope(t):
        return t * cs + pltpu.roll(t, ROPE_DIM // 2, 1) * sa + pltpu.roll(t, 128 - ROPE_DIM // 2, 1) * sb

    for c in range(A_WIDTH // 128):
        t = h[:, c * 128:(c + 1) * 128] * (HEAD_DIM ** -0.5)
        q_s[:, c * 128:(c + 1) * 128] = rope(t).astype(BF16)
    c1 = A_WIDTH
    c2 = c1 + KV_WIDTH
    c3 = c2 + KV_WIDTH
    c4 = c3 + B_WIDTH
    low = lax.broadcasted_iota(I32, (tq, 128), 1) < HEAD_DIM
    kr = rope(h[:, c1:c2])
    kx = pltpu.roll(kr, HEAD_DIM, 1)
    vr = h[:, c2:c3]
    vx = pltpu.roll(vr, HEAD_DIM, 1)
    kv_cols = [jnp.where(low, kr, kx), jnp.where(low, kx, kr),
               jnp.where(low, vr, 0.0), jnp.where(low, 0.0, vx),
               jnp.where(low, vx, 0.0), jnp.where(low, 0.0, vr)]
    for c, col in enumerate(kv_cols):
        kv_s[BLOCK:, c * 128:(c + 1) * 128] = col.astype(BF16)

    u_s[...] = jax.nn.gelu(h[:, c3:c4])
    v = jax.nn.gelu(h[:, c4:])
    gsum = gsum_ref[...]
    mean = _dot(v.astype(BF16), gsum) * (1.0 / B_GROUP_DIM)
    d = v - mean
    var = _dot((d * d).astype(BF16), gsum) * (1.0 / B_GROUP_DIM)
    vn_s[...] = (d * lax.rsqrt(var + LN_EPS) * lng_ref[...] + lnb_ref[...]).astype(BF16)

    qi = lax.broadcasted_iota(I32, (BLOCK, 2 * BLOCK), 0)
    kj = lax.broadcasted_iota(I32, (BLOCK, 2 * BLOCK), 1)
    rel = qi + BLOCK - kj
    band = jnp.logical_and(rel >= 0, rel < BLOCK)
    low_q = lax.broadcasted_iota(I32, (BLOCK, 128), 1) < HEAD_DIM
    low_k = lax.broadcasted_iota(I32, (2 * BLOCK, 128), 1) < HEAD_DIM
    ones_lo = jnp.where(low_k, 1.0, 0.0).astype(BF16)
    ones_hi = jnp.where(low_k, 0.0, 1.0).astype(BF16)
    zero_q = jnp.zeros((BLOCK, 128), BF16)

    def block_body(n, carry):
        r0 = pl.multiple_of(n * BLOCK, BLOCK)
        kv = kv_s[pl.ds(r0, 2 * BLOCK), :]
        qb = q_s[pl.ds(r0, BLOCK), :]
        first = jnp.logical_and(j == 0, n == 0)
        valid = jnp.logical_and(band, kj >= jnp.where(first, BLOCK, 0))
        for c in range(A_WIDTH // 128):
            hk = (2 * c) // A_GROUP
            qp = qb[:, c * 128:(c + 1) * 128]
            kd = kv[:, hk * 128:(hk + 1) * 128]
            v_lo = jnp.concatenate([kv[:, (2 + 2 * hk) * 128:(3 + 2 * hk) * 128], ones_lo], axis=1)
            v_hi = jnp.concatenate([kv[:, (3 + 2 * hk) * 128:(4 + 2 * hk) * 128], ones_hi], axis=1)
            res = None
            esink = []
            for half, (qm, vm) in enumerate(((jnp.where(low_q, qp, zero_q), v_lo),
                                             (jnp.where(low_q, zero_q, qp), v_hi))):
                s = jnp.where(valid, _dot_nt(qm, kd), NEG_BIG)
                sink = sinks_ref[2 * c + half]
                m = jnp.maximum(jnp.max(s, axis=-1, keepdims=True), sink)
                pv = _dot(jnp.exp(s - m).astype(BF16), vm)
                res = pv if res is None else res + pv
                esink.append(jnp.exp(sink - m))
            den = res[:, 128:] + jnp.where(low_q, esink[0], esink[1])
            mix_s[pl.ds(r0, BLOCK), c * 128:(c + 1) * 128] = (res[:, :128] / den).astype(BF16)
        vnb = vn_s[pl.ds(r0, BLOCK), :]
        parts = []
        for c in range(B_WIDTH // 128):
            vp = vnb[:, c * 128:(c + 1) * 128]
            parts.append(_dot(wt_s[2 * c], jnp.where(low_q, vp, zero_q))
                         + _dot(wt_s[2 * c + 1], jnp.where(low_q, zero_q, vp)))
        mixed = jnp.concatenate(parts, axis=1) + bs_ref[...]
        mix_s[pl.ds(r0, BLOCK), A_WIDTH:] = (u_s[pl.ds(r0, BLOCK), :] * mixed).astype(BF16)
        return carry

    lax.fori_loop(0, nblk, block_body, 0)
    kv_s[0:BLOCK, :] = kv_s[tq:tq + BLOCK, :]

    z = ALPHA * x + _dot(mix_s[...], wout_s[...])
    o_ref[...] = _layer_norm(z, g1_ref[...], b1_ref[...])


def _mixer0(x, pos_row, sinks, w_in, invf, etab, cbase, gsum, lng, lnb, w_s, bs_full, w_out, g1, b1,
            b0, bsz, s_len):
    d = x.shape[1]
    t_all = bsz * s_len
    tq = MIXER0_TILE
    nj = s_len // tq
    row = lambda bb, jj: (bb * nj + jj, 0)
    in_w = w_in.shape[1]
    return pl.pallas_call(
        _mixer0_kernel,
        grid=(bsz, nj),
        in_specs=[
            pl.BlockSpec(memory_space=pltpu.SMEM),
            pl.BlockSpec((tq, d), lambda bb, jj: ((b0 + bb) * nj + jj, 0)),
            pl.BlockSpec((1, tq), lambda bb, jj: (0, (b0 + bb) * nj + jj)),
            _const_spec((d, in_w)),
            _const_spec((ROPE_DIM // 2, 1)), _const_spec((128, 3 * 128)), _const_spec((1, 128)),
            _const_spec((B_WIDTH, B_WIDTH)),
            _const_spec((1, B_WIDTH)), _const_spec((1, B_WIDTH)),
            _const_spec((B_GROUPS, BLOCK, BLOCK)),
            _const_spec((BLOCK, B_WIDTH)),
            _const_spec((A_WIDTH + B_WIDTH, d)),
            _const_spec((1, d)), _const_spec((1, d)),
        ],
        out_specs=pl.BlockSpec((tq, d), row),
        out_shape=jax.ShapeDtypeStruct((t_all, d), F32),
        scratch_shapes=[
            pltpu.VMEM((tq, A_WIDTH), BF16),
            pltpu.VMEM((tq + BLOCK, 6 * 128), BF16),
            pltpu.VMEM((tq, B_WIDTH), F32),
            pltpu.VMEM((tq, B_WIDTH), BF16),
            pltpu.VMEM((tq, A_WIDTH + B_WIDTH), BF16),
            pltpu.VMEM((B_GROUPS, BLOCK, BLOCK), BF16),
            pltpu.VMEM((d, in_w), BF16),
            pltpu.VMEM((A_WIDTH + B_WIDTH, d), BF16),
        ],
        compiler_params=_tc_params(2),
        name="mixer0",
    )(sinks, x, pos_row, w_in, invf, etab, cbase, gsum, lng, lnb, w_s, bs_full, w_out, g1, b1)


def _expert_combine_ln(x2, y0_packed, y1_packed, rt, g, b):
    wt = pltpu.bitcast(rt, F32).T
    y = wt[:, 4:5] * _unpack_bf16_pairs(y0_packed) + wt[:, 5:6] * _unpack_bf16_pairs(y1_packed)
    return _layer_norm(ALPHA * x2 + y, g, b)


def _mixer1_kernel(x2_ref, y0_ref, y1_ref, rt_ref, g3_ref, b3_ref,
                   win_ref, pw_ref, ps_ref, wout_ref, g1_ref, b1_ref, o_ref,
                   h_s, mp_s, win_s, pw_s, wout_s):
    j = pl.program_id(1)
    tq = x2_ref.shape[0]
    gw = x2_ref.shape[1] // len(POOL_WINDOWS)
    slabs = [slice(c * SLAB_ROWS, (c + 1) * SLAB_ROWS) for c in range(tq // SLAB_ROWS)]

    @pl.when(jnp.logical_and(pl.program_id(0) == 0, j == 0))
    def _():
        win_s[...] = win_ref[...].astype(BF16)
        pw_s[...] = pw_ref[...].astype(BF16)
        wout_s[...] = wout_ref[...].astype(BF16)

    @pl.when(j == 0)
    def _():
        h_s[0:POOL_HALO, :] = jnp.zeros((POOL_HALO, h_s.shape[1]), F32)

    xs = []
    for rows in slabs:
        x = _expert_combine_ln(x2_ref[rows, :], y0_ref[rows, :], y1_ref[rows, :], rt_ref[:, rows],
                               g3_ref[...], b3_ref[...])
        h_s[POOL_HALO + rows.start:POOL_HALO + rows.stop, :] = _dot(x.astype(BF16), win_s[...])
        xs.append(x)

    for rows, x in zip(slabs, xs):
        n = rows.stop - rows.start
        t_pos = j * tq + rows.start + lax.broadcasted_iota(I32, (n, 1), 0)
        for g, win in enumerate(POOL_WINDOWS):
            lo, hi = g * gw, (g + 1) * gw
            ext = h_s[rows.start:rows.stop + POOL_HALO, lo:hi]
            acc = ext
            shift = 1
            while shift < win:
                acc = acc + pltpu.roll(acc, shift, 0)
                shift *= 2
            count = jnp.minimum(t_pos + 1, win).astype(F32)
            pooled = acc[POOL_HALO:, :] / count - ext[POOL_HALO:, :]
            mapped = _dot(pooled.astype(BF16), pw_s[g])
            mp_s[rows, lo:hi] = (mapped * ps_ref[:, lo:hi]).astype(BF16)
        z = ALPHA * x + _dot(mp_s[rows, :], wout_s[...])
        o_ref[rows, :] = _layer_norm(z, g1_ref[...], b1_ref[...])
    h_s[0:POOL_HALO, :] = h_s[tq:tq + POOL_HALO, :]


def _mixer1(x2, y01, rt, g3, b3, w_in, pool_w, pool_scale, w_out, g1, b1, bsz, s_len):
    t_all, d = x2.shape
    tq = TOKEN_TILE
    nj = s_len // tq
    nt = bsz * nj
    row = lambda bb, jj: (bb * nj + jj, 0)
    ng = len(POOL_WINDOWS)
    return pl.pallas_call(
        _mixer1_kernel,
        grid=(bsz, nj),
        in_specs=[
            pl.BlockSpec((tq, d), row),
            pl.BlockSpec((tq, d // 2), row),
            pl.BlockSpec((tq, d // 2), lambda bb, jj: (nt + bb * nj + jj, 0)),
            pl.BlockSpec((8, tq), lambda bb, jj: (0, bb * nj + jj)),
            _const_spec((1, d)), _const_spec((1, d)),
            _const_spec((d, d)),
            _const_spec((ng, d // ng, d // ng)),
            _const_spec((1, d)),
            _const_spec((d, d)),
            _const_spec((1, d)), _const_spec((1, d)),
        ],
        out_specs=pl.BlockSpec((tq, d), row),
        out_shape=jax.ShapeDtypeStruct((t_all, d), F32),
        scratch_shapes=[pltpu.VMEM((tq + POOL_HALO, d), F32), pltpu.VMEM((tq, d), BF16),
                        pltpu.VMEM((d, d), BF16), pltpu.VMEM((ng, d // ng, d // ng), BF16),
                        pltpu.VMEM((d, d), BF16)],
        compiler_params=_tc_params(2),
        name="mixer1",
    )(x2, y01, y01, rt, g3, b3, w_in, pool_w, pool_scale, w_out, g1, b1)


def _kv_kernel(mem_ref, w_ref, o_ref, w_s):
    @pl.when(pl.program_id(0) == 0)
    def _():
        w_s[...] = w_ref[...].astype(BF16)

    o_ref[...] = _dot(mem_ref[...].astype(BF16), w_s[...]).astype(BF16)


def _kv_proj(mem2d, wkv, bsz):
    rows, d = mem2d.shape
    m = rows // bsz
    return pl.pallas_call(
        _kv_kernel,
        grid=(bsz,),
        in_specs=[pl.BlockSpec((m, d), lambda i: (i, 0)), _const_spec(wkv.shape)],
        out_specs=pl.BlockSpec((m, wkv.shape[1]), lambda i: (i, 0)),
        out_shape=jax.ShapeDtypeStruct((rows, wkv.shape[1]), BF16),
        scratch_shapes=[pltpu.VMEM(wkv.shape, BF16)],
        compiler_params=_tc_params(1),
        name="kv_proj",
    )(mem2d, wkv)


def _top2_of4(v):
    hi01, lo01 = jnp.maximum(v[0], v[1]), jnp.minimum(v[0], v[1])
    hi23, lo23 = jnp.maximum(v[2], v[3]), jnp.minimum(v[2], v[3])
    return jnp.maximum(hi01, hi23) + jnp.maximum(jnp.minimum(hi01, hi23), jnp.maximum(lo01, lo23))


def _argmax_first(vals):
    best, idx = vals[0], jnp.zeros(vals[0].shape, I32)
    for i in range(1, len(vals)):
        better = vals[i] > best
        best = jnp.where(better, vals[i], best)
        idx = jnp.where(better, i, idx)
    return best, idx


def _xattn_kernel(x_ref, kv_ref, wq_ref, wo_ref, g2_ref, b2_ref, rwh_ref, rwl_ref, rb_ref,
                  x2_ref, xp_ref, rt_ref, cnt_ref, run_s, wq_s, wo_s):
    first = jnp.logical_and(pl.program_id(0) == 0, pl.program_id(1) == 0)
    tq, d = x_ref.shape

    @pl.when(first)
    def _():
        run_s[...] = jnp.zeros(run_s.shape, F32)
        wq_s[...] = wq_ref[...].astype(BF16)
        wo_s[...] = wo_ref[...].astype(BF16)

    slabs = [slice(c * SLAB_ROWS, (c + 1) * SLAB_ROWS) for c in range(tq // SLAB_ROWS)]
    run = run_s[...]
    zs = [_xattn_attend(*_xattn_query(rows, x_ref, wq_s), kv_ref, wo_s) for rows in slabs]
    logits = [_xattn_norm_logits(rows, z, g2_ref, b2_ref, rwh_ref, rwl_ref, x2_ref, xp_ref)
              for rows, z in zip(slabs, zs)]
    for rows, lg in zip(slabs, logits):
        run = _xattn_route_rows(rows, lg, run, rb_ref, rt_ref)
    run_s[...] = run
    cnt_ref[...] = jnp.broadcast_to(run, cnt_ref.shape).astype(I32)


def _xattn_query(rows, x_ref, wq_s):
    hd = x_ref.shape[1] // X_HEADS
    x = x_ref[rows, :]
    return x, _dot(x.astype(BF16), wq_s[...]) * (hd ** -0.5)


def _xattn_attend(x, q, kv_ref, wo_s):
    d = x.shape[1]
    hd = d // X_HEADS
    outs = []
    for hx in range(X_HEADS):
        qh = q[:, hx * hd:(hx + 1) * hd].astype(BF16)
        kh = kv_ref[:, hx * hd:(hx + 1) * hd]
        vh = kv_ref[:, d + hx * hd:d + (hx + 1) * hd]
        s = _dot_nt(qh, kh)
        p = jnp.exp(s - jnp.max(s, axis=-1, keepdims=True))
        o = _dot(p.astype(BF16), vh) / jnp.sum(p, axis=-1, keepdims=True)
        outs.append(o.astype(BF16))
    return ALPHA * x + _dot(jnp.concatenate(outs, axis=1), wo_s[...])


def _xattn_norm_logits(rows, z, g2_ref, b2_ref, rwh_ref, rwl_ref, x2_ref, xp_ref):
    x2 = _layer_norm(z, g2_ref[...], b2_ref[...])
    x2_ref[rows, :] = x2

    xp_ref[rows, :] = _pack_bf16_pairs(x2)

    xh, xl = _split_bf16(x2)
    return _dot_nt(rwh_ref[...], xh) + _dot_nt(rwh_ref[...], xl) + _dot_nt(rwl_ref[...], xh)


def _xattn_route_rows(rows, logits, run, rb_ref, rt_ref):
    tq = rows.stop - rows.start
    e_max = jnp.max(logits, axis=0, keepdims=True)
    ex = jnp.exp(logits - e_max)
    scores = ex / jnp.sum(ex, axis=0, keepdims=True)
    biased = scores + rb_ref[...]
    sc = [scores[e:e + 1, :] for e in range(N_EXPERTS)]
    bi = [biased[e:e + 1, :] for e in range(N_EXPERTS)]
    epg = EXPERTS_PER_GROUP
    gscore = [_top2_of4(bi[g * epg:(g + 1) * epg]) for g in range(N_EXPERT_GROUPS)]
    _, gsel = _argmax_first(gscore)

    def pick(vals):
        return [functools.reduce(lambda acc, g: jnp.where(gsel == g, vals[g * epg + i], acc),
                                 range(1, N_EXPERT_GROUPS), vals[i]) for i in range(epg)]

    in_b = pick(bi)
    in_s = pick(sc)
    _, i0 = _argmax_first(in_b)
    _, i1 = _argmax_first([jnp.where(i0 == i, -jnp.inf, in_b[i]) for i in range(epg)])

    def take(vals, idx):
        return functools.reduce(lambda acc, i: jnp.where(idx == i, vals[i], acc), range(1, epg), vals[0])

    s0, s1 = take(in_s, i0), take(in_s, i1)
    w0, w1 = s0 / (s0 + s1), s1 / (s0 + s1)
    e0, e1 = gsel * epg + i0, gsel * epg + i1

    eid = lax.broadcasted_iota(I32, (N_EXPERTS, tq), 0)
    oh0 = eid == e0
    oh1 = eid == e1
    onehot = jnp.where(jnp.logical_or(oh0, oh1), 1.0, 0.0)
    rr = lax.broadcasted_iota(I32, (tq, tq), 0)
    cc = lax.broadcasted_iota(I32, (tq, tq), 1)
    upper = jnp.where(rr < cc, 1.0, 0.0).astype(BF16)
    prefix = _dot(onehot.astype(BF16), upper) + run
    r0 = jnp.sum(jnp.where(oh0, prefix, 0.0), axis=0, keepdims=True).astype(I32)
    r1 = jnp.sum(jnp.where(oh1, prefix, 0.0), axis=0, keepdims=True).astype(I32)

    zero = jnp.zeros((1, tq), I32)
    rt_ref[:, rows] = jnp.concatenate(
        [e0, e1, r0, r1, pltpu.bitcast(w0, I32), pltpu.bitcast(w1, I32), zero, zero], axis=0)
    return run + jnp.sum(onehot, axis=1, keepdims=True)


def _xattn_route(x1, kv, mem_len, wq, wo, g2, b2, rw_hi, rw_lo, rbias, b0, bsz, s_len):
    t_all, d = x1.shape
    tq = TOKEN_TILE
    nj = s_len // tq
    m = mem_len
    row = lambda bb, jj: (bb * nj + jj, 0)
    return pl.pallas_call(
        _xattn_kernel,
        grid=(bsz, nj),
        in_specs=[
            pl.BlockSpec((tq, d), row),
            pl.BlockSpec((m, 2 * d), lambda bb, jj: (b0 + bb, 0)),
            _const_spec((d, d)), _const_spec((d, d)),
            _const_spec((1, d)), _const_spec((1, d)),
            _const_spec((N_EXPERTS, d)), _const_spec((N_EXPERTS, d)),
            _const_spec((N_EXPERTS, 1)),
        ],
        out_specs=[
            pl.BlockSpec((tq, d), row),
            pl.BlockSpec((tq, d // 2), row),
            pl.BlockSpec((8, tq), lambda bb, jj: (0, bb * nj + jj)),
            _const_spec((N_EXPERTS, 128)),
        ],
        out_shape=[
            jax.ShapeDtypeStruct((t_all, d), F32),
            jax.ShapeDtypeStruct((t_all, d // 2), I32),
            jax.ShapeDtypeStruct((8, t_all), I32),
            jax.ShapeDtypeStruct((N_EXPERTS, 128), I32),
        ],
        scratch_shapes=[pltpu.VMEM((N_EXPERTS, 1), F32), pltpu.VMEM((d, d), BF16), pltpu.VMEM((d, d), BF16)],
        compiler_params=_tc_params(2),
        name="xattn_route",
    )(x1, kv, wq, wo, g2, b2, rw_hi, rw_lo, rbias)


def _sc_mesh():
    return plsc.VectorSubcoreMesh(core_axis_name="c", subcore_axis_name="s")


def _sc_params():
    return pltpu.CompilerParams(needs_layout_passes=False)


def _worker_id():
    return lax.axis_index("s") * lax.axis_size("c") + lax.axis_index("c")


def _sc_dispatch(xp, dest2d, n_rows):
    t_all, width = xp.shape
    chunk = dest2d.shape[1]
    tok_w = t_all // SC_WORKERS
    nch = tok_w // chunk
    slot1 = t_all // chunk
    assert t_all % (SC_WORKERS * chunk * 2) == 0

    def body(x_hbm, dest_hbm, out_hbm, idx0_v, idx1_v, buf0, buf1, sem_r, sem_w):
        wid = _worker_id()
        base = wid * tok_w
        pltpu.sync_copy(dest_hbm.at[pl.ds(wid * nch, nch)], idx0_v)
        pltpu.sync_copy(dest_hbm.at[pl.ds(slot1 + wid * nch, nch)], idx1_v)

        def read(c, buf, k):
            return pltpu.make_async_copy(x_hbm.at[pl.ds(base + c * chunk, chunk)], buf, sem_r.at[k])

        def scatter(c, buf):
            a = pltpu.make_async_copy(buf, out_hbm.at[idx0_v.at[c]], sem_w.at[0])
            b = pltpu.make_async_copy(buf, out_hbm.at[idx1_v.at[c]], sem_w.at[1])
            a.start()
            b.start()
            a.wait()
            b.wait()

        read(0, buf0, 0).start()

        @pl.loop(0, nch // 2)
        def _(g):
            c = 2 * g
            read(c + 1, buf1, 1).start()
            read(c, buf0, 0).wait()
            scatter(c, buf0)

            @pl.when(c + 2 < nch)
            def _():
                read(c + 2, buf0, 0).start()

            read(c + 1, buf1, 1).wait()
            scatter(c + 1, buf1)

    return pl.kernel(
        body,
        out_type=jax.ShapeDtypeStruct((n_rows, width), xp.dtype),
        mesh=_sc_mesh(),
        scratch_types=[
            pltpu.VMEM((nch, chunk), I32),
            pltpu.VMEM((nch, chunk), I32),
            pltpu.VMEM((chunk, width), xp.dtype),
            pltpu.VMEM((chunk, width), xp.dtype),
            pltpu.SemaphoreType.DMA((2,)),
            pltpu.SemaphoreType.DMA((2,)),
        ],
        compiler_params=_sc_params(),
        name="sc_dispatch",
    )(xp, dest2d)


def _sc_combine(ys, dest2d):
    n_idx_rows, chunk = dest2d.shape
    width = ys.shape[1]
    nch = n_idx_rows // SC_WORKERS
    assert n_idx_rows % (SC_WORKERS * 2) == 0

    def body(y_hbm, dest_hbm, out_hbm, idx_v, buf0, buf1, sem_g):
        wid = _worker_id()
        base = wid * nch * chunk
        pltpu.sync_copy(dest_hbm.at[pl.ds(wid * nch, nch)], idx_v)

        def gather(c, buf, k):
            return pltpu.make_async_copy(y_hbm.at[idx_v.at[c]], buf, sem_g.at[k])

        def write(c, buf):
            pltpu.sync_copy(buf, out_hbm.at[pl.ds(base + c * chunk, chunk)])

        gather(0, buf0, 0).start()

        @pl.loop(0, nch // 2)
        def _(g):
            c = 2 * g
            gather(c + 1, buf1, 1).start()
            gather(c, buf0, 0).wait()
            write(c, buf0)

            @pl.when(c + 2 < nch)
            def _():
                gather(c + 2, buf0, 0).start()

            gather(c + 1, buf1, 1).wait()
            write(c + 1, buf1)

    return pl.kernel(
        body,
        out_type=jax.ShapeDtypeStruct((n_idx_rows * chunk, width), ys.dtype),
        mesh=_sc_mesh(),
        scratch_types=[
            pltpu.VMEM((nch, chunk), I32),
            pltpu.VMEM((chunk, width), ys.dtype),
            pltpu.VMEM((chunk, width), ys.dtype),
            pltpu.SemaphoreType.DMA((2,)),
        ],
        compiler_params=_sc_params(),
        name="sc_combine",
    )(ys, dest2d)


def _pack_bf16_pairs(v):
    half = v.shape[1] // 2
    lo = pltpu.bitcast(v[:, :half].astype(BF16).astype(F32), jnp.uint32) >> 16
    hi = pltpu.bitcast(v[:, half:].astype(BF16).astype(F32), jnp.uint32) & jnp.uint32(0xFFFF0000)
    return pltpu.bitcast(hi | lo, I32)


def _unpack_bf16_pairs(w):
    w = pltpu.bitcast(w, jnp.uint32)
    lo = pltpu.bitcast(w << 16, F32)
    hi = pltpu.bitcast(w & jnp.uint32(0xFFFF0000), F32)
    return jnp.concatenate([lo, hi], axis=1)


def _ffn_kernel(be_ref, nv_ref, nu_ref, nxt_ref, xs_ref, wg_hbm, wu_hbm, wd_hbm, o_ref,
                wg_f, wu_f, wd_f, wg_s, wu_s, wd_s, sem):
    i = pl.program_id(0)

    def fetch(e):
        return (pltpu.make_async_copy(wg_hbm.at[e], wg_f, sem.at[0]),
                pltpu.make_async_copy(wu_hbm.at[e], wu_f, sem.at[1]),
                pltpu.make_async_copy(wd_hbm.at[e], wd_f, sem.at[2]))

    @pl.when(nxt_ref[i] >= 0)
    def _():
        @pl.when(i == 0)
        def _():
            for cp in fetch(be_ref[0]):
                cp.start()

        for cp in fetch(be_ref[i]):
            cp.wait()
        wg_s[...] = wg_f[...].astype(BF16)
        wu_s[...] = wu_f[...].astype(BF16)
        wd_s[...] = wd_f[...].astype(BF16)

        @pl.when(nxt_ref[i] < N_EXPERTS)
        def _():
            for cp in fetch(nxt_ref[i]):
                cp.start()

    @pl.when(i < nu_ref[0])
    def _():
        live = lax.broadcasted_iota(I32, xs_ref.shape, 0) < nv_ref[i]
        xb = _unpack_bf16_pairs(jnp.where(live, xs_ref[...], 0)).astype(BF16)
        act = jax.nn.silu(_dot(xb, wg_s[...])) * _dot(xb, wu_s[...])
        o_ref[...] = _pack_bf16_pairs(_dot(act.astype(BF16), wd_s[...]))


def _expert_ffn(xs, blk_expert, blk_valid, n_used, blk_next, w_gate, w_up, w_down):
    n_rows, half = xs.shape
    d = 2 * half
    de = w_gate.shape[2]
    bm = EXPERT_ROWS
    rows = lambda i, be, nv, nu, nx: (jnp.minimum(i, nu[0] - 1), 0)
    hbm = pl.BlockSpec(memory_space=pl.ANY)
    return pl.pallas_call(
        _ffn_kernel,
        grid_spec=pltpu.PrefetchScalarGridSpec(
            num_scalar_prefetch=4,
            grid=(n_rows // bm,),
            in_specs=[pl.BlockSpec((bm, half), rows), hbm, hbm, hbm],
            out_specs=pl.BlockSpec((bm, half), rows),
            scratch_shapes=[
                pltpu.VMEM((d, de), F32), pltpu.VMEM((d, de), F32), pltpu.VMEM((de, d), F32),
                pltpu.VMEM((d, de), BF16), pltpu.VMEM((d, de), BF16), pltpu.VMEM((de, d), BF16),
                pltpu.SemaphoreType.DMA((3,)),
            ],
        ),
        out_shape=jax.ShapeDtypeStruct((n_rows, half), I32),
        compiler_params=_tc_params(1),
        name="expert_ffn",
    )(blk_expert, blk_valid, n_used, blk_next, xs, w_gate, w_up, w_down)


def _combine_kernel(x_ref, y0_ref, y1_ref, rt_ref, g_ref, b_ref, o_ref):
    o_ref[...] = _expert_combine_ln(x_ref[...], y0_ref[...], y1_ref[...], rt_ref[...], g_ref[...], b_ref[...])


def _combine_kernel_into(x_ref, y0_ref, y1_ref, rt_ref, g_ref, b_ref, full_ref, o_ref):
    del full_ref
    _combine_kernel(x_ref, y0_ref, y1_ref, rt_ref, g_ref, b_ref, o_ref)


def _combine_ln(x2, y01, rt, g3, b3, into=None, row0=0, full_rows=None):
    t_all, d = x2.shape
    tq = TOKEN_TILE
    nt = t_all // tq
    blk0 = row0 // tq if full_rows else 0
    in_specs = [
        pl.BlockSpec((tq, d), lambda i: (i, 0)),
        pl.BlockSpec((tq, d // 2), lambda i: (i, 0)),
        pl.BlockSpec((tq, d // 2), lambda i: (i + nt, 0)),
        pl.BlockSpec((8, tq), lambda i: (0, i)),
        _const_spec((1, d)), _const_spec((1, d)),
    ]
    args = [x2, y01, y01, rt, g3, b3]
    body, aliases = _combine_kernel, {}
    if into is not None:
        in_specs.append(pl.BlockSpec(memory_space=pl.ANY))
        args.append(into)
        body, aliases = _combine_kernel_into, {len(args) - 1: 0}
    return pl.pallas_call(
        body,
        grid=(nt,),
        in_specs=in_specs,
        out_specs=pl.BlockSpec((tq, d), lambda i: (i + blk0, 0)),
        out_shape=jax.ShapeDtypeStruct((full_rows or t_all, d), F32),
        input_output_aliases=aliases,
        compiler_params=_tc_params(1),
        name="combine_ln",
    )(*args)


def _routing_tables(rt, counts):
    bm = EXPERT_ROWS
    t_all = rt.shape[1]
    n_rows = 2 * t_all + N_EXPERTS * bm
    cnt = counts[:, 0]
    padded = (cnt + bm - 1) // bm * bm
    ends = jnp.cumsum(padded)
    offs = ends - padded
    experts = rt[0:2]
    off_tok = jnp.sum(jnp.where(experts[None] == jnp.arange(N_EXPERTS, dtype=I32)[:, None, None],
                                offs[:, None, None], 0), axis=0)
    dest2d = (off_tok + rt[2:4]).reshape(-1, SC_CHUNK).astype(I32)
    blk_start = jnp.arange(n_rows // bm, dtype=I32) * bm
    blk_expert = jnp.minimum(jnp.sum(blk_start[:, None] >= ends[None, :], axis=1), N_EXPERTS - 1).astype(I32)
    live_end = jnp.sum(jnp.where(blk_expert[:, None] == jnp.arange(N_EXPERTS, dtype=I32)[None, :],
                                 (offs + cnt)[None, :], 0), axis=1)
    blk_valid = jnp.clip(live_end - blk_start, 0, bm).astype(I32)
    n_used = (ends[-1:] // bm).astype(I32)
    eid = jnp.arange(N_EXPERTS, dtype=I32)
    later_present = jnp.logical_and(eid[None, :] > eid[:, None], (cnt > 0)[None, :])
    next_present = jnp.min(jnp.where(later_present, eid[None, :], N_EXPERTS), axis=1)
    next_of_blk = jnp.sum(jnp.where(blk_expert[:, None] == eid[None, :], next_present[None, :], 0), axis=1)
    prev_expert = jnp.concatenate([jnp.full((1,), -1, I32), blk_expert[:-1]])
    is_first = jnp.logical_and(blk_start < ends[-1], blk_expert != prev_expert)
    blk_next = jnp.where(is_first, next_of_blk, -1).astype(I32)
    return dest2d, blk_expert, blk_valid, n_used, blk_next, n_rows


def _layer_tail(x1, kv, mem_len, p, router, b0, bsz, s_len):
    rw_hi, rw_lo, rbias = router
    x2, xp, rt, counts = _xattn_route(x1, kv, mem_len, p["xq"], p["xo"], p["ln2_g"], p["ln2_b"],
                                      rw_hi, rw_lo, rbias, b0, bsz, s_len)
    dest2d, blk_expert, blk_valid, n_used, blk_next, n_rows = _routing_tables(rt, counts)
    xs = _sc_dispatch(xp, dest2d, n_rows)
    ys = _expert_ffn(xs, blk_expert, blk_valid, n_used, blk_next, p["e_gate"], p["e_up"], p["e_down"])
    y01 = _sc_combine(ys, dest2d)
    return x2, y01, rt


def _row(v):
    return v.reshape(1, -1).astype(F32)


def _common_params(xq, xkv, xo, ln2_g, ln2_b, e_gate, e_up, e_down, ln3_g, ln3_b):
    return dict(xq=xq, xkv=xkv, xo=xo, ln2_g=_row(ln2_g), ln2_b=_row(ln2_b),
                e_gate=e_gate, e_up=e_up, e_down=e_down, ln3_g=_row(ln3_g), ln3_b=_row(ln3_b))


def kernel(x, mem, positions, router_w, router_bias, l0_w_in, l0_sinks, l0_sgu_ln_g, l0_sgu_ln_b, l0_sgu_w, l0_sgu_b, l0_w_out, l0_ln1_g, l0_ln1_b, l0_xq, l0_xkv, l0_xo, l0_ln2_g, l0_ln2_b, l0_e_gate, l0_e_up, l0_e_down, l0_ln3_g, l0_ln3_b, l1_w_in, l1_pool_w, l1_pool_scale, l1_w_out, l1_ln1_g, l1_ln1_b, l1_xq, l1_xkv, l1_xo, l1_ln2_g, l1_ln2_b, l1_e_gate, l1_e_up, l1_e_down, l1_ln3_g, l1_ln3_b):
    bsz, s_len, d = x.shape
    assert s_len % TOKEN_TILE == 0 and s_len % MIXER0_TILE == 0 and MIXER0_TILE % BLOCK == 0
    xt = x.reshape(bsz * s_len, d)
    mem2d = mem.reshape(-1, d)

    rw_t = router_w.T.astype(F32)
    rw_hi = rw_t.astype(BF16)
    rw_lo = (rw_t - rw_hi.astype(F32)).astype(BF16)
    router = (rw_hi, rw_lo, router_bias.reshape(-1, 1).astype(F32))

    half = ROPE_DIM // 2
    inv_freq = (ROPE_THETA ** (-(jnp.arange(half, dtype=F32) * 2.0 / ROPE_DIM))).reshape(half, 1)
    etab_np = np.zeros((128, 3 * 128), np.float32)
    cbase_np = np.ones((1, 128), np.float32)
    for ln in range(128):
        dd = ln % HEAD_DIM
        if dd < ROPE_DIM:
            cbase_np[0, ln] = 0.0
            etab_np[[dd % half, half + dd % half], ln] = 1.0
            if dd >= half:
                etab_np[[2 * half + dd - half, 3 * half + dd - half], 128 + ln] = 1.0
            else:
                etab_np[[2 * half + dd, 3 * half + dd], 256 + ln] = -1.0
    etab = jnp.asarray(etab_np, BF16)
    cbase = jnp.asarray(cbase_np)
    pos_row = positions.reshape(1, -1).astype(I32)
    grp = jnp.arange(B_WIDTH) // B_GROUP_DIM
    gsum = (grp[:, None] == grp[None, :]).astype(BF16)
    bs_full = jnp.repeat(l0_sgu_b.T.astype(F32), B_GROUP_DIM, axis=1)

    p0 = _common_params(l0_xq, l0_xkv, l0_xo, l0_ln2_g, l0_ln2_b, l0_e_gate, l0_e_up, l0_e_down,
                        l0_ln3_g, l0_ln3_b)
    p1 = _common_params(l1_xq, l1_xkv, l1_xo, l1_ln2_g, l1_ln2_b, l1_e_gate, l1_e_up, l1_e_down,
                        l1_ln3_g, l1_ln3_b)
    kv0 = _kv_proj(mem2d, p0["xkv"], bsz)
    kv1 = _kv_proj(mem2d, p1["xkv"], bsz)
    mem_len = mem.shape[1]

    n_split = BATCH_SPLIT if bsz % BATCH_SPLIT == 0 else 1
    nb = bsz // n_split
    out = None
    for part in range(n_split):
        b0 = part * nb
        x1 = _mixer0(xt, pos_row, l0_sinks.astype(F32), l0_w_in, inv_freq, etab, cbase, gsum,
                     _row(l0_sgu_ln_g), _row(l0_sgu_ln_b), l0_sgu_w.astype(F32), bs_full,
                     l0_w_out, _row(l0_ln1_g), _row(l0_ln1_b), b0, nb, s_len)
        x2, y01, rt = _layer_tail(x1, kv0, mem_len, p0, router, b0, nb, s_len)
        x1 = _mixer1(x2, y01, rt, p0["ln3_g"], p0["ln3_b"], l1_w_in, l1_pool_w, _row(l1_pool_scale),
                     l1_w_out, _row(l1_ln1_g), _row(l1_ln1_b), nb, s_len)
        x2, y01, rt = _layer_tail(x1, kv1, mem_len, p1, router, b0, nb, s_len)
        out = _combine_ln(x2, y01, rt, p1["ln3_g"], p1["ln3_b"], into=out, row0=b0 * s_len,
                          full_rows=bsz * s_len)
    return out.reshape(bsz, s_len, d)
```

```python
import functools

import numpy as np
import jax
import jax.numpy as jnp
from jax import lax
from jax.experimental import pallas as pl
from jax.experimental.pallas import tpu as pltpu
from jax.experimental.pallas import tpu_sc as plsc

F32 = jnp.float32
BF16 = jnp.bfloat16
I32 = jnp.int32

DEPTH = 2
ALPHA = (2.0 * DEPTH) ** 0.25
LN_EPS = 1e-5

HEAD_DIM = 64
A_Q_HEADS = 8
A_KV_HEADS = 2
A_GROUP = A_Q_HEADS // A_KV_HEADS
BLOCK = 128
ROPE_THETA = 500000.0
ROPE_DIM = HEAD_DIM // 4
A_WIDTH = A_Q_HEADS * HEAD_DIM
KV_WIDTH = A_KV_HEADS * HEAD_DIM
B_GROUPS = 8
B_GROUP_DIM = 64
B_WIDTH = B_GROUPS * B_GROUP_DIM
POOL_WINDOWS = (2, 4, 8, 16)
POOL_HALO = 16
X_HEADS = 4
N_EXPERTS = 16
N_EXPERT_GROUPS = 4
EXPERTS_PER_GROUP = 4

TOKEN_TILE = 1024
MIXER0_TILE = 512
SLAB_ROWS = 512
EXPERT_ROWS = 512
BATCH_SPLIT = 2
SC_WORKERS = 32
SC_CHUNK = 64
VMEM_LIMIT = 56 * 1024 * 1024
NEG_BIG = -1e30


def _layer_norm(z, g, b):
    mu = jnp.mean(z, axis=-1, keepdims=True)
    d = z - mu
    var = jnp.mean(d * d, axis=-1, keepdims=True)
    return d * lax.rsqrt(var + LN_EPS) * g + b


def _dot(a, b):
    return jnp.dot(a, b, preferred_element_type=F32)


def _dot_nt(a, b):
    return lax.dot_general(a, b, (((1,), (1,)), ((), ())), preferred_element_type=F32)


def _split_bf16(v):
    hi = v.astype(BF16)
    lo = (v - hi.astype(F32)).astype(BF16)
    return hi, lo


def _tc_params(n_axes):
    return pltpu.CompilerParams(dimension_semantics=("arbitrary",) * n_axes,
                                vmem_limit_bytes=VMEM_LIMIT)


def _const_spec(shape):
    nd = len(shape)
    return pl.BlockSpec(shape, lambda *_: (0,) * nd, pipeline_mode=pl.Buffered(1))

def _mixer0_kernel(sinks_ref, x_ref, pos_ref, win_ref, invf_ref, etab_ref, cbase_ref, gsum_ref,
                   lng_ref, lnb_ref, ws_ref, bs_ref, wout_ref, g1_ref, b1_ref,
                   o_ref, q_s, kv_s, u_s, vn_s, mix_s, wt_s, win_s, wout_s):
    b = pl.program_id(0)
    j = pl.program_id(1)
    tq = x_ref.shape[0]
    nblk = tq // BLOCK
    kvw = kv_s.shape[1]

    @pl.when(jnp.logical_and(b == 0, j == 0))
    def _():
        win_s[...] = win_ref[...].astype(BF16)
        wout_s[...] = wout_ref[...].astype(BF16)
        r = lax.broadcasted_iota(I32, (BLOCK, BLOCK), 0)
        c = lax.broadcasted_iota(I32, (BLOCK, BLOCK), 1)
        for g in range(B_GROUPS):
            wt_s[g] = jnp.where(c <= r, ws_ref[g], 0.0).astype(BF16)

    @pl.when(j == 0)
    def _():
        kv_s[0:BLOCK, :] = jnp.zeros((BLOCK, kvw), BF16)

    x = x_ref[...]
    h = _dot(x.astype(BF16), win_s[...])

    ang = invf_ref[...] * pos_ref[...].astype(F32)
    c8 = jnp.cos(ang)
    s8 = jnp.sin(ang)
    c8h = c8.astype(BF16).astype(F32)
    s8h = s8.astype(BF16).astype(F32)
    stack = jnp.concatenate([c8h, c8 - c8h, s8h, s8 - s8h, jnp.zeros((128 - 4 * 8, tq), F32)], axis=0)
    tabs = _dot(stack.T.astype(BF16), etab_ref[...])
    cs = tabs[:, 0:128] + cbase_ref[...]
    sa = tabs[:, 128:256]
    sb = tabs[:, 256:384]

    def rope(t):
        return t * cs + pltpu.roll(t, ROPE_DIM // 2, 1) * sa + pltpu.roll(t, 128 - ROPE_DIM // 2, 1) * sb

    for c in range(A_WIDTH // 128):
        t = h[:, c * 128:(c + 1) * 128] * (HEAD_DIM ** -0.5)
        q_s[:, c * 128:(c + 1) * 128] = rope(t).astype(BF16)
    c1 = A_WIDTH
    c2 = c1 + KV_WIDTH
    c3 = c2 + KV_WIDTH
    c4 = c3 + B_WIDTH
    low = lax.broadcasted_iota(I32, (tq, 128), 1) < HEAD_DIM
    kr = rope(h[:, c1:c2])
    kx = pltpu.roll(kr, HEAD_DIM, 1)
    vr = h[:, c2:c3]
    vx = pltpu.roll(vr, HEAD_DIM, 1)
    kv_cols = [jnp.where(low, kr, kx), jnp.where(low, kx, kr),
               jnp.where(low, vr, 0.0), jnp.where(low, 0.0, vx),
               jnp.where(low, vx, 0.0), jnp.where(low, 0.0, vr)]
    for c, col in enumerate(kv_cols):
        kv_s[BLOCK:, c * 128:(c + 1) * 128] = col.astype(BF16)

    u_s[...] = jax.nn.gelu(h[:, c3:c4])
    v = jax.nn.gelu(h[:, c4:])
    gsum = gsum_ref[...]
    mean = _dot(v.astype(BF16), gsum) * (1.0 / B_GROUP_DIM)
    d = v - mean
    var = _dot((d * d).astype(BF16), gsum) * (1.0 / B_GROUP_DIM)
    vn_s[...] = (d * lax.rsqrt(var + LN_EPS) * lng_ref[...] + lnb_ref[...]).astype(BF16)

    qi = lax.broadcasted_iota(I32, (BLOCK, 2 * BLOCK), 0)
    kj = lax.broadcasted_iota(I32, (BLOCK, 2 * BLOCK), 1)
    rel = qi + BLOCK - kj
    band = jnp.logical_and(rel >= 0, rel < BLOCK)
    low_q = lax.broadcasted_iota(I32, (BLOCK, 128), 1) < HEAD_DIM
    low_k = lax.broadcasted_iota(I32, (2 * BLOCK, 128), 1) < HEAD_DIM
    ones_lo = jnp.where(low_k, 1.0, 0.0).astype(BF16)
    ones_hi = jnp.where(low_k, 0.0, 1.0).astype(BF16)
    zero_q = jnp.zeros((BLOCK, 128), BF16)

    def block_body(n):
        r0 = n * BLOCK
        kv = kv_s[pl.ds(r0, 2 * BLOCK), :]
        qb = q_s[pl.ds(r0, BLOCK), :]
        valid = jnp.logical_and(band, kj >= jnp.where(j == 0, BLOCK, 0)) if n == 0 else band
        heads = [(c, half) for c in range(A_WIDTH // 128) for half in range(2)]
        scores = []
        for c, half in heads:
            qp = qb[:, c * 128:(c + 1) * 128]
            qm = jnp.where(low_q, qp, zero_q) if half == 0 else jnp.where(low_q, zero_q, qp)
            hk = (2 * c) // A_GROUP
            scores.append(_dot_nt(qm, kv[:, hk * 128:(hk + 1) * 128]))
        vnb = vn_s[pl.ds(r0, BLOCK), :]
        parts = []
        for c in range(B_WIDTH // 128):
            vp = vnb[:, c * 128:(c + 1) * 128]
            parts.append(_dot(wt_s[2 * c], jnp.where(low_q, vp, zero_q))
                         + _dot(wt_s[2 * c + 1], jnp.where(low_q, zero_q, vp)))
        res = [None] * (A_WIDTH // 128)
        esink = {}
        for (c, half), sc in zip(heads, scores):
            hk = (2 * c) // A_GROUP
            vcol = kv[:, (2 + 2 * hk + half) * 128:(3 + 2 * hk + half) * 128]
            vm = jnp.concatenate([vcol, ones_lo if half == 0 else ones_hi], axis=1)
            s = jnp.where(valid, sc, NEG_BIG)
            sink = sinks_ref[2 * c + half]
            m = jnp.maximum(jnp.max(s, axis=-1, keepdims=True), sink)
            pv = _dot(jnp.exp(s - m).astype(BF16), vm)
            res[c] = pv if res[c] is None else res[c] + pv
            esink[c, half] = jnp.exp(sink - m)
        for c in range(A_WIDTH // 128):
            den = res[c][:, 128:] + jnp.where(low_q, esink[c, 0], esink[c, 1])
            mix_s[pl.ds(r0, BLOCK), c * 128:(c + 1) * 128] = (res[c][:, :128] / den).astype(BF16)
        mixed = jnp.concatenate(parts, axis=1) + bs_ref[...]
        mix_s[pl.ds(r0, BLOCK), A_WIDTH:] = (u_s[pl.ds(r0, BLOCK), :] * mixed).astype(BF16)

    for n in range(nblk):
        block_body(n)
    kv_s[0:BLOCK, :] = kv_s[tq:tq + BLOCK, :]

    z = ALPHA * x + _dot(mix_s[...], wout_s[...])
    o_ref[...] = _layer_norm(z, g1_ref[...], b1_ref[...])


def _mixer0(x, pos_row, sinks, w_in, invf, etab, cbase, gsum, lng, lnb, w_s, bs_full, w_out, g1, b1,
            b0, bsz, s_len):
    d = x.shape[1]
    t_all = bsz * s_len
    tq = MIXER0_TILE
    nj = s_len // tq
    row = lambda bb, jj: (bb * nj + jj, 0)
    in_w = w_in.shape[1]
    return pl.pallas_call(
        _mixer0_kernel,
        grid=(bsz, nj),
        in_specs=[
            pl.BlockSpec(memory_space=pltpu.SMEM),
            pl.BlockSpec((tq, d), lambda bb, jj: ((b0 + bb) * nj + jj, 0)),
            pl.BlockSpec((1, tq), lambda bb, jj: (0, (b0 + bb) * nj + jj)),
            _const_spec((d, in_w)),
            _const_spec((ROPE_DIM // 2, 1)), _const_spec((128, 3 * 128)), _const_spec((1, 128)),
            _const_spec((B_WIDTH, B_WIDTH)),
            _const_spec((1, B_WIDTH)), _const_spec((1, B_WIDTH)),
            _const_spec((B_GROUPS, BLOCK, BLOCK)),
            _const_spec((BLOCK, B_WIDTH)),
            _const_spec((A_WIDTH + B_WIDTH, d)),
            _const_spec((1, d)), _const_spec((1, d)),
        ],
        out_specs=pl.BlockSpec((tq, d), row),
        out_shape=jax.ShapeDtypeStruct((t_all, d), F32),
        scratch_shapes=[
            pltpu.VMEM((tq, A_WIDTH), BF16),
            pltpu.VMEM((tq + BLOCK, 6 * 128), BF16),
            pltpu.VMEM((tq, B_WIDTH), F32),
            pltpu.VMEM((tq, B_WIDTH), BF16),
            pltpu.VMEM((tq, A_WIDTH + B_WIDTH), BF16),
            pltpu.VMEM((B_GROUPS, BLOCK, BLOCK), BF16),
            pltpu.VMEM((d, in_w), BF16),
            pltpu.VMEM((A_WIDTH + B_WIDTH, d), BF16),
        ],
        compiler_params=_tc_params(2),
        name="mixer0",
    )(sinks, x, pos_row, w_in, invf, etab, cbase, gsum, lng, lnb, w_s, bs_full, w_out, g1, b1)


def _expert_combine_ln(x2, y0_packed, y1_packed, rt, g, b):
    wt = pltpu.bitcast(rt, F32).T
    y = wt[:, 4:5] * _unpack_bf16_pairs(y0_packed) + wt[:, 5:6] * _unpack_bf16_pairs(y1_packed)
    return _layer_norm(ALPHA * x2 + y, g, b)


def _mixer1_kernel(x2_ref, y0_ref, y1_ref, rt_ref, g3_ref, b3_ref,
                   win_ref, pw_ref, ps_ref, wout_ref, g1_ref, b1_ref, o_ref,
                   h_s, mp_s, win_s, pw_s, wout_s):
    j = pl.program_id(1)
    tq = x2_ref.shape[0]
    gw = x2_ref.shape[1] // len(POOL_WINDOWS)
    slabs = [slice(c * SLAB_ROWS, (c + 1) * SLAB_ROWS) for c in range(tq // SLAB_ROWS)]

    @pl.when(jnp.logical_and(pl.program_id(0) == 0, j == 0))
    def _():
        win_s[...] = win_ref[...].astype(BF16)
        pw_s[...] = pw_ref[...].astype(BF16)
        wout_s[...] = wout_ref[...].astype(BF16)

    @pl.when(j == 0)
    def _():
        h_s[0:POOL_HALO, :] = jnp.zeros((POOL_HALO, h_s.shape[1]), F32)

    xs = []
    for rows in slabs:
        x = _expert_combine_ln(x2_ref[rows, :], y0_ref[rows, :], y1_ref[rows, :], rt_ref[:, rows],
                               g3_ref[...], b3_ref[...])
        h_s[POOL_HALO + rows.start:POOL_HALO + rows.stop, :] = _dot(x.astype(BF16), win_s[...])
        xs.append(x)

    for rows, x in zip(slabs, xs):
        n = rows.stop - rows.start
        t_pos = j * tq + rows.start + lax.broadcasted_iota(I32, (n, 1), 0)
        for g, win in enumerate(POOL_WINDOWS):
            lo, hi = g * gw, (g + 1) * gw
            ext = h_s[rows.start:rows.stop + POOL_HALO, lo:hi]
            acc = ext
            shift = 1
            while shift < win:
                acc = acc + pltpu.roll(acc, shift, 0)
                shift *= 2
            count = jnp.minimum(t_pos + 1, win).astype(F32)
            pooled = acc[POOL_HALO:, :] / count - ext[POOL_HALO:, :]
            mapped = _dot(pooled.astype(BF16), pw_s[g])
            mp_s[rows, lo:hi] = (mapped * ps_ref[:, lo:hi]).astype(BF16)
        z = ALPHA * x + _dot(mp_s[rows, :], wout_s[...])
        o_ref[rows, :] = _layer_norm(z, g1_ref[...], b1_ref[...])
    h_s[0:POOL_HALO, :] = h_s[tq:tq + POOL_HALO, :]


def _mixer1(x2, y01, rt, g3, b3, w_in, pool_w, pool_scale, w_out, g1, b1, bsz, s_len):
    t_all, d = x2.shape
    tq = TOKEN_TILE
    nj = s_len // tq
    nt = bsz * nj
    row = lambda bb, jj: (bb * nj + jj, 0)
    ng = len(POOL_WINDOWS)
    return pl.pallas_call(
        _mixer1_kernel,
        grid=(bsz, nj),
        in_specs=[
            pl.BlockSpec((tq, d), row),
            pl.BlockSpec((tq, d // 2), row),
            pl.BlockSpec((tq, d // 2), lambda bb, jj: (nt + bb * nj + jj, 0)),
            pl.BlockSpec((8, tq), lambda bb, jj: (0, bb * nj + jj)),
            _const_spec((1, d)), _const_spec((1, d)),
            _const_spec((d, d)),
            _const_spec((ng, d // ng, d // ng)),
            _const_spec((1, d)),
            _const_spec((d, d)),
            _const_spec((1, d)), _const_spec((1, d)),
        ],
        out_specs=pl.BlockSpec((tq, d), row),
        out_shape=jax.ShapeDtypeStruct((t_all, d), F32),
        scratch_shapes=[pltpu.VMEM((tq + POOL_HALO, d), F32), pltpu.VMEM((tq, d), BF16),
                        pltpu.VMEM((d, d), BF16), pltpu.VMEM((ng, d // ng, d // ng), BF16),
                        pltpu.VMEM((d, d), BF16)],
        compiler_params=_tc_params(2),
        name="mixer1",
    )(x2, y01, y01, rt, g3, b3, w_in, pool_w, pool_scale, w_out, g1, b1)


def _kv_kernel(mem_ref, w_ref, o_ref, w_s):
    @pl.when(pl.program_id(0) == 0)
    def _():
        w_s[...] = w_ref[...].astype(BF16)

    o_ref[...] = _dot(mem_ref[...].astype(BF16), w_s[...]).astype(BF16)


def _kv_proj(mem2d, wkv, bsz):
    rows, d = mem2d.shape
    m = rows // bsz
    return pl.pallas_call(
        _kv_kernel,
        grid=(bsz,),
        in_specs=[pl.BlockSpec((m, d), lambda i: (i, 0)), _const_spec(wkv.shape)],
        out_specs=pl.BlockSpec((m, wkv.shape[1]), lambda i: (i, 0)),
        out_shape=jax.ShapeDtypeStruct((rows, wkv.shape[1]), BF16),
        scratch_shapes=[pltpu.VMEM(wkv.shape, BF16)],
        compiler_params=_tc_params(1),
        name="kv_proj",
    )(mem2d, wkv)


def _top2_of4(v):
    hi01, lo01 = jnp.maximum(v[0], v[1]), jnp.minimum(v[0], v[1])
    hi23, lo23 = jnp.maximum(v[2], v[3]), jnp.minimum(v[2], v[3])
    return jnp.maximum(hi01, hi23) + jnp.maximum(jnp.minimum(hi01, hi23), jnp.maximum(lo01, lo23))


def _argmax_first(vals):
    best, idx = vals[0], jnp.zeros(vals[0].shape, I32)
    for i in range(1, len(vals)):
        better = vals[i] > best
        best = jnp.where(better, vals[i], best)
        idx = jnp.where(better, i, idx)
    return best, idx


def _xattn_kernel(x_ref, kv_ref, wq_ref, wo_ref, g2_ref, b2_ref, rwh_ref, rwl_ref, rb_ref,
                  x2_ref, xp_ref, rt_ref, cnt_ref, run_s, wq_s, wo_s):
    first = jnp.logical_and(pl.program_id(0) == 0, pl.program_id(1) == 0)
    tq, d = x_ref.shape

    @pl.when(first)
    def _():
        run_s[...] = jnp.zeros(run_s.shape, F32)
        wq_s[...] = wq_ref[...].astype(BF16)
        wo_s[...] = wo_ref[...].astype(BF16)

    slabs = [slice(c * SLAB_ROWS, (c + 1) * SLAB_ROWS) for c in range(tq // SLAB_ROWS)]
    run = run_s[...]
    zs = [_xattn_attend(*_xattn_query(rows, x_ref, wq_s), kv_ref, wo_s) for rows in slabs]
    logits = [_xattn_norm_logits(rows, z, g2_ref, b2_ref, rwh_ref, rwl_ref, x2_ref, xp_ref)
              for rows, z in zip(slabs, zs)]
    for rows, lg in zip(slabs, logits):
        run = _xattn_route_rows(rows, lg, run, rb_ref, rt_ref)
    run_s[...] = run
    cnt_ref[...] = jnp.broadcast_to(run, cnt_ref.shape).astype(I32)


def _xattn_query(rows, x_ref, wq_s):
    hd = x_ref.shape[1] // X_HEADS
    x = x_ref[rows, :]
    return x, _dot(x.astype(BF16), wq_s[...]) * (hd ** -0.5)


def _xattn_attend(x, q, kv_ref, wo_s):
    d = x.shape[1]
    hd = d // X_HEADS
    outs = []
    for hx in range(X_HEADS):
        qh = q[:, hx * hd:(hx + 1) * hd].astype(BF16)
        kh = kv_ref[:, hx * hd:(hx + 1) * hd]
        vh = kv_ref[:, d + hx * hd:d + (hx + 1) * hd]
        s = _dot_nt(qh, kh)
        p = jnp.exp(s - jnp.max(s, axis=-1, keepdims=True))
        o = _dot(p.astype(BF16), vh) / jnp.sum(p, axis=-1, keepdims=True)
        outs.append(o.astype(BF16))
    return ALPHA * x + _dot(jnp.concatenate(outs, axis=1), wo_s[...])


def _xattn_norm_logits(rows, z, g2_ref, b2_ref, rwh_ref, rwl_ref, x2_ref, xp_ref):
    x2 = _layer_norm(z, g2_ref[...], b2_ref[...])
    x2_ref[rows, :] = x2

    xp_ref[rows, :] = _pack_bf16_pairs(x2)

    xh, xl = _split_bf16(x2)
    return _dot_nt(rwh_ref[...], xh) + _dot_nt(rwh_ref[...], xl) + _dot_nt(rwl_ref[...], xh)


def _xattn_route_rows(rows, logits, run, rb_ref, rt_ref):
    tq = rows.stop - rows.start
    e_max = jnp.max(logits, axis=0, keepdims=True)
    ex = jnp.exp(logits - e_max)
    scores = ex / jnp.sum(ex, axis=0, keepdims=True)
    biased = scores + rb_ref[...]
    sc = [scores[e:e + 1, :] for e in range(N_EXPERTS)]
    bi = [biased[e:e + 1, :] for e in range(N_EXPERTS)]
    epg = EXPERTS_PER_GROUP
    gscore = [_top2_of4(bi[g * epg:(g + 1) * epg]) for g in range(N_EXPERT_GROUPS)]
    _, gsel = _argmax_first(gscore)

    def pick(vals):
        return [functools.reduce(lambda acc, g: jnp.where(gsel == g, vals[g * epg + i], acc),
                                 range(1, N_EXPERT_GROUPS), vals[i]) for i in range(epg)]

    in_b = pick(bi)
    in_s = pick(sc)
    _, i0 = _argmax_first(in_b)
    _, i1 = _argmax_first([jnp.where(i0 == i, -jnp.inf, in_b[i]) for i in range(epg)])

    def take(vals, idx):
        return functools.reduce(lambda acc, i: jnp.where(idx == i, vals[i], acc), range(1, epg), vals[0])

    s0, s1 = take(in_s, i0), take(in_s, i1)
    w0, w1 = s0 / (s0 + s1), s1 / (s0 + s1)
    e0, e1 = gsel * epg + i0, gsel * epg + i1

    eid = lax.broadcasted_iota(I32, (N_EXPERTS, tq), 0)
    oh0 = eid == e0
    oh1 = eid == e1
    onehot = jnp.where(jnp.logical_or(oh0, oh1), 1.0, 0.0)
    rr = lax.broadcasted_iota(I32, (tq, tq), 0)
    cc = lax.broadcasted_iota(I32, (tq, tq), 1)
    upper = jnp.where(rr < cc, 1.0, 0.0).astype(BF16)
    prefix = _dot(onehot.astype(BF16), upper) + run
    r0 = jnp.sum(jnp.where(oh0, prefix, 0.0), axis=0, keepdims=True).astype(I32)
    r1 = jnp.sum(jnp.where(oh1, prefix, 0.0), axis=0, keepdims=True).astype(I32)

    zero = jnp.zeros((1, tq), I32)
    rt_ref[:, rows] = jnp.concatenate(
        [e0, e1, r0, r1, pltpu.bitcast(w0, I32), pltpu.bitcast(w1, I32), zero, zero], axis=0)
    return run + jnp.sum(onehot, axis=1, keepdims=True)


def _xattn_route(x1, kv, mem_len, wq, wo, g2, b2, rw_hi, rw_lo, rbias, b0, bsz, s_len):
    t_all, d = x1.shape
    tq = TOKEN_TILE
    nj = s_len // tq
    m = mem_len
    row = lambda bb, jj: (bb * nj + jj, 0)
    return pl.pallas_call(
        _xattn_kernel,
        grid=(bsz, nj),
        in_specs=[
            pl.BlockSpec((tq, d), row),
            pl.BlockSpec((m, 2 * d), lambda bb, jj: (b0 + bb, 0)),
            _const_spec((d, d)), _const_spec((d, d)),
            _const_spec((1, d)), _const_spec((1, d)),
            _const_spec((N_EXPERTS, d)), _const_spec((N_EXPERTS, d)),
            _const_spec((N_EXPERTS, 1)),
        ],
        out_specs=[
            pl.BlockSpec((tq, d), row),
            pl.BlockSpec((tq, d // 2), row),
            pl.BlockSpec((8, tq), lambda bb, jj: (0, bb * nj + jj)),
            _const_spec((N_EXPERTS, 128)),
        ],
        out_shape=[
            jax.ShapeDtypeStruct((t_all, d), F32),
            jax.ShapeDtypeStruct((t_all, d // 2), I32),
            jax.ShapeDtypeStruct((8, t_all), I32),
            jax.ShapeDtypeStruct((N_EXPERTS, 128), I32),
        ],
        scratch_shapes=[pltpu.VMEM((N_EXPERTS, 1), F32), pltpu.VMEM((d, d), BF16), pltpu.VMEM((d, d), BF16)],
        compiler_params=_tc_params(2),
        name="xattn_route",
    )(x1, kv, wq, wo, g2, b2, rw_hi, rw_lo, rbias)


def _sc_mesh():
    return plsc.VectorSubcoreMesh(core_axis_name="c", subcore_axis_name="s")


def _sc_params():
    return pltpu.CompilerParams(needs_layout_passes=False)


def _worker_id():
    return lax.axis_index("s") * lax.axis_size("c") + lax.axis_index("c")


def _sc_dispatch(xp, dest2d, n_rows):
    t_all, width = xp.shape
    chunk = dest2d.shape[1]
    tok_w = t_all // SC_WORKERS
    nch = tok_w // chunk
    slot1 = t_all // chunk
    assert t_all % (SC_WORKERS * chunk * 2) == 0

    def body(x_hbm, dest_hbm, out_hbm, idx0_v, idx1_v, buf0, buf1, sem_r, sem_w):
        wid = _worker_id()
        base = wid * tok_w
        pltpu.sync_copy(dest_hbm.at[pl.ds(wid * nch, nch)], idx0_v)
        pltpu.sync_copy(dest_hbm.at[pl.ds(slot1 + wid * nch, nch)], idx1_v)

        def read(c, buf, k):
            return pltpu.make_async_copy(x_hbm.at[pl.ds(base + c * chunk, chunk)], buf, sem_r.at[k])

        def scatter(c, buf):
            a = pltpu.make_async_copy(buf, out_hbm.at[idx0_v.at[c]], sem_w.at[0])
            b = pltpu.make_async_copy(buf, out_hbm.at[idx1_v.at[c]], sem_w.at[1])
            a.start()
            b.start()
            a.wait()
            b.wait()

        read(0, buf0, 0).start()

        @pl.loop(0, nch // 2)
        def _(g):
            c = 2 * g
            read(c + 1, buf1, 1).start()
            read(c, buf0, 0).wait()
            scatter(c, buf0)

            @pl.when(c + 2 < nch)
            def _():
                read(c + 2, buf0, 0).start()

            read(c + 1, buf1, 1).wait()
            scatter(c + 1, buf1)

    return pl.kernel(
        body,
        out_type=jax.ShapeDtypeStruct((n_rows, width), xp.dtype),
        mesh=_sc_mesh(),
        scratch_types=[
            pltpu.VMEM((nch, chunk), I32),
            pltpu.VMEM((nch, chunk), I32),
            pltpu.VMEM((chunk, width), xp.dtype),
            pltpu.VMEM((chunk, width), xp.dtype),
            pltpu.SemaphoreType.DMA((2,)),
            pltpu.SemaphoreType.DMA((2,)),
        ],
        compiler_params=_sc_params(),
        name="sc_dispatch",
    )(xp, dest2d)


def _sc_combine(ys, dest2d):
    n_idx_rows, chunk = dest2d.shape
    width = ys.shape[1]
    nch = n_idx_rows // SC_WORKERS
    assert n_idx_rows % (SC_WORKERS * 2) == 0

    def body(y_hbm, dest_hbm, out_hbm, idx_v, buf0, buf1, sem_g):
        wid = _worker_id()
        base = wid * nch * chunk
        pltpu.sync_copy(dest_hbm.at[pl.ds(wid * nch, nch)], idx_v)

        def gather(c, buf, k):
            return pltpu.make_async_copy(y_hbm.at[idx_v.at[c]], buf, sem_g.at[k])

        def write(c, buf):
            pltpu.sync_copy(buf, out_hbm.at[pl.ds(base + c * chunk, chunk)])

        gather(0, buf0, 0).start()

        @pl.loop(0, nch // 2)
        def _(g):
            c = 2 * g
            gather(c + 1, buf1, 1).start()
            gather(c, buf0, 0).wait()
            write(c, buf0)

            @pl.when(c + 2 < nch)
            def _():
                gather(c + 2, buf0, 0).start()

            gather(c + 1, buf1, 1).wait()
            write(c + 1, buf1)

    return pl.kernel(
        body,
        out_type=jax.ShapeDtypeStruct((n_idx_rows * chunk, width), ys.dtype),
        mesh=_sc_mesh(),
        scratch_types=[
            pltpu.VMEM((nch, chunk), I32),
            pltpu.VMEM((chunk, width), ys.dtype),
            pltpu.VMEM((chunk, width), ys.dtype),
            pltpu.SemaphoreType.DMA((2,)),
        ],
        compiler_params=_sc_params(),
        name="sc_combine",
    )(ys, dest2d)


def _pack_bf16_pairs(v):
    half = v.shape[1] // 2
    lo = pltpu.bitcast(v[:, :half].astype(BF16).astype(F32), jnp.uint32) >> 16
    hi = pltpu.bitcast(v[:, half:].astype(BF16).astype(F32), jnp.uint32) & jnp.uint32(0xFFFF0000)
    return pltpu.bitcast(hi | lo, I32)


def _unpack_bf16_pairs(w):
    w = pltpu.bitcast(w, jnp.uint32)
    lo = pltpu.bitcast(w << 16, F32)
    hi = pltpu.bitcast(w & jnp.uint32(0xFFFF0000), F32)
    return jnp.concatenate([lo, hi], axis=1)


def _ffn_kernel(be_ref, nv_ref, nu_ref, nxt_ref, xs_ref, wg_hbm, wu_hbm, wd_hbm, o_ref,
                wg_f, wu_f, wd_f, wg_s, wu_s, wd_s, sem):
    i = pl.program_id(0)

    def fetch(e):
        return (pltpu.make_async_copy(wg_hbm.at[e], wg_f, sem.at[0]),
                pltpu.make_async_copy(wu_hbm.at[e], wu_f, sem.at[1]),
                pltpu.make_async_copy(wd_hbm.at[e], wd_f, sem.at[2]))

    @pl.when(nxt_ref[i] >= 0)
    def _():
        @pl.when(i == 0)
        def _():
            for cp in fetch(be_ref[0]):
                cp.start()

        for cp in fetch(be_ref[i]):
            cp.wait()
        wg_s[...] = wg_f[...].astype(BF16)
        wu_s[...] = wu_f[...].astype(BF16)
        wd_s[...] = wd_f[...].astype(BF16)

        @pl.when(nxt_ref[i] < N_EXPERTS)
        def _():
            for cp in fetch(nxt_ref[i]):
                cp.start()

    @pl.when(i < nu_ref[0])
    def _():
        live = lax.broadcasted_iota(I32, xs_ref.shape, 0) < nv_ref[i]
        xb = _unpack_bf16_pairs(jnp.where(live, xs_ref[...], 0)).astype(BF16)
        act = jax.nn.silu(_dot(xb, wg_s[...])) * _dot(xb, wu_s[...])
        o_ref[...] = _pack_bf16_pairs(_dot(act.astype(BF16), wd_s[...]))


def _expert_ffn(xs, blk_expert, blk_valid, n_used, blk_next, w_gate, w_up, w_down):
    n_rows, half = xs.shape
    d = 2 * half
    de = w_gate.shape[2]
    bm = EXPERT_ROWS
    rows = lambda i, be, nv, nu, nx: (jnp.minimum(i, nu[0] - 1), 0)
    hbm = pl.BlockSpec(memory_space=pl.ANY)
    return pl.pallas_call(
        _ffn_kernel,
        grid_spec=pltpu.PrefetchScalarGridSpec(
            num_scalar_prefetch=4,
            grid=(n_rows // bm,),
            in_specs=[pl.BlockSpec((bm, half), rows), hbm, hbm, hbm],
            out_specs=pl.BlockSpec((bm, half), rows),
            scratch_shapes=[
                pltpu.VMEM((d, de), F32), pltpu.VMEM((d, de), F32), pltpu.VMEM((de, d), F32),
                pltpu.VMEM((d, de), BF16), pltpu.VMEM((d, de), BF16), pltpu.VMEM((de, d), BF16),
                pltpu.SemaphoreType.DMA((3,)),
            ],
        ),
        out_shape=jax.ShapeDtypeStruct((n_rows, half), I32),
        compiler_params=_tc_params(1),
        name="expert_ffn",
    )(blk_expert, blk_valid, n_used, blk_next, xs, w_gate, w_up, w_down)


def _combine_kernel(x_ref, y0_ref, y1_ref, rt_ref, g_ref, b_ref, o_ref):
    o_ref[...] = _expert_combine_ln(x_ref[...], y0_ref[...], y1_ref[...], rt_ref[...], g_ref[...], b_ref[...])


def _combine_kernel_into(x_ref, y0_ref, y1_ref, rt_ref, g_ref, b_ref, full_ref, o_ref):
    del full_ref
    _combine_kernel(x_ref, y0_ref, y1_ref, rt_ref, g_ref, b_ref, o_ref)


def _combine_ln(x2, y01, rt, g3, b3, into=None, row0=0, full_rows=None):
    t_all, d = x2.shape
    tq = TOKEN_TILE
    nt = t_all // tq
    blk0 = row0 // tq if full_rows else 0
    in_specs = [
        pl.BlockSpec((tq, d), lambda i: (i, 0)),
        pl.BlockSpec((tq, d // 2), lambda i: (i, 0)),
        pl.BlockSpec((tq, d // 2), lambda i: (i + nt, 0)),
        pl.BlockSpec((8, tq), lambda i: (0, i)),
        _const_spec((1, d)), _const_spec((1, d)),
    ]
    args = [x2, y01, y01, rt, g3, b3]
    body, aliases = _combine_kernel, {}
    if into is not None:
        in_specs.append(pl.BlockSpec(memory_space=pl.ANY))
        args.append(into)
        body, aliases = _combine_kernel_into, {len(args) - 1: 0}
    return pl.pallas_call(
        body,
        grid=(nt,),
        in_specs=in_specs,
        out_specs=pl.BlockSpec((tq, d), lambda i: (i + blk0, 0)),
        out_shape=jax.ShapeDtypeStruct((full_rows or t_all, d), F32),
        input_output_aliases=aliases,
        compiler_params=_tc_params(1),
        name="combine_ln",
    )(*args)


def _routing_tables(rt, counts):
    bm = EXPERT_ROWS
    t_all = rt.shape[1]
    n_rows = 2 * t_all + N_EXPERTS * bm
    cnt = counts[:, 0]
    padded = (cnt + bm - 1) // bm * bm
    ends = jnp.cumsum(padded)
    offs = ends - padded
    experts = rt[0:2]
    off_tok = jnp.sum(jnp.where(experts[None] == jnp.arange(N_EXPERTS, dtype=I32)[:, None, None],
                                offs[:, None, None], 0), axis=0)
    dest2d = (off_tok + rt[2:4]).reshape(-1, SC_CHUNK).astype(I32)
    blk_start = jnp.arange(n_rows // bm, dtype=I32) * bm
    blk_expert = jnp.minimum(jnp.sum(blk_start[:, None] >= ends[None, :], axis=1), N_EXPERTS - 1).astype(I32)
    live_end = jnp.sum(jnp.where(blk_expert[:, None] == jnp.arange(N_EXPERTS, dtype=I32)[None, :],
                                 (offs + cnt)[None, :], 0), axis=1)
    blk_valid = jnp.clip(live_end - blk_start, 0, bm).astype(I32)
    n_used = (ends[-1:] // bm).astype(I32)
    eid = jnp.arange(N_EXPERTS, dtype=I32)
    later_present = jnp.logical_and(eid[None, :] > eid[:, None], (cnt > 0)[None, :])
    next_present = jnp.min(jnp.where(later_present, eid[None, :], N_EXPERTS), axis=1)
    next_of_blk = jnp.sum(jnp.where(blk_expert[:, None] == eid[None, :], next_present[None, :], 0), axis=1)
    prev_expert = jnp.concatenate([jnp.full((1,), -1, I32), blk_expert[:-1]])
    is_first = jnp.logical_and(blk_start < ends[-1], blk_expert != prev_expert)
    blk_next = jnp.where(is_first, next_of_blk, -1).astype(I32)
    return dest2d, blk_expert, blk_valid, n_used, blk_next, n_rows


def _layer_tail(x1, kv, mem_len, p, router, b0, bsz, s_len):
    rw_hi, rw_lo, rbias = router
    x2, xp, rt, counts = _xattn_route(x1, kv, mem_len, p["xq"], p["xo"], p["ln2_g"], p["ln2_b"],
                                      rw_hi, rw_lo, rbias, b0, bsz, s_len)
    dest2d, blk_expert, blk_valid, n_used, blk_next, n_rows = _routing_tables(rt, counts)
    xs = _sc_dispatch(xp, dest2d, n_rows)
    ys = _expert_ffn(xs, blk_expert, blk_valid, n_used, blk_next, p["e_gate"], p["e_up"], p["e_down"])
    y01 = _sc_combine(ys, dest2d)
    return x2, y01, rt


def _row(v):
    return v.reshape(1, -1).astype(F32)


def _common_params(xq, xkv, xo, ln2_g, ln2_b, e_gate, e_up, e_down, ln3_g, ln3_b):
    return dict(xq=xq, xkv=xkv, xo=xo, ln2_g=_row(ln2_g), ln2_b=_row(ln2_b),
                e_gate=e_gate, e_up=e_up, e_down=e_down, ln3_g=_row(ln3_g), ln3_b=_row(ln3_b))


def kernel(x, mem, positions, router_w, router_bias, l0_w_in, l0_sinks, l0_sgu_ln_g, l0_sgu_ln_b, l0_sgu_w, l0_sgu_b, l0_w_out, l0_ln1_g, l0_ln1_b, l0_xq, l0_xkv, l0_xo, l0_ln2_g, l0_ln2_b, l0_e_gate, l0_e_up, l0_e_down, l0_ln3_g, l0_ln3_b, l1_w_in, l1_pool_w, l1_pool_scale, l1_w_out, l1_ln1_g, l1_ln1_b, l1_xq, l1_xkv, l1_xo, l1_ln2_g, l1_ln2_b, l1_e_gate, l1_e_up, l1_e_down, l1_ln3_g, l1_ln3_b):
    bsz, s_len, d = x.shape
    assert s_len % TOKEN_TILE == 0 and s_len % MIXER0_TILE == 0 and MIXER0_TILE % BLOCK == 0
    xt = x.reshape(bsz * s_len, d)
    mem2d = mem.reshape(-1, d)

    rw_t = router_w.T.astype(F32)
    rw_hi = rw_t.astype(BF16)
    rw_lo = (rw_t - rw_hi.astype(F32)).astype(BF16)
    router = (rw_hi, rw_lo, router_bias.reshape(-1, 1).astype(F32))

    half = ROPE_DIM // 2
    inv_freq = (ROPE_THETA ** (-(jnp.arange(half, dtype=F32) * 2.0 / ROPE_DIM))).reshape(half, 1)
    etab_np = np.zeros((128, 3 * 128), np.float32)
    cbase_np = np.ones((1, 128), np.float32)
    for ln in range(128):
        dd = ln % HEAD_DIM
        if dd < ROPE_DIM:
            cbase_np[0, ln] = 0.0
            etab_np[[dd % half, half + dd % half], ln] = 1.0
            if dd >= half:
                etab_np[[2 * half + dd - half, 3 * half + dd - half], 128 + ln] = 1.0
            else:
                etab_np[[2 * half + dd, 3 * half + dd], 256 + ln] = -1.0
    etab = jnp.asarray(etab_np, BF16)
    cbase = jnp.asarray(cbase_np)
    pos_row = positions.reshape(1, -1).astype(I32)
    grp = jnp.arange(B_WIDTH) // B_GROUP_DIM
    gsum = (grp[:, None] == grp[None, :]).astype(BF16)
    bs_full = jnp.repeat(l0_sgu_b.T.astype(F32), B_GROUP_DIM, axis=1)

    p0 = _common_params(l0_xq, l0_xkv, l0_xo, l0_ln2_g, l0_ln2_b, l0_e_gate, l0_e_up, l0_e_down,
                        l0_ln3_g, l0_ln3_b)
    p1 = _common_params(l1_xq, l1_xkv, l1_xo, l1_ln2_g, l1_ln2_b, l1_e_gate, l1_e_up, l1_e_down,
                        l1_ln3_g, l1_ln3_b)
    kv0 = _kv_proj(mem2d, p0["xkv"], bsz)
    kv1 = _kv_proj(mem2d, p1["xkv"], bsz)
    mem_len = mem.shape[1]

    n_split = BATCH_SPLIT if bsz % BATCH_SPLIT == 0 else 1
    nb = bsz // n_split
    out = None
    for part in range(n_split):
        b0 = part * nb
        x1 = _mixer0(xt, pos_row, l0_sinks.astype(F32), l0_w_in, inv_freq, etab, cbase, gsum,
                     _row(l0_sgu_ln_g), _row(l0_sgu_ln_b), l0_sgu_w.astype(F32), bs_full,
                     l0_w_out, _row(l0_ln1_g), _row(l0_ln1_b), b0, nb, s_len)
        x2, y01, rt = _layer_tail(x1, kv0, mem_len, p0, router, b0, nb, s_len)
        x1 = _mixer1(x2, y01, rt, p0["ln3_g"], p0["ln3_b"], l1_w_in, l1_pool_w, _row(l1_pool_scale),
                     l1_w_out, _row(l1_ln1_g), _row(l1_ln1_b), nb, s_len)
        x2, y01, rt = _layer_tail(x1, kv1, mem_len, p1, router, b0, nb, s_len)
        out = _combine_ln(x2, y01, rt, p1["ln3_g"], p1["ln3_b"], into=out, row0=b0 * s_len,
                          full_rows=bsz * s_len)
    return out.reshape(bsz, s_len, d)
```

```python
import functools

import numpy as np
import jax
import jax.numpy as jnp
from jax import lax
from jax.experimental import pallas as pl
from jax.experimental.pallas import tpu as pltpu
from jax.experimental.pallas import tpu_sc as plsc

F32 = jnp.float32
BF16 = jnp.bfloat16
I32 = jnp.int32

DEPTH = 2
ALPHA = (2.0 * DEPTH) ** 0.25
LN_EPS = 1e-5

HEAD_DIM = 64
A_Q_HEADS = 8
A_KV_HEADS = 2
A_GROUP = A_Q_HEADS // A_KV_HEADS
BLOCK = 128
ROPE_THETA = 500000.0
ROPE_DIM = HEAD_DIM // 4
A_WIDTH = A_Q_HEADS * HEAD_DIM
KV_WIDTH = A_KV_HEADS * HEAD_DIM
B_GROUPS = 8
B_GROUP_DIM = 64
B_WIDTH = B_GROUPS * B_GROUP_DIM
POOL_WINDOWS = (2, 4, 8, 16)
POOL_HALO = 16
X_HEADS = 4
N_EXPERTS = 16
N_EXPERT_GROUPS = 4
EXPERTS_PER_GROUP = 4

TOKEN_TILE = 1024
MIXER0_TILE = 1024
SLAB_ROWS = 512
EXPERT_ROWS = 512
BATCH_SPLIT = 2
SC_WORKERS = 32
SC_CHUNK = 64
VMEM_LIMIT = 56 * 1024 * 1024
NEG_BIG = -1e30


def _layer_norm(z, g, b):
    mu = jnp.mean(z, axis=-1, keepdims=True)
    d = z - mu
    var = jnp.mean(d * d, axis=-1, keepdims=True)
    return d * lax.rsqrt(var + LN_EPS) * g + b


def _dot(a, b):
    return jnp.dot(a, b, preferred_element_type=F32)


def _dot_nt(a, b):
    return lax.dot_general(a, b, (((1,), (1,)), ((), ())), preferred_element_type=F32)


def _split_bf16(v):
    hi = v.astype(BF16)
    lo = (v - hi.astype(F32)).astype(BF16)
    return hi, lo


def _tc_params(n_axes):
    return pltpu.CompilerParams(dimension_semantics=("arbitrary",) * n_axes,
                                vmem_limit_bytes=VMEM_LIMIT)


def _const_spec(shape):
    nd = len(shape)
    return pl.BlockSpec(shape, lambda *_: (0,) * nd, pipeline_mode=pl.Buffered(1))

def _mixer0_kernel(sinks_ref, x_ref, pos_ref, win_ref, invf_ref, etab_ref, cbase_ref, gsum_ref,
                   lng_ref, lnb_ref, ws_ref, bs_ref, wout_ref, g1_ref, b1_ref,
                   o_ref, q_s, kv_s, u_s, vn_s, mix_s, wt_s, win_s, wout_s):
    b = pl.program_id(0)
    j = pl.program_id(1)
    tq = x_ref.shape[0]
    nblk = tq // BLOCK
    kvw = kv_s.shape[1]

    @pl.when(jnp.logical_and(b == 0, j == 0))
    def _():
        win_s[...] = win_ref[...].astype(BF16)
        wout_s[...] = wout_ref[...].astype(BF16)
        r = lax.broadcasted_iota(I32, (BLOCK, BLOCK), 0)
        c = lax.broadcasted_iota(I32, (BLOCK, BLOCK), 1)
        for g in range(B_GROUPS):
            wt_s[g] = jnp.where(c <= r, ws_ref[g], 0.0).astype(BF16)

    @pl.when(j == 0)
    def _():
        kv_s[0:BLOCK, :] = jnp.zeros((BLOCK, kvw), BF16)

    c1 = A_WIDTH
    c2 = c1 + KV_WIDTH
    c3 = c2 + KV_WIDTH
    c4 = c3 + B_WIDTH

    def rotary_tables(rows):
        n = rows.stop - rows.start
        ang = invf_ref[...] * pos_ref[:, rows].astype(F32)
        c8 = jnp.cos(ang)
        s8 = jnp.sin(ang)
        c8h = c8.astype(BF16).astype(F32)
        s8h = s8.astype(BF16).astype(F32)
        stack = jnp.concatenate([c8h, c8 - c8h, s8h, s8 - s8h, jnp.zeros((128 - 4 * 8, n), F32)], axis=0)
        tabs = _dot(stack.T.astype(BF16), etab_ref[...])
        return tabs[:, 0:128] + cbase_ref[...], tabs[:, 128:256], tabs[:, 256:384]

    def prepare(rows, h, tables):
        n = rows.stop - rows.start
        cs, sa, sb = tables

        def rope(t):
            return t * cs + pltpu.roll(t, ROPE_DIM // 2, 1) * sa + pltpu.roll(t, 128 - ROPE_DIM // 2, 1) * sb

        for c in range(A_WIDTH // 128):
            t = h[:, c * 128:(c + 1) * 128] * (HEAD_DIM ** -0.5)
            q_s[rows, c * 128:(c + 1) * 128] = rope(t).astype(BF16)
        low = lax.broadcasted_iota(I32, (n, 128), 1) < HEAD_DIM
        kr = rope(h[:, c1:c2])
        kx = pltpu.roll(kr, HEAD_DIM, 1)
        vr = h[:, c2:c3]
        vx = pltpu.roll(vr, HEAD_DIM, 1)
        kv_cols = [jnp.where(low, kr, kx), jnp.where(low, kx, kr),
                   jnp.where(low, vr, 0.0), jnp.where(low, 0.0, vx),
                   jnp.where(low, vx, 0.0), jnp.where(low, 0.0, vr)]
        for c, col in enumerate(kv_cols):
            kv_s[BLOCK + rows.start:BLOCK + rows.stop, c * 128:(c + 1) * 128] = col.astype(BF16)

        u_s[rows, :] = jax.nn.gelu(h[:, c3:c4])
        v = jax.nn.gelu(h[:, c4:])
        gsum = gsum_ref[...]
        mean = _dot(v.astype(BF16), gsum) * (1.0 / B_GROUP_DIM)
        d = v - mean
        var = _dot((d * d).astype(BF16), gsum) * (1.0 / B_GROUP_DIM)
        vn_s[rows, :] = (d * lax.rsqrt(var + LN_EPS) * lng_ref[...] + lnb_ref[...]).astype(BF16)

    qi = lax.broadcasted_iota(I32, (BLOCK, 2 * BLOCK), 0)
    kj = lax.broadcasted_iota(I32, (BLOCK, 2 * BLOCK), 1)
    rel = qi + BLOCK - kj
    band = jnp.logical_and(rel >= 0, rel < BLOCK)
    low_q = lax.broadcasted_iota(I32, (BLOCK, 128), 1) < HEAD_DIM
    low_k = lax.broadcasted_iota(I32, (2 * BLOCK, 128), 1) < HEAD_DIM
    ones_lo = jnp.where(low_k, 1.0, 0.0).astype(BF16)
    ones_hi = jnp.where(low_k, 0.0, 1.0).astype(BF16)
    zero_q = jnp.zeros((BLOCK, 128), BF16)

    def block_body(n):
        r0 = n * BLOCK
        kv = kv_s[pl.ds(r0, 2 * BLOCK), :]
        qb = q_s[pl.ds(r0, BLOCK), :]
        valid = jnp.logical_and(band, kj >= jnp.where(j == 0, BLOCK, 0)) if n == 0 else band
        heads = [(c, half) for c in range(A_WIDTH // 128) for half in range(2)]
        scores = []
        for c, half in heads:
            qp = qb[:, c * 128:(c + 1) * 128]
            qm = jnp.where(low_q, qp, zero_q) if half == 0 else jnp.where(low_q, zero_q, qp)
            hk = (2 * c) // A_GROUP
            scores.append(_dot_nt(qm, kv[:, hk * 128:(hk + 1) * 128]))
        vnb = vn_s[pl.ds(r0, BLOCK), :]
        parts = []
        for c in range(B_WIDTH // 128):
            vp = vnb[:, c * 128:(c + 1) * 128]
            parts.append(_dot(wt_s[2 * c], jnp.where(low_q, vp, zero_q))
                         + _dot(wt_s[2 * c + 1], jnp.where(low_q, zero_q, vp)))
        res = [None] * (A_WIDTH // 128)
        esink = {}
        for (c, half), sc in zip(heads, scores):
            hk = (2 * c) // A_GROUP
            vcol = kv[:, (2 + 2 * hk + half) * 128:(3 + 2 * hk + half) * 128]
            vm = jnp.concatenate([vcol, ones_lo if half == 0 else ones_hi], axis=1)
            s = jnp.where(valid, sc, NEG_BIG)
            sink = sinks_ref[2 * c + half]
            m = jnp.maximum(jnp.max(s, axis=-1, keepdims=True), sink)
            pv = _dot(jnp.exp(s - m).astype(BF16), vm)
            res[c] = pv if res[c] is None else res[c] + pv
            esink[c, half] = jnp.exp(sink - m)
        for c in range(A_WIDTH // 128):
            den = res[c][:, 128:] + jnp.where(low_q, esink[c, 0], esink[c, 1])
            mix_s[pl.ds(r0, BLOCK), c * 128:(c + 1) * 128] = (res[c][:, :128] / den).astype(BF16)
        mixed = jnp.concatenate(parts, axis=1) + bs_ref[...]
        mix_s[pl.ds(r0, BLOCK), A_WIDTH:] = (u_s[pl.ds(r0, BLOCK), :] * mixed).astype(BF16)

    slab = min(SLAB_ROWS, tq)
    slabs = [slice(r, r + slab) for r in range(0, tq, slab)]
    tables = [rotary_tables(rows) for rows in slabs]
    xs = [x_ref[rows, :] for rows in slabs]
    hs = [_dot(x.astype(BF16), win_s[...]) for x in xs]
    zs = []
    for rows, x, h, tab in zip(slabs, xs, hs, tables):
        prepare(rows, h, tab)
        for n in range(rows.start // BLOCK, rows.stop // BLOCK):
            block_body(n)
        zs.append(ALPHA * x + _dot(mix_s[rows, :], wout_s[...]))
    kv_s[0:BLOCK, :] = kv_s[tq:tq + BLOCK, :]
    for rows, z in zip(slabs, zs):
        o_ref[rows, :] = _layer_norm(z, g1_ref[...], b1_ref[...])


def _mixer0(x, pos_row, sinks, w_in, invf, etab, cbase, gsum, lng, lnb, w_s, bs_full, w_out, g1, b1,
            b0, bsz, s_len):
    d = x.shape[1]
    t_all = bsz * s_len
    tq = MIXER0_TILE
    nj = s_len // tq
    row = lambda bb, jj: (bb * nj + jj, 0)
    in_w = w_in.shape[1]
    return pl.pallas_call(
        _mixer0_kernel,
        grid=(bsz, nj),
        in_specs=[
            pl.BlockSpec(memory_space=pltpu.SMEM),
            pl.BlockSpec((tq, d), lambda bb, jj: ((b0 + bb) * nj + jj, 0)),
            pl.BlockSpec((1, tq), lambda bb, jj: (0, (b0 + bb) * nj + jj)),
            _const_spec((d, in_w)),
            _const_spec((ROPE_DIM // 2, 1)), _const_spec((128, 3 * 128)), _const_spec((1, 128)),
            _const_spec((B_WIDTH, B_WIDTH)),
            _const_spec((1, B_WIDTH)), _const_spec((1, B_WIDTH)),
            _const_spec((B_GROUPS, BLOCK, BLOCK)),
            _const_spec((BLOCK, B_WIDTH)),
            _const_spec((A_WIDTH + B_WIDTH, d)),
            _const_spec((1, d)), _const_spec((1, d)),
        ],
        out_specs=pl.BlockSpec((tq, d), row),
        out_shape=jax.ShapeDtypeStruct((t_all, d), F32),
        scratch_shapes=[
            pltpu.VMEM((tq, A_WIDTH), BF16),
            pltpu.VMEM((tq + BLOCK, 6 * 128), BF16),
            pltpu.VMEM((tq, B_WIDTH), F32),
            pltpu.VMEM((tq, B_WIDTH), BF16),
            pltpu.VMEM((tq, A_WIDTH + B_WIDTH), BF16),
            pltpu.VMEM((B_GROUPS, BLOCK, BLOCK), BF16),
            pltpu.VMEM((d, in_w), BF16),
            pltpu.VMEM((A_WIDTH + B_WIDTH, d), BF16),
        ],
        compiler_params=_tc_params(2),
        name="mixer0",
    )(sinks, x, pos_row, w_in, invf, etab, cbase, gsum, lng, lnb, w_s, bs_full, w_out, g1, b1)


def _expert_combine_ln(x2, y0_packed, y1_packed, rt, g, b):
    wt = pltpu.bitcast(rt, F32).T
    y = wt[:, 4:5] * _unpack_bf16_pairs(y0_packed) + wt[:, 5:6] * _unpack_bf16_pairs(y1_packed)
    return _layer_norm(ALPHA * x2 + y, g, b)


def _mixer1_kernel(x2_ref, y0_ref, y1_ref, rt_ref, g3_ref, b3_ref,
                   win_ref, pw_ref, ps_ref, wout_ref, g1_ref, b1_ref, o_ref,
                   h_s, mp_s, win_s, pw_s, wout_s):
    j = pl.program_id(1)
    tq = x2_ref.shape[0]
    gw = x2_ref.shape[1] // len(POOL_WINDOWS)
    slabs = [slice(c * SLAB_ROWS, (c + 1) * SLAB_ROWS) for c in range(tq // SLAB_ROWS)]

    @pl.when(jnp.logical_and(pl.program_id(0) == 0, j == 0))
    def _():
        win_s[...] = win_ref[...].astype(BF16)
        pw_s[...] = pw_ref[...].astype(BF16)
        wout_s[...] = wout_ref[...].astype(BF16)

    @pl.when(j == 0)
    def _():
        h_s[0:POOL_HALO, :] = jnp.zeros((POOL_HALO, h_s.shape[1]), F32)

    xs = []
    for rows in slabs:
        x = _expert_combine_ln(x2_ref[rows, :], y0_ref[rows, :], y1_ref[rows, :], rt_ref[:, rows],
                               g3_ref[...], b3_ref[...])
        h_s[POOL_HALO + rows.start:POOL_HALO + rows.stop, :] = _dot(x.astype(BF16), win_s[...])
        xs.append(x)

    for rows, x in zip(slabs, xs):
        n = rows.stop - rows.start
        t_pos = j * tq + rows.start + lax.broadcasted_iota(I32, (n, 1), 0)
        for g, win in enumerate(POOL_WINDOWS):
            lo, hi = g * gw, (g + 1) * gw
            ext = h_s[rows.start:rows.stop + POOL_HALO, lo:hi]
            acc = ext
            shift = 1
            while shift < win:
                acc = acc + pltpu.roll(acc, shift, 0)
                shift *= 2
            count = jnp.minimum(t_pos + 1, win).astype(F32)
            pooled = acc[POOL_HALO:, :] / count - ext[POOL_HALO:, :]
            mapped = _dot(pooled.astype(BF16), pw_s[g])
            mp_s[rows, lo:hi] = (mapped * ps_ref[:, lo:hi]).astype(BF16)
        z = ALPHA * x + _dot(mp_s[rows, :], wout_s[...])
        o_ref[rows, :] = _layer_norm(z, g1_ref[...], b1_ref[...])
    h_s[0:POOL_HALO, :] = h_s[tq:tq + POOL_HALO, :]


def _mixer1(x2, y01, rt, g3, b3, w_in, pool_w, pool_scale, w_out, g1, b1, bsz, s_len):
    t_all, d = x2.shape
    tq = TOKEN_TILE
    nj = s_len // tq
    nt = bsz * nj
    row = lambda bb, jj: (bb * nj + jj, 0)
    ng = len(POOL_WINDOWS)
    return pl.pallas_call(
        _mixer1_kernel,
        grid=(bsz, nj),
        in_specs=[
            pl.BlockSpec((tq, d), row),
            pl.BlockSpec((tq, d // 2), row),
            pl.BlockSpec((tq, d // 2), lambda bb, jj: (nt + bb * nj + jj, 0)),
            pl.BlockSpec((8, tq), lambda bb, jj: (0, bb * nj + jj)),
            _const_spec((1, d)), _const_spec((1, d)),
            _const_spec((d, d)),
            _const_spec((ng, d // ng, d // ng)),
            _const_spec((1, d)),
            _const_spec((d, d)),
            _const_spec((1, d)), _const_spec((1, d)),
        ],
        out_specs=pl.BlockSpec((tq, d), row),
        out_shape=jax.ShapeDtypeStruct((t_all, d), F32),
        scratch_shapes=[pltpu.VMEM((tq + POOL_HALO, d), F32), pltpu.VMEM((tq, d), BF16),
                        pltpu.VMEM((d, d), BF16), pltpu.VMEM((ng, d // ng, d // ng), BF16),
                        pltpu.VMEM((d, d), BF16)],
        compiler_params=_tc_params(2),
        name="mixer1",
    )(x2, y01, y01, rt, g3, b3, w_in, pool_w, pool_scale, w_out, g1, b1)


def _kv_kernel(mem_ref, w_ref, o_ref, w_s):
    @pl.when(pl.program_id(0) == 0)
    def _():
        w_s[...] = w_ref[...].astype(BF16)

    o_ref[...] = _dot(mem_ref[...].astype(BF16), w_s[...]).astype(BF16)


def _kv_proj(mem2d, wkv, bsz):
    rows, d = mem2d.shape
    m = rows // bsz
    return pl.pallas_call(
        _kv_kernel,
        grid=(bsz,),
        in_specs=[pl.BlockSpec((m, d), lambda i: (i, 0)), _const_spec(wkv.shape)],
        out_specs=pl.BlockSpec((m, wkv.shape[1]), lambda i: (i, 0)),
        out_shape=jax.ShapeDtypeStruct((rows, wkv.shape[1]), BF16),
        scratch_shapes=[pltpu.VMEM(wkv.shape, BF16)],
        compiler_params=_tc_params(1),
        name="kv_proj",
    )(mem2d, wkv)


def _top2_of4(v):
    hi01, lo01 = jnp.maximum(v[0], v[1]), jnp.minimum(v[0], v[1])
    hi23, lo23 = jnp.maximum(v[2], v[3]), jnp.minimum(v[2], v[3])
    return jnp.maximum(hi01, hi23) + jnp.maximum(jnp.minimum(hi01, hi23), jnp.maximum(lo01, lo23))


def _argmax_first(vals):
    best, idx = vals[0], jnp.zeros(vals[0].shape, I32)
    for i in range(1, len(vals)):
        better = vals[i] > best
        best = jnp.where(better, vals[i], best)
        idx = jnp.where(better, i, idx)
    return best, idx


def _xattn_kernel(x_ref, kv_ref, wq_ref, wo_ref, g2_ref, b2_ref, rwh_ref, rwl_ref, rb_ref,
                  x2_ref, xp_ref, rt_ref, cnt_ref, run_s, wq_s, wo_s):
    first = jnp.logical_and(pl.program_id(0) == 0, pl.program_id(1) == 0)
    tq, d = x_ref.shape

    @pl.when(first)
    def _():
        run_s[...] = jnp.zeros(run_s.shape, F32)
        wq_s[...] = wq_ref[...].astype(BF16)
        wo_s[...] = wo_ref[...].astype(BF16)

    slabs = [slice(c * SLAB_ROWS, (c + 1) * SLAB_ROWS) for c in range(tq // SLAB_ROWS)]
    run = run_s[...]
    zs = [_xattn_attend(*_xattn_query(rows, x_ref, wq_s), kv_ref, wo_s) for rows in slabs]
    logits = [_xattn_norm_logits(rows, z, g2_ref, b2_ref, rwh_ref, rwl_ref, x2_ref, xp_ref)
              for rows, z in zip(slabs, zs)]
    for rows, lg in zip(slabs, logits):
        run = _xattn_route_rows(rows, lg, run, rb_ref, rt_ref)
    run_s[...] = run
    cnt_ref[...] = jnp.broadcast_to(run, cnt_ref.shape).astype(I32)


def _xattn_query(rows, x_ref, wq_s):
    hd = x_ref.shape[1] // X_HEADS
    x = x_ref[rows, :]
    return x, _dot(x.astype(BF16), wq_s[...]) * (hd ** -0.5)


def _xattn_attend(x, q, kv_ref, wo_s):
    d = x.shape[1]
    hd = d // X_HEADS
    outs = []
    for hx in range(X_HEADS):
        qh = q[:, hx * hd:(hx + 1) * hd].astype(BF16)
        kh = kv_ref[:, hx * hd:(hx + 1) * hd]
        vh = kv_ref[:, d + hx * hd:d + (hx + 1) * hd]
        s = _dot_nt(qh, kh)
        p = jnp.exp(s - jnp.max(s, axis=-1, keepdims=True))
        o = _dot(p.astype(BF16), vh) / jnp.sum(p, axis=-1, keepdims=True)
        outs.append(o.astype(BF16))
    return ALPHA * x + _dot(jnp.concatenate(outs, axis=1), wo_s[...])


def _xattn_norm_logits(rows, z, g2_ref, b2_ref, rwh_ref, rwl_ref, x2_ref, xp_ref):
    x2 = _layer_norm(z, g2_ref[...], b2_ref[...])
    x2_ref[rows, :] = x2

    xp_ref[rows, :] = _pack_bf16_pairs(x2)

    xh, xl = _split_bf16(x2)
    return _dot_nt(rwh_ref[...], xh) + _dot_nt(rwh_ref[...], xl) + _dot_nt(rwl_ref[...], xh)


def _xattn_route_rows(rows, logits, run, rb_ref, rt_ref):
    tq = rows.stop - rows.start
    e_max = jnp.max(logits, axis=0, keepdims=True)
    ex = jnp.exp(logits - e_max)
    scores = ex / jnp.sum(ex, axis=0, keepdims=True)
    biased = scores + rb_ref[...]
    sc = [scores[e:e + 1, :] for e in range(N_EXPERTS)]
    bi = [biased[e:e + 1, :] for e in range(N_EXPERTS)]
    epg = EXPERTS_PER_GROUP
    gscore = [_top2_of4(bi[g * epg:(g + 1) * epg]) for g in range(N_EXPERT_GROUPS)]
    _, gsel = _argmax_first(gscore)

    def pick(vals):
        return [functools.reduce(lambda acc, g: jnp.where(gsel == g, vals[g * epg + i], acc),
                                 range(1, N_EXPERT_GROUPS), vals[i]) for i in range(epg)]

    in_b = pick(bi)
    in_s = pick(sc)
    _, i0 = _argmax_first(in_b)
    _, i1 = _argmax_first([jnp.where(i0 == i, -jnp.inf, in_b[i]) for i in range(epg)])

    def take(vals, idx):
        return functools.reduce(lambda acc, i: jnp.where(idx == i, vals[i], acc), range(1, epg), vals[0])

    s0, s1 = take(in_s, i0), take(in_s, i1)
    w0, w1 = s0 / (s0 + s1), s1 / (s0 + s1)
    e0, e1 = gsel * epg + i0, gsel * epg + i1

    eid = lax.broadcasted_iota(I32, (N_EXPERTS, tq), 0)
    oh0 = eid == e0
    oh1 = eid == e1
    onehot = jnp.where(jnp.logical_or(oh0, oh1), 1.0, 0.0)
    rr = lax.broadcasted_iota(I32, (tq, tq), 0)
    cc = lax.broadcasted_iota(I32, (tq, tq), 1)
    upper = jnp.where(rr < cc, 1.0, 0.0).astype(BF16)
    prefix = _dot(onehot.astype(BF16), upper) + run
    r0 = jnp.sum(jnp.where(oh0, prefix, 0.0), axis=0, keepdims=True).astype(I32)
    r1 = jnp.sum(jnp.where(oh1, prefix, 0.0), axis=0, keepdims=True).astype(I32)

    zero = jnp.zeros((1, tq), I32)
    rt_ref[:, rows] = jnp.concatenate(
        [e0, e1, r0, r1, pltpu.bitcast(w0, I32), pltpu.bitcast(w1, I32), zero, zero], axis=0)
    return run + jnp.sum(onehot, axis=1, keepdims=True)


def _xattn_route(x1, kv, mem_len, wq, wo, g2, b2, rw_hi, rw_lo, rbias, b0, bsz, s_len):
    t_all, d = x1.shape
    tq = TOKEN_TILE
    nj = s_len // tq
    m = mem_len
    row = lambda bb, jj: (bb * nj + jj, 0)
    return pl.pallas_call(
        _xattn_kernel,
        grid=(bsz, nj),
        in_specs=[
            pl.BlockSpec((tq, d), row),
            pl.BlockSpec((m, 2 * d), lambda bb, jj: (b0 + bb, 0)),
            _const_spec((d, d)), _const_spec((d, d)),
            _const_spec((1, d)), _const_spec((1, d)),
            _const_spec((N_EXPERTS, d)), _const_spec((N_EXPERTS, d)),
            _const_spec((N_EXPERTS, 1)),
        ],
        out_specs=[
            pl.BlockSpec((tq, d), row),
            pl.BlockSpec((tq, d // 2), row),
            pl.BlockSpec((8, tq), lambda bb, jj: (0, bb * nj + jj)),
            _const_spec((N_EXPERTS, 128)),
        ],
        out_shape=[
            jax.ShapeDtypeStruct((t_all, d), F32),
            jax.ShapeDtypeStruct((t_all, d // 2), I32),
            jax.ShapeDtypeStruct((8, t_all), I32),
            jax.ShapeDtypeStruct((N_EXPERTS, 128), I32),
        ],
        scratch_shapes=[pltpu.VMEM((N_EXPERTS, 1), F32), pltpu.VMEM((d, d), BF16), pltpu.VMEM((d, d), BF16)],
        compiler_params=_tc_params(2),
        name="xattn_route",
    )(x1, kv, wq, wo, g2, b2, rw_hi, rw_lo, rbias)


def _sc_mesh():
    return plsc.VectorSubcoreMesh(core_axis_name="c", subcore_axis_name="s")


def _sc_params():
    return pltpu.CompilerParams(needs_layout_passes=False)


def _worker_id():
    return lax.axis_index("s") * lax.axis_size("c") + lax.axis_index("c")


def _sc_dispatch(xp, dest2d, n_rows):
    t_all, width = xp.shape
    chunk = dest2d.shape[1]
    tok_w = t_all // SC_WORKERS
    nch = tok_w // chunk
    slot1 = t_all // chunk
    assert t_all % (SC_WORKERS * chunk * 2) == 0

    def body(x_hbm, dest_hbm, out_hbm, idx0_v, idx1_v, buf0, buf1, sem_r, sem_w):
        wid = _worker_id()
        base = wid * tok_w
        pltpu.sync_copy(dest_hbm.at[pl.ds(wid * nch, nch)], idx0_v)
        pltpu.sync_copy(dest_hbm.at[pl.ds(slot1 + wid * nch, nch)], idx1_v)

        def read(c, buf, k):
            return pltpu.make_async_copy(x_hbm.at[pl.ds(base + c * chunk, chunk)], buf, sem_r.at[k])

        def scatter(c, buf):
            a = pltpu.make_async_copy(buf, out_hbm.at[idx0_v.at[c]], sem_w.at[0])
            b = pltpu.make_async_copy(buf, out_hbm.at[idx1_v.at[c]], sem_w.at[1])
            a.start()
            b.start()
            a.wait()
            b.wait()

        read(0, buf0, 0).start()

        @pl.loop(0, nch // 2)
        def _(g):
            c = 2 * g
            read(c + 1, buf1, 1).start()
            read(c, buf0, 0).wait()
            scatter(c, buf0)

            @pl.when(c + 2 < nch)
            def _():
                read(c + 2, buf0, 0).start()

            read(c + 1, buf1, 1).wait()
            scatter(c + 1, buf1)

    return pl.kernel(
        body,
        out_type=jax.ShapeDtypeStruct((n_rows, width), xp.dtype),
        mesh=_sc_mesh(),
        scratch_types=[
            pltpu.VMEM((nch, chunk), I32),
            pltpu.VMEM((nch, chunk), I32),
            pltpu.VMEM((chunk, width), xp.dtype),
            pltpu.VMEM((chunk, width), xp.dtype),
            pltpu.SemaphoreType.DMA((2,)),
            pltpu.SemaphoreType.DMA((2,)),
        ],
        compiler_params=_sc_params(),
        name="sc_dispatch",
    )(xp, dest2d)


def _sc_combine(ys, dest2d):
    n_idx_rows, chunk = dest2d.shape
    width = ys.shape[1]
    nch = n_idx_rows // SC_WORKERS
    assert n_idx_rows % (SC_WORKERS * 2) == 0

    def body(y_hbm, dest_hbm, out_hbm, idx_v, buf0, buf1, sem_g):
        wid = _worker_id()
        base = wid * nch * chunk
        pltpu.sync_copy(dest_hbm.at[pl.ds(wid * nch, nch)], idx_v)

        def gather(c, buf, k):
            return pltpu.make_async_copy(y_hbm.at[idx_v.at[c]], buf, sem_g.at[k])

        def write(c, buf):
            pltpu.sync_copy(buf, out_hbm.at[pl.ds(base + c * chunk, chunk)])

        gather(0, buf0, 0).start()

        @pl.loop(0, nch // 2)
        def _(g):
            c = 2 * g
            gather(c + 1, buf1, 1).start()
            gather(c, buf0, 0).wait()
            write(c, buf0)

            @pl.when(c + 2 < nch)
            def _():
                gather(c + 2, buf0, 0).start()

            gather(c + 1, buf1, 1).wait()
            write(c + 1, buf1)

    return pl.kernel(
        body,
        out_type=jax.ShapeDtypeStruct((n_idx_rows * chunk, width), ys.dtype),
        mesh=_sc_mesh(),
        scratch_types=[
            pltpu.VMEM((nch, chunk), I32),
            pltpu.VMEM((chunk, width), ys.dtype),
            pltpu.VMEM((chunk, width), ys.dtype),
            pltpu.SemaphoreType.DMA((2,)),
        ],
        compiler_params=_sc_params(),
        name="sc_combine",
    )(ys, dest2d)


def _pack_bf16_pairs(v):
    half = v.shape[1] // 2
    lo = pltpu.bitcast(v[:, :half].astype(BF16).astype(F32), jnp.uint32) >> 16
    hi = pltpu.bitcast(v[:, half:].astype(BF16).astype(F32), jnp.uint32) & jnp.uint32(0xFFFF0000)
    return pltpu.bitcast(hi | lo, I32)


def _unpack_bf16_pairs(w):
    w = pltpu.bitcast(w, jnp.uint32)
    lo = pltpu.bitcast(w << 16, F32)
    hi = pltpu.bitcast(w & jnp.uint32(0xFFFF0000), F32)
    return jnp.concatenate([lo, hi], axis=1)


def _ffn_kernel(be_ref, nv_ref, nu_ref, nxt_ref, xs_ref, wg_hbm, wu_hbm, wd_hbm, o_ref,
                wg_f, wu_f, wd_f, wg_s, wu_s, wd_s, sem):
    i = pl.program_id(0)

    def fetch(e):
        return (pltpu.make_async_copy(wg_hbm.at[e], wg_f, sem.at[0]),
                pltpu.make_async_copy(wu_hbm.at[e], wu_f, sem.at[1]),
                pltpu.make_async_copy(wd_hbm.at[e], wd_f, sem.at[2]))

    @pl.when(nxt_ref[i] >= 0)
    def _():
        @pl.when(i == 0)
        def _():
            for cp in fetch(be_ref[0]):
                cp.start()

        for cp in fetch(be_ref[i]):
            cp.wait()
        wg_s[...] = wg_f[...].astype(BF16)
        wu_s[...] = wu_f[...].astype(BF16)
        wd_s[...] = wd_f[...].astype(BF16)

        @pl.when(nxt_ref[i] < N_EXPERTS)
        def _():
            for cp in fetch(nxt_ref[i]):
                cp.start()

    @pl.when(i < nu_ref[0])
    def _():
        live = lax.broadcasted_iota(I32, xs_ref.shape, 0) < nv_ref[i]
        xb = _unpack_bf16_pairs(jnp.where(live, xs_ref[...], 0)).astype(BF16)
        act = jax.nn.silu(_dot(xb, wg_s[...])) * _dot(xb, wu_s[...])
        o_ref[...] = _pack_bf16_pairs(_dot(act.astype(BF16), wd_s[...]))


def _expert_ffn(xs, blk_expert, blk_valid, n_used, blk_next, w_gate, w_up, w_down):
    n_rows, half = xs.shape
    d = 2 * half
    de = w_gate.shape[2]
    bm = EXPERT_ROWS
    rows = lambda i, be, nv, nu, nx: (jnp.minimum(i, nu[0] - 1), 0)
    hbm = pl.BlockSpec(memory_space=pl.ANY)
    return pl.pallas_call(
        _ffn_kernel,
        grid_spec=pltpu.PrefetchScalarGridSpec(
            num_scalar_prefetch=4,
            grid=(n_rows // bm,),
            in_specs=[pl.BlockSpec((bm, half), rows), hbm, hbm, hbm],
            out_specs=pl.BlockSpec((bm, half), rows),
            scratch_shapes=[
                pltpu.VMEM((d, de), F32), pltpu.VMEM((d, de), F32), pltpu.VMEM((de, d), F32),
                pltpu.VMEM((d, de), BF16), pltpu.VMEM((d, de), BF16), pltpu.VMEM((de, d), BF16),
                pltpu.SemaphoreType.DMA((3,)),
            ],
        ),
        out_shape=jax.ShapeDtypeStruct((n_rows, half), I32),
        compiler_params=_tc_params(1),
        name="expert_ffn",
    )(blk_expert, blk_valid, n_used, blk_next, xs, w_gate, w_up, w_down)


def _combine_kernel(x_ref, y0_ref, y1_ref, rt_ref, g_ref, b_ref, o_ref):
    o_ref[...] = _expert_combine_ln(x_ref[...], y0_ref[...], y1_ref[...], rt_ref[...], g_ref[...], b_ref[...])


def _combine_kernel_into(x_ref, y0_ref, y1_ref, rt_ref, g_ref, b_ref, full_ref, o_ref):
    del full_ref
    _combine_kernel(x_ref, y0_ref, y1_ref, rt_ref, g_ref, b_ref, o_ref)


def _combine_ln(x2, y01, rt, g3, b3, into=None, row0=0, full_rows=None):
    t_all, d = x2.shape
    tq = TOKEN_TILE
    nt = t_all // tq
    blk0 = row0 // tq if full_rows else 0
    in_specs = [
        pl.BlockSpec((tq, d), lambda i: (i, 0)),
        pl.BlockSpec((tq, d // 2), lambda i: (i, 0)),
        pl.BlockSpec((tq, d // 2), lambda i: (i + nt, 0)),
        pl.BlockSpec((8, tq), lambda i: (0, i)),
        _const_spec((1, d)), _const_spec((1, d)),
    ]
    args = [x2, y01, y01, rt, g3, b3]
    body, aliases = _combine_kernel, {}
    if into is not None:
        in_specs.append(pl.BlockSpec(memory_space=pl.ANY))
        args.append(into)
        body, aliases = _combine_kernel_into, {len(args) - 1: 0}
    return pl.pallas_call(
        body,
        grid=(nt,),
        in_specs=in_specs,
        out_specs=pl.BlockSpec((tq, d), lambda i: (i + blk0, 0)),
        out_shape=jax.ShapeDtypeStruct((full_rows or t_all, d), F32),
        input_output_aliases=aliases,
        compiler_params=_tc_params(1),
        name="combine_ln",
    )(*args)


def _routing_tables(rt, counts):
    bm = EXPERT_ROWS
    t_all = rt.shape[1]
    n_rows = 2 * t_all + N_EXPERTS * bm
    cnt = counts[:, 0]
    padded = (cnt + bm - 1) // bm * bm
    ends = jnp.cumsum(padded)
    offs = ends - padded
    experts = rt[0:2]
    off_tok = jnp.sum(jnp.where(experts[None] == jnp.arange(N_EXPERTS, dtype=I32)[:, None, None],
                                offs[:, None, None], 0), axis=0)
    dest2d = (off_tok + rt[2:4]).reshape(-1, SC_CHUNK).astype(I32)
    blk_start = jnp.arange(n_rows // bm, dtype=I32) * bm
    blk_expert = jnp.minimum(jnp.sum(blk_start[:, None] >= ends[None, :], axis=1), N_EXPERTS - 1).astype(I32)
    live_end = jnp.sum(jnp.where(blk_expert[:, None] == jnp.arange(N_EXPERTS, dtype=I32)[None, :],
                                 (offs + cnt)[None, :], 0), axis=1)
    blk_valid = jnp.clip(live_end - blk_start, 0, bm).astype(I32)
    n_used = (ends[-1:] // bm).astype(I32)
    eid = jnp.arange(N_EXPERTS, dtype=I32)
    later_present = jnp.logical_and(eid[None, :] > eid[:, None], (cnt > 0)[None, :])
    next_present = jnp.min(jnp.where(later_present, eid[None, :], N_EXPERTS), axis=1)
    next_of_blk = jnp.sum(jnp.where(blk_expert[:, None] == eid[None, :], next_present[None, :], 0), axis=1)
    prev_expert = jnp.concatenate([jnp.full((1,), -1, I32), blk_expert[:-1]])
    is_first = jnp.logical_and(blk_start < ends[-1], blk_expert != prev_expert)
    blk_next = jnp.where(is_first, next_of_blk, -1).astype(I32)
    return dest2d, blk_expert, blk_valid, n_used, blk_next, n_rows


def _layer_tail(x1, kv, mem_len, p, router, b0, bsz, s_len):
    rw_hi, rw_lo, rbias = router
    x2, xp, rt, counts = _xattn_route(x1, kv, mem_len, p["xq"], p["xo"], p["ln2_g"], p["ln2_b"],
                                      rw_hi, rw_lo, rbias, b0, bsz, s_len)
    dest2d, blk_expert, blk_valid, n_used, blk_next, n_rows = _routing_tables(rt, counts)
    xs = _sc_dispatch(xp, dest2d, n_rows)
    ys = _expert_ffn(xs, blk_expert, blk_valid, n_used, blk_next, p["e_gate"], p["e_up"], p["e_down"])
    y01 = _sc_combine(ys, dest2d)
    return x2, y01, rt


def _row(v):
    return v.reshape(1, -1).astype(F32)


def _common_params(xq, xkv, xo, ln2_g, ln2_b, e_gate, e_up, e_down, ln3_g, ln3_b):
    return dict(xq=xq, xkv=xkv, xo=xo, ln2_g=_row(ln2_g), ln2_b=_row(ln2_b),
                e_gate=e_gate, e_up=e_up, e_down=e_down, ln3_g=_row(ln3_g), ln3_b=_row(ln3_b))


def kernel(x, mem, positions, router_w, router_bias, l0_w_in, l0_sinks, l0_sgu_ln_g, l0_sgu_ln_b, l0_sgu_w, l0_sgu_b, l0_w_out, l0_ln1_g, l0_ln1_b, l0_xq, l0_xkv, l0_xo, l0_ln2_g, l0_ln2_b, l0_e_gate, l0_e_up, l0_e_down, l0_ln3_g, l0_ln3_b, l1_w_in, l1_pool_w, l1_pool_scale, l1_w_out, l1_ln1_g, l1_ln1_b, l1_xq, l1_xkv, l1_xo, l1_ln2_g, l1_ln2_b, l1_e_gate, l1_e_up, l1_e_down, l1_ln3_g, l1_ln3_b):
    bsz, s_len, d = x.shape
    assert s_len % TOKEN_TILE == 0 and s_len % MIXER0_TILE == 0 and MIXER0_TILE % BLOCK == 0
    xt = x.reshape(bsz * s_len, d)
    mem2d = mem.reshape(-1, d)

    rw_t = router_w.T.astype(F32)
    rw_hi = rw_t.astype(BF16)
    rw_lo = (rw_t - rw_hi.astype(F32)).astype(BF16)
    router = (rw_hi, rw_lo, router_bias.reshape(-1, 1).astype(F32))

    half = ROPE_DIM // 2
    inv_freq = (ROPE_THETA ** (-(jnp.arange(half, dtype=F32) * 2.0 / ROPE_DIM))).reshape(half, 1)
    etab_np = np.zeros((128, 3 * 128), np.float32)
    cbase_np = np.ones((1, 128), np.float32)
    for ln in range(128):
        dd = ln % HEAD_DIM
        if dd < ROPE_DIM:
            cbase_np[0, ln] = 0.0
            etab_np[[dd % half, half + dd % half], ln] = 1.0
            if dd >= half:
                etab_np[[2 * half + dd - half, 3 * half + dd - half], 128 + ln] = 1.0
            else:
                etab_np[[2 * half + dd, 3 * half + dd], 256 + ln] = -1.0
    etab = jnp.asarray(etab_np, BF16)
    cbase = jnp.asarray(cbase_np)
    pos_row = positions.reshape(1, -1).astype(I32)
    grp = jnp.arange(B_WIDTH) // B_GROUP_DIM
    gsum = (grp[:, None] == grp[None, :]).astype(BF16)
    bs_full = jnp.repeat(l0_sgu_b.T.astype(F32), B_GROUP_DIM, axis=1)

    p0 = _common_params(l0_xq, l0_xkv, l0_xo, l0_ln2_g, l0_ln2_b, l0_e_gate, l0_e_up, l0_e_down,
                        l0_ln3_g, l0_ln3_b)
    p1 = _common_params(l1_xq, l1_xkv, l1_xo, l1_ln2_g, l1_ln2_b, l1_e_gate, l1_e_up, l1_e_down,
                        l1_ln3_g, l1_ln3_b)
    kv0 = _kv_proj(mem2d, p0["xkv"], bsz)
    kv1 = _kv_proj(mem2d, p1["xkv"], bsz)
    mem_len = mem.shape[1]

    n_split = BATCH_SPLIT if bsz % BATCH_SPLIT == 0 else 1
    nb = bsz // n_split
    out = None
    for part in range(n_split):
        b0 = part * nb
        x1 = _mixer0(xt, pos_row, l0_sinks.astype(F32), l0_w_in, inv_freq, etab, cbase, gsum,
                     _row(l0_sgu_ln_g), _row(l0_sgu_ln_b), l0_sgu_w.astype(F32), bs_full,
                     l0_w_out, _row(l0_ln1_g), _row(l0_ln1_b), b0, nb, s_len)
        x2, y01, rt = _layer_tail(x1, kv0, mem_len, p0, router, b0, nb, s_len)
        x1 = _mixer1(x2, y01, rt, p0["ln3_g"], p0["ln3_b"], l1_w_in, l1_pool_w, _row(l1_pool_scale),
                     l1_w_out, _row(l1_ln1_g), _row(l1_ln1_b), nb, s_len)
        x2, y01, rt = _layer_tail(x1, kv1, mem_len, p1, router, b0, nb, s_len)
        out = _combine_ln(x2, y01, rt, p1["ln3_g"], p1["ln3_b"], into=out, row0=b0 * s_len,
                          full_rows=bsz * s_len)
    return out.reshape(bsz, s_len, d)
```

```python
import functools

import numpy as np
import jax
import jax.numpy as jnp
from jax import lax
from jax.experimental import pallas as pl
from jax.experimental.pallas import tpu as pltpu
from jax.experimental.pallas import tpu_sc as plsc

F32 = jnp.float32
BF16 = jnp.bfloat16
I32 = jnp.int32

DEPTH = 2
ALPHA = (2.0 * DEPTH) ** 0.25
LN_EPS = 1e-5

HEAD_DIM = 64
A_Q_HEADS = 8
A_KV_HEADS = 2
A_GROUP = A_Q_HEADS // A_KV_HEADS
BLOCK = 128
ROPE_THETA = 500000.0
ROPE_DIM = HEAD_DIM // 4
A_WIDTH = A_Q_HEADS * HEAD_DIM
KV_WIDTH = A_KV_HEADS * HEAD_DIM
B_GROUPS = 8
B_GROUP_DIM = 64
B_WIDTH = B_GROUPS * B_GROUP_DIM
POOL_WINDOWS = (2, 4, 8, 16)
POOL_HALO = 16
X_HEADS = 4
N_EXPERTS = 16
N_EXPERT_GROUPS = 4
EXPERTS_PER_GROUP = 4

TOKEN_TILE = 1024
MIXER0_TILE = 1024
SLAB_ROWS = 512
EXPERT_ROWS = 512
FFN_GROUP = 4
BATCH_SPLIT = 2
SC_WORKERS = 32
SC_CHUNK = 64
VMEM_LIMIT = 56 * 1024 * 1024
NEG_BIG = -1e30


def _layer_norm(z, g, b):
    mu = jnp.mean(z, axis=-1, keepdims=True)
    d = z - mu
    var = jnp.mean(d * d, axis=-1, keepdims=True)
    return d * lax.rsqrt(var + LN_EPS) * g + b


def _dot(a, b):
    return jnp.dot(a, b, preferred_element_type=F32)


def _dot_nt(a, b):
    return lax.dot_general(a, b, (((1,), (1,)), ((), ())), preferred_element_type=F32)


def _split_bf16(v):
    hi = v.astype(BF16)
    lo = (v - hi.astype(F32)).astype(BF16)
    return hi, lo


def _tc_params(n_axes):
    return pltpu.CompilerParams(dimension_semantics=("arbitrary",) * n_axes,
                                vmem_limit_bytes=VMEM_LIMIT)


def _const_spec(shape):
    nd = len(shape)
    return pl.BlockSpec(shape, lambda *_: (0,) * nd, pipeline_mode=pl.Buffered(1))

def _mixer0_kernel(sinks_ref, x_ref, pos_ref, win_ref, invf_ref, etab_ref, cbase_ref, gsum_ref,
                   lng_ref, lnb_ref, ws_ref, bs_ref, wout_ref, g1_ref, b1_ref,
                   o_ref, q_s, kv_s, u_s, vn_s, mix_s, wt_s, win_s, wout_s):
    b = pl.program_id(0)
    j = pl.program_id(1)
    tq = x_ref.shape[0]
    nblk = tq // BLOCK
    kvw = kv_s.shape[1]

    @pl.when(jnp.logical_and(b == 0, j == 0))
    def _():
        win_s[...] = win_ref[...].astype(BF16)
        wout_s[...] = wout_ref[...].astype(BF16)
        r = lax.broadcasted_iota(I32, (BLOCK, BLOCK), 0)
        c = lax.broadcasted_iota(I32, (BLOCK, BLOCK), 1)
        for g in range(B_GROUPS):
            wt_s[g] = jnp.where(c <= r, ws_ref[g], 0.0).astype(BF16)

    @pl.when(j == 0)
    def _():
        kv_s[0:BLOCK, :] = jnp.zeros((BLOCK, kvw), BF16)

    c1 = A_WIDTH
    c2 = c1 + KV_WIDTH
    c3 = c2 + KV_WIDTH
    c4 = c3 + B_WIDTH

    def rotary_tables(rows):
        n = rows.stop - rows.start
        ang = invf_ref[...] * pos_ref[:, rows].astype(F32)
        c8 = jnp.cos(ang)
        s8 = jnp.sin(ang)
        c8h = c8.astype(BF16).astype(F32)
        s8h = s8.astype(BF16).astype(F32)
        stack = jnp.concatenate([c8h, c8 - c8h, s8h, s8 - s8h, jnp.zeros((128 - 4 * 8, n), F32)], axis=0)
        tabs = _dot(stack.T.astype(BF16), etab_ref[...])
        return tabs[:, 0:128] + cbase_ref[...], tabs[:, 128:256], tabs[:, 256:384]

    def prepare(rows, h, tables):
        n = rows.stop - rows.start
        cs, sa, sb = tables

        def rope(t):
            return t * cs + pltpu.roll(t, ROPE_DIM // 2, 1) * sa + pltpu.roll(t, 128 - ROPE_DIM // 2, 1) * sb

        for c in range(A_WIDTH // 128):
            t = h[:, c * 128:(c + 1) * 128] * (HEAD_DIM ** -0.5)
            q_s[rows, c * 128:(c + 1) * 128] = rope(t).astype(BF16)
        low = lax.broadcasted_iota(I32, (n, 128), 1) < HEAD_DIM
        kr = rope(h[:, c1:c2])
        kx = pltpu.roll(kr, HEAD_DIM, 1)
        vr = h[:, c2:c3]
        vx = pltpu.roll(vr, HEAD_DIM, 1)
        kv_cols = [jnp.where(low, kr, kx), jnp.where(low, kx, kr),
                   jnp.where(low, vr, 0.0), jnp.where(low, 0.0, vx),
                   jnp.where(low, vx, 0.0), jnp.where(low, 0.0, vr)]
        for c, col in enumerate(kv_cols):
            kv_s[BLOCK + rows.start:BLOCK + rows.stop, c * 128:(c + 1) * 128] = col.astype(BF16)

        u_s[rows, :] = jax.nn.gelu(h[:, c3:c4])
        v = jax.nn.gelu(h[:, c4:])
        gsum = gsum_ref[...]
        mean = _dot(v.astype(BF16), gsum) * (1.0 / B_GROUP_DIM)
        d = v - mean
        var = _dot((d * d).astype(BF16), gsum) * (1.0 / B_GROUP_DIM)
        vn_s[rows, :] = (d * lax.rsqrt(var + LN_EPS) * lng_ref[...] + lnb_ref[...]).astype(BF16)

    qi = lax.broadcasted_iota(I32, (BLOCK, 2 * BLOCK), 0)
    kj = lax.broadcasted_iota(I32, (BLOCK, 2 * BLOCK), 1)
    rel = qi + BLOCK - kj
    band = jnp.logical_and(rel >= 0, rel < BLOCK)
    low_q = lax.broadcasted_iota(I32, (BLOCK, 128), 1) < HEAD_DIM
    low_k = lax.broadcasted_iota(I32, (2 * BLOCK, 128), 1) < HEAD_DIM
    ones_lo = jnp.where(low_k, 1.0, 0.0).astype(BF16)
    ones_hi = jnp.where(low_k, 0.0, 1.0).astype(BF16)
    zero_q = jnp.zeros((BLOCK, 128), BF16)

    def block_body(n):
        r0 = n * BLOCK
        kv = kv_s[pl.ds(r0, 2 * BLOCK), :]
        qb = q_s[pl.ds(r0, BLOCK), :]
        valid = jnp.logical_and(band, kj >= jnp.where(j == 0, BLOCK, 0)) if n == 0 else band
        heads = [(c, half) for c in range(A_WIDTH // 128) for half in range(2)]
        scores = []
        for c, half in heads:
            qp = qb[:, c * 128:(c + 1) * 128]
            qm = jnp.where(low_q, qp, zero_q) if half == 0 else jnp.where(low_q, zero_q, qp)
            hk = (2 * c) // A_GROUP
            scores.append(_dot_nt(qm, kv[:, hk * 128:(hk + 1) * 128]))
        vnb = vn_s[pl.ds(r0, BLOCK), :]
        parts = []
        for c in range(B_WIDTH // 128):
            vp = vnb[:, c * 128:(c + 1) * 128]
            parts.append(_dot(wt_s[2 * c], jnp.where(low_q, vp, zero_q))
                         + _dot(wt_s[2 * c + 1], jnp.where(low_q, zero_q, vp)))
        res = [None] * (A_WIDTH // 128)
        esink = {}
        for (c, half), sc in zip(heads, scores):
            hk = (2 * c) // A_GROUP
            vcol = kv[:, (2 + 2 * hk + half) * 128:(3 + 2 * hk + half) * 128]
            vm = jnp.concatenate([vcol, ones_lo if half == 0 else ones_hi], axis=1)
            s = jnp.where(valid, sc, NEG_BIG)
            sink = sinks_ref[2 * c + half]
            m = jnp.maximum(jnp.max(s, axis=-1, keepdims=True), sink)
            pv = _dot(jnp.exp(s - m).astype(BF16), vm)
            res[c] = pv if res[c] is None else res[c] + pv
            esink[c, half] = jnp.exp(sink - m)
        for c in range(A_WIDTH // 128):
            den = res[c][:, 128:] + jnp.where(low_q, esink[c, 0], esink[c, 1])
            mix_s[pl.ds(r0, BLOCK), c * 128:(c + 1) * 128] = (res[c][:, :128] / den).astype(BF16)
        mixed = jnp.concatenate(parts, axis=1) + bs_ref[...]
        mix_s[pl.ds(r0, BLOCK), A_WIDTH:] = (u_s[pl.ds(r0, BLOCK), :] * mixed).astype(BF16)

    slab = min(SLAB_ROWS, tq)
    slabs = [slice(r, r + slab) for r in range(0, tq, slab)]
    tables = [rotary_tables(rows) for rows in slabs]
    xs = [x_ref[rows, :] for rows in slabs]
    hs = [_dot(x.astype(BF16), win_s[...]) for x in xs]
    zs = []
    for rows, x, h, tab in zip(slabs, xs, hs, tables):
        prepare(rows, h, tab)
        for n in range(rows.start // BLOCK, rows.stop // BLOCK):
            block_body(n)
        zs.append(ALPHA * x + _dot(mix_s[rows, :], wout_s[...]))
    kv_s[0:BLOCK, :] = kv_s[tq:tq + BLOCK, :]
    for rows, z in zip(slabs, zs):
        o_ref[rows, :] = _layer_norm(z, g1_ref[...], b1_ref[...])


def _mixer0(x, pos_row, sinks, w_in, invf, etab, cbase, gsum, lng, lnb, w_s, bs_full, w_out, g1, b1,
            b0, bsz, s_len):
    d = x.shape[1]
    t_all = bsz * s_len
    tq = MIXER0_TILE
    nj = s_len // tq
    row = lambda bb, jj: (bb * nj + jj, 0)
    in_w = w_in.shape[1]
    return pl.pallas_call(
        _mixer0_kernel,
        grid=(bsz, nj),
        in_specs=[
            pl.BlockSpec(memory_space=pltpu.SMEM),
            pl.BlockSpec((tq, d), lambda bb, jj: ((b0 + bb) * nj + jj, 0)),
            pl.BlockSpec((1, tq), lambda bb, jj: (0, (b0 + bb) * nj + jj)),
            _const_spec((d, in_w)),
            _const_spec((ROPE_DIM // 2, 1)), _const_spec((128, 3 * 128)), _const_spec((1, 128)),
            _const_spec((B_WIDTH, B_WIDTH)),
            _const_spec((1, B_WIDTH)), _const_spec((1, B_WIDTH)),
            _const_spec((B_GROUPS, BLOCK, BLOCK)),
            _const_spec((BLOCK, B_WIDTH)),
            _const_spec((A_WIDTH + B_WIDTH, d)),
            _const_spec((1, d)), _const_spec((1, d)),
        ],
        out_specs=pl.BlockSpec((tq, d), row),
        out_shape=jax.ShapeDtypeStruct((t_all, d), F32),
        scratch_shapes=[
            pltpu.VMEM((tq, A_WIDTH), BF16),
            pltpu.VMEM((tq + BLOCK, 6 * 128), BF16),
            pltpu.VMEM((tq, B_WIDTH), F32),
            pltpu.VMEM((tq, B_WIDTH), BF16),
            pltpu.VMEM((tq, A_WIDTH + B_WIDTH), BF16),
            pltpu.VMEM((B_GROUPS, BLOCK, BLOCK), BF16),
            pltpu.VMEM((d, in_w), BF16),
            pltpu.VMEM((A_WIDTH + B_WIDTH, d), BF16),
        ],
        compiler_params=_tc_params(2),
        name="mixer0",
    )(sinks, x, pos_row, w_in, invf, etab, cbase, gsum, lng, lnb, w_s, bs_full, w_out, g1, b1)


def _expert_combine_ln(x2, y0_packed, y1_packed, rt, g, b):
    wt = pltpu.bitcast(rt, F32).T
    y = wt[:, 4:5] * _unpack_bf16_pairs(y0_packed) + wt[:, 5:6] * _unpack_bf16_pairs(y1_packed)
    return _layer_norm(ALPHA * x2 + y, g, b)


def _mixer1_kernel(x2_ref, y0_ref, y1_ref, rt_ref, g3_ref, b3_ref,
                   win_ref, pw_ref, ps_ref, wout_ref, g1_ref, b1_ref, o_ref,
                   h_s, mp_s, win_s, pw_s, wout_s):
    j = pl.program_id(1)
    tq = x2_ref.shape[0]
    gw = x2_ref.shape[1] // len(POOL_WINDOWS)
    slabs = [slice(c * SLAB_ROWS, (c + 1) * SLAB_ROWS) for c in range(tq // SLAB_ROWS)]

    @pl.when(jnp.logical_and(pl.program_id(0) == 0, j == 0))
    def _():
        win_s[...] = win_ref[...].astype(BF16)
        pw_s[...] = pw_ref[...].astype(BF16)
        wout_s[...] = wout_ref[...].astype(BF16)

    @pl.when(j == 0)
    def _():
        h_s[0:POOL_HALO, :] = jnp.zeros((POOL_HALO, h_s.shape[1]), F32)

    xs = []
    for rows in slabs:
        x = _expert_combine_ln(x2_ref[rows, :], y0_ref[rows, :], y1_ref[rows, :], rt_ref[:, rows],
                               g3_ref[...], b3_ref[...])
        h_s[POOL_HALO + rows.start:POOL_HALO + rows.stop, :] = _dot(x.astype(BF16), win_s[...])
        xs.append(x)

    for rows, x in zip(slabs, xs):
        n = rows.stop - rows.start
        t_pos = j * tq + rows.start + lax.broadcasted_iota(I32, (n, 1), 0)
        for g, win in enumerate(POOL_WINDOWS):
            lo, hi = g * gw, (g + 1) * gw
            ext = h_s[rows.start:rows.stop + POOL_HALO, lo:hi]
            acc = ext
            shift = 1
            while shift < win:
                acc = acc + pltpu.roll(acc, shift, 0)
                shift *= 2
            count = jnp.minimum(t_pos + 1, win).astype(F32)
            pooled = acc[POOL_HALO:, :] / count - ext[POOL_HALO:, :]
            mapped = _dot(pooled.astype(BF16), pw_s[g])
            mp_s[rows, lo:hi] = (mapped * ps_ref[:, lo:hi]).astype(BF16)
        z = ALPHA * x + _dot(mp_s[rows, :], wout_s[...])
        o_ref[rows, :] = _layer_norm(z, g1_ref[...], b1_ref[...])
    h_s[0:POOL_HALO, :] = h_s[tq:tq + POOL_HALO, :]


def _mixer1(x2, y01, rt, g3, b3, w_in, pool_w, pool_scale, w_out, g1, b1, bsz, s_len):
    t_all, d = x2.shape
    tq = TOKEN_TILE
    nj = s_len // tq
    nt = bsz * nj
    row = lambda bb, jj: (bb * nj + jj, 0)
    ng = len(POOL_WINDOWS)
    return pl.pallas_call(
        _mixer1_kernel,
        grid=(bsz, nj),
        in_specs=[
            pl.BlockSpec((tq, d), row),
            pl.BlockSpec((tq, d // 2), row),
            pl.BlockSpec((tq, d // 2), lambda bb, jj: (nt + bb * nj + jj, 0)),
            pl.BlockSpec((8, tq), lambda bb, jj: (0, bb * nj + jj)),
            _const_spec((1, d)), _const_spec((1, d)),
            _const_spec((d, d)),
            _const_spec((ng, d // ng, d // ng)),
            _const_spec((1, d)),
            _const_spec((d, d)),
            _const_spec((1, d)), _const_spec((1, d)),
        ],
        out_specs=pl.BlockSpec((tq, d), row),
        out_shape=jax.ShapeDtypeStruct((t_all, d), F32),
        scratch_shapes=[pltpu.VMEM((tq + POOL_HALO, d), F32), pltpu.VMEM((tq, d), BF16),
                        pltpu.VMEM((d, d), BF16), pltpu.VMEM((ng, d // ng, d // ng), BF16),
                        pltpu.VMEM((d, d), BF16)],
        compiler_params=_tc_params(2),
        name="mixer1",
    )(x2, y01, y01, rt, g3, b3, w_in, pool_w, pool_scale, w_out, g1, b1)


def _kv_kernel(mem_ref, w_ref, o_ref, w_s):
    @pl.when(pl.program_id(0) == 0)
    def _():
        w_s[...] = w_ref[...].astype(BF16)

    o_ref[...] = _dot(mem_ref[...].astype(BF16), w_s[...]).astype(BF16)


def _kv_proj(mem2d, wkv, bsz):
    rows, d = mem2d.shape
    m = rows // bsz
    return pl.pallas_call(
        _kv_kernel,
        grid=(bsz,),
        in_specs=[pl.BlockSpec((m, d), lambda i: (i, 0)), _const_spec(wkv.shape)],
        out_specs=pl.BlockSpec((m, wkv.shape[1]), lambda i: (i, 0)),
        out_shape=jax.ShapeDtypeStruct((rows, wkv.shape[1]), BF16),
        scratch_shapes=[pltpu.VMEM(wkv.shape, BF16)],
        compiler_params=_tc_params(1),
        name="kv_proj",
    )(mem2d, wkv)


def _top2_of4(v):
    hi01, lo01 = jnp.maximum(v[0], v[1]), jnp.minimum(v[0], v[1])
    hi23, lo23 = jnp.maximum(v[2], v[3]), jnp.minimum(v[2], v[3])
    return jnp.maximum(hi01, hi23) + jnp.maximum(jnp.minimum(hi01, hi23), jnp.maximum(lo01, lo23))


def _argmax_first(vals):
    best, idx = vals[0], jnp.zeros(vals[0].shape, I32)
    for i in range(1, len(vals)):
        better = vals[i] > best
        best = jnp.where(better, vals[i], best)
        idx = jnp.where(better, i, idx)
    return best, idx


def _xattn_kernel(x_ref, kv_ref, wq_ref, wo_ref, g2_ref, b2_ref, rwh_ref, rwl_ref, rb_ref,
                  x2_ref, xp_ref, rt_ref, cnt_ref, run_s, wq_s, wo_s):
    first = jnp.logical_and(pl.program_id(0) == 0, pl.program_id(1) == 0)
    tq, d = x_ref.shape

    @pl.when(first)
    def _():
        run_s[...] = jnp.zeros(run_s.shape, F32)
        wq_s[...] = wq_ref[...].astype(BF16)
        wo_s[...] = wo_ref[...].astype(BF16)

    slabs = [slice(c * SLAB_ROWS, (c + 1) * SLAB_ROWS) for c in range(tq // SLAB_ROWS)]
    run = run_s[...]
    zs = [_xattn_attend(*_xattn_query(rows, x_ref, wq_s), kv_ref, wo_s) for rows in slabs]
    logits = [_xattn_norm_logits(rows, z, g2_ref, b2_ref, rwh_ref, rwl_ref, x2_ref, xp_ref)
              for rows, z in zip(slabs, zs)]
    for rows, lg in zip(slabs, logits):
        run = _xattn_route_rows(rows, lg, run, rb_ref, rt_ref)
    run_s[...] = run
    cnt_ref[...] = jnp.broadcast_to(run, cnt_ref.shape).astype(I32)


def _xattn_query(rows, x_ref, wq_s):
    hd = x_ref.shape[1] // X_HEADS
    x = x_ref[rows, :]
    return x, _dot(x.astype(BF16), wq_s[...]) * (hd ** -0.5)


def _xattn_attend(x, q, kv_ref, wo_s):
    d = x.shape[1]
    hd = d // X_HEADS
    outs = []
    for hx in range(X_HEADS):
        qh = q[:, hx * hd:(hx + 1) * hd].astype(BF16)
        kh = kv_ref[:, hx * hd:(hx + 1) * hd]
        vh = kv_ref[:, d + hx * hd:d + (hx + 1) * hd]
        s = _dot_nt(qh, kh)
        p = jnp.exp(s - jnp.max(s, axis=-1, keepdims=True))
        o = _dot(p.astype(BF16), vh) / jnp.sum(p, axis=-1, keepdims=True)
        outs.append(o.astype(BF16))
    return ALPHA * x + _dot(jnp.concatenate(outs, axis=1), wo_s[...])


def _xattn_norm_logits(rows, z, g2_ref, b2_ref, rwh_ref, rwl_ref, x2_ref, xp_ref):
    x2 = _layer_norm(z, g2_ref[...], b2_ref[...])
    x2_ref[rows, :] = x2

    xp_ref[rows, :] = _pack_bf16_pairs(x2)

    xh, xl = _split_bf16(x2)
    return _dot_nt(rwh_ref[...], xh) + _dot_nt(rwh_ref[...], xl) + _dot_nt(rwl_ref[...], xh)


def _xattn_route_rows(rows, logits, run, rb_ref, rt_ref):
    tq = rows.stop - rows.start
    e_max = jnp.max(logits, axis=0, keepdims=True)
    ex = jnp.exp(logits - e_max)
    scores = ex / jnp.sum(ex, axis=0, keepdims=True)
    biased = scores + rb_ref[...]
    sc = [scores[e:e + 1, :] for e in range(N_EXPERTS)]
    bi = [biased[e:e + 1, :] for e in range(N_EXPERTS)]
    epg = EXPERTS_PER_GROUP
    gscore = [_top2_of4(bi[g * epg:(g + 1) * epg]) for g in range(N_EXPERT_GROUPS)]
    _, gsel = _argmax_first(gscore)

    def pick(vals):
        return [functools.reduce(lambda acc, g: jnp.where(gsel == g, vals[g * epg + i], acc),
                                 range(1, N_EXPERT_GROUPS), vals[i]) for i in range(epg)]

    in_b = pick(bi)
    in_s = pick(sc)
    _, i0 = _argmax_first(in_b)
    _, i1 = _argmax_first([jnp.where(i0 == i, -jnp.inf, in_b[i]) for i in range(epg)])

    def take(vals, idx):
        return functools.reduce(lambda acc, i: jnp.where(idx == i, vals[i], acc), range(1, epg), vals[0])

    s0, s1 = take(in_s, i0), take(in_s, i1)
    w0, w1 = s0 / (s0 + s1), s1 / (s0 + s1)
    e0, e1 = gsel * epg + i0, gsel * epg + i1

    eid = lax.broadcasted_iota(I32, (N_EXPERTS, tq), 0)
    oh0 = eid == e0
    oh1 = eid == e1
    onehot = jnp.where(jnp.logical_or(oh0, oh1), 1.0, 0.0)
    rr = lax.broadcasted_iota(I32, (tq, tq), 0)
    cc = lax.broadcasted_iota(I32, (tq, tq), 1)
    upper = jnp.where(rr < cc, 1.0, 0.0).astype(BF16)
    prefix = _dot(onehot.astype(BF16), upper) + run
    r0 = jnp.sum(jnp.where(oh0, prefix, 0.0), axis=0, keepdims=True).astype(I32)
    r1 = jnp.sum(jnp.where(oh1, prefix, 0.0), axis=0, keepdims=True).astype(I32)

    zero = jnp.zeros((1, tq), I32)
    rt_ref[:, rows] = jnp.concatenate(
        [e0, e1, r0, r1, pltpu.bitcast(w0, I32), pltpu.bitcast(w1, I32), zero, zero], axis=0)
    return run + jnp.sum(onehot, axis=1, keepdims=True)


def _xattn_route(x1, kv, mem_len, wq, wo, g2, b2, rw_hi, rw_lo, rbias, b0, bsz, s_len):
    t_all, d = x1.shape
    tq = TOKEN_TILE
    nj = s_len // tq
    m = mem_len
    row = lambda bb, jj: (bb * nj + jj, 0)
    return pl.pallas_call(
        _xattn_kernel,
        grid=(bsz, nj),
        in_specs=[
            pl.BlockSpec((tq, d), row),
            pl.BlockSpec((m, 2 * d), lambda bb, jj: (b0 + bb, 0)),
            _const_spec((d, d)), _const_spec((d, d)),
            _const_spec((1, d)), _const_spec((1, d)),
            _const_spec((N_EXPERTS, d)), _const_spec((N_EXPERTS, d)),
            _const_spec((N_EXPERTS, 1)),
        ],
        out_specs=[
            pl.BlockSpec((tq, d), row),
            pl.BlockSpec((tq, d // 2), row),
            pl.BlockSpec((8, tq), lambda bb, jj: (0, bb * nj + jj)),
            _const_spec((N_EXPERTS, 128)),
        ],
        out_shape=[
            jax.ShapeDtypeStruct((t_all, d), F32),
            jax.ShapeDtypeStruct((t_all, d // 2), I32),
            jax.ShapeDtypeStruct((8, t_all), I32),
            jax.ShapeDtypeStruct((N_EXPERTS, 128), I32),
        ],
        scratch_shapes=[pltpu.VMEM((N_EXPERTS, 1), F32), pltpu.VMEM((d, d), BF16), pltpu.VMEM((d, d), BF16)],
        compiler_params=_tc_params(2),
        name="xattn_route",
    )(x1, kv, wq, wo, g2, b2, rw_hi, rw_lo, rbias)


def _sc_mesh():
    return plsc.VectorSubcoreMesh(core_axis_name="c", subcore_axis_name="s")


def _sc_params():
    return pltpu.CompilerParams(needs_layout_passes=False)


def _worker_id():
    return lax.axis_index("s") * lax.axis_size("c") + lax.axis_index("c")


def _sc_dispatch(xp, dest2d, n_rows):
    t_all, width = xp.shape
    chunk = dest2d.shape[1]
    tok_w = t_all // SC_WORKERS
    nch = tok_w // chunk
    slot1 = t_all // chunk
    assert t_all % (SC_WORKERS * chunk * 2) == 0

    def body(x_hbm, dest_hbm, out_hbm, idx0_v, idx1_v, buf0, buf1, sem_r, sem_w):
        wid = _worker_id()
        base = wid * tok_w
        pltpu.sync_copy(dest_hbm.at[pl.ds(wid * nch, nch)], idx0_v)
        pltpu.sync_copy(dest_hbm.at[pl.ds(slot1 + wid * nch, nch)], idx1_v)

        def read(c, buf, k):
            return pltpu.make_async_copy(x_hbm.at[pl.ds(base + c * chunk, chunk)], buf, sem_r.at[k])

        def scatter(c, buf):
            a = pltpu.make_async_copy(buf, out_hbm.at[idx0_v.at[c]], sem_w.at[0])
            b = pltpu.make_async_copy(buf, out_hbm.at[idx1_v.at[c]], sem_w.at[1])
            a.start()
            b.start()
            a.wait()
            b.wait()

        read(0, buf0, 0).start()

        @pl.loop(0, nch // 2)
        def _(g):
            c = 2 * g
            read(c + 1, buf1, 1).start()
            read(c, buf0, 0).wait()
            scatter(c, buf0)

            @pl.when(c + 2 < nch)
            def _():
                read(c + 2, buf0, 0).start()

            read(c + 1, buf1, 1).wait()
            scatter(c + 1, buf1)

    return pl.kernel(
        body,
        out_type=jax.ShapeDtypeStruct((n_rows, width), xp.dtype),
        mesh=_sc_mesh(),
        scratch_types=[
            pltpu.VMEM((nch, chunk), I32),
            pltpu.VMEM((nch, chunk), I32),
            pltpu.VMEM((chunk, width), xp.dtype),
            pltpu.VMEM((chunk, width), xp.dtype),
            pltpu.SemaphoreType.DMA((2,)),
            pltpu.SemaphoreType.DMA((2,)),
        ],
        compiler_params=_sc_params(),
        name="sc_dispatch",
    )(xp, dest2d)


def _sc_combine(ys, dest2d):
    n_idx_rows, chunk = dest2d.shape
    width = ys.shape[1]
    nch = n_idx_rows // SC_WORKERS
    assert n_idx_rows % (SC_WORKERS * 2) == 0

    def body(y_hbm, dest_hbm, out_hbm, idx_v, buf0, buf1, sem_g):
        wid = _worker_id()
        base = wid * nch * chunk
        pltpu.sync_copy(dest_hbm.at[pl.ds(wid * nch, nch)], idx_v)

        def gather(c, buf, k):
            return pltpu.make_async_copy(y_hbm.at[idx_v.at[c]], buf, sem_g.at[k])

        def write(c, buf):
            pltpu.sync_copy(buf, out_hbm.at[pl.ds(base + c * chunk, chunk)])

        gather(0, buf0, 0).start()

        @pl.loop(0, nch // 2)
        def _(g):
            c = 2 * g
            gather(c + 1, buf1, 1).start()
            gather(c, buf0, 0).wait()
            write(c, buf0)

            @pl.when(c + 2 < nch)
            def _():
                gather(c + 2, buf0, 0).start()

            gather(c + 1, buf1, 1).wait()
            write(c + 1, buf1)

    return pl.kernel(
        body,
        out_type=jax.ShapeDtypeStruct((n_idx_rows * chunk, width), ys.dtype),
        mesh=_sc_mesh(),
        scratch_types=[
            pltpu.VMEM((nch, chunk), I32),
            pltpu.VMEM((chunk, width), ys.dtype),
            pltpu.VMEM((chunk, width), ys.dtype),
            pltpu.SemaphoreType.DMA((2,)),
        ],
        compiler_params=_sc_params(),
        name="sc_combine",
    )(ys, dest2d)


def _pack_bf16_pairs(v):
    half = v.shape[1] // 2
    lo = pltpu.bitcast(v[:, :half].astype(BF16).astype(F32), jnp.uint32) >> 16
    hi = pltpu.bitcast(v[:, half:].astype(BF16).astype(F32), jnp.uint32) & jnp.uint32(0xFFFF0000)
    return pltpu.bitcast(hi | lo, I32)


def _unpack_bf16_pairs(w):
    w = pltpu.bitcast(w, jnp.uint32)
    lo = pltpu.bitcast(w << 16, F32)
    hi = pltpu.bitcast(w & jnp.uint32(0xFFFF0000), F32)
    return jnp.concatenate([lo, hi], axis=1)


def _ffn_kernel(be_ref, nv_ref, nu_ref, nxt_ref, xs_ref, wg_hbm, wu_hbm, wd_hbm, o_ref,
                wg_f, wu_f, wd_f, wg_s, wu_s, wd_s, sem):
    i = pl.program_id(0)

    def fetch(e):
        return (pltpu.make_async_copy(wg_hbm.at[e], wg_f, sem.at[0]),
                pltpu.make_async_copy(wu_hbm.at[e], wu_f, sem.at[1]),
                pltpu.make_async_copy(wd_hbm.at[e], wd_f, sem.at[2]))

    @pl.when(nxt_ref[i] >= 0)
    def _():
        @pl.when(i == 0)
        def _():
            for cp in fetch(be_ref[0]):
                cp.start()

        for cp in fetch(be_ref[i]):
            cp.wait()
        wg_s[...] = wg_f[...].astype(BF16)
        wu_s[...] = wu_f[...].astype(BF16)
        wd_s[...] = wd_f[...].astype(BF16)

        @pl.when(nxt_ref[i] < N_EXPERTS)
        def _():
            for cp in fetch(nxt_ref[i]):
                cp.start()

    @pl.when(i < nu_ref[0])
    def _():
        bm = EXPERT_ROWS
        r0 = pl.multiple_of((i % FFN_GROUP) * bm, bm)
        live = lax.broadcasted_iota(I32, (bm, xs_ref.shape[1]), 0) < nv_ref[i]
        xb = _unpack_bf16_pairs(jnp.where(live, xs_ref[pl.ds(r0, bm), :], 0)).astype(BF16)
        act = jax.nn.silu(_dot(xb, wg_s[...])) * _dot(xb, wu_s[...])
        o_ref[pl.ds(r0, bm), :] = _pack_bf16_pairs(_dot(act.astype(BF16), wd_s[...]))


def _expert_ffn(xs, blk_expert, blk_valid, n_used, blk_next, w_gate, w_up, w_down):
    n_rows, half = xs.shape
    d = 2 * half
    de = w_gate.shape[2]
    bm = EXPERT_ROWS
    rows = lambda i, be, nv, nu, nx: (jnp.minimum(i, nu[0] - 1) // FFN_GROUP, 0)
    hbm = pl.BlockSpec(memory_space=pl.ANY)
    assert (n_rows // bm) % FFN_GROUP == 0
    return pl.pallas_call(
        _ffn_kernel,
        grid_spec=pltpu.PrefetchScalarGridSpec(
            num_scalar_prefetch=4,
            grid=(n_rows // bm,),
            in_specs=[pl.BlockSpec((FFN_GROUP * bm, half), rows), hbm, hbm, hbm],
            out_specs=pl.BlockSpec((FFN_GROUP * bm, half), rows),
            scratch_shapes=[
                pltpu.VMEM((d, de), F32), pltpu.VMEM((d, de), F32), pltpu.VMEM((de, d), F32),
                pltpu.VMEM((d, de), BF16), pltpu.VMEM((d, de), BF16), pltpu.VMEM((de, d), BF16),
                pltpu.SemaphoreType.DMA((3,)),
            ],
        ),
        out_shape=jax.ShapeDtypeStruct((n_rows, half), I32),
        compiler_params=_tc_params(1),
        name="expert_ffn",
    )(blk_expert, blk_valid, n_used, blk_next, xs, w_gate, w_up, w_down)


def _combine_kernel(x_ref, y0_ref, y1_ref, rt_ref, g_ref, b_ref, o_ref):
    o_ref[...] = _expert_combine_ln(x_ref[...], y0_ref[...], y1_ref[...], rt_ref[...], g_ref[...], b_ref[...])


def _combine_kernel_into(x_ref, y0_ref, y1_ref, rt_ref, g_ref, b_ref, full_ref, o_ref):
    del full_ref
    _combine_kernel(x_ref, y0_ref, y1_ref, rt_ref, g_ref, b_ref, o_ref)


def _combine_ln(x2, y01, rt, g3, b3, into=None, row0=0, full_rows=None):
    t_all, d = x2.shape
    tq = TOKEN_TILE
    nt = t_all // tq
    blk0 = row0 // tq if full_rows else 0
    in_specs = [
        pl.BlockSpec((tq, d), lambda i: (i, 0)),
        pl.BlockSpec((tq, d // 2), lambda i: (i, 0)),
        pl.BlockSpec((tq, d // 2), lambda i: (i + nt, 0)),
        pl.BlockSpec((8, tq), lambda i: (0, i)),
        _const_spec((1, d)), _const_spec((1, d)),
    ]
    args = [x2, y01, y01, rt, g3, b3]
    body, aliases = _combine_kernel, {}
    if into is not None:
        in_specs.append(pl.BlockSpec(memory_space=pl.ANY))
        args.append(into)
        body, aliases = _combine_kernel_into, {len(args) - 1: 0}
    return pl.pallas_call(
        body,
        grid=(nt,),
        in_specs=in_specs,
        out_specs=pl.BlockSpec((tq, d), lambda i: (i + blk0, 0)),
        out_shape=jax.ShapeDtypeStruct((full_rows or t_all, d), F32),
        input_output_aliases=aliases,
        compiler_params=_tc_params(1),
        name="combine_ln",
    )(*args)


def _routing_tables(rt, counts):
    bm = EXPERT_ROWS
    t_all = rt.shape[1]
    n_rows = 2 * t_all + N_EXPERTS * bm
    cnt = counts[:, 0]
    padded = (cnt + bm - 1) // bm * bm
    ends = jnp.cumsum(padded)
    offs = ends - padded
    experts = rt[0:2]
    off_tok = jnp.sum(jnp.where(experts[None] == jnp.arange(N_EXPERTS, dtype=I32)[:, None, None],
                                offs[:, None, None], 0), axis=0)
    dest2d = (off_tok + rt[2:4]).reshape(-1, SC_CHUNK).astype(I32)
    blk_start = jnp.arange(n_rows // bm, dtype=I32) * bm
    blk_expert = jnp.minimum(jnp.sum(blk_start[:, None] >= ends[None, :], axis=1), N_EXPERTS - 1).astype(I32)
    live_end = jnp.sum(jnp.where(blk_expert[:, None] == jnp.arange(N_EXPERTS, dtype=I32)[None, :],
                                 (offs + cnt)[None, :], 0), axis=1)
    blk_valid = jnp.clip(live_end - blk_start, 0, bm).astype(I32)
    n_used = (ends[-1:] // bm).astype(I32)
    eid = jnp.arange(N_EXPERTS, dtype=I32)
    later_present = jnp.logical_and(eid[None, :] > eid[:, None], (cnt > 0)[None, :])
    next_present = jnp.min(jnp.where(later_present, eid[None, :], N_EXPERTS), axis=1)
    next_of_blk = jnp.sum(jnp.where(blk_expert[:, None] == eid[None, :], next_present[None, :], 0), axis=1)
    prev_expert = jnp.concatenate([jnp.full((1,), -1, I32), blk_expert[:-1]])
    is_first = jnp.logical_and(blk_start < ends[-1], blk_expert != prev_expert)
    blk_next = jnp.where(is_first, next_of_blk, -1).astype(I32)
    return dest2d, blk_expert, blk_valid, n_used, blk_next, n_rows


def _layer_tail(x1, kv, mem_len, p, router, b0, bsz, s_len):
    rw_hi, rw_lo, rbias = router
    x2, xp, rt, counts = _xattn_route(x1, kv, mem_len, p["xq"], p["xo"], p["ln2_g"], p["ln2_b"],
                                      rw_hi, rw_lo, rbias, b0, bsz, s_len)
    dest2d, blk_expert, blk_valid, n_used, blk_next, n_rows = _routing_tables(rt, counts)
    xs = _sc_dispatch(xp, dest2d, n_rows)
    ys = _expert_ffn(xs, blk_expert, blk_valid, n_used, blk_next, p["e_gate"], p["e_up"], p["e_down"])
    y01 = _sc_combine(ys, dest2d)
    return x2, y01, rt


def _row(v):
    return v.reshape(1, -1).astype(F32)


def _common_params(xq, xkv, xo, ln2_g, ln2_b, e_gate, e_up, e_down, ln3_g, ln3_b):
    return dict(xq=xq, xkv=xkv, xo=xo, ln2_g=_row(ln2_g), ln2_b=_row(ln2_b),
                e_gate=e_gate, e_up=e_up, e_down=e_down, ln3_g=_row(ln3_g), ln3_b=_row(ln3_b))


def kernel(x, mem, positions, router_w, router_bias, l0_w_in, l0_sinks, l0_sgu_ln_g, l0_sgu_ln_b, l0_sgu_w, l0_sgu_b, l0_w_out, l0_ln1_g, l0_ln1_b, l0_xq, l0_xkv, l0_xo, l0_ln2_g, l0_ln2_b, l0_e_gate, l0_e_up, l0_e_down, l0_ln3_g, l0_ln3_b, l1_w_in, l1_pool_w, l1_pool_scale, l1_w_out, l1_ln1_g, l1_ln1_b, l1_xq, l1_xkv, l1_xo, l1_ln2_g, l1_ln2_b, l1_e_gate, l1_e_up, l1_e_down, l1_ln3_g, l1_ln3_b):
    bsz, s_len, d = x.shape
    assert s_len % TOKEN_TILE == 0 and s_len % MIXER0_TILE == 0 and MIXER0_TILE % BLOCK == 0
    xt = x.reshape(bsz * s_len, d)
    mem2d = mem.reshape(-1, d)

    rw_t = router_w.T.astype(F32)
    rw_hi = rw_t.astype(BF16)
    rw_lo = (rw_t - rw_hi.astype(F32)).astype(BF16)
    router = (rw_hi, rw_lo, router_bias.reshape(-1, 1).astype(F32))

    half = ROPE_DIM // 2
    inv_freq = (ROPE_THETA ** (-(jnp.arange(half, dtype=F32) * 2.0 / ROPE_DIM))).reshape(half, 1)
    etab_np = np.zeros((128, 3 * 128), np.float32)
    cbase_np = np.ones((1, 128), np.float32)
    for ln in range(128):
        dd = ln % HEAD_DIM
        if dd < ROPE_DIM:
            cbase_np[0, ln] = 0.0
            etab_np[[dd % half, half + dd % half], ln] = 1.0
            if dd >= half:
                etab_np[[2 * half + dd - half, 3 * half + dd - half], 128 + ln] = 1.0
            else:
                etab_np[[2 * half + dd, 3 * half + dd], 256 + ln] = -1.0
    etab = jnp.asarray(etab_np, BF16)
    cbase = jnp.asarray(cbase_np)
    pos_row = positions.reshape(1, -1).astype(I32)
    grp = jnp.arange(B_WIDTH) // B_GROUP_DIM
    gsum = (grp[:, None] == grp[None, :]).astype(BF16)
    bs_full = jnp.repeat(l0_sgu_b.T.astype(F32), B_GROUP_DIM, axis=1)

    p0 = _common_params(l0_xq, l0_xkv, l0_xo, l0_ln2_g, l0_ln2_b, l0_e_gate, l0_e_up, l0_e_down,
                        l0_ln3_g, l0_ln3_b)
    p1 = _common_params(l1_xq, l1_xkv, l1_xo, l1_ln2_g, l1_ln2_b, l1_e_gate, l1_e_up, l1_e_down,
                        l1_ln3_g, l1_ln3_b)
    kv0 = _kv_proj(mem2d, p0["xkv"], bsz)
    kv1 = _kv_proj(mem2d, p1["xkv"], bsz)
    mem_len = mem.shape[1]

    n_split = BATCH_SPLIT if bsz % BATCH_SPLIT == 0 else 1
    nb = bsz // n_split
    out = None
    for part in range(n_split):
        b0 = part * nb
        x1 = _mixer0(xt, pos_row, l0_sinks.astype(F32), l0_w_in, inv_freq, etab, cbase, gsum,
                     _row(l0_sgu_ln_g), _row(l0_sgu_ln_b), l0_sgu_w.astype(F32), bs_full,
                     l0_w_out, _row(l0_ln1_g), _row(l0_ln1_b), b0, nb, s_len)
        x2, y01, rt = _layer_tail(x1, kv0, mem_len, p0, router, b0, nb, s_len)
        x1 = _mixer1(x2, y01, rt, p0["ln3_g"], p0["ln3_b"], l1_w_in, l1_pool_w, _row(l1_pool_scale),
                     l1_w_out, _row(l1_ln1_g), _row(l1_ln1_b), nb, s_len)
        x2, y01, rt = _layer_tail(x1, kv1, mem_len, p1, router, b0, nb, s_len)
        out = _combine_ln(x2, y01, rt, p1["ln3_g"], p1["ln3_b"], into=out, row0=b0 * s_len,
                          full_rows=bsz * s_len)
    return out.reshape(bsz, s_len, d)
```

```python
import functools

import numpy as np
import jax
import jax.numpy as jnp
from jax import lax
from jax.experimental import pallas as pl
from jax.experimental.pallas import tpu as pltpu
from jax.experimental.pallas import tpu_sc as plsc

F32 = jnp.float32
BF16 = jnp.bfloat16
I32 = jnp.int32

DEPTH = 2
ALPHA = (2.0 * DEPTH) ** 0.25
LN_EPS = 1e-5

HEAD_DIM = 64
A_Q_HEADS = 8
A_KV_HEADS = 2
A_GROUP = A_Q_HEADS // A_KV_HEADS
BLOCK = 128
ROPE_THETA = 500000.0
ROPE_DIM = HEAD_DIM // 4
A_WIDTH = A_Q_HEADS * HEAD_DIM
KV_WIDTH = A_KV_HEADS * HEAD_DIM
B_GROUPS = 8
B_GROUP_DIM = 64
B_WIDTH = B_GROUPS * B_GROUP_DIM
POOL_WINDOWS = (2, 4, 8, 16)
POOL_HALO = 16
X_HEADS = 4
N_EXPERTS = 16
N_EXPERT_GROUPS = 4
EXPERTS_PER_GROUP = 4

TOKEN_TILE = 1024
MIXER0_TILE = 1024
SLAB_ROWS = 512
EXPERT_ROWS = 512
BATCH_SPLIT = 2
SC_WORKERS = 32
SC_CHUNK = 64
VMEM_LIMIT = 56 * 1024 * 1024
NEG_BIG = -1e30


def _layer_norm(z, g, b):
    mu = jnp.mean(z, axis=-1, keepdims=True)
    d = z - mu
    var = jnp.mean(d * d, axis=-1, keepdims=True)
    return d * lax.rsqrt(var + LN_EPS) * g + b


def _dot(a, b):
    return jnp.dot(a, b, preferred_element_type=F32)


def _dot_nt(a, b):
    return lax.dot_general(a, b, (((1,), (1,)), ((), ())), preferred_element_type=F32)


def _split_bf16(v):
    hi = v.astype(BF16)
    lo = (v - hi.astype(F32)).astype(BF16)
    return hi, lo


def _tc_params(n_axes):
    return pltpu.CompilerParams(dimension_semantics=("arbitrary",) * n_axes,
                                vmem_limit_bytes=VMEM_LIMIT)


def _const_spec(shape):
    nd = len(shape)
    return pl.BlockSpec(shape, lambda *_: (0,) * nd, pipeline_mode=pl.Buffered(1))

def _mixer0_kernel(sinks_ref, x_ref, pos_ref, win_ref, invf_ref, etab_ref, cbase_ref, gsum_ref,
                   lng_ref, lnb_ref, ws_ref, bs_ref, wout_ref, g1_ref, b1_ref,
                   o_ref, q_s, kv_s, u_s, vn_s, mix_s, wt_s, win_s, wout_s):
    b = pl.program_id(0)
    j = pl.program_id(1)
    tq = x_ref.shape[0]
    nblk = tq // BLOCK
    kvw = kv_s.shape[1]

    @pl.when(jnp.logical_and(b == 0, j == 0))
    def _():
        win_s[...] = win_ref[...].astype(BF16)
        wout_s[...] = wout_ref[...].astype(BF16)
        r = lax.broadcasted_iota(I32, (BLOCK, BLOCK), 0)
        c = lax.broadcasted_iota(I32, (BLOCK, BLOCK), 1)
        for g in range(B_GROUPS):
            wt_s[g] = jnp.where(c <= r, ws_ref[g], 0.0).astype(BF16)

    @pl.when(j == 0)
    def _():
        kv_s[0:BLOCK, :] = jnp.zeros((BLOCK, kvw), BF16)

    c1 = A_WIDTH
    c2 = c1 + KV_WIDTH
    c3 = c2 + KV_WIDTH
    c4 = c3 + B_WIDTH

    def rotary_tables(rows):
        n = rows.stop - rows.start
        ang = invf_ref[...] * pos_ref[:, rows].astype(F32)
        c8 = jnp.cos(ang)
        s8 = jnp.sin(ang)
        c8h = c8.astype(BF16).astype(F32)
        s8h = s8.astype(BF16).astype(F32)
        stack = jnp.concatenate([c8h, c8 - c8h, s8h, s8 - s8h, jnp.zeros((128 - 4 * 8, n), F32)], axis=0)
        tabs = _dot(stack.T.astype(BF16), etab_ref[...])
        return tabs[:, 0:128] + cbase_ref[...], tabs[:, 128:256], tabs[:, 256:384]

    def prepare(rows, h, tables):
        n = rows.stop - rows.start
        cs, sa, sb = tables

        def rope(t):
            return t * cs + pltpu.roll(t, ROPE_DIM // 2, 1) * sa + pltpu.roll(t, 128 - ROPE_DIM // 2, 1) * sb

        for c in range(A_WIDTH // 128):
            t = h[:, c * 128:(c + 1) * 128] * (HEAD_DIM ** -0.5)
            q_s[rows, c * 128:(c + 1) * 128] = rope(t).astype(BF16)
        low = lax.broadcasted_iota(I32, (n, 128), 1) < HEAD_DIM
        kr = rope(h[:, c1:c2])
        kx = pltpu.roll(kr, HEAD_DIM, 1)
        vr = h[:, c2:c3]
        vx = pltpu.roll(vr, HEAD_DIM, 1)
        kv_cols = [jnp.where(low, kr, kx), jnp.where(low, kx, kr),
                   jnp.where(low, vr, 0.0), jnp.where(low, 0.0, vx),
                   jnp.where(low, vx, 0.0), jnp.where(low, 0.0, vr)]
        for c, col in enumerate(kv_cols):
            kv_s[BLOCK + rows.start:BLOCK + rows.stop, c * 128:(c + 1) * 128] = col.astype(BF16)

        u_s[rows, :] = jax.nn.gelu(h[:, c3:c4])
        v = jax.nn.gelu(h[:, c4:])
        gsum = gsum_ref[...]
        mean = _dot(v.astype(BF16), gsum) * (1.0 / B_GROUP_DIM)
        d = v - mean
        var = _dot((d * d).astype(BF16), gsum) * (1.0 / B_GROUP_DIM)
        vn_s[rows, :] = (d * lax.rsqrt(var + LN_EPS) * lng_ref[...] + lnb_ref[...]).astype(BF16)

    qi = lax.broadcasted_iota(I32, (BLOCK, 2 * BLOCK), 0)
    kj = lax.broadcasted_iota(I32, (BLOCK, 2 * BLOCK), 1)
    rel = qi + BLOCK - kj
    band = jnp.logical_and(rel >= 0, rel < BLOCK)
    low_q = lax.broadcasted_iota(I32, (BLOCK, 128), 1) < HEAD_DIM
    low_k = lax.broadcasted_iota(I32, (2 * BLOCK, 128), 1) < HEAD_DIM
    ones_lo = jnp.where(low_k, 1.0, 0.0).astype(BF16)
    ones_hi = jnp.where(low_k, 0.0, 1.0).astype(BF16)
    zero_q = jnp.zeros((BLOCK, 128), BF16)

    def block_body(n):
        r0 = n * BLOCK
        kv = kv_s[pl.ds(r0, 2 * BLOCK), :]
        qb = q_s[pl.ds(r0, BLOCK), :]
        valid = jnp.logical_and(band, kj >= jnp.where(j == 0, BLOCK, 0)) if n == 0 else band
        cols_per_kv = A_GROUP // 2
        scores = {}
        for hk in range(A_KV_HEADS):
            cols = range(hk * cols_per_kv, (hk + 1) * cols_per_kv)
            pieces = []
            for c in cols:
                qp = qb[:, c * 128:(c + 1) * 128]
                pieces += [jnp.where(low_q, qp, zero_q), jnp.where(low_q, zero_q, qp)]
            sc = _dot_nt(jnp.concatenate(pieces, axis=0), kv[:, hk * 128:(hk + 1) * 128])
            for i, c in enumerate(cols):
                for half in range(2):
                    r = (2 * i + half) * BLOCK
                    scores[c, half] = sc[r:r + BLOCK, :]
        vnb = vn_s[pl.ds(r0, BLOCK), :]
        parts = []
        for c in range(B_WIDTH // 128):
            vp = vnb[:, c * 128:(c + 1) * 128]
            parts.append(_dot(wt_s[2 * c], jnp.where(low_q, vp, zero_q))
                         + _dot(wt_s[2 * c + 1], jnp.where(low_q, zero_q, vp)))
        probs, esink = {}, {}
        for (c, half), sc in scores.items():
            s = jnp.where(valid, sc, NEG_BIG)
            sink = sinks_ref[2 * c + half]
            m = jnp.maximum(jnp.max(s, axis=-1, keepdims=True), sink)
            probs[c, half] = jnp.exp(s - m).astype(BF16)
            esink[c, half] = jnp.exp(sink - m)
        res = {}
        for hk in range(A_KV_HEADS):
            cols = range(hk * cols_per_kv, (hk + 1) * cols_per_kv)
            for half in range(2):
                vcol = kv[:, (2 + 2 * hk + half) * 128:(3 + 2 * hk + half) * 128]
                vm = jnp.concatenate([vcol, ones_lo if half == 0 else ones_hi], axis=1)
                pv = _dot(jnp.concatenate([probs[c, half] for c in cols], axis=0), vm)
                for i, c in enumerate(cols):
                    part = pv[i * BLOCK:(i + 1) * BLOCK, :]
                    res[c] = part if half == 0 else res[c] + part
        for c in range(A_WIDTH // 128):
            den = res[c][:, 128:] + jnp.where(low_q, esink[c, 0], esink[c, 1])
            mix_s[pl.ds(r0, BLOCK), c * 128:(c + 1) * 128] = (res[c][:, :128] / den).astype(BF16)
        mixed = jnp.concatenate(parts, axis=1) + bs_ref[...]
        mix_s[pl.ds(r0, BLOCK), A_WIDTH:] = (u_s[pl.ds(r0, BLOCK), :] * mixed).astype(BF16)

    slab = min(SLAB_ROWS, tq)
    slabs = [slice(r, r + slab) for r in range(0, tq, slab)]
    tables = [rotary_tables(rows) for rows in slabs]
    xs = [x_ref[rows, :] for rows in slabs]
    hs = [_dot(x.astype(BF16), win_s[...]) for x in xs]
    zs = []
    for rows, x, h, tab in zip(slabs, xs, hs, tables):
        prepare(rows, h, tab)
        for n in range(rows.start // BLOCK, rows.stop // BLOCK):
            block_body(n)
        zs.append(ALPHA * x + _dot(mix_s[rows, :], wout_s[...]))
    kv_s[0:BLOCK, :] = kv_s[tq:tq + BLOCK, :]
    for rows, z in zip(slabs, zs):
        o_ref[rows, :] = _layer_norm(z, g1_ref[...], b1_ref[...])


def _mixer0(x, pos_row, sinks, w_in, invf, etab, cbase, gsum, lng, lnb, w_s, bs_full, w_out, g1, b1,
            b0, bsz, s_len):
    d = x.shape[1]
    t_all = bsz * s_len
    tq = MIXER0_TILE
    nj = s_len // tq
    row = lambda bb, jj: (bb * nj + jj, 0)
    in_w = w_in.shape[1]
    return pl.pallas_call(
        _mixer0_kernel,
        grid=(bsz, nj),
        in_specs=[
            pl.BlockSpec(memory_space=pltpu.SMEM),
            pl.BlockSpec((tq, d), lambda bb, jj: ((b0 + bb) * nj + jj, 0)),
            pl.BlockSpec((1, tq), lambda bb, jj: (0, (b0 + bb) * nj + jj)),
            _const_spec((d, in_w)),
            _const_spec((ROPE_DIM // 2, 1)), _const_spec((128, 3 * 128)), _const_spec((1, 128)),
            _const_spec((B_WIDTH, B_WIDTH)),
            _const_spec((1, B_WIDTH)), _const_spec((1, B_WIDTH)),
            _const_spec((B_GROUPS, BLOCK, BLOCK)),
            _const_spec((BLOCK, B_WIDTH)),
            _const_spec((A_WIDTH + B_WIDTH, d)),
            _const_spec((1, d)), _const_spec((1, d)),
        ],
        out_specs=pl.BlockSpec((tq, d), row),
        out_shape=jax.ShapeDtypeStruct((t_all, d), F32),
        scratch_shapes=[
            pltpu.VMEM((tq, A_WIDTH), BF16),
            pltpu.VMEM((tq + BLOCK, 6 * 128), BF16),
            pltpu.VMEM((tq, B_WIDTH), F32),
            pltpu.VMEM((tq, B_WIDTH), BF16),
            pltpu.VMEM((tq, A_WIDTH + B_WIDTH), BF16),
            pltpu.VMEM((B_GROUPS, BLOCK, BLOCK), BF16),
            pltpu.VMEM((d, in_w), BF16),
            pltpu.VMEM((A_WIDTH + B_WIDTH, d), BF16),
        ],
        compiler_params=_tc_params(2),
        name="mixer0",
    )(sinks, x, pos_row, w_in, invf, etab, cbase, gsum, lng, lnb, w_s, bs_full, w_out, g1, b1)


def _expert_combine_ln(x2, y0_packed, y1_packed, rt, g, b):
    wt = pltpu.bitcast(rt, F32).T
    y = wt[:, 4:5] * _unpack_bf16_pairs(y0_packed) + wt[:, 5:6] * _unpack_bf16_pairs(y1_packed)
    return _layer_norm(ALPHA * x2 + y, g, b)


def _mixer1_kernel(x2_ref, y0_ref, y1_ref, rt_ref, g3_ref, b3_ref,
                   win_ref, pw_ref, ps_ref, wout_ref, g1_ref, b1_ref, o_ref,
                   h_s, mp_s, win_s, pw_s, wout_s):
    j = pl.program_id(1)
    tq = x2_ref.shape[0]
    gw = x2_ref.shape[1] // len(POOL_WINDOWS)
    slabs = [slice(c * SLAB_ROWS, (c + 1) * SLAB_ROWS) for c in range(tq // SLAB_ROWS)]

    @pl.when(jnp.logical_and(pl.program_id(0) == 0, j == 0))
    def _():
        win_s[...] = win_ref[...].astype(BF16)
        pw_s[...] = pw_ref[...].astype(BF16)
        wout_s[...] = wout_ref[...].astype(BF16)

    @pl.when(j == 0)
    def _():
        h_s[0:POOL_HALO, :] = jnp.zeros((POOL_HALO, h_s.shape[1]), F32)

    xs = []
    for rows in slabs:
        x = _expert_combine_ln(x2_ref[rows, :], y0_ref[rows, :], y1_ref[rows, :], rt_ref[:, rows],
                               g3_ref[...], b3_ref[...])
        h_s[POOL_HALO + rows.start:POOL_HALO + rows.stop, :] = _dot(x.astype(BF16), win_s[...])
        xs.append(x)

    for rows, x in zip(slabs, xs):
        n = rows.stop - rows.start
        t_pos = j * tq + rows.start + lax.broadcasted_iota(I32, (n, 1), 0)
        for g, win in enumerate(POOL_WINDOWS):
            lo, hi = g * gw, (g + 1) * gw
            ext = h_s[rows.start:rows.stop + POOL_HALO, lo:hi]
            acc = ext
            shift = 1
            while shift < win:
                acc = acc + pltpu.roll(acc, shift, 0)
                shift *= 2
            count = jnp.minimum(t_pos + 1, win).astype(F32)
            pooled = acc[POOL_HALO:, :] / count - ext[POOL_HALO:, :]
            mapped = _dot(pooled.astype(BF16), pw_s[g])
            mp_s[rows, lo:hi] = (mapped * ps_ref[:, lo:hi]).astype(BF16)
        z = ALPHA * x + _dot(mp_s[rows, :], wout_s[...])
        o_ref[rows, :] = _layer_norm(z, g1_ref[...], b1_ref[...])
    h_s[0:POOL_HALO, :] = h_s[tq:tq + POOL_HALO, :]


def _mixer1(x2, y01, rt, g3, b3, w_in, pool_w, pool_scale, w_out, g1, b1, bsz, s_len):
    t_all, d = x2.shape
    tq = TOKEN_TILE
    nj = s_len // tq
    nt = bsz * nj
    row = lambda bb, jj: (bb * nj + jj, 0)
    ng = len(POOL_WINDOWS)
    return pl.pallas_call(
        _mixer1_kernel,
        grid=(bsz, nj),
        in_specs=[
            pl.BlockSpec((tq, d), row),
            pl.BlockSpec((tq, d // 2), row),
            pl.BlockSpec((tq, d // 2), lambda bb, jj: (nt + bb * nj + jj, 0)),
            pl.BlockSpec((8, tq), lambda bb, jj: (0, bb * nj + jj)),
            _const_spec((1, d)), _const_spec((1, d)),
            _const_spec((d, d)),
            _const_spec((ng, d // ng, d // ng)),
            _const_spec((1, d)),
            _const_spec((d, d)),
            _const_spec((1, d)), _const_spec((1, d)),
        ],
        out_specs=pl.BlockSpec((tq, d), row),
        out_shape=jax.ShapeDtypeStruct((t_all, d), F32),
        scratch_shapes=[pltpu.VMEM((tq + POOL_HALO, d), F32), pltpu.VMEM((tq, d), BF16),
                        pltpu.VMEM((d, d), BF16), pltpu.VMEM((ng, d // ng, d // ng), BF16),
                        pltpu.VMEM((d, d), BF16)],
        compiler_params=_tc_params(2),
        name="mixer1",
    )(x2, y01, y01, rt, g3, b3, w_in, pool_w, pool_scale, w_out, g1, b1)


def _kv_kernel(mem_ref, w_ref, o_ref, w_s):
    @pl.when(pl.program_id(0) == 0)
    def _():
        w_s[...] = w_ref[...].astype(BF16)

    o_ref[...] = _dot(mem_ref[...].astype(BF16), w_s[...]).astype(BF16)


def _kv_proj(mem2d, wkv, bsz):
    rows, d = mem2d.shape
    m = rows // bsz
    return pl.pallas_call(
        _kv_kernel,
        grid=(bsz,),
        in_specs=[pl.BlockSpec((m, d), lambda i: (i, 0)), _const_spec(wkv.shape)],
        out_specs=pl.BlockSpec((m, wkv.shape[1]), lambda i: (i, 0)),
        out_shape=jax.ShapeDtypeStruct((rows, wkv.shape[1]), BF16),
        scratch_shapes=[pltpu.VMEM(wkv.shape, BF16)],
        compiler_params=_tc_params(1),
        name="kv_proj",
    )(mem2d, wkv)


def _top2_of4(v):
    hi01, lo01 = jnp.maximum(v[0], v[1]), jnp.minimum(v[0], v[1])
    hi23, lo23 = jnp.maximum(v[2], v[3]), jnp.minimum(v[2], v[3])
    return jnp.maximum(hi01, hi23) + jnp.maximum(jnp.minimum(hi01, hi23), jnp.maximum(lo01, lo23))


def _argmax_first(vals):
    best, idx = vals[0], jnp.zeros(vals[0].shape, I32)
    for i in range(1, len(vals)):
        better = vals[i] > best
        best = jnp.where(better, vals[i], best)
        idx = jnp.where(better, i, idx)
    return best, idx


def _xattn_kernel(x_ref, kv_ref, wq_ref, wo_ref, g2_ref, b2_ref, rwh_ref, rwl_ref, rb_ref,
                  x2_ref, xp_ref, rt_ref, cnt_ref, run_s, wq_s, wo_s):
    first = jnp.logical_and(pl.program_id(0) == 0, pl.program_id(1) == 0)
    tq, d = x_ref.shape

    @pl.when(first)
    def _():
        run_s[...] = jnp.zeros(run_s.shape, F32)
        wq_s[...] = wq_ref[...].astype(BF16)
        wo_s[...] = wo_ref[...].astype(BF16)

    slabs = [slice(c * SLAB_ROWS, (c + 1) * SLAB_ROWS) for c in range(tq // SLAB_ROWS)]
    run = run_s[...]
    zs = [_xattn_attend(*_xattn_query(rows, x_ref, wq_s), kv_ref, wo_s) for rows in slabs]
    logits = [_xattn_norm_logits(rows, z, g2_ref, b2_ref, rwh_ref, rwl_ref, x2_ref, xp_ref)
              for rows, z in zip(slabs, zs)]
    for rows, lg in zip(slabs, logits):
        run = _xattn_route_rows(rows, lg, run, rb_ref, rt_ref)
    run_s[...] = run
    cnt_ref[...] = jnp.broadcast_to(run, cnt_ref.shape).astype(I32)


def _xattn_query(rows, x_ref, wq_s):
    hd = x_ref.shape[1] // X_HEADS
    x = x_ref[rows, :]
    return x, _dot(x.astype(BF16), wq_s[...]) * (hd ** -0.5)


def _xattn_attend(x, q, kv_ref, wo_s):
    d = x.shape[1]
    hd = d // X_HEADS
    outs = []
    for hx in range(X_HEADS):
        qh = q[:, hx * hd:(hx + 1) * hd].astype(BF16)
        kh = kv_ref[:, hx * hd:(hx + 1) * hd]
        vh = kv_ref[:, d + hx * hd:d + (hx + 1) * hd]
        s = _dot_nt(qh, kh)
        p = jnp.exp(s - jnp.max(s, axis=-1, keepdims=True))
        o = _dot(p.astype(BF16), vh) / jnp.sum(p, axis=-1, keepdims=True)
        outs.append(o.astype(BF16))
    return ALPHA * x + _dot(jnp.concatenate(outs, axis=1), wo_s[...])


def _xattn_norm_logits(rows, z, g2_ref, b2_ref, rwh_ref, rwl_ref, x2_ref, xp_ref):
    x2 = _layer_norm(z, g2_ref[...], b2_ref[...])
    x2_ref[rows, :] = x2

    xp_ref[rows, :] = _pack_bf16_pairs(x2)

    xh, xl = _split_bf16(x2)
    both = _dot_nt(jnp.concatenate([rwh_ref[...], rwl_ref[...]], axis=0), xh)
    return both[0:N_EXPERTS] + both[N_EXPERTS:] + _dot_nt(rwh_ref[...], xl)


def _xattn_route_rows(rows, logits, run, rb_ref, rt_ref):
    tq = rows.stop - rows.start
    e_max = jnp.max(logits, axis=0, keepdims=True)
    ex = jnp.exp(logits - e_max)
    scores = ex / jnp.sum(ex, axis=0, keepdims=True)
    biased = scores + rb_ref[...]
    sc = [scores[e:e + 1, :] for e in range(N_EXPERTS)]
    bi = [biased[e:e + 1, :] for e in range(N_EXPERTS)]
    epg = EXPERTS_PER_GROUP
    gscore = [_top2_of4(bi[g * epg:(g + 1) * epg]) for g in range(N_EXPERT_GROUPS)]
    _, gsel = _argmax_first(gscore)

    def pick(vals):
        return [functools.reduce(lambda acc, g: jnp.where(gsel == g, vals[g * epg + i], acc),
                                 range(1, N_EXPERT_GROUPS), vals[i]) for i in range(epg)]

    in_b = pick(bi)
    in_s = pick(sc)
    _, i0 = _argmax_first(in_b)
    _, i1 = _argmax_first([jnp.where(i0 == i, -jnp.inf, in_b[i]) for i in range(epg)])

    def take(vals, idx):
        return functools.reduce(lambda acc, i: jnp.where(idx == i, vals[i], acc), range(1, epg), vals[0])

    s0, s1 = take(in_s, i0), take(in_s, i1)
    w0, w1 = s0 / (s0 + s1), s1 / (s0 + s1)
    e0, e1 = gsel * epg + i0, gsel * epg + i1

    eid = lax.broadcasted_iota(I32, (N_EXPERTS, tq), 0)
    oh0 = eid == e0
    oh1 = eid == e1
    onehot = jnp.where(jnp.logical_or(oh0, oh1), 1.0, 0.0)
    rr = lax.broadcasted_iota(I32, (tq, tq), 0)
    cc = lax.broadcasted_iota(I32, (tq, tq), 1)
    upper = jnp.where(rr < cc, 1.0, 0.0).astype(BF16)
    prefix = _dot(onehot.astype(BF16), upper) + run
    r0 = jnp.sum(jnp.where(oh0, prefix, 0.0), axis=0, keepdims=True).astype(I32)
    r1 = jnp.sum(jnp.where(oh1, prefix, 0.0), axis=0, keepdims=True).astype(I32)

    zero = jnp.zeros((1, tq), I32)
    rt_ref[:, rows] = jnp.concatenate(
        [e0, e1, r0, r1, pltpu.bitcast(w0, I32), pltpu.bitcast(w1, I32), zero, zero], axis=0)
    return run + jnp.sum(onehot, axis=1, keepdims=True)


def _xattn_route(x1, kv, mem_len, wq, wo, g2, b2, rw_hi, rw_lo, rbias, b0, bsz, s_len):
    t_all, d = x1.shape
    tq = TOKEN_TILE
    nj = s_len // tq
    m = mem_len
    row = lambda bb, jj: (bb * nj + jj, 0)
    return pl.pallas_call(
        _xattn_kernel,
        grid=(bsz, nj),
        in_specs=[
            pl.BlockSpec((tq, d), row),
            pl.BlockSpec((m, 2 * d), lambda bb, jj: (b0 + bb, 0)),
            _const_spec((d, d)), _const_spec((d, d)),
            _const_spec((1, d)), _const_spec((1, d)),
            _const_spec((N_EXPERTS, d)), _const_spec((N_EXPERTS, d)),
            _const_spec((N_EXPERTS, 1)),
        ],
        out_specs=[
            pl.BlockSpec((tq, d), row),
            pl.BlockSpec((tq, d // 2), row),
            pl.BlockSpec((8, tq), lambda bb, jj: (0, bb * nj + jj)),
            _const_spec((N_EXPERTS, 128)),
        ],
        out_shape=[
            jax.ShapeDtypeStruct((t_all, d), F32),
            jax.ShapeDtypeStruct((t_all, d // 2), I32),
            jax.ShapeDtypeStruct((8, t_all), I32),
            jax.ShapeDtypeStruct((N_EXPERTS, 128), I32),
        ],
        scratch_shapes=[pltpu.VMEM((N_EXPERTS, 1), F32), pltpu.VMEM((d, d), BF16), pltpu.VMEM((d, d), BF16)],
        compiler_params=_tc_params(2),
        name="xattn_route",
    )(x1, kv, wq, wo, g2, b2, rw_hi, rw_lo, rbias)


def _sc_mesh():
    return plsc.VectorSubcoreMesh(core_axis_name="c", subcore_axis_name="s")


def _sc_params():
    return pltpu.CompilerParams(needs_layout_passes=False)


def _worker_id():
    return lax.axis_index("s") * lax.axis_size("c") + lax.axis_index("c")


def _sc_dispatch(xp, dest2d, n_rows):
    t_all, width = xp.shape
    chunk = dest2d.shape[1]
    tok_w = t_all // SC_WORKERS
    nch = tok_w // chunk
    slot1 = t_all // chunk
    assert t_all % (SC_WORKERS * chunk * 2) == 0

    def body(x_hbm, dest_hbm, out_hbm, idx0_v, idx1_v, buf0, buf1, sem_r, sem_w):
        wid = _worker_id()
        base = wid * tok_w
        pltpu.sync_copy(dest_hbm.at[pl.ds(wid * nch, nch)], idx0_v)
        pltpu.sync_copy(dest_hbm.at[pl.ds(slot1 + wid * nch, nch)], idx1_v)

        def read(c, buf, k):
            return pltpu.make_async_copy(x_hbm.at[pl.ds(base + c * chunk, chunk)], buf, sem_r.at[k])

        def scatter(c, buf):
            a = pltpu.make_async_copy(buf, out_hbm.at[idx0_v.at[c]], sem_w.at[0])
            b = pltpu.make_async_copy(buf, out_hbm.at[idx1_v.at[c]], sem_w.at[1])
            a.start()
            b.start()
            a.wait()
            b.wait()

        read(0, buf0, 0).start()

        @pl.loop(0, nch // 2)
        def _(g):
            c = 2 * g
            read(c + 1, buf1, 1).start()
            read(c, buf0, 0).wait()
            scatter(c, buf0)

            @pl.when(c + 2 < nch)
            def _():
                read(c + 2, buf0, 0).start()

            read(c + 1, buf1, 1).wait()
            scatter(c + 1, buf1)

    return pl.kernel(
        body,
        out_type=jax.ShapeDtypeStruct((n_rows, width), xp.dtype),
        mesh=_sc_mesh(),
        scratch_types=[
            pltpu.VMEM((nch, chunk), I32),
            pltpu.VMEM((nch, chunk), I32),
            pltpu.VMEM((chunk, width), xp.dtype),
            pltpu.VMEM((chunk, width), xp.dtype),
            pltpu.SemaphoreType.DMA((2,)),
            pltpu.SemaphoreType.DMA((2,)),
        ],
        compiler_params=_sc_params(),
        name="sc_dispatch",
    )(xp, dest2d)


def _sc_combine(ys, dest2d):
    n_idx_rows, chunk = dest2d.shape
    width = ys.shape[1]
    nch = n_idx_rows // SC_WORKERS
    assert n_idx_rows % (SC_WORKERS * 2) == 0

    def body(y_hbm, dest_hbm, out_hbm, idx_v, buf0, buf1, sem_g):
        wid = _worker_id()
        base = wid * nch * chunk
        pltpu.sync_copy(dest_hbm.at[pl.ds(wid * nch, nch)], idx_v)

        def gather(c, buf, k):
            return pltpu.make_async_copy(y_hbm.at[idx_v.at[c]], buf, sem_g.at[k])

        def write(c, buf):
            pltpu.sync_copy(buf, out_hbm.at[pl.ds(base + c * chunk, chunk)])

        gather(0, buf0, 0).start()

        @pl.loop(0, nch // 2)
        def _(g):
            c = 2 * g
            gather(c + 1, buf1, 1).start()
            gather(c, buf0, 0).wait()
            write(c, buf0)

            @pl.when(c + 2 < nch)
            def _():
                gather(c + 2, buf0, 0).start()

            gather(c + 1, buf1, 1).wait()
            write(c + 1, buf1)

    return pl.kernel(
        body,
        out_type=jax.ShapeDtypeStruct((n_idx_rows * chunk, width), ys.dtype),
        mesh=_sc_mesh(),
        scratch_types=[
            pltpu.VMEM((nch, chunk), I32),
            pltpu.VMEM((chunk, width), ys.dtype),
            pltpu.VMEM((chunk, width), ys.dtype),
            pltpu.SemaphoreType.DMA((2,)),
        ],
        compiler_params=_sc_params(),
        name="sc_combine",
    )(ys, dest2d)


def _pack_bf16_pairs(v):
    half = v.shape[1] // 2
    lo = pltpu.bitcast(v[:, :half].astype(BF16).astype(F32), jnp.uint32) >> 16
    hi = pltpu.bitcast(v[:, half:].astype(BF16).astype(F32), jnp.uint32) & jnp.uint32(0xFFFF0000)
    return pltpu.bitcast(hi | lo, I32)


def _unpack_bf16_pairs(w):
    w = pltpu.bitcast(w, jnp.uint32)
    lo = pltpu.bitcast(w << 16, F32)
    hi = pltpu.bitcast(w & jnp.uint32(0xFFFF0000), F32)
    return jnp.concatenate([lo, hi], axis=1)


def _ffn_kernel(be_ref, nv_ref, nu_ref, nxt_ref, xs_ref, wg_hbm, wu_hbm, wd_hbm, o_ref,
                wg_f, wu_f, wd_f, wg_s, wu_s, wd_s, sem):
    i = pl.program_id(0)

    def fetch(e):
        return (pltpu.make_async_copy(wg_hbm.at[e], wg_f, sem.at[0]),
                pltpu.make_async_copy(wu_hbm.at[e], wu_f, sem.at[1]),
                pltpu.make_async_copy(wd_hbm.at[e], wd_f, sem.at[2]))

    @pl.when(nxt_ref[i] >= 0)
    def _():
        @pl.when(i == 0)
        def _():
            for cp in fetch(be_ref[0]):
                cp.start()

        for cp in fetch(be_ref[i]):
            cp.wait()
        wg_s[...] = wg_f[...].astype(BF16)
        wu_s[...] = wu_f[...].astype(BF16)
        wd_s[...] = wd_f[...].astype(BF16)

        @pl.when(nxt_ref[i] < N_EXPERTS)
        def _():
            for cp in fetch(nxt_ref[i]):
                cp.start()

    @pl.when(i < nu_ref[0])
    def _():
        live = lax.broadcasted_iota(I32, xs_ref.shape, 0) < nv_ref[i]
        xb = _unpack_bf16_pairs(jnp.where(live, xs_ref[...], 0)).astype(BF16)
        act = jax.nn.silu(_dot(xb, wg_s[...])) * _dot(xb, wu_s[...])
        o_ref[...] = _pack_bf16_pairs(_dot(act.astype(BF16), wd_s[...]))


def _expert_ffn(xs, blk_expert, blk_valid, n_used, blk_next, w_gate, w_up, w_down):
    n_rows, half = xs.shape
    d = 2 * half
    de = w_gate.shape[2]
    bm = EXPERT_ROWS
    rows = lambda i, be, nv, nu, nx: (jnp.minimum(i, nu[0] - 1), 0)
    hbm = pl.BlockSpec(memory_space=pl.ANY)
    return pl.pallas_call(
        _ffn_kernel,
        grid_spec=pltpu.PrefetchScalarGridSpec(
            num_scalar_prefetch=4,
            grid=(n_rows // bm,),
            in_specs=[pl.BlockSpec((bm, half), rows), hbm, hbm, hbm],
            out_specs=pl.BlockSpec((bm, half), rows),
            scratch_shapes=[
                pltpu.VMEM((d, de), F32), pltpu.VMEM((d, de), F32), pltpu.VMEM((de, d), F32),
                pltpu.VMEM((d, de), BF16), pltpu.VMEM((d, de), BF16), pltpu.VMEM((de, d), BF16),
                pltpu.SemaphoreType.DMA((3,)),
            ],
        ),
        out_shape=jax.ShapeDtypeStruct((n_rows, half), I32),
        compiler_params=_tc_params(1),
        name="expert_ffn",
    )(blk_expert, blk_valid, n_used, blk_next, xs, w_gate, w_up, w_down)


def _combine_kernel(x_ref, y0_ref, y1_ref, rt_ref, g_ref, b_ref, o_ref):
    o_ref[...] = _expert_combine_ln(x_ref[...], y0_ref[...], y1_ref[...], rt_ref[...], g_ref[...], b_ref[...])


def _combine_kernel_into(x_ref, y0_ref, y1_ref, rt_ref, g_ref, b_ref, full_ref, o_ref):
    del full_ref
    _combine_kernel(x_ref, y0_ref, y1_ref, rt_ref, g_ref, b_ref, o_ref)


def _combine_ln(x2, y01, rt, g3, b3, into=None, row0=0, full_rows=None):
    t_all, d = x2.shape
    tq = TOKEN_TILE
    nt = t_all // tq
    blk0 = row0 // tq if full_rows else 0
    in_specs = [
        pl.BlockSpec((tq, d), lambda i: (i, 0)),
        pl.BlockSpec((tq, d // 2), lambda i: (i, 0)),
        pl.BlockSpec((tq, d // 2), lambda i: (i + nt, 0)),
        pl.BlockSpec((8, tq), lambda i: (0, i)),
        _const_spec((1, d)), _const_spec((1, d)),
    ]
    args = [x2, y01, y01, rt, g3, b3]
    body, aliases = _combine_kernel, {}
    if into is not None:
        in_specs.append(pl.BlockSpec(memory_space=pl.ANY))
        args.append(into)
        body, aliases = _combine_kernel_into, {len(args) - 1: 0}
    return pl.pallas_call(
        body,
        grid=(nt,),
        in_specs=in_specs,
        out_specs=pl.BlockSpec((tq, d), lambda i: (i + blk0, 0)),
        out_shape=jax.ShapeDtypeStruct((full_rows or t_all, d), F32),
        input_output_aliases=aliases,
        compiler_params=_tc_params(1),
        name="combine_ln",
    )(*args)


def _routing_tables(rt, counts):
    bm = EXPERT_ROWS
    t_all = rt.shape[1]
    n_rows = 2 * t_all + N_EXPERTS * bm
    cnt = counts[:, 0]
    padded = (cnt + bm - 1) // bm * bm
    ends = jnp.cumsum(padded)
    offs = ends - padded
    experts = rt[0:2]
    off_tok = jnp.sum(jnp.where(experts[None] == jnp.arange(N_EXPERTS, dtype=I32)[:, None, None],
                                offs[:, None, None], 0), axis=0)
    dest2d = (off_tok + rt[2:4]).reshape(-1, SC_CHUNK).astype(I32)
    blk_start = jnp.arange(n_rows // bm, dtype=I32) * bm
    blk_expert = jnp.minimum(jnp.sum(blk_start[:, None] >= ends[None, :], axis=1), N_EXPERTS - 1).astype(I32)
    live_end = jnp.sum(jnp.where(blk_expert[:, None] == jnp.arange(N_EXPERTS, dtype=I32)[None, :],
                                 (offs + cnt)[None, :], 0), axis=1)
    blk_valid = jnp.clip(live_end - blk_start, 0, bm).astype(I32)
    n_used = (ends[-1:] // bm).astype(I32)
    eid = jnp.arange(N_EXPERTS, dtype=I32)
    later_present = jnp.logical_and(eid[None, :] > eid[:, None], (cnt > 0)[None, :])
    next_present = jnp.min(jnp.where(later_present, eid[None, :], N_EXPERTS), axis=1)
    next_of_blk = jnp.sum(jnp.where(blk_expert[:, None] == eid[None, :], next_present[None, :], 0), axis=1)
    prev_expert = jnp.concatenate([jnp.full((1,), -1, I32), blk_expert[:-1]])
    is_first = jnp.logical_and(blk_start < ends[-1], blk_expert != prev_expert)
    blk_next = jnp.where(is_first, next_of_blk, -1).astype(I32)
    return dest2d, blk_expert, blk_valid, n_used, blk_next, n_rows


def _layer_tail(x1, kv, mem_len, p, router, b0, bsz, s_len):
    rw_hi, rw_lo, rbias = router
    x2, xp, rt, counts = _xattn_route(x1, kv, mem_len, p["xq"], p["xo"], p["ln2_g"], p["ln2_b"],
                                      rw_hi, rw_lo, rbias, b0, bsz, s_len)
    dest2d, blk_expert, blk_valid, n_used, blk_next, n_rows = _routing_tables(rt, counts)
    xs = _sc_dispatch(xp, dest2d, n_rows)
    ys = _expert_ffn(xs, blk_expert, blk_valid, n_used, blk_next, p["e_gate"], p["e_up"], p["e_down"])
    y01 = _sc_combine(ys, dest2d)
    return x2, y01, rt


def _row(v):
    return v.reshape(1, -1).astype(F32)


def _common_params(xq, xkv, xo, ln2_g, ln2_b, e_gate, e_up, e_down, ln3_g, ln3_b):
    return dict(xq=xq, xkv=xkv, xo=xo, ln2_g=_row(ln2_g), ln2_b=_row(ln2_b),
                e_gate=e_gate, e_up=e_up, e_down=e_down, ln3_g=_row(ln3_g), ln3_b=_row(ln3_b))


def kernel(x, mem, positions, router_w, router_bias, l0_w_in, l0_sinks, l0_sgu_ln_g, l0_sgu_ln_b, l0_sgu_w, l0_sgu_b, l0_w_out, l0_ln1_g, l0_ln1_b, l0_xq, l0_xkv, l0_xo, l0_ln2_g, l0_ln2_b, l0_e_gate, l0_e_up, l0_e_down, l0_ln3_g, l0_ln3_b, l1_w_in, l1_pool_w, l1_pool_scale, l1_w_out, l1_ln1_g, l1_ln1_b, l1_xq, l1_xkv, l1_xo, l1_ln2_g, l1_ln2_b, l1_e_gate, l1_e_up, l1_e_down, l1_ln3_g, l1_ln3_b):
    bsz, s_len, d = x.shape
    assert s_len % TOKEN_TILE == 0 and s_len % MIXER0_TILE == 0 and MIXER0_TILE % BLOCK == 0
    xt = x.reshape(bsz * s_len, d)
    mem2d = mem.reshape(-1, d)

    rw_t = router_w.T.astype(F32)
    rw_hi = rw_t.astype(BF16)
    rw_lo = (rw_t - rw_hi.astype(F32)).astype(BF16)
    router = (rw_hi, rw_lo, router_bias.reshape(-1, 1).astype(F32))

    half = ROPE_DIM // 2
    inv_freq = (ROPE_THETA ** (-(jnp.arange(half, dtype=F32) * 2.0 / ROPE_DIM))).reshape(half, 1)
    etab_np = np.zeros((128, 3 * 128), np.float32)
    cbase_np = np.ones((1, 128), np.float32)
    for ln in range(128):
        dd = ln % HEAD_DIM
        if dd < ROPE_DIM:
            cbase_np[0, ln] = 0.0
            etab_np[[dd % half, half + dd % half], ln] = 1.0
            if dd >= half:
                etab_np[[2 * half + dd - half, 3 * half + dd - half], 128 + ln] = 1.0
            else:
                etab_np[[2 * half + dd, 3 * half + dd], 256 + ln] = -1.0
    etab = jnp.asarray(etab_np, BF16)
    cbase = jnp.asarray(cbase_np)
    pos_row = positions.reshape(1, -1).astype(I32)
    grp = jnp.arange(B_WIDTH) // B_GROUP_DIM
    gsum = (grp[:, None] == grp[None, :]).astype(BF16)
    bs_full = jnp.repeat(l0_sgu_b.T.astype(F32), B_GROUP_DIM, axis=1)

    p0 = _common_params(l0_xq, l0_xkv, l0_xo, l0_ln2_g, l0_ln2_b, l0_e_gate, l0_e_up, l0_e_down,
                        l0_ln3_g, l0_ln3_b)
    p1 = _common_params(l1_xq, l1_xkv, l1_xo, l1_ln2_g, l1_ln2_b, l1_e_gate, l1_e_up, l1_e_down,
                        l1_ln3_g, l1_ln3_b)
    kv0 = _kv_proj(mem2d, p0["xkv"], bsz)
    kv1 = _kv_proj(mem2d, p1["xkv"], bsz)
    mem_len = mem.shape[1]

    n_split = BATCH_SPLIT if bsz % BATCH_SPLIT == 0 else 1
    nb = bsz // n_split
    out = None
    for part in range(n_split):
        b0 = part * nb
        x1 = _mixer0(xt, pos_row, l0_sinks.astype(F32), l0_w_in, inv_freq, etab, cbase, gsum,
                     _row(l0_sgu_ln_g), _row(l0_sgu_ln_b), l0_sgu_w.astype(F32), bs_full,
                     l0_w_out, _row(l0_ln1_g), _row(l0_ln1_b), b0, nb, s_len)
        x2, y01, rt = _layer_tail(x1, kv0, mem_len, p0, router, b0, nb, s_len)
        x1 = _mixer1(x2, y01, rt, p0["ln3_g"], p0["ln3_b"], l1_w_in, l1_pool_w, _row(l1_pool_scale),
                     l1_w_out, _row(l1_ln1_g), _row(l1_ln1_b), nb, s_len)
        x2, y01, rt = _layer_tail(x1, kv1, mem_len, p1, router, b0, nb, s_len)
        out = _combine_ln(x2, y01, rt, p1["ln3_g"], p1["ln3_b"], into=out, row0=b0 * s_len,
                          full_rows=bsz * s_len)
    return out.reshape(bsz, s_len, d)
```

```python
import functools

import numpy as np
import jax
import jax.numpy as jnp
from jax import lax
from jax.experimental import pallas as pl
from jax.experimental.pallas import tpu as pltpu
from jax.experimental.pallas import tpu_sc as plsc

F32 = jnp.float32
BF16 = jnp.bfloat16
I32 = jnp.int32

DEPTH = 2
ALPHA = (2.0 * DEPTH) ** 0.25
LN_EPS = 1e-5

HEAD_DIM = 64
A_Q_HEADS = 8
A_KV_HEADS = 2
A_GROUP = A_Q_HEADS // A_KV_HEADS
BLOCK = 128
ROPE_THETA = 500000.0
ROPE_DIM = HEAD_DIM // 4
A_WIDTH = A_Q_HEADS * HEAD_DIM
KV_WIDTH = A_KV_HEADS * HEAD_DIM
B_GROUPS = 8
B_GROUP_DIM = 64
B_WIDTH = B_GROUPS * B_GROUP_DIM
POOL_WINDOWS = (2, 4, 8, 16)
POOL_HALO = 16
X_HEADS = 4
N_EXPERTS = 16
N_EXPERT_GROUPS = 4
EXPERTS_PER_GROUP = 4

TOKEN_TILE = 1024
MIXER0_TILE = 1024
SLAB_ROWS = 512
EXPERT_ROWS = 512
BATCH_SPLIT = 2
SC_WORKERS = 32
SC_CHUNK = 64
VMEM_LIMIT = 56 * 1024 * 1024
NEG_BIG = -1e30


def _layer_norm(z, g, b):
    mu = jnp.mean(z, axis=-1, keepdims=True)
    d = z - mu
    var = jnp.mean(d * d, axis=-1, keepdims=True)
    return d * lax.rsqrt(var + LN_EPS) * g + b


def _dot(a, b):
    return jnp.dot(a, b, preferred_element_type=F32)


def _dot_nt(a, b):
    return lax.dot_general(a, b, (((1,), (1,)), ((), ())), preferred_element_type=F32)


def _split_bf16(v):
    hi = v.astype(BF16)
    lo = (v - hi.astype(F32)).astype(BF16)
    return hi, lo


def _tc_params(n_axes):
    return pltpu.CompilerParams(dimension_semantics=("arbitrary",) * n_axes,
                                vmem_limit_bytes=VMEM_LIMIT)


def _const_spec(shape):
    nd = len(shape)
    return pl.BlockSpec(shape, lambda *_: (0,) * nd, pipeline_mode=pl.Buffered(1))

def _mixer0_kernel(sinks_ref, x_ref, pos_ref, win_ref, invf_ref, etab_ref, cbase_ref, gsum_ref,
                   lng_ref, lnb_ref, ws_ref, bs_ref, wout_ref, g1_ref, b1_ref,
                   o_ref, q_s, kv_s, u_s, vn_s, mix_s, wt_s, win_s, wout_s):
    b = pl.program_id(0)
    j = pl.program_id(1)
    tq = x_ref.shape[0]
    nblk = tq // BLOCK
    kvw = kv_s.shape[1]

    @pl.when(jnp.logical_and(b == 0, j == 0))
    def _():
        win_s[...] = win_ref[...].astype(BF16)
        wout_s[...] = wout_ref[...].astype(BF16)
        r = lax.broadcasted_iota(I32, (BLOCK, BLOCK), 0)
        c = lax.broadcasted_iota(I32, (BLOCK, BLOCK), 1)
        for g in range(B_GROUPS):
            wt_s[g] = jnp.where(c <= r, ws_ref[g], 0.0).astype(BF16)

    @pl.when(j == 0)
    def _():
        kv_s[0:BLOCK, :] = jnp.zeros((BLOCK, kvw), BF16)

    c1 = A_WIDTH
    c2 = c1 + KV_WIDTH
    c3 = c2 + KV_WIDTH
    c4 = c3 + B_WIDTH

    def rotary_tables(rows):
        n = rows.stop - rows.start
        ang = invf_ref[...] * pos_ref[:, rows].astype(F32)
        c8 = jnp.cos(ang)
        s8 = jnp.sin(ang)
        c8h = c8.astype(BF16).astype(F32)
        s8h = s8.astype(BF16).astype(F32)
        stack = jnp.concatenate([c8h, c8 - c8h, s8h, s8 - s8h, jnp.zeros((128 - 4 * 8, n), F32)], axis=0)
        tabs = _dot(stack.T.astype(BF16), etab_ref[...])
        return tabs[:, 0:128] + cbase_ref[...], tabs[:, 128:256], tabs[:, 256:384]

    def prepare(rows, h, tables):
        n = rows.stop - rows.start
        cs, sa, sb = tables

        def rope(t):
            return t * cs + pltpu.roll(t, ROPE_DIM // 2, 1) * sa + pltpu.roll(t, 128 - ROPE_DIM // 2, 1) * sb

        for c in range(A_WIDTH // 128):
            t = h[:, c * 128:(c + 1) * 128] * (HEAD_DIM ** -0.5)
            q_s[rows, c * 128:(c + 1) * 128] = rope(t).astype(BF16)
        low = lax.broadcasted_iota(I32, (n, 128), 1) < HEAD_DIM
        kr = rope(h[:, c1:c2])
        kx = pltpu.roll(kr, HEAD_DIM, 1)
        vr = h[:, c2:c3]
        vx = pltpu.roll(vr, HEAD_DIM, 1)
        kv_cols = [jnp.where(low, kr, kx), jnp.where(low, kx, kr),
                   jnp.where(low, vr, 0.0), jnp.where(low, 0.0, vx),
                   jnp.where(low, vx, 0.0), jnp.where(low, 0.0, vr)]
        for c, col in enumerate(kv_cols):
            kv_s[BLOCK + rows.start:BLOCK + rows.stop, c * 128:(c + 1) * 128] = col.astype(BF16)

        u_s[rows, :] = jax.nn.gelu(h[:, c3:c4])
        v = jax.nn.gelu(h[:, c4:])
        gsum = gsum_ref[...]
        mean = _dot(v.astype(BF16), gsum) * (1.0 / B_GROUP_DIM)
        d = v - mean
        var = _dot((d * d).astype(BF16), gsum) * (1.0 / B_GROUP_DIM)
        vn_s[rows, :] = (d * lax.rsqrt(var + LN_EPS) * lng_ref[...] + lnb_ref[...]).astype(BF16)

    qi = lax.broadcasted_iota(I32, (BLOCK, 2 * BLOCK), 0)
    kj = lax.broadcasted_iota(I32, (BLOCK, 2 * BLOCK), 1)
    rel = qi + BLOCK - kj
    band = jnp.logical_and(rel >= 0, rel < BLOCK)
    low_q = lax.broadcasted_iota(I32, (BLOCK, 128), 1) < HEAD_DIM
    low_k = lax.broadcasted_iota(I32, (2 * BLOCK, 128), 1) < HEAD_DIM
    ones_lo = jnp.where(low_k, 1.0, 0.0).astype(BF16)
    ones_hi = jnp.where(low_k, 0.0, 1.0).astype(BF16)
    zero_q = jnp.zeros((BLOCK, 128), BF16)

    def block_body(n):
        r0 = n * BLOCK
        kv = kv_s[pl.ds(r0, 2 * BLOCK), :]
        qb = q_s[pl.ds(r0, BLOCK), :]
        valid = jnp.logical_and(band, kj >= jnp.where(j == 0, BLOCK, 0)) if n == 0 else band
        cols_per_kv = A_GROUP // 2
        scores = {}
        for hk in range(A_KV_HEADS):
            cols = range(hk * cols_per_kv, (hk + 1) * cols_per_kv)
            pieces = []
            for c in cols:
                qp = qb[:, c * 128:(c + 1) * 128]
                pieces += [jnp.where(low_q, qp, zero_q), jnp.where(low_q, zero_q, qp)]
            sc = _dot_nt(jnp.concatenate(pieces, axis=0), kv[:, hk * 128:(hk + 1) * 128])
            for i, c in enumerate(cols):
                for half in range(2):
                    r = (2 * i + half) * BLOCK
                    scores[c, half] = sc[r:r + BLOCK, :]
        vnb = vn_s[pl.ds(r0, BLOCK), :]
        parts = []
        for c in range(B_WIDTH // 128):
            vp = vnb[:, c * 128:(c + 1) * 128]
            parts.append(_dot(wt_s[2 * c], jnp.where(low_q, vp, zero_q))
                         + _dot(wt_s[2 * c + 1], jnp.where(low_q, zero_q, vp)))
        probs, esink = {}, {}
        for (c, half), sc in scores.items():
            s = jnp.where(valid, sc, NEG_BIG)
            sink = sinks_ref[2 * c + half]
            m = jnp.maximum(jnp.max(s, axis=-1, keepdims=True), sink)
            probs[c, half] = jnp.exp(s - m).astype(BF16)
            esink[c, half] = jnp.exp(sink - m)
        res = {}
        for hk in range(A_KV_HEADS):
            cols = range(hk * cols_per_kv, (hk + 1) * cols_per_kv)
            for half in range(2):
                vcol = kv[:, (2 + 2 * hk + half) * 128:(3 + 2 * hk + half) * 128]
                vm = jnp.concatenate([vcol, ones_lo if half == 0 else ones_hi], axis=1)
                pv = _dot(jnp.concatenate([probs[c, half] for c in cols], axis=0), vm)
                for i, c in enumerate(cols):
                    part = pv[i * BLOCK:(i + 1) * BLOCK, :]
                    res[c] = part if half == 0 else res[c] + part
        for c in range(A_WIDTH // 128):
            den = res[c][:, 128:] + jnp.where(low_q, esink[c, 0], esink[c, 1])
            mix_s[pl.ds(r0, BLOCK), c * 128:(c + 1) * 128] = (res[c][:, :128] / den).astype(BF16)
        mixed = jnp.concatenate(parts, axis=1) + bs_ref[...]
        mix_s[pl.ds(r0, BLOCK), A_WIDTH:] = (u_s[pl.ds(r0, BLOCK), :] * mixed).astype(BF16)

    slab = min(SLAB_ROWS, tq)
    slabs = [slice(r, r + slab) for r in range(0, tq, slab)]
    tables = [rotary_tables(rows) for rows in slabs]
    xs = [x_ref[rows, :] for rows in slabs]
    hs = [_dot(x.astype(BF16), win_s[...]) for x in xs]
    zs = []
    for rows, x, h, tab in zip(slabs, xs, hs, tables):
        prepare(rows, h, tab)
        for n in range(rows.start // BLOCK, rows.stop // BLOCK):
            block_body(n)
        zs.append(ALPHA * x + _dot(mix_s[rows, :], wout_s[...]))
    kv_s[0:BLOCK, :] = kv_s[tq:tq + BLOCK, :]
    for rows, z in zip(slabs, zs):
        o_ref[rows, :] = _layer_norm(z, g1_ref[...], b1_ref[...])


def _mixer0(x, pos_row, sinks, w_in, invf, etab, cbase, gsum, lng, lnb, w_s, bs_full, w_out, g1, b1,
            b0, bsz, s_len):
    d = x.shape[1]
    t_all = bsz * s_len
    tq = MIXER0_TILE
    nj = s_len // tq
    row = lambda bb, jj: (bb * nj + jj, 0)
    in_w = w_in.shape[1]
    return pl.pallas_call(
        _mixer0_kernel,
        grid=(bsz, nj),
        in_specs=[
            pl.BlockSpec(memory_space=pltpu.SMEM),
            pl.BlockSpec((tq, d), lambda bb, jj: ((b0 + bb) * nj + jj, 0)),
            pl.BlockSpec((1, tq), lambda bb, jj: (0, (b0 + bb) * nj + jj)),
            _const_spec((d, in_w)),
            _const_spec((ROPE_DIM // 2, 1)), _const_spec((128, 3 * 128)), _const_spec((1, 128)),
            _const_spec((B_WIDTH, B_WIDTH)),
            _const_spec((1, B_WIDTH)), _const_spec((1, B_WIDTH)),
            _const_spec((B_GROUPS, BLOCK, BLOCK)),
            _const_spec((BLOCK, B_WIDTH)),
            _const_spec((A_WIDTH + B_WIDTH, d)),
            _const_spec((1, d)), _const_spec((1, d)),
        ],
        out_specs=pl.BlockSpec((tq, d), row),
        out_shape=jax.ShapeDtypeStruct((t_all, d), F32),
        scratch_shapes=[
            pltpu.VMEM((tq, A_WIDTH), BF16),
            pltpu.VMEM((tq + BLOCK, 6 * 128), BF16),
            pltpu.VMEM((tq, B_WIDTH), F32),
            pltpu.VMEM((tq, B_WIDTH), BF16),
            pltpu.VMEM((tq, A_WIDTH + B_WIDTH), BF16),
            pltpu.VMEM((B_GROUPS, BLOCK, BLOCK), BF16),
            pltpu.VMEM((d, in_w), BF16),
            pltpu.VMEM((A_WIDTH + B_WIDTH, d), BF16),
        ],
        compiler_params=_tc_params(2),
        name="mixer0",
    )(sinks, x, pos_row, w_in, invf, etab, cbase, gsum, lng, lnb, w_s, bs_full, w_out, g1, b1)


def _expert_combine_ln(x2, y0_packed, y1_packed, rt, g, b):
    wt = pltpu.bitcast(rt, F32).T
    y = wt[:, 4:5] * _unpack_bf16_pairs(y0_packed) + wt[:, 5:6] * _unpack_bf16_pairs(y1_packed)
    return _layer_norm(ALPHA * x2 + y, g, b)


def _mixer1_kernel(x2_ref, y0_ref, y1_ref, rt_ref, g3_ref, b3_ref,
                   win_ref, pw_ref, ps_ref, wout_ref, g1_ref, b1_ref, o_ref,
                   h_s, mp_s, win_s, pw_s, wout_s):
    j = pl.program_id(1)
    tq = x2_ref.shape[0]
    gw = x2_ref.shape[1] // len(POOL_WINDOWS)
    slabs = [slice(c * SLAB_ROWS, (c + 1) * SLAB_ROWS) for c in range(tq // SLAB_ROWS)]

    @pl.when(jnp.logical_and(pl.program_id(0) == 0, j == 0))
    def _():
        win_s[...] = win_ref[...].astype(BF16)
        pw_s[...] = pw_ref[...].astype(BF16)
        wout_s[...] = wout_ref[...].astype(BF16)

    @pl.when(j == 0)
    def _():
        h_s[0:POOL_HALO, :] = jnp.zeros((POOL_HALO, h_s.shape[1]), F32)

    ng = len(POOL_WINDOWS)
    quarter = SLAB_ROWS // ng

    def sub(rows, i):
        return slice(rows.start + i * quarter, rows.start + (i + 1) * quarter)

    def load_in(rows, i):
        r = sub(rows, i)
        return _expert_combine_ln(x2_ref[r, :], y0_ref[r, :], y1_ref[r, :], rt_ref[:, r],
                                  g3_ref[...], b3_ref[...])

    def project(rows, xb, g):
        lo, hi = g * gw, (g + 1) * gw
        h_s[POOL_HALO + rows.start:POOL_HALO + rows.stop, lo:hi] = _dot(xb, win_s[:, lo:hi])

    def pool(rows, g):
        win = POOL_WINDOWS[g]
        lo, hi = g * gw, (g + 1) * gw
        n = rows.stop - rows.start
        t_pos = j * tq + rows.start + lax.broadcasted_iota(I32, (n, 1), 0)
        ext = h_s[rows.start:rows.stop + POOL_HALO, lo:hi]
        acc = ext
        shift = 1
        while shift < win:
            acc = acc + pltpu.roll(acc, shift, 0)
            shift *= 2
        count = jnp.minimum(t_pos + 1, win).astype(F32)
        pooled = acc[POOL_HALO:, :] / count - ext[POOL_HALO:, :]
        mapped = _dot(pooled.astype(BF16), pw_s[g])
        mp_s[rows, lo:hi] = (mapped * ps_ref[:, lo:hi]).astype(BF16)

    def out_cols(rows, g):
        return _dot(mp_s[rows, :], wout_s[:, g * gw:(g + 1) * gw])

    def finish(rows, x_parts, z_cols, i):
        lo, hi = i * quarter, (i + 1) * quarter
        z = ALPHA * x_parts[i] + jnp.concatenate([zc[lo:hi, :] for zc in z_cols], axis=1)
        o_ref[sub(rows, i), :] = _layer_norm(z, g1_ref[...], b1_ref[...])

    assert len(slabs) == 2
    sa, sb = slabs
    xa = [load_in(sa, i) for i in range(ng)]
    xa_b = jnp.concatenate(xa, axis=0).astype(BF16)
    xb = []
    for g in range(ng):
        project(sa, xa_b, g)
        xb.append(load_in(sb, g))
    xb_b = jnp.concatenate(xb, axis=0).astype(BF16)
    for g in range(ng):
        project(sb, xb_b, g)
        pool(sa, g)
    za = []
    for g in range(ng):
        za.append(out_cols(sa, g))
        pool(sb, g)
    zb = []
    for g in range(ng):
        zb.append(out_cols(sb, g))
        finish(sa, xa, za, g)
    for g in range(ng):
        finish(sb, xb, zb, g)
    h_s[0:POOL_HALO, :] = h_s[tq:tq + POOL_HALO, :]


def _mixer1(x2, y01, rt, g3, b3, w_in, pool_w, pool_scale, w_out, g1, b1, bsz, s_len):
    t_all, d = x2.shape
    tq = TOKEN_TILE
    nj = s_len // tq
    nt = bsz * nj
    row = lambda bb, jj: (bb * nj + jj, 0)
    ng = len(POOL_WINDOWS)
    return pl.pallas_call(
        _mixer1_kernel,
        grid=(bsz, nj),
        in_specs=[
            pl.BlockSpec((tq, d), row),
            pl.BlockSpec((tq, d // 2), row),
            pl.BlockSpec((tq, d // 2), lambda bb, jj: (nt + bb * nj + jj, 0)),
            pl.BlockSpec((8, tq), lambda bb, jj: (0, bb * nj + jj)),
            _const_spec((1, d)), _const_spec((1, d)),
            _const_spec((d, d)),
            _const_spec((ng, d // ng, d // ng)),
            _const_spec((1, d)),
            _const_spec((d, d)),
            _const_spec((1, d)), _const_spec((1, d)),
        ],
        out_specs=pl.BlockSpec((tq, d), row),
        out_shape=jax.ShapeDtypeStruct((t_all, d), F32),
        scratch_shapes=[pltpu.VMEM((tq + POOL_HALO, d), F32), pltpu.VMEM((tq, d), BF16),
                        pltpu.VMEM((d, d), BF16), pltpu.VMEM((ng, d // ng, d // ng), BF16),
                        pltpu.VMEM((d, d), BF16)],
        compiler_params=_tc_params(2),
        name="mixer1",
    )(x2, y01, y01, rt, g3, b3, w_in, pool_w, pool_scale, w_out, g1, b1)


def _kv_kernel(mem_ref, w_ref, o_ref, w_s):
    @pl.when(pl.program_id(0) == 0)
    def _():
        w_s[...] = w_ref[...].astype(BF16)

    o_ref[...] = _dot(mem_ref[...].astype(BF16), w_s[...]).astype(BF16)


def _kv_proj(mem2d, wkv, bsz):
    rows, d = mem2d.shape
    m = rows // bsz
    return pl.pallas_call(
        _kv_kernel,
        grid=(bsz,),
        in_specs=[pl.BlockSpec((m, d), lambda i: (i, 0)), _const_spec(wkv.shape)],
        out_specs=pl.BlockSpec((m, wkv.shape[1]), lambda i: (i, 0)),
        out_shape=jax.ShapeDtypeStruct((rows, wkv.shape[1]), BF16),
        scratch_shapes=[pltpu.VMEM(wkv.shape, BF16)],
        compiler_params=_tc_params(1),
        name="kv_proj",
    )(mem2d, wkv)


def _top2_of4(v):
    hi01, lo01 = jnp.maximum(v[0], v[1]), jnp.minimum(v[0], v[1])
    hi23, lo23 = jnp.maximum(v[2], v[3]), jnp.minimum(v[2], v[3])
    return jnp.maximum(hi01, hi23) + jnp.maximum(jnp.minimum(hi01, hi23), jnp.maximum(lo01, lo23))


def _argmax_first(vals):
    best, idx = vals[0], jnp.zeros(vals[0].shape, I32)
    for i in range(1, len(vals)):
        better = vals[i] > best
        best = jnp.where(better, vals[i], best)
        idx = jnp.where(better, i, idx)
    return best, idx


def _xattn_kernel(x_ref, kv_ref, wq_ref, wo_ref, g2_ref, b2_ref, rwh_ref, rwl_ref, rb_ref,
                  x2_ref, xp_ref, rt_ref, cnt_ref, run_s, wq_s, wo_s):
    first = jnp.logical_and(pl.program_id(0) == 0, pl.program_id(1) == 0)
    tq, d = x_ref.shape

    @pl.when(first)
    def _():
        run_s[...] = jnp.zeros(run_s.shape, F32)
        wq_s[...] = wq_ref[...].astype(BF16)
        wo_s[...] = wo_ref[...].astype(BF16)

    slabs = [slice(c * SLAB_ROWS, (c + 1) * SLAB_ROWS) for c in range(tq // SLAB_ROWS)]
    run = run_s[...]
    zs = [_xattn_attend(*_xattn_query(rows, x_ref, wq_s), kv_ref, wo_s) for rows in slabs]
    logits = [_xattn_norm_logits(rows, z, g2_ref, b2_ref, rwh_ref, rwl_ref, x2_ref, xp_ref)
              for rows, z in zip(slabs, zs)]
    for rows, lg in zip(slabs, logits):
        run = _xattn_route_rows(rows, lg, run, rb_ref, rt_ref)
    run_s[...] = run
    cnt_ref[...] = jnp.broadcast_to(run, cnt_ref.shape).astype(I32)


def _xattn_query(rows, x_ref, wq_s):
    hd = x_ref.shape[1] // X_HEADS
    x = x_ref[rows, :]
    return x, _dot(x.astype(BF16), wq_s[...]) * (hd ** -0.5)


def _xattn_attend(x, q, kv_ref, wo_s):
    d = x.shape[1]
    hd = d // X_HEADS
    outs = []
    for hx in range(X_HEADS):
        qh = q[:, hx * hd:(hx + 1) * hd].astype(BF16)
        kh = kv_ref[:, hx * hd:(hx + 1) * hd]
        vh = kv_ref[:, d + hx * hd:d + (hx + 1) * hd]
        s = _dot_nt(qh, kh)
        p = jnp.exp(s - jnp.max(s, axis=-1, keepdims=True))
        o = _dot(p.astype(BF16), vh) / jnp.sum(p, axis=-1, keepdims=True)
        outs.append(o.astype(BF16))
    return ALPHA * x + _dot(jnp.concatenate(outs, axis=1), wo_s[...])


def _xattn_norm_logits(rows, z, g2_ref, b2_ref, rwh_ref, rwl_ref, x2_ref, xp_ref):
    x2 = _layer_norm(z, g2_ref[...], b2_ref[...])
    x2_ref[rows, :] = x2

    xp_ref[rows, :] = _pack_bf16_pairs(x2)

    xh, xl = _split_bf16(x2)
    both = _dot_nt(jnp.concatenate([rwh_ref[...], rwl_ref[...]], axis=0), xh)
    return both[0:N_EXPERTS] + both[N_EXPERTS:] + _dot_nt(rwh_ref[...], xl)


def _xattn_route_rows(rows, logits, run, rb_ref, rt_ref):
    tq = rows.stop - rows.start
    e_max = jnp.max(logits, axis=0, keepdims=True)
    ex = jnp.exp(logits - e_max)
    scores = ex / jnp.sum(ex, axis=0, keepdims=True)
    biased = scores + rb_ref[...]
    sc = [scores[e:e + 1, :] for e in range(N_EXPERTS)]
    bi = [biased[e:e + 1, :] for e in range(N_EXPERTS)]
    epg = EXPERTS_PER_GROUP
    gscore = [_top2_of4(bi[g * epg:(g + 1) * epg]) for g in range(N_EXPERT_GROUPS)]
    _, gsel = _argmax_first(gscore)

    def pick(vals):
        return [functools.reduce(lambda acc, g: jnp.where(gsel == g, vals[g * epg + i], acc),
                                 range(1, N_EXPERT_GROUPS), vals[i]) for i in range(epg)]

    in_b = pick(bi)
    in_s = pick(sc)
    _, i0 = _argmax_first(in_b)
    _, i1 = _argmax_first([jnp.where(i0 == i, -jnp.inf, in_b[i]) for i in range(epg)])

    def take(vals, idx):
        return functools.reduce(lambda acc, i: jnp.where(idx == i, vals[i], acc), range(1, epg), vals[0])

    s0, s1 = take(in_s, i0), take(in_s, i1)
    w0, w1 = s0 / (s0 + s1), s1 / (s0 + s1)
    e0, e1 = gsel * epg + i0, gsel * epg + i1

    eid = lax.broadcasted_iota(I32, (N_EXPERTS, tq), 0)
    oh0 = eid == e0
    oh1 = eid == e1
    onehot = jnp.where(jnp.logical_or(oh0, oh1), 1.0, 0.0)
    rr = lax.broadcasted_iota(I32, (tq, tq), 0)
    cc = lax.broadcasted_iota(I32, (tq, tq), 1)
    upper = jnp.where(rr < cc, 1.0, 0.0).astype(BF16)
    prefix = _dot(onehot.astype(BF16), upper) + run
    r0 = jnp.sum(jnp.where(oh0, prefix, 0.0), axis=0, keepdims=True).astype(I32)
    r1 = jnp.sum(jnp.where(oh1, prefix, 0.0), axis=0, keepdims=True).astype(I32)

    zero = jnp.zeros((1, tq), I32)
    rt_ref[:, rows] = jnp.concatenate(
        [e0, e1, r0, r1, pltpu.bitcast(w0, I32), pltpu.bitcast(w1, I32), zero, zero], axis=0)
    return run + jnp.sum(onehot, axis=1, keepdims=True)


def _xattn_route(x1, kv, mem_len, wq, wo, g2, b2, rw_hi, rw_lo, rbias, b0, bsz, s_len):
    t_all, d = x1.shape
    tq = TOKEN_TILE
    nj = s_len // tq
    m = mem_len
    row = lambda bb, jj: (bb * nj + jj, 0)
    return pl.pallas_call(
        _xattn_kernel,
        grid=(bsz, nj),
        in_specs=[
            pl.BlockSpec((tq, d), row),
            pl.BlockSpec((m, 2 * d), lambda bb, jj: (b0 + bb, 0)),
            _const_spec((d, d)), _const_spec((d, d)),
            _const_spec((1, d)), _const_spec((1, d)),
            _const_spec((N_EXPERTS, d)), _const_spec((N_EXPERTS, d)),
            _const_spec((N_EXPERTS, 1)),
        ],
        out_specs=[
            pl.BlockSpec((tq, d), row),
            pl.BlockSpec((tq, d // 2), row),
            pl.BlockSpec((8, tq), lambda bb, jj: (0, bb * nj + jj)),
            _const_spec((N_EXPERTS, 128)),
        ],
        out_shape=[
            jax.ShapeDtypeStruct((t_all, d), F32),
            jax.ShapeDtypeStruct((t_all, d // 2), I32),
            jax.ShapeDtypeStruct((8, t_all), I32),
            jax.ShapeDtypeStruct((N_EXPERTS, 128), I32),
        ],
        scratch_shapes=[pltpu.VMEM((N_EXPERTS, 1), F32), pltpu.VMEM((d, d), BF16), pltpu.VMEM((d, d), BF16)],
        compiler_params=_tc_params(2),
        name="xattn_route",
    )(x1, kv, wq, wo, g2, b2, rw_hi, rw_lo, rbias)


def _sc_mesh():
    return plsc.VectorSubcoreMesh(core_axis_name="c", subcore_axis_name="s")


def _sc_params():
    return pltpu.CompilerParams(needs_layout_passes=False)


def _worker_id():
    return lax.axis_index("s") * lax.axis_size("c") + lax.axis_index("c")


def _sc_dispatch(xp, dest2d, n_rows):
    t_all, width = xp.shape
    chunk = dest2d.shape[1]
    tok_w = t_all // SC_WORKERS
    nch = tok_w // chunk
    slot1 = t_all // chunk
    assert t_all % (SC_WORKERS * chunk * 2) == 0

    def body(x_hbm, dest_hbm, out_hbm, idx0_v, idx1_v, buf0, buf1, sem_r, sem_w):
        wid = _worker_id()
        base = wid * tok_w
        pltpu.sync_copy(dest_hbm.at[pl.ds(wid * nch, nch)], idx0_v)
        pltpu.sync_copy(dest_hbm.at[pl.ds(slot1 + wid * nch, nch)], idx1_v)

        def read(c, buf, k):
            return pltpu.make_async_copy(x_hbm.at[pl.ds(base + c * chunk, chunk)], buf, sem_r.at[k])

        def scatter(c, buf):
            a = pltpu.make_async_copy(buf, out_hbm.at[idx0_v.at[c]], sem_w.at[0])
            b = pltpu.make_async_copy(buf, out_hbm.at[idx1_v.at[c]], sem_w.at[1])
            a.start()
            b.start()
            a.wait()
            b.wait()

        read(0, buf0, 0).start()

        @pl.loop(0, nch // 2)
        def _(g):
            c = 2 * g
            read(c + 1, buf1, 1).start()
            read(c, buf0, 0).wait()
            scatter(c, buf0)

            @pl.when(c + 2 < nch)
            def _():
                read(c + 2, buf0, 0).start()

            read(c + 1, buf1, 1).wait()
            scatter(c + 1, buf1)

    return pl.kernel(
        body,
        out_type=jax.ShapeDtypeStruct((n_rows, width), xp.dtype),
        mesh=_sc_mesh(),
        scratch_types=[
            pltpu.VMEM((nch, chunk), I32),
            pltpu.VMEM((nch, chunk), I32),
            pltpu.VMEM((chunk, width), xp.dtype),
            pltpu.VMEM((chunk, width), xp.dtype),
            pltpu.SemaphoreType.DMA((2,)),
            pltpu.SemaphoreType.DMA((2,)),
        ],
        compiler_params=_sc_params(),
        name="sc_dispatch",
    )(xp, dest2d)


def _sc_combine(ys, dest2d):
    n_idx_rows, chunk = dest2d.shape
    width = ys.shape[1]
    nch = n_idx_rows // SC_WORKERS
    assert n_idx_rows % (SC_WORKERS * 2) == 0

    def body(y_hbm, dest_hbm, out_hbm, idx_v, buf0, buf1, sem_g):
        wid = _worker_id()
        base = wid * nch * chunk
        pltpu.sync_copy(dest_hbm.at[pl.ds(wid * nch, nch)], idx_v)

        def gather(c, buf, k):
            return pltpu.make_async_copy(y_hbm.at[idx_v.at[c]], buf, sem_g.at[k])

        def write(c, buf):
            pltpu.sync_copy(buf, out_hbm.at[pl.ds(base + c * chunk, chunk)])

        gather(0, buf0, 0).start()

        @pl.loop(0, nch // 2)
        def _(g):
            c = 2 * g
            gather(c + 1, buf1, 1).start()
            gather(c, buf0, 0).wait()
            write(c, buf0)

            @pl.when(c + 2 < nch)
            def _():
                gather(c + 2, buf0, 0).start()

            gather(c + 1, buf1, 1).wait()
            write(c + 1, buf1)

    return pl.kernel(
        body,
        out_type=jax.ShapeDtypeStruct((n_idx_rows * chunk, width), ys.dtype),
        mesh=_sc_mesh(),
        scratch_types=[
            pltpu.VMEM((nch, chunk), I32),
            pltpu.VMEM((chunk, width), ys.dtype),
            pltpu.VMEM((chunk, width), ys.dtype),
            pltpu.SemaphoreType.DMA((2,)),
        ],
        compiler_params=_sc_params(),
        name="sc_combine",
    )(ys, dest2d)


def _pack_bf16_pairs(v):
    half = v.shape[1] // 2
    lo = pltpu.bitcast(v[:, :half].astype(BF16).astype(F32), jnp.uint32) >> 16
    hi = pltpu.bitcast(v[:, half:].astype(BF16).astype(F32), jnp.uint32) & jnp.uint32(0xFFFF0000)
    return pltpu.bitcast(hi | lo, I32)


def _unpack_bf16_pairs(w):
    w = pltpu.bitcast(w, jnp.uint32)
    lo = pltpu.bitcast(w << 16, F32)
    hi = pltpu.bitcast(w & jnp.uint32(0xFFFF0000), F32)
    return jnp.concatenate([lo, hi], axis=1)


def _ffn_kernel(be_ref, nv_ref, nu_ref, nxt_ref, xs_ref, wg_hbm, wu_hbm, wd_hbm, o_ref,
                wg_f, wu_f, wd_f, wg_s, wu_s, wd_s, sem):
    i = pl.program_id(0)

    def fetch(e):
        return (pltpu.make_async_copy(wg_hbm.at[e], wg_f, sem.at[0]),
                pltpu.make_async_copy(wu_hbm.at[e], wu_f, sem.at[1]),
                pltpu.make_async_copy(wd_hbm.at[e], wd_f, sem.at[2]))

    @pl.when(nxt_ref[i] >= 0)
    def _():
        @pl.when(i == 0)
        def _():
            for cp in fetch(be_ref[0]):
                cp.start()

        for cp in fetch(be_ref[i]):
            cp.wait()
        wg_s[...] = wg_f[...].astype(BF16)
        wu_s[...] = wu_f[...].astype(BF16)
        wd_s[...] = wd_f[...].astype(BF16)

        @pl.when(nxt_ref[i] < N_EXPERTS)
        def _():
            for cp in fetch(nxt_ref[i]):
                cp.start()

    @pl.when(i < nu_ref[0])
    def _():
        live = lax.broadcasted_iota(I32, xs_ref.shape, 0) < nv_ref[i]
        xb = _unpack_bf16_pairs(jnp.where(live, xs_ref[...], 0)).astype(BF16)
        act = jax.nn.silu(_dot(xb, wg_s[...])) * _dot(xb, wu_s[...])
        o_ref[...] = _pack_bf16_pairs(_dot(act.astype(BF16), wd_s[...]))


def _expert_ffn(xs, blk_expert, blk_valid, n_used, blk_next, w_gate, w_up, w_down):
    n_rows, half = xs.shape
    d = 2 * half
    de = w_gate.shape[2]
    bm = EXPERT_ROWS
    rows = lambda i, be, nv, nu, nx: (jnp.minimum(i, nu[0] - 1), 0)
    hbm = pl.BlockSpec(memory_space=pl.ANY)
    return pl.pallas_call(
        _ffn_kernel,
        grid_spec=pltpu.PrefetchScalarGridSpec(
            num_scalar_prefetch=4,
            grid=(n_rows // bm,),
            in_specs=[pl.BlockSpec((bm, half), rows), hbm, hbm, hbm],
            out_specs=pl.BlockSpec((bm, half), rows),
            scratch_shapes=[
                pltpu.VMEM((d, de), F32), pltpu.VMEM((d, de), F32), pltpu.VMEM((de, d), F32),
                pltpu.VMEM((d, de), BF16), pltpu.VMEM((d, de), BF16), pltpu.VMEM((de, d), BF16),
                pltpu.SemaphoreType.DMA((3,)),
            ],
        ),
        out_shape=jax.ShapeDtypeStruct((n_rows, half), I32),
        compiler_params=_tc_params(1),
        name="expert_ffn",
    )(blk_expert, blk_valid, n_used, blk_next, xs, w_gate, w_up, w_down)


def _combine_kernel(x_ref, y0_ref, y1_ref, rt_ref, g_ref, b_ref, o_ref):
    o_ref[...] = _expert_combine_ln(x_ref[...], y0_ref[...], y1_ref[...], rt_ref[...], g_ref[...], b_ref[...])


def _combine_kernel_into(x_ref, y0_ref, y1_ref, rt_ref, g_ref, b_ref, full_ref, o_ref):
    del full_ref
    _combine_kernel(x_ref, y0_ref, y1_ref, rt_ref, g_ref, b_ref, o_ref)


def _combine_ln(x2, y01, rt, g3, b3, into=None, row0=0, full_rows=None):
    t_all, d = x2.shape
    tq = TOKEN_TILE
    nt = t_all // tq
    blk0 = row0 // tq if full_rows else 0
    in_specs = [
        pl.BlockSpec((tq, d), lambda i: (i, 0)),
        pl.BlockSpec((tq, d // 2), lambda i: (i, 0)),
        pl.BlockSpec((tq, d // 2), lambda i: (i + nt, 0)),
        pl.BlockSpec((8, tq), lambda i: (0, i)),
        _const_spec((1, d)), _const_spec((1, d)),
    ]
    args = [x2, y01, y01, rt, g3, b3]
    body, aliases = _combine_kernel, {}
    if into is not None:
        in_specs.append(pl.BlockSpec(memory_space=pl.ANY))
        args.append(into)
        body, aliases = _combine_kernel_into, {len(args) - 1: 0}
    return pl.pallas_call(
        body,
        grid=(nt,),
        in_specs=in_specs,
        out_specs=pl.BlockSpec((tq, d), lambda i: (i + blk0, 0)),
        out_shape=jax.ShapeDtypeStruct((full_rows or t_all, d), F32),
        input_output_aliases=aliases,
        compiler_params=_tc_params(1),
        name="combine_ln",
    )(*args)


def _routing_tables(rt, counts):
    bm = EXPERT_ROWS
    t_all = rt.shape[1]
    n_rows = 2 * t_all + N_EXPERTS * bm
    cnt = counts[:, 0]
    padded = (cnt + bm - 1) // bm * bm
    ends = jnp.cumsum(padded)
    offs = ends - padded
    experts = rt[0:2]
    off_tok = jnp.sum(jnp.where(experts[None] == jnp.arange(N_EXPERTS, dtype=I32)[:, None, None],
                                offs[:, None, None], 0), axis=0)
    dest2d = (off_tok + rt[2:4]).reshape(-1, SC_CHUNK).astype(I32)
    blk_start = jnp.arange(n_rows // bm, dtype=I32) * bm
    blk_expert = jnp.minimum(jnp.sum(blk_start[:, None] >= ends[None, :], axis=1), N_EXPERTS - 1).astype(I32)
    live_end = jnp.sum(jnp.where(blk_expert[:, None] == jnp.arange(N_EXPERTS, dtype=I32)[None, :],
                                 (offs + cnt)[None, :], 0), axis=1)
    blk_valid = jnp.clip(live_end - blk_start, 0, bm).astype(I32)
    n_used = (ends[-1:] // bm).astype(I32)
    eid = jnp.arange(N_EXPERTS, dtype=I32)
    later_present = jnp.logical_and(eid[None, :] > eid[:, None], (cnt > 0)[None, :])
    next_present = jnp.min(jnp.where(later_present, eid[None, :], N_EXPERTS), axis=1)
    next_of_blk = jnp.sum(jnp.where(blk_expert[:, None] == eid[None, :], next_present[None, :], 0), axis=1)
    prev_expert = jnp.concatenate([jnp.full((1,), -1, I32), blk_expert[:-1]])
    is_first = jnp.logical_and(blk_start < ends[-1], blk_expert != prev_expert)
    blk_next = jnp.where(is_first, next_of_blk, -1).astype(I32)
    return dest2d, blk_expert, blk_valid, n_used, blk_next, n_rows


def _layer_tail(x1, kv, mem_len, p, router, b0, bsz, s_len):
    rw_hi, rw_lo, rbias = router
    x2, xp, rt, counts = _xattn_route(x1, kv, mem_len, p["xq"], p["xo"], p["ln2_g"], p["ln2_b"],
                                      rw_hi, rw_lo, rbias, b0, bsz, s_len)
    dest2d, blk_expert, blk_valid, n_used, blk_next, n_rows = _routing_tables(rt, counts)
    xs = _sc_dispatch(xp, dest2d, n_rows)
    ys = _expert_ffn(xs, blk_expert, blk_valid, n_used, blk_next, p["e_gate"], p["e_up"], p["e_down"])
    y01 = _sc_combine(ys, dest2d)
    return x2, y01, rt


def _row(v):
    return v.reshape(1, -1).astype(F32)


def _common_params(xq, xkv, xo, ln2_g, ln2_b, e_gate, e_up, e_down, ln3_g, ln3_b):
    return dict(xq=xq, xkv=xkv, xo=xo, ln2_g=_row(ln2_g), ln2_b=_row(ln2_b),
                e_gate=e_gate, e_up=e_up, e_down=e_down, ln3_g=_row(ln3_g), ln3_b=_row(ln3_b))


def kernel(x, mem, positions, router_w, router_bias, l0_w_in, l0_sinks, l0_sgu_ln_g, l0_sgu_ln_b, l0_sgu_w, l0_sgu_b, l0_w_out, l0_ln1_g, l0_ln1_b, l0_xq, l0_xkv, l0_xo, l0_ln2_g, l0_ln2_b, l0_e_gate, l0_e_up, l0_e_down, l0_ln3_g, l0_ln3_b, l1_w_in, l1_pool_w, l1_pool_scale, l1_w_out, l1_ln1_g, l1_ln1_b, l1_xq, l1_xkv, l1_xo, l1_ln2_g, l1_ln2_b, l1_e_gate, l1_e_up, l1_e_down, l1_ln3_g, l1_ln3_b):
    bsz, s_len, d = x.shape
    assert s_len % TOKEN_TILE == 0 and s_len % MIXER0_TILE == 0 and MIXER0_TILE % BLOCK == 0
    xt = x.reshape(bsz * s_len, d)
    mem2d = mem.reshape(-1, d)

    rw_t = router_w.T.astype(F32)
    rw_hi = rw_t.astype(BF16)
    rw_lo = (rw_t - rw_hi.astype(F32)).astype(BF16)
    router = (rw_hi, rw_lo, router_bias.reshape(-1, 1).astype(F32))

    half = ROPE_DIM // 2
    inv_freq = (ROPE_THETA ** (-(jnp.arange(half, dtype=F32) * 2.0 / ROPE_DIM))).reshape(half, 1)
    etab_np = np.zeros((128, 3 * 128), np.float32)
    cbase_np = np.ones((1, 128), np.float32)
    for ln in range(128):
        dd = ln % HEAD_DIM
        if dd < ROPE_DIM:
            cbase_np[0, ln] = 0.0
            etab_np[[dd % half, half + dd % half], ln] = 1.0
            if dd >= half:
                etab_np[[2 * half + dd - half, 3 * half + dd - half], 128 + ln] = 1.0
            else:
                etab_np[[2 * half + dd, 3 * half + dd], 256 + ln] = -1.0
    etab = jnp.asarray(etab_np, BF16)
    cbase = jnp.asarray(cbase_np)
    pos_row = positions.reshape(1, -1).astype(I32)
    grp = jnp.arange(B_WIDTH) // B_GROUP_DIM
    gsum = (grp[:, None] == grp[None, :]).astype(BF16)
    bs_full = jnp.repeat(l0_sgu_b.T.astype(F32), B_GROUP_DIM, axis=1)

    p0 = _common_params(l0_xq, l0_xkv, l0_xo, l0_ln2_g, l0_ln2_b, l0_e_gate, l0_e_up, l0_e_down,
                        l0_ln3_g, l0_ln3_b)
    p1 = _common_params(l1_xq, l1_xkv, l1_xo, l1_ln2_g, l1_ln2_b, l1_e_gate, l1_e_up, l1_e_down,
                        l1_ln3_g, l1_ln3_b)
    kv0 = _kv_proj(mem2d, p0["xkv"], bsz)
    kv1 = _kv_proj(mem2d, p1["xkv"], bsz)
    mem_len = mem.shape[1]

    n_split = BATCH_SPLIT if bsz % BATCH_SPLIT == 0 else 1
    nb = bsz // n_split
    out = None
    for part in range(n_split):
        b0 = part * nb
        x1 = _mixer0(xt, pos_row, l0_sinks.astype(F32), l0_w_in, inv_freq, etab, cbase, gsum,
                     _row(l0_sgu_ln_g), _row(l0_sgu_ln_b), l0_sgu_w.astype(F32), bs_full,
                     l0_w_out, _row(l0_ln1_g), _row(l0_ln1_b), b0, nb, s_len)
        x2, y01, rt = _layer_tail(x1, kv0, mem_len, p0, router, b0, nb, s_len)
        x1 = _mixer1(x2, y01, rt, p0["ln3_g"], p0["ln3_b"], l1_w_in, l1_pool_w, _row(l1_pool_scale),
                     l1_w_out, _row(l1_ln1_g), _row(l1_ln1_b), nb, s_len)
        x2, y01, rt = _layer_tail(x1, kv1, mem_len, p1, router, b0, nb, s_len)
        out = _combine_ln(x2, y01, rt, p1["ln3_g"], p1["ln3_b"], into=out, row0=b0 * s_len,
                          full_rows=bsz * s_len)
    return out.reshape(bsz, s_len, d)
```

```python
import functools

import numpy as np
import jax
import jax.numpy as jnp
from jax import lax
from jax.experimental import pallas as pl
from jax.experimental.pallas import tpu as pltpu
from jax.experimental.pallas import tpu_sc as plsc

F32 = jnp.float32
BF16 = jnp.bfloat16
I32 = jnp.int32

DEPTH = 2
ALPHA = (2.0 * DEPTH) ** 0.25
LN_EPS = 1e-5

HEAD_DIM = 64
A_Q_HEADS = 8
A_KV_HEADS = 2
A_GROUP = A_Q_HEADS // A_KV_HEADS
BLOCK = 128
ROPE_THETA = 500000.0
ROPE_DIM = HEAD_DIM // 4
A_WIDTH = A_Q_HEADS * HEAD_DIM
KV_WIDTH = A_KV_HEADS * HEAD_DIM
B_GROUPS = 8
B_GROUP_DIM = 64
B_WIDTH = B_GROUPS * B_GROUP_DIM
POOL_WINDOWS = (2, 4, 8, 16)
POOL_HALO = 16
X_HEADS = 4
N_EXPERTS = 16
N_EXPERT_GROUPS = 4
EXPERTS_PER_GROUP = 4

TOKEN_TILE = 1024
MIXER0_TILE = 1024
SLAB_ROWS = 512
PIECE_COLS = 256
EXPERT_ROWS = 512
BATCH_SPLIT = 2
SC_WORKERS = 32
SC_CHUNK = 64
VMEM_LIMIT = 56 * 1024 * 1024
NEG_BIG = -1e30


def _layer_norm(z, g, b):
    mu = jnp.mean(z, axis=-1, keepdims=True)
    d = z - mu
    var = jnp.mean(d * d, axis=-1, keepdims=True)
    return d * lax.rsqrt(var + LN_EPS) * g + b


def _dot(a, b):
    return jnp.dot(a, b, preferred_element_type=F32)


def _dot_nt(a, b):
    return lax.dot_general(a, b, (((1,), (1,)), ((), ())), preferred_element_type=F32)


def _split_bf16(v):
    hi = v.astype(BF16)
    lo = (v - hi.astype(F32)).astype(BF16)
    return hi, lo


def _tc_params(n_axes):
    return pltpu.CompilerParams(dimension_semantics=("arbitrary",) * n_axes,
                                vmem_limit_bytes=VMEM_LIMIT)


def _const_spec(shape):
    nd = len(shape)
    return pl.BlockSpec(shape, lambda *_: (0,) * nd, pipeline_mode=pl.Buffered(1))

def _mixer0_kernel(sinks_ref, x_ref, pos_ref, win_ref, invf_ref, etab_ref, cbase_ref, gsum_ref,
                   lng_ref, lnb_ref, ws_ref, bs_ref, wout_ref, g1_ref, b1_ref,
                   o_ref, q_s, kv_s, u_s, vn_s, mix_s, wt_s, win_s, wout_s):
    b = pl.program_id(0)
    j = pl.program_id(1)
    tq = x_ref.shape[0]
    nblk = tq // BLOCK
    kvw = kv_s.shape[1]

    @pl.when(jnp.logical_and(b == 0, j == 0))
    def _():
        win_s[...] = win_ref[...].astype(BF16)
        wout_s[...] = wout_ref[...].astype(BF16)
        r = lax.broadcasted_iota(I32, (BLOCK, BLOCK), 0)
        c = lax.broadcasted_iota(I32, (BLOCK, BLOCK), 1)
        for g in range(B_GROUPS):
            wt_s[g] = jnp.where(c <= r, ws_ref[g], 0.0).astype(BF16)

    @pl.when(j == 0)
    def _():
        kv_s[0:BLOCK, :] = jnp.zeros((BLOCK, kvw), BF16)

    c1 = A_WIDTH
    c2 = c1 + KV_WIDTH
    c3 = c2 + KV_WIDTH
    c4 = c3 + B_WIDTH

    def rotary_tables(rows):
        n = rows.stop - rows.start
        ang = invf_ref[...] * pos_ref[:, rows].astype(F32)
        c8 = jnp.cos(ang)
        s8 = jnp.sin(ang)
        c8h = c8.astype(BF16).astype(F32)
        s8h = s8.astype(BF16).astype(F32)
        stack = jnp.concatenate([c8h, c8 - c8h, s8h, s8 - s8h, jnp.zeros((128 - 4 * 8, n), F32)], axis=0)
        tabs = _dot(stack.T.astype(BF16), etab_ref[...])
        return tabs[:, 0:128] + cbase_ref[...], tabs[:, 128:256], tabs[:, 256:384]

    pw = PIECE_COLS
    n_pieces = win_s.shape[1] // pw
    assert (c1 // pw, c2 // pw, c3 // pw, c4 // pw) == (2, 2, 3, 5) and c3 % pw == 0 and n_pieces == 7

    def project(xb, k):
        return _dot(xb, win_s[:, k * pw:(k + 1) * pw])

    def prepare(rows, hk, tables, k):
        n = rows.stop - rows.start
        cs, sa, sb = tables

        def rope(t):
            return t * cs + pltpu.roll(t, ROPE_DIM // 2, 1) * sa + pltpu.roll(t, 128 - ROPE_DIM // 2, 1) * sb

        if k < 2:
            for c in range(pw // 128):
                t = hk[:, c * 128:(c + 1) * 128] * (HEAD_DIM ** -0.5)
                col = k * (pw // 128) + c
                q_s[rows, col * 128:(col + 1) * 128] = rope(t).astype(BF16)
        elif k == 2:
            low = lax.broadcasted_iota(I32, (n, 128), 1) < HEAD_DIM
            kr = rope(hk[:, 0:KV_WIDTH])
            kx = pltpu.roll(kr, HEAD_DIM, 1)
            vr = hk[:, KV_WIDTH:]
            vx = pltpu.roll(vr, HEAD_DIM, 1)
            kv_cols = [jnp.where(low, kr, kx), jnp.where(low, kx, kr),
                       jnp.where(low, vr, 0.0), jnp.where(low, 0.0, vx),
                       jnp.where(low, vx, 0.0), jnp.where(low, 0.0, vr)]
            for c, col in enumerate(kv_cols):
                kv_s[BLOCK + rows.start:BLOCK + rows.stop, c * 128:(c + 1) * 128] = col.astype(BF16)
        elif k < 5:
            lo = (k - 3) * pw
            u_s[rows, lo:lo + pw] = jax.nn.gelu(hk)
        else:
            lo = (k - 5) * pw
            v = jax.nn.gelu(hk)
            gsum = gsum_ref[...]
            mean = _dot(v.astype(BF16), gsum) * (1.0 / B_GROUP_DIM)
            d = v - mean
            var = _dot((d * d).astype(BF16), gsum) * (1.0 / B_GROUP_DIM)
            vn_s[rows, lo:lo + pw] = (d * lax.rsqrt(var + LN_EPS) * lng_ref[:, lo:lo + pw]
                                      + lnb_ref[:, lo:lo + pw]).astype(BF16)

    qi = lax.broadcasted_iota(I32, (BLOCK, 2 * BLOCK), 0)
    kj = lax.broadcasted_iota(I32, (BLOCK, 2 * BLOCK), 1)
    rel = qi + BLOCK - kj
    band = jnp.logical_and(rel >= 0, rel < BLOCK)
    low_q = lax.broadcasted_iota(I32, (BLOCK, 128), 1) < HEAD_DIM
    low_k = lax.broadcasted_iota(I32, (2 * BLOCK, 128), 1) < HEAD_DIM
    ones_lo = jnp.where(low_k, 1.0, 0.0).astype(BF16)
    ones_hi = jnp.where(low_k, 0.0, 1.0).astype(BF16)
    zero_q = jnp.zeros((BLOCK, 128), BF16)

    def block_body(n):
        r0 = n * BLOCK
        kv = kv_s[pl.ds(r0, 2 * BLOCK), :]
        qb = q_s[pl.ds(r0, BLOCK), :]
        valid = jnp.logical_and(band, kj >= jnp.where(j == 0, BLOCK, 0)) if n == 0 else band
        cols_per_kv = A_GROUP // 2
        scores = {}
        for hk in range(A_KV_HEADS):
            cols = range(hk * cols_per_kv, (hk + 1) * cols_per_kv)
            pieces = []
            for c in cols:
                qp = qb[:, c * 128:(c + 1) * 128]
                pieces += [jnp.where(low_q, qp, zero_q), jnp.where(low_q, zero_q, qp)]
            sc = _dot_nt(jnp.concatenate(pieces, axis=0), kv[:, hk * 128:(hk + 1) * 128])
            for i, c in enumerate(cols):
                for half in range(2):
                    r = (2 * i + half) * BLOCK
                    scores[c, half] = sc[r:r + BLOCK, :]
        vnb = vn_s[pl.ds(r0, BLOCK), :]
        parts = []
        for c in range(B_WIDTH // 128):
            vp = vnb[:, c * 128:(c + 1) * 128]
            parts.append(_dot(wt_s[2 * c], jnp.where(low_q, vp, zero_q))
                         + _dot(wt_s[2 * c + 1], jnp.where(low_q, zero_q, vp)))
        probs, esink = {}, {}
        for (c, half), sc in scores.items():
            s = jnp.where(valid, sc, NEG_BIG)
            sink = sinks_ref[2 * c + half]
            m = jnp.maximum(jnp.max(s, axis=-1, keepdims=True), sink)
            probs[c, half] = jnp.exp(s - m).astype(BF16)
            esink[c, half] = jnp.exp(sink - m)
        res = {}
        for hk in range(A_KV_HEADS):
            cols = range(hk * cols_per_kv, (hk + 1) * cols_per_kv)
            for half in range(2):
                vcol = kv[:, (2 + 2 * hk + half) * 128:(3 + 2 * hk + half) * 128]
                vm = jnp.concatenate([vcol, ones_lo if half == 0 else ones_hi], axis=1)
                pv = _dot(jnp.concatenate([probs[c, half] for c in cols], axis=0), vm)
                for i, c in enumerate(cols):
                    part = pv[i * BLOCK:(i + 1) * BLOCK, :]
                    res[c] = part if half == 0 else res[c] + part
        for c in range(A_WIDTH // 128):
            den = res[c][:, 128:] + jnp.where(low_q, esink[c, 0], esink[c, 1])
            mix_s[pl.ds(r0, BLOCK), c * 128:(c + 1) * 128] = (res[c][:, :128] / den).astype(BF16)
        mixed = jnp.concatenate(parts, axis=1) + bs_ref[...]
        mix_s[pl.ds(r0, BLOCK), A_WIDTH:] = (u_s[pl.ds(r0, BLOCK), :] * mixed).astype(BF16)

    def out_cols(rows, c):
        return _dot(mix_s[rows, :], wout_s[:, c * pw:(c + 1) * pw])

    def finish(rows, z_cols, i):
        r = slice(rows.start + i * BLOCK, rows.start + (i + 1) * BLOCK)
        z = ALPHA * x_ref[r, :] + jnp.concatenate([zc[i * BLOCK:(i + 1) * BLOCK, :] for zc in z_cols], axis=1)
        o_ref[r, :] = _layer_norm(z, g1_ref[...], b1_ref[...])

    assert tq == 2 * SLAB_ROWS and SLAB_ROWS == 4 * BLOCK and wout_s.shape[1] == 4 * pw
    sa, sb = slice(0, SLAB_ROWS), slice(SLAB_ROWS, tq)
    tab_a, tab_b = rotary_tables(sa), rotary_tables(sb)
    xa = x_ref[sa, :].astype(BF16)
    ha = [project(xa, k) for k in range(n_pieces)]
    xb = x_ref[sb, :].astype(BF16)
    hb = []
    for k in range(n_pieces):
        hb.append(project(xb, k))
        prepare(sa, ha[k], tab_a, k)
    pieces_b = iter(range(n_pieces))
    for n in range(4):
        block_body(n)
        for k in [next(pieces_b) for _ in range(2 if n < 3 else 1)]:
            prepare(sb, hb[k], tab_b, k)
    za = []
    for n in range(4):
        block_body(4 + n)
        za.append(out_cols(sa, n))
    kv_s[0:BLOCK, :] = kv_s[tq:tq + BLOCK, :]
    zb = []
    for n in range(4):
        zb.append(out_cols(sb, n))
        finish(sa, za, n)
    for n in range(4):
        finish(sb, zb, n)


def _mixer0(x, pos_row, sinks, w_in, invf, etab, cbase, gsum, lng, lnb, w_s, bs_full, w_out, g1, b1,
            b0, bsz, s_len):
    d = x.shape[1]
    t_all = bsz * s_len
    tq = MIXER0_TILE
    nj = s_len // tq
    row = lambda bb, jj: (bb * nj + jj, 0)
    in_w = w_in.shape[1]
    return pl.pallas_call(
        _mixer0_kernel,
        grid=(bsz, nj),
        in_specs=[
            pl.BlockSpec(memory_space=pltpu.SMEM),
            pl.BlockSpec((tq, d), lambda bb, jj: ((b0 + bb) * nj + jj, 0)),
            pl.BlockSpec((1, tq), lambda bb, jj: (0, (b0 + bb) * nj + jj)),
            _const_spec((d, in_w)),
            _const_spec((ROPE_DIM // 2, 1)), _const_spec((128, 3 * 128)), _const_spec((1, 128)),
            _const_spec((PIECE_COLS, PIECE_COLS)),
            _const_spec((1, B_WIDTH)), _const_spec((1, B_WIDTH)),
            _const_spec((B_GROUPS, BLOCK, BLOCK)),
            _const_spec((BLOCK, B_WIDTH)),
            _const_spec((A_WIDTH + B_WIDTH, d)),
            _const_spec((1, d)), _const_spec((1, d)),
        ],
        out_specs=pl.BlockSpec((tq, d), row),
        out_shape=jax.ShapeDtypeStruct((t_all, d), F32),
        scratch_shapes=[
            pltpu.VMEM((tq, A_WIDTH), BF16),
            pltpu.VMEM((tq + BLOCK, 6 * 128), BF16),
            pltpu.VMEM((tq, B_WIDTH), F32),
            pltpu.VMEM((tq, B_WIDTH), BF16),
            pltpu.VMEM((tq, A_WIDTH + B_WIDTH), BF16),
            pltpu.VMEM((B_GROUPS, BLOCK, BLOCK), BF16),
            pltpu.VMEM((d, in_w), BF16),
            pltpu.VMEM((A_WIDTH + B_WIDTH, d), BF16),
        ],
        compiler_params=_tc_params(2),
        name="mixer0",
    )(sinks, x, pos_row, w_in, invf, etab, cbase, gsum, lng, lnb, w_s, bs_full, w_out, g1, b1)


def _expert_combine_ln(x2, y0_packed, y1_packed, rt, g, b):
    wt = pltpu.bitcast(rt, F32).T
    y = wt[:, 4:5] * _unpack_bf16_pairs(y0_packed) + wt[:, 5:6] * _unpack_bf16_pairs(y1_packed)
    return _layer_norm(ALPHA * x2 + y, g, b)


def _mixer1_kernel(x2_ref, y0_ref, y1_ref, rt_ref, g3_ref, b3_ref,
                   win_ref, pw_ref, ps_ref, wout_ref, g1_ref, b1_ref, o_ref,
                   h_s, mp_s, win_s, pw_s, wout_s):
    j = pl.program_id(1)
    tq = x2_ref.shape[0]
    gw = x2_ref.shape[1] // len(POOL_WINDOWS)
    slabs = [slice(c * SLAB_ROWS, (c + 1) * SLAB_ROWS) for c in range(tq // SLAB_ROWS)]

    @pl.when(jnp.logical_and(pl.program_id(0) == 0, j == 0))
    def _():
        win_s[...] = win_ref[...].astype(BF16)
        pw_s[...] = pw_ref[...].astype(BF16)
        wout_s[...] = wout_ref[...].astype(BF16)

    @pl.when(j == 0)
    def _():
        h_s[0:POOL_HALO, :] = jnp.zeros((POOL_HALO, h_s.shape[1]), F32)

    ng = len(POOL_WINDOWS)
    quarter = SLAB_ROWS // ng

    def sub(rows, i):
        return slice(rows.start + i * quarter, rows.start + (i + 1) * quarter)

    def load_in(rows, i):
        r = sub(rows, i)
        return _expert_combine_ln(x2_ref[r, :], y0_ref[r, :], y1_ref[r, :], rt_ref[:, r],
                                  g3_ref[...], b3_ref[...])

    def project(rows, xb, g):
        lo, hi = g * gw, (g + 1) * gw
        h_s[POOL_HALO + rows.start:POOL_HALO + rows.stop, lo:hi] = _dot(xb, win_s[:, lo:hi])

    def pool(rows, g):
        win = POOL_WINDOWS[g]
        lo, hi = g * gw, (g + 1) * gw
        n = rows.stop - rows.start
        t_pos = j * tq + rows.start + lax.broadcasted_iota(I32, (n, 1), 0)
        ext = h_s[rows.start:rows.stop + POOL_HALO, lo:hi]
        acc = ext
        shift = 1
        while shift < win:
            acc = acc + pltpu.roll(acc, shift, 0)
            shift *= 2
        count = jnp.minimum(t_pos + 1, win).astype(F32)
        pooled = acc[POOL_HALO:, :] / count - ext[POOL_HALO:, :]
        mapped = _dot(pooled.astype(BF16), pw_s[g])
        mp_s[rows, lo:hi] = (mapped * ps_ref[:, lo:hi]).astype(BF16)

    def out_cols(rows, g):
        return _dot(mp_s[rows, :], wout_s[:, g * gw:(g + 1) * gw])

    def finish(rows, x_parts, z_cols, i):
        lo, hi = i * quarter, (i + 1) * quarter
        z = ALPHA * x_parts[i] + jnp.concatenate([zc[lo:hi, :] for zc in z_cols], axis=1)
        o_ref[sub(rows, i), :] = _layer_norm(z, g1_ref[...], b1_ref[...])

    assert len(slabs) == 2
    sa, sb = slabs
    xa = [load_in(sa, i) for i in range(ng)]
    xa_b = jnp.concatenate(xa, axis=0).astype(BF16)
    xb = []
    for g in range(ng):
        project(sa, xa_b, g)
        xb.append(load_in(sb, g))
    xb_b = jnp.concatenate(xb, axis=0).astype(BF16)
    for g in range(ng):
        project(sb, xb_b, g)
        pool(sa, g)
    za = []
    for g in range(ng):
        za.append(out_cols(sa, g))
        pool(sb, g)
    zb = []
    for g in range(ng):
        zb.append(out_cols(sb, g))
        finish(sa, xa, za, g)
    for g in range(ng):
        finish(sb, xb, zb, g)
    h_s[0:POOL_HALO, :] = h_s[tq:tq + POOL_HALO, :]


def _mixer1(x2, y01, rt, g3, b3, w_in, pool_w, pool_scale, w_out, g1, b1, bsz, s_len):
    t_all, d = x2.shape
    tq = TOKEN_TILE
    nj = s_len // tq
    nt = bsz * nj
    row = lambda bb, jj: (bb * nj + jj, 0)
    ng = len(POOL_WINDOWS)
    return pl.pallas_call(
        _mixer1_kernel,
        grid=(bsz, nj),
        in_specs=[
            pl.BlockSpec((tq, d), row),
            pl.BlockSpec((tq, d // 2), row),
            pl.BlockSpec((tq, d // 2), lambda bb, jj: (nt + bb * nj + jj, 0)),
            pl.BlockSpec((8, tq), lambda bb, jj: (0, bb * nj + jj)),
            _const_spec((1, d)), _const_spec((1, d)),
            _const_spec((d, d)),
            _const_spec((ng, d // ng, d // ng)),
            _const_spec((1, d)),
            _const_spec((d, d)),
            _const_spec((1, d)), _const_spec((1, d)),
        ],
        out_specs=pl.BlockSpec((tq, d), row),
        out_shape=jax.ShapeDtypeStruct((t_all, d), F32),
        scratch_shapes=[pltpu.VMEM((tq + POOL_HALO, d), F32), pltpu.VMEM((tq, d), BF16),
                        pltpu.VMEM((d, d), BF16), pltpu.VMEM((ng, d // ng, d // ng), BF16),
                        pltpu.VMEM((d, d), BF16)],
        compiler_params=_tc_params(2),
        name="mixer1",
    )(x2, y01, y01, rt, g3, b3, w_in, pool_w, pool_scale, w_out, g1, b1)


def _kv_kernel(mem_ref, w_ref, o_ref, w_s):
    @pl.when(pl.program_id(0) == 0)
    def _():
        w_s[...] = w_ref[...].astype(BF16)

    o_ref[...] = _dot(mem_ref[...].astype(BF16), w_s[...]).astype(BF16)


def _kv_proj(mem2d, wkv, bsz):
    rows, d = mem2d.shape
    m = rows // bsz
    return pl.pallas_call(
        _kv_kernel,
        grid=(bsz,),
        in_specs=[pl.BlockSpec((m, d), lambda i: (i, 0)), _const_spec(wkv.shape)],
        out_specs=pl.BlockSpec((m, wkv.shape[1]), lambda i: (i, 0)),
        out_shape=jax.ShapeDtypeStruct((rows, wkv.shape[1]), BF16),
        scratch_shapes=[pltpu.VMEM(wkv.shape, BF16)],
        compiler_params=_tc_params(1),
        name="kv_proj",
    )(mem2d, wkv)


def _top2_of4(v):
    hi01, lo01 = jnp.maximum(v[0], v[1]), jnp.minimum(v[0], v[1])
    hi23, lo23 = jnp.maximum(v[2], v[3]), jnp.minimum(v[2], v[3])
    return jnp.maximum(hi01, hi23) + jnp.maximum(jnp.minimum(hi01, hi23), jnp.maximum(lo01, lo23))


def _argmax_first(vals):
    best, idx = vals[0], jnp.zeros(vals[0].shape, I32)
    for i in range(1, len(vals)):
        better = vals[i] > best
        best = jnp.where(better, vals[i], best)
        idx = jnp.where(better, i, idx)
    return best, idx


def _xattn_kernel(x_ref, kv_ref, wq_ref, wo_ref, g2_ref, b2_ref, rwh_ref, rwl_ref, rb_ref,
                  x2_ref, xp_ref, rt_ref, cnt_ref, run_s, wq_s, wo_s):
    first = jnp.logical_and(pl.program_id(0) == 0, pl.program_id(1) == 0)
    tq, d = x_ref.shape

    @pl.when(first)
    def _():
        run_s[...] = jnp.zeros(run_s.shape, F32)
        wq_s[...] = wq_ref[...].astype(BF16)
        wo_s[...] = wo_ref[...].astype(BF16)

    slabs = [slice(c * SLAB_ROWS, (c + 1) * SLAB_ROWS) for c in range(tq // SLAB_ROWS)]
    run = run_s[...]
    zs = [_xattn_attend(*_xattn_query(rows, x_ref, wq_s), kv_ref, wo_s) for rows in slabs]
    logits = [_xattn_norm_logits(rows, z, g2_ref, b2_ref, rwh_ref, rwl_ref, x2_ref, xp_ref)
              for rows, z in zip(slabs, zs)]
    for rows, lg in zip(slabs, logits):
        run = _xattn_route_rows(rows, lg, run, rb_ref, rt_ref)
    run_s[...] = run
    cnt_ref[...] = jnp.broadcast_to(run, cnt_ref.shape).astype(I32)


def _xattn_query(rows, x_ref, wq_s):
    hd = x_ref.shape[1] // X_HEADS
    x = x_ref[rows, :]
    return x, _dot(x.astype(BF16), wq_s[...]) * (hd ** -0.5)


def _xattn_attend(x, q, kv_ref, wo_s):
    d = x.shape[1]
    hd = d // X_HEADS
    outs = []
    for hx in range(X_HEADS):
        qh = q[:, hx * hd:(hx + 1) * hd].astype(BF16)
        kh = kv_ref[:, hx * hd:(hx + 1) * hd]
        vh = kv_ref[:, d + hx * hd:d + (hx + 1) * hd]
        s = _dot_nt(qh, kh)
        p = jnp.exp(s - jnp.max(s, axis=-1, keepdims=True))
        o = _dot(p.astype(BF16), vh) / jnp.sum(p, axis=-1, keepdims=True)
        outs.append(o.astype(BF16))
    return ALPHA * x + _dot(jnp.concatenate(outs, axis=1), wo_s[...])


def _xattn_norm_logits(rows, z, g2_ref, b2_ref, rwh_ref, rwl_ref, x2_ref, xp_ref):
    x2 = _layer_norm(z, g2_ref[...], b2_ref[...])
    x2_ref[rows, :] = x2

    xp_ref[rows, :] = _pack_bf16_pairs(x2)

    xh, xl = _split_bf16(x2)
    both = _dot_nt(jnp.concatenate([rwh_ref[...], rwl_ref[...]], axis=0), xh)
    return both[0:N_EXPERTS] + both[N_EXPERTS:] + _dot_nt(rwh_ref[...], xl)


def _xattn_route_rows(rows, logits, run, rb_ref, rt_ref):
    tq = rows.stop - rows.start
    e_max = jnp.max(logits, axis=0, keepdims=True)
    ex = jnp.exp(logits - e_max)
    scores = ex / jnp.sum(ex, axis=0, keepdims=True)
    biased = scores + rb_ref[...]
    sc = [scores[e:e + 1, :] for e in range(N_EXPERTS)]
    bi = [biased[e:e + 1, :] for e in range(N_EXPERTS)]
    epg = EXPERTS_PER_GROUP
    gscore = [_top2_of4(bi[g * epg:(g + 1) * epg]) for g in range(N_EXPERT_GROUPS)]
    _, gsel = _argmax_first(gscore)

    def pick(vals):
        return [functools.reduce(lambda acc, g: jnp.where(gsel == g, vals[g * epg + i], acc),
                                 range(1, N_EXPERT_GROUPS), vals[i]) for i in range(epg)]

    in_b = pick(bi)
    in_s = pick(sc)
    _, i0 = _argmax_first(in_b)
    _, i1 = _argmax_first([jnp.where(i0 == i, -jnp.inf, in_b[i]) for i in range(epg)])

    def take(vals, idx):
        return functools.reduce(lambda acc, i: jnp.where(idx == i, vals[i], acc), range(1, epg), vals[0])

    s0, s1 = take(in_s, i0), take(in_s, i1)
    w0, w1 = s0 / (s0 + s1), s1 / (s0 + s1)
    e0, e1 = gsel * epg + i0, gsel * epg + i1

    eid = lax.broadcasted_iota(I32, (N_EXPERTS, tq), 0)
    oh0 = eid == e0
    oh1 = eid == e1
    onehot = jnp.where(jnp.logical_or(oh0, oh1), 1.0, 0.0)
    rr = lax.broadcasted_iota(I32, (tq, tq), 0)
    cc = lax.broadcasted_iota(I32, (tq, tq), 1)
    upper = jnp.where(rr < cc, 1.0, 0.0).astype(BF16)
    prefix = _dot(onehot.astype(BF16), upper) + run
    r0 = jnp.sum(jnp.where(oh0, prefix, 0.0), axis=0, keepdims=True).astype(I32)
    r1 = jnp.sum(jnp.where(oh1, prefix, 0.0), axis=0, keepdims=True).astype(I32)

    zero = jnp.zeros((1, tq), I32)
    rt_ref[:, rows] = jnp.concatenate(
        [e0, e1, r0, r1, pltpu.bitcast(w0, I32), pltpu.bitcast(w1, I32), zero, zero], axis=0)
    return run + jnp.sum(onehot, axis=1, keepdims=True)


def _xattn_route(x1, kv, mem_len, wq, wo, g2, b2, rw_hi, rw_lo, rbias, b0, bsz, s_len):
    t_all, d = x1.shape
    tq = TOKEN_TILE
    nj = s_len // tq
    m = mem_len
    row = lambda bb, jj: (bb * nj + jj, 0)
    return pl.pallas_call(
        _xattn_kernel,
        grid=(bsz, nj),
        in_specs=[
            pl.BlockSpec((tq, d), row),
            pl.BlockSpec((m, 2 * d), lambda bb, jj: (b0 + bb, 0)),
            _const_spec((d, d)), _const_spec((d, d)),
            _const_spec((1, d)), _const_spec((1, d)),
            _const_spec((N_EXPERTS, d)), _const_spec((N_EXPERTS, d)),
            _const_spec((N_EXPERTS, 1)),
        ],
        out_specs=[
            pl.BlockSpec((tq, d), row),
            pl.BlockSpec((tq, d // 2), row),
            pl.BlockSpec((8, tq), lambda bb, jj: (0, bb * nj + jj)),
            _const_spec((N_EXPERTS, 128)),
        ],
        out_shape=[
            jax.ShapeDtypeStruct((t_all, d), F32),
            jax.ShapeDtypeStruct((t_all, d // 2), I32),
            jax.ShapeDtypeStruct((8, t_all), I32),
            jax.ShapeDtypeStruct((N_EXPERTS, 128), I32),
        ],
        scratch_shapes=[pltpu.VMEM((N_EXPERTS, 1), F32), pltpu.VMEM((d, d), BF16), pltpu.VMEM((d, d), BF16)],
        compiler_params=_tc_params(2),
        name="xattn_route",
    )(x1, kv, wq, wo, g2, b2, rw_hi, rw_lo, rbias)


def _sc_mesh():
    return plsc.VectorSubcoreMesh(core_axis_name="c", subcore_axis_name="s")


def _sc_params():
    return pltpu.CompilerParams(needs_layout_passes=False)


def _worker_id():
    return lax.axis_index("s") * lax.axis_size("c") + lax.axis_index("c")


def _sc_dispatch(xp, dest2d, n_rows):
    t_all, width = xp.shape
    chunk = dest2d.shape[1]
    tok_w = t_all // SC_WORKERS
    nch = tok_w // chunk
    slot1 = t_all // chunk
    assert t_all % (SC_WORKERS * chunk * 2) == 0

    def body(x_hbm, dest_hbm, out_hbm, idx0_v, idx1_v, buf0, buf1, sem_r, sem_w):
        wid = _worker_id()
        base = wid * tok_w
        pltpu.sync_copy(dest_hbm.at[pl.ds(wid * nch, nch)], idx0_v)
        pltpu.sync_copy(dest_hbm.at[pl.ds(slot1 + wid * nch, nch)], idx1_v)

        def read(c, buf, k):
            return pltpu.make_async_copy(x_hbm.at[pl.ds(base + c * chunk, chunk)], buf, sem_r.at[k])

        def scatter(c, buf):
            a = pltpu.make_async_copy(buf, out_hbm.at[idx0_v.at[c]], sem_w.at[0])
            b = pltpu.make_async_copy(buf, out_hbm.at[idx1_v.at[c]], sem_w.at[1])
            a.start()
            b.start()
            a.wait()
            b.wait()

        read(0, buf0, 0).start()

        @pl.loop(0, nch // 2)
        def _(g):
            c = 2 * g
            read(c + 1, buf1, 1).start()
            read(c, buf0, 0).wait()
            scatter(c, buf0)

            @pl.when(c + 2 < nch)
            def _():
                read(c + 2, buf0, 0).start()

            read(c + 1, buf1, 1).wait()
            scatter(c + 1, buf1)

    return pl.kernel(
        body,
        out_type=jax.ShapeDtypeStruct((n_rows, width), xp.dtype),
        mesh=_sc_mesh(),
        scratch_types=[
            pltpu.VMEM((nch, chunk), I32),
            pltpu.VMEM((nch, chunk), I32),
            pltpu.VMEM((chunk, width), xp.dtype),
            pltpu.VMEM((chunk, width), xp.dtype),
            pltpu.SemaphoreType.DMA((2,)),
            pltpu.SemaphoreType.DMA((2,)),
        ],
        compiler_params=_sc_params(),
        name="sc_dispatch",
    )(xp, dest2d)


def _sc_combine(ys, dest2d):
    n_idx_rows, chunk = dest2d.shape
    width = ys.shape[1]
    nch = n_idx_rows // SC_WORKERS
    assert n_idx_rows % (SC_WORKERS * 2) == 0

    def body(y_hbm, dest_hbm, out_hbm, idx_v, buf0, buf1, sem_g):
        wid = _worker_id()
        base = wid * nch * chunk
        pltpu.sync_copy(dest_hbm.at[pl.ds(wid * nch, nch)], idx_v)

        def gather(c, buf, k):
            return pltpu.make_async_copy(y_hbm.at[idx_v.at[c]], buf, sem_g.at[k])

        def write(c, buf):
            pltpu.sync_copy(buf, out_hbm.at[pl.ds(base + c * chunk, chunk)])

        gather(0, buf0, 0).start()

        @pl.loop(0, nch // 2)
        def _(g):
            c = 2 * g
            gather(c + 1, buf1, 1).start()
            gather(c, buf0, 0).wait()
            write(c, buf0)

            @pl.when(c + 2 < nch)
            def _():
                gather(c + 2, buf0, 0).start()

            gather(c + 1, buf1, 1).wait()
            write(c + 1, buf1)

    return pl.kernel(
        body,
        out_type=jax.ShapeDtypeStruct((n_idx_rows * chunk, width), ys.dtype),
        mesh=_sc_mesh(),
        scratch_types=[
            pltpu.VMEM((nch, chunk), I32),
            pltpu.VMEM((chunk, width), ys.dtype),
            pltpu.VMEM((chunk, width), ys.dtype),
            pltpu.SemaphoreType.DMA((2,)),
        ],
        compiler_params=_sc_params(),
        name="sc_combine",
    )(ys, dest2d)


def _pack_bf16_pairs(v):
    half = v.shape[1] // 2
    lo = pltpu.bitcast(v[:, :half].astype(BF16).astype(F32), jnp.uint32) >> 16
    hi = pltpu.bitcast(v[:, half:].astype(BF16).astype(F32), jnp.uint32) & jnp.uint32(0xFFFF0000)
    return pltpu.bitcast(hi | lo, I32)


def _unpack_bf16_pairs(w):
    w = pltpu.bitcast(w, jnp.uint32)
    lo = pltpu.bitcast(w << 16, F32)
    hi = pltpu.bitcast(w & jnp.uint32(0xFFFF0000), F32)
    return jnp.concatenate([lo, hi], axis=1)


def _ffn_kernel(be_ref, nv_ref, nu_ref, nxt_ref, xs_ref, wg_hbm, wu_hbm, wd_hbm, o_ref,
                wg_f, wu_f, wd_f, wg_s, wu_s, wd_s, sem):
    i = pl.program_id(0)

    def fetch(e):
        return (pltpu.make_async_copy(wg_hbm.at[e], wg_f, sem.at[0]),
                pltpu.make_async_copy(wu_hbm.at[e], wu_f, sem.at[1]),
                pltpu.make_async_copy(wd_hbm.at[e], wd_f, sem.at[2]))

    @pl.when(nxt_ref[i] >= 0)
    def _():
        @pl.when(i == 0)
        def _():
            for cp in fetch(be_ref[0]):
                cp.start()

        for cp in fetch(be_ref[i]):
            cp.wait()
        wg_s[...] = wg_f[...].astype(BF16)
        wu_s[...] = wu_f[...].astype(BF16)
        wd_s[...] = wd_f[...].astype(BF16)

        @pl.when(nxt_ref[i] < N_EXPERTS)
        def _():
            for cp in fetch(nxt_ref[i]):
                cp.start()

    @pl.when(i < nu_ref[0])
    def _():
        live = lax.broadcasted_iota(I32, xs_ref.shape, 0) < nv_ref[i]
        xb = _unpack_bf16_pairs(jnp.where(live, xs_ref[...], 0)).astype(BF16)
        act = jax.nn.silu(_dot(xb, wg_s[...])) * _dot(xb, wu_s[...])
        o_ref[...] = _pack_bf16_pairs(_dot(act.astype(BF16), wd_s[...]))


def _expert_ffn(xs, blk_expert, blk_valid, n_used, blk_next, w_gate, w_up, w_down):
    n_rows, half = xs.shape
    d = 2 * half
    de = w_gate.shape[2]
    bm = EXPERT_ROWS
    rows = lambda i, be, nv, nu, nx: (jnp.minimum(i, nu[0] - 1), 0)
    hbm = pl.BlockSpec(memory_space=pl.ANY)
    return pl.pallas_call(
        _ffn_kernel,
        grid_spec=pltpu.PrefetchScalarGridSpec(
            num_scalar_prefetch=4,
            grid=(n_rows // bm,),
            in_specs=[pl.BlockSpec((bm, half), rows), hbm, hbm, hbm],
            out_specs=pl.BlockSpec((bm, half), rows),
            scratch_shapes=[
                pltpu.VMEM((d, de), F32), pltpu.VMEM((d, de), F32), pltpu.VMEM((de, d), F32),
                pltpu.VMEM((d, de), BF16), pltpu.VMEM((d, de), BF16), pltpu.VMEM((de, d), BF16),
                pltpu.SemaphoreType.DMA((3,)),
            ],
        ),
        out_shape=jax.ShapeDtypeStruct((n_rows, half), I32),
        compiler_params=_tc_params(1),
        name="expert_ffn",
    )(blk_expert, blk_valid, n_used, blk_next, xs, w_gate, w_up, w_down)


def _combine_kernel(x_ref, y0_ref, y1_ref, rt_ref, g_ref, b_ref, o_ref):
    o_ref[...] = _expert_combine_ln(x_ref[...], y0_ref[...], y1_ref[...], rt_ref[...], g_ref[...], b_ref[...])


def _combine_kernel_into(x_ref, y0_ref, y1_ref, rt_ref, g_ref, b_ref, full_ref, o_ref):
    del full_ref
    _combine_kernel(x_ref, y0_ref, y1_ref, rt_ref, g_ref, b_ref, o_ref)


def _combine_ln(x2, y01, rt, g3, b3, into=None, row0=0, full_rows=None):
    t_all, d = x2.shape
    tq = TOKEN_TILE
    nt = t_all // tq
    blk0 = row0 // tq if full_rows else 0
    in_specs = [
        pl.BlockSpec((tq, d), lambda i: (i, 0)),
        pl.BlockSpec((tq, d // 2), lambda i: (i, 0)),
        pl.BlockSpec((tq, d // 2), lambda i: (i + nt, 0)),
        pl.BlockSpec((8, tq), lambda i: (0, i)),
        _const_spec((1, d)), _const_spec((1, d)),
    ]
    args = [x2, y01, y01, rt, g3, b3]
    body, aliases = _combine_kernel, {}
    if into is not None:
        in_specs.append(pl.BlockSpec(memory_space=pl.ANY))
        args.append(into)
        body, aliases = _combine_kernel_into, {len(args) - 1: 0}
    return pl.pallas_call(
        body,
        grid=(nt,),
        in_specs=in_specs,
        out_specs=pl.BlockSpec((tq, d), lambda i: (i + blk0, 0)),
        out_shape=jax.ShapeDtypeStruct((full_rows or t_all, d), F32),
        input_output_aliases=aliases,
        compiler_params=_tc_params(1),
        name="combine_ln",
    )(*args)


def _routing_tables(rt, counts):
    bm = EXPERT_ROWS
    t_all = rt.shape[1]
    n_rows = 2 * t_all + N_EXPERTS * bm
    cnt = counts[:, 0]
    padded = (cnt + bm - 1) // bm * bm
    ends = jnp.cumsum(padded)
    offs = ends - padded
    experts = rt[0:2]
    off_tok = jnp.sum(jnp.where(experts[None] == jnp.arange(N_EXPERTS, dtype=I32)[:, None, None],
                                offs[:, None, None], 0), axis=0)
    dest2d = (off_tok + rt[2:4]).reshape(-1, SC_CHUNK).astype(I32)
    blk_start = jnp.arange(n_rows // bm, dtype=I32) * bm
    blk_expert = jnp.minimum(jnp.sum(blk_start[:, None] >= ends[None, :], axis=1), N_EXPERTS - 1).astype(I32)
    live_end = jnp.sum(jnp.where(blk_expert[:, None] == jnp.arange(N_EXPERTS, dtype=I32)[None, :],
                                 (offs + cnt)[None, :], 0), axis=1)
    blk_valid = jnp.clip(live_end - blk_start, 0, bm).astype(I32)
    n_used = (ends[-1:] // bm).astype(I32)
    eid = jnp.arange(N_EXPERTS, dtype=I32)
    later_present = jnp.logical_and(eid[None, :] > eid[:, None], (cnt > 0)[None, :])
    next_present = jnp.min(jnp.where(later_present, eid[None, :], N_EXPERTS), axis=1)
    next_of_blk = jnp.sum(jnp.where(blk_expert[:, None] == eid[None, :], next_present[None, :], 0), axis=1)
    prev_expert = jnp.concatenate([jnp.full((1,), -1, I32), blk_expert[:-1]])
    is_first = jnp.logical_and(blk_start < ends[-1], blk_expert != prev_expert)
    blk_next = jnp.where(is_first, next_of_blk, -1).astype(I32)
    return dest2d, blk_expert, blk_valid, n_used, blk_next, n_rows


def _layer_tail(x1, kv, mem_len, p, router, b0, bsz, s_len):
    rw_hi, rw_lo, rbias = router
    x2, xp, rt, counts = _xattn_route(x1, kv, mem_len, p["xq"], p["xo"], p["ln2_g"], p["ln2_b"],
                                      rw_hi, rw_lo, rbias, b0, bsz, s_len)
    dest2d, blk_expert, blk_valid, n_used, blk_next, n_rows = _routing_tables(rt, counts)
    xs = _sc_dispatch(xp, dest2d, n_rows)
    ys = _expert_ffn(xs, blk_expert, blk_valid, n_used, blk_next, p["e_gate"], p["e_up"], p["e_down"])
    y01 = _sc_combine(ys, dest2d)
    return x2, y01, rt


def _row(v):
    return v.reshape(1, -1).astype(F32)


def _common_params(xq, xkv, xo, ln2_g, ln2_b, e_gate, e_up, e_down, ln3_g, ln3_b):
    return dict(xq=xq, xkv=xkv, xo=xo, ln2_g=_row(ln2_g), ln2_b=_row(ln2_b),
                e_gate=e_gate, e_up=e_up, e_down=e_down, ln3_g=_row(ln3_g), ln3_b=_row(ln3_b))


def kernel(x, mem, positions, router_w, router_bias, l0_w_in, l0_sinks, l0_sgu_ln_g, l0_sgu_ln_b, l0_sgu_w, l0_sgu_b, l0_w_out, l0_ln1_g, l0_ln1_b, l0_xq, l0_xkv, l0_xo, l0_ln2_g, l0_ln2_b, l0_e_gate, l0_e_up, l0_e_down, l0_ln3_g, l0_ln3_b, l1_w_in, l1_pool_w, l1_pool_scale, l1_w_out, l1_ln1_g, l1_ln1_b, l1_xq, l1_xkv, l1_xo, l1_ln2_g, l1_ln2_b, l1_e_gate, l1_e_up, l1_e_down, l1_ln3_g, l1_ln3_b):
    bsz, s_len, d = x.shape
    assert s_len % TOKEN_TILE == 0 and s_len % MIXER0_TILE == 0 and MIXER0_TILE % BLOCK == 0
    xt = x.reshape(bsz * s_len, d)
    mem2d = mem.reshape(-1, d)

    rw_t = router_w.T.astype(F32)
    rw_hi = rw_t.astype(BF16)
    rw_lo = (rw_t - rw_hi.astype(F32)).astype(BF16)
    router = (rw_hi, rw_lo, router_bias.reshape(-1, 1).astype(F32))

    half = ROPE_DIM // 2
    inv_freq = (ROPE_THETA ** (-(jnp.arange(half, dtype=F32) * 2.0 / ROPE_DIM))).reshape(half, 1)
    etab_np = np.zeros((128, 3 * 128), np.float32)
    cbase_np = np.ones((1, 128), np.float32)
    for ln in range(128):
        dd = ln % HEAD_DIM
        if dd < ROPE_DIM:
            cbase_np[0, ln] = 0.0
            etab_np[[dd % half, half + dd % half], ln] = 1.0
            if dd >= half:
                etab_np[[2 * half + dd - half, 3 * half + dd - half], 128 + ln] = 1.0
            else:
                etab_np[[2 * half + dd, 3 * half + dd], 256 + ln] = -1.0
    etab = jnp.asarray(etab_np, BF16)
    cbase = jnp.asarray(cbase_np)
    pos_row = positions.reshape(1, -1).astype(I32)
    grp = jnp.arange(PIECE_COLS) // B_GROUP_DIM
    gsum = (grp[:, None] == grp[None, :]).astype(BF16)
    bs_full = jnp.repeat(l0_sgu_b.T.astype(F32), B_GROUP_DIM, axis=1)

    p0 = _common_params(l0_xq, l0_xkv, l0_xo, l0_ln2_g, l0_ln2_b, l0_e_gate, l0_e_up, l0_e_down,
                        l0_ln3_g, l0_ln3_b)
    p1 = _common_params(l1_xq, l1_xkv, l1_xo, l1_ln2_g, l1_ln2_b, l1_e_gate, l1_e_up, l1_e_down,
                        l1_ln3_g, l1_ln3_b)
    kv0 = _kv_proj(mem2d, p0["xkv"], bsz)
    kv1 = _kv_proj(mem2d, p1["xkv"], bsz)
    mem_len = mem.shape[1]

    n_split = BATCH_SPLIT if bsz % BATCH_SPLIT == 0 else 1
    nb = bsz // n_split
    out = None
    for part in range(n_split):
        b0 = part * nb
        x1 = _mixer0(xt, pos_row, l0_sinks.astype(F32), l0_w_in, inv_freq, etab, cbase, gsum,
                     _row(l0_sgu_ln_g), _row(l0_sgu_ln_b), l0_sgu_w.astype(F32), bs_full,
                     l0_w_out, _row(l0_ln1_g), _row(l0_ln1_b), b0, nb, s_len)
        x2, y01, rt = _layer_tail(x1, kv0, mem_len, p0, router, b0, nb, s_len)
        x1 = _mixer1(x2, y01, rt, p0["ln3_g"], p0["ln3_b"], l1_w_in, l1_pool_w, _row(l1_pool_scale),
                     l1_w_out, _row(l1_ln1_g), _row(l1_ln1_b), nb, s_len)
        x2, y01, rt = _layer_tail(x1, kv1, mem_len, p1, router, b0, nb, s_len)
        out = _combine_ln(x2, y01, rt, p1["ln3_g"], p1["ln3_b"], into=out, row0=b0 * s_len,
                          full_rows=bsz * s_len)
    return out.reshape(bsz, s_len, d)
```

```python
import functools

import numpy as np
import jax
import jax.numpy as jnp
from jax import lax
from jax.experimental import pallas as pl
from jax.experimental.pallas import tpu as pltpu
from jax.experimental.pallas import tpu_sc as plsc

F32 = jnp.float32
BF16 = jnp.bfloat16
I32 = jnp.int32

DEPTH = 2
ALPHA = (2.0 * DEPTH) ** 0.25
LN_EPS = 1e-5

HEAD_DIM = 64
A_Q_HEADS = 8
A_KV_HEADS = 2
A_GROUP = A_Q_HEADS // A_KV_HEADS
BLOCK = 128
ROPE_THETA = 500000.0
ROPE_DIM = HEAD_DIM // 4
A_WIDTH = A_Q_HEADS * HEAD_DIM
KV_WIDTH = A_KV_HEADS * HEAD_DIM
B_GROUPS = 8
B_GROUP_DIM = 64
B_WIDTH = B_GROUPS * B_GROUP_DIM
POOL_WINDOWS = (2, 4, 8, 16)
POOL_HALO = 16
X_HEADS = 4
N_EXPERTS = 16
N_EXPERT_GROUPS = 4
EXPERTS_PER_GROUP = 4

LANES = 128
TOKEN_TILE = 1024
SLAB_ROWS = 512
PIECE_COLS = 256
EXPERT_ROWS = 512
BATCH_SPLIT = 2
SC_WORKERS = 32
SC_CHUNK = 64
VMEM_LIMIT = 56 * 1024 * 1024
NEG_BIG = -1e30


def _layer_norm(z, g, b):
    mu = jnp.mean(z, axis=-1, keepdims=True)
    d = z - mu
    var = jnp.mean(d * d, axis=-1, keepdims=True)
    return d * lax.rsqrt(var + LN_EPS) * g + b


def _dot(a, b):
    return jnp.dot(a, b, preferred_element_type=F32)


def _dot_nt(a, b):
    return lax.dot_general(a, b, (((1,), (1,)), ((), ())), preferred_element_type=F32)


def _split_bf16(v):
    hi = v.astype(BF16)
    lo = (v - hi.astype(F32)).astype(BF16)
    return hi, lo


def _tc_params(n_axes):
    return pltpu.CompilerParams(dimension_semantics=("arbitrary",) * n_axes,
                                vmem_limit_bytes=VMEM_LIMIT)


def _const_spec(shape):
    nd = len(shape)
    return pl.BlockSpec(shape, lambda *_: (0,) * nd, pipeline_mode=pl.Buffered(1))


def _mixer0_kernel(sinks_ref, x_ref, pos_ref, win_ref, invf_ref, etab_ref, cbase_ref, gsum_ref,
                   lng_ref, lnb_ref, ws_ref, bs_ref, wout_ref, g1_ref, b1_ref,
                   o_ref, q_s, kv_s, u_s, vn_s, mix_s, wt_s, win_s, wout_s):
    b = pl.program_id(0)
    j = pl.program_id(1)
    tq = x_ref.shape[0]
    kvw = kv_s.shape[1]

    @pl.when(jnp.logical_and(b == 0, j == 0))
    def _():
        win_s[...] = win_ref[...].astype(BF16)
        wout_s[...] = wout_ref[...].astype(BF16)
        r = lax.broadcasted_iota(I32, (BLOCK, BLOCK), 0)
        c = lax.broadcasted_iota(I32, (BLOCK, BLOCK), 1)
        for g in range(B_GROUPS):
            wt_s[g] = jnp.where(c <= r, ws_ref[g], 0.0).astype(BF16)

    @pl.when(j == 0)
    def _():
        kv_s[0:BLOCK, :] = jnp.zeros((BLOCK, kvw), BF16)

    c1 = A_WIDTH
    c2 = c1 + KV_WIDTH
    c3 = c2 + KV_WIDTH
    c4 = c3 + B_WIDTH

    def rotary_tables(rows):
        n = rows.stop - rows.start
        ang = invf_ref[...] * pos_ref[:, rows].astype(F32)
        c8 = jnp.cos(ang)
        s8 = jnp.sin(ang)
        c8h = c8.astype(BF16).astype(F32)
        s8h = s8.astype(BF16).astype(F32)
        pad = jnp.zeros((LANES - 4 * c8.shape[0], n), F32)
        stack = jnp.concatenate([c8h, c8 - c8h, s8h, s8 - s8h, pad], axis=0)
        tabs = _dot(stack.T.astype(BF16), etab_ref[...])
        return tabs[:, 0:LANES] + cbase_ref[...], tabs[:, LANES:2 * LANES], tabs[:, 2 * LANES:]

    pw = PIECE_COLS
    n_pieces = win_s.shape[1] // pw
    assert (c1 // pw, c2 // pw, c3 // pw, c4 // pw) == (2, 2, 3, 5) and c3 % pw == 0 and n_pieces == 7

    def project(xb, k):
        return _dot(xb, win_s[:, k * pw:(k + 1) * pw])

    def prepare(rows, hk, tables, k):
        n = rows.stop - rows.start
        cs, sa, sb = tables

        def rope(t):
            return t * cs + pltpu.roll(t, ROPE_DIM // 2, 1) * sa + pltpu.roll(t, LANES - ROPE_DIM // 2, 1) * sb

        if k < 2:
            for c in range(pw // LANES):
                t = hk[:, c * LANES:(c + 1) * LANES] * (HEAD_DIM ** -0.5)
                col = k * (pw // LANES) + c
                q_s[rows, col * LANES:(col + 1) * LANES] = rope(t).astype(BF16)
        elif k == 2:
            low = lax.broadcasted_iota(I32, (n, LANES), 1) < HEAD_DIM
            kr = rope(hk[:, 0:KV_WIDTH])
            kx = pltpu.roll(kr, HEAD_DIM, 1)
            vr = hk[:, KV_WIDTH:]
            vx = pltpu.roll(vr, HEAD_DIM, 1)
            kv_cols = [jnp.where(low, kr, kx), jnp.where(low, kx, kr),
                       jnp.where(low, vr, 0.0), jnp.where(low, 0.0, vx),
                       jnp.where(low, vx, 0.0), jnp.where(low, 0.0, vr)]
            for c, col in enumerate(kv_cols):
                kv_s[BLOCK + rows.start:BLOCK + rows.stop, c * LANES:(c + 1) * LANES] = col.astype(BF16)
        elif k < 5:
            lo = (k - 3) * pw
            u_s[rows, lo:lo + pw] = jax.nn.gelu(hk)
        else:
            lo = (k - 5) * pw
            v = jax.nn.gelu(hk)
            gsum = gsum_ref[...]
            mean = _dot(v.astype(BF16), gsum) * (1.0 / B_GROUP_DIM)
            d = v - mean
            var = _dot((d * d).astype(BF16), gsum) * (1.0 / B_GROUP_DIM)
            vn_s[rows, lo:lo + pw] = (d * lax.rsqrt(var + LN_EPS) * lng_ref[:, lo:lo + pw]
                                      + lnb_ref[:, lo:lo + pw]).astype(BF16)

    qi = lax.broadcasted_iota(I32, (BLOCK, 2 * BLOCK), 0)
    kj = lax.broadcasted_iota(I32, (BLOCK, 2 * BLOCK), 1)
    rel = qi + BLOCK - kj
    band = jnp.logical_and(rel >= 0, rel < BLOCK)
    low_q = lax.broadcasted_iota(I32, (BLOCK, LANES), 1) < HEAD_DIM
    low_k = lax.broadcasted_iota(I32, (2 * BLOCK, LANES), 1) < HEAD_DIM
    ones_lo = jnp.where(low_k, 1.0, 0.0).astype(BF16)
    ones_hi = jnp.where(low_k, 0.0, 1.0).astype(BF16)
    zero_q = jnp.zeros((BLOCK, LANES), BF16)

    def block_body(n):
        r0 = n * BLOCK
        kv = kv_s[pl.ds(r0, 2 * BLOCK), :]
        qb = q_s[pl.ds(r0, BLOCK), :]
        valid = jnp.logical_and(band, kj >= jnp.where(j == 0, BLOCK, 0)) if n == 0 else band
        cols_per_kv = A_GROUP // 2
        scores = {}
        for hk in range(A_KV_HEADS):
            cols = range(hk * cols_per_kv, (hk + 1) * cols_per_kv)
            pieces = []
            for c in cols:
                qp = qb[:, c * LANES:(c + 1) * LANES]
                pieces += [jnp.where(low_q, qp, zero_q), jnp.where(low_q, zero_q, qp)]
            sc = _dot_nt(jnp.concatenate(pieces, axis=0), kv[:, hk * LANES:(hk + 1) * LANES])
            for i, c in enumerate(cols):
                for half in range(2):
                    r = (2 * i + half) * BLOCK
                    scores[c, half] = sc[r:r + BLOCK, :]
        vnb = vn_s[pl.ds(r0, BLOCK), :]
        parts = []
        for c in range(B_WIDTH // LANES):
            vp = vnb[:, c * LANES:(c + 1) * LANES]
            parts.append(_dot(wt_s[2 * c], jnp.where(low_q, vp, zero_q))
                         + _dot(wt_s[2 * c + 1], jnp.where(low_q, zero_q, vp)))
        probs, esink = {}, {}
        for (c, half), sc in scores.items():
            s = jnp.where(valid, sc, NEG_BIG)
            sink = sinks_ref[2 * c + half]
            m = jnp.maximum(jnp.max(s, axis=-1, keepdims=True), sink)
            probs[c, half] = jnp.exp(s - m).astype(BF16)
            esink[c, half] = jnp.exp(sink - m)
        res = {}
        for hk in range(A_KV_HEADS):
            cols = range(hk * cols_per_kv, (hk + 1) * cols_per_kv)
            for half in range(2):
                vcol = kv[:, (2 + 2 * hk + half) * LANES:(3 + 2 * hk + half) * LANES]
                vm = jnp.concatenate([vcol, ones_lo if half == 0 else ones_hi], axis=1)
                pv = _dot(jnp.concatenate([probs[c, half] for c in cols], axis=0), vm)
                for i, c in enumerate(cols):
                    part = pv[i * BLOCK:(i + 1) * BLOCK, :]
                    res[c] = part if half == 0 else res[c] + part
        for c in range(A_WIDTH // LANES):
            den = res[c][:, LANES:] + jnp.where(low_q, esink[c, 0], esink[c, 1])
            mix_s[pl.ds(r0, BLOCK), c * LANES:(c + 1) * LANES] = (res[c][:, :LANES] / den).astype(BF16)
        mixed = jnp.concatenate(parts, axis=1) + bs_ref[...]
        mix_s[pl.ds(r0, BLOCK), A_WIDTH:] = (u_s[pl.ds(r0, BLOCK), :] * mixed).astype(BF16)

    def out_cols(rows, c):
        return _dot(mix_s[rows, :], wout_s[:, c * pw:(c + 1) * pw])

    def finish(rows, z_cols, i):
        r = slice(rows.start + i * BLOCK, rows.start + (i + 1) * BLOCK)
        z = ALPHA * x_ref[r, :] + jnp.concatenate([zc[i * BLOCK:(i + 1) * BLOCK, :] for zc in z_cols], axis=1)
        o_ref[r, :] = _layer_norm(z, g1_ref[...], b1_ref[...])

    assert tq == 2 * SLAB_ROWS and SLAB_ROWS == 4 * BLOCK and wout_s.shape[1] == 4 * pw
    sa, sb = slice(0, SLAB_ROWS), slice(SLAB_ROWS, tq)
    tab_a, tab_b = rotary_tables(sa), rotary_tables(sb)
    xa = x_ref[sa, :].astype(BF16)
    ha = [project(xa, k) for k in range(n_pieces)]
    xb = x_ref[sb, :].astype(BF16)
    hb = []
    for k in range(n_pieces):
        hb.append(project(xb, k))
        prepare(sa, ha[k], tab_a, k)
    pieces_b = iter(range(n_pieces))
    for n in range(4):
        block_body(n)
        for k in [next(pieces_b) for _ in range(2 if n < 3 else 1)]:
            prepare(sb, hb[k], tab_b, k)
    za = []
    for n in range(4):
        block_body(4 + n)
        za.append(out_cols(sa, n))
    kv_s[0:BLOCK, :] = kv_s[tq:tq + BLOCK, :]
    zb = []
    for n in range(4):
        zb.append(out_cols(sb, n))
        finish(sa, za, n)
    for n in range(4):
        finish(sb, zb, n)


def _mixer0(x, pos_row, sinks, w_in, invf, etab, cbase, gsum, lng, lnb, w_s, bs_full, w_out, g1, b1,
            b0, bsz, s_len):
    d = x.shape[1]
    t_all = bsz * s_len
    tq = TOKEN_TILE
    nj = s_len // tq
    row = lambda bb, jj: (bb * nj + jj, 0)
    in_w = w_in.shape[1]
    return pl.pallas_call(
        _mixer0_kernel,
        grid=(bsz, nj),
        in_specs=[
            pl.BlockSpec(memory_space=pltpu.SMEM),
            pl.BlockSpec((tq, d), lambda bb, jj: ((b0 + bb) * nj + jj, 0)),
            pl.BlockSpec((1, tq), lambda bb, jj: (0, (b0 + bb) * nj + jj)),
            _const_spec((d, in_w)),
            _const_spec((ROPE_DIM // 2, 1)), _const_spec((LANES, 3 * LANES)), _const_spec((1, LANES)),
            _const_spec((PIECE_COLS, PIECE_COLS)),
            _const_spec((1, B_WIDTH)), _const_spec((1, B_WIDTH)),
            _const_spec((B_GROUPS, BLOCK, BLOCK)),
            _const_spec((BLOCK, B_WIDTH)),
            _const_spec((A_WIDTH + B_WIDTH, d)),
            _const_spec((1, d)), _const_spec((1, d)),
        ],
        out_specs=pl.BlockSpec((tq, d), row),
        out_shape=jax.ShapeDtypeStruct((t_all, d), F32),
        scratch_shapes=[
            pltpu.VMEM((tq, A_WIDTH), BF16),
            pltpu.VMEM((tq + BLOCK, 6 * LANES), BF16),
            pltpu.VMEM((tq, B_WIDTH), F32),
            pltpu.VMEM((tq, B_WIDTH), BF16),
            pltpu.VMEM((tq, A_WIDTH + B_WIDTH), BF16),
            pltpu.VMEM((B_GROUPS, BLOCK, BLOCK), BF16),
            pltpu.VMEM((d, in_w), BF16),
            pltpu.VMEM((A_WIDTH + B_WIDTH, d), BF16),
        ],
        compiler_params=_tc_params(2),
        name="mixer0",
    )(sinks, x, pos_row, w_in, invf, etab, cbase, gsum, lng, lnb, w_s, bs_full, w_out, g1, b1)


def _expert_combine_ln(x2, y0_packed, y1_packed, rt, g, b):
    wt = pltpu.bitcast(rt, F32).T
    y = wt[:, 4:5] * _unpack_bf16_pairs(y0_packed) + wt[:, 5:6] * _unpack_bf16_pairs(y1_packed)
    return _layer_norm(ALPHA * x2 + y, g, b)


def _mixer1_kernel(x2_ref, y0_ref, y1_ref, rt_ref, g3_ref, b3_ref,
                   win_ref, pw_ref, ps_ref, wout_ref, g1_ref, b1_ref, o_ref,
                   h_s, mp_s, win_s, pw_s, wout_s):
    j = pl.program_id(1)
    tq = x2_ref.shape[0]
    gw = x2_ref.shape[1] // len(POOL_WINDOWS)
    slabs = [slice(c * SLAB_ROWS, (c + 1) * SLAB_ROWS) for c in range(tq // SLAB_ROWS)]

    @pl.when(jnp.logical_and(pl.program_id(0) == 0, j == 0))
    def _():
        win_s[...] = win_ref[...].astype(BF16)
        pw_s[...] = pw_ref[...].astype(BF16)
        wout_s[...] = wout_ref[...].astype(BF16)

    @pl.when(j == 0)
    def _():
        h_s[0:POOL_HALO, :] = jnp.zeros((POOL_HALO, h_s.shape[1]), F32)

    ng = len(POOL_WINDOWS)
    quarter = SLAB_ROWS // ng

    def sub(rows, i):
        return slice(rows.start + i * quarter, rows.start + (i + 1) * quarter)

    def load_in(rows, i):
        r = sub(rows, i)
        return _expert_combine_ln(x2_ref[r, :], y0_ref[r, :], y1_ref[r, :], rt_ref[:, r],
                                  g3_ref[...], b3_ref[...])

    def project(rows, xb, g):
        lo, hi = g * gw, (g + 1) * gw
        h_s[POOL_HALO + rows.start:POOL_HALO + rows.stop, lo:hi] = _dot(xb, win_s[:, lo:hi])

    def pool(rows, g):
        win = POOL_WINDOWS[g]
        lo, hi = g * gw, (g + 1) * gw
        n = rows.stop - rows.start
        t_pos = j * tq + rows.start + lax.broadcasted_iota(I32, (n, 1), 0)
        ext = h_s[rows.start:rows.stop + POOL_HALO, lo:hi]
        acc = ext
        shift = 1
        while shift < win:
            acc = acc + pltpu.roll(acc, shift, 0)
            shift *= 2
        count = jnp.minimum(t_pos + 1, win).astype(F32)
        pooled = acc[POOL_HALO:, :] / count - ext[POOL_HALO:, :]
        mapped = _dot(pooled.astype(BF16), pw_s[g])
        mp_s[rows, lo:hi] = (mapped * ps_ref[:, lo:hi]).astype(BF16)

    def out_cols(rows, g):
        return _dot(mp_s[rows, :], wout_s[:, g * gw:(g + 1) * gw])

    def finish(rows, x_parts, z_cols, i):
        lo, hi = i * quarter, (i + 1) * quarter
        z = ALPHA * x_parts[i] + jnp.concatenate([zc[lo:hi, :] for zc in z_cols], axis=1)
        o_ref[sub(rows, i), :] = _layer_norm(z, g1_ref[...], b1_ref[...])

    assert len(slabs) == 2
    sa, sb = slabs
    xa = [load_in(sa, i) for i in range(ng)]
    xa_b = jnp.concatenate(xa, axis=0).astype(BF16)
    xb = []
    for g in range(ng):
        project(sa, xa_b, g)
        xb.append(load_in(sb, g))
    xb_b = jnp.concatenate(xb, axis=0).astype(BF16)
    for g in range(ng):
        project(sb, xb_b, g)
        pool(sa, g)
    za = []
    for g in range(ng):
        za.append(out_cols(sa, g))
        pool(sb, g)
    zb = []
    for g in range(ng):
        zb.append(out_cols(sb, g))
        finish(sa, xa, za, g)
    for g in range(ng):
        finish(sb, xb, zb, g)
    h_s[0:POOL_HALO, :] = h_s[tq:tq + POOL_HALO, :]


def _mixer1(x2, y01, rt, g3, b3, w_in, pool_w, pool_scale, w_out, g1, b1, bsz, s_len):
    t_all, d = x2.shape
    tq = TOKEN_TILE
    nj = s_len // tq
    nt = bsz * nj
    row = lambda bb, jj: (bb * nj + jj, 0)
    ng = len(POOL_WINDOWS)
    return pl.pallas_call(
        _mixer1_kernel,
        grid=(bsz, nj),
        in_specs=[
            pl.BlockSpec((tq, d), row),
            pl.BlockSpec((tq, d // 2), row),
            pl.BlockSpec((tq, d // 2), lambda bb, jj: (nt + bb * nj + jj, 0)),
            pl.BlockSpec((8, tq), lambda bb, jj: (0, bb * nj + jj)),
            _const_spec((1, d)), _const_spec((1, d)),
            _const_spec((d, d)),
            _const_spec((ng, d // ng, d // ng)),
            _const_spec((1, d)),
            _const_spec((d, d)),
            _const_spec((1, d)), _const_spec((1, d)),
        ],
        out_specs=pl.BlockSpec((tq, d), row),
        out_shape=jax.ShapeDtypeStruct((t_all, d), F32),
        scratch_shapes=[pltpu.VMEM((tq + POOL_HALO, d), F32), pltpu.VMEM((tq, d), BF16),
                        pltpu.VMEM((d, d), BF16), pltpu.VMEM((ng, d // ng, d // ng), BF16),
                        pltpu.VMEM((d, d), BF16)],
        compiler_params=_tc_params(2),
        name="mixer1",
    )(x2, y01, y01, rt, g3, b3, w_in, pool_w, pool_scale, w_out, g1, b1)


def _kv_kernel(mem_ref, w_ref, o_ref, w_s):
    @pl.when(pl.program_id(0) == 0)
    def _():
        w_s[...] = w_ref[...].astype(BF16)

    o_ref[...] = _dot(mem_ref[...].astype(BF16), w_s[...]).astype(BF16)


def _kv_proj(mem2d, wkv, bsz):
    rows, d = mem2d.shape
    m = rows // bsz
    return pl.pallas_call(
        _kv_kernel,
        grid=(bsz,),
        in_specs=[pl.BlockSpec((m, d), lambda i: (i, 0)), _const_spec(wkv.shape)],
        out_specs=pl.BlockSpec((m, wkv.shape[1]), lambda i: (i, 0)),
        out_shape=jax.ShapeDtypeStruct((rows, wkv.shape[1]), BF16),
        scratch_shapes=[pltpu.VMEM(wkv.shape, BF16)],
        compiler_params=_tc_params(1),
        name="kv_proj",
    )(mem2d, wkv)


def _top2_of4(v):
    hi01, lo01 = jnp.maximum(v[0], v[1]), jnp.minimum(v[0], v[1])
    hi23, lo23 = jnp.maximum(v[2], v[3]), jnp.minimum(v[2], v[3])
    return jnp.maximum(hi01, hi23) + jnp.maximum(jnp.minimum(hi01, hi23), jnp.maximum(lo01, lo23))


def _argmax_first(vals):
    best, idx = vals[0], jnp.zeros(vals[0].shape, I32)
    for i in range(1, len(vals)):
        better = vals[i] > best
        best = jnp.where(better, vals[i], best)
        idx = jnp.where(better, i, idx)
    return best, idx


def _xattn_kernel(x_ref, kv_ref, wq_ref, wo_ref, g2_ref, b2_ref, rwh_ref, rwl_ref, rb_ref,
                  x2_ref, xp_ref, rt_ref, cnt_ref, run_s, wq_s, wo_s):
    first = jnp.logical_and(pl.program_id(0) == 0, pl.program_id(1) == 0)
    tq, d = x_ref.shape

    @pl.when(first)
    def _():
        run_s[...] = jnp.zeros(run_s.shape, F32)
        wq_s[...] = wq_ref[...].astype(BF16)
        wo_s[...] = wo_ref[...].astype(BF16)

    slabs = [slice(c * SLAB_ROWS, (c + 1) * SLAB_ROWS) for c in range(tq // SLAB_ROWS)]
    run = run_s[...]
    zs = [_xattn_attend(*_xattn_query(rows, x_ref, wq_s), kv_ref, wo_s) for rows in slabs]
    picks = []
    for rows, z in zip(slabs, zs):
        logits = _xattn_norm_logits(rows, z, g2_ref, b2_ref, rwh_ref, rwl_ref, x2_ref, xp_ref)
        picks.append(_xattn_select(logits, rb_ref))
    for rows, pick in zip(slabs, picks):
        run = _xattn_rank(rows, pick, run, rt_ref)
    run_s[...] = run
    cnt_ref[...] = jnp.broadcast_to(run, cnt_ref.shape).astype(I32)


def _xattn_query(rows, x_ref, wq_s):
    hd = x_ref.shape[1] // X_HEADS
    x = x_ref[rows, :]
    return x, _dot(x.astype(BF16), wq_s[...]) * (hd ** -0.5)


def _xattn_attend(x, q, kv_ref, wo_s):
    d = x.shape[1]
    hd = d // X_HEADS
    outs = []
    for hx in range(X_HEADS):
        qh = q[:, hx * hd:(hx + 1) * hd].astype(BF16)
        kh = kv_ref[:, hx * hd:(hx + 1) * hd]
        vh = kv_ref[:, d + hx * hd:d + (hx + 1) * hd]
        s = _dot_nt(qh, kh)
        p = jnp.exp(s - jnp.max(s, axis=-1, keepdims=True))
        o = _dot(p.astype(BF16), vh) / jnp.sum(p, axis=-1, keepdims=True)
        outs.append(o.astype(BF16))
    return ALPHA * x + _dot(jnp.concatenate(outs, axis=1), wo_s[...])


def _xattn_norm_logits(rows, z, g2_ref, b2_ref, rwh_ref, rwl_ref, x2_ref, xp_ref):
    x2 = _layer_norm(z, g2_ref[...], b2_ref[...])
    x2_ref[rows, :] = x2

    xp_ref[rows, :] = _pack_bf16_pairs(x2)

    xh, xl = _split_bf16(x2)
    both = _dot_nt(jnp.concatenate([rwh_ref[...], rwl_ref[...]], axis=0), xh)
    return both[0:N_EXPERTS] + both[N_EXPERTS:] + _dot_nt(rwh_ref[...], xl)


def _xattn_select(logits, rb_ref):
    e_max = jnp.max(logits, axis=0, keepdims=True)
    ex = jnp.exp(logits - e_max)
    scores = ex / jnp.sum(ex, axis=0, keepdims=True)
    biased = scores + rb_ref[...]
    sc = [scores[e:e + 1, :] for e in range(N_EXPERTS)]
    bi = [biased[e:e + 1, :] for e in range(N_EXPERTS)]
    epg = EXPERTS_PER_GROUP
    gscore = [_top2_of4(bi[g * epg:(g + 1) * epg]) for g in range(N_EXPERT_GROUPS)]
    _, gsel = _argmax_first(gscore)

    def pick(vals):
        return [functools.reduce(lambda acc, g: jnp.where(gsel == g, vals[g * epg + i], acc),
                                 range(1, N_EXPERT_GROUPS), vals[i]) for i in range(epg)]

    in_b = pick(bi)
    in_s = pick(sc)
    _, i0 = _argmax_first(in_b)
    _, i1 = _argmax_first([jnp.where(i0 == i, -jnp.inf, in_b[i]) for i in range(epg)])

    def take(vals, idx):
        return functools.reduce(lambda acc, i: jnp.where(idx == i, vals[i], acc), range(1, epg), vals[0])

    s0, s1 = take(in_s, i0), take(in_s, i1)
    w0, w1 = s0 / (s0 + s1), s1 / (s0 + s1)
    return gsel * epg + i0, gsel * epg + i1, w0, w1


def _xattn_rank(rows, selection, run, rt_ref):
    e0, e1, w0, w1 = selection
    tq = rows.stop - rows.start
    eid = lax.broadcasted_iota(I32, (N_EXPERTS, tq), 0)
    oh0 = eid == e0
    oh1 = eid == e1
    onehot = jnp.where(jnp.logical_or(oh0, oh1), 1.0, 0.0)
    rr = lax.broadcasted_iota(I32, (tq, tq), 0)
    cc = lax.broadcasted_iota(I32, (tq, tq), 1)
    upper = jnp.where(rr < cc, 1.0, 0.0).astype(BF16)
    prefix = _dot(onehot.astype(BF16), upper) + run
    r0 = jnp.sum(jnp.where(oh0, prefix, 0.0), axis=0, keepdims=True).astype(I32)
    r1 = jnp.sum(jnp.where(oh1, prefix, 0.0), axis=0, keepdims=True).astype(I32)

    zero = jnp.zeros((1, tq), I32)
    rt_ref[:, rows] = jnp.concatenate(
        [e0, e1, r0, r1, pltpu.bitcast(w0, I32), pltpu.bitcast(w1, I32), zero, zero], axis=0)
    return run + jnp.sum(onehot, axis=1, keepdims=True)


def _xattn_route(x1, kv, mem_len, wq, wo, g2, b2, rw_hi, rw_lo, rbias, b0, bsz, s_len):
    t_all, d = x1.shape
    tq = TOKEN_TILE
    nj = s_len // tq
    m = mem_len
    row = lambda bb, jj: (bb * nj + jj, 0)
    return pl.pallas_call(
        _xattn_kernel,
        grid=(bsz, nj),
        in_specs=[
            pl.BlockSpec((tq, d), row),
            pl.BlockSpec((m, 2 * d), lambda bb, jj: (b0 + bb, 0)),
            _const_spec((d, d)), _const_spec((d, d)),
            _const_spec((1, d)), _const_spec((1, d)),
            _const_spec((N_EXPERTS, d)), _const_spec((N_EXPERTS, d)),
            _const_spec((N_EXPERTS, 1)),
        ],
        out_specs=[
            pl.BlockSpec((tq, d), row),
            pl.BlockSpec((tq, d // 2), row),
            pl.BlockSpec((8, tq), lambda bb, jj: (0, bb * nj + jj)),
            _const_spec((N_EXPERTS, LANES)),
        ],
        out_shape=[
            jax.ShapeDtypeStruct((t_all, d), F32),
            jax.ShapeDtypeStruct((t_all, d // 2), I32),
            jax.ShapeDtypeStruct((8, t_all), I32),
            jax.ShapeDtypeStruct((N_EXPERTS, LANES), I32),
        ],
        scratch_shapes=[pltpu.VMEM((N_EXPERTS, 1), F32), pltpu.VMEM((d, d), BF16), pltpu.VMEM((d, d), BF16)],
        compiler_params=_tc_params(2),
        name="xattn_route",
    )(x1, kv, wq, wo, g2, b2, rw_hi, rw_lo, rbias)


def _sc_mesh():
    return plsc.VectorSubcoreMesh(core_axis_name="c", subcore_axis_name="s")


def _sc_params():
    return pltpu.CompilerParams(needs_layout_passes=False)


def _worker_id():
    return lax.axis_index("s") * lax.axis_size("c") + lax.axis_index("c")


def _sc_dispatch(xp, dest2d, n_rows):
    t_all, width = xp.shape
    chunk = dest2d.shape[1]
    tok_w = t_all // SC_WORKERS
    nch = tok_w // chunk
    slot1 = t_all // chunk
    assert t_all % (SC_WORKERS * chunk * 2) == 0

    def body(x_hbm, dest_hbm, out_hbm, idx0_v, idx1_v, buf0, buf1, sem_r, sem_w):
        wid = _worker_id()
        base = wid * tok_w
        pltpu.sync_copy(dest_hbm.at[pl.ds(wid * nch, nch)], idx0_v)
        pltpu.sync_copy(dest_hbm.at[pl.ds(slot1 + wid * nch, nch)], idx1_v)

        def read(c, buf, k):
            return pltpu.make_async_copy(x_hbm.at[pl.ds(base + c * chunk, chunk)], buf, sem_r.at[k])

        def scatter(c, buf):
            a = pltpu.make_async_copy(buf, out_hbm.at[idx0_v.at[c]], sem_w.at[0])
            b = pltpu.make_async_copy(buf, out_hbm.at[idx1_v.at[c]], sem_w.at[1])
            a.start()
            b.start()
            a.wait()
            b.wait()

        read(0, buf0, 0).start()

        @pl.loop(0, nch // 2)
        def _(g):
            c = 2 * g
            read(c + 1, buf1, 1).start()
            read(c, buf0, 0).wait()
            scatter(c, buf0)

            @pl.when(c + 2 < nch)
            def _():
                read(c + 2, buf0, 0).start()

            read(c + 1, buf1, 1).wait()
            scatter(c + 1, buf1)

    return pl.kernel(
        body,
        out_type=jax.ShapeDtypeStruct((n_rows, width), xp.dtype),
        mesh=_sc_mesh(),
        scratch_types=[
            pltpu.VMEM((nch, chunk), I32),
            pltpu.VMEM((nch, chunk), I32),
            pltpu.VMEM((chunk, width), xp.dtype),
            pltpu.VMEM((chunk, width), xp.dtype),
            pltpu.SemaphoreType.DMA((2,)),
            pltpu.SemaphoreType.DMA((2,)),
        ],
        compiler_params=_sc_params(),
        name="sc_dispatch",
    )(xp, dest2d)


def _sc_combine(ys, dest2d):
    n_idx_rows, chunk = dest2d.shape
    width = ys.shape[1]
    nch = n_idx_rows // SC_WORKERS
    assert n_idx_rows % (SC_WORKERS * 2) == 0

    def body(y_hbm, dest_hbm, out_hbm, idx_v, buf0, buf1, sem_g):
        wid = _worker_id()
        base = wid * nch * chunk
        pltpu.sync_copy(dest_hbm.at[pl.ds(wid * nch, nch)], idx_v)

        def gather(c, buf, k):
            return pltpu.make_async_copy(y_hbm.at[idx_v.at[c]], buf, sem_g.at[k])

        def write(c, buf):
            pltpu.sync_copy(buf, out_hbm.at[pl.ds(base + c * chunk, chunk)])

        gather(0, buf0, 0).start()

        @pl.loop(0, nch // 2)
        def _(g):
            c = 2 * g
            gather(c + 1, buf1, 1).start()
            gather(c, buf0, 0).wait()
            write(c, buf0)

            @pl.when(c + 2 < nch)
            def _():
                gather(c + 2, buf0, 0).start()

            gather(c + 1, buf1, 1).wait()
            write(c + 1, buf1)

    return pl.kernel(
        body,
        out_type=jax.ShapeDtypeStruct((n_idx_rows * chunk, width), ys.dtype),
        mesh=_sc_mesh(),
        scratch_types=[
            pltpu.VMEM((nch, chunk), I32),
            pltpu.VMEM((chunk, width), ys.dtype),
            pltpu.VMEM((chunk, width), ys.dtype),
            pltpu.SemaphoreType.DMA((2,)),
        ],
        compiler_params=_sc_params(),
        name="sc_combine",
    )(ys, dest2d)


def _pack_bf16_pairs(v):
    half = v.shape[1] // 2
    lo = pltpu.bitcast(v[:, :half].astype(BF16).astype(F32), jnp.uint32) >> 16
    hi = pltpu.bitcast(v[:, half:].astype(BF16).astype(F32), jnp.uint32) & jnp.uint32(0xFFFF0000)
    return pltpu.bitcast(hi | lo, I32)


def _unpack_bf16_pairs(w):
    w = pltpu.bitcast(w, jnp.uint32)
    lo = pltpu.bitcast(w << 16, F32)
    hi = pltpu.bitcast(w & jnp.uint32(0xFFFF0000), F32)
    return jnp.concatenate([lo, hi], axis=1)


def _ffn_kernel(be_ref, nv_ref, nu_ref, nxt_ref, xs_ref, wg_hbm, wu_hbm, wd_hbm, o_ref,
                wg_f, wu_f, wd_f, wg_s, wu_s, wd_s, sem):
    i = pl.program_id(0)

    def fetch(e):
        return (pltpu.make_async_copy(wg_hbm.at[e], wg_f, sem.at[0]),
                pltpu.make_async_copy(wu_hbm.at[e], wu_f, sem.at[1]),
                pltpu.make_async_copy(wd_hbm.at[e], wd_f, sem.at[2]))

    @pl.when(nxt_ref[i] >= 0)
    def _():
        @pl.when(i == 0)
        def _():
            for cp in fetch(be_ref[0]):
                cp.start()

        for cp in fetch(be_ref[i]):
            cp.wait()
        wg_s[...] = wg_f[...].astype(BF16)
        wu_s[...] = wu_f[...].astype(BF16)
        wd_s[...] = wd_f[...].astype(BF16)

        @pl.when(nxt_ref[i] < N_EXPERTS)
        def _():
            for cp in fetch(nxt_ref[i]):
                cp.start()

    @pl.when(i < nu_ref[0])
    def _():
        live = lax.broadcasted_iota(I32, xs_ref.shape, 0) < nv_ref[i]
        xb = _unpack_bf16_pairs(jnp.where(live, xs_ref[...], 0)).astype(BF16)
        act = jax.nn.silu(_dot(xb, wg_s[...])) * _dot(xb, wu_s[...])
        o_ref[...] = _pack_bf16_pairs(_dot(act.astype(BF16), wd_s[...]))


def _expert_ffn(xs, blk_expert, blk_valid, n_used, blk_next, w_gate, w_up, w_down):
    n_rows, half = xs.shape
    d = 2 * half
    de = w_gate.shape[2]
    bm = EXPERT_ROWS
    rows = lambda i, be, nv, nu, nx: (jnp.minimum(i, nu[0] - 1), 0)
    hbm = pl.BlockSpec(memory_space=pl.ANY)
    return pl.pallas_call(
        _ffn_kernel,
        grid_spec=pltpu.PrefetchScalarGridSpec(
            num_scalar_prefetch=4,
            grid=(n_rows // bm,),
            in_specs=[pl.BlockSpec((bm, half), rows), hbm, hbm, hbm],
            out_specs=pl.BlockSpec((bm, half), rows),
            scratch_shapes=[
                pltpu.VMEM((d, de), F32), pltpu.VMEM((d, de), F32), pltpu.VMEM((de, d), F32),
                pltpu.VMEM((d, de), BF16), pltpu.VMEM((d, de), BF16), pltpu.VMEM((de, d), BF16),
                pltpu.SemaphoreType.DMA((3,)),
            ],
        ),
        out_shape=jax.ShapeDtypeStruct((n_rows, half), I32),
        compiler_params=_tc_params(1),
        name="expert_ffn",
    )(blk_expert, blk_valid, n_used, blk_next, xs, w_gate, w_up, w_down)


def _combine_kernel(x_ref, y0_ref, y1_ref, rt_ref, g_ref, b_ref, o_ref):
    o_ref[...] = _expert_combine_ln(x_ref[...], y0_ref[...], y1_ref[...], rt_ref[...], g_ref[...], b_ref[...])


def _combine_kernel_into(x_ref, y0_ref, y1_ref, rt_ref, g_ref, b_ref, full_ref, o_ref):
    del full_ref
    _combine_kernel(x_ref, y0_ref, y1_ref, rt_ref, g_ref, b_ref, o_ref)


def _combine_ln(x2, y01, rt, g3, b3, into=None, row0=0, full_rows=None):
    t_all, d = x2.shape
    tq = TOKEN_TILE
    nt = t_all // tq
    blk0 = row0 // tq if full_rows else 0
    in_specs = [
        pl.BlockSpec((tq, d), lambda i: (i, 0)),
        pl.BlockSpec((tq, d // 2), lambda i: (i, 0)),
        pl.BlockSpec((tq, d // 2), lambda i: (i + nt, 0)),
        pl.BlockSpec((8, tq), lambda i: (0, i)),
        _const_spec((1, d)), _const_spec((1, d)),
    ]
    args = [x2, y01, y01, rt, g3, b3]
    body, aliases = _combine_kernel, {}
    if into is not None:
        in_specs.append(pl.BlockSpec(memory_space=pl.ANY))
        args.append(into)
        body, aliases = _combine_kernel_into, {len(args) - 1: 0}
    return pl.pallas_call(
        body,
        grid=(nt,),
        in_specs=in_specs,
        out_specs=pl.BlockSpec((tq, d), lambda i: (i + blk0, 0)),
        out_shape=jax.ShapeDtypeStruct((full_rows or t_all, d), F32),
        input_output_aliases=aliases,
        compiler_params=_tc_params(1),
        name="combine_ln",
    )(*args)


def _routing_tables(rt, counts):
    bm = EXPERT_ROWS
    t_all = rt.shape[1]
    n_rows = 2 * t_all + N_EXPERTS * bm
    cnt = counts[:, 0]
    padded = (cnt + bm - 1) // bm * bm
    ends = jnp.cumsum(padded)
    offs = ends - padded
    experts = rt[0:2]
    off_tok = jnp.sum(jnp.where(experts[None] == jnp.arange(N_EXPERTS, dtype=I32)[:, None, None],
                                offs[:, None, None], 0), axis=0)
    dest2d = (off_tok + rt[2:4]).reshape(-1, SC_CHUNK).astype(I32)
    blk_start = jnp.arange(n_rows // bm, dtype=I32) * bm
    blk_expert = jnp.minimum(jnp.sum(blk_start[:, None] >= ends[None, :], axis=1), N_EXPERTS - 1).astype(I32)
    live_end = jnp.sum(jnp.where(blk_expert[:, None] == jnp.arange(N_EXPERTS, dtype=I32)[None, :],
                                 (offs + cnt)[None, :], 0), axis=1)
    blk_valid = jnp.clip(live_end - blk_start, 0, bm).astype(I32)
    n_used = (ends[-1:] // bm).astype(I32)
    eid = jnp.arange(N_EXPERTS, dtype=I32)
    later_present = jnp.logical_and(eid[None, :] > eid[:, None], (cnt > 0)[None, :])
    next_present = jnp.min(jnp.where(later_present, eid[None, :], N_EXPERTS), axis=1)
    next_of_blk = jnp.sum(jnp.where(blk_expert[:, None] == eid[None, :], next_present[None, :], 0), axis=1)
    prev_expert = jnp.concatenate([jnp.full((1,), -1, I32), blk_expert[:-1]])
    is_first = jnp.logical_and(blk_start < ends[-1], blk_expert != prev_expert)
    blk_next = jnp.where(is_first, next_of_blk, -1).astype(I32)
    return dest2d, blk_expert, blk_valid, n_used, blk_next, n_rows


def _layer_tail(x1, kv, mem_len, p, router, b0, bsz, s_len):
    rw_hi, rw_lo, rbias = router
    x2, xp, rt, counts = _xattn_route(x1, kv, mem_len, p["xq"], p["xo"], p["ln2_g"], p["ln2_b"],
                                      rw_hi, rw_lo, rbias, b0, bsz, s_len)
    dest2d, blk_expert, blk_valid, n_used, blk_next, n_rows = _routing_tables(rt, counts)
    xs = _sc_dispatch(xp, dest2d, n_rows)
    ys = _expert_ffn(xs, blk_expert, blk_valid, n_used, blk_next, p["e_gate"], p["e_up"], p["e_down"])
    y01 = _sc_combine(ys, dest2d)
    return x2, y01, rt


def _row(v):
    return v.reshape(1, -1).astype(F32)


def _common_params(xq, xkv, xo, ln2_g, ln2_b, e_gate, e_up, e_down, ln3_g, ln3_b):
    return dict(xq=xq, xkv=xkv, xo=xo, ln2_g=_row(ln2_g), ln2_b=_row(ln2_b),
                e_gate=e_gate, e_up=e_up, e_down=e_down, ln3_g=_row(ln3_g), ln3_b=_row(ln3_b))


def kernel(x, mem, positions, router_w, router_bias, l0_w_in, l0_sinks, l0_sgu_ln_g, l0_sgu_ln_b, l0_sgu_w, l0_sgu_b, l0_w_out, l0_ln1_g, l0_ln1_b, l0_xq, l0_xkv, l0_xo, l0_ln2_g, l0_ln2_b, l0_e_gate, l0_e_up, l0_e_down, l0_ln3_g, l0_ln3_b, l1_w_in, l1_pool_w, l1_pool_scale, l1_w_out, l1_ln1_g, l1_ln1_b, l1_xq, l1_xkv, l1_xo, l1_ln2_g, l1_ln2_b, l1_e_gate, l1_e_up, l1_e_down, l1_ln3_g, l1_ln3_b):
    bsz, s_len, d = x.shape
    assert s_len % TOKEN_TILE == 0 and TOKEN_TILE == 2 * SLAB_ROWS and SLAB_ROWS % BLOCK == 0
    xt = x.reshape(bsz * s_len, d)
    mem2d = mem.reshape(-1, d)

    rw_t = router_w.T.astype(F32)
    rw_hi = rw_t.astype(BF16)
    rw_lo = (rw_t - rw_hi.astype(F32)).astype(BF16)
    router = (rw_hi, rw_lo, router_bias.reshape(-1, 1).astype(F32))

    half = ROPE_DIM // 2
    inv_freq = (ROPE_THETA ** (-(jnp.arange(half, dtype=F32) * 2.0 / ROPE_DIM))).reshape(half, 1)
    etab_np = np.zeros((LANES, 3 * LANES), np.float32)
    cbase_np = np.ones((1, LANES), np.float32)
    for ln in range(LANES):
        dd = ln % HEAD_DIM
        if dd < ROPE_DIM:
            cbase_np[0, ln] = 0.0
            etab_np[[dd % half, half + dd % half], ln] = 1.0
            if dd >= half:
                etab_np[[2 * half + dd - half, 3 * half + dd - half], LANES + ln] = 1.0
            else:
                etab_np[[2 * half + dd, 3 * half + dd], 2 * LANES + ln] = -1.0
    etab = jnp.asarray(etab_np, BF16)
    cbase = jnp.asarray(cbase_np)
    pos_row = positions.reshape(1, -1).astype(I32)
    grp = jnp.arange(PIECE_COLS) // B_GROUP_DIM
    gsum = (grp[:, None] == grp[None, :]).astype(BF16)
    bs_full = jnp.repeat(l0_sgu_b.T.astype(F32), B_GROUP_DIM, axis=1)

    p0 = _common_params(l0_xq, l0_xkv, l0_xo, l0_ln2_g, l0_ln2_b, l0_e_gate, l0_e_up, l0_e_down,
                        l0_ln3_g, l0_ln3_b)
    p1 = _common_params(l1_xq, l1_xkv, l1_xo, l1_ln2_g, l1_ln2_b, l1_e_gate, l1_e_up, l1_e_down,
                        l1_ln3_g, l1_ln3_b)
    kv0 = _kv_proj(mem2d, p0["xkv"], bsz)
    kv1 = _kv_proj(mem2d, p1["xkv"], bsz)
    mem_len = mem.shape[1]

    n_split = BATCH_SPLIT if bsz % BATCH_SPLIT == 0 else 1
    nb = bsz // n_split
    out = None
    for part in range(n_split):
        b0 = part * nb
        x1 = _mixer0(xt, pos_row, l0_sinks.astype(F32), l0_w_in, inv_freq, etab, cbase, gsum,
                     _row(l0_sgu_ln_g), _row(l0_sgu_ln_b), l0_sgu_w.astype(F32), bs_full,
                     l0_w_out, _row(l0_ln1_g), _row(l0_ln1_b), b0, nb, s_len)
        x2, y01, rt = _layer_tail(x1, kv0, mem_len, p0, router, b0, nb, s_len)
        x1 = _mixer1(x2, y01, rt, p0["ln3_g"], p0["ln3_b"], l1_w_in, l1_pool_w, _row(l1_pool_scale),
                     l1_w_out, _row(l1_ln1_g), _row(l1_ln1_b), nb, s_len)
        x2, y01, rt = _layer_tail(x1, kv1, mem_len, p1, router, b0, nb, s_len)
        out = _combine_ln(x2, y01, rt, p1["ln3_g"], p1["ln3_b"], into=out, row0=b0 * s_len,
                          full_rows=bsz * s_len)
    return out.reshape(bsz, s_len, d)
```

```python
import functools

import numpy as np
import jax
import jax.numpy as jnp
from jax import lax
from jax.experimental import pallas as pl
from jax.experimental.pallas import tpu as pltpu
from jax.experimental.pallas import tpu_sc as plsc

F32 = jnp.float32
BF16 = jnp.bfloat16
I32 = jnp.int32

DEPTH = 2
ALPHA = (2.0 * DEPTH) ** 0.25
LN_EPS = 1e-5

HEAD_DIM = 64
A_Q_HEADS = 8
A_KV_HEADS = 2
A_GROUP = A_Q_HEADS // A_KV_HEADS
BLOCK = 128
ROPE_THETA = 500000.0
ROPE_DIM = HEAD_DIM // 4
A_WIDTH = A_Q_HEADS * HEAD_DIM
KV_WIDTH = A_KV_HEADS * HEAD_DIM
B_GROUPS = 8
B_GROUP_DIM = 64
B_WIDTH = B_GROUPS * B_GROUP_DIM
POOL_WINDOWS = (2, 4, 8, 16)
POOL_HALO = 16
X_HEADS = 4
N_EXPERTS = 16
N_EXPERT_GROUPS = 4
EXPERTS_PER_GROUP = 4

LANES = 128
TOKEN_TILE = 1024
SLAB_ROWS = 512
PIECE_COLS = 256
EXPERT_ROWS = 256
BATCH_SPLIT = 2
SC_WORKERS = 32
SC_CHUNK = 64
VMEM_LIMIT = 56 * 1024 * 1024
NEG_BIG = -1e30


def _layer_norm(z, g, b):
    mu = jnp.mean(z, axis=-1, keepdims=True)
    d = z - mu
    var = jnp.mean(d * d, axis=-1, keepdims=True)
    return d * lax.rsqrt(var + LN_EPS) * g + b


def _dot(a, b):
    return jnp.dot(a, b, preferred_element_type=F32)


def _dot_nt(a, b):
    return lax.dot_general(a, b, (((1,), (1,)), ((), ())), preferred_element_type=F32)


def _split_bf16(v):
    hi = v.astype(BF16)
    lo = (v - hi.astype(F32)).astype(BF16)
    return hi, lo


def _tc_params(n_axes):
    return pltpu.CompilerParams(dimension_semantics=("arbitrary",) * n_axes,
                                vmem_limit_bytes=VMEM_LIMIT)


def _const_spec(shape):
    nd = len(shape)
    return pl.BlockSpec(shape, lambda *_: (0,) * nd, pipeline_mode=pl.Buffered(1))


def _mixer0_kernel(sinks_ref, x_ref, pos_ref, win_ref, invf_ref, etab_ref, cbase_ref, gsum_ref,
                   lng_ref, lnb_ref, ws_ref, bs_ref, wout_ref, g1_ref, b1_ref,
                   o_ref, q_s, kv_s, u_s, vn_s, mix_s, wt_s, win_s, wout_s):
    b = pl.program_id(0)
    j = pl.program_id(1)
    tq = x_ref.shape[0]
    kvw = kv_s.shape[1]

    @pl.when(jnp.logical_and(b == 0, j == 0))
    def _():
        win_s[...] = win_ref[...].astype(BF16)
        wout_s[...] = wout_ref[...].astype(BF16)
        r = lax.broadcasted_iota(I32, (BLOCK, BLOCK), 0)
        c = lax.broadcasted_iota(I32, (BLOCK, BLOCK), 1)
        for g in range(B_GROUPS):
            wt_s[g] = jnp.where(c <= r, ws_ref[g], 0.0).astype(BF16)

    @pl.when(j == 0)
    def _():
        kv_s[0:BLOCK, :] = jnp.zeros((BLOCK, kvw), BF16)

    c1 = A_WIDTH
    c2 = c1 + KV_WIDTH
    c3 = c2 + KV_WIDTH
    c4 = c3 + B_WIDTH

    def rotary_tables(rows):
        n = rows.stop - rows.start
        ang = invf_ref[...] * pos_ref[:, rows].astype(F32)
        c8 = jnp.cos(ang)
        s8 = jnp.sin(ang)
        c8h = c8.astype(BF16).astype(F32)
        s8h = s8.astype(BF16).astype(F32)
        pad = jnp.zeros((LANES - 4 * c8.shape[0], n), F32)
        stack = jnp.concatenate([c8h, c8 - c8h, s8h, s8 - s8h, pad], axis=0)
        tabs = _dot(stack.T.astype(BF16), etab_ref[...])
        return tabs[:, 0:LANES] + cbase_ref[...], tabs[:, LANES:2 * LANES], tabs[:, 2 * LANES:]

    pw = PIECE_COLS
    n_pieces = win_s.shape[1] // pw
    assert (c1 // pw, c2 // pw, c3 // pw, c4 // pw) == (2, 2, 3, 5) and c3 % pw == 0 and n_pieces == 7

    def project(xb, k):
        return _dot(xb, win_s[:, k * pw:(k + 1) * pw])

    def prepare(rows, hk, tables, k):
        n = rows.stop - rows.start
        cs, sa, sb = tables

        def rope(t):
            return t * cs + pltpu.roll(t, ROPE_DIM // 2, 1) * sa + pltpu.roll(t, LANES - ROPE_DIM // 2, 1) * sb

        if k < 2:
            for c in range(pw // LANES):
                t = hk[:, c * LANES:(c + 1) * LANES] * (HEAD_DIM ** -0.5)
                col = k * (pw // LANES) + c
                q_s[rows, col * LANES:(col + 1) * LANES] = rope(t).astype(BF16)
        elif k == 2:
            low = lax.broadcasted_iota(I32, (n, LANES), 1) < HEAD_DIM
            kr = rope(hk[:, 0:KV_WIDTH])
            kx = pltpu.roll(kr, HEAD_DIM, 1)
            vr = hk[:, KV_WIDTH:]
            vx = pltpu.roll(vr, HEAD_DIM, 1)
            kv_cols = [jnp.where(low, kr, kx), jnp.where(low, kx, kr),
                       jnp.where(low, vr, 0.0), jnp.where(low, 0.0, vx),
                       jnp.where(low, vx, 0.0), jnp.where(low, 0.0, vr)]
            for c, col in enumerate(kv_cols):
                kv_s[BLOCK + rows.start:BLOCK + rows.stop, c * LANES:(c + 1) * LANES] = col.astype(BF16)
        elif k < 5:
            lo = (k - 3) * pw
            u_s[rows, lo:lo + pw] = jax.nn.gelu(hk)
        else:
            lo = (k - 5) * pw
            v = jax.nn.gelu(hk)
            gsum = gsum_ref[...]
            mean = _dot(v.astype(BF16), gsum) * (1.0 / B_GROUP_DIM)
            d = v - mean
            var = _dot((d * d).astype(BF16), gsum) * (1.0 / B_GROUP_DIM)
            vn_s[rows, lo:lo + pw] = (d * lax.rsqrt(var + LN_EPS) * lng_ref[:, lo:lo + pw]
                                      + lnb_ref[:, lo:lo + pw]).astype(BF16)

    qi = lax.broadcasted_iota(I32, (BLOCK, 2 * BLOCK), 0)
    kj = lax.broadcasted_iota(I32, (BLOCK, 2 * BLOCK), 1)
    rel = qi + BLOCK - kj
    band = jnp.logical_and(rel >= 0, rel < BLOCK)
    low_q = lax.broadcasted_iota(I32, (BLOCK, LANES), 1) < HEAD_DIM
    low_k = lax.broadcasted_iota(I32, (2 * BLOCK, LANES), 1) < HEAD_DIM
    ones_lo = jnp.where(low_k, 1.0, 0.0).astype(BF16)
    ones_hi = jnp.where(low_k, 0.0, 1.0).astype(BF16)
    zero_q = jnp.zeros((BLOCK, LANES), BF16)

    def block_body(n):
        r0 = n * BLOCK
        kv = kv_s[pl.ds(r0, 2 * BLOCK), :]
        qb = q_s[pl.ds(r0, BLOCK), :]
        valid = jnp.logical_and(band, kj >= jnp.where(j == 0, BLOCK, 0)) if n == 0 else band
        cols_per_kv = A_GROUP // 2
        scores = {}
        for hk in range(A_KV_HEADS):
            cols = range(hk * cols_per_kv, (hk + 1) * cols_per_kv)
            pieces = []
            for c in cols:
                qp = qb[:, c * LANES:(c + 1) * LANES]
                pieces += [jnp.where(low_q, qp, zero_q), jnp.where(low_q, zero_q, qp)]
            sc = _dot_nt(jnp.concatenate(pieces, axis=0), kv[:, hk * LANES:(hk + 1) * LANES])
            for i, c in enumerate(cols):
                for half in range(2):
                    r = (2 * i + half) * BLOCK
                    scores[c, half] = sc[r:r + BLOCK, :]
        vnb = vn_s[pl.ds(r0, BLOCK), :]
        parts = []
        for c in range(B_WIDTH // LANES):
            vp = vnb[:, c * LANES:(c + 1) * LANES]
            parts.append(_dot(wt_s[2 * c], jnp.where(low_q, vp, zero_q))
                         + _dot(wt_s[2 * c + 1], jnp.where(low_q, zero_q, vp)))
        probs, esink = {}, {}
        for (c, half), sc in scores.items():
            s = jnp.where(valid, sc, NEG_BIG)
            sink = sinks_ref[2 * c + half]
            m = jnp.maximum(jnp.max(s, axis=-1, keepdims=True), sink)
            probs[c, half] = jnp.exp(s - m).astype(BF16)
            esink[c, half] = jnp.exp(sink - m)
        res = {}
        for hk in range(A_KV_HEADS):
            cols = range(hk * cols_per_kv, (hk + 1) * cols_per_kv)
            for half in range(2):
                vcol = kv[:, (2 + 2 * hk + half) * LANES:(3 + 2 * hk + half) * LANES]
                vm = jnp.concatenate([vcol, ones_lo if half == 0 else ones_hi], axis=1)
                pv = _dot(jnp.concatenate([probs[c, half] for c in cols], axis=0), vm)
                for i, c in enumerate(cols):
                    part = pv[i * BLOCK:(i + 1) * BLOCK, :]
                    res[c] = part if half == 0 else res[c] + part
        for c in range(A_WIDTH // LANES):
            den = res[c][:, LANES:] + jnp.where(low_q, esink[c, 0], esink[c, 1])
            mix_s[pl.ds(r0, BLOCK), c * LANES:(c + 1) * LANES] = (res[c][:, :LANES] / den).astype(BF16)
        mixed = jnp.concatenate(parts, axis=1) + bs_ref[...]
        mix_s[pl.ds(r0, BLOCK), A_WIDTH:] = (u_s[pl.ds(r0, BLOCK), :] * mixed).astype(BF16)

    def out_cols(rows, c):
        return _dot(mix_s[rows, :], wout_s[:, c * pw:(c + 1) * pw])

    def finish(rows, z_cols, i):
        r = slice(rows.start + i * BLOCK, rows.start + (i + 1) * BLOCK)
        z = ALPHA * x_ref[r, :] + jnp.concatenate([zc[i * BLOCK:(i + 1) * BLOCK, :] for zc in z_cols], axis=1)
        o_ref[r, :] = _layer_norm(z, g1_ref[...], b1_ref[...])

    assert tq == 2 * SLAB_ROWS and SLAB_ROWS == 4 * BLOCK and wout_s.shape[1] == 4 * pw
    sa, sb = slice(0, SLAB_ROWS), slice(SLAB_ROWS, tq)
    tab_a, tab_b = rotary_tables(sa), rotary_tables(sb)
    xa = x_ref[sa, :].astype(BF16)
    ha = [project(xa, k) for k in range(n_pieces)]
    xb = x_ref[sb, :].astype(BF16)
    hb = []
    for k in range(n_pieces):
        hb.append(project(xb, k))
        prepare(sa, ha[k], tab_a, k)
    pieces_b = iter(range(n_pieces))
    for n in range(4):
        block_body(n)
        for k in [next(pieces_b) for _ in range(2 if n < 3 else 1)]:
            prepare(sb, hb[k], tab_b, k)
    za = []
    for n in range(4):
        block_body(4 + n)
        za.append(out_cols(sa, n))
    kv_s[0:BLOCK, :] = kv_s[tq:tq + BLOCK, :]
    zb = []
    for n in range(4):
        zb.append(out_cols(sb, n))
        finish(sa, za, n)
    for n in range(4):
        finish(sb, zb, n)


def _mixer0(x, pos_row, sinks, w_in, invf, etab, cbase, gsum, lng, lnb, w_s, bs_full, w_out, g1, b1,
            b0, bsz, s_len):
    d = x.shape[1]
    t_all = bsz * s_len
    tq = TOKEN_TILE
    nj = s_len // tq
    row = lambda bb, jj: (bb * nj + jj, 0)
    in_w = w_in.shape[1]
    return pl.pallas_call(
        _mixer0_kernel,
        grid=(bsz, nj),
        in_specs=[
            pl.BlockSpec(memory_space=pltpu.SMEM),
            pl.BlockSpec((tq, d), lambda bb, jj: ((b0 + bb) * nj + jj, 0)),
            pl.BlockSpec((1, tq), lambda bb, jj: (0, (b0 + bb) * nj + jj)),
            _const_spec((d, in_w)),
            _const_spec((ROPE_DIM // 2, 1)), _const_spec((LANES, 3 * LANES)), _const_spec((1, LANES)),
            _const_spec((PIECE_COLS, PIECE_COLS)),
            _const_spec((1, B_WIDTH)), _const_spec((1, B_WIDTH)),
            _const_spec((B_GROUPS, BLOCK, BLOCK)),
            _const_spec((BLOCK, B_WIDTH)),
            _const_spec((A_WIDTH + B_WIDTH, d)),
            _const_spec((1, d)), _const_spec((1, d)),
        ],
        out_specs=pl.BlockSpec((tq, d), row),
        out_shape=jax.ShapeDtypeStruct((t_all, d), F32),
        scratch_shapes=[
            pltpu.VMEM((tq, A_WIDTH), BF16),
            pltpu.VMEM((tq + BLOCK, 6 * LANES), BF16),
            pltpu.VMEM((tq, B_WIDTH), F32),
            pltpu.VMEM((tq, B_WIDTH), BF16),
            pltpu.VMEM((tq, A_WIDTH + B_WIDTH), BF16),
            pltpu.VMEM((B_GROUPS, BLOCK, BLOCK), BF16),
            pltpu.VMEM((d, in_w), BF16),
            pltpu.VMEM((A_WIDTH + B_WIDTH, d), BF16),
        ],
        compiler_params=_tc_params(2),
        name="mixer0",
    )(sinks, x, pos_row, w_in, invf, etab, cbase, gsum, lng, lnb, w_s, bs_full, w_out, g1, b1)


def _expert_combine_ln(x2, y0_packed, y1_packed, rt, g, b):
    wt = pltpu.bitcast(rt, F32).T
    y = wt[:, 4:5] * _unpack_bf16_pairs(y0_packed) + wt[:, 5:6] * _unpack_bf16_pairs(y1_packed)
    return _layer_norm(ALPHA * x2 + y, g, b)


def _mixer1_kernel(x2_ref, y0_ref, y1_ref, rt_ref, g3_ref, b3_ref,
                   win_ref, pw_ref, ps_ref, wout_ref, g1_ref, b1_ref, o_ref,
                   h_s, mp_s, win_s, pw_s, wout_s):
    j = pl.program_id(1)
    tq = x2_ref.shape[0]
    gw = x2_ref.shape[1] // len(POOL_WINDOWS)
    slabs = [slice(c * SLAB_ROWS, (c + 1) * SLAB_ROWS) for c in range(tq // SLAB_ROWS)]

    @pl.when(jnp.logical_and(pl.program_id(0) == 0, j == 0))
    def _():
        win_s[...] = win_ref[...].astype(BF16)
        pw_s[...] = pw_ref[...].astype(BF16)
        wout_s[...] = wout_ref[...].astype(BF16)

    @pl.when(j == 0)
    def _():
        h_s[0:POOL_HALO, :] = jnp.zeros((POOL_HALO, h_s.shape[1]), F32)

    ng = len(POOL_WINDOWS)
    quarter = SLAB_ROWS // ng

    def sub(rows, i):
        return slice(rows.start + i * quarter, rows.start + (i + 1) * quarter)

    def load_in(rows, i):
        r = sub(rows, i)
        return _expert_combine_ln(x2_ref[r, :], y0_ref[r, :], y1_ref[r, :], rt_ref[:, r],
                                  g3_ref[...], b3_ref[...])

    def project(rows, xb, g):
        lo, hi = g * gw, (g + 1) * gw
        h_s[POOL_HALO + rows.start:POOL_HALO + rows.stop, lo:hi] = _dot(xb, win_s[:, lo:hi])

    def pool(rows, g):
        win = POOL_WINDOWS[g]
        lo, hi = g * gw, (g + 1) * gw
        n = rows.stop - rows.start
        t_pos = j * tq + rows.start + lax.broadcasted_iota(I32, (n, 1), 0)
        ext = h_s[rows.start:rows.stop + POOL_HALO, lo:hi]
        acc = ext
        shift = 1
        while shift < win:
            acc = acc + pltpu.roll(acc, shift, 0)
            shift *= 2
        count = jnp.minimum(t_pos + 1, win).astype(F32)
        pooled = acc[POOL_HALO:, :] / count - ext[POOL_HALO:, :]
        mapped = _dot(pooled.astype(BF16), pw_s[g])
        mp_s[rows, lo:hi] = (mapped * ps_ref[:, lo:hi]).astype(BF16)

    def out_cols(rows, g):
        return _dot(mp_s[rows, :], wout_s[:, g * gw:(g + 1) * gw])

    def finish(rows, x_parts, z_cols, i):
        lo, hi = i * quarter, (i + 1) * quarter
        z = ALPHA * x_parts[i] + jnp.concatenate([zc[lo:hi, :] for zc in z_cols], axis=1)
        o_ref[sub(rows, i), :] = _layer_norm(z, g1_ref[...], b1_ref[...])

    assert len(slabs) == 2
    sa, sb = slabs
    xa = [load_in(sa, i) for i in range(ng)]
    xa_b = jnp.concatenate(xa, axis=0).astype(BF16)
    xb = []
    for g in range(ng):
        project(sa, xa_b, g)
        xb.append(load_in(sb, g))
    xb_b = jnp.concatenate(xb, axis=0).astype(BF16)
    for g in range(ng):
        project(sb, xb_b, g)
        pool(sa, g)
    za = []
    for g in range(ng):
        za.append(out_cols(sa, g))
        pool(sb, g)
    zb = []
    for g in range(ng):
        zb.append(out_cols(sb, g))
        finish(sa, xa, za, g)
    for g in range(ng):
        finish(sb, xb, zb, g)
    h_s[0:POOL_HALO, :] = h_s[tq:tq + POOL_HALO, :]


def _mixer1(x2, y01, rt, g3, b3, w_in, pool_w, pool_scale, w_out, g1, b1, bsz, s_len):
    t_all, d = x2.shape
    tq = TOKEN_TILE
    nj = s_len // tq
    nt = bsz * nj
    row = lambda bb, jj: (bb * nj + jj, 0)
    ng = len(POOL_WINDOWS)
    return pl.pallas_call(
        _mixer1_kernel,
        grid=(bsz, nj),
        in_specs=[
            pl.BlockSpec((tq, d), row),
            pl.BlockSpec((tq, d // 2), row),
            pl.BlockSpec((tq, d // 2), lambda bb, jj: (nt + bb * nj + jj, 0)),
            pl.BlockSpec((8, tq), lambda bb, jj: (0, bb * nj + jj)),
            _const_spec((1, d)), _const_spec((1, d)),
            _const_spec((d, d)),
            _const_spec((ng, d // ng, d // ng)),
            _const_spec((1, d)),
            _const_spec((d, d)),
            _const_spec((1, d)), _const_spec((1, d)),
        ],
        out_specs=pl.BlockSpec((tq, d), row),
        out_shape=jax.ShapeDtypeStruct((t_all, d), F32),
        scratch_shapes=[pltpu.VMEM((tq + POOL_HALO, d), F32), pltpu.VMEM((tq, d), BF16),
                        pltpu.VMEM((d, d), BF16), pltpu.VMEM((ng, d // ng, d // ng), BF16),
                        pltpu.VMEM((d, d), BF16)],
        compiler_params=_tc_params(2),
        name="mixer1",
    )(x2, y01, y01, rt, g3, b3, w_in, pool_w, pool_scale, w_out, g1, b1)


def _kv_kernel(mem_ref, w_ref, o_ref, w_s):
    @pl.when(pl.program_id(0) == 0)
    def _():
        w_s[...] = w_ref[...].astype(BF16)

    o_ref[...] = _dot(mem_ref[...].astype(BF16), w_s[...]).astype(BF16)


def _kv_proj(mem2d, wkv, bsz):
    rows, d = mem2d.shape
    m = rows // bsz
    return pl.pallas_call(
        _kv_kernel,
        grid=(bsz,),
        in_specs=[pl.BlockSpec((m, d), lambda i: (i, 0)), _const_spec(wkv.shape)],
        out_specs=pl.BlockSpec((m, wkv.shape[1]), lambda i: (i, 0)),
        out_shape=jax.ShapeDtypeStruct((rows, wkv.shape[1]), BF16),
        scratch_shapes=[pltpu.VMEM(wkv.shape, BF16)],
        compiler_params=_tc_params(1),
        name="kv_proj",
    )(mem2d, wkv)


def _top2_of4(v):
    hi01, lo01 = jnp.maximum(v[0], v[1]), jnp.minimum(v[0], v[1])
    hi23, lo23 = jnp.maximum(v[2], v[3]), jnp.minimum(v[2], v[3])
    return jnp.maximum(hi01, hi23) + jnp.maximum(jnp.minimum(hi01, hi23), jnp.maximum(lo01, lo23))


def _argmax_first(vals):
    best, idx = vals[0], jnp.zeros(vals[0].shape, I32)
    for i in range(1, len(vals)):
        better = vals[i] > best
        best = jnp.where(better, vals[i], best)
        idx = jnp.where(better, i, idx)
    return best, idx


def _xattn_kernel(x_ref, kv_ref, wq_ref, wo_ref, g2_ref, b2_ref, rwh_ref, rwl_ref, rb_ref,
                  x2_ref, xp_ref, rt_ref, cnt_ref, run_s, wq_s, wo_s):
    first = jnp.logical_and(pl.program_id(0) == 0, pl.program_id(1) == 0)
    tq, d = x_ref.shape

    @pl.when(first)
    def _():
        run_s[...] = jnp.zeros(run_s.shape, F32)
        wq_s[...] = wq_ref[...].astype(BF16)
        wo_s[...] = wo_ref[...].astype(BF16)

    slabs = [slice(c * SLAB_ROWS, (c + 1) * SLAB_ROWS) for c in range(tq // SLAB_ROWS)]
    run = run_s[...]
    zs = [_xattn_attend(*_xattn_query(rows, x_ref, wq_s), kv_ref, wo_s) for rows in slabs]
    picks = []
    for rows, z in zip(slabs, zs):
        logits = _xattn_norm_logits(rows, z, g2_ref, b2_ref, rwh_ref, rwl_ref, x2_ref, xp_ref)
        picks.append(_xattn_select(logits, rb_ref))
    for rows, pick in zip(slabs, picks):
        run = _xattn_rank(rows, pick, run, rt_ref)
    run_s[...] = run
    cnt_ref[...] = jnp.broadcast_to(run, cnt_ref.shape).astype(I32)


def _xattn_query(rows, x_ref, wq_s):
    hd = x_ref.shape[1] // X_HEADS
    x = x_ref[rows, :]
    return x, _dot(x.astype(BF16), wq_s[...]) * (hd ** -0.5)


def _xattn_attend(x, q, kv_ref, wo_s):
    d = x.shape[1]
    hd = d // X_HEADS
    outs = []
    for hx in range(X_HEADS):
        qh = q[:, hx * hd:(hx + 1) * hd].astype(BF16)
        kh = kv_ref[:, hx * hd:(hx + 1) * hd]
        vh = kv_ref[:, d + hx * hd:d + (hx + 1) * hd]
        s = _dot_nt(qh, kh)
        p = jnp.exp(s - jnp.max(s, axis=-1, keepdims=True))
        o = _dot(p.astype(BF16), vh) / jnp.sum(p, axis=-1, keepdims=True)
        outs.append(o.astype(BF16))
    return ALPHA * x + _dot(jnp.concatenate(outs, axis=1), wo_s[...])


def _xattn_norm_logits(rows, z, g2_ref, b2_ref, rwh_ref, rwl_ref, x2_ref, xp_ref):
    x2 = _layer_norm(z, g2_ref[...], b2_ref[...])
    x2_ref[rows, :] = x2

    xp_ref[rows, :] = _pack_bf16_pairs(x2)

    xh, xl = _split_bf16(x2)
    both = _dot_nt(jnp.concatenate([rwh_ref[...], rwl_ref[...]], axis=0), xh)
    return both[0:N_EXPERTS] + both[N_EXPERTS:] + _dot_nt(rwh_ref[...], xl)


def _xattn_select(logits, rb_ref):
    e_max = jnp.max(logits, axis=0, keepdims=True)
    ex = jnp.exp(logits - e_max)
    scores = ex / jnp.sum(ex, axis=0, keepdims=True)
    biased = scores + rb_ref[...]
    sc = [scores[e:e + 1, :] for e in range(N_EXPERTS)]
    bi = [biased[e:e + 1, :] for e in range(N_EXPERTS)]
    epg = EXPERTS_PER_GROUP
    gscore = [_top2_of4(bi[g * epg:(g + 1) * epg]) for g in range(N_EXPERT_GROUPS)]
    _, gsel = _argmax_first(gscore)

    def pick(vals):
        return [functools.reduce(lambda acc, g: jnp.where(gsel == g, vals[g * epg + i], acc),
                                 range(1, N_EXPERT_GROUPS), vals[i]) for i in range(epg)]

    in_b = pick(bi)
    in_s = pick(sc)
    _, i0 = _argmax_first(in_b)
    _, i1 = _argmax_first([jnp.where(i0 == i, -jnp.inf, in_b[i]) for i in range(epg)])

    def take(vals, idx):
        return functools.reduce(lambda acc, i: jnp.where(idx == i, vals[i], acc), range(1, epg), vals[0])

    s0, s1 = take(in_s, i0), take(in_s, i1)
    w0, w1 = s0 / (s0 + s1), s1 / (s0 + s1)
    return gsel * epg + i0, gsel * epg + i1, w0, w1


def _xattn_rank(rows, selection, run, rt_ref):
    e0, e1, w0, w1 = selection
    tq = rows.stop - rows.start
    eid = lax.broadcasted_iota(I32, (N_EXPERTS, tq), 0)
    oh0 = eid == e0
    oh1 = eid == e1
    onehot = jnp.where(jnp.logical_or(oh0, oh1), 1.0, 0.0)
    rr = lax.broadcasted_iota(I32, (tq, tq), 0)
    cc = lax.broadcasted_iota(I32, (tq, tq), 1)
    upper = jnp.where(rr < cc, 1.0, 0.0).astype(BF16)
    prefix = _dot(onehot.astype(BF16), upper) + run
    r0 = jnp.sum(jnp.where(oh0, prefix, 0.0), axis=0, keepdims=True).astype(I32)
    r1 = jnp.sum(jnp.where(oh1, prefix, 0.0), axis=0, keepdims=True).astype(I32)

    zero = jnp.zeros((1, tq), I32)
    rt_ref[:, rows] = jnp.concatenate(
        [e0, e1, r0, r1, pltpu.bitcast(w0, I32), pltpu.bitcast(w1, I32), zero, zero], axis=0)
    return run + jnp.sum(onehot, axis=1, keepdims=True)


def _xattn_route(x1, kv, mem_len, wq, wo, g2, b2, rw_hi, rw_lo, rbias, b0, bsz, s_len):
    t_all, d = x1.shape
    tq = TOKEN_TILE
    nj = s_len // tq
    m = mem_len
    row = lambda bb, jj: (bb * nj + jj, 0)
    return pl.pallas_call(
        _xattn_kernel,
        grid=(bsz, nj),
        in_specs=[
            pl.BlockSpec((tq, d), row),
            pl.BlockSpec((m, 2 * d), lambda bb, jj: (b0 + bb, 0)),
            _const_spec((d, d)), _const_spec((d, d)),
            _const_spec((1, d)), _const_spec((1, d)),
            _const_spec((N_EXPERTS, d)), _const_spec((N_EXPERTS, d)),
            _const_spec((N_EXPERTS, 1)),
        ],
        out_specs=[
            pl.BlockSpec((tq, d), row),
            pl.BlockSpec((tq, d // 2), row),
            pl.BlockSpec((8, tq), lambda bb, jj: (0, bb * nj + jj)),
            _const_spec((N_EXPERTS, LANES)),
        ],
        out_shape=[
            jax.ShapeDtypeStruct((t_all, d), F32),
            jax.ShapeDtypeStruct((t_all, d // 2), I32),
            jax.ShapeDtypeStruct((8, t_all), I32),
            jax.ShapeDtypeStruct((N_EXPERTS, LANES), I32),
        ],
        scratch_shapes=[pltpu.VMEM((N_EXPERTS, 1), F32), pltpu.VMEM((d, d), BF16), pltpu.VMEM((d, d), BF16)],
        compiler_params=_tc_params(2),
        name="xattn_route",
    )(x1, kv, wq, wo, g2, b2, rw_hi, rw_lo, rbias)


def _sc_mesh():
    return plsc.VectorSubcoreMesh(core_axis_name="c", subcore_axis_name="s")


def _sc_params():
    return pltpu.CompilerParams(needs_layout_passes=False)


def _worker_id():
    return lax.axis_index("s") * lax.axis_size("c") + lax.axis_index("c")


def _sc_dispatch(xp, dest2d, n_rows):
    t_all, width = xp.shape
    chunk = dest2d.shape[1]
    tok_w = t_all // SC_WORKERS
    nch = tok_w // chunk
    slot1 = t_all // chunk
    assert t_all % (SC_WORKERS * chunk * 2) == 0

    def body(x_hbm, dest_hbm, out_hbm, idx0_v, idx1_v, buf0, buf1, sem_r, sem_w):
        wid = _worker_id()
        base = wid * tok_w
        pltpu.sync_copy(dest_hbm.at[pl.ds(wid * nch, nch)], idx0_v)
        pltpu.sync_copy(dest_hbm.at[pl.ds(slot1 + wid * nch, nch)], idx1_v)

        def read(c, buf, k):
            return pltpu.make_async_copy(x_hbm.at[pl.ds(base + c * chunk, chunk)], buf, sem_r.at[k])

        def scatter(c, buf):
            a = pltpu.make_async_copy(buf, out_hbm.at[idx0_v.at[c]], sem_w.at[0])
            b = pltpu.make_async_copy(buf, out_hbm.at[idx1_v.at[c]], sem_w.at[1])
            a.start()
            b.start()
            a.wait()
            b.wait()

        read(0, buf0, 0).start()

        @pl.loop(0, nch // 2)
        def _(g):
            c = 2 * g
            read(c + 1, buf1, 1).start()
            read(c, buf0, 0).wait()
            scatter(c, buf0)

            @pl.when(c + 2 < nch)
            def _():
                read(c + 2, buf0, 0).start()

            read(c + 1, buf1, 1).wait()
            scatter(c + 1, buf1)

    return pl.kernel(
        body,
        out_type=jax.ShapeDtypeStruct((n_rows, width), xp.dtype),
        mesh=_sc_mesh(),
        scratch_types=[
            pltpu.VMEM((nch, chunk), I32),
            pltpu.VMEM((nch, chunk), I32),
            pltpu.VMEM((chunk, width), xp.dtype),
            pltpu.VMEM((chunk, width), xp.dtype),
            pltpu.SemaphoreType.DMA((2,)),
            pltpu.SemaphoreType.DMA((2,)),
        ],
        compiler_params=_sc_params(),
        name="sc_dispatch",
    )(xp, dest2d)


def _sc_combine(ys, dest2d):
    n_idx_rows, chunk = dest2d.shape
    width = ys.shape[1]
    nch = n_idx_rows // SC_WORKERS
    assert n_idx_rows % (SC_WORKERS * 2) == 0

    def body(y_hbm, dest_hbm, out_hbm, idx_v, buf0, buf1, sem_g):
        wid = _worker_id()
        base = wid * nch * chunk
        pltpu.sync_copy(dest_hbm.at[pl.ds(wid * nch, nch)], idx_v)

        def gather(c, buf, k):
            return pltpu.make_async_copy(y_hbm.at[idx_v.at[c]], buf, sem_g.at[k])

        def write(c, buf):
            pltpu.sync_copy(buf, out_hbm.at[pl.ds(base + c * chunk, chunk)])

        gather(0, buf0, 0).start()

        @pl.loop(0, nch // 2)
        def _(g):
            c = 2 * g
            gather(c + 1, buf1, 1).start()
            gather(c, buf0, 0).wait()
            write(c, buf0)

            @pl.when(c + 2 < nch)
            def _():
                gather(c + 2, buf0, 0).start()

            gather(c + 1, buf1, 1).wait()
            write(c + 1, buf1)

    return pl.kernel(
        body,
        out_type=jax.ShapeDtypeStruct((n_idx_rows * chunk, width), ys.dtype),
        mesh=_sc_mesh(),
        scratch_types=[
            pltpu.VMEM((nch, chunk), I32),
            pltpu.VMEM((chunk, width), ys.dtype),
            pltpu.VMEM((chunk, width), ys.dtype),
            pltpu.SemaphoreType.DMA((2,)),
        ],
        compiler_params=_sc_params(),
        name="sc_combine",
    )(ys, dest2d)


def _pack_bf16_pairs(v):
    half = v.shape[1] // 2
    lo = pltpu.bitcast(v[:, :half].astype(BF16).astype(F32), jnp.uint32) >> 16
    hi = pltpu.bitcast(v[:, half:].astype(BF16).astype(F32), jnp.uint32) & jnp.uint32(0xFFFF0000)
    return pltpu.bitcast(hi | lo, I32)


def _unpack_bf16_pairs(w):
    w = pltpu.bitcast(w, jnp.uint32)
    lo = pltpu.bitcast(w << 16, F32)
    hi = pltpu.bitcast(w & jnp.uint32(0xFFFF0000), F32)
    return jnp.concatenate([lo, hi], axis=1)


def _ffn_kernel(be_ref, nv_ref, nu_ref, nxt_ref, xs_ref, wg_hbm, wu_hbm, wd_hbm, o_ref,
                wg_f, wu_f, wd_f, wg_s, wu_s, wd_s, sem):
    i = pl.program_id(0)

    def fetch(e):
        return (pltpu.make_async_copy(wg_hbm.at[e], wg_f, sem.at[0]),
                pltpu.make_async_copy(wu_hbm.at[e], wu_f, sem.at[1]),
                pltpu.make_async_copy(wd_hbm.at[e], wd_f, sem.at[2]))

    @pl.when(nxt_ref[i] >= 0)
    def _():
        @pl.when(i == 0)
        def _():
            for cp in fetch(be_ref[0]):
                cp.start()

        for cp in fetch(be_ref[i]):
            cp.wait()
        wg_s[...] = wg_f[...].astype(BF16)
        wu_s[...] = wu_f[...].astype(BF16)
        wd_s[...] = wd_f[...].astype(BF16)

        @pl.when(nxt_ref[i] < N_EXPERTS)
        def _():
            for cp in fetch(nxt_ref[i]):
                cp.start()

    @pl.when(i < nu_ref[0])
    def _():
        live = lax.broadcasted_iota(I32, xs_ref.shape, 0) < nv_ref[i]
        xb = _unpack_bf16_pairs(jnp.where(live, xs_ref[...], 0)).astype(BF16)
        act = jax.nn.silu(_dot(xb, wg_s[...])) * _dot(xb, wu_s[...])
        o_ref[...] = _pack_bf16_pairs(_dot(act.astype(BF16), wd_s[...]))


def _expert_ffn(xs, blk_expert, blk_valid, n_used, blk_next, w_gate, w_up, w_down):
    n_rows, half = xs.shape
    d = 2 * half
    de = w_gate.shape[2]
    bm = EXPERT_ROWS
    rows = lambda i, be, nv, nu, nx: (jnp.minimum(i, nu[0] - 1), 0)
    hbm = pl.BlockSpec(memory_space=pl.ANY)
    return pl.pallas_call(
        _ffn_kernel,
        grid_spec=pltpu.PrefetchScalarGridSpec(
            num_scalar_prefetch=4,
            grid=(n_rows // bm,),
            in_specs=[pl.BlockSpec((bm, half), rows), hbm, hbm, hbm],
            out_specs=pl.BlockSpec((bm, half), rows),
            scratch_shapes=[
                pltpu.VMEM((d, de), F32), pltpu.VMEM((d, de), F32), pltpu.VMEM((de, d), F32),
                pltpu.VMEM((d, de), BF16), pltpu.VMEM((d, de), BF16), pltpu.VMEM((de, d), BF16),
                pltpu.SemaphoreType.DMA((3,)),
            ],
        ),
        out_shape=jax.ShapeDtypeStruct((n_rows, half), I32),
        compiler_params=_tc_params(1),
        name="expert_ffn",
    )(blk_expert, blk_valid, n_used, blk_next, xs, w_gate, w_up, w_down)


def _combine_kernel(x_ref, y0_ref, y1_ref, rt_ref, g_ref, b_ref, o_ref):
    o_ref[...] = _expert_combine_ln(x_ref[...], y0_ref[...], y1_ref[...], rt_ref[...], g_ref[...], b_ref[...])


def _combine_kernel_into(x_ref, y0_ref, y1_ref, rt_ref, g_ref, b_ref, full_ref, o_ref):
    del full_ref
    _combine_kernel(x_ref, y0_ref, y1_ref, rt_ref, g_ref, b_ref, o_ref)


def _combine_ln(x2, y01, rt, g3, b3, into=None, row0=0, full_rows=None):
    t_all, d = x2.shape
    tq = TOKEN_TILE
    nt = t_all // tq
    blk0 = row0 // tq if full_rows else 0
    in_specs = [
        pl.BlockSpec((tq, d), lambda i: (i, 0)),
        pl.BlockSpec((tq, d // 2), lambda i: (i, 0)),
        pl.BlockSpec((tq, d // 2), lambda i: (i + nt, 0)),
        pl.BlockSpec((8, tq), lambda i: (0, i)),
        _const_spec((1, d)), _const_spec((1, d)),
    ]
    args = [x2, y01, y01, rt, g3, b3]
    body, aliases = _combine_kernel, {}
    if into is not None:
        in_specs.append(pl.BlockSpec(memory_space=pl.ANY))
        args.append(into)
        body, aliases = _combine_kernel_into, {len(args) - 1: 0}
    return pl.pallas_call(
        body,
        grid=(nt,),
        in_specs=in_specs,
        out_specs=pl.BlockSpec((tq, d), lambda i: (i + blk0, 0)),
        out_shape=jax.ShapeDtypeStruct((full_rows or t_all, d), F32),
        input_output_aliases=aliases,
        compiler_params=_tc_params(1),
        name="combine_ln",
    )(*args)


def _routing_tables(rt, counts):
    bm = EXPERT_ROWS
    t_all = rt.shape[1]
    n_rows = 2 * t_all + N_EXPERTS * bm
    cnt = counts[:, 0]
    padded = (cnt + bm - 1) // bm * bm
    ends = jnp.cumsum(padded)
    offs = ends - padded
    experts = rt[0:2]
    off_tok = jnp.sum(jnp.where(experts[None] == jnp.arange(N_EXPERTS, dtype=I32)[:, None, None],
                                offs[:, None, None], 0), axis=0)
    dest2d = (off_tok + rt[2:4]).reshape(-1, SC_CHUNK).astype(I32)
    blk_start = jnp.arange(n_rows // bm, dtype=I32) * bm
    blk_expert = jnp.minimum(jnp.sum(blk_start[:, None] >= ends[None, :], axis=1), N_EXPERTS - 1).astype(I32)
    live_end = jnp.sum(jnp.where(blk_expert[:, None] == jnp.arange(N_EXPERTS, dtype=I32)[None, :],
                                 (offs + cnt)[None, :], 0), axis=1)
    blk_valid = jnp.clip(live_end - blk_start, 0, bm).astype(I32)
    n_used = (ends[-1:] // bm).astype(I32)
    eid = jnp.arange(N_EXPERTS, dtype=I32)
    later_present = jnp.logical_and(eid[None, :] > eid[:, None], (cnt > 0)[None, :])
    next_present = jnp.min(jnp.where(later_present, eid[None, :], N_EXPERTS), axis=1)
    next_of_blk = jnp.sum(jnp.where(blk_expert[:, None] == eid[None, :], next_present[None, :], 0), axis=1)
    prev_expert = jnp.concatenate([jnp.full((1,), -1, I32), blk_expert[:-1]])
    is_first = jnp.logical_and(blk_start < ends[-1], blk_expert != prev_expert)
    blk_next = jnp.where(is_first, next_of_blk, -1).astype(I32)
    return dest2d, blk_expert, blk_valid, n_used, blk_next, n_rows


def _layer_tail(x1, kv, mem_len, p, router, b0, bsz, s_len):
    rw_hi, rw_lo, rbias = router
    x2, xp, rt, counts = _xattn_route(x1, kv, mem_len, p["xq"], p["xo"], p["ln2_g"], p["ln2_b"],
                                      rw_hi, rw_lo, rbias, b0, bsz, s_len)
    dest2d, blk_expert, blk_valid, n_used, blk_next, n_rows = _routing_tables(rt, counts)
    xs = _sc_dispatch(xp, dest2d, n_rows)
    ys = _expert_ffn(xs, blk_expert, blk_valid, n_used, blk_next, p["e_gate"], p["e_up"], p["e_down"])
    y01 = _sc_combine(ys, dest2d)
    return x2, y01, rt


def _row(v):
    return v.reshape(1, -1).astype(F32)


def _common_params(xq, xkv, xo, ln2_g, ln2_b, e_gate, e_up, e_down, ln3_g, ln3_b):
    return dict(xq=xq, xkv=xkv, xo=xo, ln2_g=_row(ln2_g), ln2_b=_row(ln2_b),
                e_gate=e_gate, e_up=e_up, e_down=e_down, ln3_g=_row(ln3_g), ln3_b=_row(ln3_b))


def kernel(x, mem, positions, router_w, router_bias, l0_w_in, l0_sinks, l0_sgu_ln_g, l0_sgu_ln_b, l0_sgu_w, l0_sgu_b, l0_w_out, l0_ln1_g, l0_ln1_b, l0_xq, l0_xkv, l0_xo, l0_ln2_g, l0_ln2_b, l0_e_gate, l0_e_up, l0_e_down, l0_ln3_g, l0_ln3_b, l1_w_in, l1_pool_w, l1_pool_scale, l1_w_out, l1_ln1_g, l1_ln1_b, l1_xq, l1_xkv, l1_xo, l1_ln2_g, l1_ln2_b, l1_e_gate, l1_e_up, l1_e_down, l1_ln3_g, l1_ln3_b):
    bsz, s_len, d = x.shape
    assert s_len % TOKEN_TILE == 0 and TOKEN_TILE == 2 * SLAB_ROWS and SLAB_ROWS % BLOCK == 0
    xt = x.reshape(bsz * s_len, d)
    mem2d = mem.reshape(-1, d)

    rw_t = router_w.T.astype(F32)
    rw_hi = rw_t.astype(BF16)
    rw_lo = (rw_t - rw_hi.astype(F32)).astype(BF16)
    router = (rw_hi, rw_lo, router_bias.reshape(-1, 1).astype(F32))

    half = ROPE_DIM // 2
    inv_freq = (ROPE_THETA ** (-(jnp.arange(half, dtype=F32) * 2.0 / ROPE_DIM))).reshape(half, 1)
    etab_np = np.zeros((LANES, 3 * LANES), np.float32)
    cbase_np = np.ones((1, LANES), np.float32)
    for ln in range(LANES):
        dd = ln % HEAD_DIM
        if dd < ROPE_DIM:
            cbase_np[0, ln] = 0.0
            etab_np[[dd % half, half + dd % half], ln] = 1.0
            if dd >= half:
                etab_np[[2 * half + dd - half, 3 * half + dd - half], LANES + ln] = 1.0
            else:
                etab_np[[2 * half + dd, 3 * half + dd], 2 * LANES + ln] = -1.0
    etab = jnp.asarray(etab_np, BF16)
    cbase = jnp.asarray(cbase_np)
    pos_row = positions.reshape(1, -1).astype(I32)
    grp = jnp.arange(PIECE_COLS) // B_GROUP_DIM
    gsum = (grp[:, None] == grp[None, :]).astype(BF16)
    bs_full = jnp.repeat(l0_sgu_b.T.astype(F32), B_GROUP_DIM, axis=1)

    p0 = _common_params(l0_xq, l0_xkv, l0_xo, l0_ln2_g, l0_ln2_b, l0_e_gate, l0_e_up, l0_e_down,
                        l0_ln3_g, l0_ln3_b)
    p1 = _common_params(l1_xq, l1_xkv, l1_xo, l1_ln2_g, l1_ln2_b, l1_e_gate, l1_e_up, l1_e_down,
                        l1_ln3_g, l1_ln3_b)
    kv0 = _kv_proj(mem2d, p0["xkv"], bsz)
    kv1 = _kv_proj(mem2d, p1["xkv"], bsz)
    mem_len = mem.shape[1]

    n_split = BATCH_SPLIT if bsz % BATCH_SPLIT == 0 else 1
    nb = bsz // n_split
    out = None
    for part in range(n_split):
        b0 = part * nb
        x1 = _mixer0(xt, pos_row, l0_sinks.astype(F32), l0_w_in, inv_freq, etab, cbase, gsum,
                     _row(l0_sgu_ln_g), _row(l0_sgu_ln_b), l0_sgu_w.astype(F32), bs_full,
                     l0_w_out, _row(l0_ln1_g), _row(l0_ln1_b), b0, nb, s_len)
        x2, y01, rt = _layer_tail(x1, kv0, mem_len, p0, router, b0, nb, s_len)
        x1 = _mixer1(x2, y01, rt, p0["ln3_g"], p0["ln3_b"], l1_w_in, l1_pool_w, _row(l1_pool_scale),
                     l1_w_out, _row(l1_ln1_g), _row(l1_ln1_b), nb, s_len)
        x2, y01, rt = _layer_tail(x1, kv1, mem_len, p1, router, b0, nb, s_len)
        out = _combine_ln(x2, y01, rt, p1["ln3_g"], p1["ln3_b"], into=out, row0=b0 * s_len,
                          full_rows=bsz * s_len)
    return out.reshape(bsz, s_len, d)
```

```python
import functools

import numpy as np
import jax
import jax.numpy as jnp
from jax import lax
from jax.experimental import pallas as pl
from jax.experimental.pallas import tpu as pltpu
from jax.experimental.pallas import tpu_sc as plsc

F32 = jnp.float32
BF16 = jnp.bfloat16
I32 = jnp.int32

DEPTH = 2
ALPHA = (2.0 * DEPTH) ** 0.25
LN_EPS = 1e-5

HEAD_DIM = 64
A_Q_HEADS = 8
A_KV_HEADS = 2
A_GROUP = A_Q_HEADS // A_KV_HEADS
BLOCK = 128
ROPE_THETA = 500000.0
ROPE_DIM = HEAD_DIM // 4
A_WIDTH = A_Q_HEADS * HEAD_DIM
KV_WIDTH = A_KV_HEADS * HEAD_DIM
B_GROUPS = 8
B_GROUP_DIM = 64
B_WIDTH = B_GROUPS * B_GROUP_DIM
POOL_WINDOWS = (2, 4, 8, 16)
POOL_HALO = 16
X_HEADS = 4
N_EXPERTS = 16
N_EXPERT_GROUPS = 4
EXPERTS_PER_GROUP = 4

LANES = 128
TOKEN_TILE = 1024
SLAB_ROWS = 512
PIECE_COLS = 256
EXPERT_ROWS = 1024
BATCH_SPLIT = 2
SC_WORKERS = 32
SC_CHUNK = 64
VMEM_LIMIT = 56 * 1024 * 1024
NEG_BIG = -1e30


def _layer_norm(z, g, b):
    mu = jnp.mean(z, axis=-1, keepdims=True)
    d = z - mu
    var = jnp.mean(d * d, axis=-1, keepdims=True)
    return d * lax.rsqrt(var + LN_EPS) * g + b


def _dot(a, b):
    return jnp.dot(a, b, preferred_element_type=F32)


def _dot_nt(a, b):
    return lax.dot_general(a, b, (((1,), (1,)), ((), ())), preferred_element_type=F32)


def _split_bf16(v):
    hi = v.astype(BF16)
    lo = (v - hi.astype(F32)).astype(BF16)
    return hi, lo


def _tc_params(n_axes):
    return pltpu.CompilerParams(dimension_semantics=("arbitrary",) * n_axes,
                                vmem_limit_bytes=VMEM_LIMIT)


def _const_spec(shape):
    nd = len(shape)
    return pl.BlockSpec(shape, lambda *_: (0,) * nd, pipeline_mode=pl.Buffered(1))


def _mixer0_kernel(sinks_ref, x_ref, pos_ref, win_ref, invf_ref, etab_ref, cbase_ref, gsum_ref,
                   lng_ref, lnb_ref, ws_ref, bs_ref, wout_ref, g1_ref, b1_ref,
                   o_ref, q_s, kv_s, u_s, vn_s, mix_s, wt_s, win_s, wout_s):
    b = pl.program_id(0)
    j = pl.program_id(1)
    tq = x_ref.shape[0]
    kvw = kv_s.shape[1]

    @pl.when(jnp.logical_and(b == 0, j == 0))
    def _():
        win_s[...] = win_ref[...].astype(BF16)
        wout_s[...] = wout_ref[...].astype(BF16)
        r = lax.broadcasted_iota(I32, (BLOCK, BLOCK), 0)
        c = lax.broadcasted_iota(I32, (BLOCK, BLOCK), 1)
        for g in range(B_GROUPS):
            wt_s[g] = jnp.where(c <= r, ws_ref[g], 0.0).astype(BF16)

    @pl.when(j == 0)
    def _():
        kv_s[0:BLOCK, :] = jnp.zeros((BLOCK, kvw), BF16)

    c1 = A_WIDTH
    c2 = c1 + KV_WIDTH
    c3 = c2 + KV_WIDTH
    c4 = c3 + B_WIDTH

    def rotary_tables(rows):
        n = rows.stop - rows.start
        ang = invf_ref[...] * pos_ref[:, rows].astype(F32)
        c8 = jnp.cos(ang)
        s8 = jnp.sin(ang)
        c8h = c8.astype(BF16).astype(F32)
        s8h = s8.astype(BF16).astype(F32)
        pad = jnp.zeros((LANES - 4 * c8.shape[0], n), F32)
        stack = jnp.concatenate([c8h, c8 - c8h, s8h, s8 - s8h, pad], axis=0)
        tabs = _dot(stack.T.astype(BF16), etab_ref[...])
        return tabs[:, 0:LANES] + cbase_ref[...], tabs[:, LANES:2 * LANES], tabs[:, 2 * LANES:]

    pw = PIECE_COLS
    n_pieces = win_s.shape[1] // pw
    assert (c1 // pw, c2 // pw, c3 // pw, c4 // pw) == (2, 2, 3, 5) and c3 % pw == 0 and n_pieces == 7

    def project(xb, k):
        return _dot(xb, win_s[:, k * pw:(k + 1) * pw])

    def prepare(rows, hk, tables, k):
        n = rows.stop - rows.start
        cs, sa, sb = tables

        def rope(t):
            return t * cs + pltpu.roll(t, ROPE_DIM // 2, 1) * sa + pltpu.roll(t, LANES - ROPE_DIM // 2, 1) * sb

        if k < 2:
            for c in range(pw // LANES):
                t = hk[:, c * LANES:(c + 1) * LANES] * (HEAD_DIM ** -0.5)
                col = k * (pw // LANES) + c
                q_s[rows, col * LANES:(col + 1) * LANES] = rope(t).astype(BF16)
        elif k == 2:
            low = lax.broadcasted_iota(I32, (n, LANES), 1) < HEAD_DIM
            kr = rope(hk[:, 0:KV_WIDTH])
            kx = pltpu.roll(kr, HEAD_DIM, 1)
            vr = hk[:, KV_WIDTH:]
            vx = pltpu.roll(vr, HEAD_DIM, 1)
            kv_cols = [jnp.where(low, kr, kx), jnp.where(low, kx, kr),
                       jnp.where(low, vr, 0.0), jnp.where(low, 0.0, vx),
                       jnp.where(low, vx, 0.0), jnp.where(low, 0.0, vr)]
            for c, col in enumerate(kv_cols):
                kv_s[BLOCK + rows.start:BLOCK + rows.stop, c * LANES:(c + 1) * LANES] = col.astype(BF16)
        elif k < 5:
            lo = (k - 3) * pw
            u_s[rows, lo:lo + pw] = jax.nn.gelu(hk)
        else:
            lo = (k - 5) * pw
            v = jax.nn.gelu(hk)
            gsum = gsum_ref[...]
            mean = _dot(v.astype(BF16), gsum) * (1.0 / B_GROUP_DIM)
            d = v - mean
            var = _dot((d * d).astype(BF16), gsum) * (1.0 / B_GROUP_DIM)
            vn_s[rows, lo:lo + pw] = (d * lax.rsqrt(var + LN_EPS) * lng_ref[:, lo:lo + pw]
                                      + lnb_ref[:, lo:lo + pw]).astype(BF16)

    qi = lax.broadcasted_iota(I32, (BLOCK, 2 * BLOCK), 0)
    kj = lax.broadcasted_iota(I32, (BLOCK, 2 * BLOCK), 1)
    rel = qi + BLOCK - kj
    band = jnp.logical_and(rel >= 0, rel < BLOCK)
    low_q = lax.broadcasted_iota(I32, (BLOCK, LANES), 1) < HEAD_DIM
    low_k = lax.broadcasted_iota(I32, (2 * BLOCK, LANES), 1) < HEAD_DIM
    ones_lo = jnp.where(low_k, 1.0, 0.0).astype(BF16)
    ones_hi = jnp.where(low_k, 0.0, 1.0).astype(BF16)
    zero_q = jnp.zeros((BLOCK, LANES), BF16)

    def block_body(n):
        r0 = n * BLOCK
        kv = kv_s[pl.ds(r0, 2 * BLOCK), :]
        qb = q_s[pl.ds(r0, BLOCK), :]
        valid = jnp.logical_and(band, kj >= jnp.where(j == 0, BLOCK, 0)) if n == 0 else band
        cols_per_kv = A_GROUP // 2
        scores = {}
        for hk in range(A_KV_HEADS):
            cols = range(hk * cols_per_kv, (hk + 1) * cols_per_kv)
            pieces = []
            for c in cols:
                qp = qb[:, c * LANES:(c + 1) * LANES]
                pieces += [jnp.where(low_q, qp, zero_q), jnp.where(low_q, zero_q, qp)]
            sc = _dot_nt(jnp.concatenate(pieces, axis=0), kv[:, hk * LANES:(hk + 1) * LANES])
            for i, c in enumerate(cols):
                for half in range(2):
                    r = (2 * i + half) * BLOCK
                    scores[c, half] = sc[r:r + BLOCK, :]
        vnb = vn_s[pl.ds(r0, BLOCK), :]
        parts = []
        for c in range(B_WIDTH // LANES):
            vp = vnb[:, c * LANES:(c + 1) * LANES]
            parts.append(_dot(wt_s[2 * c], jnp.where(low_q, vp, zero_q))
                         + _dot(wt_s[2 * c + 1], jnp.where(low_q, zero_q, vp)))
        probs, esink = {}, {}
        for (c, half), sc in scores.items():
            s = jnp.where(valid, sc, NEG_BIG)
            sink = sinks_ref[2 * c + half]
            m = jnp.maximum(jnp.max(s, axis=-1, keepdims=True), sink)
            probs[c, half] = jnp.exp(s - m).astype(BF16)
            esink[c, half] = jnp.exp(sink - m)
        res = {}
        for hk in range(A_KV_HEADS):
            cols = range(hk * cols_per_kv, (hk + 1) * cols_per_kv)
            for half in range(2):
                vcol = kv[:, (2 + 2 * hk + half) * LANES:(3 + 2 * hk + half) * LANES]
                vm = jnp.concatenate([vcol, ones_lo if half == 0 else ones_hi], axis=1)
                pv = _dot(jnp.concatenate([probs[c, half] for c in cols], axis=0), vm)
                for i, c in enumerate(cols):
                    part = pv[i * BLOCK:(i + 1) * BLOCK, :]
                    res[c] = part if half == 0 else res[c] + part
        for c in range(A_WIDTH // LANES):
            den = res[c][:, LANES:] + jnp.where(low_q, esink[c, 0], esink[c, 1])
            mix_s[pl.ds(r0, BLOCK), c * LANES:(c + 1) * LANES] = (res[c][:, :LANES] / den).astype(BF16)
        mixed = jnp.concatenate(parts, axis=1) + bs_ref[...]
        mix_s[pl.ds(r0, BLOCK), A_WIDTH:] = (u_s[pl.ds(r0, BLOCK), :] * mixed).astype(BF16)

    def out_cols(rows, c):
        return _dot(mix_s[rows, :], wout_s[:, c * pw:(c + 1) * pw])

    def finish(rows, z_cols, i):
        r = slice(rows.start + i * BLOCK, rows.start + (i + 1) * BLOCK)
        z = ALPHA * x_ref[r, :] + jnp.concatenate([zc[i * BLOCK:(i + 1) * BLOCK, :] for zc in z_cols], axis=1)
        o_ref[r, :] = _layer_norm(z, g1_ref[...], b1_ref[...])

    assert tq == 2 * SLAB_ROWS and SLAB_ROWS == 4 * BLOCK and wout_s.shape[1] == 4 * pw
    sa, sb = slice(0, SLAB_ROWS), slice(SLAB_ROWS, tq)
    tab_a, tab_b = rotary_tables(sa), rotary_tables(sb)
    xa = x_ref[sa, :].astype(BF16)
    ha = [project(xa, k) for k in range(n_pieces)]
    xb = x_ref[sb, :].astype(BF16)
    hb = []
    for k in range(n_pieces):
        hb.append(project(xb, k))
        prepare(sa, ha[k], tab_a, k)
    pieces_b = iter(range(n_pieces))
    for n in range(4):
        block_body(n)
        for k in [next(pieces_b) for _ in range(2 if n < 3 else 1)]:
            prepare(sb, hb[k], tab_b, k)
    za = []
    for n in range(4):
        block_body(4 + n)
        za.append(out_cols(sa, n))
    kv_s[0:BLOCK, :] = kv_s[tq:tq + BLOCK, :]
    zb = []
    for n in range(4):
        zb.append(out_cols(sb, n))
        finish(sa, za, n)
    for n in range(4):
        finish(sb, zb, n)


def _mixer0(x, pos_row, sinks, w_in, invf, etab, cbase, gsum, lng, lnb, w_s, bs_full, w_out, g1, b1,
            b0, bsz, s_len):
    d = x.shape[1]
    t_all = bsz * s_len
    tq = TOKEN_TILE
    nj = s_len // tq
    row = lambda bb, jj: (bb * nj + jj, 0)
    in_w = w_in.shape[1]
    return pl.pallas_call(
        _mixer0_kernel,
        grid=(bsz, nj),
        in_specs=[
            pl.BlockSpec(memory_space=pltpu.SMEM),
            pl.BlockSpec((tq, d), lambda bb, jj: ((b0 + bb) * nj + jj, 0)),
            pl.BlockSpec((1, tq), lambda bb, jj: (0, (b0 + bb) * nj + jj)),
            _const_spec((d, in_w)),
            _const_spec((ROPE_DIM // 2, 1)), _const_spec((LANES, 3 * LANES)), _const_spec((1, LANES)),
            _const_spec((PIECE_COLS, PIECE_COLS)),
            _const_spec((1, B_WIDTH)), _const_spec((1, B_WIDTH)),
            _const_spec((B_GROUPS, BLOCK, BLOCK)),
            _const_spec((BLOCK, B_WIDTH)),
            _const_spec((A_WIDTH + B_WIDTH, d)),
            _const_spec((1, d)), _const_spec((1, d)),
        ],
        out_specs=pl.BlockSpec((tq, d), row),
        out_shape=jax.ShapeDtypeStruct((t_all, d), F32),
        scratch_shapes=[
            pltpu.VMEM((tq, A_WIDTH), BF16),
            pltpu.VMEM((tq + BLOCK, 6 * LANES), BF16),
            pltpu.VMEM((tq, B_WIDTH), F32),
            pltpu.VMEM((tq, B_WIDTH), BF16),
            pltpu.VMEM((tq, A_WIDTH + B_WIDTH), BF16),
            pltpu.VMEM((B_GROUPS, BLOCK, BLOCK), BF16),
            pltpu.VMEM((d, in_w), BF16),
            pltpu.VMEM((A_WIDTH + B_WIDTH, d), BF16),
        ],
        compiler_params=_tc_params(2),
        name="mixer0",
    )(sinks, x, pos_row, w_in, invf, etab, cbase, gsum, lng, lnb, w_s, bs_full, w_out, g1, b1)


def _expert_combine_ln(x2, y0_packed, y1_packed, rt, g, b):
    wt = pltpu.bitcast(rt, F32).T
    y = wt[:, 4:5] * _unpack_bf16_pairs(y0_packed) + wt[:, 5:6] * _unpack_bf16_pairs(y1_packed)
    return _layer_norm(ALPHA * x2 + y, g, b)


def _mixer1_kernel(x2_ref, y0_ref, y1_ref, rt_ref, g3_ref, b3_ref,
                   win_ref, pw_ref, ps_ref, wout_ref, g1_ref, b1_ref, o_ref,
                   h_s, mp_s, win_s, pw_s, wout_s):
    j = pl.program_id(1)
    tq = x2_ref.shape[0]
    gw = x2_ref.shape[1] // len(POOL_WINDOWS)
    slabs = [slice(c * SLAB_ROWS, (c + 1) * SLAB_ROWS) for c in range(tq // SLAB_ROWS)]

    @pl.when(jnp.logical_and(pl.program_id(0) == 0, j == 0))
    def _():
        win_s[...] = win_ref[...].astype(BF16)
        pw_s[...] = pw_ref[...].astype(BF16)
        wout_s[...] = wout_ref[...].astype(BF16)

    @pl.when(j == 0)
    def _():
        h_s[0:POOL_HALO, :] = jnp.zeros((POOL_HALO, h_s.shape[1]), F32)

    ng = len(POOL_WINDOWS)
    quarter = SLAB_ROWS // ng

    def sub(rows, i):
        return slice(rows.start + i * quarter, rows.start + (i + 1) * quarter)

    def load_in(rows, i):
        r = sub(rows, i)
        return _expert_combine_ln(x2_ref[r, :], y0_ref[r, :], y1_ref[r, :], rt_ref[:, r],
                                  g3_ref[...], b3_ref[...])

    def project(rows, xb, g):
        lo, hi = g * gw, (g + 1) * gw
        h_s[POOL_HALO + rows.start:POOL_HALO + rows.stop, lo:hi] = _dot(xb, win_s[:, lo:hi])

    def pool(rows, g):
        win = POOL_WINDOWS[g]
        lo, hi = g * gw, (g + 1) * gw
        n = rows.stop - rows.start
        t_pos = j * tq + rows.start + lax.broadcasted_iota(I32, (n, 1), 0)
        ext = h_s[rows.start:rows.stop + POOL_HALO, lo:hi]
        acc = ext
        shift = 1
        while shift < win:
            acc = acc + pltpu.roll(acc, shift, 0)
            shift *= 2
        count = jnp.minimum(t_pos + 1, win).astype(F32)
        pooled = acc[POOL_HALO:, :] / count - ext[POOL_HALO:, :]
        mapped = _dot(pooled.astype(BF16), pw_s[g])
        mp_s[rows, lo:hi] = (mapped * ps_ref[:, lo:hi]).astype(BF16)

    def out_cols(rows, g):
        return _dot(mp_s[rows, :], wout_s[:, g * gw:(g + 1) * gw])

    def finish(rows, x_parts, z_cols, i):
        lo, hi = i * quarter, (i + 1) * quarter
        z = ALPHA * x_parts[i] + jnp.concatenate([zc[lo:hi, :] for zc in z_cols], axis=1)
        o_ref[sub(rows, i), :] = _layer_norm(z, g1_ref[...], b1_ref[...])

    assert len(slabs) == 2
    sa, sb = slabs
    xa = [load_in(sa, i) for i in range(ng)]
    xa_b = jnp.concatenate(xa, axis=0).astype(BF16)
    xb = []
    for g in range(ng):
        project(sa, xa_b, g)
        xb.append(load_in(sb, g))
    xb_b = jnp.concatenate(xb, axis=0).astype(BF16)
    for g in range(ng):
        project(sb, xb_b, g)
        pool(sa, g)
    za = []
    for g in range(ng):
        za.append(out_cols(sa, g))
        pool(sb, g)
    zb = []
    for g in range(ng):
        zb.append(out_cols(sb, g))
        finish(sa, xa, za, g)
    for g in range(ng):
        finish(sb, xb, zb, g)
    h_s[0:POOL_HALO, :] = h_s[tq:tq + POOL_HALO, :]


def _mixer1(x2, y01, rt, g3, b3, w_in, pool_w, pool_scale, w_out, g1, b1, bsz, s_len):
    t_all, d = x2.shape
    tq = TOKEN_TILE
    nj = s_len // tq
    nt = bsz * nj
    row = lambda bb, jj: (bb * nj + jj, 0)
    ng = len(POOL_WINDOWS)
    return pl.pallas_call(
        _mixer1_kernel,
        grid=(bsz, nj),
        in_specs=[
            pl.BlockSpec((tq, d), row),
            pl.BlockSpec((tq, d // 2), row),
            pl.BlockSpec((tq, d // 2), lambda bb, jj: (nt + bb * nj + jj, 0)),
            pl.BlockSpec((8, tq), lambda bb, jj: (0, bb * nj + jj)),
            _const_spec((1, d)), _const_spec((1, d)),
            _const_spec((d, d)),
            _const_spec((ng, d // ng, d // ng)),
            _const_spec((1, d)),
            _const_spec((d, d)),
            _const_spec((1, d)), _const_spec((1, d)),
        ],
        out_specs=pl.BlockSpec((tq, d), row),
        out_shape=jax.ShapeDtypeStruct((t_all, d), F32),
        scratch_shapes=[pltpu.VMEM((tq + POOL_HALO, d), F32), pltpu.VMEM((tq, d), BF16),
                        pltpu.VMEM((d, d), BF16), pltpu.VMEM((ng, d // ng, d // ng), BF16),
                        pltpu.VMEM((d, d), BF16)],
        compiler_params=_tc_params(2),
        name="mixer1",
    )(x2, y01, y01, rt, g3, b3, w_in, pool_w, pool_scale, w_out, g1, b1)


def _kv_kernel(mem_ref, w_ref, o_ref, w_s):
    @pl.when(pl.program_id(0) == 0)
    def _():
        w_s[...] = w_ref[...].astype(BF16)

    o_ref[...] = _dot(mem_ref[...].astype(BF16), w_s[...]).astype(BF16)


def _kv_proj(mem2d, wkv, bsz):
    rows, d = mem2d.shape
    m = rows // bsz
    return pl.pallas_call(
        _kv_kernel,
        grid=(bsz,),
        in_specs=[pl.BlockSpec((m, d), lambda i: (i, 0)), _const_spec(wkv.shape)],
        out_specs=pl.BlockSpec((m, wkv.shape[1]), lambda i: (i, 0)),
        out_shape=jax.ShapeDtypeStruct((rows, wkv.shape[1]), BF16),
        scratch_shapes=[pltpu.VMEM(wkv.shape, BF16)],
        compiler_params=_tc_params(1),
        name="kv_proj",
    )(mem2d, wkv)


def _top2_of4(v):
    hi01, lo01 = jnp.maximum(v[0], v[1]), jnp.minimum(v[0], v[1])
    hi23, lo23 = jnp.maximum(v[2], v[3]), jnp.minimum(v[2], v[3])
    return jnp.maximum(hi01, hi23) + jnp.maximum(jnp.minimum(hi01, hi23), jnp.maximum(lo01, lo23))


def _argmax_first(vals):
    best, idx = vals[0], jnp.zeros(vals[0].shape, I32)
    for i in range(1, len(vals)):
        better = vals[i] > best
        best = jnp.where(better, vals[i], best)
        idx = jnp.where(better, i, idx)
    return best, idx


def _xattn_kernel(x_ref, kv_ref, wq_ref, wo_ref, g2_ref, b2_ref, rwh_ref, rwl_ref, rb_ref,
                  x2_ref, xp_ref, rt_ref, cnt_ref, run_s, wq_s, wo_s):
    first = jnp.logical_and(pl.program_id(0) == 0, pl.program_id(1) == 0)
    tq, d = x_ref.shape

    @pl.when(first)
    def _():
        run_s[...] = jnp.zeros(run_s.shape, F32)
        wq_s[...] = wq_ref[...].astype(BF16)
        wo_s[...] = wo_ref[...].astype(BF16)

    slabs = [slice(c * SLAB_ROWS, (c + 1) * SLAB_ROWS) for c in range(tq // SLAB_ROWS)]
    run = run_s[...]
    zs = [_xattn_attend(*_xattn_query(rows, x_ref, wq_s), kv_ref, wo_s) for rows in slabs]
    picks = []
    for rows, z in zip(slabs, zs):
        logits = _xattn_norm_logits(rows, z, g2_ref, b2_ref, rwh_ref, rwl_ref, x2_ref, xp_ref)
        picks.append(_xattn_select(logits, rb_ref))
    for rows, pick in zip(slabs, picks):
        run = _xattn_rank(rows, pick, run, rt_ref)
    run_s[...] = run
    cnt_ref[...] = jnp.broadcast_to(run, cnt_ref.shape).astype(I32)


def _xattn_query(rows, x_ref, wq_s):
    hd = x_ref.shape[1] // X_HEADS
    x = x_ref[rows, :]
    return x, _dot(x.astype(BF16), wq_s[...]) * (hd ** -0.5)


def _xattn_attend(x, q, kv_ref, wo_s):
    d = x.shape[1]
    hd = d // X_HEADS
    outs = []
    for hx in range(X_HEADS):
        qh = q[:, hx * hd:(hx + 1) * hd].astype(BF16)
        kh = kv_ref[:, hx * hd:(hx + 1) * hd]
        vh = kv_ref[:, d + hx * hd:d + (hx + 1) * hd]
        s = _dot_nt(qh, kh)
        p = jnp.exp(s - jnp.max(s, axis=-1, keepdims=True))
        o = _dot(p.astype(BF16), vh) / jnp.sum(p, axis=-1, keepdims=True)
        outs.append(o.astype(BF16))
    return ALPHA * x + _dot(jnp.concatenate(outs, axis=1), wo_s[...])


def _xattn_norm_logits(rows, z, g2_ref, b2_ref, rwh_ref, rwl_ref, x2_ref, xp_ref):
    x2 = _layer_norm(z, g2_ref[...], b2_ref[...])
    x2_ref[rows, :] = x2

    xp_ref[rows, :] = _pack_bf16_pairs(x2)

    xh, xl = _split_bf16(x2)
    both = _dot_nt(jnp.concatenate([rwh_ref[...], rwl_ref[...]], axis=0), xh)
    return both[0:N_EXPERTS] + both[N_EXPERTS:] + _dot_nt(rwh_ref[...], xl)


def _xattn_select(logits, rb_ref):
    e_max = jnp.max(logits, axis=0, keepdims=True)
    ex = jnp.exp(logits - e_max)
    scores = ex / jnp.sum(ex, axis=0, keepdims=True)
    biased = scores + rb_ref[...]
    sc = [scores[e:e + 1, :] for e in range(N_EXPERTS)]
    bi = [biased[e:e + 1, :] for e in range(N_EXPERTS)]
    epg = EXPERTS_PER_GROUP
    gscore = [_top2_of4(bi[g * epg:(g + 1) * epg]) for g in range(N_EXPERT_GROUPS)]
    _, gsel = _argmax_first(gscore)

    def pick(vals):
        return [functools.reduce(lambda acc, g: jnp.where(gsel == g, vals[g * epg + i], acc),
                                 range(1, N_EXPERT_GROUPS), vals[i]) for i in range(epg)]

    in_b = pick(bi)
    in_s = pick(sc)
    _, i0 = _argmax_first(in_b)
    _, i1 = _argmax_first([jnp.where(i0 == i, -jnp.inf, in_b[i]) for i in range(epg)])

    def take(vals, idx):
        return functools.reduce(lambda acc, i: jnp.where(idx == i, vals[i], acc), range(1, epg), vals[0])

    s0, s1 = take(in_s, i0), take(in_s, i1)
    w0, w1 = s0 / (s0 + s1), s1 / (s0 + s1)
    return gsel * epg + i0, gsel * epg + i1, w0, w1


def _xattn_rank(rows, selection, run, rt_ref):
    e0, e1, w0, w1 = selection
    tq = rows.stop - rows.start
    eid = lax.broadcasted_iota(I32, (N_EXPERTS, tq), 0)
    oh0 = eid == e0
    oh1 = eid == e1
    onehot = jnp.where(jnp.logical_or(oh0, oh1), 1.0, 0.0)
    rr = lax.broadcasted_iota(I32, (tq, tq), 0)
    cc = lax.broadcasted_iota(I32, (tq, tq), 1)
    upper = jnp.where(rr < cc, 1.0, 0.0).astype(BF16)
    prefix = _dot(onehot.astype(BF16), upper) + run
    r0 = jnp.sum(jnp.where(oh0, prefix, 0.0), axis=0, keepdims=True).astype(I32)
    r1 = jnp.sum(jnp.where(oh1, prefix, 0.0), axis=0, keepdims=True).astype(I32)

    zero = jnp.zeros((1, tq), I32)
    rt_ref[:, rows] = jnp.concatenate(
        [e0, e1, r0, r1, pltpu.bitcast(w0, I32), pltpu.bitcast(w1, I32), zero, zero], axis=0)
    return run + jnp.sum(onehot, axis=1, keepdims=True)


def _xattn_route(x1, kv, mem_len, wq, wo, g2, b2, rw_hi, rw_lo, rbias, b0, bsz, s_len):
    t_all, d = x1.shape
    tq = TOKEN_TILE
    nj = s_len // tq
    m = mem_len
    row = lambda bb, jj: (bb * nj + jj, 0)
    return pl.pallas_call(
        _xattn_kernel,
        grid=(bsz, nj),
        in_specs=[
            pl.BlockSpec((tq, d), row),
            pl.BlockSpec((m, 2 * d), lambda bb, jj: (b0 + bb, 0)),
            _const_spec((d, d)), _const_spec((d, d)),
            _const_spec((1, d)), _const_spec((1, d)),
            _const_spec((N_EXPERTS, d)), _const_spec((N_EXPERTS, d)),
            _const_spec((N_EXPERTS, 1)),
        ],
        out_specs=[
            pl.BlockSpec((tq, d), row),
            pl.BlockSpec((tq, d // 2), row),
            pl.BlockSpec((8, tq), lambda bb, jj: (0, bb * nj + jj)),
            _const_spec((N_EXPERTS, LANES)),
        ],
        out_shape=[
            jax.ShapeDtypeStruct((t_all, d), F32),
            jax.ShapeDtypeStruct((t_all, d // 2), I32),
            jax.ShapeDtypeStruct((8, t_all), I32),
            jax.ShapeDtypeStruct((N_EXPERTS, LANES), I32),
        ],
        scratch_shapes=[pltpu.VMEM((N_EXPERTS, 1), F32), pltpu.VMEM((d, d), BF16), pltpu.VMEM((d, d), BF16)],
        compiler_params=_tc_params(2),
        name="xattn_route",
    )(x1, kv, wq, wo, g2, b2, rw_hi, rw_lo, rbias)


def _sc_mesh():
    return plsc.VectorSubcoreMesh(core_axis_name="c", subcore_axis_name="s")


def _sc_params():
    return pltpu.CompilerParams(needs_layout_passes=False)


def _worker_id():
    return lax.axis_index("s") * lax.axis_size("c") + lax.axis_index("c")


def _sc_dispatch(xp, dest2d, n_rows):
    t_all, width = xp.shape
    chunk = dest2d.shape[1]
    tok_w = t_all // SC_WORKERS
    nch = tok_w // chunk
    slot1 = t_all // chunk
    assert t_all % (SC_WORKERS * chunk * 2) == 0

    def body(x_hbm, dest_hbm, out_hbm, idx0_v, idx1_v, buf0, buf1, sem_r, sem_w):
        wid = _worker_id()
        base = wid * tok_w
        pltpu.sync_copy(dest_hbm.at[pl.ds(wid * nch, nch)], idx0_v)
        pltpu.sync_copy(dest_hbm.at[pl.ds(slot1 + wid * nch, nch)], idx1_v)

        def read(c, buf, k):
            return pltpu.make_async_copy(x_hbm.at[pl.ds(base + c * chunk, chunk)], buf, sem_r.at[k])

        def scatter(c, buf):
            a = pltpu.make_async_copy(buf, out_hbm.at[idx0_v.at[c]], sem_w.at[0])
            b = pltpu.make_async_copy(buf, out_hbm.at[idx1_v.at[c]], sem_w.at[1])
            a.start()
            b.start()
            a.wait()
            b.wait()

        read(0, buf0, 0).start()

        @pl.loop(0, nch // 2)
        def _(g):
            c = 2 * g
            read(c + 1, buf1, 1).start()
            read(c, buf0, 0).wait()
            scatter(c, buf0)

            @pl.when(c + 2 < nch)
            def _():
                read(c + 2, buf0, 0).start()

            read(c + 1, buf1, 1).wait()
            scatter(c + 1, buf1)

    return pl.kernel(
        body,
        out_type=jax.ShapeDtypeStruct((n_rows, width), xp.dtype),
        mesh=_sc_mesh(),
        scratch_types=[
            pltpu.VMEM((nch, chunk), I32),
            pltpu.VMEM((nch, chunk), I32),
            pltpu.VMEM((chunk, width), xp.dtype),
            pltpu.VMEM((chunk, width), xp.dtype),
            pltpu.SemaphoreType.DMA((2,)),
            pltpu.SemaphoreType.DMA((2,)),
        ],
        compiler_params=_sc_params(),
        name="sc_dispatch",
    )(xp, dest2d)


def _sc_combine(ys, dest2d):
    n_idx_rows, chunk = dest2d.shape
    width = ys.shape[1]
    nch = n_idx_rows // SC_WORKERS
    assert n_idx_rows % (SC_WORKERS * 2) == 0

    def body(y_hbm, dest_hbm, out_hbm, idx_v, buf0, buf1, sem_g):
        wid = _worker_id()
        base = wid * nch * chunk
        pltpu.sync_copy(dest_hbm.at[pl.ds(wid * nch, nch)], idx_v)

        def gather(c, buf, k):
            return pltpu.make_async_copy(y_hbm.at[idx_v.at[c]], buf, sem_g.at[k])

        def write(c, buf):
            pltpu.sync_copy(buf, out_hbm.at[pl.ds(base + c * chunk, chunk)])

        gather(0, buf0, 0).start()

        @pl.loop(0, nch // 2)
        def _(g):
            c = 2 * g
            gather(c + 1, buf1, 1).start()
            gather(c, buf0, 0).wait()
            write(c, buf0)

            @pl.when(c + 2 < nch)
            def _():
                gather(c + 2, buf0, 0).start()

            gather(c + 1, buf1, 1).wait()
            write(c + 1, buf1)

    return pl.kernel(
        body,
        out_type=jax.ShapeDtypeStruct((n_idx_rows * chunk, width), ys.dtype),
        mesh=_sc_mesh(),
        scratch_types=[
            pltpu.VMEM((nch, chunk), I32),
            pltpu.VMEM((chunk, width), ys.dtype),
            pltpu.VMEM((chunk, width), ys.dtype),
            pltpu.SemaphoreType.DMA((2,)),
        ],
        compiler_params=_sc_params(),
        name="sc_combine",
    )(ys, dest2d)


def _pack_bf16_pairs(v):
    half = v.shape[1] // 2
    lo = pltpu.bitcast(v[:, :half].astype(BF16).astype(F32), jnp.uint32) >> 16
    hi = pltpu.bitcast(v[:, half:].astype(BF16).astype(F32), jnp.uint32) & jnp.uint32(0xFFFF0000)
    return pltpu.bitcast(hi | lo, I32)


def _unpack_bf16_pairs(w):
    w = pltpu.bitcast(w, jnp.uint32)
    lo = pltpu.bitcast(w << 16, F32)
    hi = pltpu.bitcast(w & jnp.uint32(0xFFFF0000), F32)
    return jnp.concatenate([lo, hi], axis=1)


def _ffn_kernel(be_ref, nv_ref, nu_ref, nxt_ref, xs_ref, wg_hbm, wu_hbm, wd_hbm, o_ref,
                wg_f, wu_f, wd_f, wg_s, wu_s, wd_s, sem):
    i = pl.program_id(0)

    def fetch(e):
        return (pltpu.make_async_copy(wg_hbm.at[e], wg_f, sem.at[0]),
                pltpu.make_async_copy(wu_hbm.at[e], wu_f, sem.at[1]),
                pltpu.make_async_copy(wd_hbm.at[e], wd_f, sem.at[2]))

    @pl.when(nxt_ref[i] >= 0)
    def _():
        @pl.when(i == 0)
        def _():
            for cp in fetch(be_ref[0]):
                cp.start()

        for cp in fetch(be_ref[i]):
            cp.wait()
        wg_s[...] = wg_f[...].astype(BF16)
        wu_s[...] = wu_f[...].astype(BF16)
        wd_s[...] = wd_f[...].astype(BF16)

        @pl.when(nxt_ref[i] < N_EXPERTS)
        def _():
            for cp in fetch(nxt_ref[i]):
                cp.start()

    @pl.when(i < nu_ref[0])
    def _():
        live = lax.broadcasted_iota(I32, xs_ref.shape, 0) < nv_ref[i]
        xb = _unpack_bf16_pairs(jnp.where(live, xs_ref[...], 0)).astype(BF16)
        act = jax.nn.silu(_dot(xb, wg_s[...])) * _dot(xb, wu_s[...])
        o_ref[...] = _pack_bf16_pairs(_dot(act.astype(BF16), wd_s[...]))


def _expert_ffn(xs, blk_expert, blk_valid, n_used, blk_next, w_gate, w_up, w_down):
    n_rows, half = xs.shape
    d = 2 * half
    de = w_gate.shape[2]
    bm = EXPERT_ROWS
    rows = lambda i, be, nv, nu, nx: (jnp.minimum(i, nu[0] - 1), 0)
    hbm = pl.BlockSpec(memory_space=pl.ANY)
    return pl.pallas_call(
        _ffn_kernel,
        grid_spec=pltpu.PrefetchScalarGridSpec(
            num_scalar_prefetch=4,
            grid=(n_rows // bm,),
            in_specs=[pl.BlockSpec((bm, half), rows), hbm, hbm, hbm],
            out_specs=pl.BlockSpec((bm, half), rows),
            scratch_shapes=[
                pltpu.VMEM((d, de), F32), pltpu.VMEM((d, de), F32), pltpu.VMEM((de, d), F32),
                pltpu.VMEM((d, de), BF16), pltpu.VMEM((d, de), BF16), pltpu.VMEM((de, d), BF16),
                pltpu.SemaphoreType.DMA((3,)),
            ],
        ),
        out_shape=jax.ShapeDtypeStruct((n_rows, half), I32),
        compiler_params=_tc_params(1),
        name="expert_ffn",
    )(blk_expert, blk_valid, n_used, blk_next, xs, w_gate, w_up, w_down)


def _combine_kernel(x_ref, y0_ref, y1_ref, rt_ref, g_ref, b_ref, o_ref):
    o_ref[...] = _expert_combine_ln(x_ref[...], y0_ref[...], y1_ref[...], rt_ref[...], g_ref[...], b_ref[...])


def _combine_kernel_into(x_ref, y0_ref, y1_ref, rt_ref, g_ref, b_ref, full_ref, o_ref):
    del full_ref
    _combine_kernel(x_ref, y0_ref, y1_ref, rt_ref, g_ref, b_ref, o_ref)


def _combine_ln(x2, y01, rt, g3, b3, into=None, row0=0, full_rows=None):
    t_all, d = x2.shape
    tq = TOKEN_TILE
    nt = t_all // tq
    blk0 = row0 // tq if full_rows else 0
    in_specs = [
        pl.BlockSpec((tq, d), lambda i: (i, 0)),
        pl.BlockSpec((tq, d // 2), lambda i: (i, 0)),
        pl.BlockSpec((tq, d // 2), lambda i: (i + nt, 0)),
        pl.BlockSpec((8, tq), lambda i: (0, i)),
        _const_spec((1, d)), _const_spec((1, d)),
    ]
    args = [x2, y01, y01, rt, g3, b3]
    body, aliases = _combine_kernel, {}
    if into is not None:
        in_specs.append(pl.BlockSpec(memory_space=pl.ANY))
        args.append(into)
        body, aliases = _combine_kernel_into, {len(args) - 1: 0}
    return pl.pallas_call(
        body,
        grid=(nt,),
        in_specs=in_specs,
        out_specs=pl.BlockSpec((tq, d), lambda i: (i + blk0, 0)),
        out_shape=jax.ShapeDtypeStruct((full_rows or t_all, d), F32),
        input_output_aliases=aliases,
        compiler_params=_tc_params(1),
        name="combine_ln",
    )(*args)


def _routing_tables(rt, counts):
    bm = EXPERT_ROWS
    t_all = rt.shape[1]
    n_rows = 2 * t_all + N_EXPERTS * bm
    cnt = counts[:, 0]
    padded = (cnt + bm - 1) // bm * bm
    ends = jnp.cumsum(padded)
    offs = ends - padded
    experts = rt[0:2]
    off_tok = jnp.sum(jnp.where(experts[None] == jnp.arange(N_EXPERTS, dtype=I32)[:, None, None],
                                offs[:, None, None], 0), axis=0)
    dest2d = (off_tok + rt[2:4]).reshape(-1, SC_CHUNK).astype(I32)
    blk_start = jnp.arange(n_rows // bm, dtype=I32) * bm
    blk_expert = jnp.minimum(jnp.sum(blk_start[:, None] >= ends[None, :], axis=1), N_EXPERTS - 1).astype(I32)
    live_end = jnp.sum(jnp.where(blk_expert[:, None] == jnp.arange(N_EXPERTS, dtype=I32)[None, :],
                                 (offs + cnt)[None, :], 0), axis=1)
    blk_valid = jnp.clip(live_end - blk_start, 0, bm).astype(I32)
    n_used = (ends[-1:] // bm).astype(I32)
    eid = jnp.arange(N_EXPERTS, dtype=I32)
    later_present = jnp.logical_and(eid[None, :] > eid[:, None], (cnt > 0)[None, :])
    next_present = jnp.min(jnp.where(later_present, eid[None, :], N_EXPERTS), axis=1)
    next_of_blk = jnp.sum(jnp.where(blk_expert[:, None] == eid[None, :], next_present[None, :], 0), axis=1)
    prev_expert = jnp.concatenate([jnp.full((1,), -1, I32), blk_expert[:-1]])
    is_first = jnp.logical_and(blk_start < ends[-1], blk_expert != prev_expert)
    blk_next = jnp.where(is_first, next_of_blk, -1).astype(I32)
    return dest2d, blk_expert, blk_valid, n_used, blk_next, n_rows


def _layer_tail(x1, kv, mem_len, p, router, b0, bsz, s_len):
    rw_hi, rw_lo, rbias = router
    x2, xp, rt, counts = _xattn_route(x1, kv, mem_len, p["xq"], p["xo"], p["ln2_g"], p["ln2_b"],
                                      rw_hi, rw_lo, rbias, b0, bsz, s_len)
    dest2d, blk_expert, blk_valid, n_used, blk_next, n_rows = _routing_tables(rt, counts)
    xs = _sc_dispatch(xp, dest2d, n_rows)
    ys = _expert_ffn(xs, blk_expert, blk_valid, n_used, blk_next, p["e_gate"], p["e_up"], p["e_down"])
    y01 = _sc_combine(ys, dest2d)
    return x2, y01, rt


def _row(v):
    return v.reshape(1, -1).astype(F32)


def _common_params(xq, xkv, xo, ln2_g, ln2_b, e_gate, e_up, e_down, ln3_g, ln3_b):
    return dict(xq=xq, xkv=xkv, xo=xo, ln2_g=_row(ln2_g), ln2_b=_row(ln2_b),
                e_gate=e_gate, e_up=e_up, e_down=e_down, ln3_g=_row(ln3_g), ln3_b=_row(ln3_b))


def kernel(x, mem, positions, router_w, router_bias, l0_w_in, l0_sinks, l0_sgu_ln_g, l0_sgu_ln_b, l0_sgu_w, l0_sgu_b, l0_w_out, l0_ln1_g, l0_ln1_b, l0_xq, l0_xkv, l0_xo, l0_ln2_g, l0_ln2_b, l0_e_gate, l0_e_up, l0_e_down, l0_ln3_g, l0_ln3_b, l1_w_in, l1_pool_w, l1_pool_scale, l1_w_out, l1_ln1_g, l1_ln1_b, l1_xq, l1_xkv, l1_xo, l1_ln2_g, l1_ln2_b, l1_e_gate, l1_e_up, l1_e_down, l1_ln3_g, l1_ln3_b):
    bsz, s_len, d = x.shape
    assert s_len % TOKEN_TILE == 0 and TOKEN_TILE == 2 * SLAB_ROWS and SLAB_ROWS % BLOCK == 0
    xt = x.reshape(bsz * s_len, d)
    mem2d = mem.reshape(-1, d)

    rw_t = router_w.T.astype(F32)
    rw_hi = rw_t.astype(BF16)
    rw_lo = (rw_t - rw_hi.astype(F32)).astype(BF16)
    router = (rw_hi, rw_lo, router_bias.reshape(-1, 1).astype(F32))

    half = ROPE_DIM // 2
    inv_freq = (ROPE_THETA ** (-(jnp.arange(half, dtype=F32) * 2.0 / ROPE_DIM))).reshape(half, 1)
    etab_np = np.zeros((LANES, 3 * LANES), np.float32)
    cbase_np = np.ones((1, LANES), np.float32)
    for ln in range(LANES):
        dd = ln % HEAD_DIM
        if dd < ROPE_DIM:
            cbase_np[0, ln] = 0.0
            etab_np[[dd % half, half + dd % half], ln] = 1.0
            if dd >= half:
                etab_np[[2 * half + dd - half, 3 * half + dd - half], LANES + ln] = 1.0
            else:
                etab_np[[2 * half + dd, 3 * half + dd], 2 * LANES + ln] = -1.0
    etab = jnp.asarray(etab_np, BF16)
    cbase = jnp.asarray(cbase_np)
    pos_row = positions.reshape(1, -1).astype(I32)
    grp = jnp.arange(PIECE_COLS) // B_GROUP_DIM
    gsum = (grp[:, None] == grp[None, :]).astype(BF16)
    bs_full = jnp.repeat(l0_sgu_b.T.astype(F32), B_GROUP_DIM, axis=1)

    p0 = _common_params(l0_xq, l0_xkv, l0_xo, l0_ln2_g, l0_ln2_b, l0_e_gate, l0_e_up, l0_e_down,
                        l0_ln3_g, l0_ln3_b)
    p1 = _common_params(l1_xq, l1_xkv, l1_xo, l1_ln2_g, l1_ln2_b, l1_e_gate, l1_e_up, l1_e_down,
                        l1_ln3_g, l1_ln3_b)
    kv0 = _kv_proj(mem2d, p0["xkv"], bsz)
    kv1 = _kv_proj(mem2d, p1["xkv"], bsz)
    mem_len = mem.shape[1]

    n_split = BATCH_SPLIT if bsz % BATCH_SPLIT == 0 else 1
    nb = bsz // n_split
    out = None
    for part in range(n_split):
        b0 = part * nb
        x1 = _mixer0(xt, pos_row, l0_sinks.astype(F32), l0_w_in, inv_freq, etab, cbase, gsum,
                     _row(l0_sgu_ln_g), _row(l0_sgu_ln_b), l0_sgu_w.astype(F32), bs_full,
                     l0_w_out, _row(l0_ln1_g), _row(l0_ln1_b), b0, nb, s_len)
        x2, y01, rt = _layer_tail(x1, kv0, mem_len, p0, router, b0, nb, s_len)
        x1 = _mixer1(x2, y01, rt, p0["ln3_g"], p0["ln3_b"], l1_w_in, l1_pool_w, _row(l1_pool_scale),
                     l1_w_out, _row(l1_ln1_g), _row(l1_ln1_b), nb, s_len)
        x2, y01, rt = _layer_tail(x1, kv1, mem_len, p1, router, b0, nb, s_len)
        out = _combine_ln(x2, y01, rt, p1["ln3_g"], p1["ln3_b"], into=out, row0=b0 * s_len,
                          full_rows=bsz * s_len)
    return out.reshape(bsz, s_len, d)
```

```python
import functools

import numpy as np
import jax
import jax.numpy as jnp
from jax import lax
from jax.experimental import pallas as pl
from jax.experimental.pallas import tpu as pltpu
from jax.experimental.pallas import tpu_sc as plsc

F32 = jnp.float32
BF16 = jnp.bfloat16
I32 = jnp.int32

DEPTH = 2
ALPHA = (2.0 * DEPTH) ** 0.25
LN_EPS = 1e-5

HEAD_DIM = 64
A_Q_HEADS = 8
A_KV_HEADS = 2
A_GROUP = A_Q_HEADS // A_KV_HEADS
BLOCK = 128
ROPE_THETA = 500000.0
ROPE_DIM = HEAD_DIM // 4
A_WIDTH = A_Q_HEADS * HEAD_DIM
KV_WIDTH = A_KV_HEADS * HEAD_DIM
B_GROUPS = 8
B_GROUP_DIM = 64
B_WIDTH = B_GROUPS * B_GROUP_DIM
POOL_WINDOWS = (2, 4, 8, 16)
POOL_HALO = 16
X_HEADS = 4
N_EXPERTS = 16
N_EXPERT_GROUPS = 4
EXPERTS_PER_GROUP = 4

LANES = 128
TOKEN_TILE = 1024
SLAB_ROWS = 512
PIECE_COLS = 256
EXPERT_ROWS = 512
BATCH_SPLIT = 2
SC_WORKERS = 32
SC_CHUNK = 64
VMEM_LIMIT = 56 * 1024 * 1024
NEG_BIG = -1e30


def _layer_norm(z, g, b):
    mu = jnp.mean(z, axis=-1, keepdims=True)
    d = z - mu
    var = jnp.mean(d * d, axis=-1, keepdims=True)
    return d * lax.rsqrt(var + LN_EPS) * g + b


def _dot(a, b):
    return jnp.dot(a, b, preferred_element_type=F32)


def _dot_nt(a, b):
    return lax.dot_general(a, b, (((1,), (1,)), ((), ())), preferred_element_type=F32)


def _split_bf16(v):
    hi = v.astype(BF16)
    lo = (v - hi.astype(F32)).astype(BF16)
    return hi, lo


def _tc_params(n_axes):
    return pltpu.CompilerParams(dimension_semantics=("arbitrary",) * n_axes,
                                vmem_limit_bytes=VMEM_LIMIT)


def _const_spec(shape):
    nd = len(shape)
    return pl.BlockSpec(shape, lambda *_: (0,) * nd, pipeline_mode=pl.Buffered(1))


def _mixer0_kernel(sinks_ref, x_ref, pos_ref, win_ref, invf_ref, etab_ref, cbase_ref, gsum_ref,
                   lng_ref, lnb_ref, ws_ref, bs_ref, wout_ref, g1_ref, b1_ref,
                   o_ref, q_s, kv_s, u_s, vn_s, mix_s, wt_s, win_s, wout_s):
    b = pl.program_id(0)
    j = pl.program_id(1)
    tq = x_ref.shape[0]
    kvw = kv_s.shape[1]

    @pl.when(jnp.logical_and(b == 0, j == 0))
    def _():
        win_s[...] = win_ref[...].astype(BF16)
        wout_s[...] = wout_ref[...].astype(BF16)
        r = lax.broadcasted_iota(I32, (BLOCK, BLOCK), 0)
        c = lax.broadcasted_iota(I32, (BLOCK, BLOCK), 1)
        for g in range(B_GROUPS):
            wt_s[g] = jnp.where(c <= r, ws_ref[g], 0.0).astype(BF16)

    @pl.when(j == 0)
    def _():
        kv_s[0:BLOCK, :] = jnp.zeros((BLOCK, kvw), BF16)

    c1 = A_WIDTH
    c2 = c1 + KV_WIDTH
    c3 = c2 + KV_WIDTH
    c4 = c3 + B_WIDTH

    def rotary_tables(rows):
        n = rows.stop - rows.start
        ang = invf_ref[...] * pos_ref[:, rows].astype(F32)
        c8 = jnp.cos(ang)
        s8 = jnp.sin(ang)
        c8h = c8.astype(BF16).astype(F32)
        s8h = s8.astype(BF16).astype(F32)
        pad = jnp.zeros((LANES - 4 * c8.shape[0], n), F32)
        stack = jnp.concatenate([c8h, c8 - c8h, s8h, s8 - s8h, pad], axis=0)
        tabs = _dot(stack.T.astype(BF16), etab_ref[...])
        return tabs[:, 0:LANES] + cbase_ref[...], tabs[:, LANES:2 * LANES], tabs[:, 2 * LANES:]

    pw = PIECE_COLS
    n_pieces = win_s.shape[1] // pw
    assert (c1 // pw, c2 // pw, c3 // pw, c4 // pw) == (2, 2, 3, 5) and c3 % pw == 0 and n_pieces == 7

    def project(xb, k):
        return _dot(xb, win_s[:, k * pw:(k + 1) * pw])

    def prepare(rows, hk, tables, k):
        n = rows.stop - rows.start
        cs, sa, sb = tables

        def rope(t):
            return t * cs + pltpu.roll(t, ROPE_DIM // 2, 1) * sa + pltpu.roll(t, LANES - ROPE_DIM // 2, 1) * sb

        if k < 2:
            for c in range(pw // LANES):
                t = hk[:, c * LANES:(c + 1) * LANES] * (HEAD_DIM ** -0.5)
                col = k * (pw // LANES) + c
                q_s[rows, col * LANES:(col + 1) * LANES] = rope(t).astype(BF16)
        elif k == 2:
            low = lax.broadcasted_iota(I32, (n, LANES), 1) < HEAD_DIM
            kr = rope(hk[:, 0:KV_WIDTH])
            kx = pltpu.roll(kr, HEAD_DIM, 1)
            vr = hk[:, KV_WIDTH:]
            vx = pltpu.roll(vr, HEAD_DIM, 1)
            kv_cols = [jnp.where(low, kr, kx), jnp.where(low, kx, kr),
                       jnp.where(low, vr, 0.0), jnp.where(low, 0.0, vx),
                       jnp.where(low, vx, 0.0), jnp.where(low, 0.0, vr)]
            for c, col in enumerate(kv_cols):
                kv_s[BLOCK + rows.start:BLOCK + rows.stop, c * LANES:(c + 1) * LANES] = col.astype(BF16)
        elif k < 5:
            lo = (k - 3) * pw
            u_s[rows, lo:lo + pw] = jax.nn.gelu(hk)
        else:
            lo = (k - 5) * pw
            v = jax.nn.gelu(hk)
            gsum = gsum_ref[...]
            mean = _dot(v.astype(BF16), gsum) * (1.0 / B_GROUP_DIM)
            d = v - mean
            var = _dot((d * d).astype(BF16), gsum) * (1.0 / B_GROUP_DIM)
            vn_s[rows, lo:lo + pw] = (d * lax.rsqrt(var + LN_EPS) * lng_ref[:, lo:lo + pw]
                                      + lnb_ref[:, lo:lo + pw]).astype(BF16)

    qi = lax.broadcasted_iota(I32, (BLOCK, 2 * BLOCK), 0)
    kj = lax.broadcasted_iota(I32, (BLOCK, 2 * BLOCK), 1)
    rel = qi + BLOCK - kj
    band = jnp.logical_and(rel >= 0, rel < BLOCK)
    low_q = lax.broadcasted_iota(I32, (BLOCK, LANES), 1) < HEAD_DIM
    low_k = lax.broadcasted_iota(I32, (2 * BLOCK, LANES), 1) < HEAD_DIM
    ones_lo = jnp.where(low_k, 1.0, 0.0).astype(BF16)
    ones_hi = jnp.where(low_k, 0.0, 1.0).astype(BF16)
    zero_q = jnp.zeros((BLOCK, LANES), BF16)

    def block_body(n):
        r0 = n * BLOCK
        kv = kv_s[pl.ds(r0, 2 * BLOCK), :]
        qb = q_s[pl.ds(r0, BLOCK), :]
        valid = jnp.logical_and(band, kj >= jnp.where(j == 0, BLOCK, 0)) if n == 0 else band
        cols_per_kv = A_GROUP // 2
        scores = {}
        for hk in range(A_KV_HEADS):
            cols = range(hk * cols_per_kv, (hk + 1) * cols_per_kv)
            pieces = []
            for c in cols:
                qp = qb[:, c * LANES:(c + 1) * LANES]
                pieces += [jnp.where(low_q, qp, zero_q), jnp.where(low_q, zero_q, qp)]
            sc = _dot_nt(jnp.concatenate(pieces, axis=0), kv[:, hk * LANES:(hk + 1) * LANES])
            for i, c in enumerate(cols):
                for half in range(2):
                    r = (2 * i + half) * BLOCK
                    scores[c, half] = sc[r:r + BLOCK, :]
        vnb = vn_s[pl.ds(r0, BLOCK), :]
        parts = []
        for c in range(B_WIDTH // LANES):
            vp = vnb[:, c * LANES:(c + 1) * LANES]
            parts.append(_dot(wt_s[2 * c], jnp.where(low_q, vp, zero_q))
                         + _dot(wt_s[2 * c + 1], jnp.where(low_q, zero_q, vp)))
        probs, esink = {}, {}
        for (c, half), sc in scores.items():
            s = jnp.where(valid, sc, NEG_BIG)
            sink = sinks_ref[2 * c + half]
            m = jnp.maximum(jnp.max(s, axis=-1, keepdims=True), sink)
            probs[c, half] = jnp.exp(s - m).astype(BF16)
            esink[c, half] = jnp.exp(sink - m)
        res = {}
        for hk in range(A_KV_HEADS):
            cols = range(hk * cols_per_kv, (hk + 1) * cols_per_kv)
            for half in range(2):
                vcol = kv[:, (2 + 2 * hk + half) * LANES:(3 + 2 * hk + half) * LANES]
                vm = jnp.concatenate([vcol, ones_lo if half == 0 else ones_hi], axis=1)
                pv = _dot(jnp.concatenate([probs[c, half] for c in cols], axis=0), vm)
                for i, c in enumerate(cols):
                    part = pv[i * BLOCK:(i + 1) * BLOCK, :]
                    res[c] = part if half == 0 else res[c] + part
        for c in range(A_WIDTH // LANES):
            den = res[c][:, LANES:] + jnp.where(low_q, esink[c, 0], esink[c, 1])
            mix_s[pl.ds(r0, BLOCK), c * LANES:(c + 1) * LANES] = (res[c][:, :LANES] / den).astype(BF16)
        mixed = jnp.concatenate(parts, axis=1) + bs_ref[...]
        mix_s[pl.ds(r0, BLOCK), A_WIDTH:] = (u_s[pl.ds(r0, BLOCK), :] * mixed).astype(BF16)

    def out_cols(rows, c):
        return _dot(mix_s[rows, :], wout_s[:, c * pw:(c + 1) * pw])

    def finish(rows, z_cols, i):
        r = slice(rows.start + i * BLOCK, rows.start + (i + 1) * BLOCK)
        z = ALPHA * x_ref[r, :] + jnp.concatenate([zc[i * BLOCK:(i + 1) * BLOCK, :] for zc in z_cols], axis=1)
        o_ref[r, :] = _layer_norm(z, g1_ref[...], b1_ref[...])

    assert tq == 2 * SLAB_ROWS and SLAB_ROWS == 4 * BLOCK and wout_s.shape[1] == 4 * pw
    sa, sb = slice(0, SLAB_ROWS), slice(SLAB_ROWS, tq)
    tab_a, tab_b = rotary_tables(sa), rotary_tables(sb)
    xa = x_ref[sa, :].astype(BF16)
    ha = [project(xa, k) for k in range(n_pieces)]
    xb = x_ref[sb, :].astype(BF16)
    hb = []
    for k in range(n_pieces):
        hb.append(project(xb, k))
        prepare(sa, ha[k], tab_a, k)
    pieces_b = iter(range(n_pieces))
    for n in range(4):
        block_body(n)
        for k in [next(pieces_b) for _ in range(2 if n < 3 else 1)]:
            prepare(sb, hb[k], tab_b, k)
    za = []
    for n in range(4):
        block_body(4 + n)
        za.append(out_cols(sa, n))
    kv_s[0:BLOCK, :] = kv_s[tq:tq + BLOCK, :]
    zb = []
    for n in range(4):
        zb.append(out_cols(sb, n))
        finish(sa, za, n)
    for n in range(4):
        finish(sb, zb, n)


def _mixer0(x, pos_row, sinks, w_in, invf, etab, cbase, gsum, lng, lnb, w_s, bs_full, w_out, g1, b1,
            b0, bsz, s_len):
    d = x.shape[1]
    t_all = bsz * s_len
    tq = TOKEN_TILE
    nj = s_len // tq
    row = lambda bb, jj: (bb * nj + jj, 0)
    in_w = w_in.shape[1]
    return pl.pallas_call(
        _mixer0_kernel,
        grid=(bsz, nj),
        in_specs=[
            pl.BlockSpec(memory_space=pltpu.SMEM),
            pl.BlockSpec((tq, d), lambda bb, jj: ((b0 + bb) * nj + jj, 0)),
            pl.BlockSpec((1, tq), lambda bb, jj: (0, (b0 + bb) * nj + jj)),
            _const_spec((d, in_w)),
            _const_spec((ROPE_DIM // 2, 1)), _const_spec((LANES, 3 * LANES)), _const_spec((1, LANES)),
            _const_spec((PIECE_COLS, PIECE_COLS)),
            _const_spec((1, B_WIDTH)), _const_spec((1, B_WIDTH)),
            _const_spec((B_GROUPS, BLOCK, BLOCK)),
            _const_spec((BLOCK, B_WIDTH)),
            _const_spec((A_WIDTH + B_WIDTH, d)),
            _const_spec((1, d)), _const_spec((1, d)),
        ],
        out_specs=pl.BlockSpec((tq, d), row),
        out_shape=jax.ShapeDtypeStruct((t_all, d), F32),
        scratch_shapes=[
            pltpu.VMEM((tq, A_WIDTH), BF16),
            pltpu.VMEM((tq + BLOCK, 6 * LANES), BF16),
            pltpu.VMEM((tq, B_WIDTH), F32),
            pltpu.VMEM((tq, B_WIDTH), BF16),
            pltpu.VMEM((tq, A_WIDTH + B_WIDTH), BF16),
            pltpu.VMEM((B_GROUPS, BLOCK, BLOCK), BF16),
            pltpu.VMEM((d, in_w), BF16),
            pltpu.VMEM((A_WIDTH + B_WIDTH, d), BF16),
        ],
        compiler_params=_tc_params(2),
        name="mixer0",
    )(sinks, x, pos_row, w_in, invf, etab, cbase, gsum, lng, lnb, w_s, bs_full, w_out, g1, b1)


def _expert_combine_ln(x2, y0_packed, y1_packed, rt, g, b):
    wt = pltpu.bitcast(rt, F32).T
    y = wt[:, 4:5] * _unpack_bf16_pairs(y0_packed) + wt[:, 5:6] * _unpack_bf16_pairs(y1_packed)
    return _layer_norm(ALPHA * x2 + y, g, b)


def _mixer1_kernel(x2_ref, y0_ref, y1_ref, rt_ref, g3_ref, b3_ref,
                   win_ref, pw_ref, ps_ref, wout_ref, g1_ref, b1_ref, o_ref,
                   h_s, mp_s, win_s, pw_s, wout_s):
    j = pl.program_id(1)
    tq = x2_ref.shape[0]
    gw = x2_ref.shape[1] // len(POOL_WINDOWS)
    slabs = [slice(c * SLAB_ROWS, (c + 1) * SLAB_ROWS) for c in range(tq // SLAB_ROWS)]

    @pl.when(jnp.logical_and(pl.program_id(0) == 0, j == 0))
    def _():
        win_s[...] = win_ref[...].astype(BF16)
        pw_s[...] = pw_ref[...].astype(BF16)
        wout_s[...] = wout_ref[...].astype(BF16)

    @pl.when(j == 0)
    def _():
        h_s[0:POOL_HALO, :] = jnp.zeros((POOL_HALO, h_s.shape[1]), F32)

    ng = len(POOL_WINDOWS)
    quarter = SLAB_ROWS // ng

    def sub(rows, i):
        return slice(rows.start + i * quarter, rows.start + (i + 1) * quarter)

    def load_in(rows, i):
        r = sub(rows, i)
        return _expert_combine_ln(x2_ref[r, :], y0_ref[r, :], y1_ref[r, :], rt_ref[:, r],
                                  g3_ref[...], b3_ref[...])

    def project(rows, xb, g):
        lo, hi = g * gw, (g + 1) * gw
        h_s[POOL_HALO + rows.start:POOL_HALO + rows.stop, lo:hi] = _dot(xb, win_s[:, lo:hi])

    def pool(rows, g):
        win = POOL_WINDOWS[g]
        lo, hi = g * gw, (g + 1) * gw
        n = rows.stop - rows.start
        t_pos = j * tq + rows.start + lax.broadcasted_iota(I32, (n, 1), 0)
        ext = h_s[rows.start:rows.stop + POOL_HALO, lo:hi]
        acc = ext
        shift = 1
        while shift < win:
            acc = acc + pltpu.roll(acc, shift, 0)
            shift *= 2
        count = jnp.minimum(t_pos + 1, win).astype(F32)
        pooled = acc[POOL_HALO:, :] / count - ext[POOL_HALO:, :]
        mapped = _dot(pooled.astype(BF16), pw_s[g])
        mp_s[rows, lo:hi] = (mapped * ps_ref[:, lo:hi]).astype(BF16)

    def out_cols(rows, g):
        return _dot(mp_s[rows, :], wout_s[:, g * gw:(g + 1) * gw])

    def finish(rows, x_parts, z_cols, i):
        lo, hi = i * quarter, (i + 1) * quarter
        z = ALPHA * x_parts[i] + jnp.concatenate([zc[lo:hi, :] for zc in z_cols], axis=1)
        o_ref[sub(rows, i), :] = _layer_norm(z, g1_ref[...], b1_ref[...])

    assert len(slabs) == 2
    sa, sb = slabs
    xa = [load_in(sa, i) for i in range(ng)]
    xa_b = jnp.concatenate(xa, axis=0).astype(BF16)
    xb = []
    for g in range(ng):
        project(sa, xa_b, g)
        xb.append(load_in(sb, g))
    xb_b = jnp.concatenate(xb, axis=0).astype(BF16)
    for g in range(ng):
        project(sb, xb_b, g)
        pool(sa, g)
    za = []
    for g in range(ng):
        za.append(out_cols(sa, g))
        pool(sb, g)
    zb = []
    for g in range(ng):
        zb.append(out_cols(sb, g))
        finish(sa, xa, za, g)
    for g in range(ng):
        finish(sb, xb, zb, g)
    h_s[0:POOL_HALO, :] = h_s[tq:tq + POOL_HALO, :]


def _mixer1(x2, y01, rt, g3, b3, w_in, pool_w, pool_scale, w_out, g1, b1, bsz, s_len):
    t_all, d = x2.shape
    tq = TOKEN_TILE
    nj = s_len // tq
    nt = bsz * nj
    row = lambda bb, jj: (bb * nj + jj, 0)
    ng = len(POOL_WINDOWS)
    return pl.pallas_call(
        _mixer1_kernel,
        grid=(bsz, nj),
        in_specs=[
            pl.BlockSpec((tq, d), row),
            pl.BlockSpec((tq, d // 2), row),
            pl.BlockSpec((tq, d // 2), lambda bb, jj: (nt + bb * nj + jj, 0)),
            pl.BlockSpec((8, tq), lambda bb, jj: (0, bb * nj + jj)),
            _const_spec((1, d)), _const_spec((1, d)),
            _const_spec((d, d)),
            _const_spec((ng, d // ng, d // ng)),
            _const_spec((1, d)),
            _const_spec((d, d)),
            _const_spec((1, d)), _const_spec((1, d)),
        ],
        out_specs=pl.BlockSpec((tq, d), row),
        out_shape=jax.ShapeDtypeStruct((t_all, d), F32),
        scratch_shapes=[pltpu.VMEM((tq + POOL_HALO, d), F32), pltpu.VMEM((tq, d), BF16),
                        pltpu.VMEM((d, d), BF16), pltpu.VMEM((ng, d // ng, d // ng), BF16),
                        pltpu.VMEM((d, d), BF16)],
        compiler_params=_tc_params(2),
        name="mixer1",
    )(x2, y01, y01, rt, g3, b3, w_in, pool_w, pool_scale, w_out, g1, b1)


def _kv_kernel(mem_ref, w_ref, o_ref, w_s):
    @pl.when(pl.program_id(0) == 0)
    def _():
        w_s[...] = w_ref[...].astype(BF16)

    o_ref[...] = _dot(mem_ref[...].astype(BF16), w_s[...]).astype(BF16)


def _kv_proj(mem2d, wkv, bsz):
    rows, d = mem2d.shape
    m = rows // bsz
    return pl.pallas_call(
        _kv_kernel,
        grid=(bsz,),
        in_specs=[pl.BlockSpec((m, d), lambda i: (i, 0)), _const_spec(wkv.shape)],
        out_specs=pl.BlockSpec((m, wkv.shape[1]), lambda i: (i, 0)),
        out_shape=jax.ShapeDtypeStruct((rows, wkv.shape[1]), BF16),
        scratch_shapes=[pltpu.VMEM(wkv.shape, BF16)],
        compiler_params=_tc_params(1),
        name="kv_proj",
    )(mem2d, wkv)


def _top2_of4(v):
    hi01, lo01 = jnp.maximum(v[0], v[1]), jnp.minimum(v[0], v[1])
    hi23, lo23 = jnp.maximum(v[2], v[3]), jnp.minimum(v[2], v[3])
    return jnp.maximum(hi01, hi23) + jnp.maximum(jnp.minimum(hi01, hi23), jnp.maximum(lo01, lo23))


def _argmax_first(vals):
    best, idx = vals[0], jnp.zeros(vals[0].shape, I32)
    for i in range(1, len(vals)):
        better = vals[i] > best
        best = jnp.where(better, vals[i], best)
        idx = jnp.where(better, i, idx)
    return best, idx


def _xattn_kernel(x_ref, kv_ref, wq_ref, wo_ref, g2_ref, b2_ref, rwh_ref, rwl_ref, rb_ref,
                  x2_ref, xp_ref, rt_ref, cnt_ref, run_s, wq_s, wo_s):
    first = jnp.logical_and(pl.program_id(0) == 0, pl.program_id(1) == 0)
    tq, d = x_ref.shape

    @pl.when(first)
    def _():
        run_s[...] = jnp.zeros(run_s.shape, F32)
        wq_s[...] = wq_ref[...].astype(BF16)
        wo_s[...] = wo_ref[...].astype(BF16)

    slabs = [slice(c * SLAB_ROWS, (c + 1) * SLAB_ROWS) for c in range(tq // SLAB_ROWS)]
    run = run_s[...]
    zs = [_xattn_attend(*_xattn_query(rows, x_ref, wq_s), kv_ref, wo_s) for rows in slabs]
    picks = []
    for rows, z in zip(slabs, zs):
        logits = _xattn_norm_logits(rows, z, g2_ref, b2_ref, rwh_ref, rwl_ref, x2_ref, xp_ref)
        picks.append(_xattn_select(logits, rb_ref))
    for rows, pick in zip(slabs, picks):
        run = _xattn_rank(rows, pick, run, rt_ref)
    run_s[...] = run
    cnt_ref[...] = jnp.broadcast_to(run, cnt_ref.shape).astype(I32)


def _xattn_query(rows, x_ref, wq_s):
    hd = x_ref.shape[1] // X_HEADS
    x = x_ref[rows, :]
    return x, _dot(x.astype(BF16), wq_s[...]) * (hd ** -0.5)


def _xattn_attend(x, q, kv_ref, wo_s):
    d = x.shape[1]
    hd = d // X_HEADS
    outs = []
    for hx in range(X_HEADS):
        qh = q[:, hx * hd:(hx + 1) * hd].astype(BF16)
        kh = kv_ref[:, hx * hd:(hx + 1) * hd]
        vh = kv_ref[:, d + hx * hd:d + (hx + 1) * hd]
        s = _dot_nt(qh, kh)
        p = jnp.exp(s - jnp.max(s, axis=-1, keepdims=True))
        o = _dot(p.astype(BF16), vh) / jnp.sum(p, axis=-1, keepdims=True)
        outs.append(o.astype(BF16))
    return ALPHA * x + _dot(jnp.concatenate(outs, axis=1), wo_s[...])


def _xattn_norm_logits(rows, z, g2_ref, b2_ref, rwh_ref, rwl_ref, x2_ref, xp_ref):
    x2 = _layer_norm(z, g2_ref[...], b2_ref[...])
    x2_ref[rows, :] = x2

    xp_ref[rows, :] = _pack_bf16_pairs(x2)

    xh, xl = _split_bf16(x2)
    both = _dot_nt(jnp.concatenate([rwh_ref[...], rwl_ref[...]], axis=0), xh)
    return both[0:N_EXPERTS] + both[N_EXPERTS:] + _dot_nt(rwh_ref[...], xl)


def _xattn_select(logits, rb_ref):
    e_max = jnp.max(logits, axis=0, keepdims=True)
    ex = jnp.exp(logits - e_max)
    scores = ex / jnp.sum(ex, axis=0, keepdims=True)
    biased = scores + rb_ref[...]
    sc = [scores[e:e + 1, :] for e in range(N_EXPERTS)]
    bi = [biased[e:e + 1, :] for e in range(N_EXPERTS)]
    epg = EXPERTS_PER_GROUP
    gscore = [_top2_of4(bi[g * epg:(g + 1) * epg]) for g in range(N_EXPERT_GROUPS)]
    _, gsel = _argmax_first(gscore)

    def pick(vals):
        return [functools.reduce(lambda acc, g: jnp.where(gsel == g, vals[g * epg + i], acc),
                                 range(1, N_EXPERT_GROUPS), vals[i]) for i in range(epg)]

    in_b = pick(bi)
    in_s = pick(sc)
    _, i0 = _argmax_first(in_b)
    _, i1 = _argmax_first([jnp.where(i0 == i, -jnp.inf, in_b[i]) for i in range(epg)])

    def take(vals, idx):
        return functools.reduce(lambda acc, i: jnp.where(idx == i, vals[i], acc), range(1, epg), vals[0])

    s0, s1 = take(in_s, i0), take(in_s, i1)
    w0, w1 = s0 / (s0 + s1), s1 / (s0 + s1)
    return gsel * epg + i0, gsel * epg + i1, w0, w1


def _xattn_rank(rows, selection, run, rt_ref):
    e0, e1, w0, w1 = selection
    tq = rows.stop - rows.start
    eid = lax.broadcasted_iota(I32, (N_EXPERTS, tq), 0)
    oh0 = eid == e0
    oh1 = eid == e1
    onehot = jnp.where(jnp.logical_or(oh0, oh1), 1.0, 0.0)
    rr = lax.broadcasted_iota(I32, (tq, tq), 0)
    cc = lax.broadcasted_iota(I32, (tq, tq), 1)
    upper = jnp.where(rr < cc, 1.0, 0.0).astype(BF16)
    prefix = _dot(onehot.astype(BF16), upper) + run
    r0 = jnp.sum(jnp.where(oh0, prefix, 0.0), axis=0, keepdims=True).astype(I32)
    r1 = jnp.sum(jnp.where(oh1, prefix, 0.0), axis=0, keepdims=True).astype(I32)

    zero = jnp.zeros((1, tq), I32)
    rt_ref[:, rows] = jnp.concatenate(
        [e0, e1, r0, r1, pltpu.bitcast(w0, I32), pltpu.bitcast(w1, I32), zero, zero], axis=0)
    return run + jnp.sum(onehot, axis=1, keepdims=True)


def _xattn_route(x1, kv, mem_len, wq, wo, g2, b2, rw_hi, rw_lo, rbias, b0, bsz, s_len):
    t_all, d = x1.shape
    tq = TOKEN_TILE
    nj = s_len // tq
    m = mem_len
    row = lambda bb, jj: (bb * nj + jj, 0)
    return pl.pallas_call(
        _xattn_kernel,
        grid=(bsz, nj),
        in_specs=[
            pl.BlockSpec((tq, d), row),
            pl.BlockSpec((m, 2 * d), lambda bb, jj: (b0 + bb, 0)),
            _const_spec((d, d)), _const_spec((d, d)),
            _const_spec((1, d)), _const_spec((1, d)),
            _const_spec((N_EXPERTS, d)), _const_spec((N_EXPERTS, d)),
            _const_spec((N_EXPERTS, 1)),
        ],
        out_specs=[
            pl.BlockSpec((tq, d), row),
            pl.BlockSpec((tq, d // 2), row),
            pl.BlockSpec((8, tq), lambda bb, jj: (0, bb * nj + jj)),
            _const_spec((N_EXPERTS, LANES)),
        ],
        out_shape=[
            jax.ShapeDtypeStruct((t_all, d), F32),
            jax.ShapeDtypeStruct((t_all, d // 2), I32),
            jax.ShapeDtypeStruct((8, t_all), I32),
            jax.ShapeDtypeStruct((N_EXPERTS, LANES), I32),
        ],
        scratch_shapes=[pltpu.VMEM((N_EXPERTS, 1), F32), pltpu.VMEM((d, d), BF16), pltpu.VMEM((d, d), BF16)],
        compiler_params=_tc_params(2),
        name="xattn_route",
    )(x1, kv, wq, wo, g2, b2, rw_hi, rw_lo, rbias)


def _sc_mesh():
    return plsc.VectorSubcoreMesh(core_axis_name="c", subcore_axis_name="s")


def _sc_params():
    return pltpu.CompilerParams(needs_layout_passes=False)


def _worker_id():
    return lax.axis_index("s") * lax.axis_size("c") + lax.axis_index("c")


def _sc_dispatch(xp, dest2d, n_rows):
    t_all, width = xp.shape
    chunk = dest2d.shape[1]
    tok_w = t_all // SC_WORKERS
    nch = tok_w // chunk
    slot1 = t_all // chunk
    assert t_all % (SC_WORKERS * chunk * 2) == 0

    def body(x_hbm, dest_hbm, out_hbm, idx0_v, idx1_v, buf0, buf1, sem_r, sem_w):
        wid = _worker_id()
        base = wid * tok_w
        pltpu.sync_copy(dest_hbm.at[pl.ds(wid * nch, nch)], idx0_v)
        pltpu.sync_copy(dest_hbm.at[pl.ds(slot1 + wid * nch, nch)], idx1_v)

        def read(c, buf, k):
            return pltpu.make_async_copy(x_hbm.at[pl.ds(base + c * chunk, chunk)], buf, sem_r.at[k])

        def scatter(c, buf):
            a = pltpu.make_async_copy(buf, out_hbm.at[idx0_v.at[c]], sem_w.at[0])
            b = pltpu.make_async_copy(buf, out_hbm.at[idx1_v.at[c]], sem_w.at[1])
            a.start()
            b.start()
            a.wait()
            b.wait()

        read(0, buf0, 0).start()

        @pl.loop(0, nch // 2)
        def _(g):
            c = 2 * g
            read(c + 1, buf1, 1).start()
            read(c, buf0, 0).wait()
            scatter(c, buf0)

            @pl.when(c + 2 < nch)
            def _():
                read(c + 2, buf0, 0).start()

            read(c + 1, buf1, 1).wait()
            scatter(c + 1, buf1)

    return pl.kernel(
        body,
        out_type=jax.ShapeDtypeStruct((n_rows, width), xp.dtype),
        mesh=_sc_mesh(),
        scratch_types=[
            pltpu.VMEM((nch, chunk), I32),
            pltpu.VMEM((nch, chunk), I32),
            pltpu.VMEM((chunk, width), xp.dtype),
            pltpu.VMEM((chunk, width), xp.dtype),
            pltpu.SemaphoreType.DMA((2,)),
            pltpu.SemaphoreType.DMA((2,)),
        ],
        compiler_params=_sc_params(),
        name="sc_dispatch",
    )(xp, dest2d)


def _sc_pack_weights(w2d):
    n_rows, width = w2d.shape
    half = width // 2
    rows_w = n_rows // SC_WORKERS
    chunk = (32 * 1024) // width
    lanes = 16
    assert n_rows % (SC_WORKERS * chunk) == 0 and half % lanes == 0

    def body(w_hbm, out_hbm, in_v, out_v):
        base = _worker_id() * rows_w

        def rne_hi16(bits):
            return bits + 0x7FFF + jnp.bitwise_and(lax.shift_right_logical(bits, 16), 1)

        @pl.loop(0, rows_w // chunk)
        def _(c):
            r0 = base + c * chunk
            pltpu.sync_copy(w_hbm.at[pl.ds(r0, chunk)], in_v)

            @pl.loop(0, chunk)
            def _(r):
                @plsc.parallel_loop(0, half, step=lanes, unroll=4)
                def _(j):
                    lo = rne_hi16(plsc.bitcast(in_v[r, pl.ds(j, lanes)], I32))
                    hi = rne_hi16(plsc.bitcast(in_v[r, pl.ds(half + j, lanes)], I32))
                    out_v[r, pl.ds(j, lanes)] = jnp.bitwise_or(
                        lax.shift_right_logical(lo, 16), jnp.bitwise_and(hi, jnp.int32(-65536)))

            pltpu.sync_copy(out_v, out_hbm.at[pl.ds(r0, chunk)])

    return pl.kernel(
        body,
        out_type=jax.ShapeDtypeStruct((n_rows, half), I32),
        mesh=_sc_mesh(),
        scratch_types=[pltpu.VMEM((chunk, width), F32), pltpu.VMEM((chunk, half), I32)],
        compiler_params=_sc_params(),
        name="sc_pack_weights",
    )(w2d)


def _sc_combine(ys, dest2d):
    n_idx_rows, chunk = dest2d.shape
    width = ys.shape[1]
    nch = n_idx_rows // SC_WORKERS
    assert n_idx_rows % (SC_WORKERS * 2) == 0

    def body(y_hbm, dest_hbm, out_hbm, idx_v, buf0, buf1, sem_g):
        wid = _worker_id()
        base = wid * nch * chunk
        pltpu.sync_copy(dest_hbm.at[pl.ds(wid * nch, nch)], idx_v)

        def gather(c, buf, k):
            return pltpu.make_async_copy(y_hbm.at[idx_v.at[c]], buf, sem_g.at[k])

        def write(c, buf):
            pltpu.sync_copy(buf, out_hbm.at[pl.ds(base + c * chunk, chunk)])

        gather(0, buf0, 0).start()

        @pl.loop(0, nch // 2)
        def _(g):
            c = 2 * g
            gather(c + 1, buf1, 1).start()
            gather(c, buf0, 0).wait()
            write(c, buf0)

            @pl.when(c + 2 < nch)
            def _():
                gather(c + 2, buf0, 0).start()

            gather(c + 1, buf1, 1).wait()
            write(c + 1, buf1)

    return pl.kernel(
        body,
        out_type=jax.ShapeDtypeStruct((n_idx_rows * chunk, width), ys.dtype),
        mesh=_sc_mesh(),
        scratch_types=[
            pltpu.VMEM((nch, chunk), I32),
            pltpu.VMEM((chunk, width), ys.dtype),
            pltpu.VMEM((chunk, width), ys.dtype),
            pltpu.SemaphoreType.DMA((2,)),
        ],
        compiler_params=_sc_params(),
        name="sc_combine",
    )(ys, dest2d)


def _pack_bf16_pairs(v):
    half = v.shape[1] // 2
    lo = pltpu.bitcast(v[:, :half].astype(BF16).astype(F32), jnp.uint32) >> 16
    hi = pltpu.bitcast(v[:, half:].astype(BF16).astype(F32), jnp.uint32) & jnp.uint32(0xFFFF0000)
    return pltpu.bitcast(hi | lo, I32)


def _unpack_bf16_pairs(w):
    w = pltpu.bitcast(w, jnp.uint32)
    lo = pltpu.bitcast(w << 16, F32)
    hi = pltpu.bitcast(w & jnp.uint32(0xFFFF0000), F32)
    return jnp.concatenate([lo, hi], axis=1)


def _ffn_kernel(be_ref, nv_ref, nu_ref, nxt_ref, xs_ref, wg_hbm, wu_hbm, wd_hbm, o_ref,
                wg_f, wu_f, wd_f, wg_s, wu_s, wd_s, sem):
    i = pl.program_id(0)

    def fetch(e):
        return (pltpu.make_async_copy(wg_hbm.at[e], wg_f, sem.at[0]),
                pltpu.make_async_copy(wu_hbm.at[e], wu_f, sem.at[1]),
                pltpu.make_async_copy(wd_hbm.at[e], wd_f, sem.at[2]))

    @pl.when(nxt_ref[i] >= 0)
    def _():
        @pl.when(i == 0)
        def _():
            for cp in fetch(be_ref[0]):
                cp.start()

        for cp in fetch(be_ref[i]):
            cp.wait()
        wg_s[...] = _unpack_bf16_pairs(wg_f[...]).astype(BF16)
        wu_s[...] = _unpack_bf16_pairs(wu_f[...]).astype(BF16)
        wd_s[...] = _unpack_bf16_pairs(wd_f[...]).astype(BF16)

        @pl.when(nxt_ref[i] < N_EXPERTS)
        def _():
            for cp in fetch(nxt_ref[i]):
                cp.start()

    @pl.when(i < nu_ref[0])
    def _():
        live = lax.broadcasted_iota(I32, xs_ref.shape, 0) < nv_ref[i]
        xb = _unpack_bf16_pairs(jnp.where(live, xs_ref[...], 0)).astype(BF16)
        act = jax.nn.silu(_dot(xb, wg_s[...])) * _dot(xb, wu_s[...])
        o_ref[...] = _pack_bf16_pairs(_dot(act.astype(BF16), wd_s[...]))


def _expert_ffn(xs, blk_expert, blk_valid, n_used, blk_next, w_gate, w_up, w_down):
    n_rows, half = xs.shape
    d = 2 * half
    de = 2 * w_gate.shape[2]
    bm = EXPERT_ROWS
    rows = lambda i, be, nv, nu, nx: (jnp.minimum(i, nu[0] - 1), 0)
    hbm = pl.BlockSpec(memory_space=pl.ANY)
    return pl.pallas_call(
        _ffn_kernel,
        grid_spec=pltpu.PrefetchScalarGridSpec(
            num_scalar_prefetch=4,
            grid=(n_rows // bm,),
            in_specs=[pl.BlockSpec((bm, half), rows), hbm, hbm, hbm],
            out_specs=pl.BlockSpec((bm, half), rows),
            scratch_shapes=[
                pltpu.VMEM((d, de // 2), I32), pltpu.VMEM((d, de // 2), I32), pltpu.VMEM((de, d // 2), I32),
                pltpu.VMEM((d, de), BF16), pltpu.VMEM((d, de), BF16), pltpu.VMEM((de, d), BF16),
                pltpu.SemaphoreType.DMA((3,)),
            ],
        ),
        out_shape=jax.ShapeDtypeStruct((n_rows, half), I32),
        compiler_params=_tc_params(1),
        name="expert_ffn",
    )(blk_expert, blk_valid, n_used, blk_next, xs, w_gate, w_up, w_down)


def _combine_kernel(x_ref, y0_ref, y1_ref, rt_ref, g_ref, b_ref, o_ref):
    o_ref[...] = _expert_combine_ln(x_ref[...], y0_ref[...], y1_ref[...], rt_ref[...], g_ref[...], b_ref[...])


def _combine_kernel_into(x_ref, y0_ref, y1_ref, rt_ref, g_ref, b_ref, full_ref, o_ref):
    del full_ref
    _combine_kernel(x_ref, y0_ref, y1_ref, rt_ref, g_ref, b_ref, o_ref)


def _combine_ln(x2, y01, rt, g3, b3, into=None, row0=0, full_rows=None):
    t_all, d = x2.shape
    tq = TOKEN_TILE
    nt = t_all // tq
    blk0 = row0 // tq if full_rows else 0
    in_specs = [
        pl.BlockSpec((tq, d), lambda i: (i, 0)),
        pl.BlockSpec((tq, d // 2), lambda i: (i, 0)),
        pl.BlockSpec((tq, d // 2), lambda i: (i + nt, 0)),
        pl.BlockSpec((8, tq), lambda i: (0, i)),
        _const_spec((1, d)), _const_spec((1, d)),
    ]
    args = [x2, y01, y01, rt, g3, b3]
    body, aliases = _combine_kernel, {}
    if into is not None:
        in_specs.append(pl.BlockSpec(memory_space=pl.ANY))
        args.append(into)
        body, aliases = _combine_kernel_into, {len(args) - 1: 0}
    return pl.pallas_call(
        body,
        grid=(nt,),
        in_specs=in_specs,
        out_specs=pl.BlockSpec((tq, d), lambda i: (i + blk0, 0)),
        out_shape=jax.ShapeDtypeStruct((full_rows or t_all, d), F32),
        input_output_aliases=aliases,
        compiler_params=_tc_params(1),
        name="combine_ln",
    )(*args)


def _routing_tables(rt, counts):
    bm = EXPERT_ROWS
    t_all = rt.shape[1]
    n_rows = 2 * t_all + N_EXPERTS * bm
    cnt = counts[:, 0]
    padded = (cnt + bm - 1) // bm * bm
    ends = jnp.cumsum(padded)
    offs = ends - padded
    experts = rt[0:2]
    off_tok = jnp.sum(jnp.where(experts[None] == jnp.arange(N_EXPERTS, dtype=I32)[:, None, None],
                                offs[:, None, None], 0), axis=0)
    dest2d = (off_tok + rt[2:4]).reshape(-1, SC_CHUNK).astype(I32)
    blk_start = jnp.arange(n_rows // bm, dtype=I32) * bm
    blk_expert = jnp.minimum(jnp.sum(blk_start[:, None] >= ends[None, :], axis=1), N_EXPERTS - 1).astype(I32)
    live_end = jnp.sum(jnp.where(blk_expert[:, None] == jnp.arange(N_EXPERTS, dtype=I32)[None, :],
                                 (offs + cnt)[None, :], 0), axis=1)
    blk_valid = jnp.clip(live_end - blk_start, 0, bm).astype(I32)
    n_used = (ends[-1:] // bm).astype(I32)
    eid = jnp.arange(N_EXPERTS, dtype=I32)
    later_present = jnp.logical_and(eid[None, :] > eid[:, None], (cnt > 0)[None, :])
    next_present = jnp.min(jnp.where(later_present, eid[None, :], N_EXPERTS), axis=1)
    next_of_blk = jnp.sum(jnp.where(blk_expert[:, None] == eid[None, :], next_present[None, :], 0), axis=1)
    prev_expert = jnp.concatenate([jnp.full((1,), -1, I32), blk_expert[:-1]])
    is_first = jnp.logical_and(blk_start < ends[-1], blk_expert != prev_expert)
    blk_next = jnp.where(is_first, next_of_blk, -1).astype(I32)
    return dest2d, blk_expert, blk_valid, n_used, blk_next, n_rows


def _layer_tail(x1, kv, mem_len, p, router, b0, bsz, s_len):
    rw_hi, rw_lo, rbias = router
    x2, xp, rt, counts = _xattn_route(x1, kv, mem_len, p["xq"], p["xo"], p["ln2_g"], p["ln2_b"],
                                      rw_hi, rw_lo, rbias, b0, bsz, s_len)
    dest2d, blk_expert, blk_valid, n_used, blk_next, n_rows = _routing_tables(rt, counts)
    xs = _sc_dispatch(xp, dest2d, n_rows)
    ys = _expert_ffn(xs, blk_expert, blk_valid, n_used, blk_next, p["e_gate"], p["e_up"], p["e_down"])
    y01 = _sc_combine(ys, dest2d)
    return x2, y01, rt


def _row(v):
    return v.reshape(1, -1).astype(F32)


def _common_params(xq, xkv, xo, ln2_g, ln2_b, e_gate, e_up, e_down, ln3_g, ln3_b):
    def packed(w):
        e, r, c = w.shape
        return _sc_pack_weights(w.reshape(e * r, c)).reshape(e, r, c // 2)

    return dict(xq=xq, xkv=xkv, xo=xo, ln2_g=_row(ln2_g), ln2_b=_row(ln2_b),
                e_gate=packed(e_gate), e_up=packed(e_up), e_down=packed(e_down),
                ln3_g=_row(ln3_g), ln3_b=_row(ln3_b))


def kernel(x, mem, positions, router_w, router_bias, l0_w_in, l0_sinks, l0_sgu_ln_g, l0_sgu_ln_b, l0_sgu_w, l0_sgu_b, l0_w_out, l0_ln1_g, l0_ln1_b, l0_xq, l0_xkv, l0_xo, l0_ln2_g, l0_ln2_b, l0_e_gate, l0_e_up, l0_e_down, l0_ln3_g, l0_ln3_b, l1_w_in, l1_pool_w, l1_pool_scale, l1_w_out, l1_ln1_g, l1_ln1_b, l1_xq, l1_xkv, l1_xo, l1_ln2_g, l1_ln2_b, l1_e_gate, l1_e_up, l1_e_down, l1_ln3_g, l1_ln3_b):
    bsz, s_len, d = x.shape
    assert s_len % TOKEN_TILE == 0 and TOKEN_TILE == 2 * SLAB_ROWS and SLAB_ROWS % BLOCK == 0
    xt = x.reshape(bsz * s_len, d)
    mem2d = mem.reshape(-1, d)

    rw_t = router_w.T.astype(F32)
    rw_hi = rw_t.astype(BF16)
    rw_lo = (rw_t - rw_hi.astype(F32)).astype(BF16)
    router = (rw_hi, rw_lo, router_bias.reshape(-1, 1).astype(F32))

    half = ROPE_DIM // 2
    inv_freq = (ROPE_THETA ** (-(jnp.arange(half, dtype=F32) * 2.0 / ROPE_DIM))).reshape(half, 1)
    etab_np = np.zeros((LANES, 3 * LANES), np.float32)
    cbase_np = np.ones((1, LANES), np.float32)
    for ln in range(LANES):
        dd = ln % HEAD_DIM
        if dd < ROPE_DIM:
            cbase_np[0, ln] = 0.0
            etab_np[[dd % half, half + dd % half], ln] = 1.0
            if dd >= half:
                etab_np[[2 * half + dd - half, 3 * half + dd - half], LANES + ln] = 1.0
            else:
                etab_np[[2 * half + dd, 3 * half + dd], 2 * LANES + ln] = -1.0
    etab = jnp.asarray(etab_np, BF16)
    cbase = jnp.asarray(cbase_np)
    pos_row = positions.reshape(1, -1).astype(I32)
    grp = jnp.arange(PIECE_COLS) // B_GROUP_DIM
    gsum = (grp[:, None] == grp[None, :]).astype(BF16)
    bs_full = jnp.repeat(l0_sgu_b.T.astype(F32), B_GROUP_DIM, axis=1)

    p0 = _common_params(l0_xq, l0_xkv, l0_xo, l0_ln2_g, l0_ln2_b, l0_e_gate, l0_e_up, l0_e_down,
                        l0_ln3_g, l0_ln3_b)
    p1 = _common_params(l1_xq, l1_xkv, l1_xo, l1_ln2_g, l1_ln2_b, l1_e_gate, l1_e_up, l1_e_down,
                        l1_ln3_g, l1_ln3_b)
    kv0 = _kv_proj(mem2d, p0["xkv"], bsz)
    kv1 = _kv_proj(mem2d, p1["xkv"], bsz)
    mem_len = mem.shape[1]

    n_split = BATCH_SPLIT if bsz % BATCH_SPLIT == 0 else 1
    nb = bsz // n_split
    out = None
    for part in range(n_split):
        b0 = part * nb
        x1 = _mixer0(xt, pos_row, l0_sinks.astype(F32), l0_w_in, inv_freq, etab, cbase, gsum,
                     _row(l0_sgu_ln_g), _row(l0_sgu_ln_b), l0_sgu_w.astype(F32), bs_full,
                     l0_w_out, _row(l0_ln1_g), _row(l0_ln1_b), b0, nb, s_len)
        x2, y01, rt = _layer_tail(x1, kv0, mem_len, p0, router, b0, nb, s_len)
        x1 = _mixer1(x2, y01, rt, p0["ln3_g"], p0["ln3_b"], l1_w_in, l1_pool_w, _row(l1_pool_scale),
                     l1_w_out, _row(l1_ln1_g), _row(l1_ln1_b), nb, s_len)
        x2, y01, rt = _layer_tail(x1, kv1, mem_len, p1, router, b0, nb, s_len)
        out = _combine_ln(x2, y01, rt, p1["ln3_g"], p1["ln3_b"], into=out, row0=b0 * s_len,
                          full_rows=bsz * s_len)
    return out.reshape(bsz, s_len, d)
```

```python
import functools

import numpy as np
import jax
import jax.numpy as jnp
from jax import lax
from jax.experimental import pallas as pl
from jax.experimental.pallas import tpu as pltpu
from jax.experimental.pallas import tpu_sc as plsc

F32 = jnp.float32
BF16 = jnp.bfloat16
I32 = jnp.int32

DEPTH = 2
ALPHA = (2.0 * DEPTH) ** 0.25
LN_EPS = 1e-5

HEAD_DIM = 64
A_Q_HEADS = 8
A_KV_HEADS = 2
A_GROUP = A_Q_HEADS // A_KV_HEADS
BLOCK = 128
ROPE_THETA = 500000.0
ROPE_DIM = HEAD_DIM // 4
A_WIDTH = A_Q_HEADS * HEAD_DIM
KV_WIDTH = A_KV_HEADS * HEAD_DIM
B_GROUPS = 8
B_GROUP_DIM = 64
B_WIDTH = B_GROUPS * B_GROUP_DIM
POOL_WINDOWS = (2, 4, 8, 16)
POOL_HALO = 16
X_HEADS = 4
N_EXPERTS = 16
N_EXPERT_GROUPS = 4
EXPERTS_PER_GROUP = 4

LANES = 128
TOKEN_TILE = 1024
SLAB_ROWS = 512
PIECE_COLS = 256
EXPERT_ROWS = 512
BATCH_SPLIT = 2
SC_WORKERS = 32
SC_CHUNK = 64
VMEM_LIMIT = 56 * 1024 * 1024
NEG_BIG = -1e30


def _layer_norm(z, g, b):
    mu = jnp.mean(z, axis=-1, keepdims=True)
    d = z - mu
    var = jnp.mean(d * d, axis=-1, keepdims=True)
    return d * lax.rsqrt(var + LN_EPS) * g + b


def _dot(a, b):
    return jnp.dot(a, b, preferred_element_type=F32)


def _dot_nt(a, b):
    return lax.dot_general(a, b, (((1,), (1,)), ((), ())), preferred_element_type=F32)


def _split_bf16(v):
    hi = v.astype(BF16)
    lo = (v - hi.astype(F32)).astype(BF16)
    return hi, lo


def _tc_params(n_axes):
    return pltpu.CompilerParams(dimension_semantics=("arbitrary",) * n_axes,
                                vmem_limit_bytes=VMEM_LIMIT)


def _const_spec(shape):
    nd = len(shape)
    return pl.BlockSpec(shape, lambda *_: (0,) * nd, pipeline_mode=pl.Buffered(1))


def _mixer0_kernel(sinks_ref, x_ref, pos_ref, win_ref, invf_ref, etab_ref, cbase_ref, gsum_ref,
                   lng_ref, lnb_ref, ws_ref, bs_ref, wout_ref, g1_ref, b1_ref,
                   o_ref, q_s, kv_s, u_s, vn_s, mix_s, wt_s, win_s, wout_s):
    b = pl.program_id(0)
    j = pl.program_id(1)
    tq = x_ref.shape[0]
    kvw = kv_s.shape[1]

    @pl.when(jnp.logical_and(b == 0, j == 0))
    def _():
        win_s[...] = win_ref[...].astype(BF16)
        wout_s[...] = wout_ref[...].astype(BF16)
        r = lax.broadcasted_iota(I32, (BLOCK, BLOCK), 0)
        c = lax.broadcasted_iota(I32, (BLOCK, BLOCK), 1)
        for g in range(B_GROUPS):
            wt_s[g] = jnp.where(c <= r, ws_ref[g], 0.0).astype(BF16)

    @pl.when(j == 0)
    def _():
        kv_s[0:BLOCK, :] = jnp.zeros((BLOCK, kvw), BF16)

    c1 = A_WIDTH
    c2 = c1 + KV_WIDTH
    c3 = c2 + KV_WIDTH
    c4 = c3 + B_WIDTH

    def rotary_tables(rows):
        n = rows.stop - rows.start
        ang = invf_ref[...] * pos_ref[:, rows].astype(F32)
        c8 = jnp.cos(ang)
        s8 = jnp.sin(ang)
        c8h = c8.astype(BF16).astype(F32)
        s8h = s8.astype(BF16).astype(F32)
        pad = jnp.zeros((LANES - 4 * c8.shape[0], n), F32)
        stack = jnp.concatenate([c8h, c8 - c8h, s8h, s8 - s8h, pad], axis=0)
        tabs = _dot(stack.T.astype(BF16), etab_ref[...])
        return tabs[:, 0:LANES] + cbase_ref[...], tabs[:, LANES:2 * LANES], tabs[:, 2 * LANES:]

    pw = PIECE_COLS
    n_pieces = win_s.shape[1] // pw
    assert (c1 // pw, c2 // pw, c3 // pw, c4 // pw) == (2, 2, 3, 5) and c3 % pw == 0 and n_pieces == 7

    def project(xb, k):
        return _dot(xb, win_s[:, k * pw:(k + 1) * pw])

    def prepare(rows, hk, tables, k):
        n = rows.stop - rows.start
        cs, sa, sb = tables

        def rope(t):
            return t * cs + pltpu.roll(t, ROPE_DIM // 2, 1) * sa + pltpu.roll(t, LANES - ROPE_DIM // 2, 1) * sb

        if k < 2:
            for c in range(pw // LANES):
                t = hk[:, c * LANES:(c + 1) * LANES] * (HEAD_DIM ** -0.5)
                col = k * (pw // LANES) + c
                q_s[rows, col * LANES:(col + 1) * LANES] = rope(t).astype(BF16)
        elif k == 2:
            low = lax.broadcasted_iota(I32, (n, LANES), 1) < HEAD_DIM
            kr = rope(hk[:, 0:KV_WIDTH])
            kx = pltpu.roll(kr, HEAD_DIM, 1)
            vr = hk[:, KV_WIDTH:]
            vx = pltpu.roll(vr, HEAD_DIM, 1)
            kv_cols = [jnp.where(low, kr, kx), jnp.where(low, kx, kr),
                       jnp.where(low, vr, 0.0), jnp.where(low, 0.0, vx),
                       jnp.where(low, vx, 0.0), jnp.where(low, 0.0, vr)]
            for c, col in enumerate(kv_cols):
                kv_s[BLOCK + rows.start:BLOCK + rows.stop, c * LANES:(c + 1) * LANES] = col.astype(BF16)
        elif k < 5:
            lo = (k - 3) * pw
            u_s[rows, lo:lo + pw] = jax.nn.gelu(hk)
        else:
            lo = (k - 5) * pw
            v = jax.nn.gelu(hk)
            gsum = gsum_ref[...]
            mean = _dot(v.astype(BF16), gsum) * (1.0 / B_GROUP_DIM)
            d = v - mean
            var = _dot((d * d).astype(BF16), gsum) * (1.0 / B_GROUP_DIM)
            vn_s[rows, lo:lo + pw] = (d * lax.rsqrt(var + LN_EPS) * lng_ref[:, lo:lo + pw]
                                      + lnb_ref[:, lo:lo + pw]).astype(BF16)

    qi = lax.broadcasted_iota(I32, (BLOCK, 2 * BLOCK), 0)
    kj = lax.broadcasted_iota(I32, (BLOCK, 2 * BLOCK), 1)
    rel = qi + BLOCK - kj
    band = jnp.logical_and(rel >= 0, rel < BLOCK)
    low_q = lax.broadcasted_iota(I32, (BLOCK, LANES), 1) < HEAD_DIM
    low_k = lax.broadcasted_iota(I32, (2 * BLOCK, LANES), 1) < HEAD_DIM
    ones_lo = jnp.where(low_k, 1.0, 0.0).astype(BF16)
    ones_hi = jnp.where(low_k, 0.0, 1.0).astype(BF16)
    zero_q = jnp.zeros((BLOCK, LANES), BF16)

    def block_body(n):
        r0 = n * BLOCK
        kv = kv_s[pl.ds(r0, 2 * BLOCK), :]
        qb = q_s[pl.ds(r0, BLOCK), :]
        valid = jnp.logical_and(band, kj >= jnp.where(j == 0, BLOCK, 0)) if n == 0 else band
        cols_per_kv = A_GROUP // 2
        scores = {}
        for hk in range(A_KV_HEADS):
            cols = range(hk * cols_per_kv, (hk + 1) * cols_per_kv)
            pieces = []
            for c in cols:
                qp = qb[:, c * LANES:(c + 1) * LANES]
                pieces += [jnp.where(low_q, qp, zero_q), jnp.where(low_q, zero_q, qp)]
            sc = _dot_nt(jnp.concatenate(pieces, axis=0), kv[:, hk * LANES:(hk + 1) * LANES])
            for i, c in enumerate(cols):
                for half in range(2):
                    r = (2 * i + half) * BLOCK
                    scores[c, half] = sc[r:r + BLOCK, :]
        vnb = vn_s[pl.ds(r0, BLOCK), :]
        parts = []
        for c in range(B_WIDTH // LANES):
            vp = vnb[:, c * LANES:(c + 1) * LANES]
            parts.append(_dot(wt_s[2 * c], jnp.where(low_q, vp, zero_q))
                         + _dot(wt_s[2 * c + 1], jnp.where(low_q, zero_q, vp)))
        probs, esink = {}, {}
        for (c, half), sc in scores.items():
            s = jnp.where(valid, sc, NEG_BIG)
            sink = sinks_ref[2 * c + half]
            m = jnp.maximum(jnp.max(s, axis=-1, keepdims=True), sink)
            probs[c, half] = jnp.exp(s - m).astype(BF16)
            esink[c, half] = jnp.exp(sink - m)
        res = {}
        for hk in range(A_KV_HEADS):
            cols = range(hk * cols_per_kv, (hk + 1) * cols_per_kv)
            for half in range(2):
                vcol = kv[:, (2 + 2 * hk + half) * LANES:(3 + 2 * hk + half) * LANES]
                vm = jnp.concatenate([vcol, ones_lo if half == 0 else ones_hi], axis=1)
                pv = _dot(jnp.concatenate([probs[c, half] for c in cols], axis=0), vm)
                for i, c in enumerate(cols):
                    part = pv[i * BLOCK:(i + 1) * BLOCK, :]
                    res[c] = part if half == 0 else res[c] + part
        for c in range(A_WIDTH // LANES):
            den = res[c][:, LANES:] + jnp.where(low_q, esink[c, 0], esink[c, 1])
            mix_s[pl.ds(r0, BLOCK), c * LANES:(c + 1) * LANES] = (res[c][:, :LANES] / den).astype(BF16)
        mixed = jnp.concatenate(parts, axis=1) + bs_ref[...]
        mix_s[pl.ds(r0, BLOCK), A_WIDTH:] = (u_s[pl.ds(r0, BLOCK), :] * mixed).astype(BF16)

    def out_cols(rows, c):
        return _dot(mix_s[rows, :], wout_s[:, c * pw:(c + 1) * pw])

    def finish(rows, z_cols, i):
        r = slice(rows.start + i * BLOCK, rows.start + (i + 1) * BLOCK)
        z = ALPHA * x_ref[r, :] + jnp.concatenate([zc[i * BLOCK:(i + 1) * BLOCK, :] for zc in z_cols], axis=1)
        o_ref[r, :] = _layer_norm(z, g1_ref[...], b1_ref[...])

    assert tq == 2 * SLAB_ROWS and SLAB_ROWS == 4 * BLOCK and wout_s.shape[1] == 4 * pw
    sa, sb = slice(0, SLAB_ROWS), slice(SLAB_ROWS, tq)
    tab_a, tab_b = rotary_tables(sa), rotary_tables(sb)
    xa = x_ref[sa, :].astype(BF16)
    ha = [project(xa, k) for k in range(n_pieces)]
    xb = x_ref[sb, :].astype(BF16)
    hb = []
    for k in range(n_pieces):
        hb.append(project(xb, k))
        prepare(sa, ha[k], tab_a, k)
    pieces_b = iter(range(n_pieces))
    for n in range(4):
        block_body(n)
        for k in [next(pieces_b) for _ in range(2 if n < 3 else 1)]:
            prepare(sb, hb[k], tab_b, k)
    za = []
    for n in range(4):
        block_body(4 + n)
        za.append(out_cols(sa, n))
    kv_s[0:BLOCK, :] = kv_s[tq:tq + BLOCK, :]
    zb = []
    for n in range(4):
        zb.append(out_cols(sb, n))
        finish(sa, za, n)
    for n in range(4):
        finish(sb, zb, n)


def _mixer0(x, pos_row, sinks, w_in, invf, etab, cbase, gsum, lng, lnb, w_s, bs_full, w_out, g1, b1,
            b0, bsz, s_len):
    d = x.shape[1]
    t_all = bsz * s_len
    tq = TOKEN_TILE
    nj = s_len // tq
    row = lambda bb, jj: (bb * nj + jj, 0)
    in_w = w_in.shape[1]
    return pl.pallas_call(
        _mixer0_kernel,
        grid=(bsz, nj),
        in_specs=[
            pl.BlockSpec(memory_space=pltpu.SMEM),
            pl.BlockSpec((tq, d), lambda bb, jj: ((b0 + bb) * nj + jj, 0)),
            pl.BlockSpec((1, tq), lambda bb, jj: (0, (b0 + bb) * nj + jj)),
            _const_spec((d, in_w)),
            _const_spec((ROPE_DIM // 2, 1)), _const_spec((LANES, 3 * LANES)), _const_spec((1, LANES)),
            _const_spec((PIECE_COLS, PIECE_COLS)),
            _const_spec((1, B_WIDTH)), _const_spec((1, B_WIDTH)),
            _const_spec((B_GROUPS, BLOCK, BLOCK)),
            _const_spec((BLOCK, B_WIDTH)),
            _const_spec((A_WIDTH + B_WIDTH, d)),
            _const_spec((1, d)), _const_spec((1, d)),
        ],
        out_specs=pl.BlockSpec((tq, d), row),
        out_shape=jax.ShapeDtypeStruct((t_all, d), F32),
        scratch_shapes=[
            pltpu.VMEM((tq, A_WIDTH), BF16),
            pltpu.VMEM((tq + BLOCK, 6 * LANES), BF16),
            pltpu.VMEM((tq, B_WIDTH), F32),
            pltpu.VMEM((tq, B_WIDTH), BF16),
            pltpu.VMEM((tq, A_WIDTH + B_WIDTH), BF16),
            pltpu.VMEM((B_GROUPS, BLOCK, BLOCK), BF16),
            pltpu.VMEM((d, in_w), BF16),
            pltpu.VMEM((A_WIDTH + B_WIDTH, d), BF16),
        ],
        compiler_params=_tc_params(2),
        name="mixer0",
    )(sinks, x, pos_row, w_in, invf, etab, cbase, gsum, lng, lnb, w_s, bs_full, w_out, g1, b1)


def _expert_combine_ln(x2, y0_packed, y1_packed, rt, g, b):
    wt = pltpu.bitcast(rt, F32).T
    y = wt[:, 4:5] * _unpack_bf16_pairs(y0_packed) + wt[:, 5:6] * _unpack_bf16_pairs(y1_packed)
    return _layer_norm(ALPHA * x2 + y, g, b)


def _mixer1_kernel(x2_ref, y0_ref, y1_ref, rt_ref, g3_ref, b3_ref,
                   win_ref, pw_ref, ps_ref, wout_ref, g1_ref, b1_ref, o_ref,
                   h_s, mp_s, win_s, pw_s, wout_s):
    j = pl.program_id(1)
    tq = x2_ref.shape[0]
    gw = x2_ref.shape[1] // len(POOL_WINDOWS)
    slabs = [slice(c * SLAB_ROWS, (c + 1) * SLAB_ROWS) for c in range(tq // SLAB_ROWS)]

    @pl.when(jnp.logical_and(pl.program_id(0) == 0, j == 0))
    def _():
        win_s[...] = win_ref[...].astype(BF16)
        pw_s[...] = pw_ref[...].astype(BF16)
        wout_s[...] = wout_ref[...].astype(BF16)

    @pl.when(j == 0)
    def _():
        h_s[0:POOL_HALO, :] = jnp.zeros((POOL_HALO, h_s.shape[1]), F32)

    ng = len(POOL_WINDOWS)
    quarter = SLAB_ROWS // ng

    def sub(rows, i):
        return slice(rows.start + i * quarter, rows.start + (i + 1) * quarter)

    def load_in(rows, i):
        r = sub(rows, i)
        return _expert_combine_ln(x2_ref[r, :], y0_ref[r, :], y1_ref[r, :], rt_ref[:, r],
                                  g3_ref[...], b3_ref[...])

    def project(rows, xb, g):
        lo, hi = g * gw, (g + 1) * gw
        h_s[POOL_HALO + rows.start:POOL_HALO + rows.stop, lo:hi] = _dot(xb, win_s[:, lo:hi])

    def pool(rows, g):
        win = POOL_WINDOWS[g]
        lo, hi = g * gw, (g + 1) * gw
        n = rows.stop - rows.start
        t_pos = j * tq + rows.start + lax.broadcasted_iota(I32, (n, 1), 0)
        ext = h_s[rows.start:rows.stop + POOL_HALO, lo:hi]
        acc = ext
        shift = 1
        while shift < win:
            acc = acc + pltpu.roll(acc, shift, 0)
            shift *= 2
        count = jnp.minimum(t_pos + 1, win).astype(F32)
        pooled = acc[POOL_HALO:, :] / count - ext[POOL_HALO:, :]
        mapped = _dot(pooled.astype(BF16), pw_s[g])
        mp_s[rows, lo:hi] = (mapped * ps_ref[:, lo:hi]).astype(BF16)

    def out_cols(rows, g):
        return _dot(mp_s[rows, :], wout_s[:, g * gw:(g + 1) * gw])

    def finish(rows, x_parts, z_cols, i):
        lo, hi = i * quarter, (i + 1) * quarter
        z = ALPHA * x_parts[i] + jnp.concatenate([zc[lo:hi, :] for zc in z_cols], axis=1)
        o_ref[sub(rows, i), :] = _layer_norm(z, g1_ref[...], b1_ref[...])

    assert len(slabs) == 2
    sa, sb = slabs
    xa = [load_in(sa, i) for i in range(ng)]
    xa_b = jnp.concatenate(xa, axis=0).astype(BF16)
    xb = []
    for g in range(ng):
        project(sa, xa_b, g)
        xb.append(load_in(sb, g))
    xb_b = jnp.concatenate(xb, axis=0).astype(BF16)
    for g in range(ng):
        project(sb, xb_b, g)
        pool(sa, g)
    za = []
    for g in range(ng):
        za.append(out_cols(sa, g))
        pool(sb, g)
    zb = []
    for g in range(ng):
        zb.append(out_cols(sb, g))
        finish(sa, xa, za, g)
    for g in range(ng):
        finish(sb, xb, zb, g)
    h_s[0:POOL_HALO, :] = h_s[tq:tq + POOL_HALO, :]


def _mixer1(x2, y01, rt, g3, b3, w_in, pool_w, pool_scale, w_out, g1, b1, bsz, s_len):
    t_all, d = x2.shape
    tq = TOKEN_TILE
    nj = s_len // tq
    nt = bsz * nj
    row = lambda bb, jj: (bb * nj + jj, 0)
    ng = len(POOL_WINDOWS)
    return pl.pallas_call(
        _mixer1_kernel,
        grid=(bsz, nj),
        in_specs=[
            pl.BlockSpec((tq, d), row),
            pl.BlockSpec((tq, d // 2), row),
            pl.BlockSpec((tq, d // 2), lambda bb, jj: (nt + bb * nj + jj, 0)),
            pl.BlockSpec((8, tq), lambda bb, jj: (0, bb * nj + jj)),
            _const_spec((1, d)), _const_spec((1, d)),
            _const_spec((d, d)),
            _const_spec((ng, d // ng, d // ng)),
            _const_spec((1, d)),
            _const_spec((d, d)),
            _const_spec((1, d)), _const_spec((1, d)),
        ],
        out_specs=pl.BlockSpec((tq, d), row),
        out_shape=jax.ShapeDtypeStruct((t_all, d), F32),
        scratch_shapes=[pltpu.VMEM((tq + POOL_HALO, d), F32), pltpu.VMEM((tq, d), BF16),
                        pltpu.VMEM((d, d), BF16), pltpu.VMEM((ng, d // ng, d // ng), BF16),
                        pltpu.VMEM((d, d), BF16)],
        compiler_params=_tc_params(2),
        name="mixer1",
    )(x2, y01, y01, rt, g3, b3, w_in, pool_w, pool_scale, w_out, g1, b1)


def _kv_kernel(mem_ref, w_ref, o_ref, w_s):
    @pl.when(pl.program_id(0) == 0)
    def _():
        w_s[...] = w_ref[...].astype(BF16)

    o_ref[...] = _dot(mem_ref[...].astype(BF16), w_s[...]).astype(BF16)


def _kv_proj(mem2d, wkv, bsz):
    rows, d = mem2d.shape
    m = rows // bsz
    return pl.pallas_call(
        _kv_kernel,
        grid=(bsz,),
        in_specs=[pl.BlockSpec((m, d), lambda i: (i, 0)), _const_spec(wkv.shape)],
        out_specs=pl.BlockSpec((m, wkv.shape[1]), lambda i: (i, 0)),
        out_shape=jax.ShapeDtypeStruct((rows, wkv.shape[1]), BF16),
        scratch_shapes=[pltpu.VMEM(wkv.shape, BF16)],
        compiler_params=_tc_params(1),
        name="kv_proj",
    )(mem2d, wkv)


def _top2_of4(v):
    hi01, lo01 = jnp.maximum(v[0], v[1]), jnp.minimum(v[0], v[1])
    hi23, lo23 = jnp.maximum(v[2], v[3]), jnp.minimum(v[2], v[3])
    return jnp.maximum(hi01, hi23) + jnp.maximum(jnp.minimum(hi01, hi23), jnp.maximum(lo01, lo23))


def _argmax_first(vals):
    best, idx = vals[0], jnp.zeros(vals[0].shape, I32)
    for i in range(1, len(vals)):
        better = vals[i] > best
        best = jnp.where(better, vals[i], best)
        idx = jnp.where(better, i, idx)
    return best, idx


def _xattn_kernel(x_ref, kv_ref, wq_ref, wo_ref, g2_ref, b2_ref, rwh_ref, rwl_ref, rb_ref,
                  x2_ref, xp_ref, rt_ref, cnt_ref, run_s, wq_s, wo_s):
    first = jnp.logical_and(pl.program_id(0) == 0, pl.program_id(1) == 0)
    tq, d = x_ref.shape

    @pl.when(first)
    def _():
        run_s[...] = jnp.zeros(run_s.shape, F32)
        wq_s[...] = wq_ref[...].astype(BF16)
        wo_s[...] = wo_ref[...].astype(BF16)

    slabs = [slice(c * SLAB_ROWS, (c + 1) * SLAB_ROWS) for c in range(tq // SLAB_ROWS)]
    run = run_s[...]
    zs = [_xattn_attend(*_xattn_query(rows, x_ref, wq_s), kv_ref, wo_s) for rows in slabs]
    picks = []
    for rows, z in zip(slabs, zs):
        logits = _xattn_norm_logits(rows, z, g2_ref, b2_ref, rwh_ref, rwl_ref, x2_ref, xp_ref)
        picks.append(_xattn_select(logits, rb_ref))
    for rows, pick in zip(slabs, picks):
        run = _xattn_rank(rows, pick, run, rt_ref)
    run_s[...] = run
    cnt_ref[...] = jnp.broadcast_to(run, cnt_ref.shape).astype(I32)


def _xattn_query(rows, x_ref, wq_s):
    hd = x_ref.shape[1] // X_HEADS
    x = x_ref[rows, :]
    return x, _dot(x.astype(BF16), wq_s[...]) * (hd ** -0.5)


def _xattn_attend(x, q, kv_ref, wo_s):
    d = x.shape[1]
    hd = d // X_HEADS
    outs = []
    for hx in range(X_HEADS):
        qh = q[:, hx * hd:(hx + 1) * hd].astype(BF16)
        kh = kv_ref[:, hx * hd:(hx + 1) * hd]
        vh = kv_ref[:, d + hx * hd:d + (hx + 1) * hd]
        s = _dot_nt(qh, kh)
        p = jnp.exp(s - jnp.max(s, axis=-1, keepdims=True))
        o = _dot(p.astype(BF16), vh) / jnp.sum(p, axis=-1, keepdims=True)
        outs.append(o.astype(BF16))
    return ALPHA * x + _dot(jnp.concatenate(outs, axis=1), wo_s[...])


def _xattn_norm_logits(rows, z, g2_ref, b2_ref, rwh_ref, rwl_ref, x2_ref, xp_ref):
    x2 = _layer_norm(z, g2_ref[...], b2_ref[...])
    x2_ref[rows, :] = x2

    xp_ref[rows, :] = _pack_bf16_pairs(x2)

    xh, xl = _split_bf16(x2)
    both = _dot_nt(jnp.concatenate([rwh_ref[...], rwl_ref[...]], axis=0), xh)
    return both[0:N_EXPERTS] + both[N_EXPERTS:] + _dot_nt(rwh_ref[...], xl)


def _xattn_select(logits, rb_ref):
    e_max = jnp.max(logits, axis=0, keepdims=True)
    ex = jnp.exp(logits - e_max)
    scores = ex / jnp.sum(ex, axis=0, keepdims=True)
    biased = scores + rb_ref[...]
    sc = [scores[e:e + 1, :] for e in range(N_EXPERTS)]
    bi = [biased[e:e + 1, :] for e in range(N_EXPERTS)]
    epg = EXPERTS_PER_GROUP
    gscore = [_top2_of4(bi[g * epg:(g + 1) * epg]) for g in range(N_EXPERT_GROUPS)]
    _, gsel = _argmax_first(gscore)

    def pick(vals):
        return [functools.reduce(lambda acc, g: jnp.where(gsel == g, vals[g * epg + i], acc),
                                 range(1, N_EXPERT_GROUPS), vals[i]) for i in range(epg)]

    in_b = pick(bi)
    in_s = pick(sc)
    _, i0 = _argmax_first(in_b)
    _, i1 = _argmax_first([jnp.where(i0 == i, -jnp.inf, in_b[i]) for i in range(epg)])

    def take(vals, idx):
        return functools.reduce(lambda acc, i: jnp.where(idx == i, vals[i], acc), range(1, epg), vals[0])

    s0, s1 = take(in_s, i0), take(in_s, i1)
    w0, w1 = s0 / (s0 + s1), s1 / (s0 + s1)
    return gsel * epg + i0, gsel * epg + i1, w0, w1


def _xattn_rank(rows, selection, run, rt_ref):
    e0, e1, w0, w1 = selection
    tq = rows.stop - rows.start
    eid = lax.broadcasted_iota(I32, (N_EXPERTS, tq), 0)
    oh0 = eid == e0
    oh1 = eid == e1
    onehot = jnp.where(jnp.logical_or(oh0, oh1), 1.0, 0.0)
    rr = lax.broadcasted_iota(I32, (tq, tq), 0)
    cc = lax.broadcasted_iota(I32, (tq, tq), 1)
    upper = jnp.where(rr < cc, 1.0, 0.0).astype(BF16)
    prefix = _dot(onehot.astype(BF16), upper) + run
    r0 = jnp.sum(jnp.where(oh0, prefix, 0.0), axis=0, keepdims=True).astype(I32)
    r1 = jnp.sum(jnp.where(oh1, prefix, 0.0), axis=0, keepdims=True).astype(I32)

    zero = jnp.zeros((1, tq), I32)
    rt_ref[:, rows] = jnp.concatenate(
        [e0, e1, r0, r1, pltpu.bitcast(w0, I32), pltpu.bitcast(w1, I32), zero, zero], axis=0)
    return run + jnp.sum(onehot, axis=1, keepdims=True)


def _xattn_route(x1, kv, mem_len, wq, wo, g2, b2, rw_hi, rw_lo, rbias, b0, bsz, s_len):
    t_all, d = x1.shape
    tq = TOKEN_TILE
    nj = s_len // tq
    m = mem_len
    row = lambda bb, jj: (bb * nj + jj, 0)
    return pl.pallas_call(
        _xattn_kernel,
        grid=(bsz, nj),
        in_specs=[
            pl.BlockSpec((tq, d), row),
            pl.BlockSpec((m, 2 * d), lambda bb, jj: (b0 + bb, 0)),
            _const_spec((d, d)), _const_spec((d, d)),
            _const_spec((1, d)), _const_spec((1, d)),
            _const_spec((N_EXPERTS, d)), _const_spec((N_EXPERTS, d)),
            _const_spec((N_EXPERTS, 1)),
        ],
        out_specs=[
            pl.BlockSpec((tq, d), row),
            pl.BlockSpec((tq, d // 2), row),
            pl.BlockSpec((8, tq), lambda bb, jj: (0, bb * nj + jj)),
            _const_spec((N_EXPERTS, LANES)),
        ],
        out_shape=[
            jax.ShapeDtypeStruct((t_all, d), F32),
            jax.ShapeDtypeStruct((t_all, d // 2), I32),
            jax.ShapeDtypeStruct((8, t_all), I32),
            jax.ShapeDtypeStruct((N_EXPERTS, LANES), I32),
        ],
        scratch_shapes=[pltpu.VMEM((N_EXPERTS, 1), F32), pltpu.VMEM((d, d), BF16), pltpu.VMEM((d, d), BF16)],
        compiler_params=_tc_params(2),
        name="xattn_route",
    )(x1, kv, wq, wo, g2, b2, rw_hi, rw_lo, rbias)


def _sc_mesh():
    return plsc.VectorSubcoreMesh(core_axis_name="c", subcore_axis_name="s")


def _sc_params():
    return pltpu.CompilerParams(needs_layout_passes=False)


def _worker_id():
    return lax.axis_index("s") * lax.axis_size("c") + lax.axis_index("c")


def _sc_dispatch(xp, dest2d, n_rows):
    t_all, width = xp.shape
    chunk = dest2d.shape[1]
    tok_w = t_all // SC_WORKERS
    nch = tok_w // chunk
    slot1 = t_all // chunk
    assert t_all % (SC_WORKERS * chunk * 2) == 0

    def body(x_hbm, dest_hbm, out_hbm, idx0_v, idx1_v, buf0, buf1, sem_r, sem_w):
        wid = _worker_id()
        base = wid * tok_w
        pltpu.sync_copy(dest_hbm.at[pl.ds(wid * nch, nch)], idx0_v)
        pltpu.sync_copy(dest_hbm.at[pl.ds(slot1 + wid * nch, nch)], idx1_v)

        def read(c, buf, k):
            return pltpu.make_async_copy(x_hbm.at[pl.ds(base + c * chunk, chunk)], buf, sem_r.at[k])

        def scatter(c, buf):
            a = pltpu.make_async_copy(buf, out_hbm.at[idx0_v.at[c]], sem_w.at[0])
            b = pltpu.make_async_copy(buf, out_hbm.at[idx1_v.at[c]], sem_w.at[1])
            a.start()
            b.start()
            a.wait()
            b.wait()

        read(0, buf0, 0).start()

        @pl.loop(0, nch // 2)
        def _(g):
            c = 2 * g
            read(c + 1, buf1, 1).start()
            read(c, buf0, 0).wait()
            scatter(c, buf0)

            @pl.when(c + 2 < nch)
            def _():
                read(c + 2, buf0, 0).start()

            read(c + 1, buf1, 1).wait()
            scatter(c + 1, buf1)

    return pl.kernel(
        body,
        out_type=jax.ShapeDtypeStruct((n_rows, width), xp.dtype),
        mesh=_sc_mesh(),
        scratch_types=[
            pltpu.VMEM((nch, chunk), I32),
            pltpu.VMEM((nch, chunk), I32),
            pltpu.VMEM((chunk, width), xp.dtype),
            pltpu.VMEM((chunk, width), xp.dtype),
            pltpu.SemaphoreType.DMA((2,)),
            pltpu.SemaphoreType.DMA((2,)),
        ],
        compiler_params=_sc_params(),
        name="sc_dispatch",
    )(xp, dest2d)


def _sc_pack_weights(w2d):
    n_rows, width = w2d.shape
    half = width // 2
    rows_w = n_rows // SC_WORKERS
    chunk = (32 * 1024) // width
    lanes = 16
    assert n_rows % (SC_WORKERS * chunk) == 0 and half % lanes == 0

    def body(w_hbm, out_hbm, in_v, out_v):
        base = _worker_id() * rows_w

        def rne_hi16(bits):
            return bits + 0x7FFF + jnp.bitwise_and(lax.shift_right_logical(bits, 16), 1)

        @pl.loop(0, rows_w // chunk)
        def _(c):
            r0 = base + c * chunk
            pltpu.sync_copy(w_hbm.at[pl.ds(r0, chunk)], in_v)

            @pl.loop(0, chunk)
            def _(r):
                @plsc.parallel_loop(0, half, step=lanes, unroll=4)
                def _(j):
                    lo = rne_hi16(plsc.bitcast(in_v[r, pl.ds(j, lanes)], I32))
                    hi = rne_hi16(plsc.bitcast(in_v[r, pl.ds(half + j, lanes)], I32))
                    out_v[r, pl.ds(j, lanes)] = jnp.bitwise_or(
                        lax.shift_right_logical(lo, 16), jnp.bitwise_and(hi, jnp.int32(-65536)))

            pltpu.sync_copy(out_v, out_hbm.at[pl.ds(r0, chunk)])

    return pl.kernel(
        body,
        out_type=jax.ShapeDtypeStruct((n_rows, half), I32),
        mesh=_sc_mesh(),
        scratch_types=[pltpu.VMEM((chunk, width), F32), pltpu.VMEM((chunk, half), I32)],
        compiler_params=_sc_params(),
        name="sc_pack_weights",
    )(w2d)


def _sc_combine(ys, dest2d):
    n_idx_rows, chunk = dest2d.shape
    width = ys.shape[1]
    nch = n_idx_rows // SC_WORKERS
    assert n_idx_rows % (SC_WORKERS * 2) == 0

    def body(y_hbm, dest_hbm, out_hbm, idx_v, buf0, buf1, sem_g):
        wid = _worker_id()
        base = wid * nch * chunk
        pltpu.sync_copy(dest_hbm.at[pl.ds(wid * nch, nch)], idx_v)

        def gather(c, buf, k):
            return pltpu.make_async_copy(y_hbm.at[idx_v.at[c]], buf, sem_g.at[k])

        def write(c, buf):
            pltpu.sync_copy(buf, out_hbm.at[pl.ds(base + c * chunk, chunk)])

        gather(0, buf0, 0).start()

        @pl.loop(0, nch // 2)
        def _(g):
            c = 2 * g
            gather(c + 1, buf1, 1).start()
            gather(c, buf0, 0).wait()
            write(c, buf0)

            @pl.when(c + 2 < nch)
            def _():
                gather(c + 2, buf0, 0).start()

            gather(c + 1, buf1, 1).wait()
            write(c + 1, buf1)

    return pl.kernel(
        body,
        out_type=jax.ShapeDtypeStruct((n_idx_rows * chunk, width), ys.dtype),
        mesh=_sc_mesh(),
        scratch_types=[
            pltpu.VMEM((nch, chunk), I32),
            pltpu.VMEM((chunk, width), ys.dtype),
            pltpu.VMEM((chunk, width), ys.dtype),
            pltpu.SemaphoreType.DMA((2,)),
        ],
        compiler_params=_sc_params(),
        name="sc_combine",
    )(ys, dest2d)


def _pack_bf16_pairs(v):
    half = v.shape[1] // 2
    lo = pltpu.bitcast(v[:, :half].astype(BF16).astype(F32), jnp.uint32) >> 16
    hi = pltpu.bitcast(v[:, half:].astype(BF16).astype(F32), jnp.uint32) & jnp.uint32(0xFFFF0000)
    return pltpu.bitcast(hi | lo, I32)


def _unpack_bf16_pairs(w):
    w = pltpu.bitcast(w, jnp.uint32)
    lo = pltpu.bitcast(w << 16, F32)
    hi = pltpu.bitcast(w & jnp.uint32(0xFFFF0000), F32)
    return jnp.concatenate([lo, hi], axis=1)


def _ffn_kernel(be_ref, nv_ref, nu_ref, nxt_ref, xs_ref, wg_hbm, wu_hbm, wd_hbm, o_ref,
                wg_f, wu_f, wd_f, wg_s, wu_s, wd_s, sem):
    i = pl.program_id(0)

    def fetch(e):
        return (pltpu.make_async_copy(wg_hbm.at[e], wg_f, sem.at[0]),
                pltpu.make_async_copy(wu_hbm.at[e], wu_f, sem.at[1]),
                pltpu.make_async_copy(wd_hbm.at[e], wd_f, sem.at[2]))

    @pl.when(nxt_ref[i] >= 0)
    def _():
        @pl.when(i == 0)
        def _():
            for cp in fetch(be_ref[0]):
                cp.start()

        for cp in fetch(be_ref[i]):
            cp.wait()
        wg_s[...] = _unpack_bf16_pairs(wg_f[...]).astype(BF16)
        wu_s[...] = _unpack_bf16_pairs(wu_f[...]).astype(BF16)
        wd_s[...] = _unpack_bf16_pairs(wd_f[...]).astype(BF16)

        @pl.when(nxt_ref[i] < N_EXPERTS)
        def _():
            for cp in fetch(nxt_ref[i]):
                cp.start()

    @pl.when(i < nu_ref[0])
    def _():
        live = lax.broadcasted_iota(I32, xs_ref.shape, 0) < nv_ref[i]
        xb = _unpack_bf16_pairs(jnp.where(live, xs_ref[...], 0)).astype(BF16)
        act = jax.nn.silu(_dot(xb, wg_s[...])) * _dot(xb, wu_s[...])
        o_ref[...] = _pack_bf16_pairs(_dot(act.astype(BF16), wd_s[...]))


def _expert_ffn(xs, blk_expert, blk_valid, n_used, blk_next, w_gate, w_up, w_down):
    n_rows, half = xs.shape
    d = 2 * half
    de = 2 * w_gate.shape[2]
    bm = EXPERT_ROWS
    rows = lambda i, be, nv, nu, nx: (jnp.minimum(i, nu[0] - 1), 0)
    hbm = pl.BlockSpec(memory_space=pl.ANY)
    return pl.pallas_call(
        _ffn_kernel,
        grid_spec=pltpu.PrefetchScalarGridSpec(
            num_scalar_prefetch=4,
            grid=(n_rows // bm,),
            in_specs=[pl.BlockSpec((bm, half), rows), hbm, hbm, hbm],
            out_specs=pl.BlockSpec((bm, half), rows),
            scratch_shapes=[
                pltpu.VMEM((d, de // 2), I32), pltpu.VMEM((d, de // 2), I32), pltpu.VMEM((de, d // 2), I32),
                pltpu.VMEM((d, de), BF16), pltpu.VMEM((d, de), BF16), pltpu.VMEM((de, d), BF16),
                pltpu.SemaphoreType.DMA((3,)),
            ],
        ),
        out_shape=jax.ShapeDtypeStruct((n_rows, half), I32),
        compiler_params=_tc_params(1),
        name="expert_ffn",
    )(blk_expert, blk_valid, n_used, blk_next, xs, w_gate, w_up, w_down)


def _combine_kernel(x_ref, y0_ref, y1_ref, rt_ref, g_ref, b_ref, o_ref):
    o_ref[...] = _expert_combine_ln(x_ref[...], y0_ref[...], y1_ref[...], rt_ref[...], g_ref[...], b_ref[...])


def _combine_kernel_into(x_ref, y0_ref, y1_ref, rt_ref, g_ref, b_ref, full_ref, o_ref):
    del full_ref
    _combine_kernel(x_ref, y0_ref, y1_ref, rt_ref, g_ref, b_ref, o_ref)


def _combine_ln(x2, y01, rt, g3, b3, into=None, row0=0, full_rows=None):
    t_all, d = x2.shape
    tq = TOKEN_TILE
    nt = t_all // tq
    blk0 = row0 // tq if full_rows else 0
    in_specs = [
        pl.BlockSpec((tq, d), lambda i: (i, 0)),
        pl.BlockSpec((tq, d // 2), lambda i: (i, 0)),
        pl.BlockSpec((tq, d // 2), lambda i: (i + nt, 0)),
        pl.BlockSpec((8, tq), lambda i: (0, i)),
        _const_spec((1, d)), _const_spec((1, d)),
    ]
    args = [x2, y01, y01, rt, g3, b3]
    body, aliases = _combine_kernel, {}
    if into is not None:
        in_specs.append(pl.BlockSpec(memory_space=pl.ANY))
        args.append(into)
        body, aliases = _combine_kernel_into, {len(args) - 1: 0}
    return pl.pallas_call(
        body,
        grid=(nt,),
        in_specs=in_specs,
        out_specs=pl.BlockSpec((tq, d), lambda i: (i + blk0, 0)),
        out_shape=jax.ShapeDtypeStruct((full_rows or t_all, d), F32),
        input_output_aliases=aliases,
        compiler_params=_tc_params(1),
        name="combine_ln",
    )(*args)


def _routing_tables(rt, counts):
    bm = EXPERT_ROWS
    t_all = rt.shape[1]
    n_rows = 2 * t_all + N_EXPERTS * bm
    cnt = counts[:, 0]
    padded = (cnt + bm - 1) // bm * bm
    ends = jnp.cumsum(padded)
    offs = ends - padded
    experts = rt[0:2]
    off_tok = jnp.sum(jnp.where(experts[None] == jnp.arange(N_EXPERTS, dtype=I32)[:, None, None],
                                offs[:, None, None], 0), axis=0)
    dest2d = (off_tok + rt[2:4]).reshape(-1, SC_CHUNK).astype(I32)
    blk_start = jnp.arange(n_rows // bm, dtype=I32) * bm
    blk_expert = jnp.minimum(jnp.sum(blk_start[:, None] >= ends[None, :], axis=1), N_EXPERTS - 1).astype(I32)
    live_end = jnp.sum(jnp.where(blk_expert[:, None] == jnp.arange(N_EXPERTS, dtype=I32)[None, :],
                                 (offs + cnt)[None, :], 0), axis=1)
    blk_valid = jnp.clip(live_end - blk_start, 0, bm).astype(I32)
    n_used = (ends[-1:] // bm).astype(I32)
    eid = jnp.arange(N_EXPERTS, dtype=I32)
    later_present = jnp.logical_and(eid[None, :] > eid[:, None], (cnt > 0)[None, :])
    next_present = jnp.min(jnp.where(later_present, eid[None, :], N_EXPERTS), axis=1)
    next_of_blk = jnp.sum(jnp.where(blk_expert[:, None] == eid[None, :], next_present[None, :], 0), axis=1)
    prev_expert = jnp.concatenate([jnp.full((1,), -1, I32), blk_expert[:-1]])
    is_first = jnp.logical_and(blk_start < ends[-1], blk_expert != prev_expert)
    blk_next = jnp.where(is_first, next_of_blk, -1).astype(I32)
    return dest2d, blk_expert, blk_valid, n_used, blk_next, n_rows


def _layer_tail(x1, kv, mem_len, p, router, b0, bsz, s_len):
    rw_hi, rw_lo, rbias = router
    x2, xp, rt, counts = _xattn_route(x1, kv, mem_len, p["xq"], p["xo"], p["ln2_g"], p["ln2_b"],
                                      rw_hi, rw_lo, rbias, b0, bsz, s_len)
    dest2d, blk_expert, blk_valid, n_used, blk_next, n_rows = _routing_tables(rt, counts)
    xs = _sc_dispatch(xp, dest2d, n_rows)
    if "packed" not in p:
        p["packed"] = tuple(_sc_pack_weights(w.reshape(-1, w.shape[2])).reshape(w.shape[0], w.shape[1], -1)
                            for w in (p["e_gate"], p["e_up"], p["e_down"]))
    ys = _expert_ffn(xs, blk_expert, blk_valid, n_used, blk_next, *p["packed"])
    y01 = _sc_combine(ys, dest2d)
    return x2, y01, rt


def _row(v):
    return v.reshape(1, -1).astype(F32)


def _common_params(xq, xkv, xo, ln2_g, ln2_b, e_gate, e_up, e_down, ln3_g, ln3_b):
    return dict(xq=xq, xkv=xkv, xo=xo, ln2_g=_row(ln2_g), ln2_b=_row(ln2_b),
                e_gate=e_gate, e_up=e_up, e_down=e_down, ln3_g=_row(ln3_g), ln3_b=_row(ln3_b))


def kernel(x, mem, positions, router_w, router_bias, l0_w_in, l0_sinks, l0_sgu_ln_g, l0_sgu_ln_b, l0_sgu_w, l0_sgu_b, l0_w_out, l0_ln1_g, l0_ln1_b, l0_xq, l0_xkv, l0_xo, l0_ln2_g, l0_ln2_b, l0_e_gate, l0_e_up, l0_e_down, l0_ln3_g, l0_ln3_b, l1_w_in, l1_pool_w, l1_pool_scale, l1_w_out, l1_ln1_g, l1_ln1_b, l1_xq, l1_xkv, l1_xo, l1_ln2_g, l1_ln2_b, l1_e_gate, l1_e_up, l1_e_down, l1_ln3_g, l1_ln3_b):
    bsz, s_len, d = x.shape
    assert s_len % TOKEN_TILE == 0 and TOKEN_TILE == 2 * SLAB_ROWS and SLAB_ROWS % BLOCK == 0
    xt = x.reshape(bsz * s_len, d)
    mem2d = mem.reshape(-1, d)

    rw_t = router_w.T.astype(F32)
    rw_hi = rw_t.astype(BF16)
    rw_lo = (rw_t - rw_hi.astype(F32)).astype(BF16)
    router = (rw_hi, rw_lo, router_bias.reshape(-1, 1).astype(F32))

    half = ROPE_DIM // 2
    inv_freq = (ROPE_THETA ** (-(jnp.arange(half, dtype=F32) * 2.0 / ROPE_DIM))).reshape(half, 1)
    etab_np = np.zeros((LANES, 3 * LANES), np.float32)
    cbase_np = np.ones((1, LANES), np.float32)
    for ln in range(LANES):
        dd = ln % HEAD_DIM
        if dd < ROPE_DIM:
            cbase_np[0, ln] = 0.0
            etab_np[[dd % half, half + dd % half], ln] = 1.0
            if dd >= half:
                etab_np[[2 * half + dd - half, 3 * half + dd - half], LANES + ln] = 1.0
            else:
                etab_np[[2 * half + dd, 3 * half + dd], 2 * LANES + ln] = -1.0
    etab = jnp.asarray(etab_np, BF16)
    cbase = jnp.asarray(cbase_np)
    pos_row = positions.reshape(1, -1).astype(I32)
    grp = jnp.arange(PIECE_COLS) // B_GROUP_DIM
    gsum = (grp[:, None] == grp[None, :]).astype(BF16)
    bs_full = jnp.repeat(l0_sgu_b.T.astype(F32), B_GROUP_DIM, axis=1)

    p0 = _common_params(l0_xq, l0_xkv, l0_xo, l0_ln2_g, l0_ln2_b, l0_e_gate, l0_e_up, l0_e_down,
                        l0_ln3_g, l0_ln3_b)
    p1 = _common_params(l1_xq, l1_xkv, l1_xo, l1_ln2_g, l1_ln2_b, l1_e_gate, l1_e_up, l1_e_down,
                        l1_ln3_g, l1_ln3_b)
    kv0 = _kv_proj(mem2d, p0["xkv"], bsz)
    kv1 = _kv_proj(mem2d, p1["xkv"], bsz)
    mem_len = mem.shape[1]

    n_split = BATCH_SPLIT if bsz % BATCH_SPLIT == 0 else 1
    nb = bsz // n_split
    out = None
    for part in range(n_split):
        b0 = part * nb
        x1 = _mixer0(xt, pos_row, l0_sinks.astype(F32), l0_w_in, inv_freq, etab, cbase, gsum,
                     _row(l0_sgu_ln_g), _row(l0_sgu_ln_b), l0_sgu_w.astype(F32), bs_full,
                     l0_w_out, _row(l0_ln1_g), _row(l0_ln1_b), b0, nb, s_len)
        x2, y01, rt = _layer_tail(x1, kv0, mem_len, p0, router, b0, nb, s_len)
        x1 = _mixer1(x2, y01, rt, p0["ln3_g"], p0["ln3_b"], l1_w_in, l1_pool_w, _row(l1_pool_scale),
                     l1_w_out, _row(l1_ln1_g), _row(l1_ln1_b), nb, s_len)
        x2, y01, rt = _layer_tail(x1, kv1, mem_len, p1, router, b0, nb, s_len)
        out = _combine_ln(x2, y01, rt, p1["ln3_g"], p1["ln3_b"], into=out, row0=b0 * s_len,
                          full_rows=bsz * s_len)
    return out.reshape(bsz, s_len, d)
```

```python
import functools

import numpy as np
import jax
import jax.numpy as jnp
from jax import lax
from jax.experimental import pallas as pl
from jax.experimental.pallas import tpu as pltpu
from jax.experimental.pallas import tpu_sc as plsc

F32 = jnp.float32
BF16 = jnp.bfloat16
I32 = jnp.int32

DEPTH = 2
ALPHA = (2.0 * DEPTH) ** 0.25
LN_EPS = 1e-5

HEAD_DIM = 64
A_Q_HEADS = 8
A_KV_HEADS = 2
A_GROUP = A_Q_HEADS // A_KV_HEADS
BLOCK = 128
ROPE_THETA = 500000.0
ROPE_DIM = HEAD_DIM // 4
A_WIDTH = A_Q_HEADS * HEAD_DIM
KV_WIDTH = A_KV_HEADS * HEAD_DIM
B_GROUPS = 8
B_GROUP_DIM = 64
B_WIDTH = B_GROUPS * B_GROUP_DIM
POOL_WINDOWS = (2, 4, 8, 16)
POOL_HALO = 16
X_HEADS = 4
N_EXPERTS = 16
N_EXPERT_GROUPS = 4
EXPERTS_PER_GROUP = 4

LANES = 128
TOKEN_TILE = 1024
SLAB_ROWS = 512
PIECE_COLS = 256
EXPERT_ROWS = 512
FFN_GROUP = 2
BATCH_SPLIT = 2
SC_WORKERS = 32
SC_CHUNK = 64
VMEM_LIMIT = 56 * 1024 * 1024
NEG_BIG = -1e30


def _layer_norm(z, g, b):
    mu = jnp.mean(z, axis=-1, keepdims=True)
    d = z - mu
    var = jnp.mean(d * d, axis=-1, keepdims=True)
    return d * lax.rsqrt(var + LN_EPS) * g + b


def _dot(a, b):
    return jnp.dot(a, b, preferred_element_type=F32)


def _dot_nt(a, b):
    return lax.dot_general(a, b, (((1,), (1,)), ((), ())), preferred_element_type=F32)


def _split_bf16(v):
    hi = v.astype(BF16)
    lo = (v - hi.astype(F32)).astype(BF16)
    return hi, lo


def _tc_params(n_axes):
    return pltpu.CompilerParams(dimension_semantics=("arbitrary",) * n_axes,
                                vmem_limit_bytes=VMEM_LIMIT)


def _const_spec(shape):
    nd = len(shape)
    return pl.BlockSpec(shape, lambda *_: (0,) * nd, pipeline_mode=pl.Buffered(1))


def _mixer0_kernel(sinks_ref, x_ref, pos_ref, win_ref, invf_ref, etab_ref, cbase_ref, gsum_ref,
                   lng_ref, lnb_ref, ws_ref, bs_ref, wout_ref, g1_ref, b1_ref,
                   o_ref, q_s, kv_s, u_s, vn_s, mix_s, wt_s, win_s, wout_s):
    b = pl.program_id(0)
    j = pl.program_id(1)
    tq = x_ref.shape[0]
    kvw = kv_s.shape[1]

    @pl.when(jnp.logical_and(b == 0, j == 0))
    def _():
        win_s[...] = win_ref[...].astype(BF16)
        wout_s[...] = wout_ref[...].astype(BF16)
        r = lax.broadcasted_iota(I32, (BLOCK, BLOCK), 0)
        c = lax.broadcasted_iota(I32, (BLOCK, BLOCK), 1)
        for g in range(B_GROUPS):
            wt_s[g] = jnp.where(c <= r, ws_ref[g], 0.0).astype(BF16)

    @pl.when(j == 0)
    def _():
        kv_s[0:BLOCK, :] = jnp.zeros((BLOCK, kvw), BF16)

    c1 = A_WIDTH
    c2 = c1 + KV_WIDTH
    c3 = c2 + KV_WIDTH
    c4 = c3 + B_WIDTH

    def rotary_tables(rows):
        n = rows.stop - rows.start
        ang = invf_ref[...] * pos_ref[:, rows].astype(F32)
        c8 = jnp.cos(ang)
        s8 = jnp.sin(ang)
        c8h = c8.astype(BF16).astype(F32)
        s8h = s8.astype(BF16).astype(F32)
        pad = jnp.zeros((LANES - 4 * c8.shape[0], n), F32)
        stack = jnp.concatenate([c8h, c8 - c8h, s8h, s8 - s8h, pad], axis=0)
        tabs = _dot(stack.T.astype(BF16), etab_ref[...])
        return tabs[:, 0:LANES] + cbase_ref[...], tabs[:, LANES:2 * LANES], tabs[:, 2 * LANES:]

    pw = PIECE_COLS
    n_pieces = win_s.shape[1] // pw
    assert (c1 // pw, c2 // pw, c3 // pw, c4 // pw) == (2, 2, 3, 5) and c3 % pw == 0 and n_pieces == 7

    def project(xb, k):
        return _dot(xb, win_s[:, k * pw:(k + 1) * pw])

    def prepare(rows, hk, tables, k):
        n = rows.stop - rows.start
        cs, sa, sb = tables

        def rope(t):
            return t * cs + pltpu.roll(t, ROPE_DIM // 2, 1) * sa + pltpu.roll(t, LANES - ROPE_DIM // 2, 1) * sb

        if k < 2:
            for c in range(pw // LANES):
                t = hk[:, c * LANES:(c + 1) * LANES] * (HEAD_DIM ** -0.5)
                col = k * (pw // LANES) + c
                q_s[rows, col * LANES:(col + 1) * LANES] = rope(t).astype(BF16)
        elif k == 2:
            low = lax.broadcasted_iota(I32, (n, LANES), 1) < HEAD_DIM
            kr = rope(hk[:, 0:KV_WIDTH])
            kx = pltpu.roll(kr, HEAD_DIM, 1)
            vr = hk[:, KV_WIDTH:]
            vx = pltpu.roll(vr, HEAD_DIM, 1)
            kv_cols = [jnp.where(low, kr, kx), jnp.where(low, kx, kr),
                       jnp.where(low, vr, 0.0), jnp.where(low, 0.0, vx),
                       jnp.where(low, vx, 0.0), jnp.where(low, 0.0, vr)]
            for c, col in enumerate(kv_cols):
                kv_s[BLOCK + rows.start:BLOCK + rows.stop, c * LANES:(c + 1) * LANES] = col.astype(BF16)
        elif k < 5:
            lo = (k - 3) * pw
            u_s[rows, lo:lo + pw] = jax.nn.gelu(hk)
        else:
            lo = (k - 5) * pw
            v = jax.nn.gelu(hk)
            gsum = gsum_ref[...]
            mean = _dot(v.astype(BF16), gsum) * (1.0 / B_GROUP_DIM)
            d = v - mean
            var = _dot((d * d).astype(BF16), gsum) * (1.0 / B_GROUP_DIM)
            vn_s[rows, lo:lo + pw] = (d * lax.rsqrt(var + LN_EPS) * lng_ref[:, lo:lo + pw]
                                      + lnb_ref[:, lo:lo + pw]).astype(BF16)

    qi = lax.broadcasted_iota(I32, (BLOCK, 2 * BLOCK), 0)
    kj = lax.broadcasted_iota(I32, (BLOCK, 2 * BLOCK), 1)
    rel = qi + BLOCK - kj
    band = jnp.logical_and(rel >= 0, rel < BLOCK)
    low_q = lax.broadcasted_iota(I32, (BLOCK, LANES), 1) < HEAD_DIM
    low_k = lax.broadcasted_iota(I32, (2 * BLOCK, LANES), 1) < HEAD_DIM
    ones_lo = jnp.where(low_k, 1.0, 0.0).astype(BF16)
    ones_hi = jnp.where(low_k, 0.0, 1.0).astype(BF16)
    zero_q = jnp.zeros((BLOCK, LANES), BF16)

    def block_body(n):
        r0 = n * BLOCK
        kv = kv_s[pl.ds(r0, 2 * BLOCK), :]
        qb = q_s[pl.ds(r0, BLOCK), :]
        valid = jnp.logical_and(band, kj >= jnp.where(j == 0, BLOCK, 0)) if n == 0 else band
        cols_per_kv = A_GROUP // 2
        scores = {}
        for hk in range(A_KV_HEADS):
            cols = range(hk * cols_per_kv, (hk + 1) * cols_per_kv)
            pieces = []
            for c in cols:
                qp = qb[:, c * LANES:(c + 1) * LANES]
                pieces += [jnp.where(low_q, qp, zero_q), jnp.where(low_q, zero_q, qp)]
            sc = _dot_nt(jnp.concatenate(pieces, axis=0), kv[:, hk * LANES:(hk + 1) * LANES])
            for i, c in enumerate(cols):
                for half in range(2):
                    r = (2 * i + half) * BLOCK
                    scores[c, half] = sc[r:r + BLOCK, :]
        vnb = vn_s[pl.ds(r0, BLOCK), :]
        parts = []
        for c in range(B_WIDTH // LANES):
            vp = vnb[:, c * LANES:(c + 1) * LANES]
            parts.append(_dot(wt_s[2 * c], jnp.where(low_q, vp, zero_q))
                         + _dot(wt_s[2 * c + 1], jnp.where(low_q, zero_q, vp)))
        probs, esink = {}, {}
        for (c, half), sc in scores.items():
            s = jnp.where(valid, sc, NEG_BIG)
            sink = sinks_ref[2 * c + half]
            m = jnp.maximum(jnp.max(s, axis=-1, keepdims=True), sink)
            probs[c, half] = jnp.exp(s - m).astype(BF16)
            esink[c, half] = jnp.exp(sink - m)
        res = {}
        for hk in range(A_KV_HEADS):
            cols = range(hk * cols_per_kv, (hk + 1) * cols_per_kv)
            for half in range(2):
                vcol = kv[:, (2 + 2 * hk + half) * LANES:(3 + 2 * hk + half) * LANES]
                vm = jnp.concatenate([vcol, ones_lo if half == 0 else ones_hi], axis=1)
                pv = _dot(jnp.concatenate([probs[c, half] for c in cols], axis=0), vm)
                for i, c in enumerate(cols):
                    part = pv[i * BLOCK:(i + 1) * BLOCK, :]
                    res[c] = part if half == 0 else res[c] + part
        for c in range(A_WIDTH // LANES):
            den = res[c][:, LANES:] + jnp.where(low_q, esink[c, 0], esink[c, 1])
            mix_s[pl.ds(r0, BLOCK), c * LANES:(c + 1) * LANES] = (res[c][:, :LANES] / den).astype(BF16)
        mixed = jnp.concatenate(parts, axis=1) + bs_ref[...]
        mix_s[pl.ds(r0, BLOCK), A_WIDTH:] = (u_s[pl.ds(r0, BLOCK), :] * mixed).astype(BF16)

    def out_cols(rows, c):
        return _dot(mix_s[rows, :], wout_s[:, c * pw:(c + 1) * pw])

    def finish(rows, z_cols, i):
        r = slice(rows.start + i * BLOCK, rows.start + (i + 1) * BLOCK)
        z = ALPHA * x_ref[r, :] + jnp.concatenate([zc[i * BLOCK:(i + 1) * BLOCK, :] for zc in z_cols], axis=1)
        o_ref[r, :] = _layer_norm(z, g1_ref[...], b1_ref[...])

    assert tq == 2 * SLAB_ROWS and SLAB_ROWS == 4 * BLOCK and wout_s.shape[1] == 4 * pw
    sa, sb = slice(0, SLAB_ROWS), slice(SLAB_ROWS, tq)
    tab_a, tab_b = rotary_tables(sa), rotary_tables(sb)
    xa = x_ref[sa, :].astype(BF16)
    ha = [project(xa, k) for k in range(n_pieces)]
    xb = x_ref[sb, :].astype(BF16)
    hb = []
    for k in range(n_pieces):
        hb.append(project(xb, k))
        prepare(sa, ha[k], tab_a, k)
    pieces_b = iter(range(n_pieces))
    for n in range(4):
        block_body(n)
        for k in [next(pieces_b) for _ in range(2 if n < 3 else 1)]:
            prepare(sb, hb[k], tab_b, k)
    za = []
    for n in range(4):
        block_body(4 + n)
        za.append(out_cols(sa, n))
    kv_s[0:BLOCK, :] = kv_s[tq:tq + BLOCK, :]
    zb = []
    for n in range(4):
        zb.append(out_cols(sb, n))
        finish(sa, za, n)
    for n in range(4):
        finish(sb, zb, n)


def _mixer0(x, pos_row, sinks, w_in, invf, etab, cbase, gsum, lng, lnb, w_s, bs_full, w_out, g1, b1,
            b0, bsz, s_len):
    d = x.shape[1]
    t_all = bsz * s_len
    tq = TOKEN_TILE
    nj = s_len // tq
    row = lambda bb, jj: (bb * nj + jj, 0)
    in_w = w_in.shape[1]
    return pl.pallas_call(
        _mixer0_kernel,
        grid=(bsz, nj),
        in_specs=[
            pl.BlockSpec(memory_space=pltpu.SMEM),
            pl.BlockSpec((tq, d), lambda bb, jj: ((b0 + bb) * nj + jj, 0)),
            pl.BlockSpec((1, tq), lambda bb, jj: (0, (b0 + bb) * nj + jj)),
            _const_spec((d, in_w)),
            _const_spec((ROPE_DIM // 2, 1)), _const_spec((LANES, 3 * LANES)), _const_spec((1, LANES)),
            _const_spec((PIECE_COLS, PIECE_COLS)),
            _const_spec((1, B_WIDTH)), _const_spec((1, B_WIDTH)),
            _const_spec((B_GROUPS, BLOCK, BLOCK)),
            _const_spec((BLOCK, B_WIDTH)),
            _const_spec((A_WIDTH + B_WIDTH, d)),
            _const_spec((1, d)), _const_spec((1, d)),
        ],
        out_specs=pl.BlockSpec((tq, d), row),
        out_shape=jax.ShapeDtypeStruct((t_all, d), F32),
        scratch_shapes=[
            pltpu.VMEM((tq, A_WIDTH), BF16),
            pltpu.VMEM((tq + BLOCK, 6 * LANES), BF16),
            pltpu.VMEM((tq, B_WIDTH), F32),
            pltpu.VMEM((tq, B_WIDTH), BF16),
            pltpu.VMEM((tq, A_WIDTH + B_WIDTH), BF16),
            pltpu.VMEM((B_GROUPS, BLOCK, BLOCK), BF16),
            pltpu.VMEM((d, in_w), BF16),
            pltpu.VMEM((A_WIDTH + B_WIDTH, d), BF16),
        ],
        compiler_params=_tc_params(2),
        name="mixer0",
    )(sinks, x, pos_row, w_in, invf, etab, cbase, gsum, lng, lnb, w_s, bs_full, w_out, g1, b1)


def _expert_combine_ln(x2, y0_packed, y1_packed, rt, g, b):
    wt = pltpu.bitcast(rt, F32).T
    y = wt[:, 4:5] * _unpack_bf16_pairs(y0_packed) + wt[:, 5:6] * _unpack_bf16_pairs(y1_packed)
    return _layer_norm(ALPHA * x2 + y, g, b)


def _mixer1_kernel(x2_ref, y0_ref, y1_ref, rt_ref, g3_ref, b3_ref,
                   win_ref, pw_ref, ps_ref, wout_ref, g1_ref, b1_ref, o_ref,
                   h_s, mp_s, win_s, pw_s, wout_s):
    j = pl.program_id(1)
    tq = x2_ref.shape[0]
    gw = x2_ref.shape[1] // len(POOL_WINDOWS)
    slabs = [slice(c * SLAB_ROWS, (c + 1) * SLAB_ROWS) for c in range(tq // SLAB_ROWS)]

    @pl.when(jnp.logical_and(pl.program_id(0) == 0, j == 0))
    def _():
        win_s[...] = win_ref[...].astype(BF16)
        pw_s[...] = pw_ref[...].astype(BF16)
        wout_s[...] = wout_ref[...].astype(BF16)

    @pl.when(j == 0)
    def _():
        h_s[0:POOL_HALO, :] = jnp.zeros((POOL_HALO, h_s.shape[1]), F32)

    ng = len(POOL_WINDOWS)
    quarter = SLAB_ROWS // ng

    def sub(rows, i):
        return slice(rows.start + i * quarter, rows.start + (i + 1) * quarter)

    def load_in(rows, i):
        r = sub(rows, i)
        return _expert_combine_ln(x2_ref[r, :], y0_ref[r, :], y1_ref[r, :], rt_ref[:, r],
                                  g3_ref[...], b3_ref[...])

    def project(rows, xb, g):
        lo, hi = g * gw, (g + 1) * gw
        h_s[POOL_HALO + rows.start:POOL_HALO + rows.stop, lo:hi] = _dot(xb, win_s[:, lo:hi])

    def pool(rows, g):
        win = POOL_WINDOWS[g]
        lo, hi = g * gw, (g + 1) * gw
        n = rows.stop - rows.start
        t_pos = j * tq + rows.start + lax.broadcasted_iota(I32, (n, 1), 0)
        ext = h_s[rows.start:rows.stop + POOL_HALO, lo:hi]
        acc = ext
        shift = 1
        while shift < win:
            acc = acc + pltpu.roll(acc, shift, 0)
            shift *= 2
        count = jnp.minimum(t_pos + 1, win).astype(F32)
        pooled = acc[POOL_HALO:, :] / count - ext[POOL_HALO:, :]
        mapped = _dot(pooled.astype(BF16), pw_s[g])
        mp_s[rows, lo:hi] = (mapped * ps_ref[:, lo:hi]).astype(BF16)

    def out_cols(rows, g):
        return _dot(mp_s[rows, :], wout_s[:, g * gw:(g + 1) * gw])

    def finish(rows, x_parts, z_cols, i):
        lo, hi = i * quarter, (i + 1) * quarter
        z = ALPHA * x_parts[i] + jnp.concatenate([zc[lo:hi, :] for zc in z_cols], axis=1)
        o_ref[sub(rows, i), :] = _layer_norm(z, g1_ref[...], b1_ref[...])

    assert len(slabs) == 2
    sa, sb = slabs
    xa = [load_in(sa, i) for i in range(ng)]
    xa_b = jnp.concatenate(xa, axis=0).astype(BF16)
    xb = []
    for g in range(ng):
        project(sa, xa_b, g)
        xb.append(load_in(sb, g))
    xb_b = jnp.concatenate(xb, axis=0).astype(BF16)
    for g in range(ng):
        project(sb, xb_b, g)
        pool(sa, g)
    za = []
    for g in range(ng):
        za.append(out_cols(sa, g))
        pool(sb, g)
    zb = []
    for g in range(ng):
        zb.append(out_cols(sb, g))
        finish(sa, xa, za, g)
    for g in range(ng):
        finish(sb, xb, zb, g)
    h_s[0:POOL_HALO, :] = h_s[tq:tq + POOL_HALO, :]


def _mixer1(x2, y01, rt, g3, b3, w_in, pool_w, pool_scale, w_out, g1, b1, bsz, s_len):
    t_all, d = x2.shape
    tq = TOKEN_TILE
    nj = s_len // tq
    nt = bsz * nj
    row = lambda bb, jj: (bb * nj + jj, 0)
    ng = len(POOL_WINDOWS)
    return pl.pallas_call(
        _mixer1_kernel,
        grid=(bsz, nj),
        in_specs=[
            pl.BlockSpec((tq, d), row),
            pl.BlockSpec((tq, d // 2), row),
            pl.BlockSpec((tq, d // 2), lambda bb, jj: (nt + bb * nj + jj, 0)),
            pl.BlockSpec((8, tq), lambda bb, jj: (0, bb * nj + jj)),
            _const_spec((1, d)), _const_spec((1, d)),
            _const_spec((d, d)),
            _const_spec((ng, d // ng, d // ng)),
            _const_spec((1, d)),
            _const_spec((d, d)),
            _const_spec((1, d)), _const_spec((1, d)),
        ],
        out_specs=pl.BlockSpec((tq, d), row),
        out_shape=jax.ShapeDtypeStruct((t_all, d), F32),
        scratch_shapes=[pltpu.VMEM((tq + POOL_HALO, d), F32), pltpu.VMEM((tq, d), BF16),
                        pltpu.VMEM((d, d), BF16), pltpu.VMEM((ng, d // ng, d // ng), BF16),
                        pltpu.VMEM((d, d), BF16)],
        compiler_params=_tc_params(2),
        name="mixer1",
    )(x2, y01, y01, rt, g3, b3, w_in, pool_w, pool_scale, w_out, g1, b1)


def _kv_kernel(mem_ref, w_ref, o_ref, w_s):
    @pl.when(pl.program_id(0) == 0)
    def _():
        w_s[...] = w_ref[...].astype(BF16)

    o_ref[...] = _dot(mem_ref[...].astype(BF16), w_s[...]).astype(BF16)


def _kv_proj(mem2d, wkv, bsz):
    rows, d = mem2d.shape
    m = rows // bsz
    return pl.pallas_call(
        _kv_kernel,
        grid=(bsz,),
        in_specs=[pl.BlockSpec((m, d), lambda i: (i, 0)), _const_spec(wkv.shape)],
        out_specs=pl.BlockSpec((m, wkv.shape[1]), lambda i: (i, 0)),
        out_shape=jax.ShapeDtypeStruct((rows, wkv.shape[1]), BF16),
        scratch_shapes=[pltpu.VMEM(wkv.shape, BF16)],
        compiler_params=_tc_params(1),
        name="kv_proj",
    )(mem2d, wkv)


def _top2_of4(v):
    hi01, lo01 = jnp.maximum(v[0], v[1]), jnp.minimum(v[0], v[1])
    hi23, lo23 = jnp.maximum(v[2], v[3]), jnp.minimum(v[2], v[3])
    return jnp.maximum(hi01, hi23) + jnp.maximum(jnp.minimum(hi01, hi23), jnp.maximum(lo01, lo23))


def _argmax_first(vals):
    best, idx = vals[0], jnp.zeros(vals[0].shape, I32)
    for i in range(1, len(vals)):
        better = vals[i] > best
        best = jnp.where(better, vals[i], best)
        idx = jnp.where(better, i, idx)
    return best, idx


def _xattn_kernel(x_ref, kv_ref, wq_ref, wo_ref, g2_ref, b2_ref, rwh_ref, rwl_ref, rb_ref,
                  x2_ref, xp_ref, rt_ref, cnt_ref, run_s, wq_s, wo_s):
    first = jnp.logical_and(pl.program_id(0) == 0, pl.program_id(1) == 0)
    tq, d = x_ref.shape

    @pl.when(first)
    def _():
        run_s[...] = jnp.zeros(run_s.shape, F32)
        wq_s[...] = wq_ref[...].astype(BF16)
        wo_s[...] = wo_ref[...].astype(BF16)

    slabs = [slice(c * SLAB_ROWS, (c + 1) * SLAB_ROWS) for c in range(tq // SLAB_ROWS)]
    run = run_s[...]
    zs = [_xattn_attend(*_xattn_query(rows, x_ref, wq_s), kv_ref, wo_s) for rows in slabs]
    picks = []
    for rows, z in zip(slabs, zs):
        logits = _xattn_norm_logits(rows, z, g2_ref, b2_ref, rwh_ref, rwl_ref, x2_ref, xp_ref)
        picks.append(_xattn_select(logits, rb_ref))
    for rows, pick in zip(slabs, picks):
        run = _xattn_rank(rows, pick, run, rt_ref)
    run_s[...] = run
    cnt_ref[...] = jnp.broadcast_to(run, cnt_ref.shape).astype(I32)


def _xattn_query(rows, x_ref, wq_s):
    hd = x_ref.shape[1] // X_HEADS
    x = x_ref[rows, :]
    return x, _dot(x.astype(BF16), wq_s[...]) * (hd ** -0.5)


def _xattn_attend(x, q, kv_ref, wo_s):
    d = x.shape[1]
    hd = d // X_HEADS
    outs = []
    for hx in range(X_HEADS):
        qh = q[:, hx * hd:(hx + 1) * hd].astype(BF16)
        kh = kv_ref[:, hx * hd:(hx + 1) * hd]
        vh = kv_ref[:, d + hx * hd:d + (hx + 1) * hd]
        s = _dot_nt(qh, kh)
        p = jnp.exp(s - jnp.max(s, axis=-1, keepdims=True))
        o = _dot(p.astype(BF16), vh) / jnp.sum(p, axis=-1, keepdims=True)
        outs.append(o.astype(BF16))
    return ALPHA * x + _dot(jnp.concatenate(outs, axis=1), wo_s[...])


def _xattn_norm_logits(rows, z, g2_ref, b2_ref, rwh_ref, rwl_ref, x2_ref, xp_ref):
    x2 = _layer_norm(z, g2_ref[...], b2_ref[...])
    x2_ref[rows, :] = x2

    xp_ref[rows, :] = _pack_bf16_pairs(x2)

    xh, xl = _split_bf16(x2)
    both = _dot_nt(jnp.concatenate([rwh_ref[...], rwl_ref[...]], axis=0), xh)
    return both[0:N_EXPERTS] + both[N_EXPERTS:] + _dot_nt(rwh_ref[...], xl)


def _xattn_select(logits, rb_ref):
    e_max = jnp.max(logits, axis=0, keepdims=True)
    ex = jnp.exp(logits - e_max)
    scores = ex / jnp.sum(ex, axis=0, keepdims=True)
    biased = scores + rb_ref[...]
    sc = [scores[e:e + 1, :] for e in range(N_EXPERTS)]
    bi = [biased[e:e + 1, :] for e in range(N_EXPERTS)]
    epg = EXPERTS_PER_GROUP
    gscore = [_top2_of4(bi[g * epg:(g + 1) * epg]) for g in range(N_EXPERT_GROUPS)]
    _, gsel = _argmax_first(gscore)

    def pick(vals):
        return [functools.reduce(lambda acc, g: jnp.where(gsel == g, vals[g * epg + i], acc),
                                 range(1, N_EXPERT_GROUPS), vals[i]) for i in range(epg)]

    in_b = pick(bi)
    in_s = pick(sc)
    _, i0 = _argmax_first(in_b)
    _, i1 = _argmax_first([jnp.where(i0 == i, -jnp.inf, in_b[i]) for i in range(epg)])

    def take(vals, idx):
        return functools.reduce(lambda acc, i: jnp.where(idx == i, vals[i], acc), range(1, epg), vals[0])

    s0, s1 = take(in_s, i0), take(in_s, i1)
    w0, w1 = s0 / (s0 + s1), s1 / (s0 + s1)
    return gsel * epg + i0, gsel * epg + i1, w0, w1


def _xattn_rank(rows, selection, run, rt_ref):
    e0, e1, w0, w1 = selection
    tq = rows.stop - rows.start
    eid = lax.broadcasted_iota(I32, (N_EXPERTS, tq), 0)
    oh0 = eid == e0
    oh1 = eid == e1
    onehot = jnp.where(jnp.logical_or(oh0, oh1), 1.0, 0.0)
    rr = lax.broadcasted_iota(I32, (tq, tq), 0)
    cc = lax.broadcasted_iota(I32, (tq, tq), 1)
    upper = jnp.where(rr < cc, 1.0, 0.0).astype(BF16)
    prefix = _dot(onehot.astype(BF16), upper) + run
    r0 = jnp.sum(jnp.where(oh0, prefix, 0.0), axis=0, keepdims=True).astype(I32)
    r1 = jnp.sum(jnp.where(oh1, prefix, 0.0), axis=0, keepdims=True).astype(I32)

    zero = jnp.zeros((1, tq), I32)
    rt_ref[:, rows] = jnp.concatenate(
        [e0, e1, r0, r1, pltpu.bitcast(w0, I32), pltpu.bitcast(w1, I32), zero, zero], axis=0)
    return run + jnp.sum(onehot, axis=1, keepdims=True)


def _xattn_route(x1, kv, mem_len, wq, wo, g2, b2, rw_hi, rw_lo, rbias, b0, bsz, s_len):
    t_all, d = x1.shape
    tq = TOKEN_TILE
    nj = s_len // tq
    m = mem_len
    row = lambda bb, jj: (bb * nj + jj, 0)
    return pl.pallas_call(
        _xattn_kernel,
        grid=(bsz, nj),
        in_specs=[
            pl.BlockSpec((tq, d), row),
            pl.BlockSpec((m, 2 * d), lambda bb, jj: (b0 + bb, 0)),
            _const_spec((d, d)), _const_spec((d, d)),
            _const_spec((1, d)), _const_spec((1, d)),
            _const_spec((N_EXPERTS, d)), _const_spec((N_EXPERTS, d)),
            _const_spec((N_EXPERTS, 1)),
        ],
        out_specs=[
            pl.BlockSpec((tq, d), row),
            pl.BlockSpec((tq, d // 2), row),
            pl.BlockSpec((8, tq), lambda bb, jj: (0, bb * nj + jj)),
            _const_spec((N_EXPERTS, LANES)),
        ],
        out_shape=[
            jax.ShapeDtypeStruct((t_all, d), F32),
            jax.ShapeDtypeStruct((t_all, d // 2), I32),
            jax.ShapeDtypeStruct((8, t_all), I32),
            jax.ShapeDtypeStruct((N_EXPERTS, LANES), I32),
        ],
        scratch_shapes=[pltpu.VMEM((N_EXPERTS, 1), F32), pltpu.VMEM((d, d), BF16), pltpu.VMEM((d, d), BF16)],
        compiler_params=_tc_params(2),
        name="xattn_route",
    )(x1, kv, wq, wo, g2, b2, rw_hi, rw_lo, rbias)


def _sc_mesh():
    return plsc.VectorSubcoreMesh(core_axis_name="c", subcore_axis_name="s")


def _sc_params():
    return pltpu.CompilerParams(needs_layout_passes=False)


def _worker_id():
    return lax.axis_index("s") * lax.axis_size("c") + lax.axis_index("c")


def _sc_dispatch(xp, dest2d, n_rows):
    t_all, width = xp.shape
    chunk = dest2d.shape[1]
    tok_w = t_all // SC_WORKERS
    nch = tok_w // chunk
    slot1 = t_all // chunk
    assert t_all % (SC_WORKERS * chunk * 2) == 0

    def body(x_hbm, dest_hbm, out_hbm, idx0_v, idx1_v, buf0, buf1, sem_r, sem_w):
        wid = _worker_id()
        base = wid * tok_w
        pltpu.sync_copy(dest_hbm.at[pl.ds(wid * nch, nch)], idx0_v)
        pltpu.sync_copy(dest_hbm.at[pl.ds(slot1 + wid * nch, nch)], idx1_v)

        def read(c, buf, k):
            return pltpu.make_async_copy(x_hbm.at[pl.ds(base + c * chunk, chunk)], buf, sem_r.at[k])

        def scatter(c, buf):
            a = pltpu.make_async_copy(buf, out_hbm.at[idx0_v.at[c]], sem_w.at[0])
            b = pltpu.make_async_copy(buf, out_hbm.at[idx1_v.at[c]], sem_w.at[1])
            a.start()
            b.start()
            a.wait()
            b.wait()

        read(0, buf0, 0).start()

        @pl.loop(0, nch // 2)
        def _(g):
            c = 2 * g
            read(c + 1, buf1, 1).start()
            read(c, buf0, 0).wait()
            scatter(c, buf0)

            @pl.when(c + 2 < nch)
            def _():
                read(c + 2, buf0, 0).start()

            read(c + 1, buf1, 1).wait()
            scatter(c + 1, buf1)

    return pl.kernel(
        body,
        out_type=jax.ShapeDtypeStruct((n_rows, width), xp.dtype),
        mesh=_sc_mesh(),
        scratch_types=[
            pltpu.VMEM((nch, chunk), I32),
            pltpu.VMEM((nch, chunk), I32),
            pltpu.VMEM((chunk, width), xp.dtype),
            pltpu.VMEM((chunk, width), xp.dtype),
            pltpu.SemaphoreType.DMA((2,)),
            pltpu.SemaphoreType.DMA((2,)),
        ],
        compiler_params=_sc_params(),
        name="sc_dispatch",
    )(xp, dest2d)


def _sc_pack_weights(w2d):
    n_rows, width = w2d.shape
    half = width // 2
    rows_w = n_rows // SC_WORKERS
    chunk = (32 * 1024) // width
    lanes = 16
    assert n_rows % (SC_WORKERS * chunk) == 0 and half % lanes == 0

    def body(w_hbm, out_hbm, in_v, out_v):
        base = _worker_id() * rows_w

        def rne_hi16(bits):
            return bits + 0x7FFF + jnp.bitwise_and(lax.shift_right_logical(bits, 16), 1)

        @pl.loop(0, rows_w // chunk)
        def _(c):
            r0 = base + c * chunk
            pltpu.sync_copy(w_hbm.at[pl.ds(r0, chunk)], in_v)

            @pl.loop(0, chunk)
            def _(r):
                @plsc.parallel_loop(0, half, step=lanes, unroll=4)
                def _(j):
                    lo = rne_hi16(plsc.bitcast(in_v[r, pl.ds(j, lanes)], I32))
                    hi = rne_hi16(plsc.bitcast(in_v[r, pl.ds(half + j, lanes)], I32))
                    out_v[r, pl.ds(j, lanes)] = jnp.bitwise_or(
                        lax.shift_right_logical(lo, 16), jnp.bitwise_and(hi, jnp.int32(-65536)))

            pltpu.sync_copy(out_v, out_hbm.at[pl.ds(r0, chunk)])

    return pl.kernel(
        body,
        out_type=jax.ShapeDtypeStruct((n_rows, half), I32),
        mesh=_sc_mesh(),
        scratch_types=[pltpu.VMEM((chunk, width), F32), pltpu.VMEM((chunk, half), I32)],
        compiler_params=_sc_params(),
        name="sc_pack_weights",
    )(w2d)


def _sc_combine(ys, dest2d):
    n_idx_rows, chunk = dest2d.shape
    width = ys.shape[1]
    nch = n_idx_rows // SC_WORKERS
    assert n_idx_rows % (SC_WORKERS * 2) == 0

    def body(y_hbm, dest_hbm, out_hbm, idx_v, buf0, buf1, sem_g):
        wid = _worker_id()
        base = wid * nch * chunk
        pltpu.sync_copy(dest_hbm.at[pl.ds(wid * nch, nch)], idx_v)

        def gather(c, buf, k):
            return pltpu.make_async_copy(y_hbm.at[idx_v.at[c]], buf, sem_g.at[k])

        def write(c, buf):
            pltpu.sync_copy(buf, out_hbm.at[pl.ds(base + c * chunk, chunk)])

        gather(0, buf0, 0).start()

        @pl.loop(0, nch // 2)
        def _(g):
            c = 2 * g
            gather(c + 1, buf1, 1).start()
            gather(c, buf0, 0).wait()
            write(c, buf0)

            @pl.when(c + 2 < nch)
            def _():
                gather(c + 2, buf0, 0).start()

            gather(c + 1, buf1, 1).wait()
            write(c + 1, buf1)

    return pl.kernel(
        body,
        out_type=jax.ShapeDtypeStruct((n_idx_rows * chunk, width), ys.dtype),
        mesh=_sc_mesh(),
        scratch_types=[
            pltpu.VMEM((nch, chunk), I32),
            pltpu.VMEM((chunk, width), ys.dtype),
            pltpu.VMEM((chunk, width), ys.dtype),
            pltpu.SemaphoreType.DMA((2,)),
        ],
        compiler_params=_sc_params(),
        name="sc_combine",
    )(ys, dest2d)


def _pack_bf16_pairs(v):
    half = v.shape[1] // 2
    lo = pltpu.bitcast(v[:, :half].astype(BF16).astype(F32), jnp.uint32) >> 16
    hi = pltpu.bitcast(v[:, half:].astype(BF16).astype(F32), jnp.uint32) & jnp.uint32(0xFFFF0000)
    return pltpu.bitcast(hi | lo, I32)


def _unpack_bf16_pairs(w):
    w = pltpu.bitcast(w, jnp.uint32)
    lo = pltpu.bitcast(w << 16, F32)
    hi = pltpu.bitcast(w & jnp.uint32(0xFFFF0000), F32)
    return jnp.concatenate([lo, hi], axis=1)


def _ffn_kernel(be_ref, nv_ref, nu_ref, nxt_ref, xs_ref, wg_hbm, wu_hbm, wd_hbm, o_ref,
                wg_f, wu_f, wd_f, wg_s, wu_s, wd_s, sem):
    bm = EXPERT_ROWS

    def fetch(e):
        return (pltpu.make_async_copy(wg_hbm.at[e], wg_f, sem.at[0]),
                pltpu.make_async_copy(wu_hbm.at[e], wu_f, sem.at[1]),
                pltpu.make_async_copy(wd_hbm.at[e], wd_f, sem.at[2]))

    for sub in range(FFN_GROUP):
        i = pl.program_id(0) * FFN_GROUP + sub
        rows = slice(sub * bm, (sub + 1) * bm)

        @pl.when(nxt_ref[i] >= 0)
        def _():
            @pl.when(i == 0)
            def _():
                for cp in fetch(be_ref[0]):
                    cp.start()

            for cp in fetch(be_ref[i]):
                cp.wait()
            wg_s[...] = _unpack_bf16_pairs(wg_f[...]).astype(BF16)
            wu_s[...] = _unpack_bf16_pairs(wu_f[...]).astype(BF16)
            wd_s[...] = _unpack_bf16_pairs(wd_f[...]).astype(BF16)

            @pl.when(nxt_ref[i] < N_EXPERTS)
            def _():
                for cp in fetch(nxt_ref[i]):
                    cp.start()

        @pl.when(i < nu_ref[0])
        def _():
            live = lax.broadcasted_iota(I32, (bm, xs_ref.shape[1]), 0) < nv_ref[i]
            xb = _unpack_bf16_pairs(jnp.where(live, xs_ref[rows, :], 0)).astype(BF16)
            act = jax.nn.silu(_dot(xb, wg_s[...])) * _dot(xb, wu_s[...])
            o_ref[rows, :] = _pack_bf16_pairs(_dot(act.astype(BF16), wd_s[...]))


def _expert_ffn(xs, blk_expert, blk_valid, n_used, blk_next, w_gate, w_up, w_down):
    n_rows, half = xs.shape
    d = 2 * half
    de = 2 * w_gate.shape[2]
    win = FFN_GROUP * EXPERT_ROWS
    assert n_rows % win == 0
    rows = lambda s, be, nv, nu, nx: (jnp.minimum(s, (nu[0] - 1) // FFN_GROUP), 0)
    hbm = pl.BlockSpec(memory_space=pl.ANY)
    return pl.pallas_call(
        _ffn_kernel,
        grid_spec=pltpu.PrefetchScalarGridSpec(
            num_scalar_prefetch=4,
            grid=(n_rows // win,),
            in_specs=[pl.BlockSpec((win, half), rows), hbm, hbm, hbm],
            out_specs=pl.BlockSpec((win, half), rows),
            scratch_shapes=[
                pltpu.VMEM((d, de // 2), I32), pltpu.VMEM((d, de // 2), I32), pltpu.VMEM((de, d // 2), I32),
                pltpu.VMEM((d, de), BF16), pltpu.VMEM((d, de), BF16), pltpu.VMEM((de, d), BF16),
                pltpu.SemaphoreType.DMA((3,)),
            ],
        ),
        out_shape=jax.ShapeDtypeStruct((n_rows, half), I32),
        compiler_params=_tc_params(1),
        name="expert_ffn",
    )(blk_expert, blk_valid, n_used, blk_next, xs, w_gate, w_up, w_down)


def _combine_kernel(x_ref, y0_ref, y1_ref, rt_ref, g_ref, b_ref, o_ref):
    o_ref[...] = _expert_combine_ln(x_ref[...], y0_ref[...], y1_ref[...], rt_ref[...], g_ref[...], b_ref[...])


def _combine_kernel_into(x_ref, y0_ref, y1_ref, rt_ref, g_ref, b_ref, full_ref, o_ref):
    del full_ref
    _combine_kernel(x_ref, y0_ref, y1_ref, rt_ref, g_ref, b_ref, o_ref)


def _combine_ln(x2, y01, rt, g3, b3, into=None, row0=0, full_rows=None):
    t_all, d = x2.shape
    tq = TOKEN_TILE
    nt = t_all // tq
    blk0 = row0 // tq if full_rows else 0
    in_specs = [
        pl.BlockSpec((tq, d), lambda i: (i, 0)),
        pl.BlockSpec((tq, d // 2), lambda i: (i, 0)),
        pl.BlockSpec((tq, d // 2), lambda i: (i + nt, 0)),
        pl.BlockSpec((8, tq), lambda i: (0, i)),
        _const_spec((1, d)), _const_spec((1, d)),
    ]
    args = [x2, y01, y01, rt, g3, b3]
    body, aliases = _combine_kernel, {}
    if into is not None:
        in_specs.append(pl.BlockSpec(memory_space=pl.ANY))
        args.append(into)
        body, aliases = _combine_kernel_into, {len(args) - 1: 0}
    return pl.pallas_call(
        body,
        grid=(nt,),
        in_specs=in_specs,
        out_specs=pl.BlockSpec((tq, d), lambda i: (i + blk0, 0)),
        out_shape=jax.ShapeDtypeStruct((full_rows or t_all, d), F32),
        input_output_aliases=aliases,
        compiler_params=_tc_params(1),
        name="combine_ln",
    )(*args)


def _routing_tables(rt, counts):
    bm = EXPERT_ROWS
    t_all = rt.shape[1]
    n_rows = 2 * t_all + N_EXPERTS * bm
    cnt = counts[:, 0]
    padded = (cnt + bm - 1) // bm * bm
    ends = jnp.cumsum(padded)
    offs = ends - padded
    experts = rt[0:2]
    off_tok = jnp.sum(jnp.where(experts[None] == jnp.arange(N_EXPERTS, dtype=I32)[:, None, None],
                                offs[:, None, None], 0), axis=0)
    dest2d = (off_tok + rt[2:4]).reshape(-1, SC_CHUNK).astype(I32)
    blk_start = jnp.arange(n_rows // bm, dtype=I32) * bm
    blk_expert = jnp.minimum(jnp.sum(blk_start[:, None] >= ends[None, :], axis=1), N_EXPERTS - 1).astype(I32)
    live_end = jnp.sum(jnp.where(blk_expert[:, None] == jnp.arange(N_EXPERTS, dtype=I32)[None, :],
                                 (offs + cnt)[None, :], 0), axis=1)
    blk_valid = jnp.clip(live_end - blk_start, 0, bm).astype(I32)
    n_used = (ends[-1:] // bm).astype(I32)
    eid = jnp.arange(N_EXPERTS, dtype=I32)
    later_present = jnp.logical_and(eid[None, :] > eid[:, None], (cnt > 0)[None, :])
    next_present = jnp.min(jnp.where(later_present, eid[None, :], N_EXPERTS), axis=1)
    next_of_blk = jnp.sum(jnp.where(blk_expert[:, None] == eid[None, :], next_present[None, :], 0), axis=1)
    prev_expert = jnp.concatenate([jnp.full((1,), -1, I32), blk_expert[:-1]])
    is_first = jnp.logical_and(blk_start < ends[-1], blk_expert != prev_expert)
    blk_next = jnp.where(is_first, next_of_blk, -1).astype(I32)
    return dest2d, blk_expert, blk_valid, n_used, blk_next, n_rows


def _layer_tail(x1, kv, mem_len, p, router, b0, bsz, s_len):
    rw_hi, rw_lo, rbias = router
    x2, xp, rt, counts = _xattn_route(x1, kv, mem_len, p["xq"], p["xo"], p["ln2_g"], p["ln2_b"],
                                      rw_hi, rw_lo, rbias, b0, bsz, s_len)
    dest2d, blk_expert, blk_valid, n_used, blk_next, n_rows = _routing_tables(rt, counts)
    xs = _sc_dispatch(xp, dest2d, n_rows)
    if "packed" not in p:
        p["packed"] = tuple(_sc_pack_weights(w.reshape(-1, w.shape[2])).reshape(w.shape[0], w.shape[1], -1)
                            for w in (p["e_gate"], p["e_up"], p["e_down"]))
    ys = _expert_ffn(xs, blk_expert, blk_valid, n_used, blk_next, *p["packed"])
    y01 = _sc_combine(ys, dest2d)
    return x2, y01, rt


def _row(v):
    return v.reshape(1, -1).astype(F32)


def _common_params(xq, xkv, xo, ln2_g, ln2_b, e_gate, e_up, e_down, ln3_g, ln3_b):
    return dict(xq=xq, xkv=xkv, xo=xo, ln2_g=_row(ln2_g), ln2_b=_row(ln2_b),
                e_gate=e_gate, e_up=e_up, e_down=e_down, ln3_g=_row(ln3_g), ln3_b=_row(ln3_b))


def kernel(x, mem, positions, router_w, router_bias, l0_w_in, l0_sinks, l0_sgu_ln_g, l0_sgu_ln_b, l0_sgu_w, l0_sgu_b, l0_w_out, l0_ln1_g, l0_ln1_b, l0_xq, l0_xkv, l0_xo, l0_ln2_g, l0_ln2_b, l0_e_gate, l0_e_up, l0_e_down, l0_ln3_g, l0_ln3_b, l1_w_in, l1_pool_w, l1_pool_scale, l1_w_out, l1_ln1_g, l1_ln1_b, l1_xq, l1_xkv, l1_xo, l1_ln2_g, l1_ln2_b, l1_e_gate, l1_e_up, l1_e_down, l1_ln3_g, l1_ln3_b):
    bsz, s_len, d = x.shape
    assert s_len % TOKEN_TILE == 0 and TOKEN_TILE == 2 * SLAB_ROWS and SLAB_ROWS % BLOCK == 0
    xt = x.reshape(bsz * s_len, d)
    mem2d = mem.reshape(-1, d)

    rw_t = router_w.T.astype(F32)
    rw_hi = rw_t.astype(BF16)
    rw_lo = (rw_t - rw_hi.astype(F32)).astype(BF16)
    router = (rw_hi, rw_lo, router_bias.reshape(-1, 1).astype(F32))

    half = ROPE_DIM // 2
    inv_freq = (ROPE_THETA ** (-(jnp.arange(half, dtype=F32) * 2.0 / ROPE_DIM))).reshape(half, 1)
    etab_np = np.zeros((LANES, 3 * LANES), np.float32)
    cbase_np = np.ones((1, LANES), np.float32)
    for ln in range(LANES):
        dd = ln % HEAD_DIM
        if dd < ROPE_DIM:
            cbase_np[0, ln] = 0.0
            etab_np[[dd % half, half + dd % half], ln] = 1.0
            if dd >= half:
                etab_np[[2 * half + dd - half, 3 * half + dd - half], LANES + ln] = 1.0
            else:
                etab_np[[2 * half + dd, 3 * half + dd], 2 * LANES + ln] = -1.0
    etab = jnp.asarray(etab_np, BF16)
    cbase = jnp.asarray(cbase_np)
    pos_row = positions.reshape(1, -1).astype(I32)
    grp = jnp.arange(PIECE_COLS) // B_GROUP_DIM
    gsum = (grp[:, None] == grp[None, :]).astype(BF16)
    bs_full = jnp.repeat(l0_sgu_b.T.astype(F32), B_GROUP_DIM, axis=1)

    p0 = _common_params(l0_xq, l0_xkv, l0_xo, l0_ln2_g, l0_ln2_b, l0_e_gate, l0_e_up, l0_e_down,
                        l0_ln3_g, l0_ln3_b)
    p1 = _common_params(l1_xq, l1_xkv, l1_xo, l1_ln2_g, l1_ln2_b, l1_e_gate, l1_e_up, l1_e_down,
                        l1_ln3_g, l1_ln3_b)
    kv0 = _kv_proj(mem2d, p0["xkv"], bsz)
    kv1 = _kv_proj(mem2d, p1["xkv"], bsz)
    mem_len = mem.shape[1]

    n_split = BATCH_SPLIT if bsz % BATCH_SPLIT == 0 else 1
    nb = bsz // n_split
    out = None
    for part in range(n_split):
        b0 = part * nb
        x1 = _mixer0(xt, pos_row, l0_sinks.astype(F32), l0_w_in, inv_freq, etab, cbase, gsum,
                     _row(l0_sgu_ln_g), _row(l0_sgu_ln_b), l0_sgu_w.astype(F32), bs_full,
                     l0_w_out, _row(l0_ln1_g), _row(l0_ln1_b), b0, nb, s_len)
        x2, y01, rt = _layer_tail(x1, kv0, mem_len, p0, router, b0, nb, s_len)
        x1 = _mixer1(x2, y01, rt, p0["ln3_g"], p0["ln3_b"], l1_w_in, l1_pool_w, _row(l1_pool_scale),
                     l1_w_out, _row(l1_ln1_g), _row(l1_ln1_b), nb, s_len)
        x2, y01, rt = _layer_tail(x1, kv1, mem_len, p1, router, b0, nb, s_len)
        out = _combine_ln(x2, y01, rt, p1["ln3_g"], p1["ln3_b"], into=out, row0=b0 * s_len,
                          full_rows=bsz * s_len)
    return out.reshape(bsz, s_len, d)
```

```python
import functools

import numpy as np
import jax
import jax.numpy as jnp
from jax import lax
from jax.experimental import pallas as pl
from jax.experimental.pallas import tpu as pltpu
from jax.experimental.pallas import tpu_sc as plsc

F32 = jnp.float32
BF16 = jnp.bfloat16
I32 = jnp.int32

DEPTH = 2
ALPHA = (2.0 * DEPTH) ** 0.25
LN_EPS = 1e-5

HEAD_DIM = 64
A_Q_HEADS = 8
A_KV_HEADS = 2
A_GROUP = A_Q_HEADS // A_KV_HEADS
BLOCK = 128
ROPE_THETA = 500000.0
ROPE_DIM = HEAD_DIM // 4
A_WIDTH = A_Q_HEADS * HEAD_DIM
KV_WIDTH = A_KV_HEADS * HEAD_DIM
B_GROUPS = 8
B_GROUP_DIM = 64
B_WIDTH = B_GROUPS * B_GROUP_DIM
POOL_WINDOWS = (2, 4, 8, 16)
POOL_HALO = 16
X_HEADS = 4
N_EXPERTS = 16
N_EXPERT_GROUPS = 4
EXPERTS_PER_GROUP = 4

LANES = 128
TOKEN_TILE = 1024
SLAB_ROWS = 512
PIECE_COLS = 256
EXPERT_ROWS = 512
FFN_GROUP = 4
BATCH_SPLIT = 2
SC_WORKERS = 32
SC_CHUNK = 64
VMEM_LIMIT = 56 * 1024 * 1024
NEG_BIG = -1e30


def _layer_norm(z, g, b):
    mu = jnp.mean(z, axis=-1, keepdims=True)
    d = z - mu
    var = jnp.mean(d * d, axis=-1, keepdims=True)
    return d * lax.rsqrt(var + LN_EPS) * g + b


def _dot(a, b):
    return jnp.dot(a, b, preferred_element_type=F32)


def _dot_nt(a, b):
    return lax.dot_general(a, b, (((1,), (1,)), ((), ())), preferred_element_type=F32)


def _split_bf16(v):
    hi = v.astype(BF16)
    lo = (v - hi.astype(F32)).astype(BF16)
    return hi, lo


def _tc_params(n_axes):
    return pltpu.CompilerParams(dimension_semantics=("arbitrary",) * n_axes,
                                vmem_limit_bytes=VMEM_LIMIT)


def _const_spec(shape):
    nd = len(shape)
    return pl.BlockSpec(shape, lambda *_: (0,) * nd, pipeline_mode=pl.Buffered(1))


def _mixer0_kernel(sinks_ref, x_ref, pos_ref, win_ref, invf_ref, etab_ref, cbase_ref, gsum_ref,
                   lng_ref, lnb_ref, ws_ref, bs_ref, wout_ref, g1_ref, b1_ref,
                   o_ref, q_s, kv_s, u_s, vn_s, mix_s, wt_s, win_s, wout_s):
    b = pl.program_id(0)
    j = pl.program_id(1)
    tq = x_ref.shape[0]
    kvw = kv_s.shape[1]

    @pl.when(jnp.logical_and(b == 0, j == 0))
    def _():
        win_s[...] = win_ref[...].astype(BF16)
        wout_s[...] = wout_ref[...].astype(BF16)
        r = lax.broadcasted_iota(I32, (BLOCK, BLOCK), 0)
        c = lax.broadcasted_iota(I32, (BLOCK, BLOCK), 1)
        for g in range(B_GROUPS):
            wt_s[g] = jnp.where(c <= r, ws_ref[g], 0.0).astype(BF16)

    @pl.when(j == 0)
    def _():
        kv_s[0:BLOCK, :] = jnp.zeros((BLOCK, kvw), BF16)

    c1 = A_WIDTH
    c2 = c1 + KV_WIDTH
    c3 = c2 + KV_WIDTH
    c4 = c3 + B_WIDTH

    def rotary_tables(rows):
        n = rows.stop - rows.start
        ang = invf_ref[...] * pos_ref[:, rows].astype(F32)
        c8 = jnp.cos(ang)
        s8 = jnp.sin(ang)
        c8h = c8.astype(BF16).astype(F32)
        s8h = s8.astype(BF16).astype(F32)
        pad = jnp.zeros((LANES - 4 * c8.shape[0], n), F32)
        stack = jnp.concatenate([c8h, c8 - c8h, s8h, s8 - s8h, pad], axis=0)
        tabs = _dot(stack.T.astype(BF16), etab_ref[...])
        return tabs[:, 0:LANES] + cbase_ref[...], tabs[:, LANES:2 * LANES], tabs[:, 2 * LANES:]

    pw = PIECE_COLS
    n_pieces = win_s.shape[1] // pw
    assert (c1 // pw, c2 // pw, c3 // pw, c4 // pw) == (2, 2, 3, 5) and c3 % pw == 0 and n_pieces == 7

    def project(xb, k):
        return _dot(xb, win_s[:, k * pw:(k + 1) * pw])

    def prepare(rows, hk, tables, k):
        n = rows.stop - rows.start
        cs, sa, sb = tables

        def rope(t):
            return t * cs + pltpu.roll(t, ROPE_DIM // 2, 1) * sa + pltpu.roll(t, LANES - ROPE_DIM // 2, 1) * sb

        if k < 2:
            for c in range(pw // LANES):
                t = hk[:, c * LANES:(c + 1) * LANES] * (HEAD_DIM ** -0.5)
                col = k * (pw // LANES) + c
                q_s[rows, col * LANES:(col + 1) * LANES] = rope(t).astype(BF16)
        elif k == 2:
            low = lax.broadcasted_iota(I32, (n, LANES), 1) < HEAD_DIM
            kr = rope(hk[:, 0:KV_WIDTH])
            kx = pltpu.roll(kr, HEAD_DIM, 1)
            vr = hk[:, KV_WIDTH:]
            vx = pltpu.roll(vr, HEAD_DIM, 1)
            kv_cols = [jnp.where(low, kr, kx), jnp.where(low, kx, kr),
                       jnp.where(low, vr, 0.0), jnp.where(low, 0.0, vx),
                       jnp.where(low, vx, 0.0), jnp.where(low, 0.0, vr)]
            for c, col in enumerate(kv_cols):
                kv_s[BLOCK + rows.start:BLOCK + rows.stop, c * LANES:(c + 1) * LANES] = col.astype(BF16)
        elif k < 5:
            lo = (k - 3) * pw
            u_s[rows, lo:lo + pw] = jax.nn.gelu(hk)
        else:
            lo = (k - 5) * pw
            v = jax.nn.gelu(hk)
            gsum = gsum_ref[...]
            mean = _dot(v.astype(BF16), gsum) * (1.0 / B_GROUP_DIM)
            d = v - mean
            var = _dot((d * d).astype(BF16), gsum) * (1.0 / B_GROUP_DIM)
            vn_s[rows, lo:lo + pw] = (d * lax.rsqrt(var + LN_EPS) * lng_ref[:, lo:lo + pw]
                                      + lnb_ref[:, lo:lo + pw]).astype(BF16)

    qi = lax.broadcasted_iota(I32, (BLOCK, 2 * BLOCK), 0)
    kj = lax.broadcasted_iota(I32, (BLOCK, 2 * BLOCK), 1)
    rel = qi + BLOCK - kj
    band = jnp.logical_and(rel >= 0, rel < BLOCK)
    low_q = lax.broadcasted_iota(I32, (BLOCK, LANES), 1) < HEAD_DIM
    low_k = lax.broadcasted_iota(I32, (2 * BLOCK, LANES), 1) < HEAD_DIM
    ones_lo = jnp.where(low_k, 1.0, 0.0).astype(BF16)
    ones_hi = jnp.where(low_k, 0.0, 1.0).astype(BF16)
    zero_q = jnp.zeros((BLOCK, LANES), BF16)

    def block_body(n):
        r0 = n * BLOCK
        kv = kv_s[pl.ds(r0, 2 * BLOCK), :]
        qb = q_s[pl.ds(r0, BLOCK), :]
        valid = jnp.logical_and(band, kj >= jnp.where(j == 0, BLOCK, 0)) if n == 0 else band
        cols_per_kv = A_GROUP // 2
        scores = {}
        for hk in range(A_KV_HEADS):
            cols = range(hk * cols_per_kv, (hk + 1) * cols_per_kv)
            pieces = []
            for c in cols:
                qp = qb[:, c * LANES:(c + 1) * LANES]
                pieces += [jnp.where(low_q, qp, zero_q), jnp.where(low_q, zero_q, qp)]
            sc = _dot_nt(jnp.concatenate(pieces, axis=0), kv[:, hk * LANES:(hk + 1) * LANES])
            for i, c in enumerate(cols):
                for half in range(2):
                    r = (2 * i + half) * BLOCK
                    scores[c, half] = sc[r:r + BLOCK, :]
        vnb = vn_s[pl.ds(r0, BLOCK), :]
        parts = []
        for c in range(B_WIDTH // LANES):
            vp = vnb[:, c * LANES:(c + 1) * LANES]
            parts.append(_dot(wt_s[2 * c], jnp.where(low_q, vp, zero_q))
                         + _dot(wt_s[2 * c + 1], jnp.where(low_q, zero_q, vp)))
        probs, esink = {}, {}
        for (c, half), sc in scores.items():
            s = jnp.where(valid, sc, NEG_BIG)
            sink = sinks_ref[2 * c + half]
            m = jnp.maximum(jnp.max(s, axis=-1, keepdims=True), sink)
            probs[c, half] = jnp.exp(s - m).astype(BF16)
            esink[c, half] = jnp.exp(sink - m)
        res = {}
        for hk in range(A_KV_HEADS):
            cols = range(hk * cols_per_kv, (hk + 1) * cols_per_kv)
            for half in range(2):
                vcol = kv[:, (2 + 2 * hk + half) * LANES:(3 + 2 * hk + half) * LANES]
                vm = jnp.concatenate([vcol, ones_lo if half == 0 else ones_hi], axis=1)
                pv = _dot(jnp.concatenate([probs[c, half] for c in cols], axis=0), vm)
                for i, c in enumerate(cols):
                    part = pv[i * BLOCK:(i + 1) * BLOCK, :]
                    res[c] = part if half == 0 else res[c] + part
        for c in range(A_WIDTH // LANES):
            den = res[c][:, LANES:] + jnp.where(low_q, esink[c, 0], esink[c, 1])
            mix_s[pl.ds(r0, BLOCK), c * LANES:(c + 1) * LANES] = (res[c][:, :LANES] / den).astype(BF16)
        mixed = jnp.concatenate(parts, axis=1) + bs_ref[...]
        mix_s[pl.ds(r0, BLOCK), A_WIDTH:] = (u_s[pl.ds(r0, BLOCK), :] * mixed).astype(BF16)

    def out_cols(rows, c):
        return _dot(mix_s[rows, :], wout_s[:, c * pw:(c + 1) * pw])

    def finish(rows, z_cols, i):
        r = slice(rows.start + i * BLOCK, rows.start + (i + 1) * BLOCK)
        z = ALPHA * x_ref[r, :] + jnp.concatenate([zc[i * BLOCK:(i + 1) * BLOCK, :] for zc in z_cols], axis=1)
        o_ref[r, :] = _layer_norm(z, g1_ref[...], b1_ref[...])

    assert tq == 2 * SLAB_ROWS and SLAB_ROWS == 4 * BLOCK and wout_s.shape[1] == 4 * pw
    sa, sb = slice(0, SLAB_ROWS), slice(SLAB_ROWS, tq)
    tab_a, tab_b = rotary_tables(sa), rotary_tables(sb)
    xa = x_ref[sa, :].astype(BF16)
    ha = [project(xa, k) for k in range(n_pieces)]
    xb = x_ref[sb, :].astype(BF16)
    hb = []
    for k in range(n_pieces):
        hb.append(project(xb, k))
        prepare(sa, ha[k], tab_a, k)
    pieces_b = iter(range(n_pieces))
    for n in range(4):
        block_body(n)
        for k in [next(pieces_b) for _ in range(2 if n < 3 else 1)]:
            prepare(sb, hb[k], tab_b, k)
    za = []
    for n in range(4):
        block_body(4 + n)
        za.append(out_cols(sa, n))
    kv_s[0:BLOCK, :] = kv_s[tq:tq + BLOCK, :]
    zb = []
    for n in range(4):
        zb.append(out_cols(sb, n))
        finish(sa, za, n)
    for n in range(4):
        finish(sb, zb, n)


def _mixer0(x, pos_row, sinks, w_in, invf, etab, cbase, gsum, lng, lnb, w_s, bs_full, w_out, g1, b1,
            b0, bsz, s_len):
    d = x.shape[1]
    t_all = bsz * s_len
    tq = TOKEN_TILE
    nj = s_len // tq
    row = lambda bb, jj: (bb * nj + jj, 0)
    in_w = w_in.shape[1]
    return pl.pallas_call(
        _mixer0_kernel,
        grid=(bsz, nj),
        in_specs=[
            pl.BlockSpec(memory_space=pltpu.SMEM),
            pl.BlockSpec((tq, d), lambda bb, jj: ((b0 + bb) * nj + jj, 0)),
            pl.BlockSpec((1, tq), lambda bb, jj: (0, (b0 + bb) * nj + jj)),
            _const_spec((d, in_w)),
            _const_spec((ROPE_DIM // 2, 1)), _const_spec((LANES, 3 * LANES)), _const_spec((1, LANES)),
            _const_spec((PIECE_COLS, PIECE_COLS)),
            _const_spec((1, B_WIDTH)), _const_spec((1, B_WIDTH)),
            _const_spec((B_GROUPS, BLOCK, BLOCK)),
            _const_spec((BLOCK, B_WIDTH)),
            _const_spec((A_WIDTH + B_WIDTH, d)),
            _const_spec((1, d)), _const_spec((1, d)),
        ],
        out_specs=pl.BlockSpec((tq, d), row),
        out_shape=jax.ShapeDtypeStruct((t_all, d), F32),
        scratch_shapes=[
            pltpu.VMEM((tq, A_WIDTH), BF16),
            pltpu.VMEM((tq + BLOCK, 6 * LANES), BF16),
            pltpu.VMEM((tq, B_WIDTH), F32),
            pltpu.VMEM((tq, B_WIDTH), BF16),
            pltpu.VMEM((tq, A_WIDTH + B_WIDTH), BF16),
            pltpu.VMEM((B_GROUPS, BLOCK, BLOCK), BF16),
            pltpu.VMEM((d, in_w), BF16),
            pltpu.VMEM((A_WIDTH + B_WIDTH, d), BF16),
        ],
        compiler_params=_tc_params(2),
        name="mixer0",
    )(sinks, x, pos_row, w_in, invf, etab, cbase, gsum, lng, lnb, w_s, bs_full, w_out, g1, b1)


def _expert_combine_ln(x2, y0_packed, y1_packed, rt, g, b):
    wt = pltpu.bitcast(rt, F32).T
    y = wt[:, 4:5] * _unpack_bf16_pairs(y0_packed) + wt[:, 5:6] * _unpack_bf16_pairs(y1_packed)
    return _layer_norm(ALPHA * x2 + y, g, b)


def _mixer1_kernel(x2_ref, y0_ref, y1_ref, rt_ref, g3_ref, b3_ref,
                   win_ref, pw_ref, ps_ref, wout_ref, g1_ref, b1_ref, o_ref,
                   h_s, mp_s, win_s, pw_s, wout_s):
    j = pl.program_id(1)
    tq = x2_ref.shape[0]
    gw = x2_ref.shape[1] // len(POOL_WINDOWS)
    slabs = [slice(c * SLAB_ROWS, (c + 1) * SLAB_ROWS) for c in range(tq // SLAB_ROWS)]

    @pl.when(jnp.logical_and(pl.program_id(0) == 0, j == 0))
    def _():
        win_s[...] = win_ref[...].astype(BF16)
        pw_s[...] = pw_ref[...].astype(BF16)
        wout_s[...] = wout_ref[...].astype(BF16)

    @pl.when(j == 0)
    def _():
        h_s[0:POOL_HALO, :] = jnp.zeros((POOL_HALO, h_s.shape[1]), F32)

    ng = len(POOL_WINDOWS)
    quarter = SLAB_ROWS // ng

    def sub(rows, i):
        return slice(rows.start + i * quarter, rows.start + (i + 1) * quarter)

    def load_in(rows, i):
        r = sub(rows, i)
        return _expert_combine_ln(x2_ref[r, :], y0_ref[r, :], y1_ref[r, :], rt_ref[:, r],
                                  g3_ref[...], b3_ref[...])

    def project(rows, xb, g):
        lo, hi = g * gw, (g + 1) * gw
        h_s[POOL_HALO + rows.start:POOL_HALO + rows.stop, lo:hi] = _dot(xb, win_s[:, lo:hi])

    def pool(rows, g):
        win = POOL_WINDOWS[g]
        lo, hi = g * gw, (g + 1) * gw
        n = rows.stop - rows.start
        t_pos = j * tq + rows.start + lax.broadcasted_iota(I32, (n, 1), 0)
        ext = h_s[rows.start:rows.stop + POOL_HALO, lo:hi]
        acc = ext
        shift = 1
        while shift < win:
            acc = acc + pltpu.roll(acc, shift, 0)
            shift *= 2
        count = jnp.minimum(t_pos + 1, win).astype(F32)
        pooled = acc[POOL_HALO:, :] / count - ext[POOL_HALO:, :]
        mapped = _dot(pooled.astype(BF16), pw_s[g])
        mp_s[rows, lo:hi] = (mapped * ps_ref[:, lo:hi]).astype(BF16)

    def out_cols(rows, g):
        return _dot(mp_s[rows, :], wout_s[:, g * gw:(g + 1) * gw])

    def finish(rows, x_parts, z_cols, i):
        lo, hi = i * quarter, (i + 1) * quarter
        z = ALPHA * x_parts[i] + jnp.concatenate([zc[lo:hi, :] for zc in z_cols], axis=1)
        o_ref[sub(rows, i), :] = _layer_norm(z, g1_ref[...], b1_ref[...])

    assert len(slabs) == 2
    sa, sb = slabs
    xa = [load_in(sa, i) for i in range(ng)]
    xa_b = jnp.concatenate(xa, axis=0).astype(BF16)
    xb = []
    for g in range(ng):
        project(sa, xa_b, g)
        xb.append(load_in(sb, g))
    xb_b = jnp.concatenate(xb, axis=0).astype(BF16)
    for g in range(ng):
        project(sb, xb_b, g)
        pool(sa, g)
    za = []
    for g in range(ng):
        za.append(out_cols(sa, g))
        pool(sb, g)
    zb = []
    for g in range(ng):
        zb.append(out_cols(sb, g))
        finish(sa, xa, za, g)
    for g in range(ng):
        finish(sb, xb, zb, g)
    h_s[0:POOL_HALO, :] = h_s[tq:tq + POOL_HALO, :]


def _mixer1(x2, y01, rt, g3, b3, w_in, pool_w, pool_scale, w_out, g1, b1, bsz, s_len):
    t_all, d = x2.shape
    tq = TOKEN_TILE
    nj = s_len // tq
    nt = bsz * nj
    row = lambda bb, jj: (bb * nj + jj, 0)
    ng = len(POOL_WINDOWS)
    return pl.pallas_call(
        _mixer1_kernel,
        grid=(bsz, nj),
        in_specs=[
            pl.BlockSpec((tq, d), row),
            pl.BlockSpec((tq, d // 2), row),
            pl.BlockSpec((tq, d // 2), lambda bb, jj: (nt + bb * nj + jj, 0)),
            pl.BlockSpec((8, tq), lambda bb, jj: (0, bb * nj + jj)),
            _const_spec((1, d)), _const_spec((1, d)),
            _const_spec((d, d)),
            _const_spec((ng, d // ng, d // ng)),
            _const_spec((1, d)),
            _const_spec((d, d)),
            _const_spec((1, d)), _const_spec((1, d)),
        ],
        out_specs=pl.BlockSpec((tq, d), row),
        out_shape=jax.ShapeDtypeStruct((t_all, d), F32),
        scratch_shapes=[pltpu.VMEM((tq + POOL_HALO, d), F32), pltpu.VMEM((tq, d), BF16),
                        pltpu.VMEM((d, d), BF16), pltpu.VMEM((ng, d // ng, d // ng), BF16),
                        pltpu.VMEM((d, d), BF16)],
        compiler_params=_tc_params(2),
        name="mixer1",
    )(x2, y01, y01, rt, g3, b3, w_in, pool_w, pool_scale, w_out, g1, b1)


def _kv_kernel(mem_ref, w_ref, o_ref, w_s):
    @pl.when(pl.program_id(0) == 0)
    def _():
        w_s[...] = w_ref[...].astype(BF16)

    o_ref[...] = _dot(mem_ref[...].astype(BF16), w_s[...]).astype(BF16)


def _kv_proj(mem2d, wkv, bsz):
    rows, d = mem2d.shape
    m = rows // bsz
    return pl.pallas_call(
        _kv_kernel,
        grid=(bsz,),
        in_specs=[pl.BlockSpec((m, d), lambda i: (i, 0)), _const_spec(wkv.shape)],
        out_specs=pl.BlockSpec((m, wkv.shape[1]), lambda i: (i, 0)),
        out_shape=jax.ShapeDtypeStruct((rows, wkv.shape[1]), BF16),
        scratch_shapes=[pltpu.VMEM(wkv.shape, BF16)],
        compiler_params=_tc_params(1),
        name="kv_proj",
    )(mem2d, wkv)


def _top2_of4(v):
    hi01, lo01 = jnp.maximum(v[0], v[1]), jnp.minimum(v[0], v[1])
    hi23, lo23 = jnp.maximum(v[2], v[3]), jnp.minimum(v[2], v[3])
    return jnp.maximum(hi01, hi23) + jnp.maximum(jnp.minimum(hi01, hi23), jnp.maximum(lo01, lo23))


def _argmax_first(vals):
    best, idx = vals[0], jnp.zeros(vals[0].shape, I32)
    for i in range(1, len(vals)):
        better = vals[i] > best
        best = jnp.where(better, vals[i], best)
        idx = jnp.where(better, i, idx)
    return best, idx


def _xattn_kernel(x_ref, kv_ref, wq_ref, wo_ref, g2_ref, b2_ref, rwh_ref, rwl_ref, rb_ref,
                  x2_ref, xp_ref, rt_ref, cnt_ref, run_s, wq_s, wo_s):
    first = jnp.logical_and(pl.program_id(0) == 0, pl.program_id(1) == 0)
    tq, d = x_ref.shape

    @pl.when(first)
    def _():
        run_s[...] = jnp.zeros(run_s.shape, F32)
        wq_s[...] = wq_ref[...].astype(BF16)
        wo_s[...] = wo_ref[...].astype(BF16)

    slabs = [slice(c * SLAB_ROWS, (c + 1) * SLAB_ROWS) for c in range(tq // SLAB_ROWS)]
    run = run_s[...]
    zs = [_xattn_attend(*_xattn_query(rows, x_ref, wq_s), kv_ref, wo_s) for rows in slabs]
    picks = []
    for rows, z in zip(slabs, zs):
        logits = _xattn_norm_logits(rows, z, g2_ref, b2_ref, rwh_ref, rwl_ref, x2_ref, xp_ref)
        picks.append(_xattn_select(logits, rb_ref))
    for rows, pick in zip(slabs, picks):
        run = _xattn_rank(rows, pick, run, rt_ref)
    run_s[...] = run
    cnt_ref[...] = jnp.broadcast_to(run, cnt_ref.shape).astype(I32)


def _xattn_query(rows, x_ref, wq_s):
    hd = x_ref.shape[1] // X_HEADS
    x = x_ref[rows, :]
    return x, _dot(x.astype(BF16), wq_s[...]) * (hd ** -0.5)


def _xattn_attend(x, q, kv_ref, wo_s):
    d = x.shape[1]
    hd = d // X_HEADS
    outs = []
    for hx in range(X_HEADS):
        qh = q[:, hx * hd:(hx + 1) * hd].astype(BF16)
        kh = kv_ref[:, hx * hd:(hx + 1) * hd]
        vh = kv_ref[:, d + hx * hd:d + (hx + 1) * hd]
        s = _dot_nt(qh, kh)
        p = jnp.exp(s - jnp.max(s, axis=-1, keepdims=True))
        o = _dot(p.astype(BF16), vh) / jnp.sum(p, axis=-1, keepdims=True)
        outs.append(o.astype(BF16))
    return ALPHA * x + _dot(jnp.concatenate(outs, axis=1), wo_s[...])


def _xattn_norm_logits(rows, z, g2_ref, b2_ref, rwh_ref, rwl_ref, x2_ref, xp_ref):
    x2 = _layer_norm(z, g2_ref[...], b2_ref[...])
    x2_ref[rows, :] = x2

    xp_ref[rows, :] = _pack_bf16_pairs(x2)

    xh, xl = _split_bf16(x2)
    both = _dot_nt(jnp.concatenate([rwh_ref[...], rwl_ref[...]], axis=0), xh)
    return both[0:N_EXPERTS] + both[N_EXPERTS:] + _dot_nt(rwh_ref[...], xl)


def _xattn_select(logits, rb_ref):
    e_max = jnp.max(logits, axis=0, keepdims=True)
    ex = jnp.exp(logits - e_max)
    scores = ex / jnp.sum(ex, axis=0, keepdims=True)
    biased = scores + rb_ref[...]
    sc = [scores[e:e + 1, :] for e in range(N_EXPERTS)]
    bi = [biased[e:e + 1, :] for e in range(N_EXPERTS)]
    epg = EXPERTS_PER_GROUP
    gscore = [_top2_of4(bi[g * epg:(g + 1) * epg]) for g in range(N_EXPERT_GROUPS)]
    _, gsel = _argmax_first(gscore)

    def pick(vals):
        return [functools.reduce(lambda acc, g: jnp.where(gsel == g, vals[g * epg + i], acc),
                                 range(1, N_EXPERT_GROUPS), vals[i]) for i in range(epg)]

    in_b = pick(bi)
    in_s = pick(sc)
    _, i0 = _argmax_first(in_b)
    _, i1 = _argmax_first([jnp.where(i0 == i, -jnp.inf, in_b[i]) for i in range(epg)])

    def take(vals, idx):
        return functools.reduce(lambda acc, i: jnp.where(idx == i, vals[i], acc), range(1, epg), vals[0])

    s0, s1 = take(in_s, i0), take(in_s, i1)
    w0, w1 = s0 / (s0 + s1), s1 / (s0 + s1)
    return gsel * epg + i0, gsel * epg + i1, w0, w1


def _xattn_rank(rows, selection, run, rt_ref):
    e0, e1, w0, w1 = selection
    tq = rows.stop - rows.start
    eid = lax.broadcasted_iota(I32, (N_EXPERTS, tq), 0)
    oh0 = eid == e0
    oh1 = eid == e1
    onehot = jnp.where(jnp.logical_or(oh0, oh1), 1.0, 0.0)
    rr = lax.broadcasted_iota(I32, (tq, tq), 0)
    cc = lax.broadcasted_iota(I32, (tq, tq), 1)
    upper = jnp.where(rr < cc, 1.0, 0.0).astype(BF16)
    prefix = _dot(onehot.astype(BF16), upper) + run
    r0 = jnp.sum(jnp.where(oh0, prefix, 0.0), axis=0, keepdims=True).astype(I32)
    r1 = jnp.sum(jnp.where(oh1, prefix, 0.0), axis=0, keepdims=True).astype(I32)

    zero = jnp.zeros((1, tq), I32)
    rt_ref[:, rows] = jnp.concatenate(
        [e0, e1, r0, r1, pltpu.bitcast(w0, I32), pltpu.bitcast(w1, I32), zero, zero], axis=0)
    return run + jnp.sum(onehot, axis=1, keepdims=True)


def _xattn_route(x1, kv, mem_len, wq, wo, g2, b2, rw_hi, rw_lo, rbias, b0, bsz, s_len):
    t_all, d = x1.shape
    tq = TOKEN_TILE
    nj = s_len // tq
    m = mem_len
    row = lambda bb, jj: (bb * nj + jj, 0)
    return pl.pallas_call(
        _xattn_kernel,
        grid=(bsz, nj),
        in_specs=[
            pl.BlockSpec((tq, d), row),
            pl.BlockSpec((m, 2 * d), lambda bb, jj: (b0 + bb, 0)),
            _const_spec((d, d)), _const_spec((d, d)),
            _const_spec((1, d)), _const_spec((1, d)),
            _const_spec((N_EXPERTS, d)), _const_spec((N_EXPERTS, d)),
            _const_spec((N_EXPERTS, 1)),
        ],
        out_specs=[
            pl.BlockSpec((tq, d), row),
            pl.BlockSpec((tq, d // 2), row),
            pl.BlockSpec((8, tq), lambda bb, jj: (0, bb * nj + jj)),
            _const_spec((N_EXPERTS, LANES)),
        ],
        out_shape=[
            jax.ShapeDtypeStruct((t_all, d), F32),
            jax.ShapeDtypeStruct((t_all, d // 2), I32),
            jax.ShapeDtypeStruct((8, t_all), I32),
            jax.ShapeDtypeStruct((N_EXPERTS, LANES), I32),
        ],
        scratch_shapes=[pltpu.VMEM((N_EXPERTS, 1), F32), pltpu.VMEM((d, d), BF16), pltpu.VMEM((d, d), BF16)],
        compiler_params=_tc_params(2),
        name="xattn_route",
    )(x1, kv, wq, wo, g2, b2, rw_hi, rw_lo, rbias)


def _sc_mesh():
    return plsc.VectorSubcoreMesh(core_axis_name="c", subcore_axis_name="s")


def _sc_params():
    return pltpu.CompilerParams(needs_layout_passes=False)


def _worker_id():
    return lax.axis_index("s") * lax.axis_size("c") + lax.axis_index("c")


def _sc_dispatch(xp, dest2d, n_rows):
    t_all, width = xp.shape
    chunk = dest2d.shape[1]
    tok_w = t_all // SC_WORKERS
    nch = tok_w // chunk
    slot1 = t_all // chunk
    assert t_all % (SC_WORKERS * chunk * 2) == 0

    def body(x_hbm, dest_hbm, out_hbm, idx0_v, idx1_v, buf0, buf1, sem_r, sem_w):
        wid = _worker_id()
        base = wid * tok_w
        pltpu.sync_copy(dest_hbm.at[pl.ds(wid * nch, nch)], idx0_v)
        pltpu.sync_copy(dest_hbm.at[pl.ds(slot1 + wid * nch, nch)], idx1_v)

        def read(c, buf, k):
            return pltpu.make_async_copy(x_hbm.at[pl.ds(base + c * chunk, chunk)], buf, sem_r.at[k])

        def scatter(c, buf):
            a = pltpu.make_async_copy(buf, out_hbm.at[idx0_v.at[c]], sem_w.at[0])
            b = pltpu.make_async_copy(buf, out_hbm.at[idx1_v.at[c]], sem_w.at[1])
            a.start()
            b.start()
            a.wait()
            b.wait()

        read(0, buf0, 0).start()

        @pl.loop(0, nch // 2)
        def _(g):
            c = 2 * g
            read(c + 1, buf1, 1).start()
            read(c, buf0, 0).wait()
            scatter(c, buf0)

            @pl.when(c + 2 < nch)
            def _():
                read(c + 2, buf0, 0).start()

            read(c + 1, buf1, 1).wait()
            scatter(c + 1, buf1)

    return pl.kernel(
        body,
        out_type=jax.ShapeDtypeStruct((n_rows, width), xp.dtype),
        mesh=_sc_mesh(),
        scratch_types=[
            pltpu.VMEM((nch, chunk), I32),
            pltpu.VMEM((nch, chunk), I32),
            pltpu.VMEM((chunk, width), xp.dtype),
            pltpu.VMEM((chunk, width), xp.dtype),
            pltpu.SemaphoreType.DMA((2,)),
            pltpu.SemaphoreType.DMA((2,)),
        ],
        compiler_params=_sc_params(),
        name="sc_dispatch",
    )(xp, dest2d)


def _sc_pack_weights(w2d):
    n_rows, width = w2d.shape
    half = width // 2
    rows_w = n_rows // SC_WORKERS
    chunk = (32 * 1024) // width
    lanes = 16
    assert n_rows % (SC_WORKERS * chunk) == 0 and half % lanes == 0

    def body(w_hbm, out_hbm, in_v, out_v):
        base = _worker_id() * rows_w

        def rne_hi16(bits):
            return bits + 0x7FFF + jnp.bitwise_and(lax.shift_right_logical(bits, 16), 1)

        @pl.loop(0, rows_w // chunk)
        def _(c):
            r0 = base + c * chunk
            pltpu.sync_copy(w_hbm.at[pl.ds(r0, chunk)], in_v)

            @pl.loop(0, chunk)
            def _(r):
                @plsc.parallel_loop(0, half, step=lanes, unroll=4)
                def _(j):
                    lo = rne_hi16(plsc.bitcast(in_v[r, pl.ds(j, lanes)], I32))
                    hi = rne_hi16(plsc.bitcast(in_v[r, pl.ds(half + j, lanes)], I32))
                    out_v[r, pl.ds(j, lanes)] = jnp.bitwise_or(
                        lax.shift_right_logical(lo, 16), jnp.bitwise_and(hi, jnp.int32(-65536)))

            pltpu.sync_copy(out_v, out_hbm.at[pl.ds(r0, chunk)])

    return pl.kernel(
        body,
        out_type=jax.ShapeDtypeStruct((n_rows, half), I32),
        mesh=_sc_mesh(),
        scratch_types=[pltpu.VMEM((chunk, width), F32), pltpu.VMEM((chunk, half), I32)],
        compiler_params=_sc_params(),
        name="sc_pack_weights",
    )(w2d)


def _sc_combine(ys, dest2d):
    n_idx_rows, chunk = dest2d.shape
    width = ys.shape[1]
    nch = n_idx_rows // SC_WORKERS
    assert n_idx_rows % (SC_WORKERS * 2) == 0

    def body(y_hbm, dest_hbm, out_hbm, idx_v, buf0, buf1, sem_g):
        wid = _worker_id()
        base = wid * nch * chunk
        pltpu.sync_copy(dest_hbm.at[pl.ds(wid * nch, nch)], idx_v)

        def gather(c, buf, k):
            return pltpu.make_async_copy(y_hbm.at[idx_v.at[c]], buf, sem_g.at[k])

        def write(c, buf):
            pltpu.sync_copy(buf, out_hbm.at[pl.ds(base + c * chunk, chunk)])

        gather(0, buf0, 0).start()

        @pl.loop(0, nch // 2)
        def _(g):
            c = 2 * g
            gather(c + 1, buf1, 1).start()
            gather(c, buf0, 0).wait()
            write(c, buf0)

            @pl.when(c + 2 < nch)
            def _():
                gather(c + 2, buf0, 0).start()

            gather(c + 1, buf1, 1).wait()
            write(c + 1, buf1)

    return pl.kernel(
        body,
        out_type=jax.ShapeDtypeStruct((n_idx_rows * chunk, width), ys.dtype),
        mesh=_sc_mesh(),
        scratch_types=[
            pltpu.VMEM((nch, chunk), I32),
            pltpu.VMEM((chunk, width), ys.dtype),
            pltpu.VMEM((chunk, width), ys.dtype),
            pltpu.SemaphoreType.DMA((2,)),
        ],
        compiler_params=_sc_params(),
        name="sc_combine",
    )(ys, dest2d)


def _pack_bf16_pairs(v):
    half = v.shape[1] // 2
    lo = pltpu.bitcast(v[:, :half].astype(BF16).astype(F32), jnp.uint32) >> 16
    hi = pltpu.bitcast(v[:, half:].astype(BF16).astype(F32), jnp.uint32) & jnp.uint32(0xFFFF0000)
    return pltpu.bitcast(hi | lo, I32)


def _unpack_bf16_pairs(w):
    w = pltpu.bitcast(w, jnp.uint32)
    lo = pltpu.bitcast(w << 16, F32)
    hi = pltpu.bitcast(w & jnp.uint32(0xFFFF0000), F32)
    return jnp.concatenate([lo, hi], axis=1)


def _ffn_kernel(be_ref, nv_ref, nu_ref, nxt_ref, xs_ref, wg_hbm, wu_hbm, wd_hbm, o_ref,
                wg_f, wu_f, wd_f, wg_s, wu_s, wd_s, sem):
    bm = EXPERT_ROWS

    def fetch(e):
        return (pltpu.make_async_copy(wg_hbm.at[e], wg_f, sem.at[0]),
                pltpu.make_async_copy(wu_hbm.at[e], wu_f, sem.at[1]),
                pltpu.make_async_copy(wd_hbm.at[e], wd_f, sem.at[2]))

    for sub in range(FFN_GROUP):
        i = pl.program_id(0) * FFN_GROUP + sub
        rows = slice(sub * bm, (sub + 1) * bm)

        @pl.when(nxt_ref[i] >= 0)
        def _():
            @pl.when(i == 0)
            def _():
                for cp in fetch(be_ref[0]):
                    cp.start()

            for cp in fetch(be_ref[i]):
                cp.wait()
            wg_s[...] = _unpack_bf16_pairs(wg_f[...]).astype(BF16)
            wu_s[...] = _unpack_bf16_pairs(wu_f[...]).astype(BF16)
            wd_s[...] = _unpack_bf16_pairs(wd_f[...]).astype(BF16)

            @pl.when(nxt_ref[i] < N_EXPERTS)
            def _():
                for cp in fetch(nxt_ref[i]):
                    cp.start()

        @pl.when(i < nu_ref[0])
        def _():
            live = lax.broadcasted_iota(I32, (bm, xs_ref.shape[1]), 0) < nv_ref[i]
            xb = _unpack_bf16_pairs(jnp.where(live, xs_ref[rows, :], 0)).astype(BF16)
            act = jax.nn.silu(_dot(xb, wg_s[...])) * _dot(xb, wu_s[...])
            o_ref[rows, :] = _pack_bf16_pairs(_dot(act.astype(BF16), wd_s[...]))


def _expert_ffn(xs, blk_expert, blk_valid, n_used, blk_next, w_gate, w_up, w_down):
    n_rows, half = xs.shape
    d = 2 * half
    de = 2 * w_gate.shape[2]
    win = FFN_GROUP * EXPERT_ROWS
    assert n_rows % win == 0
    rows = lambda s, be, nv, nu, nx: (jnp.minimum(s, (nu[0] - 1) // FFN_GROUP), 0)
    hbm = pl.BlockSpec(memory_space=pl.ANY)
    return pl.pallas_call(
        _ffn_kernel,
        grid_spec=pltpu.PrefetchScalarGridSpec(
            num_scalar_prefetch=4,
            grid=(n_rows // win,),
            in_specs=[pl.BlockSpec((win, half), rows), hbm, hbm, hbm],
            out_specs=pl.BlockSpec((win, half), rows),
            scratch_shapes=[
                pltpu.VMEM((d, de // 2), I32), pltpu.VMEM((d, de // 2), I32), pltpu.VMEM((de, d // 2), I32),
                pltpu.VMEM((d, de), BF16), pltpu.VMEM((d, de), BF16), pltpu.VMEM((de, d), BF16),
                pltpu.SemaphoreType.DMA((3,)),
            ],
        ),
        out_shape=jax.ShapeDtypeStruct((n_rows, half), I32),
        compiler_params=_tc_params(1),
        name="expert_ffn",
    )(blk_expert, blk_valid, n_used, blk_next, xs, w_gate, w_up, w_down)


def _combine_kernel(x_ref, y0_ref, y1_ref, rt_ref, g_ref, b_ref, o_ref):
    o_ref[...] = _expert_combine_ln(x_ref[...], y0_ref[...], y1_ref[...], rt_ref[...], g_ref[...], b_ref[...])


def _combine_kernel_into(x_ref, y0_ref, y1_ref, rt_ref, g_ref, b_ref, full_ref, o_ref):
    del full_ref
    _combine_kernel(x_ref, y0_ref, y1_ref, rt_ref, g_ref, b_ref, o_ref)


def _combine_ln(x2, y01, rt, g3, b3, into=None, row0=0, full_rows=None):
    t_all, d = x2.shape
    tq = TOKEN_TILE
    nt = t_all // tq
    blk0 = row0 // tq if full_rows else 0
    in_specs = [
        pl.BlockSpec((tq, d), lambda i: (i, 0)),
        pl.BlockSpec((tq, d // 2), lambda i: (i, 0)),
        pl.BlockSpec((tq, d // 2), lambda i: (i + nt, 0)),
        pl.BlockSpec((8, tq), lambda i: (0, i)),
        _const_spec((1, d)), _const_spec((1, d)),
    ]
    args = [x2, y01, y01, rt, g3, b3]
    body, aliases = _combine_kernel, {}
    if into is not None:
        in_specs.append(pl.BlockSpec(memory_space=pl.ANY))
        args.append(into)
        body, aliases = _combine_kernel_into, {len(args) - 1: 0}
    return pl.pallas_call(
        body,
        grid=(nt,),
        in_specs=in_specs,
        out_specs=pl.BlockSpec((tq, d), lambda i: (i + blk0, 0)),
        out_shape=jax.ShapeDtypeStruct((full_rows or t_all, d), F32),
        input_output_aliases=aliases,
        compiler_params=_tc_params(1),
        name="combine_ln",
    )(*args)


def _routing_tables(rt, counts):
    bm = EXPERT_ROWS
    t_all = rt.shape[1]
    n_rows = 2 * t_all + N_EXPERTS * bm
    cnt = counts[:, 0]
    padded = (cnt + bm - 1) // bm * bm
    ends = jnp.cumsum(padded)
    offs = ends - padded
    experts = rt[0:2]
    off_tok = jnp.sum(jnp.where(experts[None] == jnp.arange(N_EXPERTS, dtype=I32)[:, None, None],
                                offs[:, None, None], 0), axis=0)
    dest2d = (off_tok + rt[2:4]).reshape(-1, SC_CHUNK).astype(I32)
    blk_start = jnp.arange(n_rows // bm, dtype=I32) * bm
    blk_expert = jnp.minimum(jnp.sum(blk_start[:, None] >= ends[None, :], axis=1), N_EXPERTS - 1).astype(I32)
    live_end = jnp.sum(jnp.where(blk_expert[:, None] == jnp.arange(N_EXPERTS, dtype=I32)[None, :],
                                 (offs + cnt)[None, :], 0), axis=1)
    blk_valid = jnp.clip(live_end - blk_start, 0, bm).astype(I32)
    n_used = (ends[-1:] // bm).astype(I32)
    eid = jnp.arange(N_EXPERTS, dtype=I32)
    later_present = jnp.logical_and(eid[None, :] > eid[:, None], (cnt > 0)[None, :])
    next_present = jnp.min(jnp.where(later_present, eid[None, :], N_EXPERTS), axis=1)
    next_of_blk = jnp.sum(jnp.where(blk_expert[:, None] == eid[None, :], next_present[None, :], 0), axis=1)
    prev_expert = jnp.concatenate([jnp.full((1,), -1, I32), blk_expert[:-1]])
    is_first = jnp.logical_and(blk_start < ends[-1], blk_expert != prev_expert)
    blk_next = jnp.where(is_first, next_of_blk, -1).astype(I32)
    return dest2d, blk_expert, blk_valid, n_used, blk_next, n_rows


def _layer_tail(x1, kv, mem_len, p, router, b0, bsz, s_len):
    rw_hi, rw_lo, rbias = router
    x2, xp, rt, counts = _xattn_route(x1, kv, mem_len, p["xq"], p["xo"], p["ln2_g"], p["ln2_b"],
                                      rw_hi, rw_lo, rbias, b0, bsz, s_len)
    dest2d, blk_expert, blk_valid, n_used, blk_next, n_rows = _routing_tables(rt, counts)
    xs = _sc_dispatch(xp, dest2d, n_rows)
    if "packed" not in p:
        p["packed"] = tuple(_sc_pack_weights(w.reshape(-1, w.shape[2])).reshape(w.shape[0], w.shape[1], -1)
                            for w in (p["e_gate"], p["e_up"], p["e_down"]))
    ys = _expert_ffn(xs, blk_expert, blk_valid, n_used, blk_next, *p["packed"])
    y01 = _sc_combine(ys, dest2d)
    return x2, y01, rt


def _row(v):
    return v.reshape(1, -1).astype(F32)


def _common_params(xq, xkv, xo, ln2_g, ln2_b, e_gate, e_up, e_down, ln3_g, ln3_b):
    return dict(xq=xq, xkv=xkv, xo=xo, ln2_g=_row(ln2_g), ln2_b=_row(ln2_b),
                e_gate=e_gate, e_up=e_up, e_down=e_down, ln3_g=_row(ln3_g), ln3_b=_row(ln3_b))


def kernel(x, mem, positions, router_w, router_bias, l0_w_in, l0_sinks, l0_sgu_ln_g, l0_sgu_ln_b, l0_sgu_w, l0_sgu_b, l0_w_out, l0_ln1_g, l0_ln1_b, l0_xq, l0_xkv, l0_xo, l0_ln2_g, l0_ln2_b, l0_e_gate, l0_e_up, l0_e_down, l0_ln3_g, l0_ln3_b, l1_w_in, l1_pool_w, l1_pool_scale, l1_w_out, l1_ln1_g, l1_ln1_b, l1_xq, l1_xkv, l1_xo, l1_ln2_g, l1_ln2_b, l1_e_gate, l1_e_up, l1_e_down, l1_ln3_g, l1_ln3_b):
    bsz, s_len, d = x.shape
    assert s_len % TOKEN_TILE == 0 and TOKEN_TILE == 2 * SLAB_ROWS and SLAB_ROWS % BLOCK == 0
    xt = x.reshape(bsz * s_len, d)
    mem2d = mem.reshape(-1, d)

    rw_t = router_w.T.astype(F32)
    rw_hi = rw_t.astype(BF16)
    rw_lo = (rw_t - rw_hi.astype(F32)).astype(BF16)
    router = (rw_hi, rw_lo, router_bias.reshape(-1, 1).astype(F32))

    half = ROPE_DIM // 2
    inv_freq = (ROPE_THETA ** (-(jnp.arange(half, dtype=F32) * 2.0 / ROPE_DIM))).reshape(half, 1)
    etab_np = np.zeros((LANES, 3 * LANES), np.float32)
    cbase_np = np.ones((1, LANES), np.float32)
    for ln in range(LANES):
        dd = ln % HEAD_DIM
        if dd < ROPE_DIM:
            cbase_np[0, ln] = 0.0
            etab_np[[dd % half, half + dd % half], ln] = 1.0
            if dd >= half:
                etab_np[[2 * half + dd - half, 3 * half + dd - half], LANES + ln] = 1.0
            else:
                etab_np[[2 * half + dd, 3 * half + dd], 2 * LANES + ln] = -1.0
    etab = jnp.asarray(etab_np, BF16)
    cbase = jnp.asarray(cbase_np)
    pos_row = positions.reshape(1, -1).astype(I32)
    grp = jnp.arange(PIECE_COLS) // B_GROUP_DIM
    gsum = (grp[:, None] == grp[None, :]).astype(BF16)
    bs_full = jnp.repeat(l0_sgu_b.T.astype(F32), B_GROUP_DIM, axis=1)

    p0 = _common_params(l0_xq, l0_xkv, l0_xo, l0_ln2_g, l0_ln2_b, l0_e_gate, l0_e_up, l0_e_down,
                        l0_ln3_g, l0_ln3_b)
    p1 = _common_params(l1_xq, l1_xkv, l1_xo, l1_ln2_g, l1_ln2_b, l1_e_gate, l1_e_up, l1_e_down,
                        l1_ln3_g, l1_ln3_b)
    kv0 = _kv_proj(mem2d, p0["xkv"], bsz)
    kv1 = _kv_proj(mem2d, p1["xkv"], bsz)
    mem_len = mem.shape[1]

    n_split = BATCH_SPLIT if bsz % BATCH_SPLIT == 0 else 1
    nb = bsz // n_split
    out = None
    for part in range(n_split):
        b0 = part * nb
        x1 = _mixer0(xt, pos_row, l0_sinks.astype(F32), l0_w_in, inv_freq, etab, cbase, gsum,
                     _row(l0_sgu_ln_g), _row(l0_sgu_ln_b), l0_sgu_w.astype(F32), bs_full,
                     l0_w_out, _row(l0_ln1_g), _row(l0_ln1_b), b0, nb, s_len)
        x2, y01, rt = _layer_tail(x1, kv0, mem_len, p0, router, b0, nb, s_len)
        x1 = _mixer1(x2, y01, rt, p0["ln3_g"], p0["ln3_b"], l1_w_in, l1_pool_w, _row(l1_pool_scale),
                     l1_w_out, _row(l1_ln1_g), _row(l1_ln1_b), nb, s_len)
        x2, y01, rt = _layer_tail(x1, kv1, mem_len, p1, router, b0, nb, s_len)
        out = _combine_ln(x2, y01, rt, p1["ln3_g"], p1["ln3_b"], into=out, row0=b0 * s_len,
                          full_rows=bsz * s_len)
    return out.reshape(bsz, s_len, d)
```

```python
import functools

import numpy as np
import jax
import jax.numpy as jnp
from jax import lax
from jax.experimental import pallas as pl
from jax.experimental.pallas import tpu as pltpu
from jax.experimental.pallas import tpu_sc as plsc

F32 = jnp.float32
BF16 = jnp.bfloat16
I32 = jnp.int32

DEPTH = 2
ALPHA = (2.0 * DEPTH) ** 0.25
LN_EPS = 1e-5

HEAD_DIM = 64
A_Q_HEADS = 8
A_KV_HEADS = 2
A_GROUP = A_Q_HEADS // A_KV_HEADS
BLOCK = 128
ROPE_THETA = 500000.0
ROPE_DIM = HEAD_DIM // 4
A_WIDTH = A_Q_HEADS * HEAD_DIM
KV_WIDTH = A_KV_HEADS * HEAD_DIM
B_GROUPS = 8
B_GROUP_DIM = 64
B_WIDTH = B_GROUPS * B_GROUP_DIM
POOL_WINDOWS = (2, 4, 8, 16)
POOL_HALO = 16
X_HEADS = 4
N_EXPERTS = 16
N_EXPERT_GROUPS = 4
EXPERTS_PER_GROUP = 4

LANES = 128
TOKEN_TILE = 1024
SLAB_ROWS = 512
PIECE_COLS = 256
EXPERT_ROWS = 512
FFN_GROUP = 2
FFN_RING = 3
BATCH_SPLIT = 2
SC_WORKERS = 32
SC_CHUNK = 64
VMEM_LIMIT = 56 * 1024 * 1024
NEG_BIG = -1e30


def _layer_norm(z, g, b):
    mu = jnp.mean(z, axis=-1, keepdims=True)
    d = z - mu
    var = jnp.mean(d * d, axis=-1, keepdims=True)
    return d * lax.rsqrt(var + LN_EPS) * g + b


def _dot(a, b):
    return jnp.dot(a, b, preferred_element_type=F32)


def _dot_nt(a, b):
    return lax.dot_general(a, b, (((1,), (1,)), ((), ())), preferred_element_type=F32)


def _split_bf16(v):
    hi = v.astype(BF16)
    lo = (v - hi.astype(F32)).astype(BF16)
    return hi, lo


def _tc_params(n_axes):
    return pltpu.CompilerParams(dimension_semantics=("arbitrary",) * n_axes,
                                vmem_limit_bytes=VMEM_LIMIT)


def _const_spec(shape):
    nd = len(shape)
    return pl.BlockSpec(shape, lambda *_: (0,) * nd, pipeline_mode=pl.Buffered(1))


def _mixer0_kernel(sinks_ref, x_ref, pos_ref, win_ref, invf_ref, etab_ref, cbase_ref, gsum_ref,
                   lng_ref, lnb_ref, ws_ref, bs_ref, wout_ref, g1_ref, b1_ref,
                   o_ref, q_s, kv_s, u_s, vn_s, mix_s, wt_s, win_s, wout_s):
    b = pl.program_id(0)
    j = pl.program_id(1)
    tq = x_ref.shape[0]
    kvw = kv_s.shape[1]

    @pl.when(jnp.logical_and(b == 0, j == 0))
    def _():
        win_s[...] = win_ref[...].astype(BF16)
        wout_s[...] = wout_ref[...].astype(BF16)
        r = lax.broadcasted_iota(I32, (BLOCK, BLOCK), 0)
        c = lax.broadcasted_iota(I32, (BLOCK, BLOCK), 1)
        for g in range(B_GROUPS):
            wt_s[g] = jnp.where(c <= r, ws_ref[g], 0.0).astype(BF16)

    @pl.when(j == 0)
    def _():
        kv_s[0:BLOCK, :] = jnp.zeros((BLOCK, kvw), BF16)

    c1 = A_WIDTH
    c2 = c1 + KV_WIDTH
    c3 = c2 + KV_WIDTH
    c4 = c3 + B_WIDTH

    def rotary_tables(rows):
        n = rows.stop - rows.start
        ang = invf_ref[...] * pos_ref[:, rows].astype(F32)
        c8 = jnp.cos(ang)
        s8 = jnp.sin(ang)
        c8h = c8.astype(BF16).astype(F32)
        s8h = s8.astype(BF16).astype(F32)
        pad = jnp.zeros((LANES - 4 * c8.shape[0], n), F32)
        stack = jnp.concatenate([c8h, c8 - c8h, s8h, s8 - s8h, pad], axis=0)
        tabs = _dot(stack.T.astype(BF16), etab_ref[...])
        return tabs[:, 0:LANES] + cbase_ref[...], tabs[:, LANES:2 * LANES], tabs[:, 2 * LANES:]

    pw = PIECE_COLS
    n_pieces = win_s.shape[1] // pw
    assert (c1 // pw, c2 // pw, c3 // pw, c4 // pw) == (2, 2, 3, 5) and c3 % pw == 0 and n_pieces == 7

    def project(xb, k):
        return _dot(xb, win_s[:, k * pw:(k + 1) * pw])

    def prepare(rows, hk, tables, k):
        n = rows.stop - rows.start
        cs, sa, sb = tables

        def rope(t):
            return t * cs + pltpu.roll(t, ROPE_DIM // 2, 1) * sa + pltpu.roll(t, LANES - ROPE_DIM // 2, 1) * sb

        if k < 2:
            for c in range(pw // LANES):
                t = hk[:, c * LANES:(c + 1) * LANES] * (HEAD_DIM ** -0.5)
                col = k * (pw // LANES) + c
                q_s[rows, col * LANES:(col + 1) * LANES] = rope(t).astype(BF16)
        elif k == 2:
            low = lax.broadcasted_iota(I32, (n, LANES), 1) < HEAD_DIM
            kr = rope(hk[:, 0:KV_WIDTH])
            kx = pltpu.roll(kr, HEAD_DIM, 1)
            vr = hk[:, KV_WIDTH:]
            vx = pltpu.roll(vr, HEAD_DIM, 1)
            kv_cols = [jnp.where(low, kr, kx), jnp.where(low, kx, kr),
                       jnp.where(low, vr, 0.0), jnp.where(low, 0.0, vx),
                       jnp.where(low, vx, 0.0), jnp.where(low, 0.0, vr)]
            for c, col in enumerate(kv_cols):
                kv_s[BLOCK + rows.start:BLOCK + rows.stop, c * LANES:(c + 1) * LANES] = col.astype(BF16)
        elif k < 5:
            lo = (k - 3) * pw
            u_s[rows, lo:lo + pw] = jax.nn.gelu(hk)
        else:
            lo = (k - 5) * pw
            v = jax.nn.gelu(hk)
            gsum = gsum_ref[...]
            mean = _dot(v.astype(BF16), gsum) * (1.0 / B_GROUP_DIM)
            d = v - mean
            var = _dot((d * d).astype(BF16), gsum) * (1.0 / B_GROUP_DIM)
            vn_s[rows, lo:lo + pw] = (d * lax.rsqrt(var + LN_EPS) * lng_ref[:, lo:lo + pw]
                                      + lnb_ref[:, lo:lo + pw]).astype(BF16)

    qi = lax.broadcasted_iota(I32, (BLOCK, 2 * BLOCK), 0)
    kj = lax.broadcasted_iota(I32, (BLOCK, 2 * BLOCK), 1)
    rel = qi + BLOCK - kj
    band = jnp.logical_and(rel >= 0, rel < BLOCK)
    low_q = lax.broadcasted_iota(I32, (BLOCK, LANES), 1) < HEAD_DIM
    low_k = lax.broadcasted_iota(I32, (2 * BLOCK, LANES), 1) < HEAD_DIM
    ones_lo = jnp.where(low_k, 1.0, 0.0).astype(BF16)
    ones_hi = jnp.where(low_k, 0.0, 1.0).astype(BF16)
    zero_q = jnp.zeros((BLOCK, LANES), BF16)

    def block_body(n):
        r0 = n * BLOCK
        kv = kv_s[pl.ds(r0, 2 * BLOCK), :]
        qb = q_s[pl.ds(r0, BLOCK), :]
        valid = jnp.logical_and(band, kj >= jnp.where(j == 0, BLOCK, 0)) if n == 0 else band
        cols_per_kv = A_GROUP // 2
        scores = {}
        for hk in range(A_KV_HEADS):
            cols = range(hk * cols_per_kv, (hk + 1) * cols_per_kv)
            pieces = []
            for c in cols:
                qp = qb[:, c * LANES:(c + 1) * LANES]
                pieces += [jnp.where(low_q, qp, zero_q), jnp.where(low_q, zero_q, qp)]
            sc = _dot_nt(jnp.concatenate(pieces, axis=0), kv[:, hk * LANES:(hk + 1) * LANES])
            for i, c in enumerate(cols):
                for half in range(2):
                    r = (2 * i + half) * BLOCK
                    scores[c, half] = sc[r:r + BLOCK, :]
        vnb = vn_s[pl.ds(r0, BLOCK), :]
        parts = []
        for c in range(B_WIDTH // LANES):
            vp = vnb[:, c * LANES:(c + 1) * LANES]
            parts.append(_dot(wt_s[2 * c], jnp.where(low_q, vp, zero_q))
                         + _dot(wt_s[2 * c + 1], jnp.where(low_q, zero_q, vp)))
        probs, esink = {}, {}
        for (c, half), sc in scores.items():
            s = jnp.where(valid, sc, NEG_BIG)
            sink = sinks_ref[2 * c + half]
            m = jnp.maximum(jnp.max(s, axis=-1, keepdims=True), sink)
            probs[c, half] = jnp.exp(s - m).astype(BF16)
            esink[c, half] = jnp.exp(sink - m)
        res = {}
        for hk in range(A_KV_HEADS):
            cols = range(hk * cols_per_kv, (hk + 1) * cols_per_kv)
            for half in range(2):
                vcol = kv[:, (2 + 2 * hk + half) * LANES:(3 + 2 * hk + half) * LANES]
                vm = jnp.concatenate([vcol, ones_lo if half == 0 else ones_hi], axis=1)
                pv = _dot(jnp.concatenate([probs[c, half] for c in cols], axis=0), vm)
                for i, c in enumerate(cols):
                    part = pv[i * BLOCK:(i + 1) * BLOCK, :]
                    res[c] = part if half == 0 else res[c] + part
        for c in range(A_WIDTH // LANES):
            den = res[c][:, LANES:] + jnp.where(low_q, esink[c, 0], esink[c, 1])
            mix_s[pl.ds(r0, BLOCK), c * LANES:(c + 1) * LANES] = (res[c][:, :LANES] / den).astype(BF16)
        mixed = jnp.concatenate(parts, axis=1) + bs_ref[...]
        mix_s[pl.ds(r0, BLOCK), A_WIDTH:] = (u_s[pl.ds(r0, BLOCK), :] * mixed).astype(BF16)

    def out_cols(rows, c):
        return _dot(mix_s[rows, :], wout_s[:, c * pw:(c + 1) * pw])

    def finish(rows, z_cols, i):
        r = slice(rows.start + i * BLOCK, rows.start + (i + 1) * BLOCK)
        z = ALPHA * x_ref[r, :] + jnp.concatenate([zc[i * BLOCK:(i + 1) * BLOCK, :] for zc in z_cols], axis=1)
        o_ref[r, :] = _layer_norm(z, g1_ref[...], b1_ref[...])

    assert tq == 2 * SLAB_ROWS and SLAB_ROWS == 4 * BLOCK and wout_s.shape[1] == 4 * pw
    sa, sb = slice(0, SLAB_ROWS), slice(SLAB_ROWS, tq)
    tab_a, tab_b = rotary_tables(sa), rotary_tables(sb)
    xa = x_ref[sa, :].astype(BF16)
    ha = [project(xa, k) for k in range(n_pieces)]
    xb = x_ref[sb, :].astype(BF16)
    hb = []
    for k in range(n_pieces):
        hb.append(project(xb, k))
        prepare(sa, ha[k], tab_a, k)
    pieces_b = iter(range(n_pieces))
    for n in range(4):
        block_body(n)
        for k in [next(pieces_b) for _ in range(2 if n < 3 else 1)]:
            prepare(sb, hb[k], tab_b, k)
    za = []
    for n in range(4):
        block_body(4 + n)
        za.append(out_cols(sa, n))
    kv_s[0:BLOCK, :] = kv_s[tq:tq + BLOCK, :]
    zb = []
    for n in range(4):
        zb.append(out_cols(sb, n))
        finish(sa, za, n)
    for n in range(4):
        finish(sb, zb, n)


def _mixer0(x, pos_row, sinks, w_in, invf, etab, cbase, gsum, lng, lnb, w_s, bs_full, w_out, g1, b1,
            b0, bsz, s_len):
    d = x.shape[1]
    t_all = bsz * s_len
    tq = TOKEN_TILE
    nj = s_len // tq
    row = lambda bb, jj: (bb * nj + jj, 0)
    in_w = w_in.shape[1]
    return pl.pallas_call(
        _mixer0_kernel,
        grid=(bsz, nj),
        in_specs=[
            pl.BlockSpec(memory_space=pltpu.SMEM),
            pl.BlockSpec((tq, d), lambda bb, jj: ((b0 + bb) * nj + jj, 0)),
            pl.BlockSpec((1, tq), lambda bb, jj: (0, (b0 + bb) * nj + jj)),
            _const_spec((d, in_w)),
            _const_spec((ROPE_DIM // 2, 1)), _const_spec((LANES, 3 * LANES)), _const_spec((1, LANES)),
            _const_spec((PIECE_COLS, PIECE_COLS)),
            _const_spec((1, B_WIDTH)), _const_spec((1, B_WIDTH)),
            _const_spec((B_GROUPS, BLOCK, BLOCK)),
            _const_spec((BLOCK, B_WIDTH)),
            _const_spec((A_WIDTH + B_WIDTH, d)),
            _const_spec((1, d)), _const_spec((1, d)),
        ],
        out_specs=pl.BlockSpec((tq, d), row),
        out_shape=jax.ShapeDtypeStruct((t_all, d), F32),
        scratch_shapes=[
            pltpu.VMEM((tq, A_WIDTH), BF16),
            pltpu.VMEM((tq + BLOCK, 6 * LANES), BF16),
            pltpu.VMEM((tq, B_WIDTH), F32),
            pltpu.VMEM((tq, B_WIDTH), BF16),
            pltpu.VMEM((tq, A_WIDTH + B_WIDTH), BF16),
            pltpu.VMEM((B_GROUPS, BLOCK, BLOCK), BF16),
            pltpu.VMEM((d, in_w), BF16),
            pltpu.VMEM((A_WIDTH + B_WIDTH, d), BF16),
        ],
        compiler_params=_tc_params(2),
        name="mixer0",
    )(sinks, x, pos_row, w_in, invf, etab, cbase, gsum, lng, lnb, w_s, bs_full, w_out, g1, b1)


def _expert_combine_ln(x2, y0_packed, y1_packed, rt, g, b):
    wt = pltpu.bitcast(rt, F32).T
    y = wt[:, 4:5] * _unpack_bf16_pairs(y0_packed) + wt[:, 5:6] * _unpack_bf16_pairs(y1_packed)
    return _layer_norm(ALPHA * x2 + y, g, b)


def _mixer1_kernel(x2_ref, y0_ref, y1_ref, rt_ref, g3_ref, b3_ref,
                   win_ref, pw_ref, ps_ref, wout_ref, g1_ref, b1_ref, o_ref,
                   h_s, mp_s, win_s, pw_s, wout_s):
    j = pl.program_id(1)
    tq = x2_ref.shape[0]
    gw = x2_ref.shape[1] // len(POOL_WINDOWS)
    slabs = [slice(c * SLAB_ROWS, (c + 1) * SLAB_ROWS) for c in range(tq // SLAB_ROWS)]

    @pl.when(jnp.logical_and(pl.program_id(0) == 0, j == 0))
    def _():
        win_s[...] = win_ref[...].astype(BF16)
        pw_s[...] = pw_ref[...].astype(BF16)
        wout_s[...] = wout_ref[...].astype(BF16)

    @pl.when(j == 0)
    def _():
        h_s[0:POOL_HALO, :] = jnp.zeros((POOL_HALO, h_s.shape[1]), F32)

    ng = len(POOL_WINDOWS)
    quarter = SLAB_ROWS // ng

    def sub(rows, i):
        return slice(rows.start + i * quarter, rows.start + (i + 1) * quarter)

    def load_in(rows, i):
        r = sub(rows, i)
        return _expert_combine_ln(x2_ref[r, :], y0_ref[r, :], y1_ref[r, :], rt_ref[:, r],
                                  g3_ref[...], b3_ref[...])

    def project(rows, xb, g):
        lo, hi = g * gw, (g + 1) * gw
        h_s[POOL_HALO + rows.start:POOL_HALO + rows.stop, lo:hi] = _dot(xb, win_s[:, lo:hi])

    def pool(rows, g):
        win = POOL_WINDOWS[g]
        lo, hi = g * gw, (g + 1) * gw
        n = rows.stop - rows.start
        t_pos = j * tq + rows.start + lax.broadcasted_iota(I32, (n, 1), 0)
        ext = h_s[rows.start:rows.stop + POOL_HALO, lo:hi]
        acc = ext
        shift = 1
        while shift < win:
            acc = acc + pltpu.roll(acc, shift, 0)
            shift *= 2
        count = jnp.minimum(t_pos + 1, win).astype(F32)
        pooled = acc[POOL_HALO:, :] / count - ext[POOL_HALO:, :]
        mapped = _dot(pooled.astype(BF16), pw_s[g])
        mp_s[rows, lo:hi] = (mapped * ps_ref[:, lo:hi]).astype(BF16)

    def out_cols(rows, g):
        return _dot(mp_s[rows, :], wout_s[:, g * gw:(g + 1) * gw])

    def finish(rows, x_parts, z_cols, i):
        lo, hi = i * quarter, (i + 1) * quarter
        z = ALPHA * x_parts[i] + jnp.concatenate([zc[lo:hi, :] for zc in z_cols], axis=1)
        o_ref[sub(rows, i), :] = _layer_norm(z, g1_ref[...], b1_ref[...])

    assert len(slabs) == 2
    sa, sb = slabs
    xa = [load_in(sa, i) for i in range(ng)]
    xa_b = jnp.concatenate(xa, axis=0).astype(BF16)
    xb = []
    for g in range(ng):
        project(sa, xa_b, g)
        xb.append(load_in(sb, g))
    xb_b = jnp.concatenate(xb, axis=0).astype(BF16)
    for g in range(ng):
        project(sb, xb_b, g)
        pool(sa, g)
    za = []
    for g in range(ng):
        za.append(out_cols(sa, g))
        pool(sb, g)
    zb = []
    for g in range(ng):
        zb.append(out_cols(sb, g))
        finish(sa, xa, za, g)
    for g in range(ng):
        finish(sb, xb, zb, g)
    h_s[0:POOL_HALO, :] = h_s[tq:tq + POOL_HALO, :]


def _mixer1(x2, y01, rt, g3, b3, w_in, pool_w, pool_scale, w_out, g1, b1, bsz, s_len):
    t_all, d = x2.shape
    tq = TOKEN_TILE
    nj = s_len // tq
    nt = bsz * nj
    row = lambda bb, jj: (bb * nj + jj, 0)
    ng = len(POOL_WINDOWS)
    return pl.pallas_call(
        _mixer1_kernel,
        grid=(bsz, nj),
        in_specs=[
            pl.BlockSpec((tq, d), row),
            pl.BlockSpec((tq, d // 2), row),
            pl.BlockSpec((tq, d // 2), lambda bb, jj: (nt + bb * nj + jj, 0)),
            pl.BlockSpec((8, tq), lambda bb, jj: (0, bb * nj + jj)),
            _const_spec((1, d)), _const_spec((1, d)),
            _const_spec((d, d)),
            _const_spec((ng, d // ng, d // ng)),
            _const_spec((1, d)),
            _const_spec((d, d)),
            _const_spec((1, d)), _const_spec((1, d)),
        ],
        out_specs=pl.BlockSpec((tq, d), row),
        out_shape=jax.ShapeDtypeStruct((t_all, d), F32),
        scratch_shapes=[pltpu.VMEM((tq + POOL_HALO, d), F32), pltpu.VMEM((tq, d), BF16),
                        pltpu.VMEM((d, d), BF16), pltpu.VMEM((ng, d // ng, d // ng), BF16),
                        pltpu.VMEM((d, d), BF16)],
        compiler_params=_tc_params(2),
        name="mixer1",
    )(x2, y01, y01, rt, g3, b3, w_in, pool_w, pool_scale, w_out, g1, b1)


def _kv_kernel(mem_ref, w_ref, o_ref, w_s):
    @pl.when(pl.program_id(0) == 0)
    def _():
        w_s[...] = w_ref[...].astype(BF16)

    o_ref[...] = _dot(mem_ref[...].astype(BF16), w_s[...]).astype(BF16)


def _kv_proj(mem2d, wkv, bsz):
    rows, d = mem2d.shape
    m = rows // bsz
    return pl.pallas_call(
        _kv_kernel,
        grid=(bsz,),
        in_specs=[pl.BlockSpec((m, d), lambda i: (i, 0)), _const_spec(wkv.shape)],
        out_specs=pl.BlockSpec((m, wkv.shape[1]), lambda i: (i, 0)),
        out_shape=jax.ShapeDtypeStruct((rows, wkv.shape[1]), BF16),
        scratch_shapes=[pltpu.VMEM(wkv.shape, BF16)],
        compiler_params=_tc_params(1),
        name="kv_proj",
    )(mem2d, wkv)


def _top2_of4(v):
    hi01, lo01 = jnp.maximum(v[0], v[1]), jnp.minimum(v[0], v[1])
    hi23, lo23 = jnp.maximum(v[2], v[3]), jnp.minimum(v[2], v[3])
    return jnp.maximum(hi01, hi23) + jnp.maximum(jnp.minimum(hi01, hi23), jnp.maximum(lo01, lo23))


def _argmax_first(vals):
    best, idx = vals[0], jnp.zeros(vals[0].shape, I32)
    for i in range(1, len(vals)):
        better = vals[i] > best
        best = jnp.where(better, vals[i], best)
        idx = jnp.where(better, i, idx)
    return best, idx


def _xattn_kernel(x_ref, kv_ref, wq_ref, wo_ref, g2_ref, b2_ref, rwh_ref, rwl_ref, rb_ref,
                  x2_ref, xp_ref, rt_ref, cnt_ref, run_s, wq_s, wo_s):
    first = jnp.logical_and(pl.program_id(0) == 0, pl.program_id(1) == 0)
    tq, d = x_ref.shape

    @pl.when(first)
    def _():
        run_s[...] = jnp.zeros(run_s.shape, F32)
        wq_s[...] = wq_ref[...].astype(BF16)
        wo_s[...] = wo_ref[...].astype(BF16)

    slabs = [slice(c * SLAB_ROWS, (c + 1) * SLAB_ROWS) for c in range(tq // SLAB_ROWS)]
    run = run_s[...]
    zs = [_xattn_attend(*_xattn_query(rows, x_ref, wq_s), kv_ref, wo_s) for rows in slabs]
    picks = []
    for rows, z in zip(slabs, zs):
        logits = _xattn_norm_logits(rows, z, g2_ref, b2_ref, rwh_ref, rwl_ref, x2_ref, xp_ref)
        picks.append(_xattn_select(logits, rb_ref))
    for rows, pick in zip(slabs, picks):
        run = _xattn_rank(rows, pick, run, rt_ref)
    run_s[...] = run
    cnt_ref[...] = jnp.broadcast_to(run, cnt_ref.shape).astype(I32)


def _xattn_query(rows, x_ref, wq_s):
    hd = x_ref.shape[1] // X_HEADS
    x = x_ref[rows, :]
    return x, _dot(x.astype(BF16), wq_s[...]) * (hd ** -0.5)


def _xattn_attend(x, q, kv_ref, wo_s):
    d = x.shape[1]
    hd = d // X_HEADS
    outs = []
    for hx in range(X_HEADS):
        qh = q[:, hx * hd:(hx + 1) * hd].astype(BF16)
        kh = kv_ref[:, hx * hd:(hx + 1) * hd]
        vh = kv_ref[:, d + hx * hd:d + (hx + 1) * hd]
        s = _dot_nt(qh, kh)
        p = jnp.exp(s - jnp.max(s, axis=-1, keepdims=True))
        o = _dot(p.astype(BF16), vh) / jnp.sum(p, axis=-1, keepdims=True)
        outs.append(o.astype(BF16))
    return ALPHA * x + _dot(jnp.concatenate(outs, axis=1), wo_s[...])


def _xattn_norm_logits(rows, z, g2_ref, b2_ref, rwh_ref, rwl_ref, x2_ref, xp_ref):
    x2 = _layer_norm(z, g2_ref[...], b2_ref[...])
    x2_ref[rows, :] = x2

    xp_ref[rows, :] = _pack_bf16_pairs(x2)

    xh, xl = _split_bf16(x2)
    both = _dot_nt(jnp.concatenate([rwh_ref[...], rwl_ref[...]], axis=0), xh)
    return both[0:N_EXPERTS] + both[N_EXPERTS:] + _dot_nt(rwh_ref[...], xl)


def _xattn_select(logits, rb_ref):
    e_max = jnp.max(logits, axis=0, keepdims=True)
    ex = jnp.exp(logits - e_max)
    scores = ex / jnp.sum(ex, axis=0, keepdims=True)
    biased = scores + rb_ref[...]
    sc = [scores[e:e + 1, :] for e in range(N_EXPERTS)]
    bi = [biased[e:e + 1, :] for e in range(N_EXPERTS)]
    epg = EXPERTS_PER_GROUP
    gscore = [_top2_of4(bi[g * epg:(g + 1) * epg]) for g in range(N_EXPERT_GROUPS)]
    _, gsel = _argmax_first(gscore)

    def pick(vals):
        return [functools.reduce(lambda acc, g: jnp.where(gsel == g, vals[g * epg + i], acc),
                                 range(1, N_EXPERT_GROUPS), vals[i]) for i in range(epg)]

    in_b = pick(bi)
    in_s = pick(sc)
    _, i0 = _argmax_first(in_b)
    _, i1 = _argmax_first([jnp.where(i0 == i, -jnp.inf, in_b[i]) for i in range(epg)])

    def take(vals, idx):
        return functools.reduce(lambda acc, i: jnp.where(idx == i, vals[i], acc), range(1, epg), vals[0])

    s0, s1 = take(in_s, i0), take(in_s, i1)
    w0, w1 = s0 / (s0 + s1), s1 / (s0 + s1)
    return gsel * epg + i0, gsel * epg + i1, w0, w1


def _xattn_rank(rows, selection, run, rt_ref):
    e0, e1, w0, w1 = selection
    tq = rows.stop - rows.start
    eid = lax.broadcasted_iota(I32, (N_EXPERTS, tq), 0)
    oh0 = eid == e0
    oh1 = eid == e1
    onehot = jnp.where(jnp.logical_or(oh0, oh1), 1.0, 0.0)
    rr = lax.broadcasted_iota(I32, (tq, tq), 0)
    cc = lax.broadcasted_iota(I32, (tq, tq), 1)
    upper = jnp.where(rr < cc, 1.0, 0.0).astype(BF16)
    prefix = _dot(onehot.astype(BF16), upper) + run
    r0 = jnp.sum(jnp.where(oh0, prefix, 0.0), axis=0, keepdims=True).astype(I32)
    r1 = jnp.sum(jnp.where(oh1, prefix, 0.0), axis=0, keepdims=True).astype(I32)

    zero = jnp.zeros((1, tq), I32)
    rt_ref[:, rows] = jnp.concatenate(
        [e0, e1, r0, r1, pltpu.bitcast(w0, I32), pltpu.bitcast(w1, I32), zero, zero], axis=0)
    return run + jnp.sum(onehot, axis=1, keepdims=True)


def _xattn_route(x1, kv, mem_len, wq, wo, g2, b2, rw_hi, rw_lo, rbias, b0, bsz, s_len):
    t_all, d = x1.shape
    tq = TOKEN_TILE
    nj = s_len // tq
    m = mem_len
    row = lambda bb, jj: (bb * nj + jj, 0)
    return pl.pallas_call(
        _xattn_kernel,
        grid=(bsz, nj),
        in_specs=[
            pl.BlockSpec((tq, d), row),
            pl.BlockSpec((m, 2 * d), lambda bb, jj: (b0 + bb, 0)),
            _const_spec((d, d)), _const_spec((d, d)),
            _const_spec((1, d)), _const_spec((1, d)),
            _const_spec((N_EXPERTS, d)), _const_spec((N_EXPERTS, d)),
            _const_spec((N_EXPERTS, 1)),
        ],
        out_specs=[
            pl.BlockSpec((tq, d), row),
            pl.BlockSpec((tq, d // 2), row),
            pl.BlockSpec((8, tq), lambda bb, jj: (0, bb * nj + jj)),
            _const_spec((N_EXPERTS, LANES)),
        ],
        out_shape=[
            jax.ShapeDtypeStruct((t_all, d), F32),
            jax.ShapeDtypeStruct((t_all, d // 2), I32),
            jax.ShapeDtypeStruct((8, t_all), I32),
            jax.ShapeDtypeStruct((N_EXPERTS, LANES), I32),
        ],
        scratch_shapes=[pltpu.VMEM((N_EXPERTS, 1), F32), pltpu.VMEM((d, d), BF16), pltpu.VMEM((d, d), BF16)],
        compiler_params=_tc_params(2),
        name="xattn_route",
    )(x1, kv, wq, wo, g2, b2, rw_hi, rw_lo, rbias)


def _sc_mesh():
    return plsc.VectorSubcoreMesh(core_axis_name="c", subcore_axis_name="s")


def _sc_params():
    return pltpu.CompilerParams(needs_layout_passes=False)


def _worker_id():
    return lax.axis_index("s") * lax.axis_size("c") + lax.axis_index("c")


def _sc_dispatch(xp, dest2d, n_rows):
    t_all, width = xp.shape
    chunk = dest2d.shape[1]
    tok_w = t_all // SC_WORKERS
    nch = tok_w // chunk
    slot1 = t_all // chunk
    assert t_all % (SC_WORKERS * chunk * 2) == 0

    def body(x_hbm, dest_hbm, out_hbm, idx0_v, idx1_v, buf0, buf1, sem_r, sem_w):
        wid = _worker_id()
        base = wid * tok_w
        pltpu.sync_copy(dest_hbm.at[pl.ds(wid * nch, nch)], idx0_v)
        pltpu.sync_copy(dest_hbm.at[pl.ds(slot1 + wid * nch, nch)], idx1_v)

        def read(c, buf, k):
            return pltpu.make_async_copy(x_hbm.at[pl.ds(base + c * chunk, chunk)], buf, sem_r.at[k])

        def scatter(c, buf):
            a = pltpu.make_async_copy(buf, out_hbm.at[idx0_v.at[c]], sem_w.at[0])
            b = pltpu.make_async_copy(buf, out_hbm.at[idx1_v.at[c]], sem_w.at[1])
            a.start()
            b.start()
            a.wait()
            b.wait()

        read(0, buf0, 0).start()

        @pl.loop(0, nch // 2)
        def _(g):
            c = 2 * g
            read(c + 1, buf1, 1).start()
            read(c, buf0, 0).wait()
            scatter(c, buf0)

            @pl.when(c + 2 < nch)
            def _():
                read(c + 2, buf0, 0).start()

            read(c + 1, buf1, 1).wait()
            scatter(c + 1, buf1)

    return pl.kernel(
        body,
        out_type=jax.ShapeDtypeStruct((n_rows, width), xp.dtype),
        mesh=_sc_mesh(),
        scratch_types=[
            pltpu.VMEM((nch, chunk), I32),
            pltpu.VMEM((nch, chunk), I32),
            pltpu.VMEM((chunk, width), xp.dtype),
            pltpu.VMEM((chunk, width), xp.dtype),
            pltpu.SemaphoreType.DMA((2,)),
            pltpu.SemaphoreType.DMA((2,)),
        ],
        compiler_params=_sc_params(),
        name="sc_dispatch",
    )(xp, dest2d)


def _sc_pack_weights(w2d):
    n_rows, width = w2d.shape
    half = width // 2
    rows_w = n_rows // SC_WORKERS
    chunk = (32 * 1024) // width
    lanes = 16
    assert n_rows % (SC_WORKERS * chunk) == 0 and half % lanes == 0

    def body(w_hbm, out_hbm, in_v, out_v):
        base = _worker_id() * rows_w

        def rne_hi16(bits):
            return bits + 0x7FFF + jnp.bitwise_and(lax.shift_right_logical(bits, 16), 1)

        @pl.loop(0, rows_w // chunk)
        def _(c):
            r0 = base + c * chunk
            pltpu.sync_copy(w_hbm.at[pl.ds(r0, chunk)], in_v)

            @pl.loop(0, chunk)
            def _(r):
                @plsc.parallel_loop(0, half, step=lanes, unroll=4)
                def _(j):
                    lo = rne_hi16(plsc.bitcast(in_v[r, pl.ds(j, lanes)], I32))
                    hi = rne_hi16(plsc.bitcast(in_v[r, pl.ds(half + j, lanes)], I32))
                    out_v[r, pl.ds(j, lanes)] = jnp.bitwise_or(
                        lax.shift_right_logical(lo, 16), jnp.bitwise_and(hi, jnp.int32(-65536)))

            pltpu.sync_copy(out_v, out_hbm.at[pl.ds(r0, chunk)])

    return pl.kernel(
        body,
        out_type=jax.ShapeDtypeStruct((n_rows, half), I32),
        mesh=_sc_mesh(),
        scratch_types=[pltpu.VMEM((chunk, width), F32), pltpu.VMEM((chunk, half), I32)],
        compiler_params=_sc_params(),
        name="sc_pack_weights",
    )(w2d)


def _sc_combine(ys, dest2d):
    n_idx_rows, chunk = dest2d.shape
    width = ys.shape[1]
    nch = n_idx_rows // SC_WORKERS
    assert n_idx_rows % (SC_WORKERS * 2) == 0

    def body(y_hbm, dest_hbm, out_hbm, idx_v, buf0, buf1, sem_g):
        wid = _worker_id()
        base = wid * nch * chunk
        pltpu.sync_copy(dest_hbm.at[pl.ds(wid * nch, nch)], idx_v)

        def gather(c, buf, k):
            return pltpu.make_async_copy(y_hbm.at[idx_v.at[c]], buf, sem_g.at[k])

        def write(c, buf):
            pltpu.sync_copy(buf, out_hbm.at[pl.ds(base + c * chunk, chunk)])

        gather(0, buf0, 0).start()

        @pl.loop(0, nch // 2)
        def _(g):
            c = 2 * g
            gather(c + 1, buf1, 1).start()
            gather(c, buf0, 0).wait()
            write(c, buf0)

            @pl.when(c + 2 < nch)
            def _():
                gather(c + 2, buf0, 0).start()

            gather(c + 1, buf1, 1).wait()
            write(c + 1, buf1)

    return pl.kernel(
        body,
        out_type=jax.ShapeDtypeStruct((n_idx_rows * chunk, width), ys.dtype),
        mesh=_sc_mesh(),
        scratch_types=[
            pltpu.VMEM((nch, chunk), I32),
            pltpu.VMEM((chunk, width), ys.dtype),
            pltpu.VMEM((chunk, width), ys.dtype),
            pltpu.SemaphoreType.DMA((2,)),
        ],
        compiler_params=_sc_params(),
        name="sc_combine",
    )(ys, dest2d)


def _pack_bf16_pairs(v):
    half = v.shape[1] // 2
    lo = pltpu.bitcast(v[:, :half].astype(BF16).astype(F32), jnp.uint32) >> 16
    hi = pltpu.bitcast(v[:, half:].astype(BF16).astype(F32), jnp.uint32) & jnp.uint32(0xFFFF0000)
    return pltpu.bitcast(hi | lo, I32)


def _unpack_bf16_pairs(w):
    w = pltpu.bitcast(w, jnp.uint32)
    lo = pltpu.bitcast(w << 16, F32)
    hi = pltpu.bitcast(w & jnp.uint32(0xFFFF0000), F32)
    return jnp.concatenate([lo, hi], axis=1)


def _ffn_kernel(be_ref, nv_ref, nu_ref, nxt_ref, xs_hbm, wg_hbm, wu_hbm, wd_hbm, o_ref,
                wg_f, wu_f, wd_f, wg_s, wu_s, wd_s, sem, xs_buf, sem_x):
    bm = EXPERT_ROWS
    step = pl.program_id(0)
    n_steps = pl.num_programs(0)
    win = FFN_GROUP * bm

    def window(t):
        slot = t % FFN_RING
        return pltpu.make_async_copy(xs_hbm.at[pl.ds(t * win, win)], xs_buf.at[slot], sem_x.at[slot])

    def used(t):
        return jnp.logical_and(t < n_steps, t * FFN_GROUP < nu_ref[0])

    @pl.when(step == 0)
    def _():
        for t in range(FFN_RING - 1):
            @pl.when(used(t))
            def _():
                window(t).start()

    @pl.when(used(step))
    def _():
        window(step).wait()

    @pl.when(used(step + FFN_RING - 1))
    def _():
        window(step + FFN_RING - 1).start()

    xs_ref = xs_buf.at[step % FFN_RING]

    def fetch(e):
        return (pltpu.make_async_copy(wg_hbm.at[e], wg_f, sem.at[0]),
                pltpu.make_async_copy(wu_hbm.at[e], wu_f, sem.at[1]),
                pltpu.make_async_copy(wd_hbm.at[e], wd_f, sem.at[2]))

    for sub in range(FFN_GROUP):
        i = step * FFN_GROUP + sub
        rows = slice(sub * bm, (sub + 1) * bm)

        @pl.when(nxt_ref[i] >= 0)
        def _():
            @pl.when(i == 0)
            def _():
                for cp in fetch(be_ref[0]):
                    cp.start()

            for cp in fetch(be_ref[i]):
                cp.wait()
            wg_s[...] = _unpack_bf16_pairs(wg_f[...]).astype(BF16)
            wu_s[...] = _unpack_bf16_pairs(wu_f[...]).astype(BF16)
            wd_s[...] = _unpack_bf16_pairs(wd_f[...]).astype(BF16)

            @pl.when(nxt_ref[i] < N_EXPERTS)
            def _():
                for cp in fetch(nxt_ref[i]):
                    cp.start()

        @pl.when(i < nu_ref[0])
        def _():
            live = lax.broadcasted_iota(I32, (bm, xs_ref.shape[1]), 0) < nv_ref[i]
            xb = _unpack_bf16_pairs(jnp.where(live, xs_ref[rows, :], 0)).astype(BF16)
            act = jax.nn.silu(_dot(xb, wg_s[...])) * _dot(xb, wu_s[...])
            o_ref[rows, :] = _pack_bf16_pairs(_dot(act.astype(BF16), wd_s[...]))


def _expert_ffn(xs, blk_expert, blk_valid, n_used, blk_next, w_gate, w_up, w_down):
    n_rows, half = xs.shape
    d = 2 * half
    de = 2 * w_gate.shape[2]
    win = FFN_GROUP * EXPERT_ROWS
    assert n_rows % win == 0
    rows = lambda s, be, nv, nu, nx: (jnp.minimum(s, (nu[0] - 1) // FFN_GROUP), 0)
    hbm = pl.BlockSpec(memory_space=pl.ANY)
    return pl.pallas_call(
        _ffn_kernel,
        grid_spec=pltpu.PrefetchScalarGridSpec(
            num_scalar_prefetch=4,
            grid=(n_rows // win,),
            in_specs=[hbm, hbm, hbm, hbm],
            out_specs=pl.BlockSpec((win, half), rows),
            scratch_shapes=[
                pltpu.VMEM((d, de // 2), I32), pltpu.VMEM((d, de // 2), I32), pltpu.VMEM((de, d // 2), I32),
                pltpu.VMEM((d, de), BF16), pltpu.VMEM((d, de), BF16), pltpu.VMEM((de, d), BF16),
                pltpu.SemaphoreType.DMA((3,)),
                pltpu.VMEM((FFN_RING, win, half), I32),
                pltpu.SemaphoreType.DMA((FFN_RING,)),
            ],
        ),
        out_shape=jax.ShapeDtypeStruct((n_rows, half), I32),
        compiler_params=_tc_params(1),
        name="expert_ffn",
    )(blk_expert, blk_valid, n_used, blk_next, xs, w_gate, w_up, w_down)


def _combine_kernel(x_ref, y0_ref, y1_ref, rt_ref, g_ref, b_ref, o_ref):
    o_ref[...] = _expert_combine_ln(x_ref[...], y0_ref[...], y1_ref[...], rt_ref[...], g_ref[...], b_ref[...])


def _combine_kernel_into(x_ref, y0_ref, y1_ref, rt_ref, g_ref, b_ref, full_ref, o_ref):
    del full_ref
    _combine_kernel(x_ref, y0_ref, y1_ref, rt_ref, g_ref, b_ref, o_ref)


def _combine_ln(x2, y01, rt, g3, b3, into=None, row0=0, full_rows=None):
    t_all, d = x2.shape
    tq = TOKEN_TILE
    nt = t_all // tq
    blk0 = row0 // tq if full_rows else 0
    in_specs = [
        pl.BlockSpec((tq, d), lambda i: (i, 0)),
        pl.BlockSpec((tq, d // 2), lambda i: (i, 0)),
        pl.BlockSpec((tq, d // 2), lambda i: (i + nt, 0)),
        pl.BlockSpec((8, tq), lambda i: (0, i)),
        _const_spec((1, d)), _const_spec((1, d)),
    ]
    args = [x2, y01, y01, rt, g3, b3]
    body, aliases = _combine_kernel, {}
    if into is not None:
        in_specs.append(pl.BlockSpec(memory_space=pl.ANY))
        args.append(into)
        body, aliases = _combine_kernel_into, {len(args) - 1: 0}
    return pl.pallas_call(
        body,
        grid=(nt,),
        in_specs=in_specs,
        out_specs=pl.BlockSpec((tq, d), lambda i: (i + blk0, 0)),
        out_shape=jax.ShapeDtypeStruct((full_rows or t_all, d), F32),
        input_output_aliases=aliases,
        compiler_params=_tc_params(1),
        name="combine_ln",
    )(*args)


def _routing_tables(rt, counts):
    bm = EXPERT_ROWS
    t_all = rt.shape[1]
    n_rows = 2 * t_all + N_EXPERTS * bm
    cnt = counts[:, 0]
    padded = (cnt + bm - 1) // bm * bm
    ends = jnp.cumsum(padded)
    offs = ends - padded
    experts = rt[0:2]
    off_tok = jnp.sum(jnp.where(experts[None] == jnp.arange(N_EXPERTS, dtype=I32)[:, None, None],
                                offs[:, None, None], 0), axis=0)
    dest2d = (off_tok + rt[2:4]).reshape(-1, SC_CHUNK).astype(I32)
    blk_start = jnp.arange(n_rows // bm, dtype=I32) * bm
    blk_expert = jnp.minimum(jnp.sum(blk_start[:, None] >= ends[None, :], axis=1), N_EXPERTS - 1).astype(I32)
    live_end = jnp.sum(jnp.where(blk_expert[:, None] == jnp.arange(N_EXPERTS, dtype=I32)[None, :],
                                 (offs + cnt)[None, :], 0), axis=1)
    blk_valid = jnp.clip(live_end - blk_start, 0, bm).astype(I32)
    n_used = (ends[-1:] // bm).astype(I32)
    eid = jnp.arange(N_EXPERTS, dtype=I32)
    later_present = jnp.logical_and(eid[None, :] > eid[:, None], (cnt > 0)[None, :])
    next_present = jnp.min(jnp.where(later_present, eid[None, :], N_EXPERTS), axis=1)
    next_of_blk = jnp.sum(jnp.where(blk_expert[:, None] == eid[None, :], next_present[None, :], 0), axis=1)
    prev_expert = jnp.concatenate([jnp.full((1,), -1, I32), blk_expert[:-1]])
    is_first = jnp.logical_and(blk_start < ends[-1], blk_expert != prev_expert)
    blk_next = jnp.where(is_first, next_of_blk, -1).astype(I32)
    return dest2d, blk_expert, blk_valid, n_used, blk_next, n_rows


def _layer_tail(x1, kv, mem_len, p, router, b0, bsz, s_len):
    rw_hi, rw_lo, rbias = router
    x2, xp, rt, counts = _xattn_route(x1, kv, mem_len, p["xq"], p["xo"], p["ln2_g"], p["ln2_b"],
                                      rw_hi, rw_lo, rbias, b0, bsz, s_len)
    dest2d, blk_expert, blk_valid, n_used, blk_next, n_rows = _routing_tables(rt, counts)
    xs = _sc_dispatch(xp, dest2d, n_rows)
    if "packed" not in p:
        p["packed"] = tuple(_sc_pack_weights(w.reshape(-1, w.shape[2])).reshape(w.shape[0], w.shape[1], -1)
                            for w in (p["e_gate"], p["e_up"], p["e_down"]))
    ys = _expert_ffn(xs, blk_expert, blk_valid, n_used, blk_next, *p["packed"])
    y01 = _sc_combine(ys, dest2d)
    return x2, y01, rt


def _row(v):
    return v.reshape(1, -1).astype(F32)


def _common_params(xq, xkv, xo, ln2_g, ln2_b, e_gate, e_up, e_down, ln3_g, ln3_b):
    return dict(xq=xq, xkv=xkv, xo=xo, ln2_g=_row(ln2_g), ln2_b=_row(ln2_b),
                e_gate=e_gate, e_up=e_up, e_down=e_down, ln3_g=_row(ln3_g), ln3_b=_row(ln3_b))


def kernel(x, mem, positions, router_w, router_bias, l0_w_in, l0_sinks, l0_sgu_ln_g, l0_sgu_ln_b, l0_sgu_w, l0_sgu_b, l0_w_out, l0_ln1_g, l0_ln1_b, l0_xq, l0_xkv, l0_xo, l0_ln2_g, l0_ln2_b, l0_e_gate, l0_e_up, l0_e_down, l0_ln3_g, l0_ln3_b, l1_w_in, l1_pool_w, l1_pool_scale, l1_w_out, l1_ln1_g, l1_ln1_b, l1_xq, l1_xkv, l1_xo, l1_ln2_g, l1_ln2_b, l1_e_gate, l1_e_up, l1_e_down, l1_ln3_g, l1_ln3_b):
    bsz, s_len, d = x.shape
    assert s_len % TOKEN_TILE == 0 and TOKEN_TILE == 2 * SLAB_ROWS and SLAB_ROWS % BLOCK == 0
    xt = x.reshape(bsz * s_len, d)
    mem2d = mem.reshape(-1, d)

    rw_t = router_w.T.astype(F32)
    rw_hi = rw_t.astype(BF16)
    rw_lo = (rw_t - rw_hi.astype(F32)).astype(BF16)
    router = (rw_hi, rw_lo, router_bias.reshape(-1, 1).astype(F32))

    half = ROPE_DIM // 2
    inv_freq = (ROPE_THETA ** (-(jnp.arange(half, dtype=F32) * 2.0 / ROPE_DIM))).reshape(half, 1)
    etab_np = np.zeros((LANES, 3 * LANES), np.float32)
    cbase_np = np.ones((1, LANES), np.float32)
    for ln in range(LANES):
        dd = ln % HEAD_DIM
        if dd < ROPE_DIM:
            cbase_np[0, ln] = 0.0
            etab_np[[dd % half, half + dd % half], ln] = 1.0
            if dd >= half:
                etab_np[[2 * half + dd - half, 3 * half + dd - half], LANES + ln] = 1.0
            else:
                etab_np[[2 * half + dd, 3 * half + dd], 2 * LANES + ln] = -1.0
    etab = jnp.asarray(etab_np, BF16)
    cbase = jnp.asarray(cbase_np)
    pos_row = positions.reshape(1, -1).astype(I32)
    grp = jnp.arange(PIECE_COLS) // B_GROUP_DIM
    gsum = (grp[:, None] == grp[None, :]).astype(BF16)
    bs_full = jnp.repeat(l0_sgu_b.T.astype(F32), B_GROUP_DIM, axis=1)

    p0 = _common_params(l0_xq, l0_xkv, l0_xo, l0_ln2_g, l0_ln2_b, l0_e_gate, l0_e_up, l0_e_down,
                        l0_ln3_g, l0_ln3_b)
    p1 = _common_params(l1_xq, l1_xkv, l1_xo, l1_ln2_g, l1_ln2_b, l1_e_gate, l1_e_up, l1_e_down,
                        l1_ln3_g, l1_ln3_b)
    kv0 = _kv_proj(mem2d, p0["xkv"], bsz)
    kv1 = _kv_proj(mem2d, p1["xkv"], bsz)
    mem_len = mem.shape[1]

    n_split = BATCH_SPLIT if bsz % BATCH_SPLIT == 0 else 1
    nb = bsz // n_split
    out = None
    for part in range(n_split):
        b0 = part * nb
        x1 = _mixer0(xt, pos_row, l0_sinks.astype(F32), l0_w_in, inv_freq, etab, cbase, gsum,
                     _row(l0_sgu_ln_g), _row(l0_sgu_ln_b), l0_sgu_w.astype(F32), bs_full,
                     l0_w_out, _row(l0_ln1_g), _row(l0_ln1_b), b0, nb, s_len)
        x2, y01, rt = _layer_tail(x1, kv0, mem_len, p0, router, b0, nb, s_len)
        x1 = _mixer1(x2, y01, rt, p0["ln3_g"], p0["ln3_b"], l1_w_in, l1_pool_w, _row(l1_pool_scale),
                     l1_w_out, _row(l1_ln1_g), _row(l1_ln1_b), nb, s_len)
        x2, y01, rt = _layer_tail(x1, kv1, mem_len, p1, router, b0, nb, s_len)
        out = _combine_ln(x2, y01, rt, p1["ln3_g"], p1["ln3_b"], into=out, row0=b0 * s_len,
                          full_rows=bsz * s_len)
    return out.reshape(bsz, s_len, d)
```

```python
import functools

import numpy as np
import jax
import jax.numpy as jnp
from jax import lax
from jax.experimental import pallas as pl
from jax.experimental.pallas import tpu as pltpu
from jax.experimental.pallas import tpu_sc as plsc

F32 = jnp.float32
BF16 = jnp.bfloat16
I32 = jnp.int32

DEPTH = 2
ALPHA = (2.0 * DEPTH) ** 0.25
LN_EPS = 1e-5

HEAD_DIM = 64
A_Q_HEADS = 8
A_KV_HEADS = 2
A_GROUP = A_Q_HEADS // A_KV_HEADS
BLOCK = 128
ROPE_THETA = 500000.0
ROPE_DIM = HEAD_DIM // 4
A_WIDTH = A_Q_HEADS * HEAD_DIM
KV_WIDTH = A_KV_HEADS * HEAD_DIM
B_GROUPS = 8
B_GROUP_DIM = 64
B_WIDTH = B_GROUPS * B_GROUP_DIM
POOL_WINDOWS = (2, 4, 8, 16)
POOL_HALO = 16
X_HEADS = 4
N_EXPERTS = 16
N_EXPERT_GROUPS = 4
EXPERTS_PER_GROUP = 4

LANES = 128
TOKEN_TILE = 1024
SLAB_ROWS = 512
PIECE_COLS = 256
EXPERT_ROWS = 512
FFN_GROUP = 2
BATCH_SPLIT = 2
SC_WORKERS = 32
SC_CHUNK = 64
VMEM_LIMIT = 56 * 1024 * 1024
NEG_BIG = -1e30


def _layer_norm(z, g, b):
    mu = jnp.mean(z, axis=-1, keepdims=True)
    d = z - mu
    var = jnp.mean(d * d, axis=-1, keepdims=True)
    return d * lax.rsqrt(var + LN_EPS) * g + b


def _dot(a, b):
    return jnp.dot(a, b, preferred_element_type=F32)


def _dot_nt(a, b):
    return lax.dot_general(a, b, (((1,), (1,)), ((), ())), preferred_element_type=F32)


def _split_bf16(v):
    hi = v.astype(BF16)
    lo = (v - hi.astype(F32)).astype(BF16)
    return hi, lo


def _tc_params(n_axes):
    return pltpu.CompilerParams(dimension_semantics=("arbitrary",) * n_axes,
                                vmem_limit_bytes=VMEM_LIMIT)


def _const_spec(shape):
    nd = len(shape)
    return pl.BlockSpec(shape, lambda *_: (0,) * nd, pipeline_mode=pl.Buffered(1))


def _mixer0_kernel(sinks_ref, x_ref, pos_ref, win_ref, invf_ref, etab_ref, cbase_ref, gsum_ref,
                   lng_ref, lnb_ref, ws_ref, bs_ref, wout_ref, g1_ref, b1_ref,
                   o_ref, q_s, kv_s, u_s, vn_s, mix_s, wt_s, win_s, wout_s):
    b = pl.program_id(0)
    j = pl.program_id(1)
    tq = x_ref.shape[0]
    kvw = kv_s.shape[1]

    @pl.when(jnp.logical_and(b == 0, j == 0))
    def _():
        win_s[...] = win_ref[...].astype(BF16)
        wout_s[...] = wout_ref[...].astype(BF16)
        r = lax.broadcasted_iota(I32, (BLOCK, BLOCK), 0)
        c = lax.broadcasted_iota(I32, (BLOCK, BLOCK), 1)
        for g in range(B_GROUPS):
            wt_s[g] = jnp.where(c <= r, ws_ref[g], 0.0).astype(BF16)

    @pl.when(j == 0)
    def _():
        kv_s[0:BLOCK, :] = jnp.zeros((BLOCK, kvw), BF16)

    c1 = A_WIDTH
    c2 = c1 + KV_WIDTH
    c3 = c2 + KV_WIDTH
    c4 = c3 + B_WIDTH

    def rotary_tables(rows):
        n = rows.stop - rows.start
        ang = invf_ref[...] * pos_ref[:, rows].astype(F32)
        c8 = jnp.cos(ang)
        s8 = jnp.sin(ang)
        c8h = c8.astype(BF16).astype(F32)
        s8h = s8.astype(BF16).astype(F32)
        pad = jnp.zeros((LANES - 4 * c8.shape[0], n), F32)
        stack = jnp.concatenate([c8h, c8 - c8h, s8h, s8 - s8h, pad], axis=0)
        tabs = _dot(stack.T.astype(BF16), etab_ref[...])
        return tabs[:, 0:LANES] + cbase_ref[...], tabs[:, LANES:2 * LANES], tabs[:, 2 * LANES:]

    pw = PIECE_COLS
    n_pieces = win_s.shape[1] // pw
    assert (c1 // pw, c2 // pw, c3 // pw, c4 // pw) == (2, 2, 3, 5) and c3 % pw == 0 and n_pieces == 7

    def project(xb, k):
        return _dot(xb, win_s[:, k * pw:(k + 1) * pw])

    def prepare(rows, hk, tables, k):
        n = rows.stop - rows.start
        cs, sa, sb = tables

        def rope(t):
            return t * cs + pltpu.roll(t, ROPE_DIM // 2, 1) * sa + pltpu.roll(t, LANES - ROPE_DIM // 2, 1) * sb

        if k < 2:
            for c in range(pw // LANES):
                t = hk[:, c * LANES:(c + 1) * LANES] * (HEAD_DIM ** -0.5)
                col = k * (pw // LANES) + c
                q_s[rows, col * LANES:(col + 1) * LANES] = rope(t).astype(BF16)
        elif k == 2:
            low = lax.broadcasted_iota(I32, (n, LANES), 1) < HEAD_DIM
            kr = rope(hk[:, 0:KV_WIDTH])
            kx = pltpu.roll(kr, HEAD_DIM, 1)
            vr = hk[:, KV_WIDTH:]
            vx = pltpu.roll(vr, HEAD_DIM, 1)
            kv_cols = [jnp.where(low, kr, kx), jnp.where(low, kx, kr),
                       jnp.where(low, vr, 0.0), jnp.where(low, 0.0, vx),
                       jnp.where(low, vx, 0.0), jnp.where(low, 0.0, vr)]
            for c, col in enumerate(kv_cols):
                kv_s[BLOCK + rows.start:BLOCK + rows.stop, c * LANES:(c + 1) * LANES] = col.astype(BF16)
        elif k < 5:
            lo = (k - 3) * pw
            u_s[rows, lo:lo + pw] = jax.nn.gelu(hk)
        else:
            lo = (k - 5) * pw
            v = jax.nn.gelu(hk)
            gsum = gsum_ref[...]
            mean = _dot(v.astype(BF16), gsum) * (1.0 / B_GROUP_DIM)
            d = v - mean
            var = _dot((d * d).astype(BF16), gsum) * (1.0 / B_GROUP_DIM)
            vn_s[rows, lo:lo + pw] = (d * lax.rsqrt(var + LN_EPS) * lng_ref[:, lo:lo + pw]
                                      + lnb_ref[:, lo:lo + pw]).astype(BF16)

    qi = lax.broadcasted_iota(I32, (BLOCK, 2 * BLOCK), 0)
    kj = lax.broadcasted_iota(I32, (BLOCK, 2 * BLOCK), 1)
    rel = qi + BLOCK - kj
    band = jnp.logical_and(rel >= 0, rel < BLOCK)
    low_q = lax.broadcasted_iota(I32, (BLOCK, LANES), 1) < HEAD_DIM
    low_k = lax.broadcasted_iota(I32, (2 * BLOCK, LANES), 1) < HEAD_DIM
    ones_lo = jnp.where(low_k, 1.0, 0.0).astype(BF16)
    ones_hi = jnp.where(low_k, 0.0, 1.0).astype(BF16)
    zero_q = jnp.zeros((BLOCK, LANES), BF16)

    def block_body(n):
        r0 = n * BLOCK
        kv = kv_s[pl.ds(r0, 2 * BLOCK), :]
        qb = q_s[pl.ds(r0, BLOCK), :]
        valid = jnp.logical_and(band, kj >= jnp.where(j == 0, BLOCK, 0)) if n == 0 else band
        cols_per_kv = A_GROUP // 2
        scores = {}
        for hk in range(A_KV_HEADS):
            cols = range(hk * cols_per_kv, (hk + 1) * cols_per_kv)
            pieces = []
            for c in cols:
                qp = qb[:, c * LANES:(c + 1) * LANES]
                pieces += [jnp.where(low_q, qp, zero_q), jnp.where(low_q, zero_q, qp)]
            sc = _dot_nt(jnp.concatenate(pieces, axis=0), kv[:, hk * LANES:(hk + 1) * LANES])
            for i, c in enumerate(cols):
                for half in range(2):
                    r = (2 * i + half) * BLOCK
                    scores[c, half] = sc[r:r + BLOCK, :]
        vnb = vn_s[pl.ds(r0, BLOCK), :]
        parts = []
        for c in range(B_WIDTH // LANES):
            vp = vnb[:, c * LANES:(c + 1) * LANES]
            parts.append(_dot(wt_s[2 * c], jnp.where(low_q, vp, zero_q))
                         + _dot(wt_s[2 * c + 1], jnp.where(low_q, zero_q, vp)))
        probs, esink = {}, {}
        for (c, half), sc in scores.items():
            s = jnp.where(valid, sc, NEG_BIG)
            sink = sinks_ref[2 * c + half]
            m = jnp.maximum(jnp.max(s, axis=-1, keepdims=True), sink)
            probs[c, half] = jnp.exp(s - m).astype(BF16)
            esink[c, half] = jnp.exp(sink - m)
        res = {}
        for hk in range(A_KV_HEADS):
            cols = range(hk * cols_per_kv, (hk + 1) * cols_per_kv)
            for half in range(2):
                vcol = kv[:, (2 + 2 * hk + half) * LANES:(3 + 2 * hk + half) * LANES]
                vm = jnp.concatenate([vcol, ones_lo if half == 0 else ones_hi], axis=1)
                pv = _dot(jnp.concatenate([probs[c, half] for c in cols], axis=0), vm)
                for i, c in enumerate(cols):
                    part = pv[i * BLOCK:(i + 1) * BLOCK, :]
                    res[c] = part if half == 0 else res[c] + part
        for c in range(A_WIDTH // LANES):
            den = res[c][:, LANES:] + jnp.where(low_q, esink[c, 0], esink[c, 1])
            mix_s[pl.ds(r0, BLOCK), c * LANES:(c + 1) * LANES] = (res[c][:, :LANES] / den).astype(BF16)
        mixed = jnp.concatenate(parts, axis=1) + bs_ref[...]
        mix_s[pl.ds(r0, BLOCK), A_WIDTH:] = (u_s[pl.ds(r0, BLOCK), :] * mixed).astype(BF16)

    def out_cols(rows, c):
        return _dot(mix_s[rows, :], wout_s[:, c * pw:(c + 1) * pw])

    def finish(rows, z_cols, i):
        r = slice(rows.start + i * BLOCK, rows.start + (i + 1) * BLOCK)
        z = ALPHA * x_ref[r, :] + jnp.concatenate([zc[i * BLOCK:(i + 1) * BLOCK, :] for zc in z_cols], axis=1)
        o_ref[r, :] = _layer_norm(z, g1_ref[...], b1_ref[...])

    assert tq == 2 * SLAB_ROWS and SLAB_ROWS == 4 * BLOCK and wout_s.shape[1] == 4 * pw
    sa, sb = slice(0, SLAB_ROWS), slice(SLAB_ROWS, tq)
    tab_a, tab_b = rotary_tables(sa), rotary_tables(sb)
    xa = x_ref[sa, :].astype(BF16)
    ha = [project(xa, k) for k in range(n_pieces)]
    xb = x_ref[sb, :].astype(BF16)
    hb = []
    for k in range(n_pieces):
        hb.append(project(xb, k))
        prepare(sa, ha[k], tab_a, k)
    pieces_b = iter(range(n_pieces))
    for n in range(4):
        block_body(n)
        for k in [next(pieces_b) for _ in range(2 if n < 3 else 1)]:
            prepare(sb, hb[k], tab_b, k)
    za = []
    for n in range(4):
        block_body(4 + n)
        za.append(out_cols(sa, n))
    kv_s[0:BLOCK, :] = kv_s[tq:tq + BLOCK, :]
    zb = []
    for n in range(4):
        zb.append(out_cols(sb, n))
        finish(sa, za, n)
    for n in range(4):
        finish(sb, zb, n)


def _mixer0(x, pos_row, sinks, w_in, invf, etab, cbase, gsum, lng, lnb, w_s, bs_full, w_out, g1, b1,
            b0, bsz, s_len):
    d = x.shape[1]
    t_all = bsz * s_len
    tq = TOKEN_TILE
    nj = s_len // tq
    row = lambda bb, jj: (bb * nj + jj, 0)
    in_w = w_in.shape[1]
    return pl.pallas_call(
        _mixer0_kernel,
        grid=(bsz, nj),
        in_specs=[
            pl.BlockSpec(memory_space=pltpu.SMEM),
            pl.BlockSpec((tq, d), lambda bb, jj: ((b0 + bb) * nj + jj, 0)),
            pl.BlockSpec((1, tq), lambda bb, jj: (0, (b0 + bb) * nj + jj)),
            _const_spec((d, in_w)),
            _const_spec((ROPE_DIM // 2, 1)), _const_spec((LANES, 3 * LANES)), _const_spec((1, LANES)),
            _const_spec((PIECE_COLS, PIECE_COLS)),
            _const_spec((1, B_WIDTH)), _const_spec((1, B_WIDTH)),
            _const_spec((B_GROUPS, BLOCK, BLOCK)),
            _const_spec((BLOCK, B_WIDTH)),
            _const_spec((A_WIDTH + B_WIDTH, d)),
            _const_spec((1, d)), _const_spec((1, d)),
        ],
        out_specs=pl.BlockSpec((tq, d), row),
        out_shape=jax.ShapeDtypeStruct((t_all, d), F32),
        scratch_shapes=[
            pltpu.VMEM((tq, A_WIDTH), BF16),
            pltpu.VMEM((tq + BLOCK, 6 * LANES), BF16),
            pltpu.VMEM((tq, B_WIDTH), F32),
            pltpu.VMEM((tq, B_WIDTH), BF16),
            pltpu.VMEM((tq, A_WIDTH + B_WIDTH), BF16),
            pltpu.VMEM((B_GROUPS, BLOCK, BLOCK), BF16),
            pltpu.VMEM((d, in_w), BF16),
            pltpu.VMEM((A_WIDTH + B_WIDTH, d), BF16),
        ],
        compiler_params=_tc_params(2),
        name="mixer0",
    )(sinks, x, pos_row, w_in, invf, etab, cbase, gsum, lng, lnb, w_s, bs_full, w_out, g1, b1)


def _expert_combine_ln(x2, y0_packed, y1_packed, rt, g, b):
    wt = pltpu.bitcast(rt, F32).T
    y = wt[:, 4:5] * _unpack_bf16_pairs(y0_packed) + wt[:, 5:6] * _unpack_bf16_pairs(y1_packed)
    return _layer_norm(ALPHA * x2 + y, g, b)


def _mixer1_kernel(x2_ref, y0_ref, y1_ref, rt_ref, g3_ref, b3_ref,
                   win_ref, pw_ref, ps_ref, wout_ref, g1_ref, b1_ref, o_ref,
                   h_s, mp_s, win_s, pw_s, wout_s):
    j = pl.program_id(1)
    tq = x2_ref.shape[0]
    gw = x2_ref.shape[1] // len(POOL_WINDOWS)
    slabs = [slice(c * SLAB_ROWS, (c + 1) * SLAB_ROWS) for c in range(tq // SLAB_ROWS)]

    @pl.when(jnp.logical_and(pl.program_id(0) == 0, j == 0))
    def _():
        win_s[...] = win_ref[...].astype(BF16)
        pw_s[...] = pw_ref[...].astype(BF16)
        wout_s[...] = wout_ref[...].astype(BF16)

    @pl.when(j == 0)
    def _():
        h_s[0:POOL_HALO, :] = jnp.zeros((POOL_HALO, h_s.shape[1]), F32)

    ng = len(POOL_WINDOWS)
    quarter = SLAB_ROWS // ng

    def sub(rows, i):
        return slice(rows.start + i * quarter, rows.start + (i + 1) * quarter)

    def load_in(rows, i):
        r = sub(rows, i)
        return _expert_combine_ln(x2_ref[r, :], y0_ref[r, :], y1_ref[r, :], rt_ref[:, r],
                                  g3_ref[...], b3_ref[...])

    def project(rows, xb, g):
        lo, hi = g * gw, (g + 1) * gw
        h_s[POOL_HALO + rows.start:POOL_HALO + rows.stop, lo:hi] = _dot(xb, win_s[:, lo:hi])

    def pool(rows, g):
        win = POOL_WINDOWS[g]
        lo, hi = g * gw, (g + 1) * gw
        n = rows.stop - rows.start
        t_pos = j * tq + rows.start + lax.broadcasted_iota(I32, (n, 1), 0)
        ext = h_s[rows.start:rows.stop + POOL_HALO, lo:hi]
        acc = ext
        shift = 1
        while shift < win:
            acc = acc + pltpu.roll(acc, shift, 0)
            shift *= 2
        count = jnp.minimum(t_pos + 1, win).astype(F32)
        pooled = acc[POOL_HALO:, :] / count - ext[POOL_HALO:, :]
        mapped = _dot(pooled.astype(BF16), pw_s[g])
        mp_s[rows, lo:hi] = (mapped * ps_ref[:, lo:hi]).astype(BF16)

    def out_cols(rows, g):
        return _dot(mp_s[rows, :], wout_s[:, g * gw:(g + 1) * gw])

    def finish(rows, x_parts, z_cols, i):
        lo, hi = i * quarter, (i + 1) * quarter
        z = ALPHA * x_parts[i] + jnp.concatenate([zc[lo:hi, :] for zc in z_cols], axis=1)
        o_ref[sub(rows, i), :] = _layer_norm(z, g1_ref[...], b1_ref[...])

    assert len(slabs) == 2
    sa, sb = slabs
    xa = [load_in(sa, i) for i in range(ng)]
    xa_b = jnp.concatenate(xa, axis=0).astype(BF16)
    xb = []
    for g in range(ng):
        project(sa, xa_b, g)
        xb.append(load_in(sb, g))
    xb_b = jnp.concatenate(xb, axis=0).astype(BF16)
    for g in range(ng):
        project(sb, xb_b, g)
        pool(sa, g)
    za = []
    for g in range(ng):
        za.append(out_cols(sa, g))
        pool(sb, g)
    zb = []
    for g in range(ng):
        zb.append(out_cols(sb, g))
        finish(sa, xa, za, g)
    for g in range(ng):
        finish(sb, xb, zb, g)
    h_s[0:POOL_HALO, :] = h_s[tq:tq + POOL_HALO, :]


def _mixer1(x2, y01, rt, g3, b3, w_in, pool_w, pool_scale, w_out, g1, b1, bsz, s_len):
    t_all, d = x2.shape
    tq = TOKEN_TILE
    nj = s_len // tq
    nt = bsz * nj
    row = lambda bb, jj: (bb * nj + jj, 0)
    ng = len(POOL_WINDOWS)
    return pl.pallas_call(
        _mixer1_kernel,
        grid=(bsz, nj),
        in_specs=[
            pl.BlockSpec((tq, d), row),
            pl.BlockSpec((tq, d // 2), row),
            pl.BlockSpec((tq, d // 2), lambda bb, jj: (nt + bb * nj + jj, 0)),
            pl.BlockSpec((8, tq), lambda bb, jj: (0, bb * nj + jj)),
            _const_spec((1, d)), _const_spec((1, d)),
            _const_spec((d, d)),
            _const_spec((ng, d // ng, d // ng)),
            _const_spec((1, d)),
            _const_spec((d, d)),
            _const_spec((1, d)), _const_spec((1, d)),
        ],
        out_specs=pl.BlockSpec((tq, d), row),
        out_shape=jax.ShapeDtypeStruct((t_all, d), F32),
        scratch_shapes=[pltpu.VMEM((tq + POOL_HALO, d), F32), pltpu.VMEM((tq, d), BF16),
                        pltpu.VMEM((d, d), BF16), pltpu.VMEM((ng, d // ng, d // ng), BF16),
                        pltpu.VMEM((d, d), BF16)],
        compiler_params=_tc_params(2),
        name="mixer1",
    )(x2, y01, y01, rt, g3, b3, w_in, pool_w, pool_scale, w_out, g1, b1)


def _kv_kernel(mem_ref, w_ref, o_ref, w_s):
    @pl.when(pl.program_id(0) == 0)
    def _():
        w_s[...] = w_ref[...].astype(BF16)

    o_ref[...] = _dot(mem_ref[...].astype(BF16), w_s[...]).astype(BF16)


def _kv_proj(mem2d, wkv, bsz):
    rows, d = mem2d.shape
    m = rows // bsz
    return pl.pallas_call(
        _kv_kernel,
        grid=(bsz,),
        in_specs=[pl.BlockSpec((m, d), lambda i: (i, 0)), _const_spec(wkv.shape)],
        out_specs=pl.BlockSpec((m, wkv.shape[1]), lambda i: (i, 0)),
        out_shape=jax.ShapeDtypeStruct((rows, wkv.shape[1]), BF16),
        scratch_shapes=[pltpu.VMEM(wkv.shape, BF16)],
        compiler_params=_tc_params(1),
        name="kv_proj",
    )(mem2d, wkv)


def _top2_of4(v):
    hi01, lo01 = jnp.maximum(v[0], v[1]), jnp.minimum(v[0], v[1])
    hi23, lo23 = jnp.maximum(v[2], v[3]), jnp.minimum(v[2], v[3])
    return jnp.maximum(hi01, hi23) + jnp.maximum(jnp.minimum(hi01, hi23), jnp.maximum(lo01, lo23))


def _argmax_first(vals):
    best, idx = vals[0], jnp.zeros(vals[0].shape, I32)
    for i in range(1, len(vals)):
        better = vals[i] > best
        best = jnp.where(better, vals[i], best)
        idx = jnp.where(better, i, idx)
    return best, idx


def _xattn_kernel(x_ref, kv_ref, wq_ref, wo_ref, g2_ref, b2_ref, rwh_ref, rwl_ref, rb_ref,
                  x2_ref, xp_ref, rt_ref, cnt_ref, run_s, wq_s, wo_s):
    first = jnp.logical_and(pl.program_id(0) == 0, pl.program_id(1) == 0)
    tq, d = x_ref.shape

    @pl.when(first)
    def _():
        run_s[...] = jnp.zeros(run_s.shape, F32)
        wq_s[...] = wq_ref[...].astype(BF16)
        wo_s[...] = wo_ref[...].astype(BF16)

    slabs = [slice(c * SLAB_ROWS, (c + 1) * SLAB_ROWS) for c in range(tq // SLAB_ROWS)]
    run = run_s[...]
    zs = [_xattn_attend(*_xattn_query(rows, x_ref, wq_s), kv_ref, wo_s) for rows in slabs]
    picks = []
    for rows, z in zip(slabs, zs):
        logits = _xattn_norm_logits(rows, z, g2_ref, b2_ref, rwh_ref, rwl_ref, x2_ref, xp_ref)
        picks.append(_xattn_select(logits, rb_ref))
    for rows, pick in zip(slabs, picks):
        run = _xattn_rank(rows, pick, run, rt_ref)
    run_s[...] = run
    cnt_ref[...] = jnp.broadcast_to(run, cnt_ref.shape).astype(I32)


def _xattn_query(rows, x_ref, wq_s):
    hd = x_ref.shape[1] // X_HEADS
    x = x_ref[rows, :]
    return x, _dot(x.astype(BF16), wq_s[...]) * (hd ** -0.5)


def _xattn_attend(x, q, kv_ref, wo_s):
    d = x.shape[1]
    hd = d // X_HEADS
    outs = []
    for hx in range(X_HEADS):
        qh = q[:, hx * hd:(hx + 1) * hd].astype(BF16)
        kh = kv_ref[:, hx * hd:(hx + 1) * hd]
        vh = kv_ref[:, d + hx * hd:d + (hx + 1) * hd]
        s = _dot_nt(qh, kh)
        p = jnp.exp(s - jnp.max(s, axis=-1, keepdims=True))
        o = _dot(p.astype(BF16), vh) / jnp.sum(p, axis=-1, keepdims=True)
        outs.append(o.astype(BF16))
    return ALPHA * x + _dot(jnp.concatenate(outs, axis=1), wo_s[...])


def _xattn_norm_logits(rows, z, g2_ref, b2_ref, rwh_ref, rwl_ref, x2_ref, xp_ref):
    x2 = _layer_norm(z, g2_ref[...], b2_ref[...])
    x2_ref[rows, :] = x2

    xp_ref[rows, :] = _pack_bf16_pairs(x2)

    xh, xl = _split_bf16(x2)
    both = _dot_nt(jnp.concatenate([rwh_ref[...], rwl_ref[...]], axis=0), xh)
    return both[0:N_EXPERTS] + both[N_EXPERTS:] + _dot_nt(rwh_ref[...], xl)


def _xattn_select(logits, rb_ref):
    e_max = jnp.max(logits, axis=0, keepdims=True)
    ex = jnp.exp(logits - e_max)
    scores = ex / jnp.sum(ex, axis=0, keepdims=True)
    biased = scores + rb_ref[...]
    sc = [scores[e:e + 1, :] for e in range(N_EXPERTS)]
    bi = [biased[e:e + 1, :] for e in range(N_EXPERTS)]
    epg = EXPERTS_PER_GROUP
    gscore = [_top2_of4(bi[g * epg:(g + 1) * epg]) for g in range(N_EXPERT_GROUPS)]
    _, gsel = _argmax_first(gscore)

    def pick(vals):
        return [functools.reduce(lambda acc, g: jnp.where(gsel == g, vals[g * epg + i], acc),
                                 range(1, N_EXPERT_GROUPS), vals[i]) for i in range(epg)]

    in_b = pick(bi)
    in_s = pick(sc)
    _, i0 = _argmax_first(in_b)
    _, i1 = _argmax_first([jnp.where(i0 == i, -jnp.inf, in_b[i]) for i in range(epg)])

    def take(vals, idx):
        return functools.reduce(lambda acc, i: jnp.where(idx == i, vals[i], acc), range(1, epg), vals[0])

    s0, s1 = take(in_s, i0), take(in_s, i1)
    w0, w1 = s0 / (s0 + s1), s1 / (s0 + s1)
    return gsel * epg + i0, gsel * epg + i1, w0, w1


def _xattn_rank(rows, selection, run, rt_ref):
    e0, e1, w0, w1 = selection
    tq = rows.stop - rows.start
    eid = lax.broadcasted_iota(I32, (N_EXPERTS, tq), 0)
    oh0 = eid == e0
    oh1 = eid == e1
    onehot = jnp.where(jnp.logical_or(oh0, oh1), 1.0, 0.0)
    rr = lax.broadcasted_iota(I32, (tq, tq), 0)
    cc = lax.broadcasted_iota(I32, (tq, tq), 1)
    upper = jnp.where(rr < cc, 1.0, 0.0).astype(BF16)
    prefix = _dot(onehot.astype(BF16), upper) + run
    r0 = jnp.sum(jnp.where(oh0, prefix, 0.0), axis=0, keepdims=True).astype(I32)
    r1 = jnp.sum(jnp.where(oh1, prefix, 0.0), axis=0, keepdims=True).astype(I32)

    zero = jnp.zeros((1, tq), I32)
    rt_ref[:, rows] = jnp.concatenate(
        [e0, e1, r0, r1, pltpu.bitcast(w0, I32), pltpu.bitcast(w1, I32), zero, zero], axis=0)
    return run + jnp.sum(onehot, axis=1, keepdims=True)


def _xattn_route(x1, kv, mem_len, wq, wo, g2, b2, rw_hi, rw_lo, rbias, b0, bsz, s_len):
    t_all, d = x1.shape
    tq = TOKEN_TILE
    nj = s_len // tq
    m = mem_len
    row = lambda bb, jj: (bb * nj + jj, 0)
    return pl.pallas_call(
        _xattn_kernel,
        grid=(bsz, nj),
        in_specs=[
            pl.BlockSpec((tq, d), row),
            pl.BlockSpec((m, 2 * d), lambda bb, jj: (b0 + bb, 0)),
            _const_spec((d, d)), _const_spec((d, d)),
            _const_spec((1, d)), _const_spec((1, d)),
            _const_spec((N_EXPERTS, d)), _const_spec((N_EXPERTS, d)),
            _const_spec((N_EXPERTS, 1)),
        ],
        out_specs=[
            pl.BlockSpec((tq, d), row),
            pl.BlockSpec((tq, d // 2), row),
            pl.BlockSpec((8, tq), lambda bb, jj: (0, bb * nj + jj)),
            _const_spec((N_EXPERTS, LANES)),
        ],
        out_shape=[
            jax.ShapeDtypeStruct((t_all, d), F32),
            jax.ShapeDtypeStruct((t_all, d // 2), I32),
            jax.ShapeDtypeStruct((8, t_all), I32),
            jax.ShapeDtypeStruct((N_EXPERTS, LANES), I32),
        ],
        scratch_shapes=[pltpu.VMEM((N_EXPERTS, 1), F32), pltpu.VMEM((d, d), BF16), pltpu.VMEM((d, d), BF16)],
        compiler_params=_tc_params(2),
        name="xattn_route",
    )(x1, kv, wq, wo, g2, b2, rw_hi, rw_lo, rbias)


def _sc_mesh():
    return plsc.VectorSubcoreMesh(core_axis_name="c", subcore_axis_name="s")


def _sc_params():
    return pltpu.CompilerParams(needs_layout_passes=False)


def _worker_id():
    return lax.axis_index("s") * lax.axis_size("c") + lax.axis_index("c")


def _sc_dispatch(xp, dest2d, n_rows):
    t_all, width = xp.shape
    chunk = dest2d.shape[1]
    tok_w = t_all // SC_WORKERS
    nch = tok_w // chunk
    slot1 = t_all // chunk
    assert t_all % (SC_WORKERS * chunk * 2) == 0

    def body(x_hbm, dest_hbm, out_hbm, idx0_v, idx1_v, buf0, buf1, sem_r, sem_w):
        wid = _worker_id()
        base = wid * tok_w
        pltpu.sync_copy(dest_hbm.at[pl.ds(wid * nch, nch)], idx0_v)
        pltpu.sync_copy(dest_hbm.at[pl.ds(slot1 + wid * nch, nch)], idx1_v)

        def read(c, buf, k):
            return pltpu.make_async_copy(x_hbm.at[pl.ds(base + c * chunk, chunk)], buf, sem_r.at[k])

        def scatter(c, buf):
            a = pltpu.make_async_copy(buf, out_hbm.at[idx0_v.at[c]], sem_w.at[0])
            b = pltpu.make_async_copy(buf, out_hbm.at[idx1_v.at[c]], sem_w.at[1])
            a.start()
            b.start()
            a.wait()
            b.wait()

        read(0, buf0, 0).start()

        @pl.loop(0, nch // 2)
        def _(g):
            c = 2 * g
            read(c + 1, buf1, 1).start()
            read(c, buf0, 0).wait()
            scatter(c, buf0)

            @pl.when(c + 2 < nch)
            def _():
                read(c + 2, buf0, 0).start()

            read(c + 1, buf1, 1).wait()
            scatter(c + 1, buf1)

    return pl.kernel(
        body,
        out_type=jax.ShapeDtypeStruct((n_rows, width), xp.dtype),
        mesh=_sc_mesh(),
        scratch_types=[
            pltpu.VMEM((nch, chunk), I32),
            pltpu.VMEM((nch, chunk), I32),
            pltpu.VMEM((chunk, width), xp.dtype),
            pltpu.VMEM((chunk, width), xp.dtype),
            pltpu.SemaphoreType.DMA((2,)),
            pltpu.SemaphoreType.DMA((2,)),
        ],
        compiler_params=_sc_params(),
        name="sc_dispatch",
    )(xp, dest2d)


def _sc_pack_weights(w2d):
    n_rows, width = w2d.shape
    half = width // 2
    rows_w = n_rows // SC_WORKERS
    chunk = (32 * 1024) // width
    lanes = 16
    assert n_rows % (SC_WORKERS * chunk) == 0 and half % lanes == 0

    def body(w_hbm, out_hbm, in_v, out_v):
        base = _worker_id() * rows_w

        def rne_hi16(bits):
            return bits + 0x7FFF + jnp.bitwise_and(lax.shift_right_logical(bits, 16), 1)

        @pl.loop(0, rows_w // chunk)
        def _(c):
            r0 = base + c * chunk
            pltpu.sync_copy(w_hbm.at[pl.ds(r0, chunk)], in_v)

            @pl.loop(0, chunk)
            def _(r):
                @plsc.parallel_loop(0, half, step=lanes, unroll=4)
                def _(j):
                    lo = rne_hi16(plsc.bitcast(in_v[r, pl.ds(j, lanes)], I32))
                    hi = rne_hi16(plsc.bitcast(in_v[r, pl.ds(half + j, lanes)], I32))
                    out_v[r, pl.ds(j, lanes)] = jnp.bitwise_or(
                        lax.shift_right_logical(lo, 16), jnp.bitwise_and(hi, jnp.int32(-65536)))

            pltpu.sync_copy(out_v, out_hbm.at[pl.ds(r0, chunk)])

    return pl.kernel(
        body,
        out_type=jax.ShapeDtypeStruct((n_rows, half), I32),
        mesh=_sc_mesh(),
        scratch_types=[pltpu.VMEM((chunk, width), F32), pltpu.VMEM((chunk, half), I32)],
        compiler_params=_sc_params(),
        name="sc_pack_weights",
    )(w2d)


def _sc_combine(ys, dest2d):
    n_idx_rows, chunk = dest2d.shape
    width = ys.shape[1]
    nch = n_idx_rows // SC_WORKERS
    assert n_idx_rows % (SC_WORKERS * 2) == 0

    def body(y_hbm, dest_hbm, out_hbm, idx_v, buf0, buf1, sem_g):
        wid = _worker_id()
        base = wid * nch * chunk
        pltpu.sync_copy(dest_hbm.at[pl.ds(wid * nch, nch)], idx_v)

        def gather(c, buf, k):
            return pltpu.make_async_copy(y_hbm.at[idx_v.at[c]], buf, sem_g.at[k])

        def write(c, buf):
            pltpu.sync_copy(buf, out_hbm.at[pl.ds(base + c * chunk, chunk)])

        gather(0, buf0, 0).start()

        @pl.loop(0, nch // 2)
        def _(g):
            c = 2 * g
            gather(c + 1, buf1, 1).start()
            gather(c, buf0, 0).wait()
            write(c, buf0)

            @pl.when(c + 2 < nch)
            def _():
                gather(c + 2, buf0, 0).start()

            gather(c + 1, buf1, 1).wait()
            write(c + 1, buf1)

    return pl.kernel(
        body,
        out_type=jax.ShapeDtypeStruct((n_idx_rows * chunk, width), ys.dtype),
        mesh=_sc_mesh(),
        scratch_types=[
            pltpu.VMEM((nch, chunk), I32),
            pltpu.VMEM((chunk, width), ys.dtype),
            pltpu.VMEM((chunk, width), ys.dtype),
            pltpu.SemaphoreType.DMA((2,)),
        ],
        compiler_params=_sc_params(),
        name="sc_combine",
    )(ys, dest2d)


def _pack_bf16_pairs(v):
    half = v.shape[1] // 2
    lo = pltpu.bitcast(v[:, :half].astype(BF16).astype(F32), jnp.uint32) >> 16
    hi = pltpu.bitcast(v[:, half:].astype(BF16).astype(F32), jnp.uint32) & jnp.uint32(0xFFFF0000)
    return pltpu.bitcast(hi | lo, I32)


def _unpack_bf16_pairs(w):
    w = pltpu.bitcast(w, jnp.uint32)
    lo = pltpu.bitcast(w << 16, F32)
    hi = pltpu.bitcast(w & jnp.uint32(0xFFFF0000), F32)
    return jnp.concatenate([lo, hi], axis=1)


def _ffn_kernel(be_ref, nv_ref, nu_ref, nxt_ref, xs_ref, wg_hbm, wu_hbm, wd_hbm, o_ref,
                wg_f, wu_f, wd_f, wg_s, wu_s, wd_s, sem):
    bm = EXPERT_ROWS

    def fetch(e):
        return (pltpu.make_async_copy(wg_hbm.at[e], wg_f, sem.at[0]),
                pltpu.make_async_copy(wu_hbm.at[e], wu_f, sem.at[1]),
                pltpu.make_async_copy(wd_hbm.at[e], wd_f, sem.at[2]))

    for sub in range(FFN_GROUP):
        i = pl.program_id(0) * FFN_GROUP + sub
        rows = slice(sub * bm, (sub + 1) * bm)

        @pl.when(nxt_ref[i] >= 0)
        def _():
            @pl.when(i == 0)
            def _():
                for cp in fetch(be_ref[0]):
                    cp.start()

            for cp in fetch(be_ref[i]):
                cp.wait()
            wg_s[...] = _unpack_bf16_pairs(wg_f[...]).astype(BF16)
            wu_s[...] = _unpack_bf16_pairs(wu_f[...]).astype(BF16)
            wd_s[...] = _unpack_bf16_pairs(wd_f[...]).astype(BF16)

            @pl.when(nxt_ref[i] < N_EXPERTS)
            def _():
                for cp in fetch(nxt_ref[i]):
                    cp.start()

        @pl.when(i < nu_ref[0])
        def _():
            live = lax.broadcasted_iota(I32, (bm, xs_ref.shape[1]), 0) < nv_ref[i]
            xb = _unpack_bf16_pairs(jnp.where(live, xs_ref[rows, :], 0)).astype(BF16)
            y = None
            for c in range(wg_s.shape[1] // PIECE_COLS):
                cols = slice(c * PIECE_COLS, (c + 1) * PIECE_COLS)
                act = jax.nn.silu(_dot(xb, wg_s[:, cols])) * _dot(xb, wu_s[:, cols])
                part = _dot(act.astype(BF16), wd_s[cols, :])
                y = part if y is None else y + part
            o_ref[rows, :] = _pack_bf16_pairs(y)


def _expert_ffn(xs, blk_expert, blk_valid, n_used, blk_next, w_gate, w_up, w_down):
    n_rows, half = xs.shape
    d = 2 * half
    de = 2 * w_gate.shape[2]
    win = FFN_GROUP * EXPERT_ROWS
    assert n_rows % win == 0
    rows = lambda s, be, nv, nu, nx: (jnp.minimum(s, (nu[0] - 1) // FFN_GROUP), 0)
    hbm = pl.BlockSpec(memory_space=pl.ANY)
    return pl.pallas_call(
        _ffn_kernel,
        grid_spec=pltpu.PrefetchScalarGridSpec(
            num_scalar_prefetch=4,
            grid=(n_rows // win,),
            in_specs=[pl.BlockSpec((win, half), rows), hbm, hbm, hbm],
            out_specs=pl.BlockSpec((win, half), rows),
            scratch_shapes=[
                pltpu.VMEM((d, de // 2), I32), pltpu.VMEM((d, de // 2), I32), pltpu.VMEM((de, d // 2), I32),
                pltpu.VMEM((d, de), BF16), pltpu.VMEM((d, de), BF16), pltpu.VMEM((de, d), BF16),
                pltpu.SemaphoreType.DMA((3,)),
            ],
        ),
        out_shape=jax.ShapeDtypeStruct((n_rows, half), I32),
        compiler_params=_tc_params(1),
        name="expert_ffn",
    )(blk_expert, blk_valid, n_used, blk_next, xs, w_gate, w_up, w_down)


def _combine_kernel(x_ref, y0_ref, y1_ref, rt_ref, g_ref, b_ref, o_ref):
    o_ref[...] = _expert_combine_ln(x_ref[...], y0_ref[...], y1_ref[...], rt_ref[...], g_ref[...], b_ref[...])


def _combine_kernel_into(x_ref, y0_ref, y1_ref, rt_ref, g_ref, b_ref, full_ref, o_ref):
    del full_ref
    _combine_kernel(x_ref, y0_ref, y1_ref, rt_ref, g_ref, b_ref, o_ref)


def _combine_ln(x2, y01, rt, g3, b3, into=None, row0=0, full_rows=None):
    t_all, d = x2.shape
    tq = TOKEN_TILE
    nt = t_all // tq
    blk0 = row0 // tq if full_rows else 0
    in_specs = [
        pl.BlockSpec((tq, d), lambda i: (i, 0)),
        pl.BlockSpec((tq, d // 2), lambda i: (i, 0)),
        pl.BlockSpec((tq, d // 2), lambda i: (i + nt, 0)),
        pl.BlockSpec((8, tq), lambda i: (0, i)),
        _const_spec((1, d)), _const_spec((1, d)),
    ]
    args = [x2, y01, y01, rt, g3, b3]
    body, aliases = _combine_kernel, {}
    if into is not None:
        in_specs.append(pl.BlockSpec(memory_space=pl.ANY))
        args.append(into)
        body, aliases = _combine_kernel_into, {len(args) - 1: 0}
    return pl.pallas_call(
        body,
        grid=(nt,),
        in_specs=in_specs,
        out_specs=pl.BlockSpec((tq, d), lambda i: (i + blk0, 0)),
        out_shape=jax.ShapeDtypeStruct((full_rows or t_all, d), F32),
        input_output_aliases=aliases,
        compiler_params=_tc_params(1),
        name="combine_ln",
    )(*args)


def _routing_tables(rt, counts):
    bm = EXPERT_ROWS
    t_all = rt.shape[1]
    n_rows = 2 * t_all + N_EXPERTS * bm
    cnt = counts[:, 0]
    padded = (cnt + bm - 1) // bm * bm
    ends = jnp.cumsum(padded)
    offs = ends - padded
    experts = rt[0:2]
    off_tok = jnp.sum(jnp.where(experts[None] == jnp.arange(N_EXPERTS, dtype=I32)[:, None, None],
                                offs[:, None, None], 0), axis=0)
    dest2d = (off_tok + rt[2:4]).reshape(-1, SC_CHUNK).astype(I32)
    blk_start = jnp.arange(n_rows // bm, dtype=I32) * bm
    blk_expert = jnp.minimum(jnp.sum(blk_start[:, None] >= ends[None, :], axis=1), N_EXPERTS - 1).astype(I32)
    live_end = jnp.sum(jnp.where(blk_expert[:, None] == jnp.arange(N_EXPERTS, dtype=I32)[None, :],
                                 (offs + cnt)[None, :], 0), axis=1)
    blk_valid = jnp.clip(live_end - blk_start, 0, bm).astype(I32)
    n_used = (ends[-1:] // bm).astype(I32)
    eid = jnp.arange(N_EXPERTS, dtype=I32)
    later_present = jnp.logical_and(eid[None, :] > eid[:, None], (cnt > 0)[None, :])
    next_present = jnp.min(jnp.where(later_present, eid[None, :], N_EXPERTS), axis=1)
    next_of_blk = jnp.sum(jnp.where(blk_expert[:, None] == eid[None, :], next_present[None, :], 0), axis=1)
    prev_expert = jnp.concatenate([jnp.full((1,), -1, I32), blk_expert[:-1]])
    is_first = jnp.logical_and(blk_start < ends[-1], blk_expert != prev_expert)
    blk_next = jnp.where(is_first, next_of_blk, -1).astype(I32)
    return dest2d, blk_expert, blk_valid, n_used, blk_next, n_rows


def _layer_tail(x1, kv, mem_len, p, router, b0, bsz, s_len):
    rw_hi, rw_lo, rbias = router
    x2, xp, rt, counts = _xattn_route(x1, kv, mem_len, p["xq"], p["xo"], p["ln2_g"], p["ln2_b"],
                                      rw_hi, rw_lo, rbias, b0, bsz, s_len)
    dest2d, blk_expert, blk_valid, n_used, blk_next, n_rows = _routing_tables(rt, counts)
    xs = _sc_dispatch(xp, dest2d, n_rows)
    if "packed" not in p:
        p["packed"] = tuple(_sc_pack_weights(w.reshape(-1, w.shape[2])).reshape(w.shape[0], w.shape[1], -1)
                            for w in (p["e_gate"], p["e_up"], p["e_down"]))
    ys = _expert_ffn(xs, blk_expert, blk_valid, n_used, blk_next, *p["packed"])
    y01 = _sc_combine(ys, dest2d)
    return x2, y01, rt


def _row(v):
    return v.reshape(1, -1).astype(F32)


def _common_params(xq, xkv, xo, ln2_g, ln2_b, e_gate, e_up, e_down, ln3_g, ln3_b):
    return dict(xq=xq, xkv=xkv, xo=xo, ln2_g=_row(ln2_g), ln2_b=_row(ln2_b),
                e_gate=e_gate, e_up=e_up, e_down=e_down, ln3_g=_row(ln3_g), ln3_b=_row(ln3_b))


def kernel(x, mem, positions, router_w, router_bias, l0_w_in, l0_sinks, l0_sgu_ln_g, l0_sgu_ln_b, l0_sgu_w, l0_sgu_b, l0_w_out, l0_ln1_g, l0_ln1_b, l0_xq, l0_xkv, l0_xo, l0_ln2_g, l0_ln2_b, l0_e_gate, l0_e_up, l0_e_down, l0_ln3_g, l0_ln3_b, l1_w_in, l1_pool_w, l1_pool_scale, l1_w_out, l1_ln1_g, l1_ln1_b, l1_xq, l1_xkv, l1_xo, l1_ln2_g, l1_ln2_b, l1_e_gate, l1_e_up, l1_e_down, l1_ln3_g, l1_ln3_b):
    bsz, s_len, d = x.shape
    assert s_len % TOKEN_TILE == 0 and TOKEN_TILE == 2 * SLAB_ROWS and SLAB_ROWS % BLOCK == 0
    xt = x.reshape(bsz * s_len, d)
    mem2d = mem.reshape(-1, d)

    rw_t = router_w.T.astype(F32)
    rw_hi = rw_t.astype(BF16)
    rw_lo = (rw_t - rw_hi.astype(F32)).astype(BF16)
    router = (rw_hi, rw_lo, router_bias.reshape(-1, 1).astype(F32))

    half = ROPE_DIM // 2
    inv_freq = (ROPE_THETA ** (-(jnp.arange(half, dtype=F32) * 2.0 / ROPE_DIM))).reshape(half, 1)
    etab_np = np.zeros((LANES, 3 * LANES), np.float32)
    cbase_np = np.ones((1, LANES), np.float32)
    for ln in range(LANES):
        dd = ln % HEAD_DIM
        if dd < ROPE_DIM:
            cbase_np[0, ln] = 0.0
            etab_np[[dd % half, half + dd % half], ln] = 1.0
            if dd >= half:
                etab_np[[2 * half + dd - half, 3 * half + dd - half], LANES + ln] = 1.0
            else:
                etab_np[[2 * half + dd, 3 * half + dd], 2 * LANES + ln] = -1.0
    etab = jnp.asarray(etab_np, BF16)
    cbase = jnp.asarray(cbase_np)
    pos_row = positions.reshape(1, -1).astype(I32)
    grp = jnp.arange(PIECE_COLS) // B_GROUP_DIM
    gsum = (grp[:, None] == grp[None, :]).astype(BF16)
    bs_full = jnp.repeat(l0_sgu_b.T.astype(F32), B_GROUP_DIM, axis=1)

    p0 = _common_params(l0_xq, l0_xkv, l0_xo, l0_ln2_g, l0_ln2_b, l0_e_gate, l0_e_up, l0_e_down,
                        l0_ln3_g, l0_ln3_b)
    p1 = _common_params(l1_xq, l1_xkv, l1_xo, l1_ln2_g, l1_ln2_b, l1_e_gate, l1_e_up, l1_e_down,
                        l1_ln3_g, l1_ln3_b)
    kv0 = _kv_proj(mem2d, p0["xkv"], bsz)
    kv1 = _kv_proj(mem2d, p1["xkv"], bsz)
    mem_len = mem.shape[1]

    n_split = BATCH_SPLIT if bsz % BATCH_SPLIT == 0 else 1
    nb = bsz // n_split
    out = None
    for part in range(n_split):
        b0 = part * nb
        x1 = _mixer0(xt, pos_row, l0_sinks.astype(F32), l0_w_in, inv_freq, etab, cbase, gsum,
                     _row(l0_sgu_ln_g), _row(l0_sgu_ln_b), l0_sgu_w.astype(F32), bs_full,
                     l0_w_out, _row(l0_ln1_g), _row(l0_ln1_b), b0, nb, s_len)
        x2, y01, rt = _layer_tail(x1, kv0, mem_len, p0, router, b0, nb, s_len)
        x1 = _mixer1(x2, y01, rt, p0["ln3_g"], p0["ln3_b"], l1_w_in, l1_pool_w, _row(l1_pool_scale),
                     l1_w_out, _row(l1_ln1_g), _row(l1_ln1_b), nb, s_len)
        x2, y01, rt = _layer_tail(x1, kv1, mem_len, p1, router, b0, nb, s_len)
        out = _combine_ln(x2, y01, rt, p1["ln3_g"], p1["ln3_b"], into=out, row0=b0 * s_len,
                          full_rows=bsz * s_len)
    return out.reshape(bsz, s_len, d)
```
